```python
import math
import numpy as np
import jax
import jax.numpy as jnp
from jax import lax

D_MODEL = 1024
BATCH = 4
SEQ = 8192
DEPTH = 2

GRID_W = 64
CTX_LEN = 256
N_EVEN = (DEPTH + 1) // 2
N_ODD = DEPTH // 2
EPS = 1e-6
ROPE_BASE = 10000.0
NEG_INF = -1e30
GLA_HEADS = 4
GLA_DK = 64
GLA_DV = 128
GLA_QK = GLA_HEADS * GLA_DK
GLA_V = GLA_HEADS * GLA_DV
GLA_RANK = 16
GLA_TAU = 16.0
GLA_CHUNK = 64
GLA_LOG_ALPHA_MIN = -1.0
S5_W = D_MODEL // 2
S5_H = 16
S5_G = S5_W // S5_H
S5_P = 64
DT_MIN = 1e-3
DT_MAX = 1e-1
NA_HEADS = 8
NA_DH = 64
NA_W = NA_HEADS * NA_DH
WIN_R = 8
WIN_C = 16
HY_W = D_MODEL // 2
HY_SHORT = 3
HY_EMB = 33
HY_ORD = 64
HY_DECAY_TARGET = 1e-2
HY_DECAY_SHORT_PCT = 0.3
HY_DECAY_LONG_PCT = 1.5
N_GROUPS = 4
EXP_PER_GROUP = 8
N_EXPERTS = N_GROUPS * EXP_PER_GROUP
D_EXPERT = D_MODEL // 2
TOP_K = 2
MOE_BLOCK = 256

EV_IN = 2 * GLA_QK + 2 * GLA_V + 2 * GLA_RANK + S5_W
EV_MIX = GLA_V + S5_W
OD_IN = 3 * NA_W + 3 * HY_W
OD_MIX = NA_W + HY_W

kernel_name = 'hybrid_gla_s5_natten_hyena_hmoe_dit'


def _rms(x):
    xf = x.astype(jnp.float32)
    return (xf * lax.rsqrt(jnp.mean(xf * xf, axis=-1, keepdims=True) + EPS)).astype(x.dtype)


def _modulate(h, shift, scale):
    return h * (1.0 + scale) + shift


def _rope_axis(x, pos):
    half = x.shape[-1] // 2
    freqs = ROPE_BASE ** (-jnp.arange(half, dtype=jnp.float32) / half)
    ang = pos.astype(jnp.float32)[:, None] * freqs
    cos, sin = jnp.cos(ang)[:, None, :], jnp.sin(ang)[:, None, :]
    x1, x2 = x[..., :half], x[..., half:]
    return jnp.concatenate([x1 * cos - x2 * sin, x1 * sin + x2 * cos], axis=-1)


def _rope_2d(x, rows, cols):
    d = x.shape[-1] // 2
    return jnp.concatenate([_rope_axis(x[..., :d], rows), _rope_axis(x[..., d:], cols)], axis=-1)


def _gla_chunked(q, k, v, log_a, s0, strict):
    bsz, seqlen, nh, _ = q.shape
    dv = v.shape[-1]
    n = seqlen // GLA_CHUNK
    rs = lambda t: t.reshape(bsz, n, GLA_CHUNK, nh, t.shape[-1])
    q, k, v, log_a = rs(q), rs(k), rs(v), rs(log_a)
    b = jnp.cumsum(log_a, axis=2)
    b_last = b[:, :, -1]
    q_e = q * jnp.exp(b)
    k_e = k * jnp.exp(-b)
    k_d = k * jnp.exp(b_last[:, :, None] - b)
    mask = jnp.tril(jnp.ones((GLA_CHUNK, GLA_CHUNK), dtype=bool), -1 if strict else 0)
    att = jnp.where(mask, jnp.einsum('bnihk,bnjhk->bnhij', q_e, k_e), 0.0)
    o_intra = jnp.einsum('bnhij,bnjhv->bnihv', att, v)
    upd = jnp.einsum('bnjhk,bnjhv->nbhkv', k_d, v)
    dec = jnp.exp(b_last).transpose(1, 0, 2, 3)

    def step(s, inp):
        d, u = inp
        return d[..., None] * s + u, s

    s_fin, s_before = lax.scan(step, s0, (dec, upd))
    o_inter = jnp.einsum('bnihk,nbhkv->bnihv', q_e, s_before)
    return (o_intra + o_inter).reshape(bsz, seqlen, nh, dv), s_fin


def _cmul(ar, ai, br, bi):
    return ar * br - ai * bi, ar * bi + ai * br


def _s5_discretise(lam_re, lam_im, log_dt, b_re, b_im):
    dt = jnp.exp(log_dt)[..., None]
    mag = jnp.exp(lam_re * dt)
    a_re, a_im = mag * jnp.cos(lam_im * dt), mag * jnp.sin(lam_im * dt)
    den = lam_re * lam_re + lam_im * lam_im
    nr = a_re - 1.0
    co_re = ((nr * lam_re + a_im * lam_im) / den)[..., None]
    co_im = ((a_im * lam_re - nr * lam_im) / den)[..., None]
    bb_re, bb_im = _cmul(co_re, co_im, b_re, b_im)
    return a_re, a_im, bb_re, bb_im


def _s5_scan_dir(u, a_re, a_im, bb_re, bb_im, c_re, c_im, h0_re, h0_im, reverse):
    if reverse:
        u = jnp.flip(u, 1)
    seqlen = u.shape[1]
    bu_re = jnp.einsum('blgh,gph->lbgp', u, bb_re)
    bu_im = jnp.einsum('blgh,gph->lbgp', u, bb_im)
    ir, ii = _cmul(a_re, a_im, h0_re, h0_im)
    bu_re = bu_re.at[0].add(ir)
    bu_im = bu_im.at[0].add(ii)
    ar = jnp.broadcast_to(a_re, (seqlen, 1) + a_re.shape)
    ai = jnp.broadcast_to(a_im, (seqlen, 1) + a_im.shape)

    def comb(e1, e2):
        ar1, ai1, br1, bi1 = e1
        ar2, ai2, br2, bi2 = e2
        a_r, a_i = _cmul(ar2, ai2, ar1, ai1)
        x_r, x_i = _cmul(ar2, ai2, br1, bi1)
        return a_r, a_i, x_r + br2, x_i + bi2

    _, _, h_re, h_im = lax.associative_scan(comb, (ar, ai, bu_re, bu_im), axis=0)
    y = jnp.einsum('ghp,lbgp->blgh', c_re, h_re) - jnp.einsum('ghp,lbgp->blgh', c_im, h_im)
    if reverse:
        y = jnp.flip(y, 1)
    return y, h_re[-1], h_im[-1]


def _even_mixer(h_c, h_l, rows, cols, w_in, w_out, wa2, ba, gnorm, s5, w_glu, need_ctx):
    f32 = jnp.float32
    splits = np.cumsum([GLA_QK, GLA_QK, GLA_V, GLA_V, 2 * GLA_RANK]).tolist()

    def project(h, rotate):
        bsz, seqlen, _ = h.shape
        q, k, v, g, a, u = jnp.split(h @ w_in, splits, axis=-1)
        q = q.reshape(bsz, seqlen, GLA_HEADS, GLA_DK)
        k = k.reshape(bsz, seqlen, GLA_HEADS, GLA_DK)
        if rotate:
            q, k = _rope_2d(q, rows, cols), _rope_2d(k, rows, cols)
        a = a.reshape(bsz, seqlen, 2, GLA_RANK)
        pre = (jnp.einsum('bldr,drk->bldk', a, wa2) + ba).astype(f32)
        la = jnp.maximum(jax.nn.log_sigmoid(pre) / GLA_TAU, GLA_LOG_ALPHA_MIN)
        la = la.reshape(bsz, seqlen, 2, GLA_HEADS, GLA_DK)
        return ((q * GLA_DK ** -0.5).astype(f32), k.astype(f32),
                v.reshape(bsz, seqlen, GLA_HEADS, GLA_DV).astype(f32), la, g,
                u.reshape(bsz, seqlen, S5_G, S5_H).astype(f32))

    q_c, k_c, v_c, la_c, g_c, u_c = project(h_c, False)
    q_l, k_l, v_l, la_l, g_l, u_l = project(h_l, True)
    fl = lambda t: jnp.flip(t, 1)
    bsz = h_l.shape[0]

    s0 = jnp.zeros((bsz, GLA_HEADS, GLA_DK, GLA_DV), f32)
    o_cf, s_cf = _gla_chunked(q_c, k_c, v_c, la_c[:, :, 0], s0, False)
    o_cb, s_cb = _gla_chunked(fl(q_c), fl(k_c), fl(v_c), fl(la_c[:, :, 1]), s0, True)
    o_lf, _ = _gla_chunked(q_l, k_l, v_l, la_l[:, :, 0], s_cf, False)
    o_lb, _ = _gla_chunked(fl(q_l), fl(k_l), fl(v_l), fl(la_l[:, :, 1]), s_cb, True)

    def gla_out(o, g):
        b_, l_ = o.shape[:2]
        return (_rms(o) * gnorm.astype(f32)).reshape(b_, l_, GLA_V) * jax.nn.silu(g.astype(f32))

    lam_re, lam_im, log_dt, b_re, b_im, c_re, c_im, d_skip = [t.astype(f32) for t in s5]
    a_re, a_im, bb_re, bb_im = _s5_discretise(lam_re, lam_im, log_dt, b_re, b_im)
    dirp = [(a_re[s], a_im[s], bb_re[s], bb_im[s], c_re[s], c_im[s]) for s in range(2)]
    h0 = jnp.zeros((bsz, S5_G, S5_P), f32)
    y_cf, hf_re, hf_im = _s5_scan_dir(u_c, *dirp[0], h0, h0, False)
    y_cb, hb_re, hb_im = _s5_scan_dir(u_c, *dirp[1], h0, h0, True)
    y_lf, _, _ = _s5_scan_dir(u_l, *dirp[0], hf_re, hf_im, False)
    y_lb, _, _ = _s5_scan_dir(u_l, *dirp[1], hb_re, hb_im, True)

    def s5_out(yf, yb, u):
        b_, l_ = u.shape[:2]
        y = jax.nn.gelu((yf + yb + d_skip * u).reshape(b_, l_, S5_W))
        return y * jax.nn.sigmoid(y @ w_glu.astype(f32))

    def merge(o_gla, y_s5, dtype):
        return (jnp.concatenate([o_gla, y_s5], axis=-1) @ w_out.astype(f32)).astype(dtype)

    mix_l = merge(gla_out(o_lf + fl(o_lb), g_l), s5_out(y_lf, y_lb, u_l), h_l.dtype)
    mix_c = merge(gla_out(o_cf + fl(o_cb), g_c), s5_out(y_cf, y_cb, u_c), h_c.dtype) if need_ctx else None
    return mix_c, mix_l


def _na_latent(q, k, v, k_c, v_c, rpb):
    bsz, seqlen, nh, dh = q.shape
    n_rows = seqlen // GRID_W
    wr = min(WIN_R, n_rows)
    ncb, kcols = GRID_W // WIN_C, 2 * WIN_C
    key_start = np.clip(np.arange(ncb) * WIN_C - WIN_C // 2, 0, GRID_W - kcols)
    key_cols = key_start[:, None] + np.arange(kcols)
    q_cols = np.arange(GRID_W).reshape(ncb, WIN_C)
    q_start = np.clip(q_cols - WIN_C // 2, 0, GRID_W - WIN_C)[..., None]
    kc = key_cols[:, None, :]
    col_valid = (kc >= q_start) & (kc < q_start + WIN_C)
    col_idx = np.clip(kc - q_cols[..., None] + WIN_C - 1, 0, 2 * WIN_C - 2)
    scale = dh ** -0.5
    n_loc = wr * kcols
    qg = q.reshape(bsz, n_rows, ncb, WIN_C, nh, dh)
    kg = k.reshape(bsz, n_rows, GRID_W, nh, dh)
    vg = v.reshape(bsz, n_rows, GRID_W, nh, dh)

    def row_block(r):
        r0 = jnp.clip(r - wr // 2, 0, n_rows - wr)
        q_r = lax.dynamic_index_in_dim(qg, r, axis=1, keepdims=False)
        k_r = lax.dynamic_slice_in_dim(kg, r0, wr, axis=1)[:, :, key_cols]
        v_r = lax.dynamic_slice_in_dim(vg, r0, wr, axis=1)[:, :, key_cols]
        s_loc = jnp.einsum('bjqhd,bwjkhd->bhjqwk', q_r, k_r).astype(jnp.float32) * scale
        row_idx = r0 + jnp.arange(wr) - r + WIN_R - 1
        bias = rpb[:, row_idx[None, None, :, None], col_idx[:, :, None, :]].astype(jnp.float32)
        s_loc = jnp.where(col_valid[:, :, None, :], s_loc + bias, NEG_INF)
        s_ctx = jnp.einsum('bjqhd,bkhd->bhjqk', q_r, k_c).astype(jnp.float32) * scale
        s = jnp.concatenate([s_loc.reshape(bsz, nh, ncb, WIN_C, n_loc), s_ctx], axis=-1)
        p = jax.nn.softmax(s, axis=-1).astype(v.dtype)
        o = (jnp.einsum('bhjqwk,bwjkhd->bjqhd', p[..., :n_loc].reshape(bsz, nh, ncb, WIN_C, wr, kcols), v_r)
             + jnp.einsum('bhjqk,bkhd->bjqhd', p[..., n_loc:], v_c))
        return o.reshape(bsz, GRID_W, nh, dh)

    out = lax.map(row_block, jnp.arange(n_rows))
    return out.transpose(1, 0, 2, 3, 4).reshape(bsz, seqlen, nh * dh)


def _hyena_filter(seqlen, fw1, fb1, fw2, fb2, fw3, fb3, freq, fw4):
    f32 = jnp.float32
    t = jnp.linspace(0.0, 1.0, seqlen, dtype=f32)[:, None]
    bands = (HY_EMB - 1) // 2
    w = 2.0 * math.pi * jnp.arange(seqlen, dtype=f32)[:, None] / seqlen
    f = jnp.linspace(1e-4, bands - 1, bands, dtype=f32)[None, :]
    feat = jnp.concatenate([t, jnp.cos(f * w), -jnp.sin(f * w)], axis=-1)
    h = jnp.sin(freq * (feat @ fw1 + fb1))
    h = jnp.sin(freq * (h @ fw2 + fb2))
    h = jnp.sin(freq * (h @ fw3 + fb3))
    h = (h @ fw4).reshape(seqlen, 2, HY_W)
    deltas = jnp.abs(jnp.linspace(math.log(HY_DECAY_TARGET) / HY_DECAY_LONG_PCT,
                                  math.log(HY_DECAY_TARGET) / HY_DECAY_SHORT_PCT, HY_W, dtype=f32))
    h = h * jnp.exp(-t * deltas)[:, None, :]
    return jnp.concatenate([h[:, 0], jnp.zeros((1, HY_W), f32), h[:0:-1, 1]], axis=0)


def _hyena(z, conv_w, conv_b, filt, bias):
    bsz, seqlen, _ = z.shape
    pad = HY_SHORT // 2
    zp = jnp.pad(z, ((0, 0), (pad, pad), (0, 0)))
    zc = conv_b + conv_w[0] * zp[:, 0:seqlen]
    for j in range(1, HY_SHORT):
        zc = zc + conv_w[j] * zp[:, j:j + seqlen]
    x0, x1, v = jnp.split(zc, 3, axis=-1)
    u = (x1 * v).astype(jnp.float32)
    filt = [p.astype(jnp.float32) for p in filt]
    kf = jnp.fft.rfft(_hyena_filter(seqlen, *filt), axis=0)
    y = jnp.fft.irfft(jnp.fft.rfft(u, n=2 * seqlen, axis=1) * kf, n=2 * seqlen, axis=1)[:, :seqlen]
    return x0.astype(jnp.float32) * (y + u * bias.astype(jnp.float32))


def _odd_mixer(h_c, h_l, w_in, w_out, qn, kn, rpb, conv_w, conv_b, filt, hbias, need_ctx):
    f32 = jnp.float32
    heads = lambda t: t.reshape(t.shape[0], t.shape[1], NA_HEADS, NA_DH)
    q_l, k_l, v_l, zh_l = jnp.split(h_l @ w_in, [NA_W, 2 * NA_W, 3 * NA_W], axis=-1)
    q_l, k_l, v_l = _rms(heads(q_l)) * qn, _rms(heads(k_l)) * kn, heads(v_l)
    if need_ctx:
        q_c, k_c, v_c, zh_c = jnp.split(h_c @ w_in, [NA_W, 2 * NA_W, 3 * NA_W], axis=-1)
    else:
        k_c, v_c = jnp.split(h_c @ w_in[:, NA_W:3 * NA_W], 2, axis=-1)
    k_c, v_c = _rms(heads(k_c)) * kn, heads(v_c)

    na_l = _na_latent(q_l, k_l, v_l, k_c, v_c, rpb).astype(f32)
    hy_l = _hyena(zh_l, conv_w, conv_b, filt, hbias)
    mix_l = (jnp.concatenate([na_l, hy_l], axis=-1) @ w_out.astype(f32)).astype(h_l.dtype)
    if not need_ctx:
        return None, mix_l
    q_c = _rms(heads(q_c)) * qn
    s = jnp.einsum('bqhd,bkhd->bhqk', q_c, k_c).astype(f32) * NA_DH ** -0.5
    na_c = jnp.einsum('bhqk,bkhd->bqhd', jax.nn.softmax(s, axis=-1).astype(v_c.dtype), v_c)
    na_c = na_c.reshape(h_c.shape[0], h_c.shape[1], NA_W).astype(f32)
    hy_c = _hyena(zh_c, conv_w, conv_b, filt, hbias)
    mix_c = (jnp.concatenate([na_c, hy_c], axis=-1) @ w_out.astype(f32)).astype(h_c.dtype)
    return mix_c, mix_l


def _moe(h, wg, bg, we, be, w1, w3, w2):
    n_tok, d = h.shape
    lg = (h @ wg + bg).astype(jnp.float32)
    grp = jnp.argmax(lg, axis=-1)
    g_w = jnp.max(jax.nn.softmax(lg, axis=-1), axis=-1, keepdims=True)
    le = (h @ we + be).astype(jnp.float32).reshape(n_tok, N_GROUPS, EXP_PER_GROUP)
    le = le[jnp.arange(n_tok), grp]
    top_p, top_i = lax.top_k(jax.nn.softmax(le, axis=-1), TOP_K)
    gate = (g_w * top_p / jnp.sum(top_p, axis=-1, keepdims=True)).reshape(-1)
    eid = (grp[:, None] * EXP_PER_GROUP + top_i).reshape(-1)
    tok = jnp.repeat(jnp.arange(n_tok), TOP_K)
    n_asg = n_tok * TOP_K
    order = jnp.argsort(eid)
    e_sorted = eid[order]
    counts = jnp.bincount(eid, length=N_EXPERTS)
    padded = (counts + MOE_BLOCK - 1) // MOE_BLOCK * MOE_BLOCK
    pad_end = jnp.cumsum(padded)
    pad_start = pad_end - padded
    cnt_start = jnp.cumsum(counts) - counts
    dest = pad_start[e_sorted] + jnp.arange(n_asg) - cnt_start[e_sorted]
    n_blocks = -(-n_asg // MOE_BLOCK) + N_EXPERTS
    tok_buf = jnp.full((n_blocks * MOE_BLOCK,), n_tok, jnp.int32).at[dest].set(tok[order])
    gate_buf = jnp.zeros((n_blocks * MOE_BLOCK,), h.dtype).at[dest].set(gate[order].astype(h.dtype))
    blk_e = jnp.minimum(jnp.searchsorted(pad_end, jnp.arange(n_blocks) * MOE_BLOCK, side='right'), N_EXPERTS - 1)
    xb = jnp.concatenate([h, jnp.zeros((1, d), h.dtype)])[tok_buf].reshape(n_blocks, MOE_BLOCK, d)

    def expert_block(args):
        xe, e = args
        return (jax.nn.silu(xe @ w1[e]) * (xe @ w3[e])) @ w2[e]

    yb = lax.map(expert_block, (xb, blk_e)).reshape(-1, d)
    out = jnp.zeros((n_tok + 1, d), yb.dtype).at[tok_buf].add(yb * gate_buf[:, None].astype(yb.dtype))
    return out[:n_tok]


def setup_inputs(seed: int = 0) -> dict:
    key = jax.random.key(seed)
    ks = iter(jax.random.split(key, 64))

    def nrm(shape, scale):
        return scale * jax.random.normal(next(ks), shape, jnp.float32)

    D = D_MODEL
    p = {}
    p['x'] = nrm((BATCH, SEQ, D), 1.0)
    p['c'] = nrm((BATCH, D), 1.0)
    p['ctx'] = nrm((BATCH, CTX_LEN, D), 1.0)
    p['c_ctx'] = nrm((D,), 1.0)
    p['ada_w'] = nrm((DEPTH, D, 6 * D), 0.5 * D ** -0.5)
    p['ada_b'] = nrm((DEPTH, 6 * D), 0.02)
    p['moe_wg'] = nrm((DEPTH, D, N_GROUPS), D ** -0.5)
    p['moe_bg'] = nrm((DEPTH, N_GROUPS), 0.01)
    p['moe_we'] = nrm((DEPTH, D, N_EXPERTS), D ** -0.5)
    p['moe_be'] = nrm((DEPTH, N_EXPERTS), 0.01)
    p['moe_w1'] = nrm((DEPTH, N_EXPERTS, D, D_EXPERT), D ** -0.5)
    p['moe_w3'] = nrm((DEPTH, N_EXPERTS, D, D_EXPERT), D ** -0.5)
    p['moe_w2'] = nrm((DEPTH, N_EXPERTS, D_EXPERT, D), D_EXPERT ** -0.5)
    p['ev_w_in'] = nrm((N_EVEN, D, EV_IN), D ** -0.5)
    p['ev_w_out'] = nrm((N_EVEN, EV_MIX, D), EV_MIX ** -0.5)
    p['gla_wa2'] = nrm((N_EVEN, 2, GLA_RANK, GLA_QK), GLA_RANK ** -0.5)
    p['gla_ba'] = nrm((N_EVEN, 2, GLA_QK), 0.5)
    p['gla_norm'] = 1.0 + nrm((N_EVEN, GLA_DV), 0.02)
    p['s5_lam_re'] = -0.5 + nrm((N_EVEN, 2, S5_G, S5_P), 0.01)
    p['s5_lam_im'] = math.pi * jnp.arange(S5_P, dtype=jnp.float32) + nrm((N_EVEN, 2, S5_G, S5_P), 0.01)
    p['s5_log_dt'] = jax.random.uniform(next(ks), (N_EVEN, 2, S5_G), jnp.float32, math.log(DT_MIN), math.log(DT_MAX))
    p['s5_b_re'] = nrm((N_EVEN, 2, S5_G, S5_P, S5_H), (2 * S5_H) ** -0.5)
    p['s5_b_im'] = nrm((N_EVEN, 2, S5_G, S5_P, S5_H), (2 * S5_H) ** -0.5)
    p['s5_c_re'] = nrm((N_EVEN, 2, S5_G, S5_H, S5_P), 2.0 * S5_P ** -0.5)
    p['s5_c_im'] = nrm((N_EVEN, 2, S5_G, S5_H, S5_P), 2.0 * S5_P ** -0.5)
    p['s5_d'] = nrm((N_EVEN, S5_G, S5_H), 0.5)
    p['s5_w_glu'] = nrm((N_EVEN, S5_W, S5_W), S5_W ** -0.5)
    p['od_w_in'] = nrm((N_ODD, D, OD_IN), D ** -0.5)
    p['od_w_out'] = nrm((N_ODD, OD_MIX, D), OD_MIX ** -0.5)
    p['na_q_norm'] = 1.0 + nrm((N_ODD, NA_DH), 0.02)
    p['na_k_norm'] = 1.0 + nrm((N_ODD, NA_DH), 0.02)
    p['na_rpb'] = nrm((N_ODD, NA_HEADS, 2 * WIN_R - 1, 2 * WIN_C - 1), 0.02)
    p['hy_conv_w'] = nrm((N_ODD, HY_SHORT, 3 * HY_W), HY_SHORT ** -0.5)
    p['hy_conv_b'] = nrm((N_ODD, 3 * HY_W), 0.02)
    p['hy_fw1'] = nrm((N_ODD, HY_EMB, HY_ORD), HY_EMB ** -0.5)
    p['hy_fb1'] = nrm((N_ODD, HY_ORD), 0.1)
    p['hy_fw2'] = nrm((N_ODD, HY_ORD, HY_ORD), HY_ORD ** -0.5)
    p['hy_fb2'] = nrm((N_ODD, HY_ORD), 0.1)
    p['hy_fw3'] = nrm((N_ODD, HY_ORD, HY_ORD), HY_ORD ** -0.5)
    p['hy_fb3'] = nrm((N_ODD, HY_ORD), 0.1)
    p['hy_freq'] = 1.0 + nrm((N_ODD, HY_ORD), 0.02)
    p['hy_fw4'] = nrm((N_ODD, HY_ORD, 2 * HY_W), 0.02)
    p['hy_bias'] = nrm((N_ODD, HY_W), 0.5)
    return p


def reference(x, c, ctx, c_ctx, ada_w, ada_b, moe_wg, moe_bg, moe_we, moe_be, moe_w1, moe_w3, moe_w2,
              ev_w_in, ev_w_out, gla_wa2, gla_ba, gla_norm, s5_lam_re, s5_lam_im, s5_log_dt,
              s5_b_re, s5_b_im, s5_c_re, s5_c_im, s5_d, s5_w_glu, od_w_in, od_w_out, na_q_norm,
              na_k_norm, na_rpb, hy_conv_w, hy_conv_b, hy_fw1, hy_fb1, hy_fw2, hy_fb2, hy_fw3,
              hy_fb3, hy_freq, hy_fw4, hy_bias):
    bsz, seqlen, _ = x.shape
    pos = jnp.arange(seqlen)
    rows, cols = pos // GRID_W, pos % GRID_W
    x_l, x_c = x, ctx
    for i in range(DEPTH):
        last = i == DEPTH - 1
        j = i // 2
        mod_l = (jax.nn.silu(c) @ ada_w[i] + ada_b[i]).reshape(bsz, 6, 1, D_MODEL)
        mod_c = (jax.nn.silu(c_ctx) @ ada_w[i] + ada_b[i]).reshape(6, 1, 1, D_MODEL)
        h_l = _modulate(_rms(x_l), mod_l[:, 0], mod_l[:, 1])
        h_c = _modulate(_rms(x_c), mod_c[0], mod_c[1])
        if i % 2 == 0:
            m_c, m_l = _even_mixer(h_c, h_l, rows, cols, ev_w_in[j], ev_w_out[j], gla_wa2[j], gla_ba[j],
                                   gla_norm[j], (s5_lam_re[j], s5_lam_im[j], s5_log_dt[j], s5_b_re[j],
                                                 s5_b_im[j], s5_c_re[j], s5_c_im[j], s5_d[j]),
                                   s5_w_glu[j], not last)
        else:
            m_c, m_l = _odd_mixer(h_c, h_l, od_w_in[j], od_w_out[j], na_q_norm[j], na_k_norm[j], na_rpb[j],
                                  hy_conv_w[j], hy_conv_b[j],
                                  (hy_fw1[j], hy_fb1[j], hy_fw2[j], hy_fb2[j], hy_fw3[j], hy_fb3[j],
                                   hy_freq[j], hy_fw4[j]), hy_bias[j], not last)
        x_l = x_l + mod_l[:, 2] * m_l
        moe_p = (moe_wg[i], moe_bg[i], moe_we[i], moe_be[i], moe_w1[i], moe_w3[i], moe_w2[i])
        h2_l = _modulate(_rms(x_l), mod_l[:, 3], mod_l[:, 4]).reshape(-1, D_MODEL)
        if last:
            y_l = _moe(h2_l, *moe_p)
        else:
            x_c = x_c + mod_c[2] * m_c
            h2_c = _modulate(_rms(x_c), mod_c[3], mod_c[4]).reshape(-1, D_MODEL)
            y = _moe(jnp.concatenate([h2_c, h2_l], axis=0), *moe_p)
            n_c = h2_c.shape[0]
            x_c = x_c + mod_c[5] * y[:n_c].reshape(x_c.shape)
            y_l = y[n_c:]
        x_l = x_l + mod_l[:, 5] * y_l.reshape(x_l.shape)
    return x_l
```

```python
import functools
import math

import numpy as np
import jax
import jax.numpy as jnp
from jax import lax
from jax.experimental import pallas as pl
from jax.experimental.pallas import tpu as pltpu

F32, BF16 = jnp.float32, jnp.bfloat16
HI = lax.Precision.HIGHEST

D_MODEL = 1024
GRID_W = 64
EPS = 1e-6
ROPE_BASE = 10000.0
NEG_INF = -1e30
GLA_HEADS, GLA_DK, GLA_DV = 4, 64, 128
GLA_QK, GLA_V = GLA_HEADS * GLA_DK, GLA_HEADS * GLA_DV
GLA_RANK = 16
GLA_TAU = 16.0
GLA_CHUNK = 64
GLA_LOG_ALPHA_MIN = -1.0
S5_W, S5_H, S5_P = 512, 16, 64
S5_G = S5_W // S5_H
S5_CHUNK = 16
S5_ROWS = 8
NA_HEADS, NA_DH = 8, 64
NA_W = NA_HEADS * NA_DH
WIN_R, WIN_C = 8, 16
HY_W = 512
HY_SHORT = 3
HY_EMB = 33
HY_DECAY_TARGET = 1e-2
HY_DECAY_SHORT_PCT = 0.3
HY_DECAY_LONG_PCT = 1.5
N_GROUPS, EXP_PER_GROUP = 4, 8
N_EXPERTS = N_GROUPS * EXP_PER_GROUP
D_EXPERT = 512
TOP_K = 2
MOE_BLOCK = 256

TB = 256
FFT_N1 = 128
FFT_CG = 16
V7X_VMEM_LIMIT = 52 * 1024 * 1024


def _cparams(sem):
    return pltpu.CompilerParams(dimension_semantics=sem, vmem_limit_bytes=V7X_VMEM_LIMIT)


def _sigmoid(x):
    return 1.0 / (1.0 + jnp.exp(-x))


def _rms_rows(x):
    return x * lax.rsqrt(jnp.mean(x * x, axis=-1, keepdims=True) + EPS)


def _mm_kernel(a_ref, w_ref, b_ref, f_ref, o_ref, *, use_sin):
    z = jnp.dot(a_ref[...], w_ref[...], precision=HI, preferred_element_type=F32) + b_ref[...]
    if use_sin:
        z = jnp.sin(f_ref[...] * z)
    o_ref[...] = z


def _mm(a, w, bias=None, freq=None):
    m, k = a.shape
    n = w.shape[1]
    mp, kp = -(-m // 8) * 8, -(-k // 128) * 128
    tm = min(mp, 1024)
    mp = -(-mp // tm) * tm
    tn = n if n <= 1024 else 1024
    assert n % tn == 0
    a = jnp.pad(a.astype(F32), ((0, mp - m), (0, kp - k)))
    w = jnp.pad(w.astype(F32), ((0, kp - k), (0, 0)))
    bias = jnp.zeros((n,), F32) if bias is None else bias.astype(F32)
    use_sin = freq is not None
    freq = jnp.ones((n,), F32) if freq is None else freq.astype(F32)
    out = pl.pallas_call(
        functools.partial(_mm_kernel, use_sin=use_sin),
        grid=(mp // tm, n // tn),
        in_specs=[pl.BlockSpec((tm, kp), lambda i, j: (i, 0)),
                  pl.BlockSpec((kp, tn), lambda i, j: (0, j)),
                  pl.BlockSpec((1, tn), lambda i, j: (0, j)),
                  pl.BlockSpec((1, tn), lambda i, j: (0, j))],
        out_specs=pl.BlockSpec((tm, tn), lambda i, j: (i, j)),
        out_shape=jax.ShapeDtypeStruct((mp, n), F32),
        compiler_params=_cparams(("parallel", "parallel")),
        name="small_dense",
    )(a, w, bias.reshape(1, n), freq.reshape(1, n))
    return out[:m]


def _mod_index(b, i):
    return (b, jnp.minimum(i, 1), 0, 0)


EV_NQ, EV_NK, EV_NV, EV_NG, EV_NU = 0, 256, 512, 1024, 1536
EV_NA = 2048
EV_NTOT = 2176


def _ev_proj_kernel(x_ref, mod_ref, w_ref, wa_ref, ba_ref, cos_ref, sin_ref,
                    q_ref, k_ref, v_ref, g_ref, u_ref, la_ref):
    x = x_ref[...]
    h = _rms_rows(x) * (1.0 + mod_ref[1:2, :]) + mod_ref[0:1, :]
    z = jnp.dot(h.astype(BF16), w_ref[...], preferred_element_type=F32)
    lane = lax.broadcasted_iota(jnp.int32, (x.shape[0], GLA_QK), 1)
    first = (lane % 32) < 16
    cos, sin = cos_ref[...], sin_ref[...]

    def rot(t):
        partner = jnp.where(first, pltpu.roll(t, GLA_QK - 16, 1), pltpu.roll(t, 16, 1))
        return t * cos + partner * sin

    q_ref[...] = rot(z[:, EV_NQ:EV_NQ + GLA_QK]) * (GLA_DK ** -0.5)
    k_ref[...] = rot(z[:, EV_NK:EV_NK + GLA_QK])
    v_ref[...] = z[:, EV_NV:EV_NV + GLA_V].astype(BF16)
    g_ref[...] = z[:, EV_NG:EV_NG + GLA_V]
    u_ref[...] = z[:, EV_NU:EV_NU + S5_W]
    a = z[:, EV_NA:EV_NA + 128]
    pre = jnp.dot(a, wa_ref[...], precision=HI, preferred_element_type=F32) + ba_ref[...]
    ls = jnp.minimum(pre, 0.0) - jnp.log1p(jnp.exp(-jnp.abs(pre)))
    la_ref[...] = jnp.maximum(ls / GLA_TAU, GLA_LOG_ALPHA_MIN)


def _ev_proj(xcat, mods, w, wa, ba, cos, sin):
    bsz, lt, _ = xcat.shape
    nblk = lt // TB
    tok = lambda n: pl.BlockSpec((None, TB, n), lambda b, i: (b, i, 0))
    const = lambda shp: pl.BlockSpec(shp, lambda b, i: tuple(0 for _ in shp))
    return pl.pallas_call(
        _ev_proj_kernel,
        grid=(bsz, nblk),
        in_specs=[tok(D_MODEL),
                  pl.BlockSpec((None, None, 6, D_MODEL), _mod_index),
                  const((D_MODEL, EV_NTOT)), const((128, 2 * GLA_QK)), const((1, 2 * GLA_QK)),
                  pl.BlockSpec((TB, GLA_QK), lambda b, i: (i, 0)),
                  pl.BlockSpec((TB, GLA_QK), lambda b, i: (i, 0))],
        out_specs=[tok(GLA_QK), tok(GLA_QK), tok(GLA_V), tok(GLA_V), tok(S5_W), tok(2 * GLA_QK)],
        out_shape=[jax.ShapeDtypeStruct((bsz, lt, GLA_QK), F32),
                   jax.ShapeDtypeStruct((bsz, lt, GLA_QK), F32),
                   jax.ShapeDtypeStruct((bsz, lt, GLA_V), BF16),
                   jax.ShapeDtypeStruct((bsz, lt, GLA_V), F32),
                   jax.ShapeDtypeStruct((bsz, lt, S5_W), F32),
                   jax.ShapeDtypeStruct((bsz, lt, 2 * GLA_QK), F32)],
        compiler_params=_cparams(("parallel", "parallel")),
        name="ev_proj",
    )(xcat, mods, w, wa, ba, cos, sin)


def _gla_kernel(qf_ref, kf_ref, vf_ref, laf_ref, qb_ref, kb_ref, vb_ref, lab_ref,
                of_ref, ob_ref, s_scr):
    i = pl.program_id(1)

    @pl.when(i == 0)
    def _():
        s_scr[...] = jnp.zeros_like(s_scr)

    c = GLA_CHUNK
    row = lax.broadcasted_iota(jnp.int32, (c, c), 0)
    col = lax.broadcasted_iota(jnp.int32, (c, c), 1)
    lane_head = lax.broadcasted_iota(jnp.int32, (c, GLA_QK), 1) // GLA_DK
    bd_mask = (lax.broadcasted_iota(jnp.int32, (GLA_QK, GLA_V), 0) // GLA_DK
               == lax.broadcasted_iota(jnp.int32, (GLA_QK, GLA_V), 1) // GLA_DV)
    ones = jnp.ones((c, 128), F32)
    nchunk = qf_ref.shape[0] // c

    def one_chunk(refs, o_ref, d, r0):
        q_ref, k_ref, v_ref, la_ref = refs
        fwd = d == 0
        sl = pl.ds(r0, c)
        qc, kc, vc, lac = q_ref[sl, :], k_ref[sl, :], v_ref[sl, :], la_ref[sl, :]
        tri = ((row >= col) if fwd else (row <= col)).astype(F32)
        b = jnp.dot(tri, lac, precision=HI, preferred_element_type=F32)
        b_last = b[c - 1:c, :] if fwd else b[0:1, :]
        qe = (qc * jnp.exp(b)).astype(BF16)
        ke = (kc * jnp.exp(-b)).astype(BF16)
        kd = (kc * jnp.exp(b_last - b)).astype(BF16)
        dec = jnp.exp(lax.dot_general(lac, ones, (((0,), (0,)), ((), ())),
                                      precision=HI, preferred_element_type=F32))
        s = s_scr[d]
        o = jnp.dot(qe, s.astype(BF16), preferred_element_type=F32)
        att_mask = (row >= col) if fwd else (row < col)
        intra = []
        for h in range(GLA_HEADS):
            qh = jnp.where(lane_head == h, qe, jnp.zeros_like(qe))
            att = lax.dot_general(qh, ke, (((1,), (1,)), ((), ())), preferred_element_type=F32)
            att = jnp.where(att_mask, att, 0.0).astype(BF16)
            intra.append(jnp.dot(att, vc[:, h * GLA_DV:(h + 1) * GLA_DV], preferred_element_type=F32))
        o_ref[sl, :] = o + jnp.concatenate(intra, axis=1)
        upd = lax.dot_general(kd, vc, (((0,), (0,)), ((), ())), preferred_element_type=F32)
        s_dec = jnp.concatenate([s[:, h * GLA_DV:(h + 1) * GLA_DV] * dec for h in range(GLA_HEADS)], axis=1)
        s_scr[d] = s_dec + jnp.where(bd_mask, upd, 0.0)

    def body(j, carry):
        one_chunk((qf_ref, kf_ref, vf_ref, laf_ref), of_ref, 0, pl.multiple_of(j * c, c))
        one_chunk((qb_ref, kb_ref, vb_ref, lab_ref), ob_ref, 1, pl.multiple_of((nchunk - 1 - j) * c, c))
        return carry

    lax.fori_loop(0, nchunk, body, 0)


def _gla(q, k, v, la):
    bsz, lt, _ = q.shape
    nblk = lt // TB
    fwd_map = lambda b, i: (b, i, 0)
    bwd_blk = lambda i: jnp.where(i == 0, 0, nblk - i)
    bwd_map = lambda b, i: (b, bwd_blk(i), 0)
    bwd_map_la = lambda b, i: (b, bwd_blk(i), 1)
    spec = lambda n, m: pl.BlockSpec((None, TB, n), m)
    return pl.pallas_call(
        _gla_kernel,
        grid=(bsz, nblk),
        in_specs=[spec(GLA_QK, fwd_map), spec(GLA_QK, fwd_map), spec(GLA_V, fwd_map), spec(GLA_QK, fwd_map),
                  spec(GLA_QK, bwd_map), spec(GLA_QK, bwd_map), spec(GLA_V, bwd_map), spec(GLA_QK, bwd_map_la)],
        out_specs=[spec(GLA_V, fwd_map), spec(GLA_V, bwd_map)],
        out_shape=[jax.ShapeDtypeStruct((bsz, lt, GLA_V), F32), jax.ShapeDtypeStruct((bsz, lt, GLA_V), F32)],
        scratch_shapes=[pltpu.VMEM((2, GLA_QK, GLA_V), F32)],
        compiler_params=_cparams(("parallel", "arbitrary")),
        name="gla_scan",
    )(q, k, v, la, q, k, v, la)


def _s5_kernel(u_ref, wm_ref, tm_ref, cm_ref, ar_ref, ai_ref, y_ref, w_scr, hp_scr):
    u = u_ref[...]
    w_scr[...] = jnp.dot(u, wm_ref[...], preferred_element_type=F32)
    ar, ai = ar_ref[...], ai_ref[...]
    nstep = u.shape[0] // S5_ROWS

    def body(cix, carry):
        hre, him = carry
        sl = pl.ds(pl.multiple_of(cix * S5_ROWS, S5_ROWS), S5_ROWS)
        hp_scr[sl, :] = jnp.concatenate([hre, him], axis=1)
        w = w_scr[sl, :]
        return (ar * hre - ai * him + w[:, :128], ar * him + ai * hre + w[:, 128:])

    z = jnp.zeros((S5_ROWS, 128), F32)
    lax.fori_loop(0, nstep, body, (z, z))
    y_ref[...] = (jnp.dot(u, tm_ref[...], preferred_element_type=F32)
                  + jnp.dot(hp_scr[...].astype(BF16), cm_ref[...], preferred_element_type=F32))


def _s5_mats(lam_re, lam_im, log_dt, b_re, b_im, c_re, c_im):
    t16 = S5_CHUNK
    dt = jnp.exp(log_dt)[:, None]
    mag = jnp.exp(lam_re * dt)
    a_re, a_im = mag * jnp.cos(lam_im * dt), mag * jnp.sin(lam_im * dt)
    den = lam_re * lam_re + lam_im * lam_im
    nr = a_re - 1.0
    co_re = ((nr * lam_re + a_im * lam_im) / den)[..., None]
    co_im = ((a_im * lam_re - nr * lam_im) / den)[..., None]
    bb_re, bb_im = co_re * b_re - co_im * b_im, co_re * b_im + co_im * b_re
    pr, pi = [jnp.ones_like(a_re)], [jnp.zeros_like(a_im)]
    for _ in range(t16):
        pr, pi = pr + [pr[-1] * a_re - pi[-1] * a_im], pi + [pr[-1] * a_im + pi[-1] * a_re]
    pw_re, pw_im = jnp.stack(pr), jnp.stack(pi)
    g = lam_re.shape[0]
    e_re, e_im = pw_re[t16 - 1::-1][:t16], pw_im[t16 - 1::-1][:t16]
    wre = jnp.einsum('sgp,gph->gshp', e_re, bb_re) - jnp.einsum('sgp,gph->gshp', e_im, bb_im)
    wim = jnp.einsum('sgp,gph->gshp', e_re, bb_im) + jnp.einsum('sgp,gph->gshp', e_im, bb_re)
    cb_re = jnp.einsum('gkp,gph->gpkh', c_re, bb_re) - jnp.einsum('gkp,gph->gpkh', c_im, bb_im)
    cb_im = jnp.einsum('gkp,gph->gpkh', c_re, bb_im) + jnp.einsum('gkp,gph->gpkh', c_im, bb_re)
    kd = jnp.einsum('dgp,gpkh->dgkh', pw_re[:t16], cb_re) - jnp.einsum('dgp,gpkh->dgkh', pw_im[:t16], cb_im)
    lag = np.arange(t16)[None, :] - np.arange(t16)[:, None]
    toe = jnp.where((lag >= 0)[:, :, None, None, None], kd[np.clip(lag, 0, t16 - 1)], 0.0)
    toe = toe.transpose(2, 0, 4, 1, 3)
    q_re, q_im = pw_re[1:], pw_im[1:]
    ca_re = jnp.einsum('gkp,tgp->gptk', c_re, q_re) - jnp.einsum('gkp,tgp->gptk', c_im, q_im)
    ca_im = jnp.einsum('gkp,tgp->gptk', c_re, q_im) + jnp.einsum('gkp,tgp->gptk', c_im, q_re)
    npair = g // 2
    sh = t16 * S5_H
    wm = jnp.zeros((npair, 2, sh, 2, 2, S5_P), F32)
    tmat = jnp.zeros((npair, 2, sh, 2, sh), F32)
    cm = jnp.zeros((npair, 2, 2, S5_P, 2, sh), F32)
    for g2 in range(2):
        wm = wm.at[:, g2, :, 0, g2, :].set(wre[g2::2].reshape(npair, sh, S5_P))
        wm = wm.at[:, g2, :, 1, g2, :].set(wim[g2::2].reshape(npair, sh, S5_P))
        tmat = tmat.at[:, g2, :, g2, :].set(toe[g2::2].reshape(npair, sh, sh))
        cm = cm.at[:, 0, g2, :, g2, :].set(ca_re[g2::2].reshape(npair, S5_P, sh))
        cm = cm.at[:, 1, g2, :, g2, :].set(-ca_im[g2::2].reshape(npair, S5_P, sh))
    a16_re = pw_re[t16].reshape(npair, 1, 2 * S5_P)
    a16_im = pw_im[t16].reshape(npair, 1, 2 * S5_P)
    return (wm.reshape(npair, 2 * sh, 4 * S5_P).astype(BF16), tmat.reshape(npair, 2 * sh, 2 * sh).astype(BF16),
            cm.reshape(npair, 4 * S5_P, 2 * sh).astype(BF16), a16_re, a16_im)


def _s5_scan(useq, mats):
    bsz, lt, _ = useq.shape
    nc = lt // S5_CHUNK
    npair = S5_G // 2
    ug = useq.astype(BF16).reshape(bsz, nc, S5_CHUNK, npair, 2, S5_H)
    ug = jnp.pad(ug.transpose(3, 1, 0, 4, 2, 5), ((0, 0), (0, 0), (0, S5_ROWS - bsz), (0, 0), (0, 0), (0, 0)))
    ug = ug.reshape(npair, nc * S5_ROWS, 2 * S5_CHUNK * S5_H)
    wm, tmat, cm, ar, ai = mats
    r = nc * S5_ROWS
    per = lambda shp: pl.BlockSpec((None,) + shp, lambda p: (p,) + tuple(0 for _ in shp))
    y = pl.pallas_call(
        _s5_kernel,
        grid=(npair,),
        in_specs=[per((r, 512)), per((512, 256)), per((512, 512)), per((256, 512)), per((1, 128)), per((1, 128))],
        out_specs=per((r, 512)),
        out_shape=jax.ShapeDtypeStruct((npair, r, 512), F32),
        scratch_shapes=[pltpu.VMEM((r, 256), F32), pltpu.VMEM((r, 256), F32)],
        compiler_params=_cparams(("parallel",)),
        name="s5_scan",
    )(ug, wm, tmat, cm, ar, ai)
    y = y.reshape(npair, nc, S5_ROWS, 2, S5_CHUNK, S5_H)[:, :, :bsz]
    return y.transpose(2, 1, 4, 0, 3, 5).reshape(bsz, lt, S5_W)


def _mixer_tail(x, mix, mod_ref, wr_ref, br_ref, x_out, h2_out, lg_out):
    xn = x + mod_ref[2:3, :] * mix
    x_out[...] = xn
    h2 = _rms_rows(xn) * (1.0 + mod_ref[4:5, :]) + mod_ref[3:4, :]
    h2_out[...] = h2
    lg_out[...] = jnp.dot(h2, wr_ref[...], precision=HI, preferred_element_type=F32) + br_ref[...]


def _ev_out_kernel(of_ref, ob_ref, g_ref, yf_ref, yb_ref, u_ref, x_ref, mod_ref,
                   gn_ref, ds_ref, wglu_ref, wout_ref, wr_ref, br_ref,
                   x_out, h2_out, lg_out):
    o = of_ref[...] + ob_ref[...]
    og = jnp.concatenate([_rms_rows(o[:, h * GLA_DV:(h + 1) * GLA_DV]) for h in range(GLA_HEADS)], axis=1)
    g = g_ref[...]
    og = og * gn_ref[...] * (g * _sigmoid(g))
    t = yf_ref[...] + yb_ref[...] + ds_ref[...] * u_ref[...]
    y = t * (0.5 * (1.0 + jnp.tanh(math.sqrt(2.0 / math.pi) * (t + 0.044715 * (t * t * t)))))
    y = y * _sigmoid(jnp.dot(y.astype(BF16), wglu_ref[...], preferred_element_type=F32))
    cat = jnp.concatenate([og, y], axis=1).astype(BF16)
    mix = jnp.dot(cat, wout_ref[...], preferred_element_type=F32)
    _mixer_tail(x_ref[...], mix, mod_ref, wr_ref, br_ref, x_out, h2_out, lg_out)


def _ev_out(o_f, o_b, g, y_f, y_b, u, xcat, mods, gn, ds, wglu, wout, wr, br):
    bsz, lt, _ = xcat.shape
    nblk = lt // TB
    tok = lambda n: pl.BlockSpec((None, TB, n), lambda b, i: (b, i, 0))
    const = lambda shp: pl.BlockSpec(shp, lambda b, i: tuple(0 for _ in shp))
    return pl.pallas_call(
        _ev_out_kernel,
        grid=(bsz, nblk),
        in_specs=[tok(512), tok(512), tok(512), tok(512), tok(512), tok(512), tok(D_MODEL),
                  pl.BlockSpec((None, None, 6, D_MODEL), _mod_index),
                  const((1, 512)), const((1, 512)), const((512, 512)), const((D_MODEL, D_MODEL)),
                  const((D_MODEL, 128)), const((1, 128))],
        out_specs=[tok(D_MODEL), tok(D_MODEL), tok(128)],
        out_shape=[jax.ShapeDtypeStruct((bsz, lt, D_MODEL), F32),
                   jax.ShapeDtypeStruct((bsz, lt, D_MODEL), F32),
                   jax.ShapeDtypeStruct((bsz, lt, 128), F32)],
        compiler_params=_cparams(("parallel", "parallel")),
        name="ev_out",
    )(o_f, o_b, g, y_f, y_b, u, xcat, mods, gn, ds, wglu, wout, wr, br)


def _route(logits):
    n_tok = logits.shape[0]
    lg = logits[:, :N_GROUPS]
    grp = jnp.argmax(lg, axis=-1)
    g_w = jnp.max(jax.nn.softmax(lg, axis=-1), axis=-1, keepdims=True)
    le = logits[:, N_GROUPS:N_GROUPS + N_EXPERTS].reshape(n_tok, N_GROUPS, EXP_PER_GROUP)
    le = le[jnp.arange(n_tok), grp]
    top_p, top_i = lax.top_k(jax.nn.softmax(le, axis=-1), TOP_K)
    gate = g_w * top_p / jnp.sum(top_p, axis=-1, keepdims=True)
    eid = (grp[:, None] * EXP_PER_GROUP + top_i).reshape(-1)
    n_asg = n_tok * TOP_K
    order = jnp.argsort(eid)
    e_sorted = eid[order]
    counts = jnp.bincount(eid, length=N_EXPERTS)
    padded = (counts + MOE_BLOCK - 1) // MOE_BLOCK * MOE_BLOCK
    pad_end = jnp.cumsum(padded)
    pad_start = pad_end - padded
    cnt_start = jnp.cumsum(counts) - counts
    dest = pad_start[e_sorted] + jnp.arange(n_asg) - cnt_start[e_sorted]
    n_blocks = -(-n_asg // MOE_BLOCK) + N_EXPERTS
    slot_buf = jnp.full((n_blocks * MOE_BLOCK,), n_asg, jnp.int32).at[dest].set(order.astype(jnp.int32))
    blk_e = jnp.minimum(jnp.searchsorted(pad_end, jnp.arange(n_blocks) * MOE_BLOCK, side='right'),
                        N_EXPERTS - 1).astype(jnp.int32)
    n_valid = jnp.sum((slot_buf < n_asg).reshape(n_blocks, MOE_BLOCK), axis=1).astype(jnp.int32)
    return slot_buf, blk_e, n_valid, gate.astype(F32)


def _moe_kernel(slot_ref, blke_ref, nvalid_ref, h_hbm, w1_ref, w3_ref, w2_ref, z_hbm,
                xbuf, ybuf, gsem, ssem, *, n_tok):
    i = pl.program_id(0)
    nblk = pl.num_programs(0)
    cur = i % 2

    def gather(blk, buf):
        def body(r, carry):
            tok = jnp.minimum(slot_ref[blk * MOE_BLOCK + r] // TOP_K, n_tok - 1)
            pltpu.make_async_copy(h_hbm.at[pl.ds(tok, 1)], xbuf.at[buf, pl.ds(r, 1)], gsem.at[buf]).start()
            return carry
        lax.fori_loop(0, MOE_BLOCK, body, 0)

    def wait_gather(buf):
        pltpu.make_async_copy(h_hbm.at[pl.ds(0, MOE_BLOCK)], xbuf.at[buf], gsem.at[buf]).wait()

    def scatter(blk, buf):
        def body(r, carry):
            slot = slot_ref[blk * MOE_BLOCK + r]
            pltpu.make_async_copy(ybuf.at[buf, pl.ds(r, 1)], z_hbm.at[pl.ds(slot, 1)], ssem.at[buf]).start()
            return carry
        lax.fori_loop(0, nvalid_ref[blk], body, 0)

    def wait_scatter(blk, buf):
        n = nvalid_ref[blk]

        @pl.when(n == MOE_BLOCK)
        def _():
            pltpu.make_async_copy(ybuf.at[buf], z_hbm.at[pl.ds(0, MOE_BLOCK)], ssem.at[buf]).wait()

        @pl.when(n < MOE_BLOCK)
        def _():
            def body(r, carry):
                pltpu.make_async_copy(ybuf.at[buf, pl.ds(0, 1)], z_hbm.at[pl.ds(0, 1)], ssem.at[buf]).wait()
                return carry
            lax.fori_loop(0, n, body, 0)

    used = nvalid_ref[i] > 0

    @pl.when(jnp.logical_and(i == 0, used))
    def _():
        gather(0, 0)

    @pl.when(used)
    def _():
        wait_gather(cur)

        @pl.when(jnp.logical_and(i + 1 < nblk, nvalid_ref[jnp.minimum(i + 1, nblk - 1)] > 0))
        def _():
            gather(i + 1, 1 - cur)

        x = xbuf[cur].astype(BF16)
        h1 = jnp.dot(x, w1_ref[...], preferred_element_type=F32)
        h3 = jnp.dot(x, w3_ref[...], preferred_element_type=F32)
        a = (h1 * _sigmoid(h1) * h3).astype(BF16)
        y = jnp.dot(a, w2_ref[...], preferred_element_type=F32)

        @pl.when(i >= 2)
        def _():
            wait_scatter(i - 2, cur)

        ybuf[cur] = y
        scatter(i, cur)

    first_unused = jnp.logical_and(jnp.logical_not(used),
                                   jnp.logical_and(i > 0, nvalid_ref[jnp.maximum(i - 1, 0)] > 0))
    last_used = jnp.logical_and(used, i == nblk - 1)

    def drain(last):
        wait_scatter(last, last % 2)

        @pl.when(last >= 1)
        def _():
            wait_scatter(last - 1, (last - 1) % 2)

    @pl.when(first_unused)
    def _():
        drain(i - 1)

    @pl.when(last_used)
    def _():
        drain(i)


def _moe_experts(h2, slot_buf, blk_e, n_valid, w1, w3, w2):
    n_tok = h2.shape[0]
    n_blocks = blk_e.shape[0]
    wspec = lambda shp: pl.BlockSpec((None,) + shp, lambda i, slot, blke, nvalid: (blke[i], 0, 0))
    grid_spec = pltpu.PrefetchScalarGridSpec(
        num_scalar_prefetch=3,
        grid=(n_blocks,),
        in_specs=[pl.BlockSpec(memory_space=pl.ANY),
                  wspec((D_MODEL, D_EXPERT)), wspec((D_MODEL, D_EXPERT)), wspec((D_EXPERT, D_MODEL))],
        out_specs=pl.BlockSpec(memory_space=pl.ANY),
        scratch_shapes=[pltpu.VMEM((2, MOE_BLOCK, D_MODEL), F32), pltpu.VMEM((2, MOE_BLOCK, D_MODEL), F32),
                        pltpu.SemaphoreType.DMA((2,)), pltpu.SemaphoreType.DMA((2,))])
    return pl.pallas_call(
        functools.partial(_moe_kernel, n_tok=n_tok),
        grid_spec=grid_spec,
        out_shape=jax.ShapeDtypeStruct((TOP_K * n_tok, D_MODEL), F32),
        compiler_params=_cparams(("arbitrary",)),
        name="moe_experts",
    )(slot_buf, blk_e, n_valid, h2, w1, w3, w2)


def _moe_combine_kernel(x_ref, z_ref, gate_ref, mod_ref, o_ref):
    gate = gate_ref[...]
    y = gate[:, 0:1] * z_ref[:, :D_MODEL] + gate[:, 1:2] * z_ref[:, D_MODEL:]
    o_ref[...] = x_ref[...] + mod_ref[5:6, :] * y


def _moe_combine(x, z, gate, mods, mod_index):
    bsz, lt, _ = x.shape
    nblk = lt // TB
    z2 = z.reshape(-1, 2 * D_MODEL)
    gate3 = gate.reshape(bsz, lt, TOP_K)
    return pl.pallas_call(
        _moe_combine_kernel,
        grid=(bsz, nblk),
        in_specs=[pl.BlockSpec((None, TB, D_MODEL), lambda b, i: (b, i, 0)),
                  pl.BlockSpec((TB, 2 * D_MODEL), lambda b, i: (b * nblk + i, 0)),
                  pl.BlockSpec((None, TB, TOP_K), lambda b, i: (b, i, 0)),
                  pl.BlockSpec((None, None, 6, D_MODEL), mod_index)],
        out_specs=pl.BlockSpec((None, TB, D_MODEL), lambda b, i: (b, i, 0)),
        out_shape=jax.ShapeDtypeStruct((bsz, lt, D_MODEL), F32),
        compiler_params=_cparams(("parallel", "parallel")),
        name="moe_combine",
    )(x, z2, gate3, mods)


def _moe(x, h2, logits, mods, mod_index, w1, w3, w2):
    bsz, lt, _ = x.shape
    slot_buf, blk_e, n_valid, gate = _route(logits.reshape(bsz * lt, 128))
    z = _moe_experts(h2.reshape(bsz * lt, D_MODEL), slot_buf, blk_e, n_valid, w1, w3, w2)
    return _moe_combine(x, z, gate, mods, mod_index)


def _od_proj_kernel(x_ref, mod_ref, w_ref, gm_ref, qn_ref, kn_ref, q_ref, k_ref, v_ref, zh_ref):
    h = _rms_rows(x_ref[...]) * (1.0 + mod_ref[1:2, :]) + mod_ref[0:1, :]
    z = jnp.dot(h.astype(BF16), w_ref[...], preferred_element_type=F32)

    def head_norm(t, gain):
        ms = jnp.dot(t * t, gm_ref[...], precision=HI, preferred_element_type=F32)
        return t * lax.rsqrt(ms + EPS) * gain

    q_ref[...] = (head_norm(z[:, :NA_W], qn_ref[...]) * (NA_DH ** -0.5)).astype(BF16)
    k_ref[...] = head_norm(z[:, NA_W:2 * NA_W], kn_ref[...]).astype(BF16)
    v_ref[...] = z[:, 2 * NA_W:3 * NA_W].astype(BF16)
    zh_ref[...] = z[:, 3 * NA_W:]


def _od_proj(xcat, mods, w, gm, qn, kn):
    bsz, lt, _ = xcat.shape
    nblk = lt // TB
    tok = lambda n: pl.BlockSpec((None, TB, n), lambda b, i: (b, i, 0))
    const = lambda shp: pl.BlockSpec(shp, lambda b, i: tuple(0 for _ in shp))
    return pl.pallas_call(
        _od_proj_kernel,
        grid=(bsz, nblk),
        in_specs=[tok(D_MODEL), pl.BlockSpec((None, None, 6, D_MODEL), _mod_index),
                  const((D_MODEL, 3 * NA_W + 3 * HY_W)), const((NA_W, NA_W)), const((1, NA_W)), const((1, NA_W))],
        out_specs=[tok(NA_W), tok(NA_W), tok(NA_W), tok(3 * HY_W)],
        out_shape=[jax.ShapeDtypeStruct((bsz, lt, NA_W), BF16), jax.ShapeDtypeStruct((bsz, lt, NA_W), BF16),
                   jax.ShapeDtypeStruct((bsz, lt, NA_W), BF16), jax.ShapeDtypeStruct((bsz, lt, 3 * HY_W), F32)],
        compiler_params=_cparams(("parallel", "parallel")),
        name="od_proj",
    )(xcat, mods, w, gm, qn, kn)


def _na_kernel(q_ref, k_ref, v_ref, t2_ref, o_ref):
    r = pl.program_id(1)
    n_rows = pl.num_programs(1)
    r0 = jnp.clip(r - WIN_R // 2, 0, n_rows - WIN_R)
    off = r0 - r + WIN_R - 1
    base = pl.multiple_of(TB + r0 * GRID_W, GRID_W)
    nloc = WIN_R * GRID_W
    q = q_ref[...]
    kw, vw = k_ref[pl.ds(base, nloc), :], v_ref[pl.ds(base, nloc), :]
    kc, vc = k_ref[0:TB, :], v_ref[0:TB, :]
    lane_hi = lax.broadcasted_iota(jnp.int32, (GRID_W, 128), 1) // NA_DH
    nt = (((1,), (1,)), ((), ()))
    outs = []
    for hp in range(NA_HEADS // 2):
        sl = slice(128 * hp, 128 * (hp + 1))
        q2, k2, v2, kc2, vc2 = q[:, sl], kw[:, sl], vw[:, sl], kc[:, sl], vc[:, sl]
        acc = jnp.zeros((GRID_W, 128), F32)
        for hh in range(2):
            h = 2 * hp + hh
            sel = lane_hi == hh
            qm = jnp.where(sel, q2, jnp.zeros_like(q2))
            bias = jnp.concatenate([t2_ref[h, off + 2 * m] for m in range(WIN_R // 2)], axis=1)
            s_loc = lax.dot_general(qm, k2, nt, preferred_element_type=F32) + bias
            s_ctx = lax.dot_general(qm, kc2, nt, preferred_element_type=F32)
            m = jnp.maximum(jnp.max(s_loc, axis=1, keepdims=True), jnp.max(s_ctx, axis=1, keepdims=True))
            p_loc, p_ctx = jnp.exp(s_loc - m), jnp.exp(s_ctx - m)
            den = jnp.sum(p_loc, axis=1, keepdims=True) + jnp.sum(p_ctx, axis=1, keepdims=True)
            o = (jnp.dot(p_loc.astype(BF16), v2, preferred_element_type=F32)
                 + jnp.dot(p_ctx.astype(BF16), vc2, preferred_element_type=F32))
            acc = jnp.where(sel, o / den, acc)
        outs.append(acc)
    o_ref[...] = jnp.concatenate(outs, axis=1)


def _na_bias_table(rpb):
    qc = np.arange(GRID_W)[:, None]
    kc = np.arange(GRID_W)[None, :]
    q_start = np.clip(qc - WIN_C // 2, 0, GRID_W - WIN_C)
    valid = (kc >= q_start) & (kc < q_start + WIN_C)
    col_idx = np.clip(kc - qc + WIN_C - 1, 0, 2 * WIN_C - 2)
    t = jnp.where(valid[None, None], rpb.astype(F32)[:, :, col_idx], NEG_INF)
    return jnp.concatenate([t[:, :-1], t[:, 1:]], axis=-1)


def _na(q, k, v, t2):
    bsz, lt, _ = q.shape
    n_rows = (lt - TB) // GRID_W
    qoff = TB // GRID_W
    return pl.pallas_call(
        _na_kernel,
        grid=(bsz, n_rows),
        in_specs=[pl.BlockSpec((None, GRID_W, NA_W), lambda b, r: (b, r + qoff, 0)),
                  pl.BlockSpec((None, lt, NA_W), lambda b, r: (b, 0, 0)),
                  pl.BlockSpec((None, lt, NA_W), lambda b, r: (b, 0, 0)),
                  pl.BlockSpec(t2.shape, lambda b, r: (0, 0, 0, 0))],
        out_specs=pl.BlockSpec((None, GRID_W, NA_W), lambda b, r: (b, r, 0)),
        out_shape=jax.ShapeDtypeStruct((bsz, lt - TB, NA_W), F32),
        compiler_params=_cparams(("parallel", "arbitrary")),
        name="na_attn",
    )(q, k, v, t2)


def _hy_pre_kernel(z_ref, zp_ref, zn_ref, cw_ref, cb_ref, x0_ref, u_ref, ut_ref):
    i = pl.program_id(1)
    n = pl.num_programs(1)
    z = z_ref[...]
    tb = z.shape[0]
    prev_row = jnp.where(i > 0, zp_ref[7:8, :], 0.0)
    next_row = jnp.where(i < n - 1, zn_ref[0:1, :], 0.0)
    rowid = lax.broadcasted_iota(jnp.int32, z.shape, 0)
    zm = jnp.where(rowid == 0, prev_row, pltpu.roll(z, 1, 0))
    zp = jnp.where(rowid == tb - 1, next_row, pltpu.roll(z, tb - 1, 0))
    zc = cb_ref[...] + cw_ref[0:1, :] * zm
    zc = zc + cw_ref[1:2, :] * z
    zc = zc + cw_ref[2:3, :] * zp
    x0_ref[...] = zc[:, :HY_W]
    u = zc[:, HY_W:2 * HY_W] * zc[:, 2 * HY_W:]
    u_ref[...] = u
    for j in range(tb // FFT_N1):
        ut_ref[j] = u[j * FFT_N1:(j + 1) * FFT_N1, :].T.astype(BF16)


def _hy_pre(zh, cw, cb):
    bsz, lt, _ = zh.shape
    l = lt - TB
    nblk = l // TB
    h8 = TB // 8
    return pl.pallas_call(
        _hy_pre_kernel,
        grid=(bsz, nblk),
        in_specs=[pl.BlockSpec((None, TB, 3 * HY_W), lambda b, i: (b, i + 1, 0)),
                  pl.BlockSpec((None, 8, 3 * HY_W), lambda b, i: (b, (i + 1) * h8 - 1, 0)),
                  pl.BlockSpec((None, 8, 3 * HY_W), lambda b, i: (b, jnp.minimum((i + 2) * h8, lt // 8 - 1), 0)),
                  pl.BlockSpec((HY_SHORT, 3 * HY_W), lambda b, i: (0, 0)),
                  pl.BlockSpec((1, 3 * HY_W), lambda b, i: (0, 0))],
        out_specs=[pl.BlockSpec((None, TB, HY_W), lambda b, i: (b, i, 0)),
                   pl.BlockSpec((None, TB, HY_W), lambda b, i: (b, i, 0)),
                   pl.BlockSpec((None, TB // FFT_N1, HY_W, FFT_N1), lambda b, i: (b, i, 0, 0))],
        out_shape=[jax.ShapeDtypeStruct((bsz, l, HY_W), F32), jax.ShapeDtypeStruct((bsz, l, HY_W), F32),
                   jax.ShapeDtypeStruct((bsz, l // FFT_N1, HY_W, FFT_N1), BF16)],
        compiler_params=_cparams(("parallel", "parallel")),
        name="hy_pre",
    )(zh, zh, zh, cw, cb)


def _fft_consts(n1_in):
    n = FFT_N1
    idx = np.arange(n)
    ang1 = 2.0 * np.pi * np.outer(idx, idx) / n
    c, s = np.cos(ang1), np.sin(ang1)
    angt = 2.0 * np.pi * np.outer(idx, idx) / (n * n)
    tw = np.concatenate([np.cos(angt), -np.sin(angt)], axis=1)
    f3 = np.block([[c, -s], [s, c]])
    f3i = np.block([[c, s], [-s, c]])
    ch, sh = c[:, :n1_in], s[:, :n1_in]
    f1_pair = np.block([[ch, sh], [-sh, ch]])
    f1_real = np.concatenate([c, -s], axis=0)
    f1i = np.block([[ch.T, -sh.T], [sh.T, ch.T]]) / (n * n)
    return tw, f3, f3i, f1_pair, f1_real, f1i


def _fft_forward(a, tw_re, tw_im, lhs_scr, ncg):
    for cix in range(ncg):
        cs = slice(cix * FFT_N1, (cix + 1) * FFT_N1)
        are, aim = a[:FFT_N1, cs], a[FFT_N1:, cs]
        lhs_scr[cs, :FFT_N1] = are * tw_re - aim * tw_im
        lhs_scr[cs, FFT_N1:] = are * tw_im + aim * tw_re


def _hy_filt_kernel(k_ref, f1_ref, tw_ref, f3_ref, o_ref, lhs_scr):
    a = jnp.dot(f1_ref[...], k_ref[...], precision=HI, preferred_element_type=F32)
    _fft_forward(a, tw_ref[:, :FFT_N1], tw_ref[:, FFT_N1:], lhs_scr, FFT_CG)
    o_ref[...] = jnp.dot(lhs_scr[...], f3_ref[...], precision=HI, preferred_element_type=F32)


def _hy_fft_kernel(u_ref, kf_ref, f1_ref, tw_ref, f3_ref, f3i_ref, f1i_ref, y_ref, lhs_scr, a2_scr):
    cgl = FFT_CG * FFT_N1
    x = u_ref[...].reshape(2 * u_ref.shape[1], cgl)
    a = jnp.dot(f1_ref[...], x, preferred_element_type=F32)
    tw_re, tw_im = tw_ref[:, :FFT_N1], tw_ref[:, FFT_N1:]
    _fft_forward(a, tw_re, tw_im, lhs_scr, FFT_CG)
    y = jnp.dot(lhs_scr[...].astype(BF16), f3_ref[...], preferred_element_type=F32)
    yre, yim = y[:, :FFT_N1], y[:, FFT_N1:]
    kre, kim = kf_ref[:, :FFT_N1], kf_ref[:, FFT_N1:]
    z = jnp.concatenate([yre * kre - yim * kim, yre * kim + yim * kre], axis=1).astype(BF16)
    bp = jnp.dot(z, f3i_ref[...], preferred_element_type=F32)
    for cix in range(FFT_CG):
        cs = slice(cix * FFT_N1, (cix + 1) * FFT_N1)
        bre, bim = bp[cs, :FFT_N1], bp[cs, FFT_N1:]
        a2_scr[:FFT_N1, cs] = bre * tw_re + bim * tw_im
        a2_scr[FFT_N1:, cs] = bim * tw_re - bre * tw_im
    out = jnp.dot(f1i_ref[...], a2_scr[...].astype(BF16), preferred_element_type=F32)
    y_ref[...] = out.reshape(2, u_ref.shape[1], cgl)


def _hy_conv(ut, kfilt):
    bsz, n1h, nch, _ = ut.shape
    assert 2 * n1h == FFT_N1 and bsz % 2 == 0
    cgl = FFT_CG * FFT_N1
    ncol = nch * FFT_N1
    tw, f3, f3i, f1_pair, f1_real, f1i = _fft_consts(n1h)
    kt = kfilt.reshape(FFT_N1, FFT_N1, nch).transpose(0, 2, 1).reshape(FFT_N1, ncol)
    const2 = lambda shp: pl.BlockSpec(shp, lambda *a: (0, 0))
    kf = pl.pallas_call(
        _hy_filt_kernel,
        grid=(nch // FFT_CG,),
        in_specs=[pl.BlockSpec((FFT_N1, cgl), lambda j: (0, j)), const2((2 * FFT_N1, FFT_N1)),
                  const2((FFT_N1, 2 * FFT_N1)), const2((2 * FFT_N1, 2 * FFT_N1))],
        out_specs=pl.BlockSpec((cgl, 2 * FFT_N1), lambda j: (j, 0)),
        out_shape=jax.ShapeDtypeStruct((ncol, 2 * FFT_N1), F32),
        scratch_shapes=[pltpu.VMEM((cgl, 2 * FFT_N1), F32)],
        compiler_params=_cparams(("parallel",)),
        name="hy_filter_dft",
    )(kt, jnp.asarray(f1_real, F32), jnp.asarray(tw, F32), jnp.asarray(f3, F32))
    u2 = ut.reshape(bsz, n1h, ncol)
    y = pl.pallas_call(
        _hy_fft_kernel,
        grid=(bsz // 2, nch // FFT_CG),
        in_specs=[pl.BlockSpec((2, n1h, cgl), lambda p, j: (p, 0, j)),
                  pl.BlockSpec((cgl, 2 * FFT_N1), lambda p, j: (j, 0)),
                  const2((2 * FFT_N1, FFT_N1)), const2((FFT_N1, 2 * FFT_N1)),
                  const2((2 * FFT_N1, 2 * FFT_N1)), const2((2 * FFT_N1, 2 * FFT_N1)), const2((FFT_N1, 2 * FFT_N1))],
        out_specs=pl.BlockSpec((2, n1h, cgl), lambda p, j: (p, 0, j)),
        out_shape=jax.ShapeDtypeStruct((bsz, n1h, ncol), F32),
        scratch_shapes=[pltpu.VMEM((cgl, 2 * FFT_N1), F32), pltpu.VMEM((2 * FFT_N1, cgl), F32)],
        compiler_params=_cparams(("parallel", "parallel")),
        name="hy_fft_conv",
    )(u2, kf, jnp.asarray(f1_pair, BF16), jnp.asarray(tw, F32), jnp.asarray(f3, BF16),
      jnp.asarray(f3i, BF16), jnp.asarray(f1i, BF16))
    return y.reshape(bsz, n1h, nch, FFT_N1)


def _hy_filter(seqlen, fw1, fb1, fw2, fb2, fw3, fb3, freq, fw4):
    t = jnp.linspace(0.0, 1.0, seqlen, dtype=F32)[:, None]
    bands = (HY_EMB - 1) // 2
    w = 2.0 * math.pi * jnp.arange(seqlen, dtype=F32)[:, None] / seqlen
    f = jnp.linspace(1e-4, bands - 1, bands, dtype=F32)[None, :]
    feat = jnp.concatenate([t, jnp.cos(f * w), -jnp.sin(f * w)], axis=-1)
    h = _mm(feat, fw1, fb1, freq)
    h = _mm(h, fw2, fb2, freq)
    h = _mm(h, fw3, fb3, freq)
    h = _mm(h, fw4).reshape(seqlen, 2, HY_W)
    deltas = jnp.abs(jnp.linspace(math.log(HY_DECAY_TARGET) / HY_DECAY_LONG_PCT,
                                  math.log(HY_DECAY_TARGET) / HY_DECAY_SHORT_PCT, HY_W, dtype=F32))
    h = h * jnp.exp(-t * deltas)[:, None, :]
    return jnp.concatenate([h[:, 0], jnp.zeros((1, HY_W), F32), h[:0:-1, 1]], axis=0)


def _od_out_kernel(na_ref, yt_ref, x0_ref, u_ref, x_ref, mod_ref, hb_ref, wout_ref, wr_ref, br_ref,
                   x_out, h2_out, lg_out):
    y = jnp.concatenate([yt_ref[j].T for j in range(yt_ref.shape[0])], axis=0)
    hy = x0_ref[...] * (y + u_ref[...] * hb_ref[...])
    cat = jnp.concatenate([na_ref[...], hy], axis=1).astype(BF16)
    mix = jnp.dot(cat, wout_ref[...], preferred_element_type=F32)
    _mixer_tail(x_ref[...], mix, mod_ref, wr_ref, br_ref, x_out, h2_out, lg_out)


def _latent_mod_index(b, i):
    return (b, 1, 0, 0)


def _od_out(na, yt, x0, u, xcat, mods, hb, wout, wr, br):
    bsz, l, _ = na.shape
    nblk = l // TB
    tok = lambda n: pl.BlockSpec((None, TB, n), lambda b, i: (b, i, 0))
    const = lambda shp: pl.BlockSpec(shp, lambda b, i: tuple(0 for _ in shp))
    return pl.pallas_call(
        _od_out_kernel,
        grid=(bsz, nblk),
        in_specs=[tok(NA_W), pl.BlockSpec((None, TB // FFT_N1, HY_W, FFT_N1), lambda b, i: (b, i, 0, 0)),
                  tok(HY_W), tok(HY_W),
                  pl.BlockSpec((None, TB, D_MODEL), lambda b, i: (b, i + 1, 0)),
                  pl.BlockSpec((None, None, 6, D_MODEL), _latent_mod_index),
                  const((1, HY_W)), const((D_MODEL, D_MODEL)), const((D_MODEL, 128)), const((1, 128))],
        out_specs=[tok(D_MODEL), tok(D_MODEL), tok(128)],
        out_shape=[jax.ShapeDtypeStruct((bsz, l, D_MODEL), F32), jax.ShapeDtypeStruct((bsz, l, D_MODEL), F32),
                   jax.ShapeDtypeStruct((bsz, l, 128), F32)],
        compiler_params=_cparams(("parallel", "parallel")),
        name="od_out",
    )(na, yt, x0, u, xcat, mods, hb, wout, wr, br)


def _mods(c, c_ctx, ada_w, ada_b):
    bsz = c.shape[0]
    cc = jnp.concatenate([c, c_ctx[None]], axis=0)
    m = _mm(cc * _sigmoid(cc), ada_w, ada_b)
    mod_l = m[:bsz].reshape(bsz, 1, 6, D_MODEL)
    mod_c = jnp.broadcast_to(m[bsz].reshape(1, 1, 6, D_MODEL), (bsz, 1, 6, D_MODEL))
    return jnp.concatenate([mod_c, mod_l], axis=1)


def _rope_tables(seqlen):
    pos = jnp.arange(seqlen)
    half = GLA_DK // 4
    freqs = ROPE_BASE ** (-jnp.arange(half, dtype=F32) / half)
    ar = (pos // GRID_W).astype(F32)[:, None] * freqs
    ac = (pos % GRID_W).astype(F32)[:, None] * freqs
    cos = jnp.concatenate([jnp.cos(ar), jnp.cos(ar), jnp.cos(ac), jnp.cos(ac)], axis=1)
    sin = jnp.concatenate([-jnp.sin(ar), jnp.sin(ar), -jnp.sin(ac), jnp.sin(ac)], axis=1)
    cos = jnp.concatenate([jnp.ones((TB, GLA_DK), F32), cos], axis=0)
    sin = jnp.concatenate([jnp.zeros((TB, GLA_DK), F32), sin], axis=0)
    return jnp.tile(cos, (1, GLA_HEADS)), jnp.tile(sin, (1, GLA_HEADS))


def _router_weights(wg, bg, we, be):
    pad = 128 - N_GROUPS - N_EXPERTS
    wr = jnp.concatenate([wg, we, jnp.zeros((D_MODEL, pad), F32)], axis=1)
    br = jnp.concatenate([bg, be, jnp.zeros((pad,), F32)]).reshape(1, 128)
    return wr, br


def _flip_segments(t):
    return jnp.concatenate([t[:, TB - 1::-1], t[:, :TB - 1:-1]], axis=1)


def kernel(x, c, ctx, c_ctx, ada_w, ada_b, moe_wg, moe_bg, moe_we, moe_be, moe_w1, moe_w3, moe_w2, ev_w_in, ev_w_out, gla_wa2, gla_ba, gla_norm, s5_lam_re, s5_lam_im, s5_log_dt, s5_b_re, s5_b_im, s5_c_re, s5_c_im, s5_d, s5_w_glu, od_w_in, od_w_out, na_q_norm, na_k_norm, na_rpb, hy_conv_w, hy_conv_b, hy_fw1, hy_fb1, hy_fw2, hy_fb2, hy_fw3, hy_fb3, hy_freq, hy_fw4, hy_bias):
    bsz, seqlen, _ = x.shape
    assert ctx.shape[1] == TB and seqlen % TB == 0
    xcat = jnp.concatenate([ctx, x], axis=1)

    mods = _mods(c, c_ctx, ada_w[0], ada_b[0])
    w_in = ev_w_in[0]
    n_a = 2 * GLA_RANK
    a0 = 2 * GLA_QK + 2 * GLA_V
    w_ev = jnp.concatenate([w_in[:, :a0], w_in[:, a0 + n_a:], w_in[:, a0:a0 + n_a],
                            jnp.zeros((D_MODEL, 128 - n_a), F32)], axis=1).astype(BF16)
    wa = jnp.zeros((128, 2 * GLA_QK), F32)
    for d in range(2):
        wa = wa.at[d * GLA_RANK:(d + 1) * GLA_RANK, d * GLA_QK:(d + 1) * GLA_QK].set(gla_wa2[0, d])
    cos, sin = _rope_tables(seqlen)
    q, k, v, g, u, la = _ev_proj(xcat, mods, w_ev, wa, gla_ba[0].reshape(1, 2 * GLA_QK), cos, sin)
    o_f, o_b = _gla(q, k, v, la)
    s5p = [t[0].astype(F32) for t in (s5_lam_re, s5_lam_im, s5_log_dt, s5_b_re, s5_b_im, s5_c_re, s5_c_im)]
    y_f = _s5_scan(u, _s5_mats(*[t[0] for t in s5p]))
    y_b = _flip_segments(_s5_scan(_flip_segments(u), _s5_mats(*[t[1] for t in s5p])))
    wr, br = _router_weights(moe_wg[0], moe_bg[0], moe_we[0], moe_be[0])
    x1, h2, lg = _ev_out(o_f, o_b, g, y_f, y_b, u, xcat, mods,
                         jnp.tile(gla_norm[0], GLA_HEADS).reshape(1, GLA_V), s5_d[0].reshape(1, S5_W),
                         s5_w_glu[0].astype(BF16), ev_w_out[0].astype(BF16), wr, br)
    xcat = _moe(x1, h2, lg, mods, _mod_index,
                moe_w1[0].astype(BF16), moe_w3[0].astype(BF16), moe_w2[0].astype(BF16))

    mods = _mods(c, c_ctx, ada_w[1], ada_b[1])
    hd = np.arange(NA_W) // NA_DH
    gm = jnp.asarray((hd[:, None] == hd[None, :]).astype(np.float32) / NA_DH)
    qh, kh, vh, zh = _od_proj(xcat, mods, od_w_in[0].astype(BF16), gm,
                              jnp.tile(na_q_norm[0], NA_HEADS).reshape(1, NA_W),
                              jnp.tile(na_k_norm[0], NA_HEADS).reshape(1, NA_W))
    na = _na(qh, kh, vh, _na_bias_table(na_rpb[0]))
    x0, uh, ut = _hy_pre(zh, hy_conv_w[0], hy_conv_b[0].reshape(1, 3 * HY_W))
    kfilt = _hy_filter(seqlen, hy_fw1[0], hy_fb1[0], hy_fw2[0], hy_fb2[0], hy_fw3[0], hy_fb3[0],
                       hy_freq[0], hy_fw4[0])
    yt = _hy_conv(ut, kfilt)
    wr, br = _router_weights(moe_wg[1], moe_bg[1], moe_we[1], moe_be[1])
    xl, h2, lg = _od_out(na, yt, x0, uh, xcat, mods, hy_bias[0].reshape(1, HY_W),
                         od_w_out[0].astype(BF16), wr, br)
    return _moe(xl, h2, lg, mods, _latent_mod_index,
                moe_w1[1].astype(BF16), moe_w3[1].astype(BF16), moe_w2[1].astype(BF16))
```

```python
import functools
import math

import numpy as np
import jax
import jax.numpy as jnp
from jax import lax
from jax.experimental import pallas as pl
from jax.experimental.pallas import tpu as pltpu

F32, BF16 = jnp.float32, jnp.bfloat16
HI = lax.Precision.HIGHEST

D_MODEL = 1024
GRID_W = 64
EPS = 1e-6
ROPE_BASE = 10000.0
NEG_INF = -1e30
GLA_HEADS, GLA_DK, GLA_DV = 4, 64, 128
GLA_QK, GLA_V = GLA_HEADS * GLA_DK, GLA_HEADS * GLA_DV
GLA_RANK = 16
GLA_TAU = 16.0
GLA_CHUNK = 64
GLA_LOG_ALPHA_MIN = -1.0
S5_W, S5_H, S5_P = 512, 16, 64
S5_G = S5_W // S5_H
S5_CHUNK = 8
S5_TBLK = 1408
NA_HEADS, NA_DH = 8, 64
NA_W = NA_HEADS * NA_DH
WIN_R, WIN_C = 8, 16
HY_W = 512
HY_SHORT = 3
HY_EMB = 33
HY_DECAY_TARGET = 1e-2
HY_DECAY_SHORT_PCT = 0.3
HY_DECAY_LONG_PCT = 1.5
N_GROUPS, EXP_PER_GROUP = 4, 8
N_EXPERTS = N_GROUPS * EXP_PER_GROUP
D_EXPERT = 512
TOP_K = 2
MOE_BLOCK = 256

TB = 256
FFT_N1 = 128
FFT_CG = 16
V7X_VMEM_LIMIT = 52 * 1024 * 1024


def _cparams(sem):
    return pltpu.CompilerParams(dimension_semantics=sem, vmem_limit_bytes=V7X_VMEM_LIMIT)


def _sigmoid(x):
    return 1.0 / (1.0 + jnp.exp(-x))


def _rms_rows(x):
    return x * lax.rsqrt(jnp.mean(x * x, axis=-1, keepdims=True) + EPS)


def _mm_kernel(a_ref, w_ref, b_ref, f_ref, o_ref, *, use_sin):
    z = jnp.dot(a_ref[...], w_ref[...], precision=HI, preferred_element_type=F32) + b_ref[...]
    if use_sin:
        z = jnp.sin(f_ref[...] * z)
    o_ref[...] = z


def _mm(a, w, bias=None, freq=None):
    m, k = a.shape
    n = w.shape[1]
    mp, kp = -(-m // 8) * 8, -(-k // 128) * 128
    tm = min(mp, 1024)
    mp = -(-mp // tm) * tm
    tn = n if n <= 1024 else 1024
    assert n % tn == 0
    a = jnp.pad(a.astype(F32), ((0, mp - m), (0, kp - k)))
    w = jnp.pad(w.astype(F32), ((0, kp - k), (0, 0)))
    bias = jnp.zeros((n,), F32) if bias is None else bias.astype(F32)
    use_sin = freq is not None
    freq = jnp.ones((n,), F32) if freq is None else freq.astype(F32)
    out = pl.pallas_call(
        functools.partial(_mm_kernel, use_sin=use_sin),
        grid=(mp // tm, n // tn),
        in_specs=[pl.BlockSpec((tm, kp), lambda i, j: (i, 0)),
                  pl.BlockSpec((kp, tn), lambda i, j: (0, j)),
                  pl.BlockSpec((1, tn), lambda i, j: (0, j)),
                  pl.BlockSpec((1, tn), lambda i, j: (0, j))],
        out_specs=pl.BlockSpec((tm, tn), lambda i, j: (i, j)),
        out_shape=jax.ShapeDtypeStruct((mp, n), F32),
        compiler_params=_cparams(("parallel", "parallel")),
        name="small_dense",
    )(a, w, bias.reshape(1, n), freq.reshape(1, n))
    return out[:m]


def _mod_index(b, i):
    return (b, jnp.minimum(i, 1), 0, 0)


def _swapped_index(nblk, b, i):
    return (b, jnp.where(i == 0, nblk - 1, i - 1), 0)


EV_NQ, EV_NK, EV_NV, EV_NG, EV_NU = 0, 256, 512, 1024, 1536
EV_NA = 2048
EV_NTOT = 2176


def _ev_proj_kernel(x_ref, mod_ref, w_ref, wa_ref, ba_ref, cos_ref, sin_ref,
                    q_ref, k_ref, v_ref, g_ref, u_ref, usw_ref, la_ref):
    x = x_ref[...]
    h = _rms_rows(x) * (1.0 + mod_ref[1:2, :]) + mod_ref[0:1, :]
    z = jnp.dot(h.astype(BF16), w_ref[...], preferred_element_type=F32)
    lane = lax.broadcasted_iota(jnp.int32, (x.shape[0], GLA_QK), 1)
    first = (lane % 32) < 16
    cos, sin = cos_ref[...], sin_ref[...]

    def rot(t):
        partner = jnp.where(first, pltpu.roll(t, GLA_QK - 16, 1), pltpu.roll(t, 16, 1))
        return t * cos + partner * sin

    q_ref[...] = rot(z[:, EV_NQ:EV_NQ + GLA_QK]) * (GLA_DK ** -0.5)
    k_ref[...] = rot(z[:, EV_NK:EV_NK + GLA_QK])
    v_ref[...] = z[:, EV_NV:EV_NV + GLA_V].astype(BF16)
    g_ref[...] = z[:, EV_NG:EV_NG + GLA_V]
    u_ref[...] = z[:, EV_NU:EV_NU + S5_W]
    usw_ref[...] = z[:, EV_NU:EV_NU + S5_W]
    a = z[:, EV_NA:EV_NA + 128]
    pre = jnp.dot(a, wa_ref[...], precision=HI, preferred_element_type=F32) + ba_ref[...]
    ls = jnp.minimum(pre, 0.0) - jnp.log1p(jnp.exp(-jnp.abs(pre)))
    la_ref[...] = jnp.maximum(ls / GLA_TAU, GLA_LOG_ALPHA_MIN)


def _ev_proj(xcat, mods, w, wa, ba, cos, sin):
    bsz, lt, _ = xcat.shape
    nblk = lt // TB
    tok = lambda n: pl.BlockSpec((None, TB, n), lambda b, i: (b, i, 0))
    const = lambda shp: pl.BlockSpec(shp, lambda b, i: tuple(0 for _ in shp))
    return pl.pallas_call(
        _ev_proj_kernel,
        grid=(bsz, nblk),
        in_specs=[tok(D_MODEL),
                  pl.BlockSpec((None, None, 6, D_MODEL), _mod_index),
                  const((D_MODEL, EV_NTOT)), const((128, 2 * GLA_QK)), const((1, 2 * GLA_QK)),
                  pl.BlockSpec((TB, GLA_QK), lambda b, i: (i, 0)),
                  pl.BlockSpec((TB, GLA_QK), lambda b, i: (i, 0))],
        out_specs=[tok(GLA_QK), tok(GLA_QK), tok(GLA_V), tok(GLA_V), tok(S5_W),
                   pl.BlockSpec((None, TB, S5_W), functools.partial(_swapped_index, nblk)), tok(2 * GLA_QK)],
        out_shape=[jax.ShapeDtypeStruct((bsz, lt, GLA_QK), F32),
                   jax.ShapeDtypeStruct((bsz, lt, GLA_QK), F32),
                   jax.ShapeDtypeStruct((bsz, lt, GLA_V), BF16),
                   jax.ShapeDtypeStruct((bsz, lt, GLA_V), F32),
                   jax.ShapeDtypeStruct((bsz, lt, S5_W), F32),
                   jax.ShapeDtypeStruct((bsz, lt, S5_W), F32),
                   jax.ShapeDtypeStruct((bsz, lt, 2 * GLA_QK), F32)],
        compiler_params=_cparams(("parallel", "parallel")),
        name="ev_proj",
    )(xcat, mods, w, wa, ba, cos, sin)


def _gla_kernel(qf_ref, kf_ref, vf_ref, laf_ref, qb_ref, kb_ref, vb_ref, lab_ref,
                of_ref, ob_ref, s_scr):
    i = pl.program_id(1)

    @pl.when(i == 0)
    def _():
        s_scr[...] = jnp.zeros_like(s_scr)

    c = GLA_CHUNK
    row = lax.broadcasted_iota(jnp.int32, (c, c), 0)
    col = lax.broadcasted_iota(jnp.int32, (c, c), 1)
    lane_head = lax.broadcasted_iota(jnp.int32, (c, GLA_QK), 1) // GLA_DK
    bd_mask = (lax.broadcasted_iota(jnp.int32, (GLA_QK, GLA_V), 0) // GLA_DK
               == lax.broadcasted_iota(jnp.int32, (GLA_QK, GLA_V), 1) // GLA_DV)
    ones = jnp.ones((c, 128), F32)
    nchunk = qf_ref.shape[0] // c

    def one_chunk(refs, o_ref, d, r0):
        q_ref, k_ref, v_ref, la_ref = refs
        fwd = d == 0
        sl = pl.ds(r0, c)
        qc, kc, vc, lac = q_ref[sl, :], k_ref[sl, :], v_ref[sl, :], la_ref[sl, :]
        tri = ((row >= col) if fwd else (row <= col)).astype(F32)
        b = jnp.dot(tri, lac, precision=HI, preferred_element_type=F32)
        b_last = b[c - 1:c, :] if fwd else b[0:1, :]
        qe = (qc * jnp.exp(b)).astype(BF16)
        ke = (kc * jnp.exp(-b)).astype(BF16)
        kd = (kc * jnp.exp(b_last - b)).astype(BF16)
        dec = jnp.exp(lax.dot_general(lac, ones, (((0,), (0,)), ((), ())),
                                      precision=HI, preferred_element_type=F32))
        s = s_scr[d]
        o = jnp.dot(qe, s.astype(BF16), preferred_element_type=F32)
        att_mask = (row >= col) if fwd else (row < col)
        intra = []
        for h in range(GLA_HEADS):
            qh = jnp.where(lane_head == h, qe, jnp.zeros_like(qe))
            att = lax.dot_general(qh, ke, (((1,), (1,)), ((), ())), preferred_element_type=F32)
            att = jnp.where(att_mask, att, 0.0).astype(BF16)
            intra.append(jnp.dot(att, vc[:, h * GLA_DV:(h + 1) * GLA_DV], preferred_element_type=F32))
        o_ref[sl, :] = o + jnp.concatenate(intra, axis=1)
        upd = lax.dot_general(kd, vc, (((0,), (0,)), ((), ())), preferred_element_type=F32)
        s_dec = jnp.concatenate([s[:, h * GLA_DV:(h + 1) * GLA_DV] * dec for h in range(GLA_HEADS)], axis=1)
        s_scr[d] = s_dec + jnp.where(bd_mask, upd, 0.0)

    def body(j, carry):
        one_chunk((qf_ref, kf_ref, vf_ref, laf_ref), of_ref, 0, pl.multiple_of(j * c, c))
        one_chunk((qb_ref, kb_ref, vb_ref, lab_ref), ob_ref, 1, pl.multiple_of((nchunk - 1 - j) * c, c))
        return carry

    lax.fori_loop(0, nchunk, body, 0)


def _gla(q, k, v, la):
    bsz, lt, _ = q.shape
    nblk = lt // TB
    fwd_map = lambda b, i: (b, i, 0)
    bwd_blk = lambda i: jnp.where(i == 0, 0, nblk - i)
    bwd_map = lambda b, i: (b, bwd_blk(i), 0)
    bwd_map_la = lambda b, i: (b, bwd_blk(i), 1)
    spec = lambda n, m: pl.BlockSpec((None, TB, n), m)
    return pl.pallas_call(
        _gla_kernel,
        grid=(bsz, nblk),
        in_specs=[spec(GLA_QK, fwd_map), spec(GLA_QK, fwd_map), spec(GLA_V, fwd_map), spec(GLA_QK, fwd_map),
                  spec(GLA_QK, bwd_map), spec(GLA_QK, bwd_map), spec(GLA_V, bwd_map), spec(GLA_QK, bwd_map_la)],
        out_specs=[spec(GLA_V, fwd_map), spec(GLA_V, bwd_map)],
        out_shape=[jax.ShapeDtypeStruct((bsz, lt, GLA_V), F32), jax.ShapeDtypeStruct((bsz, lt, GLA_V), F32)],
        scratch_shapes=[pltpu.VMEM((2, GLA_QK, GLA_V), F32)],
        compiler_params=_cparams(("parallel", "arbitrary")),
        name="gla_scan",
    )(q, k, v, la, q, k, v, la)


def _s5_kernel(u_ref, wm_ref, tm_ref, cm_ref, ar_ref, ai_ref, y_ref, w_scr, hp_scr, h_scr, *, rev):
    @pl.when(pl.program_id(1) == 0)
    def _():
        h_scr[...] = jnp.zeros_like(h_scr)

    bsz, ntok, _ = u_ref.shape
    nc = ntok // S5_CHUNK
    half = 8 * S5_P
    x = jnp.concatenate(
        [jnp.concatenate([u_ref[b, pl.ds(s, nc, stride=S5_CHUNK), :] for s in range(S5_CHUNK)], axis=1)
         for b in range(bsz)], axis=0).astype(BF16)
    w_scr[...] = jnp.dot(x, wm_ref[...], preferred_element_type=F32)
    ar, ai = ar_ref[...], ai_ref[...]

    def body(j, hs):
        c = (nc - 1 - j) if rev else j
        out = []
        for b in range(bsz):
            re, im = hs[b]
            r = b * nc + c
            hp_scr[pl.ds(r, 1), :] = jnp.concatenate([re, im], axis=1)
            w = w_scr[pl.ds(r, 1), :]
            out.append((ar * re - ai * im + w[:, :half], ar * im + ai * re + w[:, half:]))
        return tuple(out)

    hs = lax.fori_loop(0, nc, body, tuple((h_scr[b:b + 1, :half], h_scr[b:b + 1, half:]) for b in range(bsz)))
    for b in range(bsz):
        h_scr[b:b + 1, :] = jnp.concatenate(hs[b], axis=1)
    y = (jnp.dot(x, tm_ref[...], preferred_element_type=F32)
         + jnp.dot(hp_scr[...].astype(BF16), cm_ref[...], preferred_element_type=F32))
    for b in range(bsz):
        for s in range(S5_CHUNK):
            y_ref[b, pl.ds(s, nc, stride=S5_CHUNK), :] = y[b * nc:(b + 1) * nc, s * 128:(s + 1) * 128]


def _s5_mats(lam_re, lam_im, log_dt, b_re, b_im, c_re, c_im, rev):
    t16 = S5_CHUNK
    dt = jnp.exp(log_dt)[:, None]
    mag = jnp.exp(lam_re * dt)
    a_re, a_im = mag * jnp.cos(lam_im * dt), mag * jnp.sin(lam_im * dt)
    den = lam_re * lam_re + lam_im * lam_im
    nr = a_re - 1.0
    co_re = ((nr * lam_re + a_im * lam_im) / den)[..., None]
    co_im = ((a_im * lam_re - nr * lam_im) / den)[..., None]
    bb_re, bb_im = co_re * b_re - co_im * b_im, co_re * b_im + co_im * b_re
    pr, pi = [jnp.ones_like(a_re)], [jnp.zeros_like(a_im)]
    for _ in range(t16):
        pr, pi = pr + [pr[-1] * a_re - pi[-1] * a_im], pi + [pr[-1] * a_im + pi[-1] * a_re]
    pw_re, pw_im = jnp.stack(pr), jnp.stack(pi)
    g = lam_re.shape[0]
    e_re, e_im = pw_re[t16 - 1::-1][:t16], pw_im[t16 - 1::-1][:t16]
    wre = jnp.einsum('sgp,gph->gshp', e_re, bb_re) - jnp.einsum('sgp,gph->gshp', e_im, bb_im)
    wim = jnp.einsum('sgp,gph->gshp', e_re, bb_im) + jnp.einsum('sgp,gph->gshp', e_im, bb_re)
    cb_re = jnp.einsum('gkp,gph->gpkh', c_re, bb_re) - jnp.einsum('gkp,gph->gpkh', c_im, bb_im)
    cb_im = jnp.einsum('gkp,gph->gpkh', c_re, bb_im) + jnp.einsum('gkp,gph->gpkh', c_im, bb_re)
    kd = jnp.einsum('dgp,gpkh->dgkh', pw_re[:t16], cb_re) - jnp.einsum('dgp,gpkh->dgkh', pw_im[:t16], cb_im)
    lag = np.arange(t16)[None, :] - np.arange(t16)[:, None]
    toe = jnp.where((lag >= 0)[:, :, None, None, None], kd[np.clip(lag, 0, t16 - 1)], 0.0)
    toe = toe.transpose(2, 0, 4, 1, 3)
    q_re, q_im = pw_re[1:], pw_im[1:]
    ca_re = jnp.einsum('gkp,tgp->gptk', c_re, q_re) - jnp.einsum('gkp,tgp->gptk', c_im, q_im)
    ca_im = jnp.einsum('gkp,tgp->gptk', c_re, q_im) + jnp.einsum('gkp,tgp->gptk', c_im, q_re)
    if rev:
        wre, wim = wre[:, ::-1], wim[:, ::-1]
        toe = toe[:, ::-1, :, ::-1]
        ca_re, ca_im = ca_re[:, :, ::-1], ca_im[:, :, ::-1]
    nq, gl = g // 8, 8
    eye = jnp.eye(gl, dtype=F32)
    pack_w = lambda t: jnp.einsum('qgshp,gk->qsghkp', t.reshape(nq, gl, t16, S5_H, S5_P),
                                  eye).reshape(nq, t16 * 128, gl * S5_P)
    wm = jnp.concatenate([pack_w(wre), pack_w(wim)], axis=-1)
    tmat = jnp.einsum('qgshtk,gj->qsghtjk', toe.reshape(nq, gl, t16, S5_H, t16, S5_H),
                      eye).reshape(nq, t16 * 128, t16 * 128)
    pack_c = lambda t: jnp.einsum('qgptk,gj->qgptjk', t.reshape(nq, gl, S5_P, t16, S5_H),
                                  eye).reshape(nq, gl * S5_P, t16 * 128)
    cm = jnp.concatenate([pack_c(ca_re), -pack_c(ca_im)], axis=1)
    return (wm.astype(BF16), tmat.astype(BF16), cm.astype(BF16),
            pw_re[t16].reshape(nq, 1, gl * S5_P), pw_im[t16].reshape(nq, 1, gl * S5_P))


def _s5_scan(u, mats, rev):
    bsz, lt, _ = u.shape
    tblk = max(d for d in range(64, S5_TBLK + 1, 64) if lt % d == 0)
    nblk = lt // tblk
    nq = S5_G // 8
    wm, tmat, cm, ar, ai = mats
    kw, ks = S5_CHUNK * 128, 8 * S5_P
    tmap = (lambda q, t: (0, nblk - 1 - t, q)) if rev else (lambda q, t: (0, t, q))
    per = lambda shp: pl.BlockSpec((None,) + shp, lambda q, t: (q,) + tuple(0 for _ in shp))
    rows = bsz * tblk // S5_CHUNK
    return pl.pallas_call(
        functools.partial(_s5_kernel, rev=rev),
        grid=(nq, nblk),
        in_specs=[pl.BlockSpec((bsz, tblk, 128), tmap),
                  per((kw, 2 * ks)), per((kw, kw)), per((2 * ks, kw)), per((1, ks)), per((1, ks))],
        out_specs=pl.BlockSpec((bsz, tblk, 128), tmap),
        out_shape=jax.ShapeDtypeStruct((bsz, lt, S5_W), F32),
        scratch_shapes=[pltpu.VMEM((rows, 2 * ks), F32), pltpu.VMEM((rows, 2 * ks), F32),
                        pltpu.VMEM((8, 2 * ks), F32)],
        compiler_params=_cparams(("parallel", "arbitrary")),
        name="s5_scan",
    )(u, wm, tmat, cm, ar, ai)


def _mixer_tail(x, mix, mod_ref, wr_ref, br_ref, x_out, h2_out, lg_out):
    xn = x + mod_ref[2:3, :] * mix
    x_out[...] = xn
    h2 = _rms_rows(xn) * (1.0 + mod_ref[4:5, :]) + mod_ref[3:4, :]
    h2_out[...] = h2
    lg_out[...] = jnp.dot(h2, wr_ref[...], precision=HI, preferred_element_type=F32) + br_ref[...]


def _ev_out_kernel(of_ref, ob_ref, g_ref, yf_ref, yb_ref, u_ref, x_ref, mod_ref,
                   gn_ref, ds_ref, wglu_ref, wout_ref, wr_ref, br_ref,
                   x_out, h2_out, lg_out):
    o = of_ref[...] + ob_ref[...]
    og = jnp.concatenate([_rms_rows(o[:, h * GLA_DV:(h + 1) * GLA_DV]) for h in range(GLA_HEADS)], axis=1)
    g = g_ref[...]
    og = og * gn_ref[...] * (g * _sigmoid(g))
    t = yf_ref[...] + yb_ref[...] + ds_ref[...] * u_ref[...]
    y = t * (0.5 * (1.0 + jnp.tanh(math.sqrt(2.0 / math.pi) * (t + 0.044715 * (t * t * t)))))
    y = y * _sigmoid(jnp.dot(y.astype(BF16), wglu_ref[...], preferred_element_type=F32))
    cat = jnp.concatenate([og, y], axis=1).astype(BF16)
    mix = jnp.dot(cat, wout_ref[...], preferred_element_type=F32)
    _mixer_tail(x_ref[...], mix, mod_ref, wr_ref, br_ref, x_out, h2_out, lg_out)


def _ev_out(o_f, o_b, g, y_f, y_b, u, xcat, mods, gn, ds, wglu, wout, wr, br):
    bsz, lt, _ = xcat.shape
    nblk = lt // TB
    tok = lambda n: pl.BlockSpec((None, TB, n), lambda b, i: (b, i, 0))
    const = lambda shp: pl.BlockSpec(shp, lambda b, i: tuple(0 for _ in shp))
    return pl.pallas_call(
        _ev_out_kernel,
        grid=(bsz, nblk),
        in_specs=[tok(512), tok(512), tok(512), tok(512),
                  pl.BlockSpec((None, TB, S5_W), functools.partial(_swapped_index, nblk)),
                  tok(512), tok(D_MODEL),
                  pl.BlockSpec((None, None, 6, D_MODEL), _mod_index),
                  const((1, 512)), const((1, 512)), const((512, 512)), const((D_MODEL, D_MODEL)),
                  const((D_MODEL, 128)), const((1, 128))],
        out_specs=[tok(D_MODEL), tok(D_MODEL), tok(128)],
        out_shape=[jax.ShapeDtypeStruct((bsz, lt, D_MODEL), F32),
                   jax.ShapeDtypeStruct((bsz, lt, D_MODEL), F32),
                   jax.ShapeDtypeStruct((bsz, lt, 128), F32)],
        compiler_params=_cparams(("parallel", "parallel")),
        name="ev_out",
    )(o_f, o_b, g, y_f, y_b, u, xcat, mods, gn, ds, wglu, wout, wr, br)


def _route(logits):
    n_tok = logits.shape[0]
    lg = logits[:, :N_GROUPS]
    grp = jnp.argmax(lg, axis=-1)
    g_w = jnp.max(jax.nn.softmax(lg, axis=-1), axis=-1, keepdims=True)
    le = logits[:, N_GROUPS:N_GROUPS + N_EXPERTS].reshape(n_tok, N_GROUPS, EXP_PER_GROUP)
    le = le[jnp.arange(n_tok), grp]
    top_p, top_i = lax.top_k(jax.nn.softmax(le, axis=-1), TOP_K)
    gate = g_w * top_p / jnp.sum(top_p, axis=-1, keepdims=True)
    eid = (grp[:, None] * EXP_PER_GROUP + top_i).reshape(-1)
    n_asg = n_tok * TOP_K
    order = jnp.argsort(eid)
    e_sorted = eid[order]
    counts = jnp.bincount(eid, length=N_EXPERTS)
    padded = (counts + MOE_BLOCK - 1) // MOE_BLOCK * MOE_BLOCK
    pad_end = jnp.cumsum(padded)
    pad_start = pad_end - padded
    cnt_start = jnp.cumsum(counts) - counts
    dest = pad_start[e_sorted] + jnp.arange(n_asg) - cnt_start[e_sorted]
    n_blocks = -(-n_asg // MOE_BLOCK) + N_EXPERTS
    slot_buf = jnp.full((n_blocks * MOE_BLOCK,), n_asg, jnp.int32).at[dest].set(order.astype(jnp.int32))
    blk_e = jnp.minimum(jnp.searchsorted(pad_end, jnp.arange(n_blocks) * MOE_BLOCK, side='right'),
                        N_EXPERTS - 1).astype(jnp.int32)
    n_valid = jnp.sum((slot_buf < n_asg).reshape(n_blocks, MOE_BLOCK), axis=1).astype(jnp.int32)
    return slot_buf, blk_e, n_valid, gate.astype(F32)


def _moe_kernel(slot_ref, blke_ref, nvalid_ref, h_hbm, w1_ref, w3_ref, w2_ref, z_hbm,
                xbuf, ybuf, gsem, ssem, *, n_tok):
    i = pl.program_id(0)
    nblk = pl.num_programs(0)
    cur = i % 2

    def gather(blk, buf):
        def body(r, carry):
            tok = jnp.minimum(slot_ref[blk * MOE_BLOCK + r] // TOP_K, n_tok - 1)
            pltpu.make_async_copy(h_hbm.at[pl.ds(tok, 1)], xbuf.at[buf, pl.ds(r, 1)], gsem.at[buf]).start()
            return carry
        lax.fori_loop(0, MOE_BLOCK, body, 0)

    def wait_gather(buf):
        pltpu.make_async_copy(h_hbm.at[pl.ds(0, MOE_BLOCK)], xbuf.at[buf], gsem.at[buf]).wait()

    def scatter(blk, buf):
        def body(r, carry):
            asg = slot_ref[blk * MOE_BLOCK + r]
            row = (asg % TOP_K) * n_tok + asg // TOP_K
            pltpu.make_async_copy(ybuf.at[buf, pl.ds(r, 1)], z_hbm.at[pl.ds(row, 1)], ssem.at[buf]).start()
            return carry
        lax.fori_loop(0, nvalid_ref[blk], body, 0)

    def wait_scatter(blk, buf):
        n = nvalid_ref[blk]

        @pl.when(n == MOE_BLOCK)
        def _():
            pltpu.make_async_copy(ybuf.at[buf], z_hbm.at[pl.ds(0, MOE_BLOCK)], ssem.at[buf]).wait()

        @pl.when(n < MOE_BLOCK)
        def _():
            def body(r, carry):
                pltpu.make_async_copy(ybuf.at[buf, pl.ds(0, 1)], z_hbm.at[pl.ds(0, 1)], ssem.at[buf]).wait()
                return carry
            lax.fori_loop(0, n, body, 0)

    used = nvalid_ref[i] > 0

    @pl.when(jnp.logical_and(i == 0, used))
    def _():
        gather(0, 0)

    @pl.when(used)
    def _():
        wait_gather(cur)

        @pl.when(jnp.logical_and(i + 1 < nblk, nvalid_ref[jnp.minimum(i + 1, nblk - 1)] > 0))
        def _():
            gather(i + 1, 1 - cur)

        x = xbuf[cur].astype(BF16)
        h1 = jnp.dot(x, w1_ref[...], preferred_element_type=F32)
        h3 = jnp.dot(x, w3_ref[...], preferred_element_type=F32)
        a = (h1 * _sigmoid(h1) * h3).astype(BF16)
        y = jnp.dot(a, w2_ref[...], preferred_element_type=F32)

        @pl.when(i >= 2)
        def _():
            wait_scatter(i - 2, cur)

        ybuf[cur] = y
        scatter(i, cur)

    first_unused = jnp.logical_and(jnp.logical_not(used),
                                   jnp.logical_and(i > 0, nvalid_ref[jnp.maximum(i - 1, 0)] > 0))
    last_used = jnp.logical_and(used, i == nblk - 1)

    def drain(last):
        wait_scatter(last, last % 2)

        @pl.when(last >= 1)
        def _():
            wait_scatter(last - 1, (last - 1) % 2)

    @pl.when(first_unused)
    def _():
        drain(i - 1)

    @pl.when(last_used)
    def _():
        drain(i)


def _moe_experts(h2, slot_buf, blk_e, n_valid, w1, w3, w2):
    n_tok = h2.shape[0]
    n_blocks = blk_e.shape[0]
    wspec = lambda shp: pl.BlockSpec((None,) + shp, lambda i, slot, blke, nvalid: (blke[i], 0, 0))
    grid_spec = pltpu.PrefetchScalarGridSpec(
        num_scalar_prefetch=3,
        grid=(n_blocks,),
        in_specs=[pl.BlockSpec(memory_space=pl.ANY),
                  wspec((D_MODEL, D_EXPERT)), wspec((D_MODEL, D_EXPERT)), wspec((D_EXPERT, D_MODEL))],
        out_specs=pl.BlockSpec(memory_space=pl.ANY),
        scratch_shapes=[pltpu.VMEM((2, MOE_BLOCK, D_MODEL), F32), pltpu.VMEM((2, MOE_BLOCK, D_MODEL), F32),
                        pltpu.SemaphoreType.DMA((2,)), pltpu.SemaphoreType.DMA((2,))])
    return pl.pallas_call(
        functools.partial(_moe_kernel, n_tok=n_tok),
        grid_spec=grid_spec,
        out_shape=jax.ShapeDtypeStruct((TOP_K * n_tok, D_MODEL), F32),
        compiler_params=_cparams(("arbitrary",)),
        name="moe_experts",
    )(slot_buf, blk_e, n_valid, h2, w1, w3, w2)


def _moe_combine_kernel(x_ref, z0_ref, z1_ref, gate_ref, mod_ref, o_ref):
    gate = gate_ref[...]
    y = gate[:, 0:1] * z0_ref[...] + gate[:, 1:2] * z1_ref[...]
    o_ref[...] = x_ref[...] + mod_ref[5:6, :] * y


def _moe_combine(x, z, gate, mods, mod_index):
    bsz, lt, _ = x.shape
    nblk = lt // TB
    gate3 = gate.reshape(bsz, lt, TOP_K)
    return pl.pallas_call(
        _moe_combine_kernel,
        grid=(bsz, nblk),
        in_specs=[pl.BlockSpec((None, TB, D_MODEL), lambda b, i: (b, i, 0)),
                  pl.BlockSpec((TB, D_MODEL), lambda b, i: (b * nblk + i, 0)),
                  pl.BlockSpec((TB, D_MODEL), lambda b, i: ((bsz + b) * nblk + i, 0)),
                  pl.BlockSpec((None, TB, TOP_K), lambda b, i: (b, i, 0)),
                  pl.BlockSpec((None, None, 6, D_MODEL), mod_index)],
        out_specs=pl.BlockSpec((None, TB, D_MODEL), lambda b, i: (b, i, 0)),
        out_shape=jax.ShapeDtypeStruct((bsz, lt, D_MODEL), F32),
        compiler_params=_cparams(("parallel", "parallel")),
        name="moe_combine",
    )(x, z, z, gate3, mods)


def _moe(x, h2, logits, mods, mod_index, w1, w3, w2):
    bsz, lt, _ = x.shape
    slot_buf, blk_e, n_valid, gate = _route(logits.reshape(bsz * lt, 128))
    z = _moe_experts(h2.reshape(bsz * lt, D_MODEL), slot_buf, blk_e, n_valid, w1, w3, w2)
    return _moe_combine(x, z, gate, mods, mod_index)


def _od_proj_kernel(x_ref, mod_ref, w_ref, gm_ref, qn_ref, kn_ref, q_ref, k_ref, v_ref, zh_ref):
    h = _rms_rows(x_ref[...]) * (1.0 + mod_ref[1:2, :]) + mod_ref[0:1, :]
    z = jnp.dot(h.astype(BF16), w_ref[...], preferred_element_type=F32)

    def head_norm(t, gain):
        ms = jnp.dot(t * t, gm_ref[...], precision=HI, preferred_element_type=F32)
        return t * lax.rsqrt(ms + EPS) * gain

    q_ref[...] = (head_norm(z[:, :NA_W], qn_ref[...]) * (NA_DH ** -0.5)).astype(BF16)
    k_ref[...] = head_norm(z[:, NA_W:2 * NA_W], kn_ref[...]).astype(BF16)
    v_ref[...] = z[:, 2 * NA_W:3 * NA_W].astype(BF16)
    zh_ref[...] = z[:, 3 * NA_W:]


def _od_proj(xcat, mods, w, gm, qn, kn):
    bsz, lt, _ = xcat.shape
    nblk = lt // TB
    tok = lambda n: pl.BlockSpec((None, TB, n), lambda b, i: (b, i, 0))
    const = lambda shp: pl.BlockSpec(shp, lambda b, i: tuple(0 for _ in shp))
    return pl.pallas_call(
        _od_proj_kernel,
        grid=(bsz, nblk),
        in_specs=[tok(D_MODEL), pl.BlockSpec((None, None, 6, D_MODEL), _mod_index),
                  const((D_MODEL, 3 * NA_W + 3 * HY_W)), const((NA_W, NA_W)), const((1, NA_W)), const((1, NA_W))],
        out_specs=[tok(NA_W), tok(NA_W), tok(NA_W), tok(3 * HY_W)],
        out_shape=[jax.ShapeDtypeStruct((bsz, lt, NA_W), BF16), jax.ShapeDtypeStruct((bsz, lt, NA_W), BF16),
                   jax.ShapeDtypeStruct((bsz, lt, NA_W), BF16), jax.ShapeDtypeStruct((bsz, lt, 3 * HY_W), F32)],
        compiler_params=_cparams(("parallel", "parallel")),
        name="od_proj",
    )(xcat, mods, w, gm, qn, kn)


def _na_kernel(q_ref, k_ref, v_ref, t2_ref, o_ref):
    r = pl.program_id(1)
    n_rows = pl.num_programs(1)
    r0 = jnp.clip(r - WIN_R // 2, 0, n_rows - WIN_R)
    off = r0 - r + WIN_R - 1
    base = pl.multiple_of(TB + r0 * GRID_W, GRID_W)
    nloc = WIN_R * GRID_W
    q = q_ref[...]
    kw, vw = k_ref[pl.ds(base, nloc), :], v_ref[pl.ds(base, nloc), :]
    kc, vc = k_ref[0:TB, :], v_ref[0:TB, :]
    lane_hi = lax.broadcasted_iota(jnp.int32, (GRID_W, 128), 1) // NA_DH
    nt = (((1,), (1,)), ((), ()))
    outs = []
    for hp in range(NA_HEADS // 2):
        sl = slice(128 * hp, 128 * (hp + 1))
        q2, k2, v2, kc2, vc2 = q[:, sl], kw[:, sl], vw[:, sl], kc[:, sl], vc[:, sl]
        acc = jnp.zeros((GRID_W, 128), F32)
        for hh in range(2):
            h = 2 * hp + hh
            sel = lane_hi == hh
            qm = jnp.where(sel, q2, jnp.zeros_like(q2))
            bias = jnp.concatenate([t2_ref[h, off + 2 * m] for m in range(WIN_R // 2)], axis=1)
            s_loc = lax.dot_general(qm, k2, nt, preferred_element_type=F32) + bias
            s_ctx = lax.dot_general(qm, kc2, nt, preferred_element_type=F32)
            m = jnp.maximum(jnp.max(s_loc, axis=1, keepdims=True), jnp.max(s_ctx, axis=1, keepdims=True))
            p_loc, p_ctx = jnp.exp(s_loc - m), jnp.exp(s_ctx - m)
            den = jnp.sum(p_loc, axis=1, keepdims=True) + jnp.sum(p_ctx, axis=1, keepdims=True)
            o = (jnp.dot(p_loc.astype(BF16), v2, preferred_element_type=F32)
                 + jnp.dot(p_ctx.astype(BF16), vc2, preferred_element_type=F32))
            acc = jnp.where(sel, o / den, acc)
        outs.append(acc)
    o_ref[...] = jnp.concatenate(outs, axis=1)


def _na_bias_table(rpb):
    qc = np.arange(GRID_W)[:, None]
    kc = np.arange(GRID_W)[None, :]
    q_start = np.clip(qc - WIN_C // 2, 0, GRID_W - WIN_C)
    valid = (kc >= q_start) & (kc < q_start + WIN_C)
    col_idx = np.clip(kc - qc + WIN_C - 1, 0, 2 * WIN_C - 2)
    t = jnp.where(valid[None, None], rpb.astype(F32)[:, :, col_idx], NEG_INF)
    return jnp.concatenate([t[:, :-1], t[:, 1:]], axis=-1)


def _na(q, k, v, t2):
    bsz, lt, _ = q.shape
    n_rows = (lt - TB) // GRID_W
    qoff = TB // GRID_W
    return pl.pallas_call(
        _na_kernel,
        grid=(bsz, n_rows),
        in_specs=[pl.BlockSpec((None, GRID_W, NA_W), lambda b, r: (b, r + qoff, 0)),
                  pl.BlockSpec((None, lt, NA_W), lambda b, r: (b, 0, 0)),
                  pl.BlockSpec((None, lt, NA_W), lambda b, r: (b, 0, 0)),
                  pl.BlockSpec(t2.shape, lambda b, r: (0, 0, 0, 0))],
        out_specs=pl.BlockSpec((None, GRID_W, NA_W), lambda b, r: (b, r, 0)),
        out_shape=jax.ShapeDtypeStruct((bsz, lt - TB, NA_W), F32),
        compiler_params=_cparams(("parallel", "arbitrary")),
        name="na_attn",
    )(q, k, v, t2)


def _hy_pre_kernel(z_ref, zp_ref, zn_ref, cw_ref, cb_ref, x0_ref, u_ref, ut_ref):
    i = pl.program_id(1)
    n = pl.num_programs(1)
    z = z_ref[...]
    tb = z.shape[0]
    prev_row = jnp.where(i > 0, zp_ref[7:8, :], 0.0)
    next_row = jnp.where(i < n - 1, zn_ref[0:1, :], 0.0)
    rowid = lax.broadcasted_iota(jnp.int32, z.shape, 0)
    zm = jnp.where(rowid == 0, prev_row, pltpu.roll(z, 1, 0))
    zp = jnp.where(rowid == tb - 1, next_row, pltpu.roll(z, tb - 1, 0))
    zc = cb_ref[...] + cw_ref[0:1, :] * zm
    zc = zc + cw_ref[1:2, :] * z
    zc = zc + cw_ref[2:3, :] * zp
    x0_ref[...] = zc[:, :HY_W]
    u = zc[:, HY_W:2 * HY_W] * zc[:, 2 * HY_W:]
    u_ref[...] = u
    for j in range(tb // FFT_N1):
        ut_ref[j] = u[j * FFT_N1:(j + 1) * FFT_N1, :].T.astype(BF16)


def _hy_pre(zh, cw, cb):
    bsz, lt, _ = zh.shape
    l = lt - TB
    nblk = l // TB
    h8 = TB // 8
    return pl.pallas_call(
        _hy_pre_kernel,
        grid=(bsz, nblk),
        in_specs=[pl.BlockSpec((None, TB, 3 * HY_W), lambda b, i: (b, i + 1, 0)),
                  pl.BlockSpec((None, 8, 3 * HY_W), lambda b, i: (b, (i + 1) * h8 - 1, 0)),
                  pl.BlockSpec((None, 8, 3 * HY_W), lambda b, i: (b, jnp.minimum((i + 2) * h8, lt // 8 - 1), 0)),
                  pl.BlockSpec((HY_SHORT, 3 * HY_W), lambda b, i: (0, 0)),
                  pl.BlockSpec((1, 3 * HY_W), lambda b, i: (0, 0))],
        out_specs=[pl.BlockSpec((None, TB, HY_W), lambda b, i: (b, i, 0)),
                   pl.BlockSpec((None, TB, HY_W), lambda b, i: (b, i, 0)),
                   pl.BlockSpec((None, TB // FFT_N1, HY_W, FFT_N1), lambda b, i: (b, i, 0, 0))],
        out_shape=[jax.ShapeDtypeStruct((bsz, l, HY_W), F32), jax.ShapeDtypeStruct((bsz, l, HY_W), F32),
                   jax.ShapeDtypeStruct((bsz, l // FFT_N1, HY_W, FFT_N1), BF16)],
        compiler_params=_cparams(("parallel", "parallel")),
        name="hy_pre",
    )(zh, zh, zh, cw, cb)


def _fft_consts(n1_in):
    n = FFT_N1
    idx = np.arange(n)
    ang1 = 2.0 * np.pi * np.outer(idx, idx) / n
    c, s = np.cos(ang1), np.sin(ang1)
    angt = 2.0 * np.pi * np.outer(idx, idx) / (n * n)
    tw = np.concatenate([np.cos(angt), -np.sin(angt)], axis=1)
    f3 = np.block([[c, -s], [s, c]])
    f3i = np.block([[c, s], [-s, c]])
    ch, sh = c[:, :n1_in], s[:, :n1_in]
    f1_pair = np.block([[ch, sh], [-sh, ch]])
    f1_real = np.concatenate([c, -s], axis=0)
    f1i = np.block([[ch.T, -sh.T], [sh.T, ch.T]]) / (n * n)
    return tw, f3, f3i, f1_pair, f1_real, f1i


def _fft_forward(a, tw_re, tw_im, lhs_scr, ncg):
    for cix in range(ncg):
        cs = slice(cix * FFT_N1, (cix + 1) * FFT_N1)
        are, aim = a[:FFT_N1, cs], a[FFT_N1:, cs]
        lhs_scr[cs, :FFT_N1] = are * tw_re - aim * tw_im
        lhs_scr[cs, FFT_N1:] = are * tw_im + aim * tw_re


def _hy_filt_kernel(k_ref, f1_ref, tw_ref, f3_ref, o_ref, lhs_scr):
    a = jnp.dot(f1_ref[...], k_ref[...], precision=HI, preferred_element_type=F32)
    _fft_forward(a, tw_ref[:, :FFT_N1], tw_ref[:, FFT_N1:], lhs_scr, FFT_CG)
    o_ref[...] = jnp.dot(lhs_scr[...], f3_ref[...], precision=HI, preferred_element_type=F32)


def _hy_fft_kernel(u_ref, kf_ref, f1_ref, tw_ref, f3_ref, f3i_ref, f1i_ref, y_ref, lhs_scr, a2_scr):
    cgl = FFT_CG * FFT_N1
    x = u_ref[...].reshape(2 * u_ref.shape[1], cgl)
    a = jnp.dot(f1_ref[...], x, preferred_element_type=F32)
    tw_re, tw_im = tw_ref[:, :FFT_N1], tw_ref[:, FFT_N1:]
    _fft_forward(a, tw_re, tw_im, lhs_scr, FFT_CG)
    y = jnp.dot(lhs_scr[...].astype(BF16), f3_ref[...], preferred_element_type=F32)
    yre, yim = y[:, :FFT_N1], y[:, FFT_N1:]
    kre, kim = kf_ref[:, :FFT_N1], kf_ref[:, FFT_N1:]
    z = jnp.concatenate([yre * kre - yim * kim, yre * kim + yim * kre], axis=1).astype(BF16)
    bp = jnp.dot(z, f3i_ref[...], preferred_element_type=F32)
    for cix in range(FFT_CG):
        cs = slice(cix * FFT_N1, (cix + 1) * FFT_N1)
        bre, bim = bp[cs, :FFT_N1], bp[cs, FFT_N1:]
        a2_scr[:FFT_N1, cs] = bre * tw_re + bim * tw_im
        a2_scr[FFT_N1:, cs] = bim * tw_re - bre * tw_im
    out = jnp.dot(f1i_ref[...], a2_scr[...].astype(BF16), preferred_element_type=F32)
    y_ref[...] = out.reshape(2, u_ref.shape[1], cgl)


def _hy_conv(ut, kfilt):
    bsz, n1h, nch, _ = ut.shape
    assert 2 * n1h == FFT_N1 and bsz % 2 == 0
    cgl = FFT_CG * FFT_N1
    ncol = nch * FFT_N1
    tw, f3, f3i, f1_pair, f1_real, f1i = _fft_consts(n1h)
    kt = kfilt.reshape(FFT_N1, FFT_N1, nch).transpose(0, 2, 1).reshape(FFT_N1, ncol)
    const2 = lambda shp: pl.BlockSpec(shp, lambda *a: (0, 0))
    kf = pl.pallas_call(
        _hy_filt_kernel,
        grid=(nch // FFT_CG,),
        in_specs=[pl.BlockSpec((FFT_N1, cgl), lambda j: (0, j)), const2((2 * FFT_N1, FFT_N1)),
                  const2((FFT_N1, 2 * FFT_N1)), const2((2 * FFT_N1, 2 * FFT_N1))],
        out_specs=pl.BlockSpec((cgl, 2 * FFT_N1), lambda j: (j, 0)),
        out_shape=jax.ShapeDtypeStruct((ncol, 2 * FFT_N1), F32),
        scratch_shapes=[pltpu.VMEM((cgl, 2 * FFT_N1), F32)],
        compiler_params=_cparams(("parallel",)),
        name="hy_filter_dft",
    )(kt, jnp.asarray(f1_real, F32), jnp.asarray(tw, F32), jnp.asarray(f3, F32))
    u2 = ut.reshape(bsz, n1h, ncol)
    y = pl.pallas_call(
        _hy_fft_kernel,
        grid=(bsz // 2, nch // FFT_CG),
        in_specs=[pl.BlockSpec((2, n1h, cgl), lambda p, j: (p, 0, j)),
                  pl.BlockSpec((cgl, 2 * FFT_N1), lambda p, j: (j, 0)),
                  const2((2 * FFT_N1, FFT_N1)), const2((FFT_N1, 2 * FFT_N1)),
                  const2((2 * FFT_N1, 2 * FFT_N1)), const2((2 * FFT_N1, 2 * FFT_N1)), const2((FFT_N1, 2 * FFT_N1))],
        out_specs=pl.BlockSpec((2, n1h, cgl), lambda p, j: (p, 0, j)),
        out_shape=jax.ShapeDtypeStruct((bsz, n1h, ncol), F32),
        scratch_shapes=[pltpu.VMEM((cgl, 2 * FFT_N1), F32), pltpu.VMEM((2 * FFT_N1, cgl), F32)],
        compiler_params=_cparams(("parallel", "parallel")),
        name="hy_fft_conv",
    )(u2, kf, jnp.asarray(f1_pair, BF16), jnp.asarray(tw, F32), jnp.asarray(f3, BF16),
      jnp.asarray(f3i, BF16), jnp.asarray(f1i, BF16))
    return y.reshape(bsz, n1h, nch, FFT_N1)


def _hy_filter(seqlen, fw1, fb1, fw2, fb2, fw3, fb3, freq, fw4):
    t = jnp.linspace(0.0, 1.0, seqlen, dtype=F32)[:, None]
    bands = (HY_EMB - 1) // 2
    w = 2.0 * math.pi * jnp.arange(seqlen, dtype=F32)[:, None] / seqlen
    f = jnp.linspace(1e-4, bands - 1, bands, dtype=F32)[None, :]
    feat = jnp.concatenate([t, jnp.cos(f * w), -jnp.sin(f * w)], axis=-1)
    h = _mm(feat, fw1, fb1, freq)
    h = _mm(h, fw2, fb2, freq)
    h = _mm(h, fw3, fb3, freq)
    h = _mm(h, fw4).reshape(seqlen, 2, HY_W)
    deltas = jnp.abs(jnp.linspace(math.log(HY_DECAY_TARGET) / HY_DECAY_LONG_PCT,
                                  math.log(HY_DECAY_TARGET) / HY_DECAY_SHORT_PCT, HY_W, dtype=F32))
    h = h * jnp.exp(-t * deltas)[:, None, :]
    return jnp.concatenate([h[:, 0], jnp.zeros((1, HY_W), F32), h[:0:-1, 1]], axis=0)


def _od_out_kernel(na_ref, yt_ref, x0_ref, u_ref, x_ref, mod_ref, hb_ref, wout_ref, wr_ref, br_ref,
                   x_out, h2_out, lg_out):
    y = jnp.concatenate([yt_ref[j].T for j in range(yt_ref.shape[0])], axis=0)
    hy = x0_ref[...] * (y + u_ref[...] * hb_ref[...])
    cat = jnp.concatenate([na_ref[...], hy], axis=1).astype(BF16)
    mix = jnp.dot(cat, wout_ref[...], preferred_element_type=F32)
    _mixer_tail(x_ref[...], mix, mod_ref, wr_ref, br_ref, x_out, h2_out, lg_out)


def _latent_mod_index(b, i):
    return (b, 1, 0, 0)


def _od_out(na, yt, x0, u, xcat, mods, hb, wout, wr, br):
    bsz, l, _ = na.shape
    nblk = l // TB
    tok = lambda n: pl.BlockSpec((None, TB, n), lambda b, i: (b, i, 0))
    const = lambda shp: pl.BlockSpec(shp, lambda b, i: tuple(0 for _ in shp))
    return pl.pallas_call(
        _od_out_kernel,
        grid=(bsz, nblk),
        in_specs=[tok(NA_W), pl.BlockSpec((None, TB // FFT_N1, HY_W, FFT_N1), lambda b, i: (b, i, 0, 0)),
                  tok(HY_W), tok(HY_W),
                  pl.BlockSpec((None, TB, D_MODEL), lambda b, i: (b, i + 1, 0)),
                  pl.BlockSpec((None, None, 6, D_MODEL), _latent_mod_index),
                  const((1, HY_W)), const((D_MODEL, D_MODEL)), const((D_MODEL, 128)), const((1, 128))],
        out_specs=[tok(D_MODEL), tok(D_MODEL), tok(128)],
        out_shape=[jax.ShapeDtypeStruct((bsz, l, D_MODEL), F32), jax.ShapeDtypeStruct((bsz, l, D_MODEL), F32),
                   jax.ShapeDtypeStruct((bsz, l, 128), F32)],
        compiler_params=_cparams(("parallel", "parallel")),
        name="od_out",
    )(na, yt, x0, u, xcat, mods, hb, wout, wr, br)


def _mods(c, c_ctx, ada_w, ada_b):
    bsz = c.shape[0]
    cc = jnp.concatenate([c, c_ctx[None]], axis=0)
    m = _mm(cc * _sigmoid(cc), ada_w, ada_b)
    mod_l = m[:bsz].reshape(bsz, 1, 6, D_MODEL)
    mod_c = jnp.broadcast_to(m[bsz].reshape(1, 1, 6, D_MODEL), (bsz, 1, 6, D_MODEL))
    return jnp.concatenate([mod_c, mod_l], axis=1)


def _rope_tables(seqlen):
    pos = jnp.arange(seqlen)
    half = GLA_DK // 4
    freqs = ROPE_BASE ** (-jnp.arange(half, dtype=F32) / half)
    ar = (pos // GRID_W).astype(F32)[:, None] * freqs
    ac = (pos % GRID_W).astype(F32)[:, None] * freqs
    cos = jnp.concatenate([jnp.cos(ar), jnp.cos(ar), jnp.cos(ac), jnp.cos(ac)], axis=1)
    sin = jnp.concatenate([-jnp.sin(ar), jnp.sin(ar), -jnp.sin(ac), jnp.sin(ac)], axis=1)
    cos = jnp.concatenate([jnp.ones((TB, GLA_DK), F32), cos], axis=0)
    sin = jnp.concatenate([jnp.zeros((TB, GLA_DK), F32), sin], axis=0)
    return jnp.tile(cos, (1, GLA_HEADS)), jnp.tile(sin, (1, GLA_HEADS))


def _router_weights(wg, bg, we, be):
    pad = 128 - N_GROUPS - N_EXPERTS
    wr = jnp.concatenate([wg, we, jnp.zeros((D_MODEL, pad), F32)], axis=1)
    br = jnp.concatenate([bg, be, jnp.zeros((pad,), F32)]).reshape(1, 128)
    return wr, br


def kernel(x, c, ctx, c_ctx, ada_w, ada_b, moe_wg, moe_bg, moe_we, moe_be, moe_w1, moe_w3, moe_w2, ev_w_in, ev_w_out, gla_wa2, gla_ba, gla_norm, s5_lam_re, s5_lam_im, s5_log_dt, s5_b_re, s5_b_im, s5_c_re, s5_c_im, s5_d, s5_w_glu, od_w_in, od_w_out, na_q_norm, na_k_norm, na_rpb, hy_conv_w, hy_conv_b, hy_fw1, hy_fb1, hy_fw2, hy_fb2, hy_fw3, hy_fb3, hy_freq, hy_fw4, hy_bias):
    bsz, seqlen, _ = x.shape
    assert ctx.shape[1] == TB and seqlen % TB == 0
    xcat = jnp.concatenate([ctx, x], axis=1)

    mods = _mods(c, c_ctx, ada_w[0], ada_b[0])
    w_in = ev_w_in[0]
    n_a = 2 * GLA_RANK
    a0 = 2 * GLA_QK + 2 * GLA_V
    w_ev = jnp.concatenate([w_in[:, :a0], w_in[:, a0 + n_a:], w_in[:, a0:a0 + n_a],
                            jnp.zeros((D_MODEL, 128 - n_a), F32)], axis=1).astype(BF16)
    wa = jnp.zeros((128, 2 * GLA_QK), F32)
    for d in range(2):
        wa = wa.at[d * GLA_RANK:(d + 1) * GLA_RANK, d * GLA_QK:(d + 1) * GLA_QK].set(gla_wa2[0, d])
    cos, sin = _rope_tables(seqlen)
    q, k, v, g, u, u_sw, la = _ev_proj(xcat, mods, w_ev, wa, gla_ba[0].reshape(1, 2 * GLA_QK), cos, sin)
    o_f, o_b = _gla(q, k, v, la)
    s5p = [t[0].astype(F32) for t in (s5_lam_re, s5_lam_im, s5_log_dt, s5_b_re, s5_b_im, s5_c_re, s5_c_im)]
    y_f = _s5_scan(u, _s5_mats(*[t[0] for t in s5p], rev=False), rev=False)
    y_b = _s5_scan(u_sw, _s5_mats(*[t[1] for t in s5p], rev=True), rev=True)
    wr, br = _router_weights(moe_wg[0], moe_bg[0], moe_we[0], moe_be[0])
    x1, h2, lg = _ev_out(o_f, o_b, g, y_f, y_b, u, xcat, mods,
                         jnp.tile(gla_norm[0], GLA_HEADS).reshape(1, GLA_V), s5_d[0].reshape(1, S5_W),
                         s5_w_glu[0].astype(BF16), ev_w_out[0].astype(BF16), wr, br)
    xcat = _moe(x1, h2, lg, mods, _mod_index,
                moe_w1[0].astype(BF16), moe_w3[0].astype(BF16), moe_w2[0].astype(BF16))

    mods = _mods(c, c_ctx, ada_w[1], ada_b[1])
    hd = np.arange(NA_W) // NA_DH
    gm = jnp.asarray((hd[:, None] == hd[None, :]).astype(np.float32) / NA_DH)
    qh, kh, vh, zh = _od_proj(xcat, mods, od_w_in[0].astype(BF16), gm,
                              jnp.tile(na_q_norm[0], NA_HEADS).reshape(1, NA_W),
                              jnp.tile(na_k_norm[0], NA_HEADS).reshape(1, NA_W))
    na = _na(qh, kh, vh, _na_bias_table(na_rpb[0]))
    x0, uh, ut = _hy_pre(zh, hy_conv_w[0], hy_conv_b[0].reshape(1, 3 * HY_W))
    kfilt = _hy_filter(seqlen, hy_fw1[0], hy_fb1[0], hy_fw2[0], hy_fb2[0], hy_fw3[0], hy_fb3[0],
                       hy_freq[0], hy_fw4[0])
    yt = _hy_conv(ut, kfilt)
    wr, br = _router_weights(moe_wg[1], moe_bg[1], moe_we[1], moe_be[1])
    xl, h2, lg = _od_out(na, yt, x0, uh, xcat, mods, hy_bias[0].reshape(1, HY_W),
                         od_w_out[0].astype(BF16), wr, br)
    return _moe(xl, h2, lg, mods, _latent_mod_index,
                moe_w1[1].astype(BF16), moe_w3[1].astype(BF16), moe_w2[1].astype(BF16))
```

```python
import functools
import math

import numpy as np
import jax
import jax.numpy as jnp
from jax import lax
from jax.experimental import pallas as pl
from jax.experimental.pallas import tpu as pltpu

F32, BF16 = jnp.float32, jnp.bfloat16
HI = lax.Precision.HIGHEST

D_MODEL = 1024
GRID_W = 64
EPS = 1e-6
ROPE_BASE = 10000.0
NEG_INF = -1e30
GLA_HEADS, GLA_DK, GLA_DV = 4, 64, 128
GLA_QK, GLA_V = GLA_HEADS * GLA_DK, GLA_HEADS * GLA_DV
GLA_RANK = 16
GLA_TAU = 16.0
GLA_CHUNK = 64
GLA_LOG_ALPHA_MIN = -1.0
S5_W, S5_H, S5_P = 512, 16, 64
S5_G = S5_W // S5_H
S5_CHUNK = 8
S5_TBLK = 1408
NA_HEADS, NA_DH = 8, 64
NA_W = NA_HEADS * NA_DH
WIN_R, WIN_C = 8, 16
NA_HG = 4
HY_W = 512
HY_SHORT = 3
HY_EMB = 33
HY_DECAY_TARGET = 1e-2
HY_DECAY_SHORT_PCT = 0.3
HY_DECAY_LONG_PCT = 1.5
N_GROUPS, EXP_PER_GROUP = 4, 8
N_EXPERTS = N_GROUPS * EXP_PER_GROUP
D_EXPERT = 512
TOP_K = 2
MOE_BLOCK = 256
MOE_ISSUE_UNROLL = 8

TB = 256
PROJ_TILE = 768
FFT_N1 = 128
FFT_CG = 16
V7X_VMEM_LIMIT = 52 * 1024 * 1024


def _cparams(sem):
    return pltpu.CompilerParams(dimension_semantics=sem, vmem_limit_bytes=V7X_VMEM_LIMIT)


def _sigmoid(x):
    return 1.0 / (1.0 + jnp.exp(-x))


ROW_TILE = D_MODEL // 128


def _to_token_tiles(ref, val):
    n = val.shape[0]
    for j in range(ROW_TILE):
        ref[pl.ds(j, n, stride=ROW_TILE), :] = val[:, j * 128:(j + 1) * 128]


def _from_token_tiles(ref, n):
    return jnp.concatenate([ref[pl.ds(j, n, stride=ROW_TILE), :] for j in range(ROW_TILE)], axis=1)


def _rms_rows(x):
    return x * lax.rsqrt(jnp.mean(x * x, axis=-1, keepdims=True) + EPS)


def _mm_kernel(a_ref, w_ref, b_ref, f_ref, o_ref, *, use_sin):
    z = jnp.dot(a_ref[...], w_ref[...], precision=HI, preferred_element_type=F32) + b_ref[...]
    if use_sin:
        z = jnp.sin(f_ref[...] * z)
    o_ref[...] = z


def _mm(a, w, bias=None, freq=None):
    m, k = a.shape
    n = w.shape[1]
    mp, kp = -(-m // 8) * 8, -(-k // 128) * 128
    tm = min(mp, 1024)
    mp = -(-mp // tm) * tm
    tn = n if n <= 1024 else 1024
    assert n % tn == 0
    a = jnp.pad(a.astype(F32), ((0, mp - m), (0, kp - k)))
    w = jnp.pad(w.astype(F32), ((0, kp - k), (0, 0)))
    bias = jnp.zeros((n,), F32) if bias is None else bias.astype(F32)
    use_sin = freq is not None
    freq = jnp.ones((n,), F32) if freq is None else freq.astype(F32)
    out = pl.pallas_call(
        functools.partial(_mm_kernel, use_sin=use_sin),
        grid=(mp // tm, n // tn),
        in_specs=[pl.BlockSpec((tm, kp), lambda i, j: (i, 0)),
                  pl.BlockSpec((kp, tn), lambda i, j: (0, j)),
                  pl.BlockSpec((1, tn), lambda i, j: (0, j)),
                  pl.BlockSpec((1, tn), lambda i, j: (0, j))],
        out_specs=pl.BlockSpec((tm, tn), lambda i, j: (i, j)),
        out_shape=jax.ShapeDtypeStruct((mp, n), F32),
        compiler_params=_cparams(("parallel", "parallel")),
        name="small_dense",
    )(a, w, bias.reshape(1, n), freq.reshape(1, n))
    return out[:m]


def _mod_index(b, i):
    return (b, jnp.minimum(i, 1), 0, 0)


def _swapped_index(nblk, b, i):
    return (b, jnp.where(i == 0, nblk - 1, i - 1), 0)


EV_NQ, EV_NK, EV_NV, EV_NG, EV_NU = 0, 256, 512, 1024, 1536
EV_NA = 2048
EV_NTOT = 2176


def _stream_tile(ctx_ref, x_ref):
    return jnp.where(pl.program_id(1) == 0, ctx_ref[...], x_ref[...])


def _stream_specs():
    return [pl.BlockSpec((None, TB, D_MODEL), lambda b, i: (b, 0, 0)),
            pl.BlockSpec((None, TB, D_MODEL), lambda b, i: (b, jnp.maximum(i - 1, 0), 0))]


def _ev_proj_kernel(ctx_ref, x_ref, mod_ref, w_ref, wa_ref, ba_ref, cos_ref, sin_ref,
                    q_ref, k_ref, v_ref, g_ref, u_ref, usw_ref, la_ref):
    x = _stream_tile(ctx_ref, x_ref)
    h = _rms_rows(x) * (1.0 + mod_ref[1:2, :]) + mod_ref[0:1, :]
    z = jnp.dot(h.astype(BF16), w_ref[...], preferred_element_type=F32)
    lane = lax.broadcasted_iota(jnp.int32, (x.shape[0], GLA_QK), 1)
    first = (lane % 32) < 16
    cos, sin = cos_ref[...], sin_ref[...]

    def rot(t):
        partner = jnp.where(first, pltpu.roll(t, GLA_QK - 16, 1), pltpu.roll(t, 16, 1))
        return t * cos + partner * sin

    q_ref[...] = rot(z[:, EV_NQ:EV_NQ + GLA_QK]) * (GLA_DK ** -0.5)
    k_ref[...] = rot(z[:, EV_NK:EV_NK + GLA_QK])
    v_ref[...] = z[:, EV_NV:EV_NV + GLA_V].astype(BF16)
    g_ref[...] = z[:, EV_NG:EV_NG + GLA_V]
    u_ref[...] = z[:, EV_NU:EV_NU + S5_W]
    usw_ref[...] = z[:, EV_NU:EV_NU + S5_W]
    a = z[:, EV_NA:EV_NA + 128]
    pre = jnp.dot(a, wa_ref[...], precision=HI, preferred_element_type=F32) + ba_ref[...]
    ls = jnp.minimum(pre, 0.0) - jnp.log1p(jnp.exp(-jnp.abs(pre)))
    la_ref[...] = jnp.maximum(ls / GLA_TAU, GLA_LOG_ALPHA_MIN)


def _ev_proj(ctx, x, mods, w, wa, ba, cos, sin):
    bsz = x.shape[0]
    lt = ctx.shape[1] + x.shape[1]
    nblk = lt // TB
    tok = lambda n: pl.BlockSpec((None, TB, n), lambda b, i: (b, i, 0))
    const = lambda shp: pl.BlockSpec(shp, lambda b, i: tuple(0 for _ in shp))
    return pl.pallas_call(
        _ev_proj_kernel,
        grid=(bsz, nblk),
        in_specs=_stream_specs() + [
                  pl.BlockSpec((None, None, 6, D_MODEL), _mod_index),
                  const((D_MODEL, EV_NTOT)), const((128, 2 * GLA_QK)), const((1, 2 * GLA_QK)),
                  pl.BlockSpec((TB, GLA_QK), lambda b, i: (i, 0)),
                  pl.BlockSpec((TB, GLA_QK), lambda b, i: (i, 0))],
        out_specs=[tok(GLA_QK), tok(GLA_QK), tok(GLA_V), tok(GLA_V), tok(S5_W),
                   pl.BlockSpec((None, TB, S5_W), functools.partial(_swapped_index, nblk)), tok(2 * GLA_QK)],
        out_shape=[jax.ShapeDtypeStruct((bsz, lt, GLA_QK), F32),
                   jax.ShapeDtypeStruct((bsz, lt, GLA_QK), F32),
                   jax.ShapeDtypeStruct((bsz, lt, GLA_V), BF16),
                   jax.ShapeDtypeStruct((bsz, lt, GLA_V), F32),
                   jax.ShapeDtypeStruct((bsz, lt, S5_W), F32),
                   jax.ShapeDtypeStruct((bsz, lt, S5_W), F32),
                   jax.ShapeDtypeStruct((bsz, lt, 2 * GLA_QK), F32)],
        compiler_params=_cparams(("parallel", "parallel")),
        name="ev_proj",
    )(ctx, x, mods, w, wa, ba, cos, sin)


def _gla_kernel(qf_ref, kf_ref, vf_ref, laf_ref, qb_ref, kb_ref, vb_ref, lab_ref,
                of_ref, ob_ref, s_scr):
    i = pl.program_id(1)

    @pl.when(i == 0)
    def _():
        s_scr[...] = jnp.zeros_like(s_scr)

    c = GLA_CHUNK
    row = lax.broadcasted_iota(jnp.int32, (c, c), 0)
    col = lax.broadcasted_iota(jnp.int32, (c, c), 1)
    lane_head = lax.broadcasted_iota(jnp.int32, (c, GLA_QK), 1) // GLA_DK
    bd_mask = (lax.broadcasted_iota(jnp.int32, (GLA_QK, GLA_V), 0) // GLA_DK
               == lax.broadcasted_iota(jnp.int32, (GLA_QK, GLA_V), 1) // GLA_DV)
    ones = jnp.ones((c, 128), F32)
    nchunk = qf_ref.shape[0] // c

    def one_chunk(refs, o_ref, d, r0):
        q_ref, k_ref, v_ref, la_ref = refs
        fwd = d == 0
        sl = pl.ds(r0, c)
        qc, kc, vc, lac = q_ref[sl, :], k_ref[sl, :], v_ref[sl, :], la_ref[sl, :]
        tri = ((row >= col) if fwd else (row <= col)).astype(F32)
        b = jnp.dot(tri, lac, precision=HI, preferred_element_type=F32)
        b_last = b[c - 1:c, :] if fwd else b[0:1, :]
        qe = (qc * jnp.exp(b)).astype(BF16)
        ke = (kc * jnp.exp(-b)).astype(BF16)
        kd = (kc * jnp.exp(b_last - b)).astype(BF16)
        dec = jnp.exp(lax.dot_general(lac, ones, (((0,), (0,)), ((), ())),
                                      precision=HI, preferred_element_type=F32))
        s = s_scr[d]
        o = jnp.dot(qe, s.astype(BF16), preferred_element_type=F32)
        att_mask = (row >= col) if fwd else (row < col)
        intra = []
        for h in range(GLA_HEADS):
            qh = jnp.where(lane_head == h, qe, jnp.zeros_like(qe))
            att = lax.dot_general(qh, ke, (((1,), (1,)), ((), ())), preferred_element_type=F32)
            att = jnp.where(att_mask, att, 0.0).astype(BF16)
            intra.append(jnp.dot(att, vc[:, h * GLA_DV:(h + 1) * GLA_DV], preferred_element_type=F32))
        o_ref[sl, :] = o + jnp.concatenate(intra, axis=1)
        upd = lax.dot_general(kd, vc, (((0,), (0,)), ((), ())), preferred_element_type=F32)
        s_dec = jnp.concatenate([s[:, h * GLA_DV:(h + 1) * GLA_DV] * dec for h in range(GLA_HEADS)], axis=1)
        s_scr[d] = s_dec + jnp.where(bd_mask, upd, 0.0)

    def body(j, carry):
        one_chunk((qf_ref, kf_ref, vf_ref, laf_ref), of_ref, 0, pl.multiple_of(j * c, c))
        one_chunk((qb_ref, kb_ref, vb_ref, lab_ref), ob_ref, 1, pl.multiple_of((nchunk - 1 - j) * c, c))
        return carry

    lax.fori_loop(0, nchunk, body, 0, unroll=True)


def _gla(q, k, v, la):
    bsz, lt, _ = q.shape
    nblk = lt // TB
    fwd_map = lambda b, i: (b, i, 0)
    bwd_blk = lambda i: jnp.where(i == 0, 0, nblk - i)
    bwd_map = lambda b, i: (b, bwd_blk(i), 0)
    bwd_map_la = lambda b, i: (b, bwd_blk(i), 1)
    spec = lambda n, m: pl.BlockSpec((None, TB, n), m)
    return pl.pallas_call(
        _gla_kernel,
        grid=(bsz, nblk),
        in_specs=[spec(GLA_QK, fwd_map), spec(GLA_QK, fwd_map), spec(GLA_V, fwd_map), spec(GLA_QK, fwd_map),
                  spec(GLA_QK, bwd_map), spec(GLA_QK, bwd_map), spec(GLA_V, bwd_map), spec(GLA_QK, bwd_map_la)],
        out_specs=[spec(GLA_V, fwd_map), spec(GLA_V, bwd_map)],
        out_shape=[jax.ShapeDtypeStruct((bsz, lt, GLA_V), F32), jax.ShapeDtypeStruct((bsz, lt, GLA_V), F32)],
        scratch_shapes=[pltpu.VMEM((2, GLA_QK, GLA_V), F32)],
        compiler_params=_cparams(("parallel", "arbitrary")),
        name="gla_scan",
    )(q, k, v, la, q, k, v, la)


def _s5_kernel(u_ref, wm_ref, tm_ref, cm_ref, ar_ref, ai_ref, y_ref, w_scr, hp_scr, h_scr, *, rev):
    @pl.when(pl.program_id(1) == 0)
    def _():
        h_scr[...] = jnp.zeros_like(h_scr)

    bsz, ntok, _ = u_ref.shape
    nc = ntok // S5_CHUNK
    half = 8 * S5_P
    x = jnp.concatenate(
        [jnp.concatenate([u_ref[b, pl.ds(s, nc, stride=S5_CHUNK), :] for s in range(S5_CHUNK)], axis=1)
         for b in range(bsz)], axis=0).astype(BF16)
    w_scr[...] = jnp.dot(x, wm_ref[...], preferred_element_type=F32)
    ar, ai = ar_ref[...], ai_ref[...]

    def body(j, hs):
        c = (nc - 1 - j) if rev else j
        out = []
        for b in range(bsz):
            re, im = hs[b]
            r = b * nc + c
            hp_scr[pl.ds(r, 1), :] = jnp.concatenate([re, im], axis=1)
            w = w_scr[pl.ds(r, 1), :]
            out.append((ar * re - ai * im + w[:, :half], ar * im + ai * re + w[:, half:]))
        return tuple(out)

    hs = lax.fori_loop(0, nc, body, tuple((h_scr[b:b + 1, :half], h_scr[b:b + 1, half:]) for b in range(bsz)))
    for b in range(bsz):
        h_scr[b:b + 1, :] = jnp.concatenate(hs[b], axis=1)
    y = (jnp.dot(x, tm_ref[...], preferred_element_type=F32)
         + jnp.dot(hp_scr[...].astype(BF16), cm_ref[...], preferred_element_type=F32))
    for b in range(bsz):
        for s in range(S5_CHUNK):
            y_ref[b, pl.ds(s, nc, stride=S5_CHUNK), :] = y[b * nc:(b + 1) * nc, s * 128:(s + 1) * 128]


def _s5_mats(lam_re, lam_im, log_dt, b_re, b_im, c_re, c_im, rev):
    t16 = S5_CHUNK
    dt = jnp.exp(log_dt)[:, None]
    mag = jnp.exp(lam_re * dt)
    a_re, a_im = mag * jnp.cos(lam_im * dt), mag * jnp.sin(lam_im * dt)
    den = lam_re * lam_re + lam_im * lam_im
    nr = a_re - 1.0
    co_re = ((nr * lam_re + a_im * lam_im) / den)[..., None]
    co_im = ((a_im * lam_re - nr * lam_im) / den)[..., None]
    bb_re, bb_im = co_re * b_re - co_im * b_im, co_re * b_im + co_im * b_re
    pr, pi = [jnp.ones_like(a_re)], [jnp.zeros_like(a_im)]
    for _ in range(t16):
        pr, pi = pr + [pr[-1] * a_re - pi[-1] * a_im], pi + [pr[-1] * a_im + pi[-1] * a_re]
    pw_re, pw_im = jnp.stack(pr), jnp.stack(pi)
    g = lam_re.shape[0]
    e_re, e_im = pw_re[t16 - 1::-1][:t16], pw_im[t16 - 1::-1][:t16]
    wre = jnp.einsum('sgp,gph->gshp', e_re, bb_re) - jnp.einsum('sgp,gph->gshp', e_im, bb_im)
    wim = jnp.einsum('sgp,gph->gshp', e_re, bb_im) + jnp.einsum('sgp,gph->gshp', e_im, bb_re)
    cb_re = jnp.einsum('gkp,gph->gpkh', c_re, bb_re) - jnp.einsum('gkp,gph->gpkh', c_im, bb_im)
    cb_im = jnp.einsum('gkp,gph->gpkh', c_re, bb_im) + jnp.einsum('gkp,gph->gpkh', c_im, bb_re)
    kd = jnp.einsum('dgp,gpkh->dgkh', pw_re[:t16], cb_re) - jnp.einsum('dgp,gpkh->dgkh', pw_im[:t16], cb_im)
    lag = np.arange(t16)[None, :] - np.arange(t16)[:, None]
    toe = jnp.where((lag >= 0)[:, :, None, None, None], kd[np.clip(lag, 0, t16 - 1)], 0.0)
    toe = toe.transpose(2, 0, 4, 1, 3)
    q_re, q_im = pw_re[1:], pw_im[1:]
    ca_re = jnp.einsum('gkp,tgp->gptk', c_re, q_re) - jnp.einsum('gkp,tgp->gptk', c_im, q_im)
    ca_im = jnp.einsum('gkp,tgp->gptk', c_re, q_im) + jnp.einsum('gkp,tgp->gptk', c_im, q_re)
    if rev:
        wre, wim = wre[:, ::-1], wim[:, ::-1]
        toe = toe[:, ::-1, :, ::-1]
        ca_re, ca_im = ca_re[:, :, ::-1], ca_im[:, :, ::-1]
    nq, gl = g // 8, 8
    eye = jnp.eye(gl, dtype=F32)
    pack_w = lambda t: jnp.einsum('qgshp,gk->qsghkp', t.reshape(nq, gl, t16, S5_H, S5_P),
                                  eye).reshape(nq, t16 * 128, gl * S5_P)
    wm = jnp.concatenate([pack_w(wre), pack_w(wim)], axis=-1)
    tmat = jnp.einsum('qgshtk,gj->qsghtjk', toe.reshape(nq, gl, t16, S5_H, t16, S5_H),
                      eye).reshape(nq, t16 * 128, t16 * 128)
    pack_c = lambda t: jnp.einsum('qgptk,gj->qgptjk', t.reshape(nq, gl, S5_P, t16, S5_H),
                                  eye).reshape(nq, gl * S5_P, t16 * 128)
    cm = jnp.concatenate([pack_c(ca_re), -pack_c(ca_im)], axis=1)
    return (wm.astype(BF16), tmat.astype(BF16), cm.astype(BF16),
            pw_re[t16].reshape(nq, 1, gl * S5_P), pw_im[t16].reshape(nq, 1, gl * S5_P))


def _s5_scan(u, mats, rev):
    bsz, lt, _ = u.shape
    tblk = max(d for d in range(64, S5_TBLK + 1, 64) if lt % d == 0)
    nblk = lt // tblk
    nq = S5_G // 8
    wm, tmat, cm, ar, ai = mats
    kw, ks = S5_CHUNK * 128, 8 * S5_P
    tmap = (lambda q, t: (0, nblk - 1 - t, q)) if rev else (lambda q, t: (0, t, q))
    per = lambda shp: pl.BlockSpec((None,) + shp, lambda q, t: (q,) + tuple(0 for _ in shp))
    rows = bsz * tblk // S5_CHUNK
    return pl.pallas_call(
        functools.partial(_s5_kernel, rev=rev),
        grid=(nq, nblk),
        in_specs=[pl.BlockSpec((bsz, tblk, 128), tmap),
                  per((kw, 2 * ks)), per((kw, kw)), per((2 * ks, kw)), per((1, ks)), per((1, ks))],
        out_specs=pl.BlockSpec((bsz, tblk, 128), tmap),
        out_shape=jax.ShapeDtypeStruct((bsz, lt, S5_W), F32),
        scratch_shapes=[pltpu.VMEM((rows, 2 * ks), F32), pltpu.VMEM((rows, 2 * ks), F32),
                        pltpu.VMEM((8, 2 * ks), F32)],
        compiler_params=_cparams(("parallel", "arbitrary")),
        name="s5_scan",
    )(u, wm, tmat, cm, ar, ai)


def _mixer_tail(x, mix, mod_ref, wr_ref, br_ref, x_out, h2_out, lg_out):
    xn = x + mod_ref[2:3, :] * mix
    x_out[...] = xn
    h2 = _rms_rows(xn) * (1.0 + mod_ref[4:5, :]) + mod_ref[3:4, :]
    _to_token_tiles(h2_out, h2)
    lg_out[...] = jnp.dot(h2, wr_ref[...], precision=HI, preferred_element_type=F32) + br_ref[...]


def _ev_out_kernel(of_ref, ob_ref, g_ref, yf_ref, yb_ref, u_ref, ctx_ref, x_ref, mod_ref,
                   gn_ref, ds_ref, wglu_ref, wout_ref, wr_ref, br_ref,
                   x_out, h2_out, lg_out):
    o = of_ref[...] + ob_ref[...]
    og = jnp.concatenate([_rms_rows(o[:, h * GLA_DV:(h + 1) * GLA_DV]) for h in range(GLA_HEADS)], axis=1)
    g = g_ref[...]
    og = og * gn_ref[...] * (g * _sigmoid(g))
    t = yf_ref[...] + yb_ref[...] + ds_ref[...] * u_ref[...]
    y = t * (0.5 * (1.0 + jnp.tanh(math.sqrt(2.0 / math.pi) * (t + 0.044715 * (t * t * t)))))
    y = y * _sigmoid(jnp.dot(y.astype(BF16), wglu_ref[...], preferred_element_type=F32))
    cat = jnp.concatenate([og, y], axis=1).astype(BF16)
    mix = jnp.dot(cat, wout_ref[...], preferred_element_type=F32)
    _mixer_tail(_stream_tile(ctx_ref, x_ref), mix, mod_ref, wr_ref, br_ref, x_out, h2_out, lg_out)


def _ev_out(o_f, o_b, g, y_f, y_b, u, ctx, x, mods, gn, ds, wglu, wout, wr, br):
    bsz, lt, _ = u.shape
    nblk = lt // TB
    tok = lambda n: pl.BlockSpec((None, TB, n), lambda b, i: (b, i, 0))
    const = lambda shp: pl.BlockSpec(shp, lambda b, i: tuple(0 for _ in shp))
    return pl.pallas_call(
        _ev_out_kernel,
        grid=(bsz, nblk),
        in_specs=[tok(512), tok(512), tok(512), tok(512),
                  pl.BlockSpec((None, TB, S5_W), functools.partial(_swapped_index, nblk)),
                  tok(512)] + _stream_specs() + [
                  pl.BlockSpec((None, None, 6, D_MODEL), _mod_index),
                  const((1, 512)), const((1, 512)), const((512, 512)), const((D_MODEL, D_MODEL)),
                  const((D_MODEL, 128)), const((1, 128))],
        out_specs=[tok(D_MODEL), pl.BlockSpec((TB * ROW_TILE, 128), lambda b, i: (b * nblk + i, 0)), tok(128)],
        out_shape=[jax.ShapeDtypeStruct((bsz, lt, D_MODEL), F32),
                   jax.ShapeDtypeStruct((bsz * lt * ROW_TILE, 128), F32),
                   jax.ShapeDtypeStruct((bsz, lt, 128), F32)],
        compiler_params=_cparams(("parallel", "parallel")),
        name="ev_out",
    )(o_f, o_b, g, y_f, y_b, u, ctx, x, mods, gn, ds, wglu, wout, wr, br)


def _route(logits):
    n_tok = logits.shape[0]
    lg = logits[:, :N_GROUPS]
    grp = jnp.argmax(lg, axis=-1)
    g_w = jnp.max(jax.nn.softmax(lg, axis=-1), axis=-1, keepdims=True)
    le = logits[:, N_GROUPS:N_GROUPS + N_EXPERTS].reshape(n_tok, N_GROUPS, EXP_PER_GROUP)
    le = le[jnp.arange(n_tok), grp]
    top_p, top_i = lax.top_k(jax.nn.softmax(le, axis=-1), TOP_K)
    gate = g_w * top_p / jnp.sum(top_p, axis=-1, keepdims=True)
    eid = (grp[:, None] * EXP_PER_GROUP + top_i).reshape(-1)
    n_asg = n_tok * TOP_K
    order = jnp.argsort(eid).astype(jnp.int32)
    counts = jnp.sum((eid[:, None] == jnp.arange(N_EXPERTS)[None, :]).astype(jnp.int32), axis=0)
    padded = (counts + MOE_BLOCK - 1) // MOE_BLOCK * MOE_BLOCK
    pad_end = jnp.cumsum(padded)
    pad_start = pad_end - padded
    cnt_start = jnp.cumsum(counts) - counts
    n_blocks = -(-n_asg // MOE_BLOCK) + N_EXPERTS
    blk_start = jnp.arange(n_blocks, dtype=jnp.int32) * MOE_BLOCK
    blk_e = jnp.minimum(jnp.sum((pad_end[None, :] <= blk_start[:, None]).astype(jnp.int32), axis=1), N_EXPERTS - 1)
    pos = jnp.arange(n_blocks * MOE_BLOCK, dtype=jnp.int32)
    pos_e = jnp.repeat(blk_e, MOE_BLOCK)
    rank = pos - pad_start[pos_e]
    src = jnp.clip(cnt_start[pos_e] + rank, 0, n_asg - 1)
    slot_buf = jnp.where(rank < counts[pos_e], order[src], n_asg).astype(jnp.int32)
    n_valid = jnp.sum((slot_buf < n_asg).reshape(n_blocks, MOE_BLOCK), axis=1).astype(jnp.int32)
    return slot_buf, blk_e, n_valid, gate.astype(F32)


def _moe_kernel(slot_ref, blke_ref, nvalid_ref, h_hbm, w1_ref, w3_ref, w2_ref, z_hbm,
                xbuf, ybuf, wb1, wb3, wb2, gsem, ssem, *, n_tok):
    i = pl.program_id(0)
    nblk = pl.num_programs(0)
    cur = i % 2

    ns = ROW_TILE

    def gather_row(blk, buf, r, priority):
        asg = slot_ref[blk * MOE_BLOCK + r]
        tok = jnp.minimum(lax.shift_right_logical(asg, 1), n_tok - 1)
        pltpu.make_async_copy(h_hbm.at[pl.ds(pl.multiple_of(tok * ns, ns), ns)],
                              xbuf.at[buf, pl.ds(pl.multiple_of(r * ns, ns), ns)],
                              gsem.at[buf]).start(priority=priority)

    def scatter_row(blk, buf, r, priority):
        asg = slot_ref[blk * MOE_BLOCK + r]
        row = (asg & 1) * n_tok + lax.shift_right_logical(asg, 1)
        pltpu.make_async_copy(ybuf.at[buf, pl.ds(pl.multiple_of(r * ns, ns), ns)],
                              z_hbm.at[pl.ds(pl.multiple_of(row * ns, ns), ns)],
                              ssem.at[buf]).start(priority=priority)

    def all_rows(row_fn, blk, buf):
        def body(r8, carry):
            for j in range(MOE_ISSUE_UNROLL):
                row_fn(blk, buf, r8 * MOE_ISSUE_UNROLL + j, j % 2)
            return carry
        lax.fori_loop(0, MOE_BLOCK // MOE_ISSUE_UNROLL, body, 0)

    def gather(blk, buf):
        all_rows(gather_row, blk, buf)

    def wait_gather(buf):
        pltpu.make_async_copy(h_hbm.at[pl.ds(0, MOE_BLOCK * ns)], xbuf.at[buf], gsem.at[buf]).wait()

    def scatter(blk, buf):
        n = nvalid_ref[blk]

        @pl.when(n == MOE_BLOCK)
        def _():
            all_rows(scatter_row, blk, buf)

        @pl.when(n < MOE_BLOCK)
        def _():
            def body(r, carry):
                scatter_row(blk, buf, r, 0)
                return carry
            lax.fori_loop(0, n, body, 0)

    def wait_scatter(blk, buf):
        n = nvalid_ref[blk]

        @pl.when(n == MOE_BLOCK)
        def _():
            pltpu.make_async_copy(ybuf.at[buf], z_hbm.at[pl.ds(0, MOE_BLOCK * ns)], ssem.at[buf]).wait()

        @pl.when(n < MOE_BLOCK)
        def _():
            def body(r, carry):
                pltpu.make_async_copy(ybuf.at[buf, pl.ds(0, ns)], z_hbm.at[pl.ds(0, ns)], ssem.at[buf]).wait()
                return carry
            lax.fori_loop(0, n, body, 0)

    used = nvalid_ref[i] > 0

    @pl.when(jnp.logical_and(i == 0, used))
    def _():
        gather(0, 0)

    @pl.when(used)
    def _():
        wait_gather(cur)

        @pl.when(jnp.logical_and(i + 1 < nblk, nvalid_ref[jnp.minimum(i + 1, nblk - 1)] > 0))
        def _():
            gather(i + 1, 1 - cur)

        x = _from_token_tiles(xbuf.at[cur], MOE_BLOCK).astype(BF16)
        @pl.when(jnp.logical_or(i == 0, blke_ref[i] != blke_ref[jnp.maximum(i - 1, 0)]))
        def _():
            wb1[...] = w1_ref[...].astype(BF16)
            wb3[...] = w3_ref[...].astype(BF16)
            wb2[...] = w2_ref[...].astype(BF16)

        h1 = jnp.dot(x, wb1[...], preferred_element_type=F32)
        h3 = jnp.dot(x, wb3[...], preferred_element_type=F32)
        a = (h1 * _sigmoid(h1) * h3).astype(BF16)
        y = jnp.dot(a, wb2[...], preferred_element_type=F32)

        @pl.when(i >= 2)
        def _():
            wait_scatter(i - 2, cur)

        _to_token_tiles(ybuf.at[cur], y)
        scatter(i, cur)

    first_unused = jnp.logical_and(jnp.logical_not(used),
                                   jnp.logical_and(i > 0, nvalid_ref[jnp.maximum(i - 1, 0)] > 0))
    last_used = jnp.logical_and(used, i == nblk - 1)

    def drain(last):
        wait_scatter(last, last % 2)

        @pl.when(last >= 1)
        def _():
            wait_scatter(last - 1, (last - 1) % 2)

    @pl.when(first_unused)
    def _():
        drain(i - 1)

    @pl.when(last_used)
    def _():
        drain(i)


def _moe_experts(h2, slot_buf, blk_e, n_valid, w1, w3, w2, layer):
    n_tok = h2.shape[0] // ROW_TILE
    n_blocks = blk_e.shape[0]
    wspec = lambda shp: pl.BlockSpec((None, None) + shp, lambda i, slot, blke, nvalid: (layer, blke[i], 0, 0))
    grid_spec = pltpu.PrefetchScalarGridSpec(
        num_scalar_prefetch=3,
        grid=(n_blocks,),
        in_specs=[pl.BlockSpec(memory_space=pl.ANY),
                  wspec((D_MODEL, D_EXPERT)), wspec((D_MODEL, D_EXPERT)), wspec((D_EXPERT, D_MODEL))],
        out_specs=pl.BlockSpec(memory_space=pl.ANY),
        scratch_shapes=[pltpu.VMEM((2, MOE_BLOCK * ROW_TILE, 128), F32), pltpu.VMEM((2, MOE_BLOCK * ROW_TILE, 128), F32),
                        pltpu.VMEM((D_MODEL, D_EXPERT), BF16), pltpu.VMEM((D_MODEL, D_EXPERT), BF16),
                        pltpu.VMEM((D_EXPERT, D_MODEL), BF16),
                        pltpu.SemaphoreType.DMA((2,)), pltpu.SemaphoreType.DMA((2,))])
    return pl.pallas_call(
        functools.partial(_moe_kernel, n_tok=n_tok),
        grid_spec=grid_spec,
        out_shape=jax.ShapeDtypeStruct((TOP_K * n_tok * ROW_TILE, 128), F32),
        compiler_params=_cparams(("arbitrary",)),
        name="moe_experts",
    )(slot_buf, blk_e, n_valid, h2, w1, w3, w2)


def _moe_combine_kernel(x_ref, z0_ref, z1_ref, gate_ref, mod_ref, o_ref):
    gate = gate_ref[...]
    tb = x_ref.shape[0]
    y = gate[:, 0:1] * _from_token_tiles(z0_ref, tb) + gate[:, 1:2] * _from_token_tiles(z1_ref, tb)
    o_ref[...] = x_ref[...] + mod_ref[5:6, :] * y


def _moe_combine(x, z, gate, mods, mod_index):
    bsz, lt, _ = x.shape
    nblk = lt // TB
    gate3 = gate.reshape(bsz, lt, TOP_K)
    return pl.pallas_call(
        _moe_combine_kernel,
        grid=(bsz, nblk),
        in_specs=[pl.BlockSpec((None, TB, D_MODEL), lambda b, i: (b, i, 0)),
                  pl.BlockSpec((TB * ROW_TILE, 128), lambda b, i: (b * nblk + i, 0)),
                  pl.BlockSpec((TB * ROW_TILE, 128), lambda b, i: ((bsz + b) * nblk + i, 0)),
                  pl.BlockSpec((None, TB, TOP_K), lambda b, i: (b, i, 0)),
                  pl.BlockSpec((None, None, 6, D_MODEL), mod_index)],
        out_specs=pl.BlockSpec((None, TB, D_MODEL), lambda b, i: (b, i, 0)),
        out_shape=jax.ShapeDtypeStruct((bsz, lt, D_MODEL), F32),
        compiler_params=_cparams(("parallel", "parallel")),
        name="moe_combine",
    )(x, z, z, gate3, mods)


def _moe(x, h2, logits, mods, mod_index, w1, w3, w2, layer):
    bsz, lt, _ = x.shape
    slot_buf, blk_e, n_valid, gate = _route(logits.reshape(bsz * lt, 128))
    z = _moe_experts(h2, slot_buf, blk_e, n_valid, w1, w3, w2, layer)
    return _moe_combine(x, z, gate, mods, mod_index)


def _od_proj_kernel(x_ref, mod_ref, w_ref, gm_ref, qn_ref, kn_ref, q_ref, k_ref, v_ref, zh_ref):
    tm = x_ref.shape[0]
    is_ctx = pl.program_id(1) * tm + lax.broadcasted_iota(jnp.int32, (tm, 1), 0) < TB
    shift = jnp.where(is_ctx, mod_ref[0, 0:1, :], mod_ref[1, 0:1, :])
    scale = jnp.where(is_ctx, mod_ref[0, 1:2, :], mod_ref[1, 1:2, :])
    h = _rms_rows(x_ref[...]) * (1.0 + scale) + shift
    z = jnp.dot(h.astype(BF16), w_ref[...], preferred_element_type=F32)

    def head_norm(t, gain):
        ms = jnp.dot(t * t, gm_ref[...], precision=HI, preferred_element_type=F32)
        return t * lax.rsqrt(ms + EPS) * gain

    q_ref[...] = (head_norm(z[:, :NA_W], qn_ref[...]) * (NA_DH ** -0.5)).astype(BF16)
    k_ref[...] = head_norm(z[:, NA_W:2 * NA_W], kn_ref[...]).astype(BF16)
    v_ref[...] = z[:, 2 * NA_W:3 * NA_W].astype(BF16)
    zh_ref[...] = z[:, 3 * NA_W:]


def _od_proj(xcat, mods, w, gm, qn, kn):
    bsz, lt, _ = xcat.shape
    tm = max(d for d in range(TB, PROJ_TILE + 1, TB) if lt % d == 0)
    nblk = lt // tm
    tok = lambda n: pl.BlockSpec((None, tm, n), lambda b, i: (b, i, 0))
    const = lambda shp: pl.BlockSpec(shp, lambda b, i: tuple(0 for _ in shp))
    return pl.pallas_call(
        _od_proj_kernel,
        grid=(bsz, nblk),
        in_specs=[tok(D_MODEL), pl.BlockSpec((None, 2, 6, D_MODEL), lambda b, i: (b, 0, 0, 0)),
                  const((D_MODEL, 3 * NA_W + 3 * HY_W)), const((NA_W, NA_W)), const((1, NA_W)), const((1, NA_W))],
        out_specs=[tok(NA_W), tok(NA_W), tok(NA_W), tok(3 * HY_W)],
        out_shape=[jax.ShapeDtypeStruct((bsz, lt, NA_W), BF16), jax.ShapeDtypeStruct((bsz, lt, NA_W), BF16),
                   jax.ShapeDtypeStruct((bsz, lt, NA_W), BF16), jax.ShapeDtypeStruct((bsz, lt, 3 * HY_W), F32)],
        compiler_params=_cparams(("parallel", "parallel")),
        name="od_proj",
    )(xcat, mods, w, gm, qn, kn)


def _na_kernel(q_ref, k_ref, v_ref, t2_ref, o_ref):
    r = pl.program_id(1)
    n_rows = pl.num_programs(1)
    r0 = jnp.clip(r - WIN_R // 2, 0, n_rows - WIN_R)
    off = r0 - r + WIN_R - 1
    base = pl.multiple_of(TB + r0 * GRID_W, GRID_W)
    nloc = WIN_R * GRID_W
    q = q_ref[...]
    hg = NA_HG
    gw = hg * NA_DH
    lane_head = lax.broadcasted_iota(jnp.int32, (GRID_W, gw), 1) // NA_DH
    nt = (((1,), (1,)), ((), ()))
    outs = []
    for grp in range(NA_HEADS // hg):
        cs = slice(gw * grp, gw * (grp + 1))
        q2 = q[:, cs]
        q4 = jnp.concatenate([jnp.where(lane_head == h, q2, jnp.zeros_like(q2)) for h in range(hg)], axis=0)
        kw, vw = k_ref[pl.ds(base, nloc), cs], v_ref[pl.ds(base, nloc), cs]
        kc, vc = k_ref[0:TB, cs], v_ref[0:TB, cs]
        bias = jnp.concatenate(
            [jnp.concatenate([t2_ref[hg * grp + h, off + 2 * m] for m in range(WIN_R // 2)], axis=1)
             for h in range(hg)], axis=0)
        s_loc = lax.dot_general(q4, kw, nt, preferred_element_type=F32) + bias
        s_ctx = lax.dot_general(q4, kc, nt, preferred_element_type=F32)
        m = jnp.maximum(jnp.max(s_loc, axis=1, keepdims=True), jnp.max(s_ctx, axis=1, keepdims=True))
        p_loc, p_ctx = jnp.exp(s_loc - m), jnp.exp(s_ctx - m)
        den = jnp.sum(p_loc, axis=1, keepdims=True) + jnp.sum(p_ctx, axis=1, keepdims=True)
        o4 = (jnp.dot(p_loc.astype(BF16), vw, preferred_element_type=F32)
              + jnp.dot(p_ctx.astype(BF16), vc, preferred_element_type=F32)) / den
        acc = jnp.zeros((GRID_W, gw), F32)
        for h in range(hg):
            acc = jnp.where(lane_head == h, o4[h * GRID_W:(h + 1) * GRID_W, :], acc)
        outs.append(acc)
    o_ref[...] = jnp.concatenate(outs, axis=1)


def _na_bias_table(rpb):
    qc = np.arange(GRID_W)[:, None]
    kc = np.arange(GRID_W)[None, :]
    q_start = np.clip(qc - WIN_C // 2, 0, GRID_W - WIN_C)
    valid = (kc >= q_start) & (kc < q_start + WIN_C)
    col_idx = np.clip(kc - qc + WIN_C - 1, 0, 2 * WIN_C - 2)
    t = jnp.where(valid[None, None], rpb.astype(F32)[:, :, col_idx], NEG_INF)
    return jnp.concatenate([t[:, :-1], t[:, 1:]], axis=-1)


def _na(q, k, v, t2):
    bsz, lt, _ = q.shape
    n_rows = (lt - TB) // GRID_W
    qoff = TB // GRID_W
    return pl.pallas_call(
        _na_kernel,
        grid=(bsz, n_rows),
        in_specs=[pl.BlockSpec((None, GRID_W, NA_W), lambda b, r: (b, r + qoff, 0)),
                  pl.BlockSpec((None, lt, NA_W), lambda b, r: (b, 0, 0)),
                  pl.BlockSpec((None, lt, NA_W), lambda b, r: (b, 0, 0)),
                  pl.BlockSpec(t2.shape, lambda b, r: (0, 0, 0, 0))],
        out_specs=pl.BlockSpec((None, GRID_W, NA_W), lambda b, r: (b, r, 0)),
        out_shape=jax.ShapeDtypeStruct((bsz, lt - TB, NA_W), F32),
        compiler_params=_cparams(("parallel", "arbitrary")),
        name="na_attn",
    )(q, k, v, t2)


def _hy_pre_kernel(z_ref, zp_ref, zn_ref, cw_ref, cb_ref, x0_ref, u_ref, ut_ref):
    i = pl.program_id(1)
    n = pl.num_programs(1)
    z = z_ref[...]
    tb = z.shape[0]
    prev_row = jnp.where(i > 0, zp_ref[7:8, :], 0.0)
    next_row = jnp.where(i < n - 1, zn_ref[0:1, :], 0.0)
    rowid = lax.broadcasted_iota(jnp.int32, z.shape, 0)
    zm = jnp.where(rowid == 0, prev_row, pltpu.roll(z, 1, 0))
    zp = jnp.where(rowid == tb - 1, next_row, pltpu.roll(z, tb - 1, 0))
    zc = cb_ref[...] + cw_ref[0:1, :] * zm
    zc = zc + cw_ref[1:2, :] * z
    zc = zc + cw_ref[2:3, :] * zp
    x0_ref[...] = zc[:, :HY_W]
    u = zc[:, HY_W:2 * HY_W] * zc[:, 2 * HY_W:]
    u_ref[...] = u
    for j in range(tb // FFT_N1):
        ut_ref[j] = u[j * FFT_N1:(j + 1) * FFT_N1, :].T.astype(BF16)


def _hy_pre(zh, cw, cb):
    bsz, lt, _ = zh.shape
    l = lt - TB
    nblk = l // TB
    h8 = TB // 8
    return pl.pallas_call(
        _hy_pre_kernel,
        grid=(bsz, nblk),
        in_specs=[pl.BlockSpec((None, TB, 3 * HY_W), lambda b, i: (b, i + 1, 0)),
                  pl.BlockSpec((None, 8, 3 * HY_W), lambda b, i: (b, (i + 1) * h8 - 1, 0)),
                  pl.BlockSpec((None, 8, 3 * HY_W), lambda b, i: (b, jnp.minimum((i + 2) * h8, lt // 8 - 1), 0)),
                  pl.BlockSpec((HY_SHORT, 3 * HY_W), lambda b, i: (0, 0)),
                  pl.BlockSpec((1, 3 * HY_W), lambda b, i: (0, 0))],
        out_specs=[pl.BlockSpec((None, TB, HY_W), lambda b, i: (b, i, 0)),
                   pl.BlockSpec((None, TB, HY_W), lambda b, i: (b, i, 0)),
                   pl.BlockSpec((None, TB // FFT_N1, HY_W, FFT_N1), lambda b, i: (b, i, 0, 0))],
        out_shape=[jax.ShapeDtypeStruct((bsz, l, HY_W), F32), jax.ShapeDtypeStruct((bsz, l, HY_W), F32),
                   jax.ShapeDtypeStruct((bsz, l // FFT_N1, HY_W, FFT_N1), BF16)],
        compiler_params=_cparams(("parallel", "parallel")),
        name="hy_pre",
    )(zh, zh, zh, cw, cb)


def _fft_consts(n1_in):
    n = FFT_N1
    idx = np.arange(n)
    ang1 = 2.0 * np.pi * np.outer(idx, idx) / n
    c, s = np.cos(ang1), np.sin(ang1)
    angt = 2.0 * np.pi * np.outer(idx, idx) / (n * n)
    tw = np.concatenate([np.cos(angt), -np.sin(angt)], axis=1)
    f3 = np.block([[c, -s], [s, c]])
    f3i = np.block([[c, s], [-s, c]])
    ch, sh = c[:, :n1_in], s[:, :n1_in]
    f1_pair = np.block([[ch, sh], [-sh, ch]])
    f1_real = np.concatenate([c, -s], axis=0)
    f1i = np.block([[ch.T, -sh.T], [sh.T, ch.T]]) / (n * n)
    return tw, f3, f3i, f1_pair, f1_real, f1i


def _fft_forward(a, tw_re, tw_im, lhs_scr, ncg):
    for cix in range(ncg):
        cs = slice(cix * FFT_N1, (cix + 1) * FFT_N1)
        are, aim = a[:FFT_N1, cs], a[FFT_N1:, cs]
        lhs_scr[cs, :FFT_N1] = are * tw_re - aim * tw_im
        lhs_scr[cs, FFT_N1:] = are * tw_im + aim * tw_re


def _hy_filt_kernel(k_ref, f1_ref, tw_ref, f3_ref, o_ref, lhs_scr):
    a = jnp.dot(f1_ref[...], k_ref[...], precision=HI, preferred_element_type=F32)
    _fft_forward(a, tw_ref[:, :FFT_N1], tw_ref[:, FFT_N1:], lhs_scr, FFT_CG)
    o_ref[...] = jnp.dot(lhs_scr[...], f3_ref[...], precision=HI, preferred_element_type=F32)


def _hy_fft_kernel(u_ref, kf_ref, f1_ref, tw_ref, f3_ref, f3i_ref, f1i_ref, y_ref, lhs_scr, a2_scr):
    cgl = FFT_CG * FFT_N1
    x = u_ref[...].reshape(2 * u_ref.shape[1], cgl)
    a = jnp.dot(f1_ref[...], x, preferred_element_type=F32)
    tw_re, tw_im = tw_ref[:, :FFT_N1], tw_ref[:, FFT_N1:]
    _fft_forward(a, tw_re, tw_im, lhs_scr, FFT_CG)
    y = jnp.dot(lhs_scr[...].astype(BF16), f3_ref[...], preferred_element_type=F32)
    yre, yim = y[:, :FFT_N1], y[:, FFT_N1:]
    kre, kim = kf_ref[:, :FFT_N1], kf_ref[:, FFT_N1:]
    z = jnp.concatenate([yre * kre - yim * kim, yre * kim + yim * kre], axis=1).astype(BF16)
    bp = jnp.dot(z, f3i_ref[...], preferred_element_type=F32)
    for cix in range(FFT_CG):
        cs = slice(cix * FFT_N1, (cix + 1) * FFT_N1)
        bre, bim = bp[cs, :FFT_N1], bp[cs, FFT_N1:]
        a2_scr[:FFT_N1, cs] = bre * tw_re + bim * tw_im
        a2_scr[FFT_N1:, cs] = bim * tw_re - bre * tw_im
    out = jnp.dot(f1i_ref[...], a2_scr[...].astype(BF16), preferred_element_type=F32)
    y_ref[...] = out.reshape(2, u_ref.shape[1], cgl)


def _hy_conv(ut, kfilt):
    bsz, n1h, nch, _ = ut.shape
    assert 2 * n1h == FFT_N1 and bsz % 2 == 0
    cgl = FFT_CG * FFT_N1
    ncol = nch * FFT_N1
    tw, f3, f3i, f1_pair, f1_real, f1i = _fft_consts(n1h)
    kt = kfilt.reshape(FFT_N1, FFT_N1, nch).transpose(0, 2, 1).reshape(FFT_N1, ncol)
    const2 = lambda shp: pl.BlockSpec(shp, lambda *a: (0, 0))
    kf = pl.pallas_call(
        _hy_filt_kernel,
        grid=(nch // FFT_CG,),
        in_specs=[pl.BlockSpec((FFT_N1, cgl), lambda j: (0, j)), const2((2 * FFT_N1, FFT_N1)),
                  const2((FFT_N1, 2 * FFT_N1)), const2((2 * FFT_N1, 2 * FFT_N1))],
        out_specs=pl.BlockSpec((cgl, 2 * FFT_N1), lambda j: (j, 0)),
        out_shape=jax.ShapeDtypeStruct((ncol, 2 * FFT_N1), F32),
        scratch_shapes=[pltpu.VMEM((cgl, 2 * FFT_N1), F32)],
        compiler_params=_cparams(("parallel",)),
        name="hy_filter_dft",
    )(kt, jnp.asarray(f1_real, F32), jnp.asarray(tw, F32), jnp.asarray(f3, F32))
    u2 = ut.reshape(bsz, n1h, ncol)
    y = pl.pallas_call(
        _hy_fft_kernel,
        grid=(bsz // 2, nch // FFT_CG),
        in_specs=[pl.BlockSpec((2, n1h, cgl), lambda p, j: (p, 0, j)),
                  pl.BlockSpec((cgl, 2 * FFT_N1), lambda p, j: (j, 0)),
                  const2((2 * FFT_N1, FFT_N1)), const2((FFT_N1, 2 * FFT_N1)),
                  const2((2 * FFT_N1, 2 * FFT_N1)), const2((2 * FFT_N1, 2 * FFT_N1)), const2((FFT_N1, 2 * FFT_N1))],
        out_specs=pl.BlockSpec((2, n1h, cgl), lambda p, j: (p, 0, j)),
        out_shape=jax.ShapeDtypeStruct((bsz, n1h, ncol), F32),
        scratch_shapes=[pltpu.VMEM((cgl, 2 * FFT_N1), F32), pltpu.VMEM((2 * FFT_N1, cgl), F32)],
        compiler_params=_cparams(("parallel", "parallel")),
        name="hy_fft_conv",
    )(u2, kf, jnp.asarray(f1_pair, F32).astype(BF16), jnp.asarray(tw, F32), jnp.asarray(f3, F32).astype(BF16),
      jnp.asarray(f3i, F32).astype(BF16), jnp.asarray(f1i, F32).astype(BF16))
    return y.reshape(bsz, n1h, nch, FFT_N1)


def _hy_filter(seqlen, fw1, fb1, fw2, fb2, fw3, fb3, freq, fw4):
    t = jnp.linspace(0.0, 1.0, seqlen, dtype=F32)[:, None]
    bands = (HY_EMB - 1) // 2
    w = 2.0 * math.pi * jnp.arange(seqlen, dtype=F32)[:, None] / seqlen
    f = jnp.linspace(1e-4, bands - 1, bands, dtype=F32)[None, :]
    feat = jnp.concatenate([t, jnp.cos(f * w), -jnp.sin(f * w)], axis=-1)
    feat2 = jnp.concatenate([feat, feat[::-1]], axis=0)
    h = _mm(feat2, fw1, fb1, freq)
    h = _mm(h, fw2, fb2, freq)
    h = _mm(h, fw3, fb3, freq)
    deltas = jnp.abs(jnp.linspace(math.log(HY_DECAY_TARGET) / HY_DECAY_LONG_PCT,
                                  math.log(HY_DECAY_TARGET) / HY_DECAY_SHORT_PCT, HY_W, dtype=F32))
    hf = _mm(h[:seqlen], fw4[:, :HY_W]) * jnp.exp(-t * deltas)
    hb = _mm(h[seqlen:], fw4[:, HY_W:]) * jnp.exp(-t[::-1] * deltas)
    return jnp.concatenate([hf, jnp.zeros((1, HY_W), F32), hb[:seqlen - 1]], axis=0)


def _od_out_kernel(na_ref, yt_ref, x0_ref, u_ref, x_ref, mod_ref, hb_ref, wout_ref, wr_ref, br_ref,
                   x_out, h2_out, lg_out):
    y = jnp.concatenate([yt_ref[j].T for j in range(yt_ref.shape[0])], axis=0)
    hy = x0_ref[...] * (y + u_ref[...] * hb_ref[...])
    cat = jnp.concatenate([na_ref[...], hy], axis=1).astype(BF16)
    mix = jnp.dot(cat, wout_ref[...], preferred_element_type=F32)
    _mixer_tail(x_ref[...], mix, mod_ref, wr_ref, br_ref, x_out, h2_out, lg_out)


def _latent_mod_index(b, i):
    return (b, 1, 0, 0)


def _od_out(na, yt, x0, u, xcat, mods, hb, wout, wr, br):
    bsz, l, _ = na.shape
    nblk = l // TB
    tok = lambda n: pl.BlockSpec((None, TB, n), lambda b, i: (b, i, 0))
    const = lambda shp: pl.BlockSpec(shp, lambda b, i: tuple(0 for _ in shp))
    return pl.pallas_call(
        _od_out_kernel,
        grid=(bsz, nblk),
        in_specs=[tok(NA_W), pl.BlockSpec((None, TB // FFT_N1, HY_W, FFT_N1), lambda b, i: (b, i, 0, 0)),
                  tok(HY_W), tok(HY_W),
                  pl.BlockSpec((None, TB, D_MODEL), lambda b, i: (b, i + 1, 0)),
                  pl.BlockSpec((None, None, 6, D_MODEL), _latent_mod_index),
                  const((1, HY_W)), const((D_MODEL, D_MODEL)), const((D_MODEL, 128)), const((1, 128))],
        out_specs=[tok(D_MODEL), pl.BlockSpec((TB * ROW_TILE, 128), lambda b, i: (b * nblk + i, 0)), tok(128)],
        out_shape=[jax.ShapeDtypeStruct((bsz, l, D_MODEL), F32), jax.ShapeDtypeStruct((bsz * l * ROW_TILE, 128), F32),
                   jax.ShapeDtypeStruct((bsz, l, 128), F32)],
        compiler_params=_cparams(("parallel", "parallel")),
        name="od_out",
    )(na, yt, x0, u, xcat, mods, hb, wout, wr, br)


def _mods(c, c_ctx, ada_w, ada_b):
    bsz = c.shape[0]
    depth, _, n = ada_w.shape
    cc = jnp.concatenate([c, c_ctx[None]], axis=0)
    a = jnp.pad(cc * _sigmoid(cc), ((0, 8 - (bsz + 1) % 8), (0, 0)))
    mp, tn = a.shape[0], 1024
    m = pl.pallas_call(
        functools.partial(_mm_kernel, use_sin=False),
        grid=(depth, n // tn),
        in_specs=[pl.BlockSpec((mp, D_MODEL), lambda l, j: (0, 0)),
                  pl.BlockSpec((None, D_MODEL, tn), lambda l, j: (l, 0, j)),
                  pl.BlockSpec((None, 1, tn), lambda l, j: (l, 0, j)),
                  pl.BlockSpec((1, tn), lambda l, j: (0, 0))],
        out_specs=pl.BlockSpec((None, mp, tn), lambda l, j: (l, 0, j)),
        out_shape=jax.ShapeDtypeStruct((depth, mp, n), F32),
        compiler_params=_cparams(("parallel", "parallel")),
        name="adaln_dense",
    )(a, ada_w, ada_b.reshape(depth, 1, n), jnp.ones((1, tn), F32))
    mod_l = m[:, :bsz].reshape(depth, bsz, 1, 6, D_MODEL)
    mod_c = jnp.broadcast_to(m[:, bsz].reshape(depth, 1, 1, 6, D_MODEL), (depth, bsz, 1, 6, D_MODEL))
    return jnp.concatenate([mod_c, mod_l], axis=2)


def _rope_tables(seqlen):
    pos = jnp.arange(seqlen)
    half = GLA_DK // 4
    freqs = ROPE_BASE ** (-jnp.arange(half, dtype=F32) / half)
    ar = (pos // GRID_W).astype(F32)[:, None] * freqs
    ac = (pos % GRID_W).astype(F32)[:, None] * freqs
    cos = jnp.concatenate([jnp.cos(ar), jnp.cos(ar), jnp.cos(ac), jnp.cos(ac)], axis=1)
    sin = jnp.concatenate([-jnp.sin(ar), jnp.sin(ar), -jnp.sin(ac), jnp.sin(ac)], axis=1)
    cos = jnp.concatenate([jnp.ones((TB, GLA_DK), F32), cos], axis=0)
    sin = jnp.concatenate([jnp.zeros((TB, GLA_DK), F32), sin], axis=0)
    return jnp.tile(cos, (1, GLA_HEADS)), jnp.tile(sin, (1, GLA_HEADS))


def _router_weights(wg, bg, we, be):
    pad = 128 - N_GROUPS - N_EXPERTS
    wr = jnp.concatenate([wg, we, jnp.zeros((D_MODEL, pad), F32)], axis=1)
    br = jnp.concatenate([bg, be, jnp.zeros((pad,), F32)]).reshape(1, 128)
    return wr, br


def kernel(x, c, ctx, c_ctx, ada_w, ada_b, moe_wg, moe_bg, moe_we, moe_be, moe_w1, moe_w3, moe_w2, ev_w_in, ev_w_out, gla_wa2, gla_ba, gla_norm, s5_lam_re, s5_lam_im, s5_log_dt, s5_b_re, s5_b_im, s5_c_re, s5_c_im, s5_d, s5_w_glu, od_w_in, od_w_out, na_q_norm, na_k_norm, na_rpb, hy_conv_w, hy_conv_b, hy_fw1, hy_fb1, hy_fw2, hy_fb2, hy_fw3, hy_fb3, hy_freq, hy_fw4, hy_bias):
    bsz, seqlen, _ = x.shape
    assert ctx.shape[1] == TB and seqlen % TB == 0

    mods_all = _mods(c, c_ctx, ada_w, ada_b)
    mods = mods_all[0]
    w_in = ev_w_in[0]
    n_a = 2 * GLA_RANK
    a0 = 2 * GLA_QK + 2 * GLA_V
    w_ev = jnp.concatenate([w_in[:, :a0], w_in[:, a0 + n_a:], w_in[:, a0:a0 + n_a],
                            jnp.zeros((D_MODEL, 128 - n_a), F32)], axis=1).astype(BF16)
    wa = jnp.zeros((128, 2 * GLA_QK), F32)
    for d in range(2):
        wa = wa.at[d * GLA_RANK:(d + 1) * GLA_RANK, d * GLA_QK:(d + 1) * GLA_QK].set(gla_wa2[0, d])
    cos, sin = _rope_tables(seqlen)
    q, k, v, g, u, u_sw, la = _ev_proj(ctx, x, mods, w_ev, wa, gla_ba[0].reshape(1, 2 * GLA_QK), cos, sin)
    o_f, o_b = _gla(q, k, v, la)
    s5p = [t[0].astype(F32) for t in (s5_lam_re, s5_lam_im, s5_log_dt, s5_b_re, s5_b_im, s5_c_re, s5_c_im)]
    y_f = _s5_scan(u, _s5_mats(*[t[0] for t in s5p], rev=False), rev=False)
    y_b = _s5_scan(u_sw, _s5_mats(*[t[1] for t in s5p], rev=True), rev=True)
    wr, br = _router_weights(moe_wg[0], moe_bg[0], moe_we[0], moe_be[0])
    x1, h2, lg = _ev_out(o_f, o_b, g, y_f, y_b, u, ctx, x, mods,
                         jnp.tile(gla_norm[0], GLA_HEADS).reshape(1, GLA_V), s5_d[0].reshape(1, S5_W),
                         s5_w_glu[0].astype(BF16), ev_w_out[0].astype(BF16), wr, br)
    xcat = _moe(x1, h2, lg, mods, _mod_index,
                moe_w1, moe_w3, moe_w2, 0)

    mods = mods_all[1]
    hd = np.arange(NA_W) // NA_DH
    gm = jnp.asarray((hd[:, None] == hd[None, :]).astype(np.float32) / NA_DH)
    qh, kh, vh, zh = _od_proj(xcat, mods, od_w_in[0].astype(BF16), gm,
                              jnp.tile(na_q_norm[0], NA_HEADS).reshape(1, NA_W),
                              jnp.tile(na_k_norm[0], NA_HEADS).reshape(1, NA_W))
    na = _na(qh, kh, vh, _na_bias_table(na_rpb[0]))
    x0, uh, ut = _hy_pre(zh, hy_conv_w[0], hy_conv_b[0].reshape(1, 3 * HY_W))
    kfilt = _hy_filter(seqlen, hy_fw1[0], hy_fb1[0], hy_fw2[0], hy_fb2[0], hy_fw3[0], hy_fb3[0],
                       hy_freq[0], hy_fw4[0])
    yt = _hy_conv(ut, kfilt)
    wr, br = _router_weights(moe_wg[1], moe_bg[1], moe_we[1], moe_be[1])
    xl, h2, lg = _od_out(na, yt, x0, uh, xcat, mods, hy_bias[0].reshape(1, HY_W),
                         od_w_out[0].astype(BF16), wr, br)
    return _moe(xl, h2, lg, mods, _latent_mod_index,
                moe_w1, moe_w3, moe_w2, 1)
```

```python
import functools
import math

import numpy as np
import jax
import jax.numpy as jnp
from jax import lax
from jax.experimental import pallas as pl
from jax.experimental.pallas import tpu as pltpu

F32, BF16 = jnp.float32, jnp.bfloat16
HI = lax.Precision.HIGHEST

D_MODEL = 1024
GRID_W = 64
EPS = 1e-6
ROPE_BASE = 10000.0
NEG_INF = -1e30
GLA_HEADS, GLA_DK, GLA_DV = 4, 64, 128
GLA_QK, GLA_V = GLA_HEADS * GLA_DK, GLA_HEADS * GLA_DV
GLA_RANK = 16
GLA_TAU = 16.0
GLA_CHUNK = 64
GLA_LOG_ALPHA_MIN = -1.0
S5_W, S5_H, S5_P = 512, 16, 64
S5_G = S5_W // S5_H
S5_CHUNK = 8
S5_TBLK = 1408
NA_HEADS, NA_DH = 8, 64
NA_W = NA_HEADS * NA_DH
WIN_R, WIN_C = 8, 16
NA_HG = 4
HY_W = 512
HY_SHORT = 3
HY_EMB = 33
HY_DECAY_TARGET = 1e-2
HY_DECAY_SHORT_PCT = 0.3
HY_DECAY_LONG_PCT = 1.5
N_GROUPS, EXP_PER_GROUP = 4, 8
N_EXPERTS = N_GROUPS * EXP_PER_GROUP
D_EXPERT = 512
TOP_K = 2
MOE_BLOCK = 256
MOE_ISSUE_UNROLL = 8

TB = 256
PROJ_TILE = 768
FFT_N1 = 128
FFT_CG = 16
V7X_VMEM_LIMIT = 52 * 1024 * 1024


def _cparams(sem):
    return pltpu.CompilerParams(dimension_semantics=sem, vmem_limit_bytes=V7X_VMEM_LIMIT)


def _sigmoid(x):
    return 1.0 / (1.0 + jnp.exp(-x))


ROW_TILE = D_MODEL // 128


def _to_token_tiles(ref, val):
    n = val.shape[0]
    for j in range(ROW_TILE):
        ref[pl.ds(j, n, stride=ROW_TILE), :] = val[:, j * 128:(j + 1) * 128]


def _from_token_tiles(ref, n):
    return jnp.concatenate([ref[pl.ds(j, n, stride=ROW_TILE), :] for j in range(ROW_TILE)], axis=1)


def _split_bf16(x):
    hi = x.astype(BF16)
    return hi, (x - hi.astype(F32)).astype(BF16)


def _dot_x3(a, b):
    a_hi, a_lo = _split_bf16(a)
    b_hi, b_lo = _split_bf16(b)
    d = lambda p, q: jnp.dot(p, q, preferred_element_type=F32)
    return d(a_hi, b_hi) + d(a_lo, b_hi) + d(a_hi, b_lo)


def _rms_rows(x):
    return x * lax.rsqrt(jnp.mean(x * x, axis=-1, keepdims=True) + EPS)


def _mm_kernel(a_ref, w_ref, b_ref, f_ref, o_ref, *, use_sin):
    z = jnp.dot(a_ref[...], w_ref[...], precision=HI, preferred_element_type=F32) + b_ref[...]
    if use_sin:
        z = jnp.sin(f_ref[...] * z)
    o_ref[...] = z


def _mm(a, w, bias=None, freq=None):
    m, k = a.shape
    n = w.shape[1]
    mp, kp = -(-m // 8) * 8, -(-k // 128) * 128
    tm = min(mp, 1024)
    mp = -(-mp // tm) * tm
    tn = n if n <= 1024 else 1024
    assert n % tn == 0
    a = jnp.pad(a.astype(F32), ((0, mp - m), (0, kp - k)))
    w = jnp.pad(w.astype(F32), ((0, kp - k), (0, 0)))
    bias = jnp.zeros((n,), F32) if bias is None else bias.astype(F32)
    use_sin = freq is not None
    freq = jnp.ones((n,), F32) if freq is None else freq.astype(F32)
    out = pl.pallas_call(
        functools.partial(_mm_kernel, use_sin=use_sin),
        grid=(mp // tm, n // tn),
        in_specs=[pl.BlockSpec((tm, kp), lambda i, j: (i, 0)),
                  pl.BlockSpec((kp, tn), lambda i, j: (0, j)),
                  pl.BlockSpec((1, tn), lambda i, j: (0, j)),
                  pl.BlockSpec((1, tn), lambda i, j: (0, j))],
        out_specs=pl.BlockSpec((tm, tn), lambda i, j: (i, j)),
        out_shape=jax.ShapeDtypeStruct((mp, n), F32),
        compiler_params=_cparams(("parallel", "parallel")),
        name="small_dense",
    )(a, w, bias.reshape(1, n), freq.reshape(1, n))
    return out[:m]


def _mod_index(b, i):
    return (b, jnp.minimum(i, 1), 0, 0)


def _swapped_index(nblk, b, i):
    return (b, jnp.where(i == 0, nblk - 1, i - 1), 0)


EV_NQ, EV_NK, EV_NV, EV_NG, EV_NU = 0, 256, 512, 1024, 1536
EV_NA = 2048
EV_NTOT = 2176


def _stream_tile(ctx_ref, x_ref):
    return jnp.where(pl.program_id(1) == 0, ctx_ref[...], x_ref[...])


def _stream_specs():
    return [pl.BlockSpec((None, TB, D_MODEL), lambda b, i: (b, 0, 0)),
            pl.BlockSpec((None, TB, D_MODEL), lambda b, i: (b, jnp.maximum(i - 1, 0), 0))]


def _ev_proj_kernel(ctx_ref, x_ref, mod_ref, w_ref, wa_ref, ba_ref, cos_ref, sin_ref,
                    q_ref, k_ref, v_ref, g_ref, u_ref, usw_ref, la_ref):
    x = _stream_tile(ctx_ref, x_ref)
    h = _rms_rows(x) * (1.0 + mod_ref[1:2, :]) + mod_ref[0:1, :]
    z = jnp.dot(h.astype(BF16), w_ref[...], preferred_element_type=F32)
    lane = lax.broadcasted_iota(jnp.int32, (x.shape[0], GLA_QK), 1)
    first = (lane % 32) < 16
    cos, sin = cos_ref[...], sin_ref[...]

    def rot(t):
        partner = jnp.where(first, pltpu.roll(t, GLA_QK - 16, 1), pltpu.roll(t, 16, 1))
        return t * cos + partner * sin

    q_ref[...] = rot(z[:, EV_NQ:EV_NQ + GLA_QK]) * (GLA_DK ** -0.5)
    k_ref[...] = rot(z[:, EV_NK:EV_NK + GLA_QK])
    v_ref[...] = z[:, EV_NV:EV_NV + GLA_V].astype(BF16)
    g_ref[...] = z[:, EV_NG:EV_NG + GLA_V]
    u_ref[...] = z[:, EV_NU:EV_NU + S5_W]
    usw_ref[...] = z[:, EV_NU:EV_NU + S5_W]
    a = z[:, EV_NA:EV_NA + 128]
    pre = _dot_x3(a, wa_ref[...]) + ba_ref[...]
    ls = jnp.minimum(pre, 0.0) - jnp.log1p(jnp.exp(-jnp.abs(pre)))
    la_ref[...] = jnp.maximum(ls / GLA_TAU, GLA_LOG_ALPHA_MIN)


def _ev_proj(ctx, x, mods, w, wa, ba, cos, sin):
    bsz = x.shape[0]
    lt = ctx.shape[1] + x.shape[1]
    nblk = lt // TB
    tok = lambda n: pl.BlockSpec((None, TB, n), lambda b, i: (b, i, 0))
    const = lambda shp: pl.BlockSpec(shp, lambda b, i: tuple(0 for _ in shp))
    return pl.pallas_call(
        _ev_proj_kernel,
        grid=(bsz, nblk),
        in_specs=_stream_specs() + [
                  pl.BlockSpec((None, None, 6, D_MODEL), _mod_index),
                  const((D_MODEL, EV_NTOT)), const((128, 2 * GLA_QK)), const((1, 2 * GLA_QK)),
                  pl.BlockSpec((TB, GLA_QK), lambda b, i: (i, 0)),
                  pl.BlockSpec((TB, GLA_QK), lambda b, i: (i, 0))],
        out_specs=[tok(GLA_QK), tok(GLA_QK), tok(GLA_V), tok(GLA_V), tok(S5_W),
                   pl.BlockSpec((None, TB, S5_W), functools.partial(_swapped_index, nblk)), tok(2 * GLA_QK)],
        out_shape=[jax.ShapeDtypeStruct((bsz, lt, GLA_QK), F32),
                   jax.ShapeDtypeStruct((bsz, lt, GLA_QK), F32),
                   jax.ShapeDtypeStruct((bsz, lt, GLA_V), BF16),
                   jax.ShapeDtypeStruct((bsz, lt, GLA_V), F32),
                   jax.ShapeDtypeStruct((bsz, lt, S5_W), F32),
                   jax.ShapeDtypeStruct((bsz, lt, S5_W), F32),
                   jax.ShapeDtypeStruct((bsz, lt, 2 * GLA_QK), F32)],
        compiler_params=_cparams(("parallel", "parallel")),
        name="ev_proj",
    )(ctx, x, mods, w, wa, ba, cos, sin)


def _gla_kernel(qf_ref, kf_ref, vf_ref, laf_ref, qb_ref, kb_ref, vb_ref, lab_ref,
                of_ref, ob_ref, s_scr):
    i = pl.program_id(1)

    @pl.when(i == 0)
    def _():
        s_scr[...] = jnp.zeros_like(s_scr)

    c = GLA_CHUNK
    nh = GLA_HEADS
    row = lax.broadcasted_iota(jnp.int32, (c, c), 0)
    col = lax.broadcasted_iota(jnp.int32, (c, c), 1)
    row4 = lax.broadcasted_iota(jnp.int32, (nh * c, c), 0) % c
    col4 = lax.broadcasted_iota(jnp.int32, (nh * c, c), 1)
    lane_head = lax.broadcasted_iota(jnp.int32, (c, GLA_QK), 1) // GLA_DK
    out_head = lax.broadcasted_iota(jnp.int32, (c, GLA_V), 1) // GLA_DV
    bd_mask = (lax.broadcasted_iota(jnp.int32, (GLA_V, GLA_QK), 0) // GLA_DV
               == lax.broadcasted_iota(jnp.int32, (GLA_V, GLA_QK), 1) // GLA_DK)
    nchunk = qf_ref.shape[0] // c
    nt = (((1,), (1,)), ((), ()))
    tn = (((0,), (0,)), ((), ()))

    def one_chunk(refs, o_ref, d, r0):
        q_ref, k_ref, v_ref, la_ref = refs
        fwd = d == 0
        sl = pl.ds(r0, c)
        qc, kc, vc, lac = q_ref[sl, :], k_ref[sl, :], v_ref[sl, :], la_ref[sl, :]
        tri = ((row >= col) if fwd else (row <= col)).astype(BF16)
        la_hi, la_lo = _split_bf16(lac)
        b = (jnp.dot(tri, la_hi, preferred_element_type=F32)
             + jnp.dot(tri, la_lo, preferred_element_type=F32))
        b_last = b[c - 1:c, :] if fwd else b[0:1, :]
        qe = (qc * jnp.exp(b)).astype(BF16)
        ke = (kc * jnp.exp(-b)).astype(BF16)
        kd = (kc * jnp.exp(b_last - b)).astype(BF16)
        st = s_scr[d]
        o = lax.dot_general(qe, st.astype(BF16), nt, preferred_element_type=F32)
        q4 = jnp.concatenate([jnp.where(lane_head == h, qe, jnp.zeros_like(qe)) for h in range(nh)], axis=0)
        att = lax.dot_general(q4, ke, nt, preferred_element_type=F32)
        att_mask = (row4 >= col4) if fwd else (row4 < col4)
        o4 = jnp.dot(jnp.where(att_mask, att, 0.0).astype(BF16), vc, preferred_element_type=F32)
        for h in range(nh):
            o = o + jnp.where(out_head == h, o4[h * c:(h + 1) * c, :], 0.0)
        o_ref[sl, :] = o
        upd_t = lax.dot_general(vc, kd, tn, preferred_element_type=F32)
        s_scr[d] = st * jnp.exp(b_last) + jnp.where(bd_mask, upd_t, 0.0)

    def body(j, carry):
        one_chunk((qf_ref, kf_ref, vf_ref, laf_ref), of_ref, 0, pl.multiple_of(j * c, c))
        one_chunk((qb_ref, kb_ref, vb_ref, lab_ref), ob_ref, 1, pl.multiple_of((nchunk - 1 - j) * c, c))
        return carry

    lax.fori_loop(0, nchunk, body, 0, unroll=True)


def _gla(q, k, v, la):
    bsz, lt, _ = q.shape
    nblk = lt // TB
    fwd_map = lambda b, i: (b, i, 0)
    bwd_blk = lambda i: jnp.where(i == 0, 0, nblk - i)
    bwd_map = lambda b, i: (b, bwd_blk(i), 0)
    bwd_map_la = lambda b, i: (b, bwd_blk(i), 1)
    spec = lambda n, m: pl.BlockSpec((None, TB, n), m)
    return pl.pallas_call(
        _gla_kernel,
        grid=(bsz, nblk),
        in_specs=[spec(GLA_QK, fwd_map), spec(GLA_QK, fwd_map), spec(GLA_V, fwd_map), spec(GLA_QK, fwd_map),
                  spec(GLA_QK, bwd_map), spec(GLA_QK, bwd_map), spec(GLA_V, bwd_map), spec(GLA_QK, bwd_map_la)],
        out_specs=[spec(GLA_V, fwd_map), spec(GLA_V, bwd_map)],
        out_shape=[jax.ShapeDtypeStruct((bsz, lt, GLA_V), F32), jax.ShapeDtypeStruct((bsz, lt, GLA_V), F32)],
        scratch_shapes=[pltpu.VMEM((2, GLA_V, GLA_QK), F32)],
        compiler_params=_cparams(("parallel", "arbitrary")),
        name="gla_scan",
    )(q, k, v, la, q, k, v, la)


def _s5_kernel(u_ref, wm_ref, tm_ref, cm_ref, ar_ref, ai_ref, y_ref, w_scr, hp_scr, h_scr, *, rev):
    @pl.when(pl.program_id(1) == 0)
    def _():
        h_scr[...] = jnp.zeros_like(h_scr)

    bsz, ntok, _ = u_ref.shape
    nc = ntok // S5_CHUNK
    half = 8 * S5_P
    x = jnp.concatenate(
        [jnp.concatenate([u_ref[b, pl.ds(s, nc, stride=S5_CHUNK), :] for s in range(S5_CHUNK)], axis=1)
         for b in range(bsz)], axis=0).astype(BF16)
    w_scr[...] = jnp.dot(x, wm_ref[...], preferred_element_type=F32)
    ar, ai = ar_ref[...], ai_ref[...]

    def body(j, hs):
        c = (nc - 1 - j) if rev else j
        out = []
        for b in range(bsz):
            re, im = hs[b]
            r = b * nc + c
            hp_scr[pl.ds(r, 1), :] = jnp.concatenate([re, im], axis=1)
            w = w_scr[pl.ds(r, 1), :]
            out.append((ar * re - ai * im + w[:, :half], ar * im + ai * re + w[:, half:]))
        return tuple(out)

    hs = lax.fori_loop(0, nc, body, tuple((h_scr[b:b + 1, :half], h_scr[b:b + 1, half:]) for b in range(bsz)))
    for b in range(bsz):
        h_scr[b:b + 1, :] = jnp.concatenate(hs[b], axis=1)
    y = (jnp.dot(x, tm_ref[...], preferred_element_type=F32)
         + jnp.dot(hp_scr[...].astype(BF16), cm_ref[...], preferred_element_type=F32))
    for b in range(bsz):
        for s in range(S5_CHUNK):
            y_ref[b, pl.ds(s, nc, stride=S5_CHUNK), :] = y[b * nc:(b + 1) * nc, s * 128:(s + 1) * 128]


def _s5_mats(lam_re, lam_im, log_dt, b_re, b_im, c_re, c_im, rev):
    t16 = S5_CHUNK
    dt = jnp.exp(log_dt)[:, None]
    mag = jnp.exp(lam_re * dt)
    a_re, a_im = mag * jnp.cos(lam_im * dt), mag * jnp.sin(lam_im * dt)
    den = lam_re * lam_re + lam_im * lam_im
    nr = a_re - 1.0
    co_re = ((nr * lam_re + a_im * lam_im) / den)[..., None]
    co_im = ((a_im * lam_re - nr * lam_im) / den)[..., None]
    bb_re, bb_im = co_re * b_re - co_im * b_im, co_re * b_im + co_im * b_re
    pr, pi = [jnp.ones_like(a_re)], [jnp.zeros_like(a_im)]
    for _ in range(t16):
        pr, pi = pr + [pr[-1] * a_re - pi[-1] * a_im], pi + [pr[-1] * a_im + pi[-1] * a_re]
    pw_re, pw_im = jnp.stack(pr), jnp.stack(pi)
    g = lam_re.shape[0]
    e_re, e_im = pw_re[t16 - 1::-1][:t16], pw_im[t16 - 1::-1][:t16]
    wre = jnp.einsum('sgp,gph->gshp', e_re, bb_re) - jnp.einsum('sgp,gph->gshp', e_im, bb_im)
    wim = jnp.einsum('sgp,gph->gshp', e_re, bb_im) + jnp.einsum('sgp,gph->gshp', e_im, bb_re)
    cb_re = jnp.einsum('gkp,gph->gpkh', c_re, bb_re) - jnp.einsum('gkp,gph->gpkh', c_im, bb_im)
    cb_im = jnp.einsum('gkp,gph->gpkh', c_re, bb_im) + jnp.einsum('gkp,gph->gpkh', c_im, bb_re)
    kd = jnp.einsum('dgp,gpkh->dgkh', pw_re[:t16], cb_re) - jnp.einsum('dgp,gpkh->dgkh', pw_im[:t16], cb_im)
    lag = np.arange(t16)[None, :] - np.arange(t16)[:, None]
    toe = jnp.where((lag >= 0)[:, :, None, None, None], kd[np.clip(lag, 0, t16 - 1)], 0.0)
    toe = toe.transpose(2, 0, 4, 1, 3)
    q_re, q_im = pw_re[1:], pw_im[1:]
    ca_re = jnp.einsum('gkp,tgp->gptk', c_re, q_re) - jnp.einsum('gkp,tgp->gptk', c_im, q_im)
    ca_im = jnp.einsum('gkp,tgp->gptk', c_re, q_im) + jnp.einsum('gkp,tgp->gptk', c_im, q_re)
    if rev:
        wre, wim = wre[:, ::-1], wim[:, ::-1]
        toe = toe[:, ::-1, :, ::-1]
        ca_re, ca_im = ca_re[:, :, ::-1], ca_im[:, :, ::-1]
    nq, gl = g // 8, 8
    eye = jnp.eye(gl, dtype=F32)
    pack_w = lambda t: jnp.einsum('qgshp,gk->qsghkp', t.reshape(nq, gl, t16, S5_H, S5_P),
                                  eye).reshape(nq, t16 * 128, gl * S5_P)
    wm = jnp.concatenate([pack_w(wre), pack_w(wim)], axis=-1)
    tmat = jnp.einsum('qgshtk,gj->qsghtjk', toe.reshape(nq, gl, t16, S5_H, t16, S5_H),
                      eye).reshape(nq, t16 * 128, t16 * 128)
    pack_c = lambda t: jnp.einsum('qgptk,gj->qgptjk', t.reshape(nq, gl, S5_P, t16, S5_H),
                                  eye).reshape(nq, gl * S5_P, t16 * 128)
    cm = jnp.concatenate([pack_c(ca_re), -pack_c(ca_im)], axis=1)
    return (wm.astype(BF16), tmat.astype(BF16), cm.astype(BF16),
            pw_re[t16].reshape(nq, 1, gl * S5_P), pw_im[t16].reshape(nq, 1, gl * S5_P))


def _s5_scan(u, mats, rev):
    bsz, lt, _ = u.shape
    tblk = max(d for d in range(64, S5_TBLK + 1, 64) if lt % d == 0)
    nblk = lt // tblk
    nq = S5_G // 8
    wm, tmat, cm, ar, ai = mats
    kw, ks = S5_CHUNK * 128, 8 * S5_P
    tmap = (lambda q, t: (0, nblk - 1 - t, q)) if rev else (lambda q, t: (0, t, q))
    per = lambda shp: pl.BlockSpec((None,) + shp, lambda q, t: (q,) + tuple(0 for _ in shp))
    rows = bsz * tblk // S5_CHUNK
    return pl.pallas_call(
        functools.partial(_s5_kernel, rev=rev),
        grid=(nq, nblk),
        in_specs=[pl.BlockSpec((bsz, tblk, 128), tmap),
                  per((kw, 2 * ks)), per((kw, kw)), per((2 * ks, kw)), per((1, ks)), per((1, ks))],
        out_specs=pl.BlockSpec((bsz, tblk, 128), tmap),
        out_shape=jax.ShapeDtypeStruct((bsz, lt, S5_W), F32),
        scratch_shapes=[pltpu.VMEM((rows, 2 * ks), F32), pltpu.VMEM((rows, 2 * ks), F32),
                        pltpu.VMEM((8, 2 * ks), F32)],
        compiler_params=_cparams(("parallel", "arbitrary")),
        name="s5_scan",
    )(u, wm, tmat, cm, ar, ai)


def _mixer_tail(x, mix, mod_ref, wr_ref, br_ref, x_out, h2_out, lg_out):
    xn = x + mod_ref[2:3, :] * mix
    x_out[...] = xn
    h2 = _rms_rows(xn) * (1.0 + mod_ref[4:5, :]) + mod_ref[3:4, :]
    _to_token_tiles(h2_out, h2)
    lg_out[...] = _dot_x3(h2, wr_ref[...]) + br_ref[...]


def _ev_out_kernel(of_ref, ob_ref, g_ref, yf_ref, yb_ref, u_ref, ctx_ref, x_ref, mod_ref,
                   gn_ref, ds_ref, wglu_ref, wout_ref, wr_ref, br_ref,
                   x_out, h2_out, lg_out):
    o = of_ref[...] + ob_ref[...]
    og = jnp.concatenate([_rms_rows(o[:, h * GLA_DV:(h + 1) * GLA_DV]) for h in range(GLA_HEADS)], axis=1)
    g = g_ref[...]
    og = og * gn_ref[...] * (g * _sigmoid(g))
    t = yf_ref[...] + yb_ref[...] + ds_ref[...] * u_ref[...]
    y = t * (0.5 * (1.0 + jnp.tanh(math.sqrt(2.0 / math.pi) * (t + 0.044715 * (t * t * t)))))
    y = y * _sigmoid(jnp.dot(y.astype(BF16), wglu_ref[...], preferred_element_type=F32))
    cat = jnp.concatenate([og, y], axis=1).astype(BF16)
    mix = jnp.dot(cat, wout_ref[...], preferred_element_type=F32)
    _mixer_tail(_stream_tile(ctx_ref, x_ref), mix, mod_ref, wr_ref, br_ref, x_out, h2_out, lg_out)


def _ev_out(o_f, o_b, g, y_f, y_b, u, ctx, x, mods, gn, ds, wglu, wout, wr, br):
    bsz, lt, _ = u.shape
    nblk = lt // TB
    tok = lambda n: pl.BlockSpec((None, TB, n), lambda b, i: (b, i, 0))
    const = lambda shp: pl.BlockSpec(shp, lambda b, i: tuple(0 for _ in shp))
    return pl.pallas_call(
        _ev_out_kernel,
        grid=(bsz, nblk),
        in_specs=[tok(512), tok(512), tok(512), tok(512),
                  pl.BlockSpec((None, TB, S5_W), functools.partial(_swapped_index, nblk)),
                  tok(512)] + _stream_specs() + [
                  pl.BlockSpec((None, None, 6, D_MODEL), _mod_index),
                  const((1, 512)), const((1, 512)), const((512, 512)), const((D_MODEL, D_MODEL)),
                  const((D_MODEL, 128)), const((1, 128))],
        out_specs=[tok(D_MODEL), pl.BlockSpec((TB * ROW_TILE, 128), lambda b, i: (b * nblk + i, 0)), tok(128)],
        out_shape=[jax.ShapeDtypeStruct((bsz, lt, D_MODEL), F32),
                   jax.ShapeDtypeStruct((bsz * lt * ROW_TILE, 128), F32),
                   jax.ShapeDtypeStruct((bsz, lt, 128), F32)],
        compiler_params=_cparams(("parallel", "parallel")),
        name="ev_out",
    )(o_f, o_b, g, y_f, y_b, u, ctx, x, mods, gn, ds, wglu, wout, wr, br)


def _route(logits):
    n_tok = logits.shape[0]
    lg = logits[:, :N_GROUPS]
    grp = jnp.argmax(lg, axis=-1)
    g_w = jnp.max(jax.nn.softmax(lg, axis=-1), axis=-1, keepdims=True)
    le = logits[:, N_GROUPS:N_GROUPS + N_EXPERTS].reshape(n_tok, N_GROUPS, EXP_PER_GROUP)
    le = le[jnp.arange(n_tok), grp]
    top_p, top_i = lax.top_k(jax.nn.softmax(le, axis=-1), TOP_K)
    gate = g_w * top_p / jnp.sum(top_p, axis=-1, keepdims=True)
    eid = (grp[:, None] * EXP_PER_GROUP + top_i).reshape(-1)
    n_asg = n_tok * TOP_K
    order = jnp.argsort(eid).astype(jnp.int32)
    counts = jnp.sum((eid[:, None] == jnp.arange(N_EXPERTS)[None, :]).astype(jnp.int32), axis=0)
    padded = (counts + MOE_BLOCK - 1) // MOE_BLOCK * MOE_BLOCK
    pad_end = jnp.cumsum(padded)
    pad_start = pad_end - padded
    cnt_start = jnp.cumsum(counts) - counts
    n_blocks = -(-n_asg // MOE_BLOCK) + N_EXPERTS
    blk_start = jnp.arange(n_blocks, dtype=jnp.int32) * MOE_BLOCK
    blk_e = jnp.minimum(jnp.sum((pad_end[None, :] <= blk_start[:, None]).astype(jnp.int32), axis=1), N_EXPERTS - 1)
    pos = jnp.arange(n_blocks * MOE_BLOCK, dtype=jnp.int32)
    pos_e = jnp.repeat(blk_e, MOE_BLOCK)
    rank = pos - pad_start[pos_e]
    src = jnp.clip(cnt_start[pos_e] + rank, 0, n_asg - 1)
    slot_buf = jnp.where(rank < counts[pos_e], order[src], n_asg).astype(jnp.int32)
    n_valid = jnp.sum((slot_buf < n_asg).reshape(n_blocks, MOE_BLOCK), axis=1).astype(jnp.int32)
    return slot_buf, blk_e, n_valid, gate.astype(F32)


def _moe_kernel(slot_ref, blke_ref, nvalid_ref, h_hbm, w1_ref, w3_ref, w2_ref, z_hbm,
                xbuf, ybuf, wb1, wb3, wb2, gsem, ssem, *, n_tok):
    i = pl.program_id(0)
    nblk = pl.num_programs(0)
    cur = i % 2

    ns = ROW_TILE

    def gather_row(blk, buf, r, priority):
        asg = slot_ref[blk * MOE_BLOCK + r]
        tok = jnp.minimum(lax.shift_right_logical(asg, 1), n_tok - 1)
        pltpu.make_async_copy(h_hbm.at[pl.ds(pl.multiple_of(tok * ns, ns), ns)],
                              xbuf.at[buf, pl.ds(pl.multiple_of(r * ns, ns), ns)],
                              gsem.at[buf]).start(priority=priority)

    def scatter_row(blk, buf, r, priority):
        asg = slot_ref[blk * MOE_BLOCK + r]
        row = (asg & 1) * n_tok + lax.shift_right_logical(asg, 1)
        pltpu.make_async_copy(ybuf.at[buf, pl.ds(pl.multiple_of(r * ns, ns), ns)],
                              z_hbm.at[pl.ds(pl.multiple_of(row * ns, ns), ns)],
                              ssem.at[buf]).start(priority=priority)

    def all_rows(row_fn, blk, buf):
        def body(r8, carry):
            for j in range(MOE_ISSUE_UNROLL):
                row_fn(blk, buf, r8 * MOE_ISSUE_UNROLL + j, j % 2)
            return carry
        lax.fori_loop(0, MOE_BLOCK // MOE_ISSUE_UNROLL, body, 0)

    def gather(blk, buf):
        all_rows(gather_row, blk, buf)

    def wait_gather(buf):
        pltpu.make_async_copy(h_hbm.at[pl.ds(0, MOE_BLOCK * ns)], xbuf.at[buf], gsem.at[buf]).wait()

    def scatter(blk, buf):
        n = nvalid_ref[blk]

        @pl.when(n == MOE_BLOCK)
        def _():
            all_rows(scatter_row, blk, buf)

        @pl.when(n < MOE_BLOCK)
        def _():
            def body(r, carry):
                scatter_row(blk, buf, r, 0)
                return carry
            lax.fori_loop(0, n, body, 0)

    def wait_scatter(blk, buf):
        n = nvalid_ref[blk]

        @pl.when(n == MOE_BLOCK)
        def _():
            pltpu.make_async_copy(ybuf.at[buf], z_hbm.at[pl.ds(0, MOE_BLOCK * ns)], ssem.at[buf]).wait()

        @pl.when(n < MOE_BLOCK)
        def _():
            def body(r, carry):
                pltpu.make_async_copy(ybuf.at[buf, pl.ds(0, ns)], z_hbm.at[pl.ds(0, ns)], ssem.at[buf]).wait()
                return carry
            lax.fori_loop(0, n, body, 0)

    used = nvalid_ref[i] > 0

    @pl.when(jnp.logical_and(i == 0, used))
    def _():
        gather(0, 0)

    @pl.when(used)
    def _():
        wait_gather(cur)

        @pl.when(jnp.logical_and(i + 1 < nblk, nvalid_ref[jnp.minimum(i + 1, nblk - 1)] > 0))
        def _():
            gather(i + 1, 1 - cur)

        x = _from_token_tiles(xbuf.at[cur], MOE_BLOCK).astype(BF16)
        @pl.when(jnp.logical_or(i == 0, blke_ref[i] != blke_ref[jnp.maximum(i - 1, 0)]))
        def _():
            wb1[...] = w1_ref[...].astype(BF16)
            wb3[...] = w3_ref[...].astype(BF16)
            wb2[...] = w2_ref[...].astype(BF16)

        h1 = jnp.dot(x, wb1[...], preferred_element_type=F32)
        h3 = jnp.dot(x, wb3[...], preferred_element_type=F32)
        a = (h1 * _sigmoid(h1) * h3).astype(BF16)
        y = jnp.dot(a, wb2[...], preferred_element_type=F32)

        @pl.when(i >= 2)
        def _():
            wait_scatter(i - 2, cur)

        _to_token_tiles(ybuf.at[cur], y)
        scatter(i, cur)

    first_unused = jnp.logical_and(jnp.logical_not(used),
                                   jnp.logical_and(i > 0, nvalid_ref[jnp.maximum(i - 1, 0)] > 0))
    last_used = jnp.logical_and(used, i == nblk - 1)

    def drain(last):
        wait_scatter(last, last % 2)

        @pl.when(last >= 1)
        def _():
            wait_scatter(last - 1, (last - 1) % 2)

    @pl.when(first_unused)
    def _():
        drain(i - 1)

    @pl.when(last_used)
    def _():
        drain(i)


def _moe_experts(h2, slot_buf, blk_e, n_valid, w1, w3, w2, layer):
    n_tok = h2.shape[0] // ROW_TILE
    n_blocks = blk_e.shape[0]
    wspec = lambda shp: pl.BlockSpec((None, None) + shp, lambda i, slot, blke, nvalid: (layer, blke[i], 0, 0))
    grid_spec = pltpu.PrefetchScalarGridSpec(
        num_scalar_prefetch=3,
        grid=(n_blocks,),
        in_specs=[pl.BlockSpec(memory_space=pl.ANY),
                  wspec((D_MODEL, D_EXPERT)), wspec((D_MODEL, D_EXPERT)), wspec((D_EXPERT, D_MODEL))],
        out_specs=pl.BlockSpec(memory_space=pl.ANY),
        scratch_shapes=[pltpu.VMEM((2, MOE_BLOCK * ROW_TILE, 128), F32), pltpu.VMEM((2, MOE_BLOCK * ROW_TILE, 128), F32),
                        pltpu.VMEM((D_MODEL, D_EXPERT), BF16), pltpu.VMEM((D_MODEL, D_EXPERT), BF16),
                        pltpu.VMEM((D_EXPERT, D_MODEL), BF16),
                        pltpu.SemaphoreType.DMA((2,)), pltpu.SemaphoreType.DMA((2,))])
    return pl.pallas_call(
        functools.partial(_moe_kernel, n_tok=n_tok),
        grid_spec=grid_spec,
        out_shape=jax.ShapeDtypeStruct((TOP_K * n_tok * ROW_TILE, 128), F32),
        compiler_params=_cparams(("arbitrary",)),
        name="moe_experts",
    )(slot_buf, blk_e, n_valid, h2, w1, w3, w2)


def _moe_combine_kernel(x_ref, z0_ref, z1_ref, gate_ref, mod_ref, o_ref):
    gate = gate_ref[...]
    tb = x_ref.shape[0]
    y = gate[:, 0:1] * _from_token_tiles(z0_ref, tb) + gate[:, 1:2] * _from_token_tiles(z1_ref, tb)
    o_ref[...] = x_ref[...] + mod_ref[5:6, :] * y


def _moe_combine(x, z, gate, mods, mod_index):
    bsz, lt, _ = x.shape
    nblk = lt // TB
    gate3 = gate.reshape(bsz, lt, TOP_K)
    return pl.pallas_call(
        _moe_combine_kernel,
        grid=(bsz, nblk),
        in_specs=[pl.BlockSpec((None, TB, D_MODEL), lambda b, i: (b, i, 0)),
                  pl.BlockSpec((TB * ROW_TILE, 128), lambda b, i: (b * nblk + i, 0)),
                  pl.BlockSpec((TB * ROW_TILE, 128), lambda b, i: ((bsz + b) * nblk + i, 0)),
                  pl.BlockSpec((None, TB, TOP_K), lambda b, i: (b, i, 0)),
                  pl.BlockSpec((None, None, 6, D_MODEL), mod_index)],
        out_specs=pl.BlockSpec((None, TB, D_MODEL), lambda b, i: (b, i, 0)),
        out_shape=jax.ShapeDtypeStruct((bsz, lt, D_MODEL), F32),
        compiler_params=_cparams(("parallel", "parallel")),
        name="moe_combine",
    )(x, z, z, gate3, mods)


def _moe(x, h2, logits, mods, mod_index, w1, w3, w2, layer):
    bsz, lt, _ = x.shape
    slot_buf, blk_e, n_valid, gate = _route(logits.reshape(bsz * lt, 128))
    z = _moe_experts(h2, slot_buf, blk_e, n_valid, w1, w3, w2, layer)
    return _moe_combine(x, z, gate, mods, mod_index)


def _od_proj_kernel(x_ref, mod_ref, w_ref, gm_ref, qn_ref, kn_ref, q_ref, k_ref, v_ref, zh_ref):
    tm = x_ref.shape[0]
    is_ctx = pl.program_id(1) * tm + lax.broadcasted_iota(jnp.int32, (tm, 1), 0) < TB
    shift = jnp.where(is_ctx, mod_ref[0, 0:1, :], mod_ref[1, 0:1, :])
    scale = jnp.where(is_ctx, mod_ref[0, 1:2, :], mod_ref[1, 1:2, :])
    h = _rms_rows(x_ref[...]) * (1.0 + scale) + shift
    z = jnp.dot(h.astype(BF16), w_ref[...], preferred_element_type=F32)

    def head_norm(t, gain):
        sq_hi, sq_lo = _split_bf16(t * t)
        gm = gm_ref[...].astype(BF16)
        ms = jnp.dot(sq_hi, gm, preferred_element_type=F32) + jnp.dot(sq_lo, gm, preferred_element_type=F32)
        return t * lax.rsqrt(ms + EPS) * gain

    q_ref[...] = (head_norm(z[:, :NA_W], qn_ref[...]) * (NA_DH ** -0.5)).astype(BF16)
    k_ref[...] = head_norm(z[:, NA_W:2 * NA_W], kn_ref[...]).astype(BF16)
    v_ref[...] = z[:, 2 * NA_W:3 * NA_W].astype(BF16)
    zh_ref[...] = z[:, 3 * NA_W:]


def _od_proj(xcat, mods, w, gm, qn, kn):
    bsz, lt, _ = xcat.shape
    tm = max(d for d in range(TB, PROJ_TILE + 1, TB) if lt % d == 0)
    nblk = lt // tm
    tok = lambda n: pl.BlockSpec((None, tm, n), lambda b, i: (b, i, 0))
    const = lambda shp: pl.BlockSpec(shp, lambda b, i: tuple(0 for _ in shp))
    return pl.pallas_call(
        _od_proj_kernel,
        grid=(bsz, nblk),
        in_specs=[tok(D_MODEL), pl.BlockSpec((None, 2, 6, D_MODEL), lambda b, i: (b, 0, 0, 0)),
                  const((D_MODEL, 3 * NA_W + 3 * HY_W)), const((NA_W, NA_W)), const((1, NA_W)), const((1, NA_W))],
        out_specs=[tok(NA_W), tok(NA_W), tok(NA_W), tok(3 * HY_W)],
        out_shape=[jax.ShapeDtypeStruct((bsz, lt, NA_W), BF16), jax.ShapeDtypeStruct((bsz, lt, NA_W), BF16),
                   jax.ShapeDtypeStruct((bsz, lt, NA_W), BF16), jax.ShapeDtypeStruct((bsz, lt, 3 * HY_W), F32)],
        compiler_params=_cparams(("parallel", "parallel")),
        name="od_proj",
    )(xcat, mods, w, gm, qn, kn)


def _na_kernel(q_ref, k_ref, v_ref, t2_ref, o_ref):
    r = pl.program_id(1)
    n_rows = pl.num_programs(1)
    r0 = jnp.clip(r - WIN_R // 2, 0, n_rows - WIN_R)
    off = r0 - r + WIN_R - 1
    base = pl.multiple_of(TB + r0 * GRID_W, GRID_W)
    nloc = WIN_R * GRID_W
    q = q_ref[...]
    hg = NA_HG
    gw = hg * NA_DH
    lane_head = lax.broadcasted_iota(jnp.int32, (GRID_W, gw), 1) // NA_DH
    nt = (((1,), (1,)), ((), ()))
    outs = []
    for grp in range(NA_HEADS // hg):
        cs = slice(gw * grp, gw * (grp + 1))
        q2 = q[:, cs]
        q4 = jnp.concatenate([jnp.where(lane_head == h, q2, jnp.zeros_like(q2)) for h in range(hg)], axis=0)
        kw, vw = k_ref[pl.ds(base, nloc), cs], v_ref[pl.ds(base, nloc), cs]
        kc, vc = k_ref[0:TB, cs], v_ref[0:TB, cs]
        bias = jnp.concatenate(
            [jnp.concatenate([t2_ref[hg * grp + h, off + 2 * m] for m in range(WIN_R // 2)], axis=1)
             for h in range(hg)], axis=0)
        s_loc = lax.dot_general(q4, kw, nt, preferred_element_type=F32) + bias
        s_ctx = lax.dot_general(q4, kc, nt, preferred_element_type=F32)
        m = jnp.maximum(jnp.max(s_loc, axis=1, keepdims=True), jnp.max(s_ctx, axis=1, keepdims=True))
        p_loc, p_ctx = jnp.exp(s_loc - m), jnp.exp(s_ctx - m)
        den = jnp.sum(p_loc, axis=1, keepdims=True) + jnp.sum(p_ctx, axis=1, keepdims=True)
        o4 = (jnp.dot(p_loc.astype(BF16), vw, preferred_element_type=F32)
              + jnp.dot(p_ctx.astype(BF16), vc, preferred_element_type=F32)) / den
        acc = jnp.zeros((GRID_W, gw), F32)
        for h in range(hg):
            acc = jnp.where(lane_head == h, o4[h * GRID_W:(h + 1) * GRID_W, :], acc)
        outs.append(acc)
    o_ref[...] = jnp.concatenate(outs, axis=1)


def _na_bias_table(rpb):
    qc = np.arange(GRID_W)[:, None]
    kc = np.arange(GRID_W)[None, :]
    q_start = np.clip(qc - WIN_C // 2, 0, GRID_W - WIN_C)
    valid = (kc >= q_start) & (kc < q_start + WIN_C)
    col_idx = np.clip(kc - qc + WIN_C - 1, 0, 2 * WIN_C - 2)
    t = jnp.where(valid[None, None], rpb.astype(F32)[:, :, col_idx], NEG_INF)
    return jnp.concatenate([t[:, :-1], t[:, 1:]], axis=-1)


def _na(q, k, v, t2):
    bsz, lt, _ = q.shape
    n_rows = (lt - TB) // GRID_W
    qoff = TB // GRID_W
    return pl.pallas_call(
        _na_kernel,
        grid=(bsz, n_rows),
        in_specs=[pl.BlockSpec((None, GRID_W, NA_W), lambda b, r: (b, r + qoff, 0)),
                  pl.BlockSpec((None, lt, NA_W), lambda b, r: (b, 0, 0)),
                  pl.BlockSpec((None, lt, NA_W), lambda b, r: (b, 0, 0)),
                  pl.BlockSpec(t2.shape, lambda b, r: (0, 0, 0, 0))],
        out_specs=pl.BlockSpec((None, GRID_W, NA_W), lambda b, r: (b, r, 0)),
        out_shape=jax.ShapeDtypeStruct((bsz, lt - TB, NA_W), F32),
        compiler_params=_cparams(("parallel", "arbitrary")),
        name="na_attn",
    )(q, k, v, t2)


def _hy_pre_kernel(z_ref, zp_ref, zn_ref, cw_ref, cb_ref, x0_ref, u_ref, ut_ref):
    i = pl.program_id(1)
    n = pl.num_programs(1)
    z = z_ref[...]
    tb = z.shape[0]
    prev_row = jnp.where(i > 0, zp_ref[7:8, :], 0.0)
    next_row = jnp.where(i < n - 1, zn_ref[0:1, :], 0.0)
    rowid = lax.broadcasted_iota(jnp.int32, z.shape, 0)
    zm = jnp.where(rowid == 0, prev_row, pltpu.roll(z, 1, 0))
    zp = jnp.where(rowid == tb - 1, next_row, pltpu.roll(z, tb - 1, 0))
    zc = cb_ref[...] + cw_ref[0:1, :] * zm
    zc = zc + cw_ref[1:2, :] * z
    zc = zc + cw_ref[2:3, :] * zp
    x0_ref[...] = zc[:, :HY_W]
    u = zc[:, HY_W:2 * HY_W] * zc[:, 2 * HY_W:]
    u_ref[...] = u
    for j in range(tb // FFT_N1):
        ut_ref[j] = u[j * FFT_N1:(j + 1) * FFT_N1, :].T.astype(BF16)


def _hy_pre(zh, cw, cb):
    bsz, lt, _ = zh.shape
    l = lt - TB
    nblk = l // TB
    h8 = TB // 8
    return pl.pallas_call(
        _hy_pre_kernel,
        grid=(bsz, nblk),
        in_specs=[pl.BlockSpec((None, TB, 3 * HY_W), lambda b, i: (b, i + 1, 0)),
                  pl.BlockSpec((None, 8, 3 * HY_W), lambda b, i: (b, (i + 1) * h8 - 1, 0)),
                  pl.BlockSpec((None, 8, 3 * HY_W), lambda b, i: (b, jnp.minimum((i + 2) * h8, lt // 8 - 1), 0)),
                  pl.BlockSpec((HY_SHORT, 3 * HY_W), lambda b, i: (0, 0)),
                  pl.BlockSpec((1, 3 * HY_W), lambda b, i: (0, 0))],
        out_specs=[pl.BlockSpec((None, TB, HY_W), lambda b, i: (b, i, 0)),
                   pl.BlockSpec((None, TB, HY_W), lambda b, i: (b, i, 0)),
                   pl.BlockSpec((None, TB // FFT_N1, HY_W, FFT_N1), lambda b, i: (b, i, 0, 0))],
        out_shape=[jax.ShapeDtypeStruct((bsz, l, HY_W), F32), jax.ShapeDtypeStruct((bsz, l, HY_W), F32),
                   jax.ShapeDtypeStruct((bsz, l // FFT_N1, HY_W, FFT_N1), BF16)],
        compiler_params=_cparams(("parallel", "parallel")),
        name="hy_pre",
    )(zh, zh, zh, cw, cb)


def _fft_consts(n1_in):
    n = FFT_N1
    idx = np.arange(n)
    ang1 = 2.0 * np.pi * np.outer(idx, idx) / n
    c, s = np.cos(ang1), np.sin(ang1)
    angt = 2.0 * np.pi * np.outer(idx, idx) / (n * n)
    tw = np.concatenate([np.cos(angt), -np.sin(angt)], axis=1)
    f3 = np.block([[c, -s], [s, c]])
    f3i = np.block([[c, s], [-s, c]])
    ch, sh = c[:, :n1_in], s[:, :n1_in]
    f1_pair = np.block([[ch, sh], [-sh, ch]])
    f1_real = np.concatenate([c, -s], axis=0)
    f1i = np.block([[ch.T, -sh.T], [sh.T, ch.T]]) / (n * n)
    return tw, f3, f3i, f1_pair, f1_real, f1i


def _fft_forward(a, tw_re, tw_im, lhs_scr, ncg):
    for cix in range(ncg):
        cs = slice(cix * FFT_N1, (cix + 1) * FFT_N1)
        are, aim = a[:FFT_N1, cs], a[FFT_N1:, cs]
        lhs_scr[cs, :FFT_N1] = are * tw_re - aim * tw_im
        lhs_scr[cs, FFT_N1:] = are * tw_im + aim * tw_re


def _hy_filt_kernel(k_ref, f1_ref, tw_ref, f3_ref, o_ref, lhs_scr):
    a = _dot_x3(f1_ref[...], k_ref[...])
    _fft_forward(a, tw_ref[:, :FFT_N1], tw_ref[:, FFT_N1:], lhs_scr, FFT_CG)
    o_ref[...] = _dot_x3(lhs_scr[...], f3_ref[...])


def _hy_fft_kernel(u_ref, kf_ref, f1_ref, tw_ref, f3_ref, f3i_ref, f1i_ref, y_ref, lhs_scr, a2_scr):
    cgl = FFT_CG * FFT_N1
    x = u_ref[...].reshape(2 * u_ref.shape[1], cgl)
    a = jnp.dot(f1_ref[...], x, preferred_element_type=F32)
    tw_re, tw_im = tw_ref[:, :FFT_N1], tw_ref[:, FFT_N1:]
    _fft_forward(a, tw_re, tw_im, lhs_scr, FFT_CG)
    y = jnp.dot(lhs_scr[...].astype(BF16), f3_ref[...], preferred_element_type=F32)
    yre, yim = y[:, :FFT_N1], y[:, FFT_N1:]
    kre, kim = kf_ref[:, :FFT_N1], kf_ref[:, FFT_N1:]
    z = jnp.concatenate([yre * kre - yim * kim, yre * kim + yim * kre], axis=1).astype(BF16)
    bp = jnp.dot(z, f3i_ref[...], preferred_element_type=F32)
    for cix in range(FFT_CG):
        cs = slice(cix * FFT_N1, (cix + 1) * FFT_N1)
        bre, bim = bp[cs, :FFT_N1], bp[cs, FFT_N1:]
        a2_scr[:FFT_N1, cs] = bre * tw_re + bim * tw_im
        a2_scr[FFT_N1:, cs] = bim * tw_re - bre * tw_im
    out = jnp.dot(f1i_ref[...], a2_scr[...].astype(BF16), preferred_element_type=F32)
    y_ref[...] = out.reshape(2, u_ref.shape[1], cgl)


def _hy_conv(ut, kfilt):
    bsz, n1h, nch, _ = ut.shape
    assert 2 * n1h == FFT_N1 and bsz % 2 == 0
    cgl = FFT_CG * FFT_N1
    ncol = nch * FFT_N1
    tw, f3, f3i, f1_pair, f1_real, f1i = _fft_consts(n1h)
    kt = kfilt.reshape(FFT_N1, FFT_N1, nch).transpose(0, 2, 1).reshape(FFT_N1, ncol)
    const2 = lambda shp: pl.BlockSpec(shp, lambda *a: (0, 0))
    kf = pl.pallas_call(
        _hy_filt_kernel,
        grid=(nch // FFT_CG,),
        in_specs=[pl.BlockSpec((FFT_N1, cgl), lambda j: (0, j)), const2((2 * FFT_N1, FFT_N1)),
                  const2((FFT_N1, 2 * FFT_N1)), const2((2 * FFT_N1, 2 * FFT_N1))],
        out_specs=pl.BlockSpec((cgl, 2 * FFT_N1), lambda j: (j, 0)),
        out_shape=jax.ShapeDtypeStruct((ncol, 2 * FFT_N1), F32),
        scratch_shapes=[pltpu.VMEM((cgl, 2 * FFT_N1), F32)],
        compiler_params=_cparams(("parallel",)),
        name="hy_filter_dft",
    )(kt, jnp.asarray(f1_real, F32), jnp.asarray(tw, F32), jnp.asarray(f3, F32))
    u2 = ut.reshape(bsz, n1h, ncol)
    y = pl.pallas_call(
        _hy_fft_kernel,
        grid=(bsz // 2, nch // FFT_CG),
        in_specs=[pl.BlockSpec((2, n1h, cgl), lambda p, j: (p, 0, j)),
                  pl.BlockSpec((cgl, 2 * FFT_N1), lambda p, j: (j, 0)),
                  const2((2 * FFT_N1, FFT_N1)), const2((FFT_N1, 2 * FFT_N1)),
                  const2((2 * FFT_N1, 2 * FFT_N1)), const2((2 * FFT_N1, 2 * FFT_N1)), const2((FFT_N1, 2 * FFT_N1))],
        out_specs=pl.BlockSpec((2, n1h, cgl), lambda p, j: (p, 0, j)),
        out_shape=jax.ShapeDtypeStruct((bsz, n1h, ncol), F32),
        scratch_shapes=[pltpu.VMEM((cgl, 2 * FFT_N1), F32), pltpu.VMEM((2 * FFT_N1, cgl), F32)],
        compiler_params=_cparams(("parallel", "parallel")),
        name="hy_fft_conv",
    )(u2, kf, jnp.asarray(f1_pair, F32).astype(BF16), jnp.asarray(tw, F32), jnp.asarray(f3, F32).astype(BF16),
      jnp.asarray(f3i, F32).astype(BF16), jnp.asarray(f1i, F32).astype(BF16))
    return y.reshape(bsz, n1h, nch, FFT_N1)


def _hy_filter(seqlen, fw1, fb1, fw2, fb2, fw3, fb3, freq, fw4):
    t = jnp.linspace(0.0, 1.0, seqlen, dtype=F32)[:, None]
    bands = (HY_EMB - 1) // 2
    w = 2.0 * math.pi * jnp.arange(seqlen, dtype=F32)[:, None] / seqlen
    f = jnp.linspace(1e-4, bands - 1, bands, dtype=F32)[None, :]
    feat = jnp.concatenate([t, jnp.cos(f * w), -jnp.sin(f * w)], axis=-1)
    feat2 = jnp.concatenate([feat, feat[::-1]], axis=0)
    h = _mm(feat2, fw1, fb1, freq)
    h = _mm(h, fw2, fb2, freq)
    h = _mm(h, fw3, fb3, freq)
    deltas = jnp.abs(jnp.linspace(math.log(HY_DECAY_TARGET) / HY_DECAY_LONG_PCT,
                                  math.log(HY_DECAY_TARGET) / HY_DECAY_SHORT_PCT, HY_W, dtype=F32))
    hf = _mm(h[:seqlen], fw4[:, :HY_W]) * jnp.exp(-t * deltas)
    hb = _mm(h[seqlen:], fw4[:, HY_W:]) * jnp.exp(-t[::-1] * deltas)
    return jnp.concatenate([hf, jnp.zeros((1, HY_W), F32), hb[:seqlen - 1]], axis=0)


def _od_out_kernel(na_ref, yt_ref, x0_ref, u_ref, x_ref, mod_ref, hb_ref, wout_ref, wr_ref, br_ref,
                   x_out, h2_out, lg_out):
    y = jnp.concatenate([yt_ref[j].T for j in range(yt_ref.shape[0])], axis=0)
    hy = x0_ref[...] * (y + u_ref[...] * hb_ref[...])
    cat = jnp.concatenate([na_ref[...], hy], axis=1).astype(BF16)
    mix = jnp.dot(cat, wout_ref[...], preferred_element_type=F32)
    _mixer_tail(x_ref[...], mix, mod_ref, wr_ref, br_ref, x_out, h2_out, lg_out)


def _latent_mod_index(b, i):
    return (b, 1, 0, 0)


def _od_out(na, yt, x0, u, xcat, mods, hb, wout, wr, br):
    bsz, l, _ = na.shape
    nblk = l // TB
    tok = lambda n: pl.BlockSpec((None, TB, n), lambda b, i: (b, i, 0))
    const = lambda shp: pl.BlockSpec(shp, lambda b, i: tuple(0 for _ in shp))
    return pl.pallas_call(
        _od_out_kernel,
        grid=(bsz, nblk),
        in_specs=[tok(NA_W), pl.BlockSpec((None, TB // FFT_N1, HY_W, FFT_N1), lambda b, i: (b, i, 0, 0)),
                  tok(HY_W), tok(HY_W),
                  pl.BlockSpec((None, TB, D_MODEL), lambda b, i: (b, i + 1, 0)),
                  pl.BlockSpec((None, None, 6, D_MODEL), _latent_mod_index),
                  const((1, HY_W)), const((D_MODEL, D_MODEL)), const((D_MODEL, 128)), const((1, 128))],
        out_specs=[tok(D_MODEL), pl.BlockSpec((TB * ROW_TILE, 128), lambda b, i: (b * nblk + i, 0)), tok(128)],
        out_shape=[jax.ShapeDtypeStruct((bsz, l, D_MODEL), F32), jax.ShapeDtypeStruct((bsz * l * ROW_TILE, 128), F32),
                   jax.ShapeDtypeStruct((bsz, l, 128), F32)],
        compiler_params=_cparams(("parallel", "parallel")),
        name="od_out",
    )(na, yt, x0, u, xcat, mods, hb, wout, wr, br)


def _mods(c, c_ctx, ada_w, ada_b):
    bsz = c.shape[0]
    depth, _, n = ada_w.shape
    cc = jnp.concatenate([c, c_ctx[None]], axis=0)
    a = jnp.pad(cc * _sigmoid(cc), ((0, 8 - (bsz + 1) % 8), (0, 0)))
    mp, tn = a.shape[0], 1024
    m = pl.pallas_call(
        functools.partial(_mm_kernel, use_sin=False),
        grid=(depth, n // tn),
        in_specs=[pl.BlockSpec((mp, D_MODEL), lambda l, j: (0, 0)),
                  pl.BlockSpec((None, D_MODEL, tn), lambda l, j: (l, 0, j)),
                  pl.BlockSpec((None, 1, tn), lambda l, j: (l, 0, j)),
                  pl.BlockSpec((1, tn), lambda l, j: (0, 0))],
        out_specs=pl.BlockSpec((None, mp, tn), lambda l, j: (l, 0, j)),
        out_shape=jax.ShapeDtypeStruct((depth, mp, n), F32),
        compiler_params=_cparams(("parallel", "parallel")),
        name="adaln_dense",
    )(a, ada_w, ada_b.reshape(depth, 1, n), jnp.ones((1, tn), F32))
    mod_l = m[:, :bsz].reshape(depth, bsz, 1, 6, D_MODEL)
    mod_c = jnp.broadcast_to(m[:, bsz].reshape(depth, 1, 1, 6, D_MODEL), (depth, bsz, 1, 6, D_MODEL))
    return jnp.concatenate([mod_c, mod_l], axis=2)


def _rope_tables(seqlen):
    pos = jnp.arange(seqlen)
    half = GLA_DK // 4
    freqs = ROPE_BASE ** (-jnp.arange(half, dtype=F32) / half)
    ar = (pos // GRID_W).astype(F32)[:, None] * freqs
    ac = (pos % GRID_W).astype(F32)[:, None] * freqs
    cos = jnp.concatenate([jnp.cos(ar), jnp.cos(ar), jnp.cos(ac), jnp.cos(ac)], axis=1)
    sin = jnp.concatenate([-jnp.sin(ar), jnp.sin(ar), -jnp.sin(ac), jnp.sin(ac)], axis=1)
    cos = jnp.concatenate([jnp.ones((TB, GLA_DK), F32), cos], axis=0)
    sin = jnp.concatenate([jnp.zeros((TB, GLA_DK), F32), sin], axis=0)
    return jnp.tile(cos, (1, GLA_HEADS)), jnp.tile(sin, (1, GLA_HEADS))


def _router_weights(wg, bg, we, be):
    pad = 128 - N_GROUPS - N_EXPERTS
    wr = jnp.concatenate([wg, we, jnp.zeros((D_MODEL, pad), F32)], axis=1)
    br = jnp.concatenate([bg, be, jnp.zeros((pad,), F32)]).reshape(1, 128)
    return wr, br


def kernel(x, c, ctx, c_ctx, ada_w, ada_b, moe_wg, moe_bg, moe_we, moe_be, moe_w1, moe_w3, moe_w2, ev_w_in, ev_w_out, gla_wa2, gla_ba, gla_norm, s5_lam_re, s5_lam_im, s5_log_dt, s5_b_re, s5_b_im, s5_c_re, s5_c_im, s5_d, s5_w_glu, od_w_in, od_w_out, na_q_norm, na_k_norm, na_rpb, hy_conv_w, hy_conv_b, hy_fw1, hy_fb1, hy_fw2, hy_fb2, hy_fw3, hy_fb3, hy_freq, hy_fw4, hy_bias):
    bsz, seqlen, _ = x.shape
    assert ctx.shape[1] == TB and seqlen % TB == 0

    mods_all = _mods(c, c_ctx, ada_w, ada_b)
    mods = mods_all[0]
    w_in = ev_w_in[0]
    n_a = 2 * GLA_RANK
    a0 = 2 * GLA_QK + 2 * GLA_V
    w_ev = jnp.concatenate([w_in[:, :a0], w_in[:, a0 + n_a:], w_in[:, a0:a0 + n_a],
                            jnp.zeros((D_MODEL, 128 - n_a), F32)], axis=1).astype(BF16)
    wa = jnp.zeros((128, 2 * GLA_QK), F32)
    for d in range(2):
        wa = wa.at[d * GLA_RANK:(d + 1) * GLA_RANK, d * GLA_QK:(d + 1) * GLA_QK].set(gla_wa2[0, d])
    cos, sin = _rope_tables(seqlen)
    q, k, v, g, u, u_sw, la = _ev_proj(ctx, x, mods, w_ev, wa, gla_ba[0].reshape(1, 2 * GLA_QK), cos, sin)
    o_f, o_b = _gla(q, k, v, la)
    s5p = [t[0].astype(F32) for t in (s5_lam_re, s5_lam_im, s5_log_dt, s5_b_re, s5_b_im, s5_c_re, s5_c_im)]
    y_f = _s5_scan(u, _s5_mats(*[t[0] for t in s5p], rev=False), rev=False)
    y_b = _s5_scan(u_sw, _s5_mats(*[t[1] for t in s5p], rev=True), rev=True)
    wr, br = _router_weights(moe_wg[0], moe_bg[0], moe_we[0], moe_be[0])
    x1, h2, lg = _ev_out(o_f, o_b, g, y_f, y_b, u, ctx, x, mods,
                         jnp.tile(gla_norm[0], GLA_HEADS).reshape(1, GLA_V), s5_d[0].reshape(1, S5_W),
                         s5_w_glu[0].astype(BF16), ev_w_out[0].astype(BF16), wr, br)
    xcat = _moe(x1, h2, lg, mods, _mod_index,
                moe_w1, moe_w3, moe_w2, 0)

    mods = mods_all[1]
    hd = np.arange(NA_W) // NA_DH
    gm = jnp.asarray((hd[:, None] == hd[None, :]).astype(np.float32) / NA_DH)
    qh, kh, vh, zh = _od_proj(xcat, mods, od_w_in[0].astype(BF16), gm,
                              jnp.tile(na_q_norm[0], NA_HEADS).reshape(1, NA_W),
                              jnp.tile(na_k_norm[0], NA_HEADS).reshape(1, NA_W))
    na = _na(qh, kh, vh, _na_bias_table(na_rpb[0]))
    x0, uh, ut = _hy_pre(zh, hy_conv_w[0], hy_conv_b[0].reshape(1, 3 * HY_W))
    kfilt = _hy_filter(seqlen, hy_fw1[0], hy_fb1[0], hy_fw2[0], hy_fb2[0], hy_fw3[0], hy_fb3[0],
                       hy_freq[0], hy_fw4[0])
    yt = _hy_conv(ut, kfilt)
    wr, br = _router_weights(moe_wg[1], moe_bg[1], moe_we[1], moe_be[1])
    xl, h2, lg = _od_out(na, yt, x0, uh, xcat, mods, hy_bias[0].reshape(1, HY_W),
                         od_w_out[0].astype(BF16), wr, br)
    return _moe(xl, h2, lg, mods, _latent_mod_index,
                moe_w1, moe_w3, moe_w2, 1)
```

```python
import functools
import math

import numpy as np
import jax
import jax.numpy as jnp
from jax import lax
from jax.experimental import pallas as pl
from jax.experimental.pallas import tpu as pltpu

F32, BF16 = jnp.float32, jnp.bfloat16
HI = lax.Precision.HIGHEST

D_MODEL = 1024
GRID_W = 64
EPS = 1e-6
ROPE_BASE = 10000.0
NEG_INF = -1e30
GLA_HEADS, GLA_DK, GLA_DV = 4, 64, 128
GLA_QK, GLA_V = GLA_HEADS * GLA_DK, GLA_HEADS * GLA_DV
GLA_RANK = 16
GLA_TAU = 16.0
GLA_CHUNK = 64
GLA_LOG_ALPHA_MIN = -1.0
S5_W, S5_H, S5_P = 512, 16, 64
S5_G = S5_W // S5_H
S5_CHUNK = 8
S5_TBLK = 1408
NA_HEADS, NA_DH = 8, 64
NA_W = NA_HEADS * NA_DH
WIN_R, WIN_C = 8, 16
NA_HG = 4
HY_W = 512
HY_SHORT = 3
HY_EMB = 33
HY_DECAY_TARGET = 1e-2
HY_DECAY_SHORT_PCT = 0.3
HY_DECAY_LONG_PCT = 1.5
N_GROUPS, EXP_PER_GROUP = 4, 8
N_EXPERTS = N_GROUPS * EXP_PER_GROUP
D_EXPERT = 512
TOP_K = 2
MOE_BLOCK = 256
MOE_ISSUE_UNROLL = 8

TB = 256
PROJ_TILE = 768
FFT_N1 = 128
FFT_CG = 16
V7X_VMEM_LIMIT = 52 * 1024 * 1024


def _cparams(sem):
    return pltpu.CompilerParams(dimension_semantics=sem, vmem_limit_bytes=V7X_VMEM_LIMIT)


def _sigmoid(x):
    return 1.0 / (1.0 + jnp.exp(-x))


ROW_TILE = D_MODEL // 128


def _to_token_tiles(ref, val):
    n = val.shape[0]
    for j in range(ROW_TILE):
        ref[pl.ds(j, n, stride=ROW_TILE), :] = val[:, j * 128:(j + 1) * 128]


def _from_token_tiles(ref, n):
    return jnp.concatenate([ref[pl.ds(j, n, stride=ROW_TILE), :] for j in range(ROW_TILE)], axis=1)


def _split_bf16(x):
    hi = x.astype(BF16)
    return hi, (x - hi.astype(F32)).astype(BF16)


def _dot_x3(a, b):
    a_hi, a_lo = _split_bf16(a)
    b_hi, b_lo = _split_bf16(b)
    d = lambda p, q: jnp.dot(p, q, preferred_element_type=F32)
    return d(a_hi, b_hi) + d(a_lo, b_hi) + d(a_hi, b_lo)


def _rms_rows(x):
    return x * lax.rsqrt(jnp.mean(x * x, axis=-1, keepdims=True) + EPS)


def _mm_kernel(a_ref, w_ref, b_ref, f_ref, o_ref, *, use_sin):
    z = jnp.dot(a_ref[...], w_ref[...], precision=HI, preferred_element_type=F32) + b_ref[...]
    if use_sin:
        z = jnp.sin(f_ref[...] * z)
    o_ref[...] = z


def _mm(a, w, bias=None, freq=None):
    m, k = a.shape
    n = w.shape[1]
    mp, kp = -(-m // 8) * 8, -(-k // 128) * 128
    tm = min(mp, 1024)
    mp = -(-mp // tm) * tm
    tn = n if n <= 1024 else 1024
    assert n % tn == 0
    a = jnp.pad(a.astype(F32), ((0, mp - m), (0, kp - k)))
    w = jnp.pad(w.astype(F32), ((0, kp - k), (0, 0)))
    bias = jnp.zeros((n,), F32) if bias is None else bias.astype(F32)
    use_sin = freq is not None
    freq = jnp.ones((n,), F32) if freq is None else freq.astype(F32)
    out = pl.pallas_call(
        functools.partial(_mm_kernel, use_sin=use_sin),
        grid=(mp // tm, n // tn),
        in_specs=[pl.BlockSpec((tm, kp), lambda i, j: (i, 0)),
                  pl.BlockSpec((kp, tn), lambda i, j: (0, j)),
                  pl.BlockSpec((1, tn), lambda i, j: (0, j)),
                  pl.BlockSpec((1, tn), lambda i, j: (0, j))],
        out_specs=pl.BlockSpec((tm, tn), lambda i, j: (i, j)),
        out_shape=jax.ShapeDtypeStruct((mp, n), F32),
        compiler_params=_cparams(("parallel", "parallel")),
        name="small_dense",
    )(a, w, bias.reshape(1, n), freq.reshape(1, n))
    return out[:m]


def _mod_index(b, i):
    return (b, jnp.minimum(i, 1), 0, 0)


def _swapped_index(nblk, b, i):
    return (b, jnp.where(i == 0, nblk - 1, i - 1), 0)


EV_NQ, EV_NK, EV_NV, EV_NG, EV_NU = 0, 256, 512, 1024, 1536
EV_NA = 2048
EV_NTOT = 2176


def _stream_tile(ctx_ref, x_ref):
    return jnp.where(pl.program_id(1) == 0, ctx_ref[...], x_ref[...])


def _stream_specs():
    return [pl.BlockSpec((None, TB, D_MODEL), lambda b, i: (b, 0, 0)),
            pl.BlockSpec((None, TB, D_MODEL), lambda b, i: (b, jnp.maximum(i - 1, 0), 0))]


def _ev_proj_kernel(ctx_ref, x_ref, mod_ref, w_ref, wa_ref, ba_ref, cos_ref, sin_ref,
                    q_ref, k_ref, v_ref, g_ref, u_ref, usw_ref, la_ref):
    x = _stream_tile(ctx_ref, x_ref)
    h = _rms_rows(x) * (1.0 + mod_ref[1:2, :]) + mod_ref[0:1, :]
    z = jnp.dot(h.astype(BF16), w_ref[...], preferred_element_type=F32)
    lane = lax.broadcasted_iota(jnp.int32, (x.shape[0], GLA_QK), 1)
    first = (lane % 32) < 16
    cos, sin = cos_ref[...], sin_ref[...]

    def rot(t):
        partner = jnp.where(first, pltpu.roll(t, GLA_QK - 16, 1), pltpu.roll(t, 16, 1))
        return t * cos + partner * sin

    q_ref[...] = rot(z[:, EV_NQ:EV_NQ + GLA_QK]) * (GLA_DK ** -0.5)
    k_ref[...] = rot(z[:, EV_NK:EV_NK + GLA_QK])
    v_ref[...] = z[:, EV_NV:EV_NV + GLA_V].astype(BF16)
    g_ref[...] = z[:, EV_NG:EV_NG + GLA_V]
    u_ref[...] = z[:, EV_NU:EV_NU + S5_W]
    usw_ref[...] = z[:, EV_NU:EV_NU + S5_W]
    a = z[:, EV_NA:EV_NA + 128]
    pre = _dot_x3(a, wa_ref[...]) + ba_ref[...]
    ls = jnp.minimum(pre, 0.0) - jnp.log1p(jnp.exp(-jnp.abs(pre)))
    la_ref[...] = jnp.maximum(ls / GLA_TAU, GLA_LOG_ALPHA_MIN)


def _ev_proj(ctx, x, mods, w, wa, ba, cos, sin):
    bsz = x.shape[0]
    lt = ctx.shape[1] + x.shape[1]
    nblk = lt // TB
    tok = lambda n: pl.BlockSpec((None, TB, n), lambda b, i: (b, i, 0))
    const = lambda shp: pl.BlockSpec(shp, lambda b, i: tuple(0 for _ in shp))
    return pl.pallas_call(
        _ev_proj_kernel,
        grid=(bsz, nblk),
        in_specs=_stream_specs() + [
                  pl.BlockSpec((None, None, 6, D_MODEL), _mod_index),
                  const((D_MODEL, EV_NTOT)), const((128, 2 * GLA_QK)), const((1, 2 * GLA_QK)),
                  pl.BlockSpec((TB, GLA_QK), lambda b, i: (i, 0)),
                  pl.BlockSpec((TB, GLA_QK), lambda b, i: (i, 0))],
        out_specs=[tok(GLA_QK), tok(GLA_QK), tok(GLA_V), tok(GLA_V), tok(S5_W),
                   pl.BlockSpec((None, TB, S5_W), functools.partial(_swapped_index, nblk)), tok(2 * GLA_QK)],
        out_shape=[jax.ShapeDtypeStruct((bsz, lt, GLA_QK), F32),
                   jax.ShapeDtypeStruct((bsz, lt, GLA_QK), F32),
                   jax.ShapeDtypeStruct((bsz, lt, GLA_V), BF16),
                   jax.ShapeDtypeStruct((bsz, lt, GLA_V), F32),
                   jax.ShapeDtypeStruct((bsz, lt, S5_W), F32),
                   jax.ShapeDtypeStruct((bsz, lt, S5_W), F32),
                   jax.ShapeDtypeStruct((bsz, lt, 2 * GLA_QK), F32)],
        compiler_params=_cparams(("parallel", "parallel")),
        name="ev_proj",
    )(ctx, x, mods, w, wa, ba, cos, sin)


def _gla_kernel(qf_ref, kf_ref, vf_ref, laf_ref, qb_ref, kb_ref, vb_ref, lab_ref,
                of_ref, ob_ref, s_scr):
    i = pl.program_id(1)

    @pl.when(i == 0)
    def _():
        s_scr[...] = jnp.zeros_like(s_scr)

    c = GLA_CHUNK
    nh = GLA_HEADS
    row = lax.broadcasted_iota(jnp.int32, (c, c), 0)
    col = lax.broadcasted_iota(jnp.int32, (c, c), 1)
    row4 = lax.broadcasted_iota(jnp.int32, (nh * c, c), 0) % c
    col4 = lax.broadcasted_iota(jnp.int32, (nh * c, c), 1)
    lane_head = lax.broadcasted_iota(jnp.int32, (c, GLA_QK), 1) // GLA_DK
    out_head = lax.broadcasted_iota(jnp.int32, (c, GLA_V), 1) // GLA_DV
    bd_mask = (lax.broadcasted_iota(jnp.int32, (GLA_V, GLA_QK), 0) // GLA_DV
               == lax.broadcasted_iota(jnp.int32, (GLA_V, GLA_QK), 1) // GLA_DK)
    nchunk = qf_ref.shape[0] // c
    nt = (((1,), (1,)), ((), ()))
    tn = (((0,), (0,)), ((), ()))

    def one_chunk(refs, o_ref, d, r0):
        q_ref, k_ref, v_ref, la_ref = refs
        fwd = d == 0
        sl = pl.ds(r0, c)
        qc, kc, vc, lac = q_ref[sl, :], k_ref[sl, :], v_ref[sl, :], la_ref[sl, :]
        tri = ((row >= col) if fwd else (row <= col)).astype(BF16)
        la_hi, la_lo = _split_bf16(lac)
        b = (jnp.dot(tri, la_hi, preferred_element_type=F32)
             + jnp.dot(tri, la_lo, preferred_element_type=F32))
        b_last = b[c - 1:c, :] if fwd else b[0:1, :]
        qe = (qc * jnp.exp(b)).astype(BF16)
        ke = (kc * jnp.exp(-b)).astype(BF16)
        kd = (kc * jnp.exp(b_last - b)).astype(BF16)
        st = s_scr[d]
        o = lax.dot_general(qe, st.astype(BF16), nt, preferred_element_type=F32)
        q4 = jnp.concatenate([jnp.where(lane_head == h, qe, jnp.zeros_like(qe)) for h in range(nh)], axis=0)
        att = lax.dot_general(q4, ke, nt, preferred_element_type=F32)
        att_mask = (row4 >= col4) if fwd else (row4 < col4)
        o4 = jnp.dot(jnp.where(att_mask, att, 0.0).astype(BF16), vc, preferred_element_type=F32)
        for h in range(nh):
            o = o + jnp.where(out_head == h, o4[h * c:(h + 1) * c, :], 0.0)
        o_ref[sl, :] = o
        upd_t = lax.dot_general(vc, kd, tn, preferred_element_type=F32)
        s_scr[d] = st * jnp.exp(b_last) + jnp.where(bd_mask, upd_t, 0.0)

    def body(j, carry):
        one_chunk((qf_ref, kf_ref, vf_ref, laf_ref), of_ref, 0, pl.multiple_of(j * c, c))
        one_chunk((qb_ref, kb_ref, vb_ref, lab_ref), ob_ref, 1, pl.multiple_of((nchunk - 1 - j) * c, c))
        return carry

    lax.fori_loop(0, nchunk, body, 0, unroll=True)


def _gla(q, k, v, la):
    bsz, lt, _ = q.shape
    nblk = lt // TB
    fwd_map = lambda b, i: (b, i, 0)
    bwd_blk = lambda i: jnp.where(i == 0, 0, nblk - i)
    bwd_map = lambda b, i: (b, bwd_blk(i), 0)
    bwd_map_la = lambda b, i: (b, bwd_blk(i), 1)
    spec = lambda n, m: pl.BlockSpec((None, TB, n), m)
    return pl.pallas_call(
        _gla_kernel,
        grid=(bsz, nblk),
        in_specs=[spec(GLA_QK, fwd_map), spec(GLA_QK, fwd_map), spec(GLA_V, fwd_map), spec(GLA_QK, fwd_map),
                  spec(GLA_QK, bwd_map), spec(GLA_QK, bwd_map), spec(GLA_V, bwd_map), spec(GLA_QK, bwd_map_la)],
        out_specs=[spec(GLA_V, fwd_map), spec(GLA_V, bwd_map)],
        out_shape=[jax.ShapeDtypeStruct((bsz, lt, GLA_V), F32), jax.ShapeDtypeStruct((bsz, lt, GLA_V), F32)],
        scratch_shapes=[pltpu.VMEM((2, GLA_V, GLA_QK), F32)],
        compiler_params=_cparams(("parallel", "arbitrary")),
        name="gla_scan",
    )(q, k, v, la, q, k, v, la)


def _s5_kernel(u_ref, wm_ref, tm_ref, cm_ref, ar_ref, ai_ref, y_ref, w_scr, hp_scr, h_scr, *, rev):
    @pl.when(pl.program_id(1) == 0)
    def _():
        h_scr[...] = jnp.zeros_like(h_scr)

    bsz, ntok, _ = u_ref.shape
    nc = ntok // S5_CHUNK
    half = 8 * S5_P
    x = jnp.concatenate(
        [jnp.concatenate([u_ref[b, pl.ds(s, nc, stride=S5_CHUNK), :] for s in range(S5_CHUNK)], axis=1)
         for b in range(bsz)], axis=0).astype(BF16)
    w_scr[...] = jnp.dot(x, wm_ref[...], preferred_element_type=F32)
    ar, ai = ar_ref[...], ai_ref[...]

    def body(j, hs):
        c = (nc - 1 - j) if rev else j
        out = []
        for b in range(bsz):
            re, im = hs[b]
            r = b * nc + c
            hp_scr[pl.ds(r, 1), :] = jnp.concatenate([re, im], axis=1)
            w = w_scr[pl.ds(r, 1), :]
            out.append((ar * re - ai * im + w[:, :half], ar * im + ai * re + w[:, half:]))
        return tuple(out)

    hs = lax.fori_loop(0, nc, body, tuple((h_scr[b:b + 1, :half], h_scr[b:b + 1, half:]) for b in range(bsz)))
    for b in range(bsz):
        h_scr[b:b + 1, :] = jnp.concatenate(hs[b], axis=1)
    y = (jnp.dot(x, tm_ref[...], preferred_element_type=F32)
         + jnp.dot(hp_scr[...].astype(BF16), cm_ref[...], preferred_element_type=F32))
    for b in range(bsz):
        for s in range(S5_CHUNK):
            y_ref[b, pl.ds(s, nc, stride=S5_CHUNK), :] = y[b * nc:(b + 1) * nc, s * 128:(s + 1) * 128]


def _s5_mats(lam_re, lam_im, log_dt, b_re, b_im, c_re, c_im, rev):
    t16 = S5_CHUNK
    dt = jnp.exp(log_dt)[:, None]
    mag = jnp.exp(lam_re * dt)
    a_re, a_im = mag * jnp.cos(lam_im * dt), mag * jnp.sin(lam_im * dt)
    den = lam_re * lam_re + lam_im * lam_im
    nr = a_re - 1.0
    co_re = ((nr * lam_re + a_im * lam_im) / den)[..., None]
    co_im = ((a_im * lam_re - nr * lam_im) / den)[..., None]
    bb_re, bb_im = co_re * b_re - co_im * b_im, co_re * b_im + co_im * b_re
    pr, pi = [jnp.ones_like(a_re)], [jnp.zeros_like(a_im)]
    for _ in range(t16):
        pr, pi = pr + [pr[-1] * a_re - pi[-1] * a_im], pi + [pr[-1] * a_im + pi[-1] * a_re]
    pw_re, pw_im = jnp.stack(pr), jnp.stack(pi)
    g = lam_re.shape[0]
    e_re, e_im = pw_re[t16 - 1::-1][:t16], pw_im[t16 - 1::-1][:t16]
    wre = jnp.einsum('sgp,gph->gshp', e_re, bb_re) - jnp.einsum('sgp,gph->gshp', e_im, bb_im)
    wim = jnp.einsum('sgp,gph->gshp', e_re, bb_im) + jnp.einsum('sgp,gph->gshp', e_im, bb_re)
    cb_re = jnp.einsum('gkp,gph->gpkh', c_re, bb_re) - jnp.einsum('gkp,gph->gpkh', c_im, bb_im)
    cb_im = jnp.einsum('gkp,gph->gpkh', c_re, bb_im) + jnp.einsum('gkp,gph->gpkh', c_im, bb_re)
    kd = jnp.einsum('dgp,gpkh->dgkh', pw_re[:t16], cb_re) - jnp.einsum('dgp,gpkh->dgkh', pw_im[:t16], cb_im)
    lag = np.arange(t16)[None, :] - np.arange(t16)[:, None]
    toe = jnp.where((lag >= 0)[:, :, None, None, None], kd[np.clip(lag, 0, t16 - 1)], 0.0)
    toe = toe.transpose(2, 0, 4, 1, 3)
    q_re, q_im = pw_re[1:], pw_im[1:]
    ca_re = jnp.einsum('gkp,tgp->gptk', c_re, q_re) - jnp.einsum('gkp,tgp->gptk', c_im, q_im)
    ca_im = jnp.einsum('gkp,tgp->gptk', c_re, q_im) + jnp.einsum('gkp,tgp->gptk', c_im, q_re)
    if rev:
        wre, wim = wre[:, ::-1], wim[:, ::-1]
        toe = toe[:, ::-1, :, ::-1]
        ca_re, ca_im = ca_re[:, :, ::-1], ca_im[:, :, ::-1]
    nq, gl = g // 8, 8
    eye = jnp.eye(gl, dtype=F32)
    pack_w = lambda t: jnp.einsum('qgshp,gk->qsghkp', t.reshape(nq, gl, t16, S5_H, S5_P),
                                  eye).reshape(nq, t16 * 128, gl * S5_P)
    wm = jnp.concatenate([pack_w(wre), pack_w(wim)], axis=-1)
    tmat = jnp.einsum('qgshtk,gj->qsghtjk', toe.reshape(nq, gl, t16, S5_H, t16, S5_H),
                      eye).reshape(nq, t16 * 128, t16 * 128)
    pack_c = lambda t: jnp.einsum('qgptk,gj->qgptjk', t.reshape(nq, gl, S5_P, t16, S5_H),
                                  eye).reshape(nq, gl * S5_P, t16 * 128)
    cm = jnp.concatenate([pack_c(ca_re), -pack_c(ca_im)], axis=1)
    return (wm.astype(BF16), tmat.astype(BF16), cm.astype(BF16),
            pw_re[t16].reshape(nq, 1, gl * S5_P), pw_im[t16].reshape(nq, 1, gl * S5_P))


def _s5_scan(u, mats, rev):
    bsz, lt, _ = u.shape
    tblk = max(d for d in range(64, S5_TBLK + 1, 64) if lt % d == 0)
    nblk = lt // tblk
    nq = S5_G // 8
    wm, tmat, cm, ar, ai = mats
    kw, ks = S5_CHUNK * 128, 8 * S5_P
    tmap = (lambda q, t: (0, nblk - 1 - t, q)) if rev else (lambda q, t: (0, t, q))
    per = lambda shp: pl.BlockSpec((None,) + shp, lambda q, t: (q,) + tuple(0 for _ in shp))
    rows = bsz * tblk // S5_CHUNK
    return pl.pallas_call(
        functools.partial(_s5_kernel, rev=rev),
        grid=(nq, nblk),
        in_specs=[pl.BlockSpec((bsz, tblk, 128), tmap),
                  per((kw, 2 * ks)), per((kw, kw)), per((2 * ks, kw)), per((1, ks)), per((1, ks))],
        out_specs=pl.BlockSpec((bsz, tblk, 128), tmap),
        out_shape=jax.ShapeDtypeStruct((bsz, lt, S5_W), F32),
        scratch_shapes=[pltpu.VMEM((rows, 2 * ks), F32), pltpu.VMEM((rows, 2 * ks), F32),
                        pltpu.VMEM((8, 2 * ks), F32)],
        compiler_params=_cparams(("parallel", "arbitrary")),
        name="s5_scan",
    )(u, wm, tmat, cm, ar, ai)


def _mixer_tail(x, mix, mod_ref, wr_ref, br_ref, x_out, h2_out, lg_out):
    xn = x + mod_ref[2:3, :] * mix
    x_out[...] = xn
    h2 = _rms_rows(xn) * (1.0 + mod_ref[4:5, :]) + mod_ref[3:4, :]
    _to_token_tiles(h2_out, h2)
    lg_out[...] = _dot_x3(h2, wr_ref[...]) + br_ref[...]


def _ev_out_kernel(of_ref, ob_ref, g_ref, yf_ref, yb_ref, u_ref, ctx_ref, x_ref, mod_ref,
                   gn_ref, ds_ref, wglu_ref, wout_ref, wr_ref, br_ref,
                   x_out, h2_out, lg_out):
    o = of_ref[...] + ob_ref[...]
    og = jnp.concatenate([_rms_rows(o[:, h * GLA_DV:(h + 1) * GLA_DV]) for h in range(GLA_HEADS)], axis=1)
    g = g_ref[...]
    og = og * gn_ref[...] * (g * _sigmoid(g))
    t = yf_ref[...] + yb_ref[...] + ds_ref[...] * u_ref[...]
    y = t * (0.5 * (1.0 + jnp.tanh(math.sqrt(2.0 / math.pi) * (t + 0.044715 * (t * t * t)))))
    y = y * _sigmoid(jnp.dot(y.astype(BF16), wglu_ref[...], preferred_element_type=F32))
    cat = jnp.concatenate([og, y], axis=1).astype(BF16)
    mix = jnp.dot(cat, wout_ref[...], preferred_element_type=F32)
    _mixer_tail(_stream_tile(ctx_ref, x_ref), mix, mod_ref, wr_ref, br_ref, x_out, h2_out, lg_out)


def _ev_out(o_f, o_b, g, y_f, y_b, u, ctx, x, mods, gn, ds, wglu, wout, wr, br):
    bsz, lt, _ = u.shape
    nblk = lt // TB
    tok = lambda n: pl.BlockSpec((None, TB, n), lambda b, i: (b, i, 0))
    const = lambda shp: pl.BlockSpec(shp, lambda b, i: tuple(0 for _ in shp))
    return pl.pallas_call(
        _ev_out_kernel,
        grid=(bsz, nblk),
        in_specs=[tok(512), tok(512), tok(512), tok(512),
                  pl.BlockSpec((None, TB, S5_W), functools.partial(_swapped_index, nblk)),
                  tok(512)] + _stream_specs() + [
                  pl.BlockSpec((None, None, 6, D_MODEL), _mod_index),
                  const((1, 512)), const((1, 512)), const((512, 512)), const((D_MODEL, D_MODEL)),
                  const((D_MODEL, 128)), const((1, 128))],
        out_specs=[tok(D_MODEL), pl.BlockSpec((TB * ROW_TILE, 128), lambda b, i: (b * nblk + i, 0)), tok(128)],
        out_shape=[jax.ShapeDtypeStruct((bsz, lt, D_MODEL), F32),
                   jax.ShapeDtypeStruct((bsz * lt * ROW_TILE, 128), F32),
                   jax.ShapeDtypeStruct((bsz, lt, 128), F32)],
        compiler_params=_cparams(("parallel", "parallel")),
        name="ev_out",
    )(o_f, o_b, g, y_f, y_b, u, ctx, x, mods, gn, ds, wglu, wout, wr, br)


def _route(logits):
    n_tok = logits.shape[0]
    lg = logits[:, :N_GROUPS]
    grp = jnp.argmax(lg, axis=-1)
    g_w = jnp.max(jax.nn.softmax(lg, axis=-1), axis=-1, keepdims=True)
    le = logits[:, N_GROUPS:N_GROUPS + N_EXPERTS].reshape(n_tok, N_GROUPS, EXP_PER_GROUP)
    le = le[jnp.arange(n_tok), grp]
    top_p, top_i = lax.top_k(jax.nn.softmax(le, axis=-1), TOP_K)
    gate = g_w * top_p / jnp.sum(top_p, axis=-1, keepdims=True)
    eid = (grp[:, None] * EXP_PER_GROUP + top_i).reshape(-1)
    n_asg = n_tok * TOP_K
    order = jnp.argsort(eid).astype(jnp.int32)
    counts = jnp.sum((eid[:, None] == jnp.arange(N_EXPERTS)[None, :]).astype(jnp.int32), axis=0)
    padded = (counts + MOE_BLOCK - 1) // MOE_BLOCK * MOE_BLOCK
    pad_end = jnp.cumsum(padded)
    pad_start = pad_end - padded
    cnt_start = jnp.cumsum(counts) - counts
    n_blocks = -(-n_asg // MOE_BLOCK) + N_EXPERTS
    blk_start = jnp.arange(n_blocks, dtype=jnp.int32) * MOE_BLOCK
    blk_e = jnp.minimum(jnp.sum((pad_end[None, :] <= blk_start[:, None]).astype(jnp.int32), axis=1), N_EXPERTS - 1)
    pos = jnp.arange(n_blocks * MOE_BLOCK, dtype=jnp.int32)
    pos_e = jnp.repeat(blk_e, MOE_BLOCK)
    rank = pos - pad_start[pos_e]
    src = jnp.clip(cnt_start[pos_e] + rank, 0, n_asg - 1)
    slot_buf = jnp.where(rank < counts[pos_e], order[src], n_asg).astype(jnp.int32)
    n_valid = jnp.sum((slot_buf < n_asg).reshape(n_blocks, MOE_BLOCK), axis=1).astype(jnp.int32)
    tok = lax.shift_right_logical(slot_buf, 1)
    src_rows = jnp.minimum(tok, n_tok - 1) * ROW_TILE
    spare = n_asg + (pos // MOE_BLOCK % 2) * MOE_BLOCK + pos % MOE_BLOCK
    dst_rows = jnp.where(slot_buf < n_asg, (slot_buf & 1) * n_tok + tok, spare)
    lead = n_asg + jnp.arange(2 * MOE_BLOCK, dtype=jnp.int32)
    dst_rows = jnp.concatenate([lead, dst_rows]) * ROW_TILE
    return src_rows, dst_rows, blk_e, n_valid, gate.astype(F32)


def _moe_kernel(src_ref, dst_ref, blke_ref, nvalid_ref, h_hbm, w1_ref, w3_ref, w2_ref, z_hbm,
                xbuf0, xbuf1, ybuf0, ybuf1, wb1, wb3, wb2, gsem, ssem):
    i = pl.program_id(0)
    nblk = pl.num_programs(0)
    ns = ROW_TILE
    xb, yb = (xbuf0, xbuf1), (ybuf0, ybuf1)
    lead = 2 * MOE_BLOCK

    def issue_gather(blk, buf):
        base = blk * MOE_BLOCK
        for r in range(MOE_BLOCK):
            src = pl.multiple_of(src_ref[base + r], ns)
            pltpu.make_async_copy(h_hbm.at[pl.ds(src, ns)], xb[buf].at[pl.ds(r * ns, ns)],
                                  gsem.at[buf]).start(priority=r % 2)

    def issue_scatter(blk, buf):
        base = lead + blk * MOE_BLOCK
        for r in range(MOE_BLOCK):
            dst = pl.multiple_of(dst_ref[base + r], ns)
            pltpu.make_async_copy(yb[buf].at[pl.ds(r * ns, ns)], z_hbm.at[pl.ds(dst, ns)],
                                  ssem.at[buf]).start(priority=r % 2)

    def wait_gather(buf):
        pltpu.make_async_copy(h_hbm.at[pl.ds(0, MOE_BLOCK * ns)], xb[buf], gsem.at[buf]).wait()

    def wait_scatter(buf):
        pltpu.make_async_copy(yb[buf], z_hbm.at[pl.ds(0, MOE_BLOCK * ns)], ssem.at[buf]).wait()

    used = nvalid_ref[i] > 0

    @pl.when(i == 0)
    def _():
        ybuf0[...] = jnp.zeros_like(ybuf0)
        ybuf1[...] = jnp.zeros_like(ybuf1)
        issue_scatter(-2, 0)
        issue_gather(0, 0)

    def step(cur):
        oth = 1 - cur
        wait_gather(cur)
        issue_gather(jnp.minimum(i + 1, nblk - 1), oth)
        issue_scatter(i - 1, oth)
        x = _from_token_tiles(xb[cur], MOE_BLOCK).astype(BF16)

        @pl.when(jnp.logical_or(i == 0, blke_ref[i] != blke_ref[jnp.maximum(i - 1, 0)]))
        def _():
            wb1[...] = w1_ref[...].astype(BF16)
            wb3[...] = w3_ref[...].astype(BF16)
            wb2[...] = w2_ref[...].astype(BF16)

        h1 = jnp.dot(x, wb1[...], preferred_element_type=F32)
        h3 = jnp.dot(x, wb3[...], preferred_element_type=F32)
        a = (h1 * _sigmoid(h1) * h3).astype(BF16)
        y = jnp.dot(a, wb2[...], preferred_element_type=F32)
        wait_scatter(cur)
        _to_token_tiles(yb[cur], y)

    def drain(last_par):
        wait_gather(1 - last_par)
        issue_scatter(jnp.where(used, i, i - 1), last_par)
        wait_scatter(1 - last_par)
        wait_scatter(last_par)

    for par in range(2):
        @pl.when(jnp.logical_and(used, i % 2 == par))
        def _():
            step(par)

    first_unused = jnp.logical_and(jnp.logical_not(used),
                                   jnp.logical_and(i > 0, nvalid_ref[jnp.maximum(i - 1, 0)] > 0))
    last_used = jnp.logical_and(used, i == nblk - 1)
    for par in range(2):
        @pl.when(jnp.logical_or(jnp.logical_and(first_unused, (i - 1) % 2 == par),
                                jnp.logical_and(last_used, i % 2 == par)))
        def _():
            drain(par)


def _moe_kernel_loops(src_ref, dst_ref, blke_ref, nvalid_ref, h_hbm, w1_ref, w3_ref, w2_ref, z_hbm,
                xbuf, ybuf, wb1, wb3, wb2, gsem, ssem):
    i = pl.program_id(0)
    nblk = pl.num_programs(0)
    cur = i % 2

    ns = ROW_TILE

    def gather_row(blk, buf, r, priority):
        src = pl.multiple_of(src_ref[blk * MOE_BLOCK + r], ns)
        pltpu.make_async_copy(h_hbm.at[pl.ds(src, ns)],
                              xbuf.at[buf, pl.ds(pl.multiple_of(r * ns, ns), ns)],
                              gsem.at[buf]).start(priority=priority)

    def scatter_row(blk, buf, r, priority):
        dst = pl.multiple_of(dst_ref[blk * MOE_BLOCK + r], ns)
        pltpu.make_async_copy(ybuf.at[buf, pl.ds(pl.multiple_of(r * ns, ns), ns)],
                              z_hbm.at[pl.ds(dst, ns)],
                              ssem.at[buf]).start(priority=priority)

    def all_rows(row_fn, blk, buf):
        def body(r8, carry):
            for j in range(MOE_ISSUE_UNROLL):
                row_fn(blk, buf, r8 * MOE_ISSUE_UNROLL + j, j % 2)
            return carry
        lax.fori_loop(0, MOE_BLOCK // MOE_ISSUE_UNROLL, body, 0)

    def gather(blk, buf):
        all_rows(gather_row, blk, buf)

    def wait_gather(buf):
        pltpu.make_async_copy(h_hbm.at[pl.ds(0, MOE_BLOCK * ns)], xbuf.at[buf], gsem.at[buf]).wait()

    def scatter(blk, buf):
        n = nvalid_ref[blk]

        @pl.when(n == MOE_BLOCK)
        def _():
            all_rows(scatter_row, blk, buf)

        @pl.when(n < MOE_BLOCK)
        def _():
            def body(r, carry):
                scatter_row(blk, buf, r, 0)
                return carry
            lax.fori_loop(0, n, body, 0)

    def wait_scatter(blk, buf):
        n = nvalid_ref[blk]

        @pl.when(n == MOE_BLOCK)
        def _():
            pltpu.make_async_copy(ybuf.at[buf], z_hbm.at[pl.ds(0, MOE_BLOCK * ns)], ssem.at[buf]).wait()

        @pl.when(n < MOE_BLOCK)
        def _():
            def body(r, carry):
                pltpu.make_async_copy(ybuf.at[buf, pl.ds(0, ns)], z_hbm.at[pl.ds(0, ns)], ssem.at[buf]).wait()
                return carry
            lax.fori_loop(0, n, body, 0)

    used = nvalid_ref[i] > 0

    @pl.when(jnp.logical_and(i == 0, used))
    def _():
        gather(0, 0)

    @pl.when(used)
    def _():
        wait_gather(cur)

        @pl.when(jnp.logical_and(i + 1 < nblk, nvalid_ref[jnp.minimum(i + 1, nblk - 1)] > 0))
        def _():
            gather(i + 1, 1 - cur)

        x = _from_token_tiles(xbuf.at[cur], MOE_BLOCK).astype(BF16)
        @pl.when(jnp.logical_or(i == 0, blke_ref[i] != blke_ref[jnp.maximum(i - 1, 0)]))
        def _():
            wb1[...] = w1_ref[...].astype(BF16)
            wb3[...] = w3_ref[...].astype(BF16)
            wb2[...] = w2_ref[...].astype(BF16)

        h1 = jnp.dot(x, wb1[...], preferred_element_type=F32)
        h3 = jnp.dot(x, wb3[...], preferred_element_type=F32)
        a = (h1 * _sigmoid(h1) * h3).astype(BF16)
        y = jnp.dot(a, wb2[...], preferred_element_type=F32)

        @pl.when(i >= 2)
        def _():
            wait_scatter(i - 2, cur)

        _to_token_tiles(ybuf.at[cur], y)
        scatter(i, cur)

    first_unused = jnp.logical_and(jnp.logical_not(used),
                                   jnp.logical_and(i > 0, nvalid_ref[jnp.maximum(i - 1, 0)] > 0))
    last_used = jnp.logical_and(used, i == nblk - 1)

    def drain(last):
        wait_scatter(last, last % 2)

        @pl.when(last >= 1)
        def _():
            wait_scatter(last - 1, (last - 1) % 2)

    @pl.when(first_unused)
    def _():
        drain(i - 1)

    @pl.when(last_used)
    def _():
        drain(i)


def _moe_experts(h2, src_rows, dst_rows, blk_e, n_valid, w1, w3, w2, layer):
    n_tok = h2.shape[0] // ROW_TILE
    n_blocks = blk_e.shape[0]
    wspec = lambda shp: pl.BlockSpec((None, None) + shp, lambda i, src, dst, blke, nvalid: (layer, blke[i], 0, 0))
    grid_spec = pltpu.PrefetchScalarGridSpec(
        num_scalar_prefetch=4,
        grid=(n_blocks,),
        in_specs=[pl.BlockSpec(memory_space=pl.ANY),
                  wspec((D_MODEL, D_EXPERT)), wspec((D_MODEL, D_EXPERT)), wspec((D_EXPERT, D_MODEL))],
        out_specs=pl.BlockSpec(memory_space=pl.ANY),
        scratch_shapes=[pltpu.VMEM((MOE_BLOCK * ROW_TILE, 128), F32), pltpu.VMEM((MOE_BLOCK * ROW_TILE, 128), F32),
                        pltpu.VMEM((MOE_BLOCK * ROW_TILE, 128), F32), pltpu.VMEM((MOE_BLOCK * ROW_TILE, 128), F32),
                        pltpu.VMEM((D_MODEL, D_EXPERT), BF16), pltpu.VMEM((D_MODEL, D_EXPERT), BF16),
                        pltpu.VMEM((D_EXPERT, D_MODEL), BF16),
                        pltpu.SemaphoreType.DMA((2,)), pltpu.SemaphoreType.DMA((2,))])
    return pl.pallas_call(
        _moe_kernel,
        grid_spec=grid_spec,
        out_shape=jax.ShapeDtypeStruct(((TOP_K * n_tok + 2 * MOE_BLOCK) * ROW_TILE, 128), F32),
        compiler_params=_cparams(("arbitrary",)),
        name="moe_experts",
    )(src_rows, dst_rows, blk_e, n_valid, h2, w1, w3, w2)


def _moe_combine_kernel(x_ref, z0_ref, z1_ref, gate_ref, mod_ref, o_ref):
    gate = gate_ref[...]
    tb = x_ref.shape[0]
    y = gate[:, 0:1] * _from_token_tiles(z0_ref, tb) + gate[:, 1:2] * _from_token_tiles(z1_ref, tb)
    o_ref[...] = x_ref[...] + mod_ref[5:6, :] * y


def _moe_combine(x, z, gate, mods, mod_index):
    bsz, lt, _ = x.shape
    nblk = lt // TB
    gate3 = gate.reshape(bsz, lt, TOP_K)
    return pl.pallas_call(
        _moe_combine_kernel,
        grid=(bsz, nblk),
        in_specs=[pl.BlockSpec((None, TB, D_MODEL), lambda b, i: (b, i, 0)),
                  pl.BlockSpec((TB * ROW_TILE, 128), lambda b, i: (b * nblk + i, 0)),
                  pl.BlockSpec((TB * ROW_TILE, 128), lambda b, i: ((bsz + b) * nblk + i, 0)),
                  pl.BlockSpec((None, TB, TOP_K), lambda b, i: (b, i, 0)),
                  pl.BlockSpec((None, None, 6, D_MODEL), mod_index)],
        out_specs=pl.BlockSpec((None, TB, D_MODEL), lambda b, i: (b, i, 0)),
        out_shape=jax.ShapeDtypeStruct((bsz, lt, D_MODEL), F32),
        compiler_params=_cparams(("parallel", "parallel")),
        name="moe_combine",
    )(x, z, z, gate3, mods)


def _moe(x, h2, logits, mods, mod_index, w1, w3, w2, layer):
    bsz, lt, _ = x.shape
    src_rows, dst_rows, blk_e, n_valid, gate = _route(logits.reshape(bsz * lt, 128))
    z = _moe_experts(h2, src_rows, dst_rows, blk_e, n_valid, w1, w3, w2, layer)
    return _moe_combine(x, z, gate, mods, mod_index)


def _od_proj_kernel(x_ref, mod_ref, w_ref, gm_ref, qn_ref, kn_ref, q_ref, k_ref, v_ref, zh_ref):
    tm = x_ref.shape[0]
    is_ctx = pl.program_id(1) * tm + lax.broadcasted_iota(jnp.int32, (tm, 1), 0) < TB
    shift = jnp.where(is_ctx, mod_ref[0, 0:1, :], mod_ref[1, 0:1, :])
    scale = jnp.where(is_ctx, mod_ref[0, 1:2, :], mod_ref[1, 1:2, :])
    h = _rms_rows(x_ref[...]) * (1.0 + scale) + shift
    z = jnp.dot(h.astype(BF16), w_ref[...], preferred_element_type=F32)

    def head_norm(t, gain):
        sq_hi, sq_lo = _split_bf16(t * t)
        gm = gm_ref[...].astype(BF16)
        ms = jnp.dot(sq_hi, gm, preferred_element_type=F32) + jnp.dot(sq_lo, gm, preferred_element_type=F32)
        return t * lax.rsqrt(ms + EPS) * gain

    q_ref[...] = (head_norm(z[:, :NA_W], qn_ref[...]) * (NA_DH ** -0.5)).astype(BF16)
    k_ref[...] = head_norm(z[:, NA_W:2 * NA_W], kn_ref[...]).astype(BF16)
    v_ref[...] = z[:, 2 * NA_W:3 * NA_W].astype(BF16)
    zh_ref[...] = z[:, 3 * NA_W:]


def _od_proj(xcat, mods, w, gm, qn, kn):
    bsz, lt, _ = xcat.shape
    tm = max(d for d in range(TB, PROJ_TILE + 1, TB) if lt % d == 0)
    nblk = lt // tm
    tok = lambda n: pl.BlockSpec((None, tm, n), lambda b, i: (b, i, 0))
    const = lambda shp: pl.BlockSpec(shp, lambda b, i: tuple(0 for _ in shp))
    return pl.pallas_call(
        _od_proj_kernel,
        grid=(bsz, nblk),
        in_specs=[tok(D_MODEL), pl.BlockSpec((None, 2, 6, D_MODEL), lambda b, i: (b, 0, 0, 0)),
                  const((D_MODEL, 3 * NA_W + 3 * HY_W)), const((NA_W, NA_W)), const((1, NA_W)), const((1, NA_W))],
        out_specs=[tok(NA_W), tok(NA_W), tok(NA_W), tok(3 * HY_W)],
        out_shape=[jax.ShapeDtypeStruct((bsz, lt, NA_W), BF16), jax.ShapeDtypeStruct((bsz, lt, NA_W), BF16),
                   jax.ShapeDtypeStruct((bsz, lt, NA_W), BF16), jax.ShapeDtypeStruct((bsz, lt, 3 * HY_W), F32)],
        compiler_params=_cparams(("parallel", "parallel")),
        name="od_proj",
    )(xcat, mods, w, gm, qn, kn)


def _na_kernel(q_ref, k_ref, v_ref, t2_ref, o_ref):
    r = pl.program_id(1)
    n_rows = pl.num_programs(1)
    r0 = jnp.clip(r - WIN_R // 2, 0, n_rows - WIN_R)
    off = r0 - r + WIN_R - 1
    base = pl.multiple_of(TB + r0 * GRID_W, GRID_W)
    nloc = WIN_R * GRID_W
    q = q_ref[...]
    hg = NA_HG
    gw = hg * NA_DH
    lane_head = lax.broadcasted_iota(jnp.int32, (GRID_W, gw), 1) // NA_DH
    nt = (((1,), (1,)), ((), ()))
    outs = []
    for grp in range(NA_HEADS // hg):
        cs = slice(gw * grp, gw * (grp + 1))
        q2 = q[:, cs]
        q4 = jnp.concatenate([jnp.where(lane_head == h, q2, jnp.zeros_like(q2)) for h in range(hg)], axis=0)
        kw, vw = k_ref[pl.ds(base, nloc), cs], v_ref[pl.ds(base, nloc), cs]
        kc, vc = k_ref[0:TB, cs], v_ref[0:TB, cs]
        bias = jnp.concatenate(
            [jnp.concatenate([t2_ref[hg * grp + h, off + 2 * m] for m in range(WIN_R // 2)], axis=1)
             for h in range(hg)], axis=0)
        s_loc = lax.dot_general(q4, kw, nt, preferred_element_type=F32) + bias
        s_ctx = lax.dot_general(q4, kc, nt, preferred_element_type=F32)
        m = jnp.maximum(jnp.max(s_loc, axis=1, keepdims=True), jnp.max(s_ctx, axis=1, keepdims=True))
        p_loc, p_ctx = jnp.exp(s_loc - m), jnp.exp(s_ctx - m)
        den = jnp.sum(p_loc, axis=1, keepdims=True) + jnp.sum(p_ctx, axis=1, keepdims=True)
        o4 = (jnp.dot(p_loc.astype(BF16), vw, preferred_element_type=F32)
              + jnp.dot(p_ctx.astype(BF16), vc, preferred_element_type=F32)) / den
        acc = jnp.zeros((GRID_W, gw), F32)
        for h in range(hg):
            acc = jnp.where(lane_head == h, o4[h * GRID_W:(h + 1) * GRID_W, :], acc)
        outs.append(acc)
    o_ref[...] = jnp.concatenate(outs, axis=1)


def _na_bias_table(rpb):
    qc = np.arange(GRID_W)[:, None]
    kc = np.arange(GRID_W)[None, :]
    q_start = np.clip(qc - WIN_C // 2, 0, GRID_W - WIN_C)
    valid = (kc >= q_start) & (kc < q_start + WIN_C)
    col_idx = np.clip(kc - qc + WIN_C - 1, 0, 2 * WIN_C - 2)
    t = jnp.where(valid[None, None], rpb.astype(F32)[:, :, col_idx], NEG_INF)
    return jnp.concatenate([t[:, :-1], t[:, 1:]], axis=-1)


def _na(q, k, v, t2):
    bsz, lt, _ = q.shape
    n_rows = (lt - TB) // GRID_W
    qoff = TB // GRID_W
    return pl.pallas_call(
        _na_kernel,
        grid=(bsz, n_rows),
        in_specs=[pl.BlockSpec((None, GRID_W, NA_W), lambda b, r: (b, r + qoff, 0)),
                  pl.BlockSpec((None, lt, NA_W), lambda b, r: (b, 0, 0)),
                  pl.BlockSpec((None, lt, NA_W), lambda b, r: (b, 0, 0)),
                  pl.BlockSpec(t2.shape, lambda b, r: (0, 0, 0, 0))],
        out_specs=pl.BlockSpec((None, GRID_W, NA_W), lambda b, r: (b, r, 0)),
        out_shape=jax.ShapeDtypeStruct((bsz, lt - TB, NA_W), F32),
        compiler_params=_cparams(("parallel", "arbitrary")),
        name="na_attn",
    )(q, k, v, t2)


def _hy_pre_kernel(z_ref, zp_ref, zn_ref, cw_ref, cb_ref, x0_ref, u_ref, ut_ref):
    i = pl.program_id(1)
    n = pl.num_programs(1)
    z = z_ref[...]
    tb = z.shape[0]
    prev_row = jnp.where(i > 0, zp_ref[7:8, :], 0.0)
    next_row = jnp.where(i < n - 1, zn_ref[0:1, :], 0.0)
    rowid = lax.broadcasted_iota(jnp.int32, z.shape, 0)
    zm = jnp.where(rowid == 0, prev_row, pltpu.roll(z, 1, 0))
    zp = jnp.where(rowid == tb - 1, next_row, pltpu.roll(z, tb - 1, 0))
    zc = cb_ref[...] + cw_ref[0:1, :] * zm
    zc = zc + cw_ref[1:2, :] * z
    zc = zc + cw_ref[2:3, :] * zp
    x0_ref[...] = zc[:, :HY_W]
    u = zc[:, HY_W:2 * HY_W] * zc[:, 2 * HY_W:]
    u_ref[...] = u
    for j in range(tb // FFT_N1):
        ut_ref[j] = u[j * FFT_N1:(j + 1) * FFT_N1, :].T.astype(BF16)


def _hy_pre(zh, cw, cb):
    bsz, lt, _ = zh.shape
    l = lt - TB
    nblk = l // TB
    h8 = TB // 8
    return pl.pallas_call(
        _hy_pre_kernel,
        grid=(bsz, nblk),
        in_specs=[pl.BlockSpec((None, TB, 3 * HY_W), lambda b, i: (b, i + 1, 0)),
                  pl.BlockSpec((None, 8, 3 * HY_W), lambda b, i: (b, (i + 1) * h8 - 1, 0)),
                  pl.BlockSpec((None, 8, 3 * HY_W), lambda b, i: (b, jnp.minimum((i + 2) * h8, lt // 8 - 1), 0)),
                  pl.BlockSpec((HY_SHORT, 3 * HY_W), lambda b, i: (0, 0)),
                  pl.BlockSpec((1, 3 * HY_W), lambda b, i: (0, 0))],
        out_specs=[pl.BlockSpec((None, TB, HY_W), lambda b, i: (b, i, 0)),
                   pl.BlockSpec((None, TB, HY_W), lambda b, i: (b, i, 0)),
                   pl.BlockSpec((None, TB // FFT_N1, HY_W, FFT_N1), lambda b, i: (b, i, 0, 0))],
        out_shape=[jax.ShapeDtypeStruct((bsz, l, HY_W), F32), jax.ShapeDtypeStruct((bsz, l, HY_W), F32),
                   jax.ShapeDtypeStruct((bsz, l // FFT_N1, HY_W, FFT_N1), BF16)],
        compiler_params=_cparams(("parallel", "parallel")),
        name="hy_pre",
    )(zh, zh, zh, cw, cb)


def _fft_consts(n1_in):
    n = FFT_N1
    idx = np.arange(n)
    ang1 = 2.0 * np.pi * np.outer(idx, idx) / n
    c, s = np.cos(ang1), np.sin(ang1)
    angt = 2.0 * np.pi * np.outer(idx, idx) / (n * n)
    tw = np.concatenate([np.cos(angt), -np.sin(angt)], axis=1)
    f3 = np.block([[c, -s], [s, c]])
    f3i = np.block([[c, s], [-s, c]])
    ch, sh = c[:, :n1_in], s[:, :n1_in]
    f1_pair = np.block([[ch, sh], [-sh, ch]])
    f1_real = np.concatenate([c, -s], axis=0)
    f1i = np.block([[ch.T, -sh.T], [sh.T, ch.T]]) / (n * n)
    return tw, f3, f3i, f1_pair, f1_real, f1i


def _fft_forward(a, tw_re, tw_im, lhs_scr, ncg):
    for cix in range(ncg):
        cs = slice(cix * FFT_N1, (cix + 1) * FFT_N1)
        are, aim = a[:FFT_N1, cs], a[FFT_N1:, cs]
        lhs_scr[cs, :FFT_N1] = are * tw_re - aim * tw_im
        lhs_scr[cs, FFT_N1:] = are * tw_im + aim * tw_re


def _hy_filt_kernel(k_ref, f1_ref, tw_ref, f3_ref, o_ref, lhs_scr):
    a = _dot_x3(f1_ref[...], k_ref[...])
    _fft_forward(a, tw_ref[:, :FFT_N1], tw_ref[:, FFT_N1:], lhs_scr, FFT_CG)
    o_ref[...] = _dot_x3(lhs_scr[...], f3_ref[...])


def _hy_fft_kernel(u_ref, kf_ref, f1_ref, tw_ref, f3_ref, f3i_ref, f1i_ref, y_ref, lhs_scr, a2_scr):
    cgl = FFT_CG * FFT_N1
    x = u_ref[...].reshape(2 * u_ref.shape[1], cgl)
    a = jnp.dot(f1_ref[...], x, preferred_element_type=F32)
    tw_re, tw_im = tw_ref[:, :FFT_N1], tw_ref[:, FFT_N1:]
    _fft_forward(a, tw_re, tw_im, lhs_scr, FFT_CG)
    y = jnp.dot(lhs_scr[...].astype(BF16), f3_ref[...], preferred_element_type=F32)
    yre, yim = y[:, :FFT_N1], y[:, FFT_N1:]
    kre, kim = kf_ref[:, :FFT_N1], kf_ref[:, FFT_N1:]
    z = jnp.concatenate([yre * kre - yim * kim, yre * kim + yim * kre], axis=1).astype(BF16)
    bp = jnp.dot(z, f3i_ref[...], preferred_element_type=F32)
    for cix in range(FFT_CG):
        cs = slice(cix * FFT_N1, (cix + 1) * FFT_N1)
        bre, bim = bp[cs, :FFT_N1], bp[cs, FFT_N1:]
        a2_scr[:FFT_N1, cs] = bre * tw_re + bim * tw_im
        a2_scr[FFT_N1:, cs] = bim * tw_re - bre * tw_im
    out = jnp.dot(f1i_ref[...], a2_scr[...].astype(BF16), preferred_element_type=F32)
    y_ref[...] = out.reshape(2, u_ref.shape[1], cgl)


def _hy_conv(ut, kfilt):
    bsz, n1h, nch, _ = ut.shape
    assert 2 * n1h == FFT_N1 and bsz % 2 == 0
    cgl = FFT_CG * FFT_N1
    ncol = nch * FFT_N1
    tw, f3, f3i, f1_pair, f1_real, f1i = _fft_consts(n1h)
    kt = kfilt.reshape(FFT_N1, FFT_N1, nch).transpose(0, 2, 1).reshape(FFT_N1, ncol)
    const2 = lambda shp: pl.BlockSpec(shp, lambda *a: (0, 0))
    kf = pl.pallas_call(
        _hy_filt_kernel,
        grid=(nch // FFT_CG,),
        in_specs=[pl.BlockSpec((FFT_N1, cgl), lambda j: (0, j)), const2((2 * FFT_N1, FFT_N1)),
                  const2((FFT_N1, 2 * FFT_N1)), const2((2 * FFT_N1, 2 * FFT_N1))],
        out_specs=pl.BlockSpec((cgl, 2 * FFT_N1), lambda j: (j, 0)),
        out_shape=jax.ShapeDtypeStruct((ncol, 2 * FFT_N1), F32),
        scratch_shapes=[pltpu.VMEM((cgl, 2 * FFT_N1), F32)],
        compiler_params=_cparams(("parallel",)),
        name="hy_filter_dft",
    )(kt, jnp.asarray(f1_real, F32), jnp.asarray(tw, F32), jnp.asarray(f3, F32))
    u2 = ut.reshape(bsz, n1h, ncol)
    y = pl.pallas_call(
        _hy_fft_kernel,
        grid=(bsz // 2, nch // FFT_CG),
        in_specs=[pl.BlockSpec((2, n1h, cgl), lambda p, j: (p, 0, j)),
                  pl.BlockSpec((cgl, 2 * FFT_N1), lambda p, j: (j, 0)),
                  const2((2 * FFT_N1, FFT_N1)), const2((FFT_N1, 2 * FFT_N1)),
                  const2((2 * FFT_N1, 2 * FFT_N1)), const2((2 * FFT_N1, 2 * FFT_N1)), const2((FFT_N1, 2 * FFT_N1))],
        out_specs=pl.BlockSpec((2, n1h, cgl), lambda p, j: (p, 0, j)),
        out_shape=jax.ShapeDtypeStruct((bsz, n1h, ncol), F32),
        scratch_shapes=[pltpu.VMEM((cgl, 2 * FFT_N1), F32), pltpu.VMEM((2 * FFT_N1, cgl), F32)],
        compiler_params=_cparams(("parallel", "parallel")),
        name="hy_fft_conv",
    )(u2, kf, jnp.asarray(f1_pair, F32).astype(BF16), jnp.asarray(tw, F32), jnp.asarray(f3, F32).astype(BF16),
      jnp.asarray(f3i, F32).astype(BF16), jnp.asarray(f1i, F32).astype(BF16))
    return y.reshape(bsz, n1h, nch, FFT_N1)


def _hy_filter(seqlen, fw1, fb1, fw2, fb2, fw3, fb3, freq, fw4):
    t = jnp.linspace(0.0, 1.0, seqlen, dtype=F32)[:, None]
    bands = (HY_EMB - 1) // 2
    w = 2.0 * math.pi * jnp.arange(seqlen, dtype=F32)[:, None] / seqlen
    f = jnp.linspace(1e-4, bands - 1, bands, dtype=F32)[None, :]
    feat = jnp.concatenate([t, jnp.cos(f * w), -jnp.sin(f * w)], axis=-1)
    feat2 = jnp.concatenate([feat, feat[::-1]], axis=0)
    h = _mm(feat2, fw1, fb1, freq)
    h = _mm(h, fw2, fb2, freq)
    h = _mm(h, fw3, fb3, freq)
    deltas = jnp.abs(jnp.linspace(math.log(HY_DECAY_TARGET) / HY_DECAY_LONG_PCT,
                                  math.log(HY_DECAY_TARGET) / HY_DECAY_SHORT_PCT, HY_W, dtype=F32))
    hf = _mm(h[:seqlen], fw4[:, :HY_W]) * jnp.exp(-t * deltas)
    hb = _mm(h[seqlen:], fw4[:, HY_W:]) * jnp.exp(-t[::-1] * deltas)
    return jnp.concatenate([hf, jnp.zeros((1, HY_W), F32), hb[:seqlen - 1]], axis=0)


def _od_out_kernel(na_ref, yt_ref, x0_ref, u_ref, x_ref, mod_ref, hb_ref, wout_ref, wr_ref, br_ref,
                   x_out, h2_out, lg_out):
    y = jnp.concatenate([yt_ref[j].T for j in range(yt_ref.shape[0])], axis=0)
    hy = x0_ref[...] * (y + u_ref[...] * hb_ref[...])
    cat = jnp.concatenate([na_ref[...], hy], axis=1).astype(BF16)
    mix = jnp.dot(cat, wout_ref[...], preferred_element_type=F32)
    _mixer_tail(x_ref[...], mix, mod_ref, wr_ref, br_ref, x_out, h2_out, lg_out)


def _latent_mod_index(b, i):
    return (b, 1, 0, 0)


def _od_out(na, yt, x0, u, xcat, mods, hb, wout, wr, br):
    bsz, l, _ = na.shape
    nblk = l // TB
    tok = lambda n: pl.BlockSpec((None, TB, n), lambda b, i: (b, i, 0))
    const = lambda shp: pl.BlockSpec(shp, lambda b, i: tuple(0 for _ in shp))
    return pl.pallas_call(
        _od_out_kernel,
        grid=(bsz, nblk),
        in_specs=[tok(NA_W), pl.BlockSpec((None, TB // FFT_N1, HY_W, FFT_N1), lambda b, i: (b, i, 0, 0)),
                  tok(HY_W), tok(HY_W),
                  pl.BlockSpec((None, TB, D_MODEL), lambda b, i: (b, i + 1, 0)),
                  pl.BlockSpec((None, None, 6, D_MODEL), _latent_mod_index),
                  const((1, HY_W)), const((D_MODEL, D_MODEL)), const((D_MODEL, 128)), const((1, 128))],
        out_specs=[tok(D_MODEL), pl.BlockSpec((TB * ROW_TILE, 128), lambda b, i: (b * nblk + i, 0)), tok(128)],
        out_shape=[jax.ShapeDtypeStruct((bsz, l, D_MODEL), F32), jax.ShapeDtypeStruct((bsz * l * ROW_TILE, 128), F32),
                   jax.ShapeDtypeStruct((bsz, l, 128), F32)],
        compiler_params=_cparams(("parallel", "parallel")),
        name="od_out",
    )(na, yt, x0, u, xcat, mods, hb, wout, wr, br)


def _mods(c, c_ctx, ada_w, ada_b):
    bsz = c.shape[0]
    depth, _, n = ada_w.shape
    cc = jnp.concatenate([c, c_ctx[None]], axis=0)
    a = jnp.pad(cc * _sigmoid(cc), ((0, 8 - (bsz + 1) % 8), (0, 0)))
    mp, tn = a.shape[0], 1024
    m = pl.pallas_call(
        functools.partial(_mm_kernel, use_sin=False),
        grid=(depth, n // tn),
        in_specs=[pl.BlockSpec((mp, D_MODEL), lambda l, j: (0, 0)),
                  pl.BlockSpec((None, D_MODEL, tn), lambda l, j: (l, 0, j)),
                  pl.BlockSpec((None, 1, tn), lambda l, j: (l, 0, j)),
                  pl.BlockSpec((1, tn), lambda l, j: (0, 0))],
        out_specs=pl.BlockSpec((None, mp, tn), lambda l, j: (l, 0, j)),
        out_shape=jax.ShapeDtypeStruct((depth, mp, n), F32),
        compiler_params=_cparams(("parallel", "parallel")),
        name="adaln_dense",
    )(a, ada_w, ada_b.reshape(depth, 1, n), jnp.ones((1, tn), F32))
    mod_l = m[:, :bsz].reshape(depth, bsz, 1, 6, D_MODEL)
    mod_c = jnp.broadcast_to(m[:, bsz].reshape(depth, 1, 1, 6, D_MODEL), (depth, bsz, 1, 6, D_MODEL))
    return jnp.concatenate([mod_c, mod_l], axis=2)


def _rope_tables(seqlen):
    pos = jnp.arange(seqlen)
    half = GLA_DK // 4
    freqs = ROPE_BASE ** (-jnp.arange(half, dtype=F32) / half)
    ar = (pos // GRID_W).astype(F32)[:, None] * freqs
    ac = (pos % GRID_W).astype(F32)[:, None] * freqs
    cos = jnp.concatenate([jnp.cos(ar), jnp.cos(ar), jnp.cos(ac), jnp.cos(ac)], axis=1)
    sin = jnp.concatenate([-jnp.sin(ar), jnp.sin(ar), -jnp.sin(ac), jnp.sin(ac)], axis=1)
    cos = jnp.concatenate([jnp.ones((TB, GLA_DK), F32), cos], axis=0)
    sin = jnp.concatenate([jnp.zeros((TB, GLA_DK), F32), sin], axis=0)
    return jnp.tile(cos, (1, GLA_HEADS)), jnp.tile(sin, (1, GLA_HEADS))


def _router_weights(wg, bg, we, be):
    pad = 128 - N_GROUPS - N_EXPERTS
    wr = jnp.concatenate([wg, we, jnp.zeros((D_MODEL, pad), F32)], axis=1)
    br = jnp.concatenate([bg, be, jnp.zeros((pad,), F32)]).reshape(1, 128)
    return wr, br


def kernel(x, c, ctx, c_ctx, ada_w, ada_b, moe_wg, moe_bg, moe_we, moe_be, moe_w1, moe_w3, moe_w2, ev_w_in, ev_w_out, gla_wa2, gla_ba, gla_norm, s5_lam_re, s5_lam_im, s5_log_dt, s5_b_re, s5_b_im, s5_c_re, s5_c_im, s5_d, s5_w_glu, od_w_in, od_w_out, na_q_norm, na_k_norm, na_rpb, hy_conv_w, hy_conv_b, hy_fw1, hy_fb1, hy_fw2, hy_fb2, hy_fw3, hy_fb3, hy_freq, hy_fw4, hy_bias):
    bsz, seqlen, _ = x.shape
    assert ctx.shape[1] == TB and seqlen % TB == 0

    mods_all = _mods(c, c_ctx, ada_w, ada_b)
    mods = mods_all[0]
    w_in = ev_w_in[0]
    n_a = 2 * GLA_RANK
    a0 = 2 * GLA_QK + 2 * GLA_V
    w_ev = jnp.concatenate([w_in[:, :a0], w_in[:, a0 + n_a:], w_in[:, a0:a0 + n_a],
                            jnp.zeros((D_MODEL, 128 - n_a), F32)], axis=1).astype(BF16)
    wa = jnp.zeros((128, 2 * GLA_QK), F32)
    for d in range(2):
        wa = wa.at[d * GLA_RANK:(d + 1) * GLA_RANK, d * GLA_QK:(d + 1) * GLA_QK].set(gla_wa2[0, d])
    cos, sin = _rope_tables(seqlen)
    q, k, v, g, u, u_sw, la = _ev_proj(ctx, x, mods, w_ev, wa, gla_ba[0].reshape(1, 2 * GLA_QK), cos, sin)
    o_f, o_b = _gla(q, k, v, la)
    s5p = [t[0].astype(F32) for t in (s5_lam_re, s5_lam_im, s5_log_dt, s5_b_re, s5_b_im, s5_c_re, s5_c_im)]
    y_f = _s5_scan(u, _s5_mats(*[t[0] for t in s5p], rev=False), rev=False)
    y_b = _s5_scan(u_sw, _s5_mats(*[t[1] for t in s5p], rev=True), rev=True)
    wr, br = _router_weights(moe_wg[0], moe_bg[0], moe_we[0], moe_be[0])
    x1, h2, lg = _ev_out(o_f, o_b, g, y_f, y_b, u, ctx, x, mods,
                         jnp.tile(gla_norm[0], GLA_HEADS).reshape(1, GLA_V), s5_d[0].reshape(1, S5_W),
                         s5_w_glu[0].astype(BF16), ev_w_out[0].astype(BF16), wr, br)
    xcat = _moe(x1, h2, lg, mods, _mod_index,
                moe_w1, moe_w3, moe_w2, 0)

    mods = mods_all[1]
    hd = np.arange(NA_W) // NA_DH
    gm = jnp.asarray((hd[:, None] == hd[None, :]).astype(np.float32) / NA_DH)
    qh, kh, vh, zh = _od_proj(xcat, mods, od_w_in[0].astype(BF16), gm,
                              jnp.tile(na_q_norm[0], NA_HEADS).reshape(1, NA_W),
                              jnp.tile(na_k_norm[0], NA_HEADS).reshape(1, NA_W))
    na = _na(qh, kh, vh, _na_bias_table(na_rpb[0]))
    x0, uh, ut = _hy_pre(zh, hy_conv_w[0], hy_conv_b[0].reshape(1, 3 * HY_W))
    kfilt = _hy_filter(seqlen, hy_fw1[0], hy_fb1[0], hy_fw2[0], hy_fb2[0], hy_fw3[0], hy_fb3[0],
                       hy_freq[0], hy_fw4[0])
    yt = _hy_conv(ut, kfilt)
    wr, br = _router_weights(moe_wg[1], moe_bg[1], moe_we[1], moe_be[1])
    xl, h2, lg = _od_out(na, yt, x0, uh, xcat, mods, hy_bias[0].reshape(1, HY_W),
                         od_w_out[0].astype(BF16), wr, br)
    return _moe(xl, h2, lg, mods, _latent_mod_index,
                moe_w1, moe_w3, moe_w2, 1)
```

```python
import functools
import math

import numpy as np
import jax
import jax.numpy as jnp
from jax import lax
from jax.experimental import pallas as pl
from jax.experimental.pallas import tpu as pltpu

F32, BF16 = jnp.float32, jnp.bfloat16
HI = lax.Precision.HIGHEST

D_MODEL = 1024
GRID_W = 64
EPS = 1e-6
ROPE_BASE = 10000.0
NEG_INF = -1e30
GLA_HEADS, GLA_DK, GLA_DV = 4, 64, 128
GLA_QK, GLA_V = GLA_HEADS * GLA_DK, GLA_HEADS * GLA_DV
GLA_RANK = 16
GLA_TAU = 16.0
GLA_CHUNK = 64
GLA_LOG_ALPHA_MIN = -1.0
S5_W, S5_H, S5_P = 512, 16, 64
S5_G = S5_W // S5_H
S5_CHUNK = 8
S5_TBLK = 1408
NA_HEADS, NA_DH = 8, 64
NA_W = NA_HEADS * NA_DH
WIN_R, WIN_C = 8, 16
NA_HG = 4
NA_ROWS = 4
HY_W = 512
HY_SHORT = 3
HY_EMB = 33
HY_DECAY_TARGET = 1e-2
HY_DECAY_SHORT_PCT = 0.3
HY_DECAY_LONG_PCT = 1.5
N_GROUPS, EXP_PER_GROUP = 4, 8
N_EXPERTS = N_GROUPS * EXP_PER_GROUP
D_EXPERT = 512
TOP_K = 2
MOE_BLOCK = 256

TB = 256
PROJ_TILE = 768
FFT_N1 = 128
FFT_CG = 16
V7X_VMEM_LIMIT = 52 * 1024 * 1024


def _cparams(sem):
    return pltpu.CompilerParams(dimension_semantics=sem, vmem_limit_bytes=V7X_VMEM_LIMIT)


def _sigmoid(x):
    return 1.0 / (1.0 + jnp.exp(-x))


ROW_TILE = D_MODEL // 128


def _to_token_tiles(ref, val):
    n = val.shape[0]
    for j in range(ROW_TILE):
        ref[pl.ds(j, n, stride=ROW_TILE), :] = val[:, j * 128:(j + 1) * 128]


def _from_token_tiles(ref, n):
    return jnp.concatenate([ref[pl.ds(j, n, stride=ROW_TILE), :] for j in range(ROW_TILE)], axis=1)


def _split_bf16(x):
    hi = x.astype(BF16)
    return hi, (x - hi.astype(F32)).astype(BF16)


def _dot_x3(a, b):
    a_hi, a_lo = _split_bf16(a)
    b_hi, b_lo = _split_bf16(b)
    d = lambda p, q: jnp.dot(p, q, preferred_element_type=F32)
    return d(a_hi, b_hi) + d(a_lo, b_hi) + d(a_hi, b_lo)


def _rms_rows(x):
    return x * lax.rsqrt(jnp.mean(x * x, axis=-1, keepdims=True) + EPS)


def _mm_kernel(a_ref, w_ref, b_ref, f_ref, o_ref, *, use_sin):
    z = jnp.dot(a_ref[...], w_ref[...], precision=HI, preferred_element_type=F32) + b_ref[...]
    if use_sin:
        z = jnp.sin(f_ref[...] * z)
    o_ref[...] = z


def _mm(a, w, bias=None, freq=None):
    m, k = a.shape
    n = w.shape[1]
    mp, kp = -(-m // 8) * 8, -(-k // 128) * 128
    tm = min(mp, 1024)
    mp = -(-mp // tm) * tm
    tn = n if n <= 1024 else 1024
    assert n % tn == 0
    a = jnp.pad(a.astype(F32), ((0, mp - m), (0, kp - k)))
    w = jnp.pad(w.astype(F32), ((0, kp - k), (0, 0)))
    bias = jnp.zeros((n,), F32) if bias is None else bias.astype(F32)
    use_sin = freq is not None
    freq = jnp.ones((n,), F32) if freq is None else freq.astype(F32)
    out = pl.pallas_call(
        functools.partial(_mm_kernel, use_sin=use_sin),
        grid=(mp // tm, n // tn),
        in_specs=[pl.BlockSpec((tm, kp), lambda i, j: (i, 0)),
                  pl.BlockSpec((kp, tn), lambda i, j: (0, j)),
                  pl.BlockSpec((1, tn), lambda i, j: (0, j)),
                  pl.BlockSpec((1, tn), lambda i, j: (0, j))],
        out_specs=pl.BlockSpec((tm, tn), lambda i, j: (i, j)),
        out_shape=jax.ShapeDtypeStruct((mp, n), F32),
        compiler_params=_cparams(("parallel", "parallel")),
        name="small_dense",
    )(a, w, bias.reshape(1, n), freq.reshape(1, n))
    return out[:m]


def _mod_index(b, i):
    return (b, jnp.minimum(i, 1), 0, 0)


def _swapped_index(nblk, b, i):
    return (b, jnp.where(i == 0, nblk - 1, i - 1), 0)


EV_NQ, EV_NK, EV_NV, EV_NG, EV_NU = 0, 256, 512, 1024, 1536
EV_NA = 2048
EV_NTOT = 2176


def _stream_tile(ctx_ref, x_ref):
    return jnp.where(pl.program_id(1) == 0, ctx_ref[...], x_ref[...])


def _stream_specs():
    return [pl.BlockSpec((None, TB, D_MODEL), lambda b, i: (b, 0, 0)),
            pl.BlockSpec((None, TB, D_MODEL), lambda b, i: (b, jnp.maximum(i - 1, 0), 0))]


def _ev_proj_kernel(ctx_ref, x_ref, mod_ref, w_ref, wa_ref, ba_ref, cos_ref, sin_ref,
                    q_ref, k_ref, v_ref, g_ref, u_ref, usw_ref, la_ref):
    x = _stream_tile(ctx_ref, x_ref)
    h = _rms_rows(x) * (1.0 + mod_ref[1:2, :]) + mod_ref[0:1, :]
    z = jnp.dot(h.astype(BF16), w_ref[...], preferred_element_type=F32)
    lane = lax.broadcasted_iota(jnp.int32, (x.shape[0], GLA_QK), 1)
    first = (lane % 32) < 16
    cos, sin = cos_ref[...], sin_ref[...]

    def rot(t):
        partner = jnp.where(first, pltpu.roll(t, GLA_QK - 16, 1), pltpu.roll(t, 16, 1))
        return t * cos + partner * sin

    q_ref[...] = rot(z[:, EV_NQ:EV_NQ + GLA_QK]) * (GLA_DK ** -0.5)
    k_ref[...] = rot(z[:, EV_NK:EV_NK + GLA_QK])
    v_ref[...] = z[:, EV_NV:EV_NV + GLA_V].astype(BF16)
    g_ref[...] = z[:, EV_NG:EV_NG + GLA_V].astype(BF16)
    u_ref[...] = z[:, EV_NU:EV_NU + S5_W]
    usw_ref[...] = z[:, EV_NU:EV_NU + S5_W]
    a = z[:, EV_NA:EV_NA + 128]
    pre = _dot_x3(a, wa_ref[...]) + ba_ref[...]
    ls = jnp.minimum(pre, 0.0) - jnp.log1p(jnp.exp(-jnp.abs(pre)))
    la_ref[...] = jnp.maximum(ls / GLA_TAU, GLA_LOG_ALPHA_MIN)


def _ev_proj(ctx, x, mods, w, wa, ba, cos, sin):
    bsz = x.shape[0]
    lt = ctx.shape[1] + x.shape[1]
    nblk = lt // TB
    tok = lambda n: pl.BlockSpec((None, TB, n), lambda b, i: (b, i, 0))
    const = lambda shp: pl.BlockSpec(shp, lambda b, i: tuple(0 for _ in shp))
    return pl.pallas_call(
        _ev_proj_kernel,
        grid=(bsz, nblk),
        in_specs=_stream_specs() + [
                  pl.BlockSpec((None, None, 6, D_MODEL), _mod_index),
                  const((D_MODEL, EV_NTOT)), const((128, 2 * GLA_QK)), const((1, 2 * GLA_QK)),
                  pl.BlockSpec((TB, GLA_QK), lambda b, i: (i, 0)),
                  pl.BlockSpec((TB, GLA_QK), lambda b, i: (i, 0))],
        out_specs=[tok(GLA_QK), tok(GLA_QK), tok(GLA_V), tok(GLA_V), tok(S5_W),
                   pl.BlockSpec((None, TB, S5_W), functools.partial(_swapped_index, nblk)), tok(2 * GLA_QK)],
        out_shape=[jax.ShapeDtypeStruct((bsz, lt, GLA_QK), F32),
                   jax.ShapeDtypeStruct((bsz, lt, GLA_QK), F32),
                   jax.ShapeDtypeStruct((bsz, lt, GLA_V), BF16),
                   jax.ShapeDtypeStruct((bsz, lt, GLA_V), BF16),
                   jax.ShapeDtypeStruct((bsz, lt, S5_W), F32),
                   jax.ShapeDtypeStruct((bsz, lt, S5_W), F32),
                   jax.ShapeDtypeStruct((bsz, lt, 2 * GLA_QK), F32)],
        compiler_params=_cparams(("parallel", "parallel")),
        name="ev_proj",
    )(ctx, x, mods, w, wa, ba, cos, sin)


def _gla_kernel(qf_ref, kf_ref, vf_ref, laf_ref, qb_ref, kb_ref, vb_ref, lab_ref,
                of_ref, ob_ref, s_scr):
    i = pl.program_id(1)

    @pl.when(i == 0)
    def _():
        s_scr[...] = jnp.zeros_like(s_scr)

    c = GLA_CHUNK
    nh = GLA_HEADS
    row = lax.broadcasted_iota(jnp.int32, (c, c), 0)
    col = lax.broadcasted_iota(jnp.int32, (c, c), 1)
    row4 = lax.broadcasted_iota(jnp.int32, (nh * c, c), 0) % c
    col4 = lax.broadcasted_iota(jnp.int32, (nh * c, c), 1)
    lane_head = lax.broadcasted_iota(jnp.int32, (c, GLA_QK), 1) // GLA_DK
    out_head = lax.broadcasted_iota(jnp.int32, (c, GLA_V), 1) // GLA_DV
    bd_mask = (lax.broadcasted_iota(jnp.int32, (GLA_V, GLA_QK), 0) // GLA_DV
               == lax.broadcasted_iota(jnp.int32, (GLA_V, GLA_QK), 1) // GLA_DK)
    nchunk = qf_ref.shape[0] // c
    nt = (((1,), (1,)), ((), ()))
    tn = (((0,), (0,)), ((), ()))

    def one_chunk(refs, o_ref, d, r0):
        q_ref, k_ref, v_ref, la_ref = refs
        fwd = d == 0
        sl = pl.ds(r0, c)
        qc, kc, vc, lac = q_ref[sl, :], k_ref[sl, :], v_ref[sl, :], la_ref[sl, :]
        tri = ((row >= col) if fwd else (row <= col)).astype(BF16)
        la_hi, la_lo = _split_bf16(lac)
        b = (jnp.dot(tri, la_hi, preferred_element_type=F32)
             + jnp.dot(tri, la_lo, preferred_element_type=F32))
        b_last = b[c - 1:c, :] if fwd else b[0:1, :]
        qe = (qc * jnp.exp(b)).astype(BF16)
        ke = (kc * jnp.exp(-b)).astype(BF16)
        kd = (kc * jnp.exp(b_last - b)).astype(BF16)
        st = s_scr[d]
        o = lax.dot_general(qe, st.astype(BF16), nt, preferred_element_type=F32)
        q4 = jnp.concatenate([jnp.where(lane_head == h, qe, jnp.zeros_like(qe)) for h in range(nh)], axis=0)
        att = lax.dot_general(q4, ke, nt, preferred_element_type=F32)
        att_mask = (row4 >= col4) if fwd else (row4 < col4)
        o4 = jnp.dot(jnp.where(att_mask, att, 0.0).astype(BF16), vc, preferred_element_type=F32)
        for h in range(nh):
            o = o + jnp.where(out_head == h, o4[h * c:(h + 1) * c, :], 0.0)
        o_ref[sl, :] = o.astype(o_ref.dtype)
        upd_t = lax.dot_general(vc, kd, tn, preferred_element_type=F32)
        s_scr[d] = st * jnp.exp(b_last) + jnp.where(bd_mask, upd_t, 0.0)

    def body(j, carry):
        one_chunk((qf_ref, kf_ref, vf_ref, laf_ref), of_ref, 0, pl.multiple_of(j * c, c))
        one_chunk((qb_ref, kb_ref, vb_ref, lab_ref), ob_ref, 1, pl.multiple_of((nchunk - 1 - j) * c, c))
        return carry

    lax.fori_loop(0, nchunk, body, 0, unroll=True)


def _gla(q, k, v, la):
    bsz, lt, _ = q.shape
    nblk = lt // TB
    fwd_map = lambda b, i: (b, i, 0)
    bwd_blk = lambda i: jnp.where(i == 0, 0, nblk - i)
    bwd_map = lambda b, i: (b, bwd_blk(i), 0)
    bwd_map_la = lambda b, i: (b, bwd_blk(i), 1)
    spec = lambda n, m: pl.BlockSpec((None, TB, n), m)
    return pl.pallas_call(
        _gla_kernel,
        grid=(bsz, nblk),
        in_specs=[spec(GLA_QK, fwd_map), spec(GLA_QK, fwd_map), spec(GLA_V, fwd_map), spec(GLA_QK, fwd_map),
                  spec(GLA_QK, bwd_map), spec(GLA_QK, bwd_map), spec(GLA_V, bwd_map), spec(GLA_QK, bwd_map_la)],
        out_specs=[spec(GLA_V, fwd_map), spec(GLA_V, bwd_map)],
        out_shape=[jax.ShapeDtypeStruct((bsz, lt, GLA_V), BF16), jax.ShapeDtypeStruct((bsz, lt, GLA_V), BF16)],
        scratch_shapes=[pltpu.VMEM((2, GLA_V, GLA_QK), F32)],
        compiler_params=_cparams(("parallel", "arbitrary")),
        name="gla_scan",
    )(q, k, v, la, q, k, v, la)


def _s5_kernel(u_ref, wm_ref, tm_ref, cm_ref, ar_ref, ai_ref, y_ref, w_scr, hp_scr, h_scr, *, rev):
    @pl.when(pl.program_id(1) == 0)
    def _():
        h_scr[...] = jnp.zeros_like(h_scr)

    bsz, ntok, _ = u_ref.shape
    nc = ntok // S5_CHUNK
    half = 8 * S5_P
    x = jnp.concatenate(
        [jnp.concatenate([u_ref[b, pl.ds(s, nc, stride=S5_CHUNK), :] for s in range(S5_CHUNK)], axis=1)
         for b in range(bsz)], axis=0).astype(BF16)
    w_scr[...] = jnp.dot(x, wm_ref[...], preferred_element_type=F32)
    ar, ai = ar_ref[...], ai_ref[...]

    def body(j, hs):
        c = (nc - 1 - j) if rev else j
        out = []
        for b in range(bsz):
            re, im = hs[b]
            r = b * nc + c
            hp_scr[pl.ds(r, 1), :] = jnp.concatenate([re, im], axis=1)
            w = w_scr[pl.ds(r, 1), :]
            out.append((ar * re - ai * im + w[:, :half], ar * im + ai * re + w[:, half:]))
        return tuple(out)

    hs = lax.fori_loop(0, nc, body, tuple((h_scr[b:b + 1, :half], h_scr[b:b + 1, half:]) for b in range(bsz)))
    for b in range(bsz):
        h_scr[b:b + 1, :] = jnp.concatenate(hs[b], axis=1)
    mt = 256
    ntile = x.shape[1] // mt
    cols = []
    for jt in range(ntile):
        acc = None
        for it in (range(jt, ntile) if rev else range(jt + 1)):
            part = jnp.dot(x[:, it * mt:(it + 1) * mt], tm_ref[it * mt:(it + 1) * mt, jt * mt:(jt + 1) * mt],
                           preferred_element_type=F32)
            acc = part if acc is None else acc + part
        cols.append(acc)
    y = (jnp.concatenate(cols, axis=1)
         + jnp.dot(hp_scr[...].astype(BF16), cm_ref[...], preferred_element_type=F32))
    for b in range(bsz):
        for s in range(S5_CHUNK):
            y_ref[b, pl.ds(s, nc, stride=S5_CHUNK), :] = y[b * nc:(b + 1) * nc, s * 128:(s + 1) * 128]


def _s5_mats(lam_re, lam_im, log_dt, b_re, b_im, c_re, c_im, rev):
    t16 = S5_CHUNK
    dt = jnp.exp(log_dt)[:, None]
    mag = jnp.exp(lam_re * dt)
    a_re, a_im = mag * jnp.cos(lam_im * dt), mag * jnp.sin(lam_im * dt)
    den = lam_re * lam_re + lam_im * lam_im
    nr = a_re - 1.0
    co_re = ((nr * lam_re + a_im * lam_im) / den)[..., None]
    co_im = ((a_im * lam_re - nr * lam_im) / den)[..., None]
    bb_re, bb_im = co_re * b_re - co_im * b_im, co_re * b_im + co_im * b_re
    pr, pi = [jnp.ones_like(a_re)], [jnp.zeros_like(a_im)]
    for _ in range(t16):
        pr, pi = pr + [pr[-1] * a_re - pi[-1] * a_im], pi + [pr[-1] * a_im + pi[-1] * a_re]
    pw_re, pw_im = jnp.stack(pr), jnp.stack(pi)
    g = lam_re.shape[0]
    e_re, e_im = pw_re[t16 - 1::-1][:t16], pw_im[t16 - 1::-1][:t16]
    wre = jnp.einsum('sgp,gph->gshp', e_re, bb_re) - jnp.einsum('sgp,gph->gshp', e_im, bb_im)
    wim = jnp.einsum('sgp,gph->gshp', e_re, bb_im) + jnp.einsum('sgp,gph->gshp', e_im, bb_re)
    cb_re = jnp.einsum('gkp,gph->gpkh', c_re, bb_re) - jnp.einsum('gkp,gph->gpkh', c_im, bb_im)
    cb_im = jnp.einsum('gkp,gph->gpkh', c_re, bb_im) + jnp.einsum('gkp,gph->gpkh', c_im, bb_re)
    kd = jnp.einsum('dgp,gpkh->dgkh', pw_re[:t16], cb_re) - jnp.einsum('dgp,gpkh->dgkh', pw_im[:t16], cb_im)
    lag = np.arange(t16)[None, :] - np.arange(t16)[:, None]
    toe = jnp.where((lag >= 0)[:, :, None, None, None], kd[np.clip(lag, 0, t16 - 1)], 0.0)
    toe = toe.transpose(2, 0, 4, 1, 3)
    q_re, q_im = pw_re[1:], pw_im[1:]
    ca_re = jnp.einsum('gkp,tgp->gptk', c_re, q_re) - jnp.einsum('gkp,tgp->gptk', c_im, q_im)
    ca_im = jnp.einsum('gkp,tgp->gptk', c_re, q_im) + jnp.einsum('gkp,tgp->gptk', c_im, q_re)
    if rev:
        wre, wim = wre[:, ::-1], wim[:, ::-1]
        toe = toe[:, ::-1, :, ::-1]
        ca_re, ca_im = ca_re[:, :, ::-1], ca_im[:, :, ::-1]
    nq, gl = g // 8, 8
    nd = t16 * S5_H
    tok_hot = np.zeros((gl, nd, t16 * 128), np.float32)
    st_hot = np.zeros((gl, 2 * S5_P, 2 * gl * S5_P), np.float32)
    for gi in range(gl):
        a = np.arange(nd)
        tok_hot[gi, a, (a // S5_H) * 128 + gi * S5_H + a % S5_H] = 1.0
        a = np.arange(2 * S5_P)
        st_hot[gi, a, (a // S5_P) * gl * S5_P + gi * S5_P + a % S5_P] = 1.0
    place = lambda rows, blk, cols: jnp.einsum('gar,qgab,gbc->qrc', rows, blk, cols)
    wg = jnp.concatenate([wre, wim], axis=-1).reshape(nq, gl, nd, 2 * S5_P)
    tg = toe.reshape(nq, gl, nd, nd)
    cg = jnp.concatenate([ca_re, -ca_im], axis=1).reshape(nq, gl, 2 * S5_P, nd)
    wm = place(tok_hot, wg, st_hot)
    tmat = place(tok_hot, tg, tok_hot)
    cm = place(st_hot, cg, tok_hot)
    return (wm.astype(BF16), tmat.astype(BF16), cm.astype(BF16),
            pw_re[t16].reshape(nq, 1, gl * S5_P), pw_im[t16].reshape(nq, 1, gl * S5_P))


def _s5_scan(u, mats, rev):
    bsz, lt, _ = u.shape
    tblk = max(d for d in range(64, S5_TBLK + 1, 64) if lt % d == 0)
    nblk = lt // tblk
    nq = S5_G // 8
    wm, tmat, cm, ar, ai = mats
    kw, ks = S5_CHUNK * 128, 8 * S5_P
    tmap = (lambda q, t: (0, nblk - 1 - t, q)) if rev else (lambda q, t: (0, t, q))
    per = lambda shp: pl.BlockSpec((None,) + shp, lambda q, t: (q,) + tuple(0 for _ in shp))
    rows = bsz * tblk // S5_CHUNK
    return pl.pallas_call(
        functools.partial(_s5_kernel, rev=rev),
        grid=(nq, nblk),
        in_specs=[pl.BlockSpec((bsz, tblk, 128), tmap),
                  per((kw, 2 * ks)), per((kw, kw)), per((2 * ks, kw)), per((1, ks)), per((1, ks))],
        out_specs=pl.BlockSpec((bsz, tblk, 128), tmap),
        out_shape=jax.ShapeDtypeStruct((bsz, lt, S5_W), F32),
        scratch_shapes=[pltpu.VMEM((rows, 2 * ks), F32), pltpu.VMEM((rows, 2 * ks), F32),
                        pltpu.VMEM((8, 2 * ks), F32)],
        compiler_params=_cparams(("parallel", "arbitrary")),
        name="s5_scan",
    )(u, wm, tmat, cm, ar, ai)


def _mixer_tail(x, mix, mod_ref, wr_ref, br_ref, x_out, h2_out, lg_out):
    xn = x + mod_ref[2:3, :] * mix
    x_out[...] = xn
    h2 = _rms_rows(xn) * (1.0 + mod_ref[4:5, :]) + mod_ref[3:4, :]
    _to_token_tiles(h2_out, h2)
    lg_out[...] = _dot_x3(h2, wr_ref[...]) + br_ref[...]


def _ev_out_kernel(of_ref, ob_ref, g_ref, yf_ref, yb_ref, u_ref, ctx_ref, x_ref, mod_ref,
                   gn_ref, ds_ref, wglu_ref, wout_ref, wr_ref, br_ref,
                   x_out, h2_out, lg_out):
    o = of_ref[...].astype(F32) + ob_ref[...].astype(F32)
    og = jnp.concatenate([_rms_rows(o[:, h * GLA_DV:(h + 1) * GLA_DV]) for h in range(GLA_HEADS)], axis=1)
    g = g_ref[...].astype(F32)
    og = og * gn_ref[...] * (g * _sigmoid(g))
    t = yf_ref[...] + yb_ref[...] + ds_ref[...] * u_ref[...]
    y = t * (0.5 * (1.0 + jnp.tanh(math.sqrt(2.0 / math.pi) * (t + 0.044715 * (t * t * t)))))
    y = y * _sigmoid(jnp.dot(y.astype(BF16), wglu_ref[...], preferred_element_type=F32))
    cat = jnp.concatenate([og, y], axis=1).astype(BF16)
    mix = jnp.dot(cat, wout_ref[...], preferred_element_type=F32)
    _mixer_tail(_stream_tile(ctx_ref, x_ref), mix, mod_ref, wr_ref, br_ref, x_out, h2_out, lg_out)


def _ev_out(o_f, o_b, g, y_f, y_b, u, ctx, x, mods, gn, ds, wglu, wout, wr, br):
    bsz, lt, _ = u.shape
    nblk = lt // TB
    tok = lambda n: pl.BlockSpec((None, TB, n), lambda b, i: (b, i, 0))
    const = lambda shp: pl.BlockSpec(shp, lambda b, i: tuple(0 for _ in shp))
    return pl.pallas_call(
        _ev_out_kernel,
        grid=(bsz, nblk),
        in_specs=[tok(512), tok(512), tok(512), tok(512),
                  pl.BlockSpec((None, TB, S5_W), functools.partial(_swapped_index, nblk)),
                  tok(512)] + _stream_specs() + [
                  pl.BlockSpec((None, None, 6, D_MODEL), _mod_index),
                  const((1, 512)), const((1, 512)), const((512, 512)), const((D_MODEL, D_MODEL)),
                  const((D_MODEL, 128)), const((1, 128))],
        out_specs=[tok(D_MODEL), pl.BlockSpec((TB * ROW_TILE, 128), lambda b, i: (b * nblk + i, 0)), tok(128)],
        out_shape=[jax.ShapeDtypeStruct((bsz, lt, D_MODEL), F32),
                   jax.ShapeDtypeStruct((bsz * lt * ROW_TILE, 128), F32),
                   jax.ShapeDtypeStruct((bsz, lt, 128), F32)],
        compiler_params=_cparams(("parallel", "parallel")),
        name="ev_out",
    )(o_f, o_b, g, y_f, y_b, u, ctx, x, mods, gn, ds, wglu, wout, wr, br)


def _route(logits):
    n_tok = logits.shape[0]
    lg = logits[:, :N_GROUPS]
    grp = jnp.argmax(lg, axis=-1)
    g_w = jnp.max(jax.nn.softmax(lg, axis=-1), axis=-1, keepdims=True)
    le = logits[:, N_GROUPS:N_GROUPS + N_EXPERTS].reshape(n_tok, N_GROUPS, EXP_PER_GROUP)
    le = le[jnp.arange(n_tok), grp]
    top_p, top_i = lax.top_k(jax.nn.softmax(le, axis=-1), TOP_K)
    gate = g_w * top_p / jnp.sum(top_p, axis=-1, keepdims=True)
    eid = (grp[:, None] * EXP_PER_GROUP + top_i).reshape(-1)
    n_asg = n_tok * TOP_K
    order = jnp.argsort(eid).astype(jnp.int32)
    counts = jnp.sum((eid[:, None] == jnp.arange(N_EXPERTS)[None, :]).astype(jnp.int32), axis=0)
    padded = (counts + MOE_BLOCK - 1) // MOE_BLOCK * MOE_BLOCK
    pad_end = jnp.cumsum(padded)
    pad_start = pad_end - padded
    cnt_start = jnp.cumsum(counts) - counts
    n_blocks = -(-n_asg // MOE_BLOCK) + N_EXPERTS
    blk_start = jnp.arange(n_blocks, dtype=jnp.int32) * MOE_BLOCK
    blk_e = jnp.minimum(jnp.sum((pad_end[None, :] <= blk_start[:, None]).astype(jnp.int32), axis=1), N_EXPERTS - 1)
    pos = jnp.arange(n_blocks * MOE_BLOCK, dtype=jnp.int32)
    pos_e = jnp.repeat(blk_e, MOE_BLOCK)
    rank = pos - pad_start[pos_e]
    src = jnp.clip(cnt_start[pos_e] + rank, 0, n_asg - 1)
    slot_buf = jnp.where(rank < counts[pos_e], order[src], n_asg).astype(jnp.int32)
    n_valid = jnp.sum((slot_buf < n_asg).reshape(n_blocks, MOE_BLOCK), axis=1).astype(jnp.int32)
    tok = lax.shift_right_logical(slot_buf, 1)
    src_rows = jnp.minimum(tok, n_tok - 1) * ROW_TILE
    spare = n_asg + (pos // MOE_BLOCK % 2) * MOE_BLOCK + pos % MOE_BLOCK
    dst_rows = jnp.where(slot_buf < n_asg, (slot_buf & 1) * n_tok + tok, spare)
    lead = n_asg + jnp.arange(2 * MOE_BLOCK, dtype=jnp.int32)
    dst_rows = jnp.concatenate([lead, dst_rows]) * ROW_TILE
    return src_rows, dst_rows, blk_e, n_valid, gate.astype(F32)


def _moe_kernel(src_ref, dst_ref, blke_ref, nvalid_ref, h_hbm, w1_ref, w3_ref, w2_ref, z_hbm,
                xbuf0, xbuf1, ybuf0, ybuf1, wb1, wb3, wb2, gsem, ssem):
    i = pl.program_id(0)
    nblk = pl.num_programs(0)
    ns = ROW_TILE
    xb, yb = (xbuf0, xbuf1), (ybuf0, ybuf1)
    lead = 2 * MOE_BLOCK

    def issue_gather(blk, buf):
        base = blk * MOE_BLOCK
        for r in range(MOE_BLOCK):
            src = pl.multiple_of(src_ref[base + r], ns)
            pltpu.make_async_copy(h_hbm.at[pl.ds(src, ns)], xb[buf].at[pl.ds(r * ns, ns)],
                                  gsem.at[buf]).start(priority=r % 2)

    def issue_scatter(blk, buf):
        base = lead + blk * MOE_BLOCK
        for r in range(MOE_BLOCK):
            dst = pl.multiple_of(dst_ref[base + r], ns)
            pltpu.make_async_copy(yb[buf].at[pl.ds(r * ns, ns)], z_hbm.at[pl.ds(dst, ns)],
                                  ssem.at[buf]).start(priority=r % 2)

    def wait_gather(buf):
        pltpu.make_async_copy(h_hbm.at[pl.ds(0, MOE_BLOCK * ns)], xb[buf], gsem.at[buf]).wait()

    def wait_scatter(buf):
        pltpu.make_async_copy(yb[buf], z_hbm.at[pl.ds(0, MOE_BLOCK * ns)], ssem.at[buf]).wait()

    used = nvalid_ref[i] > 0

    @pl.when(i == 0)
    def _():
        ybuf0[...] = jnp.zeros_like(ybuf0)
        ybuf1[...] = jnp.zeros_like(ybuf1)
        issue_scatter(-2, 0)
        issue_gather(0, 0)

    def step(cur):
        oth = 1 - cur
        wait_gather(cur)
        issue_gather(jnp.minimum(i + 1, nblk - 1), oth)
        issue_scatter(i - 1, oth)
        x = _from_token_tiles(xb[cur], MOE_BLOCK).astype(BF16)

        @pl.when(jnp.logical_or(i == 0, blke_ref[i] != blke_ref[jnp.maximum(i - 1, 0)]))
        def _():
            wb1[...] = w1_ref[...].astype(BF16)
            wb3[...] = w3_ref[...].astype(BF16)
            wb2[...] = w2_ref[...].astype(BF16)

        h1 = jnp.dot(x, wb1[...], preferred_element_type=F32)
        h3 = jnp.dot(x, wb3[...], preferred_element_type=F32)
        a = (h1 * _sigmoid(h1) * h3).astype(BF16)
        y = jnp.dot(a, wb2[...], preferred_element_type=F32)
        wait_scatter(cur)
        _to_token_tiles(yb[cur], y)

    def drain(last_par):
        wait_gather(1 - last_par)
        issue_scatter(jnp.where(used, i, i - 1), last_par)
        wait_scatter(1 - last_par)
        wait_scatter(last_par)

    for par in range(2):
        @pl.when(jnp.logical_and(used, i % 2 == par))
        def _():
            step(par)

    first_unused = jnp.logical_and(jnp.logical_not(used),
                                   jnp.logical_and(i > 0, nvalid_ref[jnp.maximum(i - 1, 0)] > 0))
    last_used = jnp.logical_and(used, i == nblk - 1)
    for par in range(2):
        @pl.when(jnp.logical_or(jnp.logical_and(first_unused, (i - 1) % 2 == par),
                                jnp.logical_and(last_used, i % 2 == par)))
        def _():
            drain(par)


def _moe_experts(h2, src_rows, dst_rows, blk_e, n_valid, w1, w3, w2, layer):
    n_tok = h2.shape[0] // ROW_TILE
    n_blocks = blk_e.shape[0]
    wspec = lambda shp: pl.BlockSpec((None, None) + shp, lambda i, src, dst, blke, nvalid: (layer, blke[i], 0, 0))
    grid_spec = pltpu.PrefetchScalarGridSpec(
        num_scalar_prefetch=4,
        grid=(n_blocks,),
        in_specs=[pl.BlockSpec(memory_space=pl.ANY),
                  wspec((D_MODEL, D_EXPERT)), wspec((D_MODEL, D_EXPERT)), wspec((D_EXPERT, D_MODEL))],
        out_specs=pl.BlockSpec(memory_space=pl.ANY),
        scratch_shapes=[pltpu.VMEM((MOE_BLOCK * ROW_TILE, 128), F32), pltpu.VMEM((MOE_BLOCK * ROW_TILE, 128), F32),
                        pltpu.VMEM((MOE_BLOCK * ROW_TILE, 128), F32), pltpu.VMEM((MOE_BLOCK * ROW_TILE, 128), F32),
                        pltpu.VMEM((D_MODEL, D_EXPERT), BF16), pltpu.VMEM((D_MODEL, D_EXPERT), BF16),
                        pltpu.VMEM((D_EXPERT, D_MODEL), BF16),
                        pltpu.SemaphoreType.DMA((2,)), pltpu.SemaphoreType.DMA((2,))])
    return pl.pallas_call(
        _moe_kernel,
        grid_spec=grid_spec,
        out_shape=jax.ShapeDtypeStruct(((TOP_K * n_tok + 2 * MOE_BLOCK) * ROW_TILE, 128), F32),
        compiler_params=_cparams(("arbitrary",)),
        name="moe_experts",
    )(src_rows, dst_rows, blk_e, n_valid, h2, w1, w3, w2)


def _moe_combine_kernel(x_ref, z0_ref, z1_ref, gate_ref, mod_ref, o_ref):
    gate = gate_ref[...]
    tb = x_ref.shape[0]
    y = gate[:, 0:1] * _from_token_tiles(z0_ref, tb) + gate[:, 1:2] * _from_token_tiles(z1_ref, tb)
    o_ref[...] = x_ref[...] + mod_ref[5:6, :] * y


def _moe_combine(x, z, gate, mods, mod_index):
    bsz, lt, _ = x.shape
    nblk = lt // TB
    gate3 = gate.reshape(bsz, lt, TOP_K)
    return pl.pallas_call(
        _moe_combine_kernel,
        grid=(bsz, nblk),
        in_specs=[pl.BlockSpec((None, TB, D_MODEL), lambda b, i: (b, i, 0)),
                  pl.BlockSpec((TB * ROW_TILE, 128), lambda b, i: (b * nblk + i, 0)),
                  pl.BlockSpec((TB * ROW_TILE, 128), lambda b, i: ((bsz + b) * nblk + i, 0)),
                  pl.BlockSpec((None, TB, TOP_K), lambda b, i: (b, i, 0)),
                  pl.BlockSpec((None, None, 6, D_MODEL), mod_index)],
        out_specs=pl.BlockSpec((None, TB, D_MODEL), lambda b, i: (b, i, 0)),
        out_shape=jax.ShapeDtypeStruct((bsz, lt, D_MODEL), F32),
        compiler_params=_cparams(("parallel", "parallel")),
        name="moe_combine",
    )(x, z, z, gate3, mods)


def _moe(x, h2, logits, mods, mod_index, w1, w3, w2, layer):
    bsz, lt, _ = x.shape
    src_rows, dst_rows, blk_e, n_valid, gate = _route(logits.reshape(bsz * lt, 128))
    z = _moe_experts(h2, src_rows, dst_rows, blk_e, n_valid, w1, w3, w2, layer)
    return _moe_combine(x, z, gate, mods, mod_index)


def _od_proj_kernel(x_ref, mod_ref, w_ref, gm_ref, qn_ref, kn_ref, q_ref, k_ref, v_ref, zh_ref):
    tm = x_ref.shape[0]
    is_ctx = pl.program_id(1) * tm + lax.broadcasted_iota(jnp.int32, (tm, 1), 0) < TB
    shift = jnp.where(is_ctx, mod_ref[0, 0:1, :], mod_ref[1, 0:1, :])
    scale = jnp.where(is_ctx, mod_ref[0, 1:2, :], mod_ref[1, 1:2, :])
    h = _rms_rows(x_ref[...]) * (1.0 + scale) + shift
    z = jnp.dot(h.astype(BF16), w_ref[...], preferred_element_type=F32)

    def head_norm(t, gain):
        sq_hi, sq_lo = _split_bf16(t * t)
        gm = gm_ref[...].astype(BF16)
        ms = jnp.dot(sq_hi, gm, preferred_element_type=F32) + jnp.dot(sq_lo, gm, preferred_element_type=F32)
        return t * lax.rsqrt(ms + EPS) * gain

    q_ref[...] = (head_norm(z[:, :NA_W], qn_ref[...]) * (NA_DH ** -0.5)).astype(BF16)
    k_ref[...] = head_norm(z[:, NA_W:2 * NA_W], kn_ref[...]).astype(BF16)
    v_ref[...] = z[:, 2 * NA_W:3 * NA_W].astype(BF16)
    zh_ref[...] = z[:, 3 * NA_W:]


def _od_proj(xcat, mods, w, gm, qn, kn):
    bsz, lt, _ = xcat.shape
    tm = max(d for d in range(TB, PROJ_TILE + 1, TB) if lt % d == 0)
    nblk = lt // tm
    tok = lambda n: pl.BlockSpec((None, tm, n), lambda b, i: (b, i, 0))
    const = lambda shp: pl.BlockSpec(shp, lambda b, i: tuple(0 for _ in shp))
    return pl.pallas_call(
        _od_proj_kernel,
        grid=(bsz, nblk),
        in_specs=[tok(D_MODEL), pl.BlockSpec((None, 2, 6, D_MODEL), lambda b, i: (b, 0, 0, 0)),
                  const((D_MODEL, 3 * NA_W + 3 * HY_W)), const((NA_W, NA_W)), const((1, NA_W)), const((1, NA_W))],
        out_specs=[tok(NA_W), tok(NA_W), tok(NA_W), tok(3 * HY_W)],
        out_shape=[jax.ShapeDtypeStruct((bsz, lt, NA_W), BF16), jax.ShapeDtypeStruct((bsz, lt, NA_W), BF16),
                   jax.ShapeDtypeStruct((bsz, lt, NA_W), BF16), jax.ShapeDtypeStruct((bsz, lt, 3 * HY_W), F32)],
        compiler_params=_cparams(("parallel", "parallel")),
        name="od_proj",
    )(xcat, mods, w, gm, qn, kn)


def _na_kernel(q_ref, k_ref, v_ref, t2_ref, o_ref):
    for j in range(NA_ROWS):
        _na_one_row(q_ref, k_ref, v_ref, t2_ref, o_ref, j)


def _na_one_row(q_ref, k_ref, v_ref, t2_ref, o_ref, j):
    r = pl.program_id(1) * NA_ROWS + j
    n_rows = pl.num_programs(1) * NA_ROWS
    r0 = jnp.clip(r - WIN_R // 2, 0, n_rows - WIN_R)
    off = r0 - r + WIN_R - 1
    base = pl.multiple_of(TB + r0 * GRID_W, GRID_W)
    nloc = WIN_R * GRID_W
    q = q_ref[j * GRID_W:(j + 1) * GRID_W, :]
    hg = NA_HG
    gw = hg * NA_DH
    lane_head = lax.broadcasted_iota(jnp.int32, (GRID_W, gw), 1) // NA_DH
    nt = (((1,), (1,)), ((), ()))
    outs = []
    for grp in range(NA_HEADS // hg):
        cs = slice(gw * grp, gw * (grp + 1))
        q2 = q[:, cs]
        q4 = jnp.concatenate([jnp.where(lane_head == h, q2, jnp.zeros_like(q2)) for h in range(hg)], axis=0)
        kw, vw = k_ref[pl.ds(base, nloc), cs], v_ref[pl.ds(base, nloc), cs]
        kc, vc = k_ref[0:TB, cs], v_ref[0:TB, cs]
        bias = jnp.concatenate(
            [jnp.concatenate([t2_ref[hg * grp + h, off + 2 * m] for m in range(WIN_R // 2)], axis=1)
             for h in range(hg)], axis=0)
        s_loc = lax.dot_general(q4, kw, nt, preferred_element_type=F32) + bias
        s_ctx = lax.dot_general(q4, kc, nt, preferred_element_type=F32)
        m = jnp.maximum(jnp.max(s_loc, axis=1, keepdims=True), jnp.max(s_ctx, axis=1, keepdims=True))
        p_loc, p_ctx = jnp.exp(s_loc - m), jnp.exp(s_ctx - m)
        den = jnp.sum(p_loc, axis=1, keepdims=True) + jnp.sum(p_ctx, axis=1, keepdims=True)
        o4 = (jnp.dot(p_loc.astype(BF16), vw, preferred_element_type=F32)
              + jnp.dot(p_ctx.astype(BF16), vc, preferred_element_type=F32)) / den
        acc = jnp.zeros((GRID_W, gw), F32)
        for h in range(hg):
            acc = jnp.where(lane_head == h, o4[h * GRID_W:(h + 1) * GRID_W, :], acc)
        outs.append(acc)
    o_ref[j * GRID_W:(j + 1) * GRID_W, :] = jnp.concatenate(outs, axis=1)


def _na_bias_table(rpb):
    qc = np.arange(GRID_W)[:, None]
    kc = np.arange(GRID_W)[None, :]
    q_start = np.clip(qc - WIN_C // 2, 0, GRID_W - WIN_C)
    valid = (kc >= q_start) & (kc < q_start + WIN_C)
    col_idx = np.clip(kc - qc + WIN_C - 1, 0, 2 * WIN_C - 2)
    t = jnp.where(valid[None, None], rpb.astype(F32)[:, :, col_idx], NEG_INF)
    return jnp.concatenate([t[:, :-1], t[:, 1:]], axis=-1)


def _na(q, k, v, t2):
    bsz, lt, _ = q.shape
    qrows = NA_ROWS * GRID_W
    n_rows = (lt - TB) // qrows
    qoff = TB // qrows
    return pl.pallas_call(
        _na_kernel,
        grid=(bsz, n_rows),
        in_specs=[pl.BlockSpec((None, qrows, NA_W), lambda b, r: (b, r + qoff, 0)),
                  pl.BlockSpec((None, lt, NA_W), lambda b, r: (b, 0, 0)),
                  pl.BlockSpec((None, lt, NA_W), lambda b, r: (b, 0, 0)),
                  pl.BlockSpec(t2.shape, lambda b, r: (0, 0, 0, 0))],
        out_specs=pl.BlockSpec((None, qrows, NA_W), lambda b, r: (b, r, 0)),
        out_shape=jax.ShapeDtypeStruct((bsz, lt - TB, NA_W), F32),
        compiler_params=_cparams(("parallel", "arbitrary")),
        name="na_attn",
    )(q, k, v, t2)


def _hy_pre_kernel(z_ref, zp_ref, zn_ref, cw_ref, cb_ref, x0_ref, u_ref, ut_ref):
    i = pl.program_id(1)
    n = pl.num_programs(1)
    z = z_ref[...]
    tb = z.shape[0]
    prev_row = jnp.where(i > 0, zp_ref[7:8, :], 0.0)
    next_row = jnp.where(i < n - 1, zn_ref[0:1, :], 0.0)
    rowid = lax.broadcasted_iota(jnp.int32, z.shape, 0)
    zm = jnp.where(rowid == 0, prev_row, pltpu.roll(z, 1, 0))
    zp = jnp.where(rowid == tb - 1, next_row, pltpu.roll(z, tb - 1, 0))
    zc = cb_ref[...] + cw_ref[0:1, :] * zm
    zc = zc + cw_ref[1:2, :] * z
    zc = zc + cw_ref[2:3, :] * zp
    x0_ref[...] = zc[:, :HY_W]
    u = zc[:, HY_W:2 * HY_W] * zc[:, 2 * HY_W:]
    u_ref[...] = u
    for j in range(tb // FFT_N1):
        ut_ref[j] = u[j * FFT_N1:(j + 1) * FFT_N1, :].T.astype(BF16)


def _hy_pre(zh, cw, cb):
    bsz, lt, _ = zh.shape
    l = lt - TB
    nblk = l // TB
    h8 = TB // 8
    return pl.pallas_call(
        _hy_pre_kernel,
        grid=(bsz, nblk),
        in_specs=[pl.BlockSpec((None, TB, 3 * HY_W), lambda b, i: (b, i + 1, 0)),
                  pl.BlockSpec((None, 8, 3 * HY_W), lambda b, i: (b, (i + 1) * h8 - 1, 0)),
                  pl.BlockSpec((None, 8, 3 * HY_W), lambda b, i: (b, jnp.minimum((i + 2) * h8, lt // 8 - 1), 0)),
                  pl.BlockSpec((HY_SHORT, 3 * HY_W), lambda b, i: (0, 0)),
                  pl.BlockSpec((1, 3 * HY_W), lambda b, i: (0, 0))],
        out_specs=[pl.BlockSpec((None, TB, HY_W), lambda b, i: (b, i, 0)),
                   pl.BlockSpec((None, TB, HY_W), lambda b, i: (b, i, 0)),
                   pl.BlockSpec((None, TB // FFT_N1, HY_W, FFT_N1), lambda b, i: (b, i, 0, 0))],
        out_shape=[jax.ShapeDtypeStruct((bsz, l, HY_W), F32), jax.ShapeDtypeStruct((bsz, l, HY_W), F32),
                   jax.ShapeDtypeStruct((bsz, l // FFT_N1, HY_W, FFT_N1), BF16)],
        compiler_params=_cparams(("parallel", "parallel")),
        name="hy_pre",
    )(zh, zh, zh, cw, cb)


def _fft_consts(n1_in):
    n = FFT_N1
    idx = np.arange(n)
    ang1 = 2.0 * np.pi * np.outer(idx, idx) / n
    c, s = np.cos(ang1), np.sin(ang1)
    angt = 2.0 * np.pi * np.outer(idx, idx) / (n * n)
    tw = np.concatenate([np.cos(angt), -np.sin(angt)], axis=1)
    f3 = np.block([[c, -s], [s, c]])
    f3i = np.block([[c, s], [-s, c]])
    ch, sh = c[:, :n1_in], s[:, :n1_in]
    f1_pair = np.block([[ch, sh], [-sh, ch]])
    f1_real = np.concatenate([c, -s], axis=0)
    f1i = np.block([[ch.T, -sh.T], [sh.T, ch.T]]) / (n * n)
    return tw, f3, f3i, f1_pair, f1_real, f1i


def _fft_forward(a, tw_re, tw_im, lhs_scr, ncg):
    for cix in range(ncg):
        cs = slice(cix * FFT_N1, (cix + 1) * FFT_N1)
        are, aim = a[:FFT_N1, cs], a[FFT_N1:, cs]
        lhs_scr[cs, :FFT_N1] = are * tw_re - aim * tw_im
        lhs_scr[cs, FFT_N1:] = are * tw_im + aim * tw_re


def _hy_filt_kernel(k_ref, f1_ref, tw_ref, f3_ref, o_ref, lhs_scr):
    a = _dot_x3(f1_ref[...], k_ref[...])
    _fft_forward(a, tw_ref[:, :FFT_N1], tw_ref[:, FFT_N1:], lhs_scr, FFT_CG)
    o_ref[...] = _dot_x3(lhs_scr[...], f3_ref[...])


def _hy_fft_kernel(u_ref, kf_ref, f1_ref, tw_ref, f3_ref, f3i_ref, f1i_ref, y_ref, lhs_scr, a2_scr):
    cgl = FFT_CG * FFT_N1
    x = u_ref[...].reshape(2 * u_ref.shape[1], cgl)
    a = jnp.dot(f1_ref[...], x, preferred_element_type=F32)
    tw_re, tw_im = tw_ref[:, :FFT_N1], tw_ref[:, FFT_N1:]
    _fft_forward(a, tw_re, tw_im, lhs_scr, FFT_CG)
    y = jnp.dot(lhs_scr[...].astype(BF16), f3_ref[...], preferred_element_type=F32)
    yre, yim = y[:, :FFT_N1], y[:, FFT_N1:]
    kre, kim = kf_ref[:, :FFT_N1], kf_ref[:, FFT_N1:]
    z = jnp.concatenate([yre * kre - yim * kim, yre * kim + yim * kre], axis=1).astype(BF16)
    bp = jnp.dot(z, f3i_ref[...], preferred_element_type=F32)
    for cix in range(FFT_CG):
        cs = slice(cix * FFT_N1, (cix + 1) * FFT_N1)
        bre, bim = bp[cs, :FFT_N1], bp[cs, FFT_N1:]
        a2_scr[:FFT_N1, cs] = bre * tw_re + bim * tw_im
        a2_scr[FFT_N1:, cs] = bim * tw_re - bre * tw_im
    out = jnp.dot(f1i_ref[...], a2_scr[...].astype(BF16), preferred_element_type=F32)
    y_ref[...] = out.reshape(2, u_ref.shape[1], cgl)


def _hy_conv(ut, kfilt):
    bsz, n1h, nch, _ = ut.shape
    assert 2 * n1h == FFT_N1 and bsz % 2 == 0
    cgl = FFT_CG * FFT_N1
    ncol = nch * FFT_N1
    tw, f3, f3i, f1_pair, f1_real, f1i = _fft_consts(n1h)
    kt = kfilt.reshape(FFT_N1, FFT_N1, nch).transpose(0, 2, 1).reshape(FFT_N1, ncol)
    const2 = lambda shp: pl.BlockSpec(shp, lambda *a: (0, 0))
    kf = pl.pallas_call(
        _hy_filt_kernel,
        grid=(nch // FFT_CG,),
        in_specs=[pl.BlockSpec((FFT_N1, cgl), lambda j: (0, j)), const2((2 * FFT_N1, FFT_N1)),
                  const2((FFT_N1, 2 * FFT_N1)), const2((2 * FFT_N1, 2 * FFT_N1))],
        out_specs=pl.BlockSpec((cgl, 2 * FFT_N1), lambda j: (j, 0)),
        out_shape=jax.ShapeDtypeStruct((ncol, 2 * FFT_N1), F32),
        scratch_shapes=[pltpu.VMEM((cgl, 2 * FFT_N1), F32)],
        compiler_params=_cparams(("parallel",)),
        name="hy_filter_dft",
    )(kt, jnp.asarray(f1_real, F32), jnp.asarray(tw, F32), jnp.asarray(f3, F32))
    u2 = ut.reshape(bsz, n1h, ncol)
    y = pl.pallas_call(
        _hy_fft_kernel,
        grid=(bsz // 2, nch // FFT_CG),
        in_specs=[pl.BlockSpec((2, n1h, cgl), lambda p, j: (p, 0, j)),
                  pl.BlockSpec((cgl, 2 * FFT_N1), lambda p, j: (j, 0)),
                  const2((2 * FFT_N1, FFT_N1)), const2((FFT_N1, 2 * FFT_N1)),
                  const2((2 * FFT_N1, 2 * FFT_N1)), const2((2 * FFT_N1, 2 * FFT_N1)), const2((FFT_N1, 2 * FFT_N1))],
        out_specs=pl.BlockSpec((2, n1h, cgl), lambda p, j: (p, 0, j)),
        out_shape=jax.ShapeDtypeStruct((bsz, n1h, ncol), F32),
        scratch_shapes=[pltpu.VMEM((cgl, 2 * FFT_N1), F32), pltpu.VMEM((2 * FFT_N1, cgl), F32)],
        compiler_params=_cparams(("parallel", "parallel")),
        name="hy_fft_conv",
    )(u2, kf, jnp.asarray(f1_pair, F32).astype(BF16), jnp.asarray(tw, F32), jnp.asarray(f3, F32).astype(BF16),
      jnp.asarray(f3i, F32).astype(BF16), jnp.asarray(f1i, F32).astype(BF16))
    return y.reshape(bsz, n1h, nch, FFT_N1)


def _hy_filter(seqlen, fw1, fb1, fw2, fb2, fw3, fb3, freq, fw4):
    t = jnp.linspace(0.0, 1.0, seqlen, dtype=F32)[:, None]
    bands = (HY_EMB - 1) // 2
    w = 2.0 * math.pi * jnp.arange(seqlen, dtype=F32)[:, None] / seqlen
    f = jnp.linspace(1e-4, bands - 1, bands, dtype=F32)[None, :]
    feat = jnp.concatenate([t, jnp.cos(f * w), -jnp.sin(f * w)], axis=-1)
    feat2 = jnp.concatenate([feat, feat[::-1]], axis=0)
    h = _mm(feat2, fw1, fb1, freq)
    h = _mm(h, fw2, fb2, freq)
    h = _mm(h, fw3, fb3, freq)
    deltas = jnp.abs(jnp.linspace(math.log(HY_DECAY_TARGET) / HY_DECAY_LONG_PCT,
                                  math.log(HY_DECAY_TARGET) / HY_DECAY_SHORT_PCT, HY_W, dtype=F32))
    hf = _mm(h[:seqlen], fw4[:, :HY_W]) * jnp.exp(-t * deltas)
    hb = _mm(h[seqlen:], fw4[:, HY_W:]) * jnp.exp(-t[::-1] * deltas)
    return jnp.concatenate([hf, jnp.zeros((1, HY_W), F32), hb[:seqlen - 1]], axis=0)


def _od_out_kernel(na_ref, yt_ref, x0_ref, u_ref, x_ref, mod_ref, hb_ref, wout_ref, wr_ref, br_ref,
                   x_out, h2_out, lg_out):
    y = jnp.concatenate([yt_ref[j].T for j in range(yt_ref.shape[0])], axis=0)
    hy = x0_ref[...] * (y + u_ref[...] * hb_ref[...])
    cat = jnp.concatenate([na_ref[...], hy], axis=1).astype(BF16)
    mix = jnp.dot(cat, wout_ref[...], preferred_element_type=F32)
    _mixer_tail(x_ref[...], mix, mod_ref, wr_ref, br_ref, x_out, h2_out, lg_out)


def _latent_mod_index(b, i):
    return (b, 1, 0, 0)


def _od_out(na, yt, x0, u, xcat, mods, hb, wout, wr, br):
    bsz, l, _ = na.shape
    nblk = l // TB
    tok = lambda n: pl.BlockSpec((None, TB, n), lambda b, i: (b, i, 0))
    const = lambda shp: pl.BlockSpec(shp, lambda b, i: tuple(0 for _ in shp))
    return pl.pallas_call(
        _od_out_kernel,
        grid=(bsz, nblk),
        in_specs=[tok(NA_W), pl.BlockSpec((None, TB // FFT_N1, HY_W, FFT_N1), lambda b, i: (b, i, 0, 0)),
                  tok(HY_W), tok(HY_W),
                  pl.BlockSpec((None, TB, D_MODEL), lambda b, i: (b, i + 1, 0)),
                  pl.BlockSpec((None, None, 6, D_MODEL), _latent_mod_index),
                  const((1, HY_W)), const((D_MODEL, D_MODEL)), const((D_MODEL, 128)), const((1, 128))],
        out_specs=[tok(D_MODEL), pl.BlockSpec((TB * ROW_TILE, 128), lambda b, i: (b * nblk + i, 0)), tok(128)],
        out_shape=[jax.ShapeDtypeStruct((bsz, l, D_MODEL), F32), jax.ShapeDtypeStruct((bsz * l * ROW_TILE, 128), F32),
                   jax.ShapeDtypeStruct((bsz, l, 128), F32)],
        compiler_params=_cparams(("parallel", "parallel")),
        name="od_out",
    )(na, yt, x0, u, xcat, mods, hb, wout, wr, br)


def _mods(c, c_ctx, ada_w, ada_b):
    bsz = c.shape[0]
    depth, _, n = ada_w.shape
    cc = jnp.concatenate([c, c_ctx[None]], axis=0)
    a = jnp.pad(cc * _sigmoid(cc), ((0, 8 - (bsz + 1) % 8), (0, 0)))
    mp, tn = a.shape[0], 1024
    m = pl.pallas_call(
        functools.partial(_mm_kernel, use_sin=False),
        grid=(depth, n // tn),
        in_specs=[pl.BlockSpec((mp, D_MODEL), lambda l, j: (0, 0)),
                  pl.BlockSpec((None, D_MODEL, tn), lambda l, j: (l, 0, j)),
                  pl.BlockSpec((None, 1, tn), lambda l, j: (l, 0, j)),
                  pl.BlockSpec((1, tn), lambda l, j: (0, 0))],
        out_specs=pl.BlockSpec((None, mp, tn), lambda l, j: (l, 0, j)),
        out_shape=jax.ShapeDtypeStruct((depth, mp, n), F32),
        compiler_params=_cparams(("parallel", "parallel")),
        name="adaln_dense",
    )(a, ada_w, ada_b.reshape(depth, 1, n), jnp.ones((1, tn), F32))
    mod_l = m[:, :bsz].reshape(depth, bsz, 1, 6, D_MODEL)
    mod_c = jnp.broadcast_to(m[:, bsz].reshape(depth, 1, 1, 6, D_MODEL), (depth, bsz, 1, 6, D_MODEL))
    return jnp.concatenate([mod_c, mod_l], axis=2)


def _rope_tables(seqlen):
    pos = jnp.arange(seqlen)
    half = GLA_DK // 4
    freqs = ROPE_BASE ** (-jnp.arange(half, dtype=F32) / half)
    ar = (pos // GRID_W).astype(F32)[:, None] * freqs
    ac = (pos % GRID_W).astype(F32)[:, None] * freqs
    cos = jnp.concatenate([jnp.cos(ar), jnp.cos(ar), jnp.cos(ac), jnp.cos(ac)], axis=1)
    sin = jnp.concatenate([-jnp.sin(ar), jnp.sin(ar), -jnp.sin(ac), jnp.sin(ac)], axis=1)
    cos = jnp.concatenate([jnp.ones((TB, GLA_DK), F32), cos], axis=0)
    sin = jnp.concatenate([jnp.zeros((TB, GLA_DK), F32), sin], axis=0)
    return jnp.tile(cos, (1, GLA_HEADS)), jnp.tile(sin, (1, GLA_HEADS))


def _router_weights(wg, bg, we, be):
    pad = 128 - N_GROUPS - N_EXPERTS
    wr = jnp.concatenate([wg, we, jnp.zeros((D_MODEL, pad), F32)], axis=1)
    br = jnp.concatenate([bg, be, jnp.zeros((pad,), F32)]).reshape(1, 128)
    return wr, br


def kernel(x, c, ctx, c_ctx, ada_w, ada_b, moe_wg, moe_bg, moe_we, moe_be, moe_w1, moe_w3, moe_w2, ev_w_in, ev_w_out, gla_wa2, gla_ba, gla_norm, s5_lam_re, s5_lam_im, s5_log_dt, s5_b_re, s5_b_im, s5_c_re, s5_c_im, s5_d, s5_w_glu, od_w_in, od_w_out, na_q_norm, na_k_norm, na_rpb, hy_conv_w, hy_conv_b, hy_fw1, hy_fb1, hy_fw2, hy_fb2, hy_fw3, hy_fb3, hy_freq, hy_fw4, hy_bias):
    bsz, seqlen, _ = x.shape
    assert ctx.shape[1] == TB and seqlen % TB == 0

    mods_all = _mods(c, c_ctx, ada_w, ada_b)
    mods = mods_all[0]
    w_in = ev_w_in[0]
    n_a = 2 * GLA_RANK
    a0 = 2 * GLA_QK + 2 * GLA_V
    w_ev = jnp.concatenate([w_in[:, :a0], w_in[:, a0 + n_a:], w_in[:, a0:a0 + n_a],
                            jnp.zeros((D_MODEL, 128 - n_a), F32)], axis=1).astype(BF16)
    wa = jnp.zeros((128, 2 * GLA_QK), F32)
    for d in range(2):
        wa = wa.at[d * GLA_RANK:(d + 1) * GLA_RANK, d * GLA_QK:(d + 1) * GLA_QK].set(gla_wa2[0, d])
    cos, sin = _rope_tables(seqlen)
    q, k, v, g, u, u_sw, la = _ev_proj(ctx, x, mods, w_ev, wa, gla_ba[0].reshape(1, 2 * GLA_QK), cos, sin)
    o_f, o_b = _gla(q, k, v, la)
    s5p = [t[0].astype(F32) for t in (s5_lam_re, s5_lam_im, s5_log_dt, s5_b_re, s5_b_im, s5_c_re, s5_c_im)]
    y_f = _s5_scan(u, _s5_mats(*[t[0] for t in s5p], rev=False), rev=False)
    y_b = _s5_scan(u_sw, _s5_mats(*[t[1] for t in s5p], rev=True), rev=True)
    wr, br = _router_weights(moe_wg[0], moe_bg[0], moe_we[0], moe_be[0])
    x1, h2, lg = _ev_out(o_f, o_b, g, y_f, y_b, u, ctx, x, mods,
                         jnp.tile(gla_norm[0], GLA_HEADS).reshape(1, GLA_V), s5_d[0].reshape(1, S5_W),
                         s5_w_glu[0].astype(BF16), ev_w_out[0].astype(BF16), wr, br)
    xcat = _moe(x1, h2, lg, mods, _mod_index,
                moe_w1, moe_w3, moe_w2, 0)

    mods = mods_all[1]
    hd = np.arange(NA_W) // NA_DH
    gm = jnp.asarray((hd[:, None] == hd[None, :]).astype(np.float32) / NA_DH)
    qh, kh, vh, zh = _od_proj(xcat, mods, od_w_in[0].astype(BF16), gm,
                              jnp.tile(na_q_norm[0], NA_HEADS).reshape(1, NA_W),
                              jnp.tile(na_k_norm[0], NA_HEADS).reshape(1, NA_W))
    na = _na(qh, kh, vh, _na_bias_table(na_rpb[0]))
    x0, uh, ut = _hy_pre(zh, hy_conv_w[0], hy_conv_b[0].reshape(1, 3 * HY_W))
    kfilt = _hy_filter(seqlen, hy_fw1[0], hy_fb1[0], hy_fw2[0], hy_fb2[0], hy_fw3[0], hy_fb3[0],
                       hy_freq[0], hy_fw4[0])
    yt = _hy_conv(ut, kfilt)
    wr, br = _router_weights(moe_wg[1], moe_bg[1], moe_we[1], moe_be[1])
    xl, h2, lg = _od_out(na, yt, x0, uh, xcat, mods, hy_bias[0].reshape(1, HY_W),
                         od_w_out[0].astype(BF16), wr, br)
    return _moe(xl, h2, lg, mods, _latent_mod_index,
                moe_w1, moe_w3, moe_w2, 1)
```

```python
import functools
import math

import numpy as np
import jax
import jax.numpy as jnp
from jax import lax
from jax.experimental import pallas as pl
from jax.experimental.pallas import tpu as pltpu

F32, BF16 = jnp.float32, jnp.bfloat16
HI = lax.Precision.HIGHEST

D_MODEL = 1024
GRID_W = 64
EPS = 1e-6
ROPE_BASE = 10000.0
NEG_INF = -1e30
GLA_HEADS, GLA_DK, GLA_DV = 4, 64, 128
GLA_QK, GLA_V = GLA_HEADS * GLA_DK, GLA_HEADS * GLA_DV
GLA_RANK = 16
GLA_TAU = 16.0
GLA_CHUNK = 64
GLA_LOG_ALPHA_MIN = -1.0
S5_W, S5_H, S5_P = 512, 16, 64
S5_G = S5_W // S5_H
S5_CHUNK = 8
S5_TBLK = 1408
NA_HEADS, NA_DH = 8, 64
NA_W = NA_HEADS * NA_DH
WIN_R, WIN_C = 8, 16
NA_HG = 4
NA_ROWS = 4
HY_W = 512
HY_SHORT = 3
HY_EMB = 33
HY_DECAY_TARGET = 1e-2
HY_DECAY_SHORT_PCT = 0.3
HY_DECAY_LONG_PCT = 1.5
N_GROUPS, EXP_PER_GROUP = 4, 8
N_EXPERTS = N_GROUPS * EXP_PER_GROUP
D_EXPERT = 512
TOP_K = 2
MOE_BLOCK = 256

TB = 256
PROJ_TILE = 768
HALO = 16
FFT_N1 = 128
FFT_CG = 16
V7X_VMEM_LIMIT = 52 * 1024 * 1024


def _cparams(sem):
    return pltpu.CompilerParams(dimension_semantics=sem, vmem_limit_bytes=V7X_VMEM_LIMIT)


def _sigmoid(x):
    return 1.0 / (1.0 + jnp.exp(-x))


ROW_TILE = D_MODEL // 128


def _to_token_tiles(ref, val):
    n = val.shape[0]
    for j in range(ROW_TILE):
        ref[pl.ds(j, n, stride=ROW_TILE), :] = val[:, j * 128:(j + 1) * 128]


def _from_token_tiles(ref, n):
    return jnp.concatenate([ref[pl.ds(j, n, stride=ROW_TILE), :] for j in range(ROW_TILE)], axis=1)


def _split_bf16(x):
    hi = x.astype(BF16)
    return hi, (x - hi.astype(F32)).astype(BF16)


def _dot_x3(a, b):
    a_hi, a_lo = _split_bf16(a)
    b_hi, b_lo = _split_bf16(b)
    d = lambda p, q: jnp.dot(p, q, preferred_element_type=F32)
    return d(a_hi, b_hi) + d(a_lo, b_hi) + d(a_hi, b_lo)


def _rms_rows(x):
    return x * lax.rsqrt(jnp.mean(x * x, axis=-1, keepdims=True) + EPS)


def _mm_kernel(a_ref, w_ref, b_ref, f_ref, o_ref, *, use_sin):
    z = jnp.dot(a_ref[...], w_ref[...], precision=HI, preferred_element_type=F32) + b_ref[...]
    if use_sin:
        z = jnp.sin(f_ref[...] * z)
    o_ref[...] = z


def _mm(a, w, bias=None, freq=None):
    m, k = a.shape
    n = w.shape[1]
    mp, kp = -(-m // 8) * 8, -(-k // 128) * 128
    tm = min(mp, 1024)
    mp = -(-mp // tm) * tm
    tn = n if n <= 1024 else 1024
    assert n % tn == 0
    a = jnp.pad(a.astype(F32), ((0, mp - m), (0, kp - k)))
    w = jnp.pad(w.astype(F32), ((0, kp - k), (0, 0)))
    bias = jnp.zeros((n,), F32) if bias is None else bias.astype(F32)
    use_sin = freq is not None
    freq = jnp.ones((n,), F32) if freq is None else freq.astype(F32)
    out = pl.pallas_call(
        functools.partial(_mm_kernel, use_sin=use_sin),
        grid=(mp // tm, n // tn),
        in_specs=[pl.BlockSpec((tm, kp), lambda i, j: (i, 0)),
                  pl.BlockSpec((kp, tn), lambda i, j: (0, j)),
                  pl.BlockSpec((1, tn), lambda i, j: (0, j)),
                  pl.BlockSpec((1, tn), lambda i, j: (0, j))],
        out_specs=pl.BlockSpec((tm, tn), lambda i, j: (i, j)),
        out_shape=jax.ShapeDtypeStruct((mp, n), F32),
        compiler_params=_cparams(("parallel", "parallel")),
        name="small_dense",
    )(a, w, bias.reshape(1, n), freq.reshape(1, n))
    return out[:m]


def _mod_index(b, i):
    return (b, jnp.minimum(i, 1), 0, 0)


def _swapped_index(nblk, b, i):
    return (b, jnp.where(i == 0, nblk - 1, i - 1), 0)


EV_NQ, EV_NK, EV_NV, EV_NG, EV_NU = 0, 256, 512, 1024, 1536
EV_NA = 2048
EV_NTOT = 2176


def _stream_tile(ctx_ref, x_ref):
    return jnp.where(pl.program_id(1) == 0, ctx_ref[...], x_ref[...])


def _stream_specs():
    return [pl.BlockSpec((None, TB, D_MODEL), lambda b, i: (b, 0, 0)),
            pl.BlockSpec((None, TB, D_MODEL), lambda b, i: (b, jnp.maximum(i - 1, 0), 0))]


def _ev_proj_kernel(ctx_ref, x_ref, mod_ref, w_ref, wa_ref, ba_ref, cos_ref, sin_ref,
                    q_ref, k_ref, v_ref, g_ref, u_ref, usw_ref, la_ref):
    x = _stream_tile(ctx_ref, x_ref)
    h = _rms_rows(x) * (1.0 + mod_ref[1:2, :]) + mod_ref[0:1, :]
    z = jnp.dot(h.astype(BF16), w_ref[...], preferred_element_type=F32)
    lane = lax.broadcasted_iota(jnp.int32, (x.shape[0], GLA_QK), 1)
    first = (lane % 32) < 16
    cos, sin = cos_ref[...], sin_ref[...]

    def rot(t):
        partner = jnp.where(first, pltpu.roll(t, GLA_QK - 16, 1), pltpu.roll(t, 16, 1))
        return t * cos + partner * sin

    q_ref[...] = rot(z[:, EV_NQ:EV_NQ + GLA_QK]) * (GLA_DK ** -0.5)
    k_ref[...] = rot(z[:, EV_NK:EV_NK + GLA_QK])
    v_ref[...] = z[:, EV_NV:EV_NV + GLA_V].astype(BF16)
    g_ref[...] = z[:, EV_NG:EV_NG + GLA_V].astype(BF16)
    u_ref[...] = z[:, EV_NU:EV_NU + S5_W]
    usw_ref[...] = z[:, EV_NU:EV_NU + S5_W]
    a = z[:, EV_NA:EV_NA + 128]
    pre = _dot_x3(a, wa_ref[...]) + ba_ref[...]
    ls = jnp.minimum(pre, 0.0) - jnp.log1p(jnp.exp(-jnp.abs(pre)))
    la_ref[...] = jnp.maximum(ls / GLA_TAU, GLA_LOG_ALPHA_MIN)


def _ev_proj(ctx, x, mods, w, wa, ba, cos, sin):
    bsz = x.shape[0]
    lt = ctx.shape[1] + x.shape[1]
    nblk = lt // TB
    tok = lambda n: pl.BlockSpec((None, TB, n), lambda b, i: (b, i, 0))
    const = lambda shp: pl.BlockSpec(shp, lambda b, i: tuple(0 for _ in shp))
    return pl.pallas_call(
        _ev_proj_kernel,
        grid=(bsz, nblk),
        in_specs=_stream_specs() + [
                  pl.BlockSpec((None, None, 6, D_MODEL), _mod_index),
                  const((D_MODEL, EV_NTOT)), const((128, 2 * GLA_QK)), const((1, 2 * GLA_QK)),
                  pl.BlockSpec((TB, GLA_QK), lambda b, i: (i, 0)),
                  pl.BlockSpec((TB, GLA_QK), lambda b, i: (i, 0))],
        out_specs=[tok(GLA_QK), tok(GLA_QK), tok(GLA_V), tok(GLA_V), tok(S5_W),
                   pl.BlockSpec((None, TB, S5_W), functools.partial(_swapped_index, nblk)), tok(2 * GLA_QK)],
        out_shape=[jax.ShapeDtypeStruct((bsz, lt, GLA_QK), F32),
                   jax.ShapeDtypeStruct((bsz, lt, GLA_QK), F32),
                   jax.ShapeDtypeStruct((bsz, lt, GLA_V), BF16),
                   jax.ShapeDtypeStruct((bsz, lt, GLA_V), BF16),
                   jax.ShapeDtypeStruct((bsz, lt, S5_W), F32),
                   jax.ShapeDtypeStruct((bsz, lt, S5_W), F32),
                   jax.ShapeDtypeStruct((bsz, lt, 2 * GLA_QK), F32)],
        compiler_params=_cparams(("parallel", "parallel")),
        name="ev_proj",
    )(ctx, x, mods, w, wa, ba, cos, sin)


def _gla_kernel(qf_ref, kf_ref, vf_ref, laf_ref, qb_ref, kb_ref, vb_ref, lab_ref,
                of_ref, ob_ref, s_scr):
    i = pl.program_id(1)

    @pl.when(i == 0)
    def _():
        s_scr[...] = jnp.zeros_like(s_scr)

    c = GLA_CHUNK
    nh = GLA_HEADS
    row = lax.broadcasted_iota(jnp.int32, (c, c), 0)
    col = lax.broadcasted_iota(jnp.int32, (c, c), 1)
    row4 = lax.broadcasted_iota(jnp.int32, (nh * c, c), 0) % c
    col4 = lax.broadcasted_iota(jnp.int32, (nh * c, c), 1)
    lane_head = lax.broadcasted_iota(jnp.int32, (c, GLA_QK), 1) // GLA_DK
    out_head = lax.broadcasted_iota(jnp.int32, (c, GLA_V), 1) // GLA_DV
    bd_mask = (lax.broadcasted_iota(jnp.int32, (GLA_V, GLA_QK), 0) // GLA_DV
               == lax.broadcasted_iota(jnp.int32, (GLA_V, GLA_QK), 1) // GLA_DK)
    nchunk = qf_ref.shape[0] // c
    nt = (((1,), (1,)), ((), ()))
    tn = (((0,), (0,)), ((), ()))

    def one_chunk(refs, o_ref, d, r0):
        q_ref, k_ref, v_ref, la_ref = refs
        fwd = d == 0
        sl = pl.ds(r0, c)
        qc, kc, vc, lac = q_ref[sl, :], k_ref[sl, :], v_ref[sl, :], la_ref[sl, :]
        tri = ((row >= col) if fwd else (row <= col)).astype(BF16)
        la_hi, la_lo = _split_bf16(lac)
        b = (jnp.dot(tri, la_hi, preferred_element_type=F32)
             + jnp.dot(tri, la_lo, preferred_element_type=F32))
        b_last = b[c - 1:c, :] if fwd else b[0:1, :]
        qe = (qc * jnp.exp(b)).astype(BF16)
        ke = (kc * jnp.exp(-b)).astype(BF16)
        kd = (kc * jnp.exp(b_last - b)).astype(BF16)
        st = s_scr[d]
        o = lax.dot_general(qe, st.astype(BF16), nt, preferred_element_type=F32)
        q4 = jnp.concatenate([jnp.where(lane_head == h, qe, jnp.zeros_like(qe)) for h in range(nh)], axis=0)
        att = lax.dot_general(q4, ke, nt, preferred_element_type=F32)
        att_mask = (row4 >= col4) if fwd else (row4 < col4)
        o4 = jnp.dot(jnp.where(att_mask, att, 0.0).astype(BF16), vc, preferred_element_type=F32)
        for h in range(nh):
            o = o + jnp.where(out_head == h, o4[h * c:(h + 1) * c, :], 0.0)
        o_ref[sl, :] = o.astype(o_ref.dtype)
        upd_t = lax.dot_general(vc, kd, tn, preferred_element_type=F32)
        s_scr[d] = st * jnp.exp(b_last) + jnp.where(bd_mask, upd_t, 0.0)

    def body(j, carry):
        one_chunk((qf_ref, kf_ref, vf_ref, laf_ref), of_ref, 0, pl.multiple_of(j * c, c))
        one_chunk((qb_ref, kb_ref, vb_ref, lab_ref), ob_ref, 1, pl.multiple_of((nchunk - 1 - j) * c, c))
        return carry

    lax.fori_loop(0, nchunk, body, 0, unroll=True)


def _gla(q, k, v, la):
    bsz, lt, _ = q.shape
    nblk = lt // TB
    fwd_map = lambda b, i: (b, i, 0)
    bwd_blk = lambda i: jnp.where(i == 0, 0, nblk - i)
    bwd_map = lambda b, i: (b, bwd_blk(i), 0)
    bwd_map_la = lambda b, i: (b, bwd_blk(i), 1)
    spec = lambda n, m: pl.BlockSpec((None, TB, n), m)
    return pl.pallas_call(
        _gla_kernel,
        grid=(bsz, nblk),
        in_specs=[spec(GLA_QK, fwd_map), spec(GLA_QK, fwd_map), spec(GLA_V, fwd_map), spec(GLA_QK, fwd_map),
                  spec(GLA_QK, bwd_map), spec(GLA_QK, bwd_map), spec(GLA_V, bwd_map), spec(GLA_QK, bwd_map_la)],
        out_specs=[spec(GLA_V, fwd_map), spec(GLA_V, bwd_map)],
        out_shape=[jax.ShapeDtypeStruct((bsz, lt, GLA_V), BF16), jax.ShapeDtypeStruct((bsz, lt, GLA_V), BF16)],
        scratch_shapes=[pltpu.VMEM((2, GLA_V, GLA_QK), F32)],
        compiler_params=_cparams(("parallel", "arbitrary")),
        name="gla_scan",
    )(q, k, v, la, q, k, v, la)


def _s5_kernel(u_ref, wm_ref, tm_ref, cm_ref, ar_ref, ai_ref, y_ref, w_scr, hp_scr, h_scr, *, rev):
    @pl.when(pl.program_id(1) == 0)
    def _():
        h_scr[...] = jnp.zeros_like(h_scr)

    bsz, ntok, _ = u_ref.shape
    nc = ntok // S5_CHUNK
    half = 8 * S5_P
    x = jnp.concatenate(
        [jnp.concatenate([u_ref[b, pl.ds(s, nc, stride=S5_CHUNK), :] for s in range(S5_CHUNK)], axis=1)
         for b in range(bsz)], axis=0).astype(BF16)
    w_scr[...] = jnp.dot(x, wm_ref[...], preferred_element_type=F32)
    ar, ai = ar_ref[...], ai_ref[...]

    def body(j, hs):
        c = (nc - 1 - j) if rev else j
        out = []
        for b in range(bsz):
            re, im = hs[b]
            r = b * nc + c
            hp_scr[pl.ds(r, 1), :] = jnp.concatenate([re, im], axis=1)
            w = w_scr[pl.ds(r, 1), :]
            out.append((ar * re - ai * im + w[:, :half], ar * im + ai * re + w[:, half:]))
        return tuple(out)

    hs = lax.fori_loop(0, nc, body, tuple((h_scr[b:b + 1, :half], h_scr[b:b + 1, half:]) for b in range(bsz)))
    for b in range(bsz):
        h_scr[b:b + 1, :] = jnp.concatenate(hs[b], axis=1)
    mt = 256
    ntile = x.shape[1] // mt
    cols = []
    for jt in range(ntile):
        acc = None
        for it in (range(jt, ntile) if rev else range(jt + 1)):
            part = jnp.dot(x[:, it * mt:(it + 1) * mt], tm_ref[it * mt:(it + 1) * mt, jt * mt:(jt + 1) * mt],
                           preferred_element_type=F32)
            acc = part if acc is None else acc + part
        cols.append(acc)
    y = (jnp.concatenate(cols, axis=1)
         + jnp.dot(hp_scr[...].astype(BF16), cm_ref[...], preferred_element_type=F32))
    for b in range(bsz):
        for s in range(S5_CHUNK):
            y_ref[b, pl.ds(s, nc, stride=S5_CHUNK), :] = y[b * nc:(b + 1) * nc, s * 128:(s + 1) * 128]


def _s5_mats(lam_re, lam_im, log_dt, b_re, b_im, c_re, c_im, rev):
    t16 = S5_CHUNK
    dt = jnp.exp(log_dt)[:, None]
    mag = jnp.exp(lam_re * dt)
    a_re, a_im = mag * jnp.cos(lam_im * dt), mag * jnp.sin(lam_im * dt)
    den = lam_re * lam_re + lam_im * lam_im
    nr = a_re - 1.0
    co_re = ((nr * lam_re + a_im * lam_im) / den)[..., None]
    co_im = ((a_im * lam_re - nr * lam_im) / den)[..., None]
    bb_re, bb_im = co_re * b_re - co_im * b_im, co_re * b_im + co_im * b_re
    pr, pi = [jnp.ones_like(a_re)], [jnp.zeros_like(a_im)]
    for _ in range(t16):
        pr, pi = pr + [pr[-1] * a_re - pi[-1] * a_im], pi + [pr[-1] * a_im + pi[-1] * a_re]
    pw_re, pw_im = jnp.stack(pr), jnp.stack(pi)
    g = lam_re.shape[0]
    e_re, e_im = pw_re[t16 - 1::-1][:t16], pw_im[t16 - 1::-1][:t16]
    wre = jnp.einsum('sgp,gph->gshp', e_re, bb_re) - jnp.einsum('sgp,gph->gshp', e_im, bb_im)
    wim = jnp.einsum('sgp,gph->gshp', e_re, bb_im) + jnp.einsum('sgp,gph->gshp', e_im, bb_re)
    cb_re = jnp.einsum('gkp,gph->gpkh', c_re, bb_re) - jnp.einsum('gkp,gph->gpkh', c_im, bb_im)
    cb_im = jnp.einsum('gkp,gph->gpkh', c_re, bb_im) + jnp.einsum('gkp,gph->gpkh', c_im, bb_re)
    kd = jnp.einsum('dgp,gpkh->dgkh', pw_re[:t16], cb_re) - jnp.einsum('dgp,gpkh->dgkh', pw_im[:t16], cb_im)
    lag = np.arange(t16)[None, :] - np.arange(t16)[:, None]
    toe = jnp.where((lag >= 0)[:, :, None, None, None], kd[np.clip(lag, 0, t16 - 1)], 0.0)
    toe = toe.transpose(2, 0, 4, 1, 3)
    q_re, q_im = pw_re[1:], pw_im[1:]
    ca_re = jnp.einsum('gkp,tgp->gptk', c_re, q_re) - jnp.einsum('gkp,tgp->gptk', c_im, q_im)
    ca_im = jnp.einsum('gkp,tgp->gptk', c_re, q_im) + jnp.einsum('gkp,tgp->gptk', c_im, q_re)
    if rev:
        wre, wim = wre[:, ::-1], wim[:, ::-1]
        toe = toe[:, ::-1, :, ::-1]
        ca_re, ca_im = ca_re[:, :, ::-1], ca_im[:, :, ::-1]
    nq, gl = g // 8, 8
    nd = t16 * S5_H
    tok_hot = np.zeros((gl, nd, t16 * 128), np.float32)
    st_hot = np.zeros((gl, 2 * S5_P, 2 * gl * S5_P), np.float32)
    for gi in range(gl):
        a = np.arange(nd)
        tok_hot[gi, a, (a // S5_H) * 128 + gi * S5_H + a % S5_H] = 1.0
        a = np.arange(2 * S5_P)
        st_hot[gi, a, (a // S5_P) * gl * S5_P + gi * S5_P + a % S5_P] = 1.0
    place = lambda rows, blk, cols: jnp.einsum('gar,qgab,gbc->qrc', rows, blk, cols)
    wg = jnp.concatenate([wre, wim], axis=-1).reshape(nq, gl, nd, 2 * S5_P)
    tg = toe.reshape(nq, gl, nd, nd)
    cg = jnp.concatenate([ca_re, -ca_im], axis=1).reshape(nq, gl, 2 * S5_P, nd)
    wm = place(tok_hot, wg, st_hot)
    tmat = place(tok_hot, tg, tok_hot)
    cm = place(st_hot, cg, tok_hot)
    return (wm.astype(BF16), tmat.astype(BF16), cm.astype(BF16),
            pw_re[t16].reshape(nq, 1, gl * S5_P), pw_im[t16].reshape(nq, 1, gl * S5_P))


def _s5_scan(u, mats, rev):
    bsz, lt, _ = u.shape
    tblk = max(d for d in range(64, S5_TBLK + 1, 64) if lt % d == 0)
    nblk = lt // tblk
    nq = S5_G // 8
    wm, tmat, cm, ar, ai = mats
    kw, ks = S5_CHUNK * 128, 8 * S5_P
    tmap = (lambda q, t: (0, nblk - 1 - t, q)) if rev else (lambda q, t: (0, t, q))
    per = lambda shp: pl.BlockSpec((None,) + shp, lambda q, t: (q,) + tuple(0 for _ in shp))
    rows = bsz * tblk // S5_CHUNK
    return pl.pallas_call(
        functools.partial(_s5_kernel, rev=rev),
        grid=(nq, nblk),
        in_specs=[pl.BlockSpec((bsz, tblk, 128), tmap),
                  per((kw, 2 * ks)), per((kw, kw)), per((2 * ks, kw)), per((1, ks)), per((1, ks))],
        out_specs=pl.BlockSpec((bsz, tblk, 128), tmap),
        out_shape=jax.ShapeDtypeStruct((bsz, lt, S5_W), F32),
        scratch_shapes=[pltpu.VMEM((rows, 2 * ks), F32), pltpu.VMEM((rows, 2 * ks), F32),
                        pltpu.VMEM((8, 2 * ks), F32)],
        compiler_params=_cparams(("parallel", "arbitrary")),
        name="s5_scan",
    )(u, wm, tmat, cm, ar, ai)


def _mixer_tail(x, mix, mod_ref, wr_ref, br_ref, x_out, h2_out, lg_out):
    xn = x + mod_ref[2:3, :] * mix
    x_out[...] = xn
    h2 = _rms_rows(xn) * (1.0 + mod_ref[4:5, :]) + mod_ref[3:4, :]
    _to_token_tiles(h2_out, h2)
    lg_out[...] = _route_tail(_dot_x3(h2, wr_ref[...]) + br_ref[...])


def _route_tail(lg):
    lane = lax.broadcasted_iota(jnp.int32, lg.shape, 1)
    lane_f = lane.astype(F32)
    big = jnp.float32(128.0)
    row_max = lambda t: jnp.max(t, axis=1, keepdims=True)
    first_lane = lambda hit: jnp.min(jnp.where(hit, lane_f, big), axis=1, keepdims=True)
    is_g = lane < N_GROUPS
    g_logit = jnp.where(is_g, lg, NEG_INF)
    g_max = row_max(g_logit)
    grp = first_lane(g_logit == g_max)
    g_w = 1.0 / jnp.sum(jnp.where(is_g, jnp.exp(lg - g_max), 0.0), axis=1, keepdims=True)
    lo = N_GROUPS + EXP_PER_GROUP * grp
    in_grp = jnp.logical_and(lane_f >= lo, lane_f < lo + EXP_PER_GROUP)
    e_max = row_max(jnp.where(in_grp, lg, NEG_INF))
    pe = jnp.where(in_grp, jnp.exp(lg - e_max), 0.0)
    p = pe / jnp.sum(pe, axis=1, keepdims=True)
    cand = jnp.where(in_grp, p, -1.0)
    p1 = row_max(cand)
    i1 = first_lane(cand == p1)
    cand2 = jnp.where(lane_f == i1, -1.0, cand)
    p2 = row_max(cand2)
    i2 = first_lane(cand2 == p2)
    scale = g_w / (p1 + p2)
    out = jnp.where(lane == 0, scale * p1, jnp.where(lane == 1, scale * p2,
                    jnp.where(lane == 2, i1 - N_GROUPS, jnp.where(lane == 3, i2 - N_GROUPS, 0.0))))
    return out


def _ev_out_kernel(of_ref, ob_ref, g_ref, yf_ref, yb_ref, u_ref, ctx_ref, x_ref, mod_ref,
                   gn_ref, ds_ref, wglu_ref, wout_ref, wr_ref, br_ref,
                   x_out, h2_out, lg_out):
    o = of_ref[...].astype(F32) + ob_ref[...].astype(F32)
    og = jnp.concatenate([_rms_rows(o[:, h * GLA_DV:(h + 1) * GLA_DV]) for h in range(GLA_HEADS)], axis=1)
    g = g_ref[...].astype(F32)
    og = og * gn_ref[...] * (g * _sigmoid(g))
    t = yf_ref[...] + yb_ref[...] + ds_ref[...] * u_ref[...]
    y = t * (0.5 * (1.0 + jnp.tanh(math.sqrt(2.0 / math.pi) * (t + 0.044715 * (t * t * t)))))
    y = y * _sigmoid(jnp.dot(y.astype(BF16), wglu_ref[...], preferred_element_type=F32))
    cat = jnp.concatenate([og, y], axis=1).astype(BF16)
    mix = jnp.dot(cat, wout_ref[...], preferred_element_type=F32)
    _mixer_tail(_stream_tile(ctx_ref, x_ref), mix, mod_ref, wr_ref, br_ref, x_out, h2_out, lg_out)


def _ev_out(o_f, o_b, g, y_f, y_b, u, ctx, x, mods, gn, ds, wglu, wout, wr, br):
    bsz, lt, _ = u.shape
    nblk = lt // TB
    tok = lambda n: pl.BlockSpec((None, TB, n), lambda b, i: (b, i, 0))
    const = lambda shp: pl.BlockSpec(shp, lambda b, i: tuple(0 for _ in shp))
    return pl.pallas_call(
        _ev_out_kernel,
        grid=(bsz, nblk),
        in_specs=[tok(512), tok(512), tok(512), tok(512),
                  pl.BlockSpec((None, TB, S5_W), functools.partial(_swapped_index, nblk)),
                  tok(512)] + _stream_specs() + [
                  pl.BlockSpec((None, None, 6, D_MODEL), _mod_index),
                  const((1, 512)), const((1, 512)), const((512, 512)), const((D_MODEL, D_MODEL)),
                  const((D_MODEL, 128)), const((1, 128))],
        out_specs=[tok(D_MODEL), pl.BlockSpec((TB * ROW_TILE, 128), lambda b, i: (b * nblk + i, 0)), tok(128)],
        out_shape=[jax.ShapeDtypeStruct((bsz, lt, D_MODEL), F32),
                   jax.ShapeDtypeStruct((bsz * lt * ROW_TILE, 128), F32),
                   jax.ShapeDtypeStruct((bsz, lt, 128), F32)],
        compiler_params=_cparams(("parallel", "parallel")),
        name="ev_out",
    )(o_f, o_b, g, y_f, y_b, u, ctx, x, mods, gn, ds, wglu, wout, wr, br)


def _route(route):
    n_tok = route.shape[0]
    gate = route[:, :TOP_K]
    eid = route[:, TOP_K:2 * TOP_K].astype(jnp.int32).reshape(-1)
    n_asg = n_tok * TOP_K
    order = jnp.argsort(eid).astype(jnp.int32)
    counts = jnp.sum((eid[:, None] == jnp.arange(N_EXPERTS)[None, :]).astype(jnp.int32), axis=0)
    padded = (counts + MOE_BLOCK - 1) // MOE_BLOCK * MOE_BLOCK
    pad_end = jnp.cumsum(padded)
    pad_start = pad_end - padded
    cnt_start = jnp.cumsum(counts) - counts
    n_blocks = -(-n_asg // MOE_BLOCK) + N_EXPERTS
    blk_start = jnp.arange(n_blocks, dtype=jnp.int32) * MOE_BLOCK
    blk_e = jnp.minimum(jnp.sum((pad_end[None, :] <= blk_start[:, None]).astype(jnp.int32), axis=1), N_EXPERTS - 1)
    pos = jnp.arange(n_blocks * MOE_BLOCK, dtype=jnp.int32)
    pos_e = jnp.repeat(blk_e, MOE_BLOCK)
    rank = pos - pad_start[pos_e]
    src = jnp.clip(cnt_start[pos_e] + rank, 0, n_asg - 1)
    slot_buf = jnp.where(rank < counts[pos_e], order[src], n_asg).astype(jnp.int32)
    n_valid = jnp.sum((slot_buf < n_asg).reshape(n_blocks, MOE_BLOCK), axis=1).astype(jnp.int32)
    tok = lax.shift_right_logical(slot_buf, 1)
    src_rows = jnp.minimum(tok, n_tok - 1) * ROW_TILE
    spare = n_asg + (pos // MOE_BLOCK % 2) * MOE_BLOCK + pos % MOE_BLOCK
    dst_rows = jnp.where(slot_buf < n_asg, (slot_buf & 1) * n_tok + tok, spare)
    lead = n_asg + jnp.arange(2 * MOE_BLOCK, dtype=jnp.int32)
    dst_rows = jnp.concatenate([lead, dst_rows]) * ROW_TILE
    return src_rows, dst_rows, blk_e, n_valid, gate.astype(F32)


def _moe_kernel(src_ref, dst_ref, blke_ref, nvalid_ref, h_hbm, w1_ref, w3_ref, w2_ref, z_hbm,
                xbuf0, xbuf1, ybuf0, ybuf1, wb1, wb3, wb2, gsem, ssem):
    i = pl.program_id(0)
    nblk = pl.num_programs(0)
    ns = ROW_TILE
    xb, yb = (xbuf0, xbuf1), (ybuf0, ybuf1)
    lead = 2 * MOE_BLOCK

    def issue_gather(blk, buf):
        base = blk * MOE_BLOCK
        for r in range(MOE_BLOCK):
            src = pl.multiple_of(src_ref[base + r], ns)
            pltpu.make_async_copy(h_hbm.at[pl.ds(src, ns)], xb[buf].at[pl.ds(r * ns, ns)],
                                  gsem.at[buf]).start(priority=r % 2)

    def issue_scatter(blk, buf):
        base = lead + blk * MOE_BLOCK
        for r in range(MOE_BLOCK):
            dst = pl.multiple_of(dst_ref[base + r], ns)
            pltpu.make_async_copy(yb[buf].at[pl.ds(r * ns, ns)], z_hbm.at[pl.ds(dst, ns)],
                                  ssem.at[buf]).start(priority=r % 2)

    def wait_gather(buf):
        pltpu.make_async_copy(h_hbm.at[pl.ds(0, MOE_BLOCK * ns)], xb[buf], gsem.at[buf]).wait()

    def wait_scatter(buf):
        pltpu.make_async_copy(yb[buf], z_hbm.at[pl.ds(0, MOE_BLOCK * ns)], ssem.at[buf]).wait()

    used = nvalid_ref[i] > 0

    @pl.when(i == 0)
    def _():
        ybuf0[...] = jnp.zeros_like(ybuf0)
        ybuf1[...] = jnp.zeros_like(ybuf1)
        issue_scatter(-2, 0)
        issue_gather(0, 0)

    def step(cur):
        oth = 1 - cur
        wait_gather(cur)
        issue_gather(jnp.minimum(i + 1, nblk - 1), oth)
        issue_scatter(i - 1, oth)
        x = _from_token_tiles(xb[cur], MOE_BLOCK).astype(BF16)

        @pl.when(jnp.logical_or(i == 0, blke_ref[i] != blke_ref[jnp.maximum(i - 1, 0)]))
        def _():
            wb1[...] = w1_ref[...].astype(BF16)
            wb3[...] = w3_ref[...].astype(BF16)
            wb2[...] = w2_ref[...].astype(BF16)

        h1 = jnp.dot(x, wb1[...], preferred_element_type=F32)
        h3 = jnp.dot(x, wb3[...], preferred_element_type=F32)
        a = (h1 * _sigmoid(h1) * h3).astype(BF16)
        y = jnp.dot(a, wb2[...], preferred_element_type=F32)
        wait_scatter(cur)
        _to_token_tiles(yb[cur], y)

    def drain(last_par):
        wait_gather(1 - last_par)
        issue_scatter(jnp.where(used, i, i - 1), last_par)
        wait_scatter(1 - last_par)
        wait_scatter(last_par)

    for par in range(2):
        @pl.when(jnp.logical_and(used, i % 2 == par))
        def _():
            step(par)

    first_unused = jnp.logical_and(jnp.logical_not(used),
                                   jnp.logical_and(i > 0, nvalid_ref[jnp.maximum(i - 1, 0)] > 0))
    last_used = jnp.logical_and(used, i == nblk - 1)
    for par in range(2):
        @pl.when(jnp.logical_or(jnp.logical_and(first_unused, (i - 1) % 2 == par),
                                jnp.logical_and(last_used, i % 2 == par)))
        def _():
            drain(par)


def _moe_experts(h2, src_rows, dst_rows, blk_e, n_valid, w1, w3, w2, layer):
    n_tok = h2.shape[0] // ROW_TILE
    n_blocks = blk_e.shape[0]
    wspec = lambda shp: pl.BlockSpec((None, None) + shp, lambda i, src, dst, blke, nvalid: (layer, blke[i], 0, 0))
    grid_spec = pltpu.PrefetchScalarGridSpec(
        num_scalar_prefetch=4,
        grid=(n_blocks,),
        in_specs=[pl.BlockSpec(memory_space=pl.ANY),
                  wspec((D_MODEL, D_EXPERT)), wspec((D_MODEL, D_EXPERT)), wspec((D_EXPERT, D_MODEL))],
        out_specs=pl.BlockSpec(memory_space=pl.ANY),
        scratch_shapes=[pltpu.VMEM((MOE_BLOCK * ROW_TILE, 128), F32), pltpu.VMEM((MOE_BLOCK * ROW_TILE, 128), F32),
                        pltpu.VMEM((MOE_BLOCK * ROW_TILE, 128), F32), pltpu.VMEM((MOE_BLOCK * ROW_TILE, 128), F32),
                        pltpu.VMEM((D_MODEL, D_EXPERT), BF16), pltpu.VMEM((D_MODEL, D_EXPERT), BF16),
                        pltpu.VMEM((D_EXPERT, D_MODEL), BF16),
                        pltpu.SemaphoreType.DMA((2,)), pltpu.SemaphoreType.DMA((2,))])
    return pl.pallas_call(
        _moe_kernel,
        grid_spec=grid_spec,
        out_shape=jax.ShapeDtypeStruct(((TOP_K * n_tok + 2 * MOE_BLOCK) * ROW_TILE, 128), F32),
        compiler_params=_cparams(("arbitrary",)),
        name="moe_experts",
    )(src_rows, dst_rows, blk_e, n_valid, h2, w1, w3, w2)


def _moe_combine_kernel(x_ref, z0_ref, z1_ref, gate_ref, mod_ref, o_ref):
    gate = gate_ref[...]
    tb = x_ref.shape[0]
    y = gate[:, 0:1] * _from_token_tiles(z0_ref, tb) + gate[:, 1:2] * _from_token_tiles(z1_ref, tb)
    o_ref[...] = x_ref[...] + mod_ref[5:6, :] * y


def _moe_combine(x, z, gate, mods, mod_index):
    bsz, lt, _ = x.shape
    nblk = lt // TB
    gate3 = gate.reshape(bsz, lt, TOP_K)
    return pl.pallas_call(
        _moe_combine_kernel,
        grid=(bsz, nblk),
        in_specs=[pl.BlockSpec((None, TB, D_MODEL), lambda b, i: (b, i, 0)),
                  pl.BlockSpec((TB * ROW_TILE, 128), lambda b, i: (b * nblk + i, 0)),
                  pl.BlockSpec((TB * ROW_TILE, 128), lambda b, i: ((bsz + b) * nblk + i, 0)),
                  pl.BlockSpec((None, TB, TOP_K), lambda b, i: (b, i, 0)),
                  pl.BlockSpec((None, None, 6, D_MODEL), mod_index)],
        out_specs=pl.BlockSpec((None, TB, D_MODEL), lambda b, i: (b, i, 0)),
        out_shape=jax.ShapeDtypeStruct((bsz, lt, D_MODEL), F32),
        compiler_params=_cparams(("parallel", "parallel")),
        name="moe_combine",
    )(x, z, z, gate3, mods)


def _moe(x, h2, logits, mods, mod_index, w1, w3, w2, layer):
    bsz, lt, _ = x.shape
    src_rows, dst_rows, blk_e, n_valid, gate = _route(logits.reshape(bsz * lt, 128))
    z = _moe_experts(h2, src_rows, dst_rows, blk_e, n_valid, w1, w3, w2, layer)
    return _moe_combine(x, z, gate, mods, mod_index)


def _od_proj_kernel(x_ref, mod_ref, w_ref, gm_ref, qn_ref, kn_ref, q_ref, k_ref, v_ref, zh_ref):
    tm = x_ref.shape[0]
    is_ctx = pl.program_id(1) * tm + lax.broadcasted_iota(jnp.int32, (tm, 1), 0) < TB
    shift = jnp.where(is_ctx, mod_ref[0, 0:1, :], mod_ref[1, 0:1, :])
    scale = jnp.where(is_ctx, mod_ref[0, 1:2, :], mod_ref[1, 1:2, :])
    h = _rms_rows(x_ref[...]) * (1.0 + scale) + shift
    z = jnp.dot(h.astype(BF16), w_ref[...], preferred_element_type=F32)

    def head_norm(t, gain):
        sq_hi, sq_lo = _split_bf16(t * t)
        gm = gm_ref[...].astype(BF16)
        ms = jnp.dot(sq_hi, gm, preferred_element_type=F32) + jnp.dot(sq_lo, gm, preferred_element_type=F32)
        return t * lax.rsqrt(ms + EPS) * gain

    q_ref[...] = (head_norm(z[:, :NA_W], qn_ref[...]) * (NA_DH ** -0.5)).astype(BF16)
    k_ref[...] = head_norm(z[:, NA_W:2 * NA_W], kn_ref[...]).astype(BF16)
    v_ref[...] = z[:, 2 * NA_W:3 * NA_W].astype(BF16)
    zh_ref[...] = z[:, 3 * NA_W:].astype(BF16)


def _od_proj(xcat, mods, w, gm, qn, kn):
    bsz, lt, _ = xcat.shape
    tm = max(d for d in range(TB, PROJ_TILE + 1, TB) if lt % d == 0)
    nblk = lt // tm
    tok = lambda n: pl.BlockSpec((None, tm, n), lambda b, i: (b, i, 0))
    const = lambda shp: pl.BlockSpec(shp, lambda b, i: tuple(0 for _ in shp))
    return pl.pallas_call(
        _od_proj_kernel,
        grid=(bsz, nblk),
        in_specs=[tok(D_MODEL), pl.BlockSpec((None, 2, 6, D_MODEL), lambda b, i: (b, 0, 0, 0)),
                  const((D_MODEL, 3 * NA_W + 3 * HY_W)), const((NA_W, NA_W)), const((1, NA_W)), const((1, NA_W))],
        out_specs=[tok(NA_W), tok(NA_W), tok(NA_W), tok(3 * HY_W)],
        out_shape=[jax.ShapeDtypeStruct((bsz, lt, NA_W), BF16), jax.ShapeDtypeStruct((bsz, lt, NA_W), BF16),
                   jax.ShapeDtypeStruct((bsz, lt, NA_W), BF16), jax.ShapeDtypeStruct((bsz, lt, 3 * HY_W), BF16)],
        compiler_params=_cparams(("parallel", "parallel")),
        name="od_proj",
    )(xcat, mods, w, gm, qn, kn)


def _na_kernel(q_ref, k_ref, v_ref, t2_ref, o_ref):
    for j in range(NA_ROWS):
        _na_one_row(q_ref, k_ref, v_ref, t2_ref, o_ref, j)


def _na_one_row(q_ref, k_ref, v_ref, t2_ref, o_ref, j):
    r = pl.program_id(1) * NA_ROWS + j
    n_rows = pl.num_programs(1) * NA_ROWS
    r0 = jnp.clip(r - WIN_R // 2, 0, n_rows - WIN_R)
    off = r0 - r + WIN_R - 1
    base = pl.multiple_of(TB + r0 * GRID_W, GRID_W)
    nloc = WIN_R * GRID_W
    q = q_ref[j * GRID_W:(j + 1) * GRID_W, :]
    hg = NA_HG
    gw = hg * NA_DH
    lane_head = lax.broadcasted_iota(jnp.int32, (GRID_W, gw), 1) // NA_DH
    nt = (((1,), (1,)), ((), ()))
    outs = []
    for grp in range(NA_HEADS // hg):
        cs = slice(gw * grp, gw * (grp + 1))
        q2 = q[:, cs]
        q4 = jnp.concatenate([jnp.where(lane_head == h, q2, jnp.zeros_like(q2)) for h in range(hg)], axis=0)
        kw, vw = k_ref[pl.ds(base, nloc), cs], v_ref[pl.ds(base, nloc), cs]
        kc, vc = k_ref[0:TB, cs], v_ref[0:TB, cs]
        bias = jnp.concatenate(
            [jnp.concatenate([t2_ref[hg * grp + h, off + 2 * m] for m in range(WIN_R // 2)], axis=1)
             for h in range(hg)], axis=0)
        s_loc = lax.dot_general(q4, kw, nt, preferred_element_type=F32) + bias
        s_ctx = lax.dot_general(q4, kc, nt, preferred_element_type=F32)
        m = jnp.maximum(jnp.max(s_loc, axis=1, keepdims=True), jnp.max(s_ctx, axis=1, keepdims=True))
        p_loc, p_ctx = jnp.exp(s_loc - m), jnp.exp(s_ctx - m)
        den = jnp.sum(p_loc, axis=1, keepdims=True) + jnp.sum(p_ctx, axis=1, keepdims=True)
        o4 = (jnp.dot(p_loc.astype(BF16), vw, preferred_element_type=F32)
              + jnp.dot(p_ctx.astype(BF16), vc, preferred_element_type=F32)) / den
        acc = jnp.zeros((GRID_W, gw), F32)
        for h in range(hg):
            acc = jnp.where(lane_head == h, o4[h * GRID_W:(h + 1) * GRID_W, :], acc)
        outs.append(acc)
    o_ref[j * GRID_W:(j + 1) * GRID_W, :] = jnp.concatenate(outs, axis=1)


def _na_bias_table(rpb):
    qc = np.arange(GRID_W)[:, None]
    kc = np.arange(GRID_W)[None, :]
    q_start = np.clip(qc - WIN_C // 2, 0, GRID_W - WIN_C)
    valid = (kc >= q_start) & (kc < q_start + WIN_C)
    col_idx = np.clip(kc - qc + WIN_C - 1, 0, 2 * WIN_C - 2)
    t = jnp.where(valid[None, None], rpb.astype(F32)[:, :, col_idx], NEG_INF)
    return jnp.concatenate([t[:, :-1], t[:, 1:]], axis=-1)


def _na(q, k, v, t2):
    bsz, lt, _ = q.shape
    qrows = NA_ROWS * GRID_W
    n_rows = (lt - TB) // qrows
    qoff = TB // qrows
    return pl.pallas_call(
        _na_kernel,
        grid=(bsz, n_rows),
        in_specs=[pl.BlockSpec((None, qrows, NA_W), lambda b, r: (b, r + qoff, 0)),
                  pl.BlockSpec((None, lt, NA_W), lambda b, r: (b, 0, 0)),
                  pl.BlockSpec((None, lt, NA_W), lambda b, r: (b, 0, 0)),
                  pl.BlockSpec(t2.shape, lambda b, r: (0, 0, 0, 0))],
        out_specs=pl.BlockSpec((None, qrows, NA_W), lambda b, r: (b, r, 0)),
        out_shape=jax.ShapeDtypeStruct((bsz, lt - TB, NA_W), F32),
        compiler_params=_cparams(("parallel", "arbitrary")),
        name="na_attn",
    )(q, k, v, t2)


def _hy_pre_kernel(z_ref, zp_ref, zn_ref, cw_ref, cb_ref, x0_ref, u_ref, ut_ref):
    i = pl.program_id(1)
    n = pl.num_programs(1)
    z = z_ref[...].astype(F32)
    tb = z.shape[0]
    prev_row = jnp.where(i > 0, zp_ref[HALO - 1:HALO, :].astype(F32), 0.0)
    next_row = jnp.where(i < n - 1, zn_ref[0:1, :].astype(F32), 0.0)
    rowid = lax.broadcasted_iota(jnp.int32, z.shape, 0)
    zm = jnp.where(rowid == 0, prev_row, pltpu.roll(z, 1, 0))
    zp = jnp.where(rowid == tb - 1, next_row, pltpu.roll(z, tb - 1, 0))
    zc = cb_ref[...] + cw_ref[0:1, :] * zm
    zc = zc + cw_ref[1:2, :] * z
    zc = zc + cw_ref[2:3, :] * zp
    x0_ref[...] = zc[:, :HY_W].astype(BF16)
    u = zc[:, HY_W:2 * HY_W] * zc[:, 2 * HY_W:]
    u_ref[...] = u.astype(BF16)
    for j in range(tb // FFT_N1):
        ut_ref[j] = u[j * FFT_N1:(j + 1) * FFT_N1, :].T.astype(BF16)


def _hy_pre(zh, cw, cb):
    bsz, lt, _ = zh.shape
    l = lt - TB
    nblk = l // TB
    h8 = TB // HALO
    return pl.pallas_call(
        _hy_pre_kernel,
        grid=(bsz, nblk),
        in_specs=[pl.BlockSpec((None, TB, 3 * HY_W), lambda b, i: (b, i + 1, 0)),
                  pl.BlockSpec((None, HALO, 3 * HY_W), lambda b, i: (b, (i + 1) * h8 - 1, 0)),
                  pl.BlockSpec((None, HALO, 3 * HY_W), lambda b, i: (b, jnp.minimum((i + 2) * h8, lt // HALO - 1), 0)),
                  pl.BlockSpec((HY_SHORT, 3 * HY_W), lambda b, i: (0, 0)),
                  pl.BlockSpec((1, 3 * HY_W), lambda b, i: (0, 0))],
        out_specs=[pl.BlockSpec((None, TB, HY_W), lambda b, i: (b, i, 0)),
                   pl.BlockSpec((None, TB, HY_W), lambda b, i: (b, i, 0)),
                   pl.BlockSpec((None, TB // FFT_N1, HY_W, FFT_N1), lambda b, i: (b, i, 0, 0))],
        out_shape=[jax.ShapeDtypeStruct((bsz, l, HY_W), BF16), jax.ShapeDtypeStruct((bsz, l, HY_W), BF16),
                   jax.ShapeDtypeStruct((bsz, l // FFT_N1, HY_W, FFT_N1), BF16)],
        compiler_params=_cparams(("parallel", "parallel")),
        name="hy_pre",
    )(zh, zh, zh, cw, cb)


def _fft_consts(n1_in):
    n = FFT_N1
    idx = np.arange(n)
    ang1 = 2.0 * np.pi * np.outer(idx, idx) / n
    c, s = np.cos(ang1), np.sin(ang1)
    angt = 2.0 * np.pi * np.outer(idx, idx) / (n * n)
    tw = np.concatenate([np.cos(angt), -np.sin(angt)], axis=1)
    f3 = np.block([[c, -s], [s, c]])
    f3i = np.block([[c, s], [-s, c]])
    ch, sh = c[:, :n1_in], s[:, :n1_in]
    f1_pair = np.block([[ch, sh], [-sh, ch]])
    f1_real = np.concatenate([c, -s], axis=0)
    f1i = np.block([[ch.T, -sh.T], [sh.T, ch.T]]) / (n * n)
    return tw, f3, f3i, f1_pair, f1_real, f1i


def _fft_forward(a, tw_re, tw_im, lhs_scr, ncg):
    for cix in range(ncg):
        cs = slice(cix * FFT_N1, (cix + 1) * FFT_N1)
        are, aim = a[:FFT_N1, cs], a[FFT_N1:, cs]
        lhs_scr[cs, :FFT_N1] = are * tw_re - aim * tw_im
        lhs_scr[cs, FFT_N1:] = are * tw_im + aim * tw_re


def _hy_filt_kernel(k_ref, f1_ref, tw_ref, f3_ref, o_ref, lhs_scr):
    a = _dot_x3(f1_ref[...], k_ref[...])
    _fft_forward(a, tw_ref[:, :FFT_N1], tw_ref[:, FFT_N1:], lhs_scr, FFT_CG)
    o_ref[...] = _dot_x3(lhs_scr[...], f3_ref[...])


def _hy_fft_kernel(u_ref, kf_ref, f1_ref, tw_ref, f3_ref, f3i_ref, f1i_ref, y_ref, lhs_scr, a2_scr):
    cgl = FFT_CG * FFT_N1
    x = u_ref[...].reshape(2 * u_ref.shape[1], cgl)
    a = jnp.dot(f1_ref[...], x, preferred_element_type=F32)
    tw_re, tw_im = tw_ref[:, :FFT_N1], tw_ref[:, FFT_N1:]
    _fft_forward(a, tw_re, tw_im, lhs_scr, FFT_CG)
    y = jnp.dot(lhs_scr[...].astype(BF16), f3_ref[...], preferred_element_type=F32)
    yre, yim = y[:, :FFT_N1], y[:, FFT_N1:]
    kre, kim = kf_ref[:, :FFT_N1], kf_ref[:, FFT_N1:]
    z = jnp.concatenate([yre * kre - yim * kim, yre * kim + yim * kre], axis=1).astype(BF16)
    bp = jnp.dot(z, f3i_ref[...], preferred_element_type=F32)
    for cix in range(FFT_CG):
        cs = slice(cix * FFT_N1, (cix + 1) * FFT_N1)
        bre, bim = bp[cs, :FFT_N1], bp[cs, FFT_N1:]
        a2_scr[:FFT_N1, cs] = bre * tw_re + bim * tw_im
        a2_scr[FFT_N1:, cs] = bim * tw_re - bre * tw_im
    out = jnp.dot(f1i_ref[...], a2_scr[...].astype(BF16), preferred_element_type=F32)
    y_ref[...] = out.reshape(2, u_ref.shape[1], cgl)


def _hy_conv(ut, kfilt):
    bsz, n1h, nch, _ = ut.shape
    assert 2 * n1h == FFT_N1 and bsz % 2 == 0
    cgl = FFT_CG * FFT_N1
    ncol = nch * FFT_N1
    tw, f3, f3i, f1_pair, f1_real, f1i = _fft_consts(n1h)
    kt = kfilt.reshape(FFT_N1, FFT_N1, nch).transpose(0, 2, 1).reshape(FFT_N1, ncol)
    const2 = lambda shp: pl.BlockSpec(shp, lambda *a: (0, 0))
    kf = pl.pallas_call(
        _hy_filt_kernel,
        grid=(nch // FFT_CG,),
        in_specs=[pl.BlockSpec((FFT_N1, cgl), lambda j: (0, j)), const2((2 * FFT_N1, FFT_N1)),
                  const2((FFT_N1, 2 * FFT_N1)), const2((2 * FFT_N1, 2 * FFT_N1))],
        out_specs=pl.BlockSpec((cgl, 2 * FFT_N1), lambda j: (j, 0)),
        out_shape=jax.ShapeDtypeStruct((ncol, 2 * FFT_N1), F32),
        scratch_shapes=[pltpu.VMEM((cgl, 2 * FFT_N1), F32)],
        compiler_params=_cparams(("parallel",)),
        name="hy_filter_dft",
    )(kt, jnp.asarray(f1_real, F32), jnp.asarray(tw, F32), jnp.asarray(f3, F32))
    u2 = ut.reshape(bsz, n1h, ncol)
    y = pl.pallas_call(
        _hy_fft_kernel,
        grid=(bsz // 2, nch // FFT_CG),
        in_specs=[pl.BlockSpec((2, n1h, cgl), lambda p, j: (p, 0, j)),
                  pl.BlockSpec((cgl, 2 * FFT_N1), lambda p, j: (j, 0)),
                  const2((2 * FFT_N1, FFT_N1)), const2((FFT_N1, 2 * FFT_N1)),
                  const2((2 * FFT_N1, 2 * FFT_N1)), const2((2 * FFT_N1, 2 * FFT_N1)), const2((FFT_N1, 2 * FFT_N1))],
        out_specs=pl.BlockSpec((2, n1h, cgl), lambda p, j: (p, 0, j)),
        out_shape=jax.ShapeDtypeStruct((bsz, n1h, ncol), F32),
        scratch_shapes=[pltpu.VMEM((cgl, 2 * FFT_N1), F32), pltpu.VMEM((2 * FFT_N1, cgl), F32)],
        compiler_params=_cparams(("parallel", "parallel")),
        name="hy_fft_conv",
    )(u2, kf, jnp.asarray(f1_pair, F32).astype(BF16), jnp.asarray(tw, F32), jnp.asarray(f3, F32).astype(BF16),
      jnp.asarray(f3i, F32).astype(BF16), jnp.asarray(f1i, F32).astype(BF16))
    return y.reshape(bsz, n1h, nch, FFT_N1)


def _hy_filter(seqlen, fw1, fb1, fw2, fb2, fw3, fb3, freq, fw4):
    t = jnp.linspace(0.0, 1.0, seqlen, dtype=F32)[:, None]
    bands = (HY_EMB - 1) // 2
    w = 2.0 * math.pi * jnp.arange(seqlen, dtype=F32)[:, None] / seqlen
    f = jnp.linspace(1e-4, bands - 1, bands, dtype=F32)[None, :]
    feat = jnp.concatenate([t, jnp.cos(f * w), -jnp.sin(f * w)], axis=-1)
    feat2 = jnp.concatenate([feat, feat[::-1]], axis=0)
    h = _mm(feat2, fw1, fb1, freq)
    h = _mm(h, fw2, fb2, freq)
    h = _mm(h, fw3, fb3, freq)
    deltas = jnp.abs(jnp.linspace(math.log(HY_DECAY_TARGET) / HY_DECAY_LONG_PCT,
                                  math.log(HY_DECAY_TARGET) / HY_DECAY_SHORT_PCT, HY_W, dtype=F32))
    hf = _mm(h[:seqlen], fw4[:, :HY_W]) * jnp.exp(-t * deltas)
    hb = _mm(h[seqlen:], fw4[:, HY_W:]) * jnp.exp(-t[::-1] * deltas)
    return jnp.concatenate([hf, jnp.zeros((1, HY_W), F32), hb[:seqlen - 1]], axis=0)


def _od_out_kernel(na_ref, yt_ref, x0_ref, u_ref, x_ref, mod_ref, hb_ref, wout_ref, wr_ref, br_ref,
                   x_out, h2_out, lg_out):
    y = jnp.concatenate([yt_ref[j].T for j in range(yt_ref.shape[0])], axis=0)
    hy = x0_ref[...].astype(F32) * (y + u_ref[...].astype(F32) * hb_ref[...])
    cat = jnp.concatenate([na_ref[...], hy], axis=1).astype(BF16)
    mix = jnp.dot(cat, wout_ref[...], preferred_element_type=F32)
    _mixer_tail(x_ref[...], mix, mod_ref, wr_ref, br_ref, x_out, h2_out, lg_out)


def _latent_mod_index(b, i):
    return (b, 1, 0, 0)


def _od_out(na, yt, x0, u, xcat, mods, hb, wout, wr, br):
    bsz, l, _ = na.shape
    nblk = l // TB
    tok = lambda n: pl.BlockSpec((None, TB, n), lambda b, i: (b, i, 0))
    const = lambda shp: pl.BlockSpec(shp, lambda b, i: tuple(0 for _ in shp))
    return pl.pallas_call(
        _od_out_kernel,
        grid=(bsz, nblk),
        in_specs=[tok(NA_W), pl.BlockSpec((None, TB // FFT_N1, HY_W, FFT_N1), lambda b, i: (b, i, 0, 0)),
                  tok(HY_W), tok(HY_W),
                  pl.BlockSpec((None, TB, D_MODEL), lambda b, i: (b, i + 1, 0)),
                  pl.BlockSpec((None, None, 6, D_MODEL), _latent_mod_index),
                  const((1, HY_W)), const((D_MODEL, D_MODEL)), const((D_MODEL, 128)), const((1, 128))],
        out_specs=[tok(D_MODEL), pl.BlockSpec((TB * ROW_TILE, 128), lambda b, i: (b * nblk + i, 0)), tok(128)],
        out_shape=[jax.ShapeDtypeStruct((bsz, l, D_MODEL), F32), jax.ShapeDtypeStruct((bsz * l * ROW_TILE, 128), F32),
                   jax.ShapeDtypeStruct((bsz, l, 128), F32)],
        compiler_params=_cparams(("parallel", "parallel")),
        name="od_out",
    )(na, yt, x0, u, xcat, mods, hb, wout, wr, br)


def _mods(c, c_ctx, ada_w, ada_b):
    bsz = c.shape[0]
    depth, _, n = ada_w.shape
    cc = jnp.concatenate([c, c_ctx[None]], axis=0)
    a = jnp.pad(cc * _sigmoid(cc), ((0, 8 - (bsz + 1) % 8), (0, 0)))
    mp, tn = a.shape[0], 1024
    m = pl.pallas_call(
        functools.partial(_mm_kernel, use_sin=False),
        grid=(depth, n // tn),
        in_specs=[pl.BlockSpec((mp, D_MODEL), lambda l, j: (0, 0)),
                  pl.BlockSpec((None, D_MODEL, tn), lambda l, j: (l, 0, j)),
                  pl.BlockSpec((None, 1, tn), lambda l, j: (l, 0, j)),
                  pl.BlockSpec((1, tn), lambda l, j: (0, 0))],
        out_specs=pl.BlockSpec((None, mp, tn), lambda l, j: (l, 0, j)),
        out_shape=jax.ShapeDtypeStruct((depth, mp, n), F32),
        compiler_params=_cparams(("parallel", "parallel")),
        name="adaln_dense",
    )(a, ada_w, ada_b.reshape(depth, 1, n), jnp.ones((1, tn), F32))
    mod_l = m[:, :bsz].reshape(depth, bsz, 1, 6, D_MODEL)
    mod_c = jnp.broadcast_to(m[:, bsz].reshape(depth, 1, 1, 6, D_MODEL), (depth, bsz, 1, 6, D_MODEL))
    return jnp.concatenate([mod_c, mod_l], axis=2)


def _rope_tables(seqlen):
    pos = jnp.arange(seqlen)
    half = GLA_DK // 4
    freqs = ROPE_BASE ** (-jnp.arange(half, dtype=F32) / half)
    ar = (pos // GRID_W).astype(F32)[:, None] * freqs
    ac = (pos % GRID_W).astype(F32)[:, None] * freqs
    cos = jnp.concatenate([jnp.cos(ar), jnp.cos(ar), jnp.cos(ac), jnp.cos(ac)], axis=1)
    sin = jnp.concatenate([-jnp.sin(ar), jnp.sin(ar), -jnp.sin(ac), jnp.sin(ac)], axis=1)
    cos = jnp.concatenate([jnp.ones((TB, GLA_DK), F32), cos], axis=0)
    sin = jnp.concatenate([jnp.zeros((TB, GLA_DK), F32), sin], axis=0)
    return jnp.tile(cos, (1, GLA_HEADS)), jnp.tile(sin, (1, GLA_HEADS))


def _router_weights(wg, bg, we, be):
    pad = 128 - N_GROUPS - N_EXPERTS
    wr = jnp.concatenate([wg, we, jnp.zeros((D_MODEL, pad), F32)], axis=1)
    br = jnp.concatenate([bg, be, jnp.zeros((pad,), F32)]).reshape(1, 128)
    return wr, br


def kernel(x, c, ctx, c_ctx, ada_w, ada_b, moe_wg, moe_bg, moe_we, moe_be, moe_w1, moe_w3, moe_w2, ev_w_in, ev_w_out, gla_wa2, gla_ba, gla_norm, s5_lam_re, s5_lam_im, s5_log_dt, s5_b_re, s5_b_im, s5_c_re, s5_c_im, s5_d, s5_w_glu, od_w_in, od_w_out, na_q_norm, na_k_norm, na_rpb, hy_conv_w, hy_conv_b, hy_fw1, hy_fb1, hy_fw2, hy_fb2, hy_fw3, hy_fb3, hy_freq, hy_fw4, hy_bias):
    bsz, seqlen, _ = x.shape
    assert ctx.shape[1] == TB and seqlen % TB == 0

    mods_all = _mods(c, c_ctx, ada_w, ada_b)
    mods = mods_all[0]
    w_in = ev_w_in[0]
    n_a = 2 * GLA_RANK
    a0 = 2 * GLA_QK + 2 * GLA_V
    w_ev = jnp.concatenate([w_in[:, :a0], w_in[:, a0 + n_a:], w_in[:, a0:a0 + n_a],
                            jnp.zeros((D_MODEL, 128 - n_a), F32)], axis=1).astype(BF16)
    wa = jnp.zeros((128, 2 * GLA_QK), F32)
    for d in range(2):
        wa = wa.at[d * GLA_RANK:(d + 1) * GLA_RANK, d * GLA_QK:(d + 1) * GLA_QK].set(gla_wa2[0, d])
    cos, sin = _rope_tables(seqlen)
    q, k, v, g, u, u_sw, la = _ev_proj(ctx, x, mods, w_ev, wa, gla_ba[0].reshape(1, 2 * GLA_QK), cos, sin)
    o_f, o_b = _gla(q, k, v, la)
    s5p = [t[0].astype(F32) for t in (s5_lam_re, s5_lam_im, s5_log_dt, s5_b_re, s5_b_im, s5_c_re, s5_c_im)]
    y_f = _s5_scan(u, _s5_mats(*[t[0] for t in s5p], rev=False), rev=False)
    y_b = _s5_scan(u_sw, _s5_mats(*[t[1] for t in s5p], rev=True), rev=True)
    wr, br = _router_weights(moe_wg[0], moe_bg[0], moe_we[0], moe_be[0])
    x1, h2, lg = _ev_out(o_f, o_b, g, y_f, y_b, u, ctx, x, mods,
                         jnp.tile(gla_norm[0], GLA_HEADS).reshape(1, GLA_V), s5_d[0].reshape(1, S5_W),
                         s5_w_glu[0].astype(BF16), ev_w_out[0].astype(BF16), wr, br)
    xcat = _moe(x1, h2, lg, mods, _mod_index,
                moe_w1, moe_w3, moe_w2, 0)

    mods = mods_all[1]
    hd = np.arange(NA_W) // NA_DH
    gm = jnp.asarray((hd[:, None] == hd[None, :]).astype(np.float32) / NA_DH)
    qh, kh, vh, zh = _od_proj(xcat, mods, od_w_in[0].astype(BF16), gm,
                              jnp.tile(na_q_norm[0], NA_HEADS).reshape(1, NA_W),
                              jnp.tile(na_k_norm[0], NA_HEADS).reshape(1, NA_W))
    na = _na(qh, kh, vh, _na_bias_table(na_rpb[0]))
    x0, uh, ut = _hy_pre(zh, hy_conv_w[0], hy_conv_b[0].reshape(1, 3 * HY_W))
    kfilt = _hy_filter(seqlen, hy_fw1[0], hy_fb1[0], hy_fw2[0], hy_fb2[0], hy_fw3[0], hy_fb3[0],
                       hy_freq[0], hy_fw4[0])
    yt = _hy_conv(ut, kfilt)
    wr, br = _router_weights(moe_wg[1], moe_bg[1], moe_we[1], moe_be[1])
    xl, h2, lg = _od_out(na, yt, x0, uh, xcat, mods, hy_bias[0].reshape(1, HY_W),
                         od_w_out[0].astype(BF16), wr, br)
    return _moe(xl, h2, lg, mods, _latent_mod_index,
                moe_w1, moe_w3, moe_w2, 1)
```

```python
import functools
import math

import numpy as np
import jax
import jax.numpy as jnp
from jax import lax
from jax.experimental import pallas as pl
from jax.experimental.pallas import tpu as pltpu

F32, BF16 = jnp.float32, jnp.bfloat16
HI = lax.Precision.HIGHEST

D_MODEL = 1024
GRID_W = 64
EPS = 1e-6
ROPE_BASE = 10000.0
NEG_INF = -1e30
GLA_HEADS, GLA_DK, GLA_DV = 4, 64, 128
GLA_QK, GLA_V = GLA_HEADS * GLA_DK, GLA_HEADS * GLA_DV
GLA_RANK = 16
GLA_TAU = 16.0
GLA_CHUNK = 64
GLA_LOG_ALPHA_MIN = -1.0
S5_W, S5_H, S5_P = 512, 16, 64
S5_G = S5_W // S5_H
S5_CHUNK = 8
S5_TBLK = 1408
NA_HEADS, NA_DH = 8, 64
NA_W = NA_HEADS * NA_DH
WIN_R, WIN_C = 8, 16
NA_HG = 4
NA_ROWS = 4
HY_W = 512
HY_SHORT = 3
HY_EMB = 33
HY_DECAY_TARGET = 1e-2
HY_DECAY_SHORT_PCT = 0.3
HY_DECAY_LONG_PCT = 1.5
N_GROUPS, EXP_PER_GROUP = 4, 8
N_EXPERTS = N_GROUPS * EXP_PER_GROUP
D_EXPERT = 512
TOP_K = 2
MOE_BLOCK = 256

TB = 256
PROJ_TILE = 768
HALO = 16
FFT_N1 = 128
FFT_CG = 16
V7X_VMEM_LIMIT = 52 * 1024 * 1024


def _cparams(sem):
    return pltpu.CompilerParams(dimension_semantics=sem, vmem_limit_bytes=V7X_VMEM_LIMIT)


def _sigmoid(x):
    return 1.0 / (1.0 + jnp.exp(-x))


ROW_TILE = D_MODEL // 128


def _to_token_tiles(ref, val):
    n = val.shape[0]
    for j in range(ROW_TILE):
        ref[pl.ds(j, n, stride=ROW_TILE), :] = val[:, j * 128:(j + 1) * 128]


def _from_token_tiles(ref, n):
    return jnp.concatenate([ref[pl.ds(j, n, stride=ROW_TILE), :] for j in range(ROW_TILE)], axis=1)


def _split_bf16(x):
    hi = x.astype(BF16)
    return hi, (x - hi.astype(F32)).astype(BF16)


def _dot_x3(a, b):
    a_hi, a_lo = _split_bf16(a)
    b_hi, b_lo = _split_bf16(b)
    d = lambda p, q: jnp.dot(p, q, preferred_element_type=F32)
    return d(a_hi, b_hi) + d(a_lo, b_hi) + d(a_hi, b_lo)


def _rms_rows(x):
    return x * lax.rsqrt(jnp.mean(x * x, axis=-1, keepdims=True) + EPS)


def _adaln_kernel(a_ref, w_ref, b_ref, o_ref):
    o_ref[...] = jnp.dot(a_ref[...], w_ref[...], precision=HI, preferred_element_type=F32) + b_ref[...]


def _mod_index(b, i):
    return (b, jnp.minimum(i, 1), 0, 0)


def _swapped_index(nblk, b, i):
    return (b, jnp.where(i == 0, nblk - 1, i - 1), 0)


EV_NQ, EV_NK, EV_NV, EV_NG, EV_NU = 0, 256, 512, 1024, 1536
EV_NA = 2048
EV_NTOT = 2176


def _stream_tile(ctx_ref, x_ref):
    return jnp.where(pl.program_id(1) == 0, ctx_ref[...], x_ref[...])


def _stream_specs():
    return [pl.BlockSpec((None, TB, D_MODEL), lambda b, i: (b, 0, 0)),
            pl.BlockSpec((None, TB, D_MODEL), lambda b, i: (b, jnp.maximum(i - 1, 0), 0))]


def _ev_proj_kernel(ctx_ref, x_ref, mod_ref, w_ref, wa_ref, ba_ref, cos_ref, sin_ref,
                    q_ref, k_ref, v_ref, g_ref, u_ref, usw_ref, la_ref):
    x = _stream_tile(ctx_ref, x_ref)
    h = _rms_rows(x) * (1.0 + mod_ref[1:2, :]) + mod_ref[0:1, :]
    z = jnp.dot(h.astype(BF16), w_ref[...], preferred_element_type=F32)
    lane = lax.broadcasted_iota(jnp.int32, (x.shape[0], GLA_QK), 1)
    first = (lane % 32) < 16
    cos, sin = cos_ref[...], sin_ref[...]

    def rot(t):
        partner = jnp.where(first, pltpu.roll(t, GLA_QK - 16, 1), pltpu.roll(t, 16, 1))
        return t * cos + partner * sin

    q_ref[...] = rot(z[:, EV_NQ:EV_NQ + GLA_QK]) * (GLA_DK ** -0.5)
    k_ref[...] = rot(z[:, EV_NK:EV_NK + GLA_QK])
    v_ref[...] = z[:, EV_NV:EV_NV + GLA_V].astype(BF16)
    g_ref[...] = z[:, EV_NG:EV_NG + GLA_V].astype(BF16)
    u_ref[...] = z[:, EV_NU:EV_NU + S5_W]
    usw_ref[...] = z[:, EV_NU:EV_NU + S5_W]
    a = z[:, EV_NA:EV_NA + 128]
    pre = _dot_x3(a, wa_ref[...]) + ba_ref[...]
    ls = jnp.minimum(pre, 0.0) - jnp.log1p(jnp.exp(-jnp.abs(pre)))
    la_ref[...] = jnp.maximum(ls / GLA_TAU, GLA_LOG_ALPHA_MIN)


def _ev_proj(ctx, x, mods, w, wa, ba, cos, sin):
    bsz = x.shape[0]
    lt = ctx.shape[1] + x.shape[1]
    nblk = lt // TB
    tok = lambda n: pl.BlockSpec((None, TB, n), lambda b, i: (b, i, 0))
    const = lambda shp: pl.BlockSpec(shp, lambda b, i: tuple(0 for _ in shp))
    return pl.pallas_call(
        _ev_proj_kernel,
        grid=(bsz, nblk),
        in_specs=_stream_specs() + [
                  pl.BlockSpec((None, None, 6, D_MODEL), _mod_index),
                  const((D_MODEL, EV_NTOT)), const((128, 2 * GLA_QK)), const((1, 2 * GLA_QK)),
                  pl.BlockSpec((TB, GLA_QK), lambda b, i: (i, 0)),
                  pl.BlockSpec((TB, GLA_QK), lambda b, i: (i, 0))],
        out_specs=[tok(GLA_QK), tok(GLA_QK), tok(GLA_V), tok(GLA_V), tok(S5_W),
                   pl.BlockSpec((None, TB, S5_W), functools.partial(_swapped_index, nblk)), tok(2 * GLA_QK)],
        out_shape=[jax.ShapeDtypeStruct((bsz, lt, GLA_QK), F32),
                   jax.ShapeDtypeStruct((bsz, lt, GLA_QK), F32),
                   jax.ShapeDtypeStruct((bsz, lt, GLA_V), BF16),
                   jax.ShapeDtypeStruct((bsz, lt, GLA_V), BF16),
                   jax.ShapeDtypeStruct((bsz, lt, S5_W), F32),
                   jax.ShapeDtypeStruct((bsz, lt, S5_W), F32),
                   jax.ShapeDtypeStruct((bsz, lt, 2 * GLA_QK), F32)],
        compiler_params=_cparams(("parallel", "parallel")),
        name="ev_proj",
    )(ctx, x, mods, w, wa, ba, cos, sin)


def _gla_kernel(qf_ref, kf_ref, vf_ref, laf_ref, qb_ref, kb_ref, vb_ref, lab_ref,
                of_ref, ob_ref, s_scr):
    i = pl.program_id(1)

    @pl.when(i == 0)
    def _():
        s_scr[...] = jnp.zeros_like(s_scr)

    c = GLA_CHUNK
    nh = GLA_HEADS
    row = lax.broadcasted_iota(jnp.int32, (c, c), 0)
    col = lax.broadcasted_iota(jnp.int32, (c, c), 1)
    row4 = lax.broadcasted_iota(jnp.int32, (nh * c, c), 0) % c
    col4 = lax.broadcasted_iota(jnp.int32, (nh * c, c), 1)
    lane_head = lax.broadcasted_iota(jnp.int32, (c, GLA_QK), 1) // GLA_DK
    out_head = lax.broadcasted_iota(jnp.int32, (c, GLA_V), 1) // GLA_DV
    bd_mask = (lax.broadcasted_iota(jnp.int32, (GLA_V, GLA_QK), 0) // GLA_DV
               == lax.broadcasted_iota(jnp.int32, (GLA_V, GLA_QK), 1) // GLA_DK)
    nchunk = qf_ref.shape[0] // c
    nt = (((1,), (1,)), ((), ()))
    tn = (((0,), (0,)), ((), ()))

    def one_chunk(refs, o_ref, d, r0):
        q_ref, k_ref, v_ref, la_ref = refs
        fwd = d == 0
        sl = pl.ds(r0, c)
        qc, kc, vc, lac = q_ref[sl, :], k_ref[sl, :], v_ref[sl, :], la_ref[sl, :]
        tri = ((row >= col) if fwd else (row <= col)).astype(BF16)
        la_hi, la_lo = _split_bf16(lac)
        b = (jnp.dot(tri, la_hi, preferred_element_type=F32)
             + jnp.dot(tri, la_lo, preferred_element_type=F32))
        b_last = b[c - 1:c, :] if fwd else b[0:1, :]
        qe = (qc * jnp.exp(b)).astype(BF16)
        ke = (kc * jnp.exp(-b)).astype(BF16)
        kd = (kc * jnp.exp(b_last - b)).astype(BF16)
        st = s_scr[d]
        o = lax.dot_general(qe, st.astype(BF16), nt, preferred_element_type=F32)
        q4 = jnp.concatenate([jnp.where(lane_head == h, qe, jnp.zeros_like(qe)) for h in range(nh)], axis=0)
        att = lax.dot_general(q4, ke, nt, preferred_element_type=F32)
        att_mask = (row4 >= col4) if fwd else (row4 < col4)
        o4 = jnp.dot(jnp.where(att_mask, att, 0.0).astype(BF16), vc, preferred_element_type=F32)
        for h in range(nh):
            o = o + jnp.where(out_head == h, o4[h * c:(h + 1) * c, :], 0.0)
        o_ref[sl, :] = o.astype(o_ref.dtype)
        upd_t = lax.dot_general(vc, kd, tn, preferred_element_type=F32)
        s_scr[d] = st * jnp.exp(b_last) + jnp.where(bd_mask, upd_t, 0.0)

    def body(j, carry):
        one_chunk((qf_ref, kf_ref, vf_ref, laf_ref), of_ref, 0, pl.multiple_of(j * c, c))
        one_chunk((qb_ref, kb_ref, vb_ref, lab_ref), ob_ref, 1, pl.multiple_of((nchunk - 1 - j) * c, c))
        return carry

    lax.fori_loop(0, nchunk, body, 0, unroll=True)


def _gla(q, k, v, la):
    bsz, lt, _ = q.shape
    nblk = lt // TB
    fwd_map = lambda b, i: (b, i, 0)
    bwd_blk = lambda i: jnp.where(i == 0, 0, nblk - i)
    bwd_map = lambda b, i: (b, bwd_blk(i), 0)
    bwd_map_la = lambda b, i: (b, bwd_blk(i), 1)
    spec = lambda n, m: pl.BlockSpec((None, TB, n), m)
    return pl.pallas_call(
        _gla_kernel,
        grid=(bsz, nblk),
        in_specs=[spec(GLA_QK, fwd_map), spec(GLA_QK, fwd_map), spec(GLA_V, fwd_map), spec(GLA_QK, fwd_map),
                  spec(GLA_QK, bwd_map), spec(GLA_QK, bwd_map), spec(GLA_V, bwd_map), spec(GLA_QK, bwd_map_la)],
        out_specs=[spec(GLA_V, fwd_map), spec(GLA_V, bwd_map)],
        out_shape=[jax.ShapeDtypeStruct((bsz, lt, GLA_V), BF16), jax.ShapeDtypeStruct((bsz, lt, GLA_V), BF16)],
        scratch_shapes=[pltpu.VMEM((2, GLA_V, GLA_QK), F32)],
        compiler_params=_cparams(("parallel", "arbitrary")),
        name="gla_scan",
    )(q, k, v, la, q, k, v, la)


def _s5_kernel(u_ref, wm_ref, tm_ref, cm_ref, ar_ref, ai_ref, y_ref, w_scr, hp_scr, h_scr, *, rev):
    @pl.when(pl.program_id(1) == 0)
    def _():
        h_scr[...] = jnp.zeros_like(h_scr)

    bsz, ntok, _ = u_ref.shape
    nc = ntok // S5_CHUNK
    half = 8 * S5_P
    x = jnp.concatenate(
        [jnp.concatenate([u_ref[b, pl.ds(s, nc, stride=S5_CHUNK), :] for s in range(S5_CHUNK)], axis=1)
         for b in range(bsz)], axis=0).astype(BF16)
    w_scr[...] = jnp.dot(x, wm_ref[...], preferred_element_type=F32)
    ar, ai = ar_ref[...], ai_ref[...]

    def body(j, hs):
        c = (nc - 1 - j) if rev else j
        out = []
        for b in range(bsz):
            re, im = hs[b]
            r = b * nc + c
            hp_scr[pl.ds(r, 1), :] = jnp.concatenate([re, im], axis=1)
            w = w_scr[pl.ds(r, 1), :]
            out.append((ar * re - ai * im + w[:, :half], ar * im + ai * re + w[:, half:]))
        return tuple(out)

    hs = lax.fori_loop(0, nc, body, tuple((h_scr[b:b + 1, :half], h_scr[b:b + 1, half:]) for b in range(bsz)))
    for b in range(bsz):
        h_scr[b:b + 1, :] = jnp.concatenate(hs[b], axis=1)
    mt = 256
    ntile = x.shape[1] // mt
    cols = []
    for jt in range(ntile):
        acc = None
        for it in (range(jt, ntile) if rev else range(jt + 1)):
            part = jnp.dot(x[:, it * mt:(it + 1) * mt], tm_ref[it * mt:(it + 1) * mt, jt * mt:(jt + 1) * mt],
                           preferred_element_type=F32)
            acc = part if acc is None else acc + part
        cols.append(acc)
    y = (jnp.concatenate(cols, axis=1)
         + jnp.dot(hp_scr[...].astype(BF16), cm_ref[...], preferred_element_type=F32))
    for b in range(bsz):
        for s in range(S5_CHUNK):
            y_ref[b, pl.ds(s, nc, stride=S5_CHUNK), :] = y[b * nc:(b + 1) * nc, s * 128:(s + 1) * 128]


def _s5_mats(lam_re, lam_im, log_dt, b_re, b_im, c_re, c_im, rev):
    t16 = S5_CHUNK
    dt = jnp.exp(log_dt)[:, None]
    mag = jnp.exp(lam_re * dt)
    a_re, a_im = mag * jnp.cos(lam_im * dt), mag * jnp.sin(lam_im * dt)
    den = lam_re * lam_re + lam_im * lam_im
    nr = a_re - 1.0
    co_re = ((nr * lam_re + a_im * lam_im) / den)[..., None]
    co_im = ((a_im * lam_re - nr * lam_im) / den)[..., None]
    bb_re, bb_im = co_re * b_re - co_im * b_im, co_re * b_im + co_im * b_re
    pr, pi = [jnp.ones_like(a_re)], [jnp.zeros_like(a_im)]
    for _ in range(t16):
        pr, pi = pr + [pr[-1] * a_re - pi[-1] * a_im], pi + [pr[-1] * a_im + pi[-1] * a_re]
    pw_re, pw_im = jnp.stack(pr), jnp.stack(pi)
    g = lam_re.shape[0]
    e_re, e_im = pw_re[t16 - 1::-1][:t16], pw_im[t16 - 1::-1][:t16]
    wre = jnp.einsum('sgp,gph->gshp', e_re, bb_re) - jnp.einsum('sgp,gph->gshp', e_im, bb_im)
    wim = jnp.einsum('sgp,gph->gshp', e_re, bb_im) + jnp.einsum('sgp,gph->gshp', e_im, bb_re)
    cb_re = jnp.einsum('gkp,gph->gpkh', c_re, bb_re) - jnp.einsum('gkp,gph->gpkh', c_im, bb_im)
    cb_im = jnp.einsum('gkp,gph->gpkh', c_re, bb_im) + jnp.einsum('gkp,gph->gpkh', c_im, bb_re)
    kd = jnp.einsum('dgp,gpkh->dgkh', pw_re[:t16], cb_re) - jnp.einsum('dgp,gpkh->dgkh', pw_im[:t16], cb_im)
    lag = np.arange(t16)[None, :] - np.arange(t16)[:, None]
    toe = jnp.where((lag >= 0)[:, :, None, None, None], kd[np.clip(lag, 0, t16 - 1)], 0.0)
    toe = toe.transpose(2, 0, 4, 1, 3)
    q_re, q_im = pw_re[1:], pw_im[1:]
    ca_re = jnp.einsum('gkp,tgp->gptk', c_re, q_re) - jnp.einsum('gkp,tgp->gptk', c_im, q_im)
    ca_im = jnp.einsum('gkp,tgp->gptk', c_re, q_im) + jnp.einsum('gkp,tgp->gptk', c_im, q_re)
    if rev:
        wre, wim = wre[:, ::-1], wim[:, ::-1]
        toe = toe[:, ::-1, :, ::-1]
        ca_re, ca_im = ca_re[:, :, ::-1], ca_im[:, :, ::-1]
    nq, gl = g // 8, 8
    nd = t16 * S5_H
    tok_hot = np.zeros((gl, nd, t16 * 128), np.float32)
    st_hot = np.zeros((gl, 2 * S5_P, 2 * gl * S5_P), np.float32)
    for gi in range(gl):
        a = np.arange(nd)
        tok_hot[gi, a, (a // S5_H) * 128 + gi * S5_H + a % S5_H] = 1.0
        a = np.arange(2 * S5_P)
        st_hot[gi, a, (a // S5_P) * gl * S5_P + gi * S5_P + a % S5_P] = 1.0
    place = lambda rows, blk, cols: jnp.einsum('gar,qgab,gbc->qrc', rows, blk, cols)
    wg = jnp.concatenate([wre, wim], axis=-1).reshape(nq, gl, nd, 2 * S5_P)
    tg = toe.reshape(nq, gl, nd, nd)
    cg = jnp.concatenate([ca_re, -ca_im], axis=1).reshape(nq, gl, 2 * S5_P, nd)
    wm = place(tok_hot, wg, st_hot)
    tmat = place(tok_hot, tg, tok_hot)
    cm = place(st_hot, cg, tok_hot)
    return (wm.astype(BF16), tmat.astype(BF16), cm.astype(BF16),
            pw_re[t16].reshape(nq, 1, gl * S5_P), pw_im[t16].reshape(nq, 1, gl * S5_P))


def _s5_scan(u, mats, rev):
    bsz, lt, _ = u.shape
    tblk = max(d for d in range(64, S5_TBLK + 1, 64) if lt % d == 0)
    nblk = lt // tblk
    nq = S5_G // 8
    wm, tmat, cm, ar, ai = mats
    kw, ks = S5_CHUNK * 128, 8 * S5_P
    tmap = (lambda q, t: (0, nblk - 1 - t, q)) if rev else (lambda q, t: (0, t, q))
    per = lambda shp: pl.BlockSpec((None,) + shp, lambda q, t: (q,) + tuple(0 for _ in shp))
    rows = bsz * tblk // S5_CHUNK
    return pl.pallas_call(
        functools.partial(_s5_kernel, rev=rev),
        grid=(nq, nblk),
        in_specs=[pl.BlockSpec((bsz, tblk, 128), tmap),
                  per((kw, 2 * ks)), per((kw, kw)), per((2 * ks, kw)), per((1, ks)), per((1, ks))],
        out_specs=pl.BlockSpec((bsz, tblk, 128), tmap),
        out_shape=jax.ShapeDtypeStruct((bsz, lt, S5_W), F32),
        scratch_shapes=[pltpu.VMEM((rows, 2 * ks), F32), pltpu.VMEM((rows, 2 * ks), F32),
                        pltpu.VMEM((8, 2 * ks), F32)],
        compiler_params=_cparams(("parallel", "arbitrary")),
        name="s5_scan",
    )(u, wm, tmat, cm, ar, ai)


def _mixer_tail(x, mix, mod_ref, wr_ref, br_ref, x_out, h2_out, lg_out):
    xn = x + mod_ref[2:3, :] * mix
    x_out[...] = xn
    h2 = _rms_rows(xn) * (1.0 + mod_ref[4:5, :]) + mod_ref[3:4, :]
    _to_token_tiles(h2_out, h2)
    lg_out[...] = _route_tail(_dot_x3(h2, wr_ref[...]) + br_ref[...])


def _route_tail(lg):
    lane = lax.broadcasted_iota(jnp.int32, lg.shape, 1)
    lane_f = lane.astype(F32)
    big = jnp.float32(128.0)
    row_max = lambda t: jnp.max(t, axis=1, keepdims=True)
    first_lane = lambda hit: jnp.min(jnp.where(hit, lane_f, big), axis=1, keepdims=True)
    is_g = lane < N_GROUPS
    g_logit = jnp.where(is_g, lg, NEG_INF)
    g_max = row_max(g_logit)
    grp = first_lane(g_logit == g_max)
    g_w = 1.0 / jnp.sum(jnp.where(is_g, jnp.exp(lg - g_max), 0.0), axis=1, keepdims=True)
    lo = N_GROUPS + EXP_PER_GROUP * grp
    in_grp = jnp.logical_and(lane_f >= lo, lane_f < lo + EXP_PER_GROUP)
    e_max = row_max(jnp.where(in_grp, lg, NEG_INF))
    pe = jnp.where(in_grp, jnp.exp(lg - e_max), 0.0)
    p = pe / jnp.sum(pe, axis=1, keepdims=True)
    cand = jnp.where(in_grp, p, -1.0)
    p1 = row_max(cand)
    i1 = first_lane(cand == p1)
    cand2 = jnp.where(lane_f == i1, -1.0, cand)
    p2 = row_max(cand2)
    i2 = first_lane(cand2 == p2)
    scale = g_w / (p1 + p2)
    out = jnp.where(lane == 0, scale * p1, jnp.where(lane == 1, scale * p2,
                    jnp.where(lane == 2, i1 - N_GROUPS, jnp.where(lane == 3, i2 - N_GROUPS, 0.0))))
    return out


def _ev_out_kernel(of_ref, ob_ref, g_ref, yf_ref, yb_ref, u_ref, ctx_ref, x_ref, mod_ref,
                   gn_ref, ds_ref, wglu_ref, wout_ref, wr_ref, br_ref,
                   x_out, h2_out, lg_out):
    o = of_ref[...].astype(F32) + ob_ref[...].astype(F32)
    og = jnp.concatenate([_rms_rows(o[:, h * GLA_DV:(h + 1) * GLA_DV]) for h in range(GLA_HEADS)], axis=1)
    g = g_ref[...].astype(F32)
    og = og * gn_ref[...] * (g * _sigmoid(g))
    t = yf_ref[...] + yb_ref[...] + ds_ref[...] * u_ref[...]
    y = t * (0.5 * (1.0 + jnp.tanh(math.sqrt(2.0 / math.pi) * (t + 0.044715 * (t * t * t)))))
    y = y * _sigmoid(jnp.dot(y.astype(BF16), wglu_ref[...], preferred_element_type=F32))
    cat = jnp.concatenate([og, y], axis=1).astype(BF16)
    mix = jnp.dot(cat, wout_ref[...], preferred_element_type=F32)
    _mixer_tail(_stream_tile(ctx_ref, x_ref), mix, mod_ref, wr_ref, br_ref, x_out, h2_out, lg_out)


def _ev_out(o_f, o_b, g, y_f, y_b, u, ctx, x, mods, gn, ds, wglu, wout, wr, br):
    bsz, lt, _ = u.shape
    nblk = lt // TB
    tok = lambda n: pl.BlockSpec((None, TB, n), lambda b, i: (b, i, 0))
    const = lambda shp: pl.BlockSpec(shp, lambda b, i: tuple(0 for _ in shp))
    return pl.pallas_call(
        _ev_out_kernel,
        grid=(bsz, nblk),
        in_specs=[tok(512), tok(512), tok(512), tok(512),
                  pl.BlockSpec((None, TB, S5_W), functools.partial(_swapped_index, nblk)),
                  tok(512)] + _stream_specs() + [
                  pl.BlockSpec((None, None, 6, D_MODEL), _mod_index),
                  const((1, 512)), const((1, 512)), const((512, 512)), const((D_MODEL, D_MODEL)),
                  const((D_MODEL, 128)), const((1, 128))],
        out_specs=[tok(D_MODEL), pl.BlockSpec((TB * ROW_TILE, 128), lambda b, i: (b * nblk + i, 0)), tok(128)],
        out_shape=[jax.ShapeDtypeStruct((bsz, lt, D_MODEL), F32),
                   jax.ShapeDtypeStruct((bsz * lt * ROW_TILE, 128), F32),
                   jax.ShapeDtypeStruct((bsz, lt, 128), F32)],
        compiler_params=_cparams(("parallel", "parallel")),
        name="ev_out",
    )(o_f, o_b, g, y_f, y_b, u, ctx, x, mods, gn, ds, wglu, wout, wr, br)


def _route(route):
    n_tok = route.shape[0]
    gate = route[:, :TOP_K]
    eid = route[:, TOP_K:2 * TOP_K].astype(jnp.int32).reshape(-1)
    n_asg = n_tok * TOP_K
    order = jnp.argsort(eid).astype(jnp.int32)
    counts = jnp.sum((eid[:, None] == jnp.arange(N_EXPERTS)[None, :]).astype(jnp.int32), axis=0)
    padded = (counts + MOE_BLOCK - 1) // MOE_BLOCK * MOE_BLOCK
    pad_end = jnp.cumsum(padded)
    pad_start = pad_end - padded
    cnt_start = jnp.cumsum(counts) - counts
    n_blocks = -(-n_asg // MOE_BLOCK) + N_EXPERTS
    blk_start = jnp.arange(n_blocks, dtype=jnp.int32) * MOE_BLOCK
    blk_e = jnp.minimum(jnp.sum((pad_end[None, :] <= blk_start[:, None]).astype(jnp.int32), axis=1), N_EXPERTS - 1)
    pos = jnp.arange(n_blocks * MOE_BLOCK, dtype=jnp.int32)
    pos_e = jnp.repeat(blk_e, MOE_BLOCK)
    rank = pos - pad_start[pos_e]
    src = jnp.clip(cnt_start[pos_e] + rank, 0, n_asg - 1)
    slot_buf = jnp.where(rank < counts[pos_e], order[src], n_asg).astype(jnp.int32)
    n_valid = jnp.sum((slot_buf < n_asg).reshape(n_blocks, MOE_BLOCK), axis=1).astype(jnp.int32)
    tok = lax.shift_right_logical(slot_buf, 1)
    src_rows = jnp.minimum(tok, n_tok - 1) * ROW_TILE
    spare = n_asg + (pos // MOE_BLOCK % 2) * MOE_BLOCK + pos % MOE_BLOCK
    dst_rows = jnp.where(slot_buf < n_asg, (slot_buf & 1) * n_tok + tok, spare)
    lead = n_asg + jnp.arange(2 * MOE_BLOCK, dtype=jnp.int32)
    dst_rows = jnp.concatenate([lead, dst_rows]) * ROW_TILE
    return src_rows, dst_rows, blk_e, n_valid, gate.astype(F32)


def _moe_kernel(src_ref, dst_ref, blke_ref, nvalid_ref, h_hbm, w1_ref, w3_ref, w2_ref, z_hbm,
                xbuf0, xbuf1, ybuf0, ybuf1, wb1, wb3, wb2, gsem, ssem):
    i = pl.program_id(0)
    nblk = pl.num_programs(0)
    ns = ROW_TILE
    xb, yb = (xbuf0, xbuf1), (ybuf0, ybuf1)
    lead = 2 * MOE_BLOCK

    def issue_gather(blk, buf):
        base = blk * MOE_BLOCK
        for r in range(MOE_BLOCK):
            src = pl.multiple_of(src_ref[base + r], ns)
            pltpu.make_async_copy(h_hbm.at[pl.ds(src, ns)], xb[buf].at[pl.ds(r * ns, ns)],
                                  gsem.at[buf]).start(priority=r % 2)

    def issue_scatter(blk, buf):
        base = lead + blk * MOE_BLOCK
        for r in range(MOE_BLOCK):
            dst = pl.multiple_of(dst_ref[base + r], ns)
            pltpu.make_async_copy(yb[buf].at[pl.ds(r * ns, ns)], z_hbm.at[pl.ds(dst, ns)],
                                  ssem.at[buf]).start(priority=r % 2)

    def wait_gather(buf):
        pltpu.make_async_copy(h_hbm.at[pl.ds(0, MOE_BLOCK * ns)], xb[buf], gsem.at[buf]).wait()

    def wait_scatter(buf):
        pltpu.make_async_copy(yb[buf], z_hbm.at[pl.ds(0, MOE_BLOCK * ns)], ssem.at[buf]).wait()

    used = nvalid_ref[i] > 0

    @pl.when(i == 0)
    def _():
        ybuf0[...] = jnp.zeros_like(ybuf0)
        ybuf1[...] = jnp.zeros_like(ybuf1)
        issue_scatter(-2, 0)
        issue_gather(0, 0)

    def step(cur):
        oth = 1 - cur
        wait_gather(cur)
        issue_gather(jnp.minimum(i + 1, nblk - 1), oth)
        issue_scatter(i - 1, oth)
        x = _from_token_tiles(xb[cur], MOE_BLOCK).astype(BF16)

        @pl.when(jnp.logical_or(i == 0, blke_ref[i] != blke_ref[jnp.maximum(i - 1, 0)]))
        def _():
            wb1[...] = w1_ref[...].astype(BF16)
            wb3[...] = w3_ref[...].astype(BF16)
            wb2[...] = w2_ref[...].astype(BF16)

        h1 = jnp.dot(x, wb1[...], preferred_element_type=F32)
        h3 = jnp.dot(x, wb3[...], preferred_element_type=F32)
        a = (h1 * _sigmoid(h1) * h3).astype(BF16)
        y = jnp.dot(a, wb2[...], preferred_element_type=F32)
        wait_scatter(cur)
        _to_token_tiles(yb[cur], y)

    def drain(last_par):
        wait_gather(1 - last_par)
        issue_scatter(jnp.where(used, i, i - 1), last_par)
        wait_scatter(1 - last_par)
        wait_scatter(last_par)

    for par in range(2):
        @pl.when(jnp.logical_and(used, i % 2 == par))
        def _():
            step(par)

    first_unused = jnp.logical_and(jnp.logical_not(used),
                                   jnp.logical_and(i > 0, nvalid_ref[jnp.maximum(i - 1, 0)] > 0))
    last_used = jnp.logical_and(used, i == nblk - 1)
    for par in range(2):
        @pl.when(jnp.logical_or(jnp.logical_and(first_unused, (i - 1) % 2 == par),
                                jnp.logical_and(last_used, i % 2 == par)))
        def _():
            drain(par)


def _moe_experts(h2, src_rows, dst_rows, blk_e, n_valid, w1, w3, w2, layer):
    n_tok = h2.shape[0] // ROW_TILE
    n_blocks = blk_e.shape[0]
    wspec = lambda shp: pl.BlockSpec((None, None) + shp, lambda i, src, dst, blke, nvalid: (layer, blke[i], 0, 0))
    grid_spec = pltpu.PrefetchScalarGridSpec(
        num_scalar_prefetch=4,
        grid=(n_blocks,),
        in_specs=[pl.BlockSpec(memory_space=pl.ANY),
                  wspec((D_MODEL, D_EXPERT)), wspec((D_MODEL, D_EXPERT)), wspec((D_EXPERT, D_MODEL))],
        out_specs=pl.BlockSpec(memory_space=pl.ANY),
        scratch_shapes=[pltpu.VMEM((MOE_BLOCK * ROW_TILE, 128), F32), pltpu.VMEM((MOE_BLOCK * ROW_TILE, 128), F32),
                        pltpu.VMEM((MOE_BLOCK * ROW_TILE, 128), F32), pltpu.VMEM((MOE_BLOCK * ROW_TILE, 128), F32),
                        pltpu.VMEM((D_MODEL, D_EXPERT), BF16), pltpu.VMEM((D_MODEL, D_EXPERT), BF16),
                        pltpu.VMEM((D_EXPERT, D_MODEL), BF16),
                        pltpu.SemaphoreType.DMA((2,)), pltpu.SemaphoreType.DMA((2,))])
    return pl.pallas_call(
        _moe_kernel,
        grid_spec=grid_spec,
        out_shape=jax.ShapeDtypeStruct(((TOP_K * n_tok + 2 * MOE_BLOCK) * ROW_TILE, 128), F32),
        compiler_params=_cparams(("arbitrary",)),
        name="moe_experts",
    )(src_rows, dst_rows, blk_e, n_valid, h2, w1, w3, w2)


def _moe_combine_kernel(x_ref, z0_ref, z1_ref, gate_ref, mod_ref, o_ref):
    gate = gate_ref[...]
    tb = x_ref.shape[0]
    y = gate[:, 0:1] * _from_token_tiles(z0_ref, tb) + gate[:, 1:2] * _from_token_tiles(z1_ref, tb)
    o_ref[...] = x_ref[...] + mod_ref[5:6, :] * y


def _moe_combine(x, z, gate, mods, mod_index):
    bsz, lt, _ = x.shape
    nblk = lt // TB
    gate3 = gate.reshape(bsz, lt, TOP_K)
    return pl.pallas_call(
        _moe_combine_kernel,
        grid=(bsz, nblk),
        in_specs=[pl.BlockSpec((None, TB, D_MODEL), lambda b, i: (b, i, 0)),
                  pl.BlockSpec((TB * ROW_TILE, 128), lambda b, i: (b * nblk + i, 0)),
                  pl.BlockSpec((TB * ROW_TILE, 128), lambda b, i: ((bsz + b) * nblk + i, 0)),
                  pl.BlockSpec((None, TB, TOP_K), lambda b, i: (b, i, 0)),
                  pl.BlockSpec((None, None, 6, D_MODEL), mod_index)],
        out_specs=pl.BlockSpec((None, TB, D_MODEL), lambda b, i: (b, i, 0)),
        out_shape=jax.ShapeDtypeStruct((bsz, lt, D_MODEL), F32),
        compiler_params=_cparams(("parallel", "parallel")),
        name="moe_combine",
    )(x, z, z, gate3, mods)


def _moe(x, h2, logits, mods, mod_index, w1, w3, w2, layer):
    bsz, lt, _ = x.shape
    src_rows, dst_rows, blk_e, n_valid, gate = _route(logits.reshape(bsz * lt, 128))
    z = _moe_experts(h2, src_rows, dst_rows, blk_e, n_valid, w1, w3, w2, layer)
    return _moe_combine(x, z, gate, mods, mod_index)


def _od_proj_kernel(x_ref, mod_ref, w_ref, gm_ref, qn_ref, kn_ref, q_ref, k_ref, v_ref, zh_ref):
    tm = x_ref.shape[0]
    is_ctx = pl.program_id(1) * tm + lax.broadcasted_iota(jnp.int32, (tm, 1), 0) < TB
    shift = jnp.where(is_ctx, mod_ref[0, 0:1, :], mod_ref[1, 0:1, :])
    scale = jnp.where(is_ctx, mod_ref[0, 1:2, :], mod_ref[1, 1:2, :])
    h = _rms_rows(x_ref[...]) * (1.0 + scale) + shift
    z = jnp.dot(h.astype(BF16), w_ref[...], preferred_element_type=F32)

    def head_norm(t, gain):
        sq_hi, sq_lo = _split_bf16(t * t)
        gm = gm_ref[...].astype(BF16)
        ms = jnp.dot(sq_hi, gm, preferred_element_type=F32) + jnp.dot(sq_lo, gm, preferred_element_type=F32)
        return t * lax.rsqrt(ms + EPS) * gain

    q_ref[...] = (head_norm(z[:, :NA_W], qn_ref[...]) * (NA_DH ** -0.5)).astype(BF16)
    k_ref[...] = head_norm(z[:, NA_W:2 * NA_W], kn_ref[...]).astype(BF16)
    v_ref[...] = z[:, 2 * NA_W:3 * NA_W].astype(BF16)
    zh_ref[...] = z[:, 3 * NA_W:].astype(BF16)


def _od_proj(xcat, mods, w, gm, qn, kn):
    bsz, lt, _ = xcat.shape
    tm = max(d for d in range(TB, PROJ_TILE + 1, TB) if lt % d == 0)
    nblk = lt // tm
    tok = lambda n: pl.BlockSpec((None, tm, n), lambda b, i: (b, i, 0))
    const = lambda shp: pl.BlockSpec(shp, lambda b, i: tuple(0 for _ in shp))
    return pl.pallas_call(
        _od_proj_kernel,
        grid=(bsz, nblk),
        in_specs=[tok(D_MODEL), pl.BlockSpec((None, 2, 6, D_MODEL), lambda b, i: (b, 0, 0, 0)),
                  const((D_MODEL, 3 * NA_W + 3 * HY_W)), const((NA_W, NA_W)), const((1, NA_W)), const((1, NA_W))],
        out_specs=[tok(NA_W), tok(NA_W), tok(NA_W), tok(3 * HY_W)],
        out_shape=[jax.ShapeDtypeStruct((bsz, lt, NA_W), BF16), jax.ShapeDtypeStruct((bsz, lt, NA_W), BF16),
                   jax.ShapeDtypeStruct((bsz, lt, NA_W), BF16), jax.ShapeDtypeStruct((bsz, lt, 3 * HY_W), BF16)],
        compiler_params=_cparams(("parallel", "parallel")),
        name="od_proj",
    )(xcat, mods, w, gm, qn, kn)


def _na_kernel(q_ref, k_ref, v_ref, t2_ref, o_ref):
    for j in range(NA_ROWS):
        _na_one_row(q_ref, k_ref, v_ref, t2_ref, o_ref, j)


def _na_one_row(q_ref, k_ref, v_ref, t2_ref, o_ref, j):
    r = pl.program_id(1) * NA_ROWS + j
    n_rows = pl.num_programs(1) * NA_ROWS
    r0 = jnp.clip(r - WIN_R // 2, 0, n_rows - WIN_R)
    off = r0 - r + WIN_R - 1
    base = pl.multiple_of(TB + r0 * GRID_W, GRID_W)
    nloc = WIN_R * GRID_W
    q = q_ref[j * GRID_W:(j + 1) * GRID_W, :]
    hg = NA_HG
    gw = hg * NA_DH
    lane_head = lax.broadcasted_iota(jnp.int32, (GRID_W, gw), 1) // NA_DH
    nt = (((1,), (1,)), ((), ()))
    outs = []
    for grp in range(NA_HEADS // hg):
        cs = slice(gw * grp, gw * (grp + 1))
        q2 = q[:, cs]
        q4 = jnp.concatenate([jnp.where(lane_head == h, q2, jnp.zeros_like(q2)) for h in range(hg)], axis=0)
        kw, vw = k_ref[pl.ds(base, nloc), cs], v_ref[pl.ds(base, nloc), cs]
        kc, vc = k_ref[0:TB, cs], v_ref[0:TB, cs]
        bias = jnp.concatenate(
            [jnp.concatenate([t2_ref[hg * grp + h, off + 2 * m] for m in range(WIN_R // 2)], axis=1)
             for h in range(hg)], axis=0)
        s_loc = lax.dot_general(q4, kw, nt, preferred_element_type=F32) + bias
        s_ctx = lax.dot_general(q4, kc, nt, preferred_element_type=F32)
        m = jnp.maximum(jnp.max(s_loc, axis=1, keepdims=True), jnp.max(s_ctx, axis=1, keepdims=True))
        p_loc, p_ctx = jnp.exp(s_loc - m), jnp.exp(s_ctx - m)
        den = jnp.sum(p_loc, axis=1, keepdims=True) + jnp.sum(p_ctx, axis=1, keepdims=True)
        o4 = (jnp.dot(p_loc.astype(BF16), vw, preferred_element_type=F32)
              + jnp.dot(p_ctx.astype(BF16), vc, preferred_element_type=F32)) / den
        acc = jnp.zeros((GRID_W, gw), F32)
        for h in range(hg):
            acc = jnp.where(lane_head == h, o4[h * GRID_W:(h + 1) * GRID_W, :], acc)
        outs.append(acc)
    o_ref[j * GRID_W:(j + 1) * GRID_W, :] = jnp.concatenate(outs, axis=1)


def _na_bias_table(rpb):
    qc = np.arange(GRID_W)[:, None]
    kc = np.arange(GRID_W)[None, :]
    q_start = np.clip(qc - WIN_C // 2, 0, GRID_W - WIN_C)
    valid = (kc >= q_start) & (kc < q_start + WIN_C)
    col_idx = np.clip(kc - qc + WIN_C - 1, 0, 2 * WIN_C - 2)
    hot = np.zeros((GRID_W, GRID_W, 2 * WIN_C - 1), np.float32)
    hot[qc, kc, col_idx] = 1.0
    t = jnp.einsum('hrj,qkj->hrqk', rpb.astype(F32), hot, precision=HI)
    t = jnp.where(valid[None, None], t, NEG_INF)
    return jnp.concatenate([t[:, :-1], t[:, 1:]], axis=-1)


def _na(q, k, v, t2):
    bsz, lt, _ = q.shape
    qrows = NA_ROWS * GRID_W
    n_rows = (lt - TB) // qrows
    qoff = TB // qrows
    return pl.pallas_call(
        _na_kernel,
        grid=(bsz, n_rows),
        in_specs=[pl.BlockSpec((None, qrows, NA_W), lambda b, r: (b, r + qoff, 0)),
                  pl.BlockSpec((None, lt, NA_W), lambda b, r: (b, 0, 0)),
                  pl.BlockSpec((None, lt, NA_W), lambda b, r: (b, 0, 0)),
                  pl.BlockSpec(t2.shape, lambda b, r: (0, 0, 0, 0))],
        out_specs=pl.BlockSpec((None, qrows, NA_W), lambda b, r: (b, r, 0)),
        out_shape=jax.ShapeDtypeStruct((bsz, lt - TB, NA_W), F32),
        compiler_params=_cparams(("parallel", "arbitrary")),
        name="na_attn",
    )(q, k, v, t2)


def _hy_pre_kernel(z_ref, zp_ref, zn_ref, cw_ref, cb_ref, x0_ref, u_ref, ut_ref):
    i = pl.program_id(1)
    n = pl.num_programs(1)
    z = z_ref[...].astype(F32)
    tb = z.shape[0]
    prev_row = jnp.where(i > 0, zp_ref[HALO - 1:HALO, :].astype(F32), 0.0)
    next_row = jnp.where(i < n - 1, zn_ref[0:1, :].astype(F32), 0.0)
    rowid = lax.broadcasted_iota(jnp.int32, z.shape, 0)
    zm = jnp.where(rowid == 0, prev_row, pltpu.roll(z, 1, 0))
    zp = jnp.where(rowid == tb - 1, next_row, pltpu.roll(z, tb - 1, 0))
    zc = cb_ref[...] + cw_ref[0:1, :] * zm
    zc = zc + cw_ref[1:2, :] * z
    zc = zc + cw_ref[2:3, :] * zp
    x0_ref[...] = zc[:, :HY_W].astype(BF16)
    u = zc[:, HY_W:2 * HY_W] * zc[:, 2 * HY_W:]
    u_ref[...] = u.astype(BF16)
    for j in range(tb // FFT_N1):
        ut_ref[j] = u[j * FFT_N1:(j + 1) * FFT_N1, :].T.astype(BF16)


def _hy_pre(zh, cw, cb):
    bsz, lt, _ = zh.shape
    l = lt - TB
    nblk = l // TB
    h8 = TB // HALO
    return pl.pallas_call(
        _hy_pre_kernel,
        grid=(bsz, nblk),
        in_specs=[pl.BlockSpec((None, TB, 3 * HY_W), lambda b, i: (b, i + 1, 0)),
                  pl.BlockSpec((None, HALO, 3 * HY_W), lambda b, i: (b, (i + 1) * h8 - 1, 0)),
                  pl.BlockSpec((None, HALO, 3 * HY_W), lambda b, i: (b, jnp.minimum((i + 2) * h8, lt // HALO - 1), 0)),
                  pl.BlockSpec((HY_SHORT, 3 * HY_W), lambda b, i: (0, 0)),
                  pl.BlockSpec((1, 3 * HY_W), lambda b, i: (0, 0))],
        out_specs=[pl.BlockSpec((None, TB, HY_W), lambda b, i: (b, i, 0)),
                   pl.BlockSpec((None, TB, HY_W), lambda b, i: (b, i, 0)),
                   pl.BlockSpec((None, TB // FFT_N1, HY_W, FFT_N1), lambda b, i: (b, i, 0, 0))],
        out_shape=[jax.ShapeDtypeStruct((bsz, l, HY_W), BF16), jax.ShapeDtypeStruct((bsz, l, HY_W), BF16),
                   jax.ShapeDtypeStruct((bsz, l // FFT_N1, HY_W, FFT_N1), BF16)],
        compiler_params=_cparams(("parallel", "parallel")),
        name="hy_pre",
    )(zh, zh, zh, cw, cb)


def _fft_consts(n1_in):
    n = FFT_N1
    idx = np.arange(n)
    ang1 = 2.0 * np.pi * np.outer(idx, idx) / n
    c, s = np.cos(ang1), np.sin(ang1)
    angt = 2.0 * np.pi * np.outer(idx, idx) / (n * n)
    tw = np.concatenate([np.cos(angt), -np.sin(angt)], axis=1)
    f3 = np.block([[c, -s], [s, c]])
    f3i = np.block([[c, s], [-s, c]])
    ch, sh = c[:, :n1_in], s[:, :n1_in]
    f1_pair = np.block([[ch, sh], [-sh, ch]])
    f1_real = np.concatenate([c, -s], axis=0)
    f1i = np.block([[ch.T, -sh.T], [sh.T, ch.T]]) / (n * n)
    return tw, f3, f3i, f1_pair, f1_real, f1i


def _fft_forward(a, tw_re, tw_im, lhs_scr, ncg):
    for cix in range(ncg):
        cs = slice(cix * FFT_N1, (cix + 1) * FFT_N1)
        are, aim = a[:FFT_N1, cs], a[FFT_N1:, cs]
        lhs_scr[cs, :FFT_N1] = are * tw_re - aim * tw_im
        lhs_scr[cs, FFT_N1:] = are * tw_im + aim * tw_re


def _hy_filt_kernel(k_ref, f1_ref, tw_ref, f3_ref, o_ref, lhs_scr):
    a = _dot_x3(f1_ref[...], k_ref[...])
    _fft_forward(a, tw_ref[:, :FFT_N1], tw_ref[:, FFT_N1:], lhs_scr, FFT_CG)
    o_ref[...] = _dot_x3(lhs_scr[...], f3_ref[...])


def _hy_fft_kernel(u_ref, kf_ref, f1_ref, tw_ref, f3_ref, f3i_ref, f1i_ref, y_ref, lhs_scr, a2_scr):
    cgl = FFT_CG * FFT_N1
    x = u_ref[...].reshape(2 * u_ref.shape[1], cgl)
    a = jnp.dot(f1_ref[...], x, preferred_element_type=F32)
    tw_re, tw_im = tw_ref[:, :FFT_N1], tw_ref[:, FFT_N1:]
    _fft_forward(a, tw_re, tw_im, lhs_scr, FFT_CG)
    y = jnp.dot(lhs_scr[...].astype(BF16), f3_ref[...], preferred_element_type=F32)
    yre, yim = y[:, :FFT_N1], y[:, FFT_N1:]
    kre, kim = kf_ref[:, :FFT_N1], kf_ref[:, FFT_N1:]
    z = jnp.concatenate([yre * kre - yim * kim, yre * kim + yim * kre], axis=1).astype(BF16)
    bp = jnp.dot(z, f3i_ref[...], preferred_element_type=F32)
    for cix in range(FFT_CG):
        cs = slice(cix * FFT_N1, (cix + 1) * FFT_N1)
        bre, bim = bp[cs, :FFT_N1], bp[cs, FFT_N1:]
        a2_scr[:FFT_N1, cs] = bre * tw_re + bim * tw_im
        a2_scr[FFT_N1:, cs] = bim * tw_re - bre * tw_im
    out = jnp.dot(f1i_ref[...], a2_scr[...].astype(BF16), preferred_element_type=F32)
    y_ref[...] = out.reshape(2, u_ref.shape[1], cgl)


def _hy_conv(ut, kfilt):
    bsz, n1h, nch, _ = ut.shape
    assert 2 * n1h == FFT_N1 and bsz % 2 == 0
    cgl = FFT_CG * FFT_N1
    ncol = nch * FFT_N1
    tw, f3, f3i, f1_pair, f1_real, f1i = _fft_consts(n1h)
    kt = kfilt.reshape(FFT_N1, FFT_N1, nch).transpose(0, 2, 1).reshape(FFT_N1, ncol)
    const2 = lambda shp: pl.BlockSpec(shp, lambda *a: (0, 0))
    kf = pl.pallas_call(
        _hy_filt_kernel,
        grid=(nch // FFT_CG,),
        in_specs=[pl.BlockSpec((FFT_N1, cgl), lambda j: (0, j)), const2((2 * FFT_N1, FFT_N1)),
                  const2((FFT_N1, 2 * FFT_N1)), const2((2 * FFT_N1, 2 * FFT_N1))],
        out_specs=pl.BlockSpec((cgl, 2 * FFT_N1), lambda j: (j, 0)),
        out_shape=jax.ShapeDtypeStruct((ncol, 2 * FFT_N1), F32),
        scratch_shapes=[pltpu.VMEM((cgl, 2 * FFT_N1), F32)],
        compiler_params=_cparams(("parallel",)),
        name="hy_filter_dft",
    )(kt, jnp.asarray(f1_real, F32), jnp.asarray(tw, F32), jnp.asarray(f3, F32))
    u2 = ut.reshape(bsz, n1h, ncol)
    y = pl.pallas_call(
        _hy_fft_kernel,
        grid=(bsz // 2, nch // FFT_CG),
        in_specs=[pl.BlockSpec((2, n1h, cgl), lambda p, j: (p, 0, j)),
                  pl.BlockSpec((cgl, 2 * FFT_N1), lambda p, j: (j, 0)),
                  const2((2 * FFT_N1, FFT_N1)), const2((FFT_N1, 2 * FFT_N1)),
                  const2((2 * FFT_N1, 2 * FFT_N1)), const2((2 * FFT_N1, 2 * FFT_N1)), const2((FFT_N1, 2 * FFT_N1))],
        out_specs=pl.BlockSpec((2, n1h, cgl), lambda p, j: (p, 0, j)),
        out_shape=jax.ShapeDtypeStruct((bsz, n1h, ncol), F32),
        scratch_shapes=[pltpu.VMEM((cgl, 2 * FFT_N1), F32), pltpu.VMEM((2 * FFT_N1, cgl), F32)],
        compiler_params=_cparams(("parallel", "parallel")),
        name="hy_fft_conv",
    )(u2, kf, jnp.asarray(f1_pair, F32).astype(BF16), jnp.asarray(tw, F32), jnp.asarray(f3, F32).astype(BF16),
      jnp.asarray(f3i, F32).astype(BF16), jnp.asarray(f1i, F32).astype(BF16))
    return y.reshape(bsz, n1h, nch, FFT_N1)


def _hy_filter_kernel(f_ref, w1_ref, b1_ref, w2_ref, b2_ref, w3_ref, b3_ref, fr_ref, w4_ref, dl_ref, o_ref,
                      *, zero_row):
    f = f_ref[...]
    fr = fr_ref[...]
    h = jnp.sin(fr * (_dot_x3(f, w1_ref[...]) + b1_ref[...]))
    h = jnp.sin(fr * (_dot_x3(h, w2_ref[...]) + b2_ref[...]))
    h = jnp.sin(fr * (_dot_x3(h, w3_ref[...]) + b3_ref[...]))
    out = _dot_x3(h, w4_ref[...]) * jnp.exp(-f[:, 0:1] * dl_ref[...])
    tm = f.shape[0]
    row = pl.program_id(0) * tm + lax.broadcasted_iota(jnp.int32, (tm, 1), 0)
    o_ref[...] = jnp.where(row == zero_row, 0.0, out)


def _hy_filter(seqlen, fw1, fb1, fw2, fb2, fw3, fb3, freq, fw4):
    t = jnp.linspace(0.0, 1.0, seqlen, dtype=F32)[:, None]
    bands = (HY_EMB - 1) // 2
    w = 2.0 * math.pi * jnp.arange(seqlen, dtype=F32)[:, None] / seqlen
    f = jnp.linspace(1e-4, bands - 1, bands, dtype=F32)[None, :]
    feat = jnp.concatenate([t, jnp.cos(f * w), -jnp.sin(f * w)], axis=-1)
    feat2 = jnp.concatenate([feat, feat[:1], feat[:0:-1]], axis=0)
    feat2 = jnp.pad(feat2, ((0, 0), (0, 128 - HY_EMB)))
    w1 = jnp.pad(fw1.astype(F32), ((0, 128 - HY_EMB), (0, 0)))
    deltas = jnp.abs(jnp.linspace(math.log(HY_DECAY_TARGET) / HY_DECAY_LONG_PCT,
                                  math.log(HY_DECAY_TARGET) / HY_DECAY_SHORT_PCT, HY_W, dtype=F32))
    tm = 1024
    nhalf = seqlen // tm
    nh = fw2.shape[0]
    const = lambda shp: pl.BlockSpec(shp, lambda i: (0, 0))
    row = lambda v: v.astype(F32).reshape(1, -1)
    return pl.pallas_call(
        functools.partial(_hy_filter_kernel, zero_row=seqlen),
        grid=(2 * nhalf,),
        in_specs=[pl.BlockSpec((tm, 128), lambda i: (i, 0)),
                  const((128, nh)), const((1, nh)), const((nh, nh)), const((1, nh)), const((nh, nh)), const((1, nh)),
                  const((1, nh)),
                  pl.BlockSpec((nh, HY_W), lambda i: (0, i // nhalf)),
                  const((1, HY_W))],
        out_specs=pl.BlockSpec((tm, HY_W), lambda i: (i, 0)),
        out_shape=jax.ShapeDtypeStruct((2 * seqlen, HY_W), F32),
        compiler_params=_cparams(("parallel",)),
        name="hy_filter_mlp",
    )(feat2, w1, row(fb1), fw2.astype(F32), row(fb2), fw3.astype(F32), row(fb3), row(freq), fw4.astype(F32),
      row(deltas))


def _od_out_kernel(na_ref, yt_ref, x0_ref, u_ref, x_ref, mod_ref, hb_ref, wout_ref, wr_ref, br_ref,
                   x_out, h2_out, lg_out):
    y = jnp.concatenate([yt_ref[j].T for j in range(yt_ref.shape[0])], axis=0)
    hy = x0_ref[...].astype(F32) * (y + u_ref[...].astype(F32) * hb_ref[...])
    cat = jnp.concatenate([na_ref[...], hy], axis=1).astype(BF16)
    mix = jnp.dot(cat, wout_ref[...], preferred_element_type=F32)
    _mixer_tail(x_ref[...], mix, mod_ref, wr_ref, br_ref, x_out, h2_out, lg_out)


def _latent_mod_index(b, i):
    return (b, 1, 0, 0)


def _od_out(na, yt, x0, u, xcat, mods, hb, wout, wr, br):
    bsz, l, _ = na.shape
    nblk = l // TB
    tok = lambda n: pl.BlockSpec((None, TB, n), lambda b, i: (b, i, 0))
    const = lambda shp: pl.BlockSpec(shp, lambda b, i: tuple(0 for _ in shp))
    return pl.pallas_call(
        _od_out_kernel,
        grid=(bsz, nblk),
        in_specs=[tok(NA_W), pl.BlockSpec((None, TB // FFT_N1, HY_W, FFT_N1), lambda b, i: (b, i, 0, 0)),
                  tok(HY_W), tok(HY_W),
                  pl.BlockSpec((None, TB, D_MODEL), lambda b, i: (b, i + 1, 0)),
                  pl.BlockSpec((None, None, 6, D_MODEL), _latent_mod_index),
                  const((1, HY_W)), const((D_MODEL, D_MODEL)), const((D_MODEL, 128)), const((1, 128))],
        out_specs=[tok(D_MODEL), pl.BlockSpec((TB * ROW_TILE, 128), lambda b, i: (b * nblk + i, 0)), tok(128)],
        out_shape=[jax.ShapeDtypeStruct((bsz, l, D_MODEL), F32), jax.ShapeDtypeStruct((bsz * l * ROW_TILE, 128), F32),
                   jax.ShapeDtypeStruct((bsz, l, 128), F32)],
        compiler_params=_cparams(("parallel", "parallel")),
        name="od_out",
    )(na, yt, x0, u, xcat, mods, hb, wout, wr, br)


def _mods(c, c_ctx, ada_w, ada_b):
    bsz = c.shape[0]
    depth, _, n = ada_w.shape
    cc = jnp.concatenate([c, c_ctx[None]], axis=0)
    a = jnp.pad(cc * _sigmoid(cc), ((0, 8 - (bsz + 1) % 8), (0, 0)))
    mp, tn = a.shape[0], 1024
    m = pl.pallas_call(
        _adaln_kernel,
        grid=(depth, n // tn),
        in_specs=[pl.BlockSpec((mp, D_MODEL), lambda l, j: (0, 0)),
                  pl.BlockSpec((None, D_MODEL, tn), lambda l, j: (l, 0, j)),
                  pl.BlockSpec((None, 1, tn), lambda l, j: (l, 0, j))],
        out_specs=pl.BlockSpec((None, mp, tn), lambda l, j: (l, 0, j)),
        out_shape=jax.ShapeDtypeStruct((depth, mp, n), F32),
        compiler_params=_cparams(("parallel", "parallel")),
        name="adaln_dense",
    )(a, ada_w, ada_b.reshape(depth, 1, n))
    mod_l = m[:, :bsz].reshape(depth, bsz, 1, 6, D_MODEL)
    mod_c = jnp.broadcast_to(m[:, bsz].reshape(depth, 1, 1, 6, D_MODEL), (depth, bsz, 1, 6, D_MODEL))
    return jnp.concatenate([mod_c, mod_l], axis=2)


def _rope_tables(seqlen):
    pos = jnp.arange(seqlen)
    half = GLA_DK // 4
    freqs = ROPE_BASE ** (-jnp.arange(half, dtype=F32) / half)
    ar = (pos // GRID_W).astype(F32)[:, None] * freqs
    ac = (pos % GRID_W).astype(F32)[:, None] * freqs
    cos = jnp.concatenate([jnp.cos(ar), jnp.cos(ar), jnp.cos(ac), jnp.cos(ac)], axis=1)
    sin = jnp.concatenate([-jnp.sin(ar), jnp.sin(ar), -jnp.sin(ac), jnp.sin(ac)], axis=1)
    cos = jnp.concatenate([jnp.ones((TB, GLA_DK), F32), cos], axis=0)
    sin = jnp.concatenate([jnp.zeros((TB, GLA_DK), F32), sin], axis=0)
    return jnp.tile(cos, (1, GLA_HEADS)), jnp.tile(sin, (1, GLA_HEADS))


def _router_weights(wg, bg, we, be):
    pad = 128 - N_GROUPS - N_EXPERTS
    wr = jnp.concatenate([wg, we, jnp.zeros((D_MODEL, pad), F32)], axis=1)
    br = jnp.concatenate([bg, be, jnp.zeros((pad,), F32)]).reshape(1, 128)
    return wr, br


def kernel(x, c, ctx, c_ctx, ada_w, ada_b, moe_wg, moe_bg, moe_we, moe_be, moe_w1, moe_w3, moe_w2, ev_w_in, ev_w_out, gla_wa2, gla_ba, gla_norm, s5_lam_re, s5_lam_im, s5_log_dt, s5_b_re, s5_b_im, s5_c_re, s5_c_im, s5_d, s5_w_glu, od_w_in, od_w_out, na_q_norm, na_k_norm, na_rpb, hy_conv_w, hy_conv_b, hy_fw1, hy_fb1, hy_fw2, hy_fb2, hy_fw3, hy_fb3, hy_freq, hy_fw4, hy_bias):
    bsz, seqlen, _ = x.shape
    assert ctx.shape[1] == TB and seqlen % TB == 0

    mods_all = _mods(c, c_ctx, ada_w, ada_b)
    mods = mods_all[0]
    w_in = ev_w_in[0]
    n_a = 2 * GLA_RANK
    a0 = 2 * GLA_QK + 2 * GLA_V
    w_ev = jnp.concatenate([w_in[:, :a0], w_in[:, a0 + n_a:], w_in[:, a0:a0 + n_a],
                            jnp.zeros((D_MODEL, 128 - n_a), F32)], axis=1).astype(BF16)
    wa = jnp.zeros((128, 2 * GLA_QK), F32)
    for d in range(2):
        wa = wa.at[d * GLA_RANK:(d + 1) * GLA_RANK, d * GLA_QK:(d + 1) * GLA_QK].set(gla_wa2[0, d])
    cos, sin = _rope_tables(seqlen)
    q, k, v, g, u, u_sw, la = _ev_proj(ctx, x, mods, w_ev, wa, gla_ba[0].reshape(1, 2 * GLA_QK), cos, sin)
    o_f, o_b = _gla(q, k, v, la)
    s5p = [t[0].astype(F32) for t in (s5_lam_re, s5_lam_im, s5_log_dt, s5_b_re, s5_b_im, s5_c_re, s5_c_im)]
    y_f = _s5_scan(u, _s5_mats(*[t[0] for t in s5p], rev=False), rev=False)
    y_b = _s5_scan(u_sw, _s5_mats(*[t[1] for t in s5p], rev=True), rev=True)
    wr, br = _router_weights(moe_wg[0], moe_bg[0], moe_we[0], moe_be[0])
    x1, h2, lg = _ev_out(o_f, o_b, g, y_f, y_b, u, ctx, x, mods,
                         jnp.tile(gla_norm[0], GLA_HEADS).reshape(1, GLA_V), s5_d[0].reshape(1, S5_W),
                         s5_w_glu[0].astype(BF16), ev_w_out[0].astype(BF16), wr, br)
    xcat = _moe(x1, h2, lg, mods, _mod_index,
                moe_w1, moe_w3, moe_w2, 0)

    mods = mods_all[1]
    hd = np.arange(NA_W) // NA_DH
    gm = jnp.asarray((hd[:, None] == hd[None, :]).astype(np.float32) / NA_DH)
    qh, kh, vh, zh = _od_proj(xcat, mods, od_w_in[0].astype(BF16), gm,
                              jnp.tile(na_q_norm[0], NA_HEADS).reshape(1, NA_W),
                              jnp.tile(na_k_norm[0], NA_HEADS).reshape(1, NA_W))
    na = _na(qh, kh, vh, _na_bias_table(na_rpb[0]))
    x0, uh, ut = _hy_pre(zh, hy_conv_w[0], hy_conv_b[0].reshape(1, 3 * HY_W))
    kfilt = _hy_filter(seqlen, hy_fw1[0], hy_fb1[0], hy_fw2[0], hy_fb2[0], hy_fw3[0], hy_fb3[0],
                       hy_freq[0], hy_fw4[0])
    yt = _hy_conv(ut, kfilt)
    wr, br = _router_weights(moe_wg[1], moe_bg[1], moe_we[1], moe_be[1])
    xl, h2, lg = _od_out(na, yt, x0, uh, xcat, mods, hy_bias[0].reshape(1, HY_W),
                         od_w_out[0].astype(BF16), wr, br)
    return _moe(xl, h2, lg, mods, _latent_mod_index,
                moe_w1, moe_w3, moe_w2, 1)
```

```python
import functools
import math

import numpy as np
import jax
import jax.numpy as jnp
from jax import lax
from jax.experimental import pallas as pl
from jax.experimental.pallas import tpu as pltpu

F32, BF16 = jnp.float32, jnp.bfloat16
HI = lax.Precision.HIGHEST

D_MODEL = 1024
GRID_W = 64
EPS = 1e-6
ROPE_BASE = 10000.0
NEG_INF = -1e30
GLA_HEADS, GLA_DK, GLA_DV = 4, 64, 128
GLA_QK, GLA_V = GLA_HEADS * GLA_DK, GLA_HEADS * GLA_DV
GLA_RANK = 16
GLA_TAU = 16.0
GLA_CHUNK = 64
GLA_LOG_ALPHA_MIN = -1.0
S5_W, S5_H, S5_P = 512, 16, 64
S5_G = S5_W // S5_H
S5_CHUNK = 8
S5_TBLK = 1408
NA_HEADS, NA_DH = 8, 64
NA_W = NA_HEADS * NA_DH
WIN_R, WIN_C = 8, 16
NA_HG = 4
NA_ROWS = 4
HY_W = 512
HY_SHORT = 3
HY_EMB = 33
HY_DECAY_TARGET = 1e-2
HY_DECAY_SHORT_PCT = 0.3
HY_DECAY_LONG_PCT = 1.5
N_GROUPS, EXP_PER_GROUP = 4, 8
N_EXPERTS = N_GROUPS * EXP_PER_GROUP
D_EXPERT = 512
TOP_K = 2
MOE_BLOCK = 256

TB = 256
PROJ_TILE = 768
COMBINE_TILE = 1024
HALO = 16
FFT_N1 = 128
FFT_CG = 32
V7X_VMEM_LIMIT = 52 * 1024 * 1024


def _cparams(sem):
    return pltpu.CompilerParams(dimension_semantics=sem, vmem_limit_bytes=V7X_VMEM_LIMIT)


def _sigmoid(x):
    return 1.0 / (1.0 + jnp.exp(-x))


ROW_TILE = D_MODEL // 128


def _to_token_tiles(ref, val):
    n = val.shape[0]
    for j in range(ROW_TILE):
        ref[pl.ds(j, n, stride=ROW_TILE), :] = val[:, j * 128:(j + 1) * 128]


def _from_token_tiles(ref, n):
    return jnp.concatenate([ref[pl.ds(j, n, stride=ROW_TILE), :] for j in range(ROW_TILE)], axis=1)


def _split_bf16(x):
    hi = x.astype(BF16)
    return hi, (x - hi.astype(F32)).astype(BF16)


def _dot_x3(a, b):
    a_hi, a_lo = _split_bf16(a)
    b_hi, b_lo = _split_bf16(b)
    d = lambda p, q: jnp.dot(p, q, preferred_element_type=F32)
    return d(a_hi, b_hi) + d(a_lo, b_hi) + d(a_hi, b_lo)


def _rms_rows(x):
    return x * lax.rsqrt(jnp.mean(x * x, axis=-1, keepdims=True) + EPS)


def _adaln_kernel(a_ref, w_ref, b_ref, o_ref):
    o_ref[...] = jnp.dot(a_ref[...], w_ref[...], precision=HI, preferred_element_type=F32) + b_ref[...]


def _mod_index(b, i):
    return (b, jnp.minimum(i, 1), 0, 0)


def _swapped_index(nblk, b, i):
    return (b, jnp.where(i == 0, nblk - 1, i - 1), 0)


EV_NQ, EV_NK, EV_NV, EV_NG, EV_NU = 0, 256, 512, 1024, 1536
EV_NA = 2048
EV_NTOT = 2176


def _stream_tile(ctx_ref, x_ref):
    return jnp.where(pl.program_id(1) == 0, ctx_ref[...], x_ref[...])


def _stream_specs():
    return [pl.BlockSpec((None, TB, D_MODEL), lambda b, i: (b, 0, 0)),
            pl.BlockSpec((None, TB, D_MODEL), lambda b, i: (b, jnp.maximum(i - 1, 0), 0))]


def _ev_proj_kernel(ctx_ref, x_ref, mod_ref, w_ref, wa_ref, ba_ref, cos_ref, sin_ref,
                    q_ref, k_ref, v_ref, g_ref, u_ref, usw_ref, la_ref):
    x = _stream_tile(ctx_ref, x_ref)
    h = _rms_rows(x) * (1.0 + mod_ref[1:2, :]) + mod_ref[0:1, :]
    z = jnp.dot(h.astype(BF16), w_ref[...], preferred_element_type=F32)
    lane = lax.broadcasted_iota(jnp.int32, (x.shape[0], GLA_QK), 1)
    first = (lane % 32) < 16
    cos, sin = cos_ref[...], sin_ref[...]

    def rot(t):
        partner = jnp.where(first, pltpu.roll(t, GLA_QK - 16, 1), pltpu.roll(t, 16, 1))
        return t * cos + partner * sin

    q_ref[...] = rot(z[:, EV_NQ:EV_NQ + GLA_QK]) * (GLA_DK ** -0.5)
    k_ref[...] = rot(z[:, EV_NK:EV_NK + GLA_QK])
    v_ref[...] = z[:, EV_NV:EV_NV + GLA_V].astype(BF16)
    g_ref[...] = z[:, EV_NG:EV_NG + GLA_V].astype(BF16)
    u_ref[...] = z[:, EV_NU:EV_NU + S5_W]
    usw_ref[...] = z[:, EV_NU:EV_NU + S5_W]
    a = z[:, EV_NA:EV_NA + 128]
    pre = _dot_x3(a, wa_ref[...]) + ba_ref[...]
    ls = jnp.minimum(pre, 0.0) - jnp.log1p(jnp.exp(-jnp.abs(pre)))
    la_ref[...] = jnp.maximum(ls / GLA_TAU, GLA_LOG_ALPHA_MIN)


def _ev_proj(ctx, x, mods, w, wa, ba, cos, sin):
    bsz = x.shape[0]
    lt = ctx.shape[1] + x.shape[1]
    nblk = lt // TB
    tok = lambda n: pl.BlockSpec((None, TB, n), lambda b, i: (b, i, 0))
    const = lambda shp: pl.BlockSpec(shp, lambda b, i: tuple(0 for _ in shp))
    return pl.pallas_call(
        _ev_proj_kernel,
        grid=(bsz, nblk),
        in_specs=_stream_specs() + [
                  pl.BlockSpec((None, None, 6, D_MODEL), _mod_index),
                  const((D_MODEL, EV_NTOT)), const((128, 2 * GLA_QK)), const((1, 2 * GLA_QK)),
                  pl.BlockSpec((TB, GLA_QK), lambda b, i: (i, 0)),
                  pl.BlockSpec((TB, GLA_QK), lambda b, i: (i, 0))],
        out_specs=[tok(GLA_QK), tok(GLA_QK), tok(GLA_V), tok(GLA_V), tok(S5_W),
                   pl.BlockSpec((None, TB, S5_W), functools.partial(_swapped_index, nblk)), tok(2 * GLA_QK)],
        out_shape=[jax.ShapeDtypeStruct((bsz, lt, GLA_QK), F32),
                   jax.ShapeDtypeStruct((bsz, lt, GLA_QK), F32),
                   jax.ShapeDtypeStruct((bsz, lt, GLA_V), BF16),
                   jax.ShapeDtypeStruct((bsz, lt, GLA_V), BF16),
                   jax.ShapeDtypeStruct((bsz, lt, S5_W), F32),
                   jax.ShapeDtypeStruct((bsz, lt, S5_W), F32),
                   jax.ShapeDtypeStruct((bsz, lt, 2 * GLA_QK), F32)],
        compiler_params=_cparams(("parallel", "parallel")),
        name="ev_proj",
    )(ctx, x, mods, w, wa, ba, cos, sin)


def _gla_kernel(qf_ref, kf_ref, vf_ref, laf_ref, qb_ref, kb_ref, vb_ref, lab_ref,
                of_ref, ob_ref, s_scr):
    i = pl.program_id(1)

    @pl.when(i == 0)
    def _():
        s_scr[...] = jnp.zeros_like(s_scr)

    c = GLA_CHUNK
    nh = GLA_HEADS
    row = lax.broadcasted_iota(jnp.int32, (c, c), 0)
    col = lax.broadcasted_iota(jnp.int32, (c, c), 1)
    row4 = lax.broadcasted_iota(jnp.int32, (nh * c, c), 0) % c
    col4 = lax.broadcasted_iota(jnp.int32, (nh * c, c), 1)
    lane_head = lax.broadcasted_iota(jnp.int32, (c, GLA_QK), 1) // GLA_DK
    out_head = lax.broadcasted_iota(jnp.int32, (c, GLA_V), 1) // GLA_DV
    bd_mask = (lax.broadcasted_iota(jnp.int32, (GLA_V, GLA_QK), 0) // GLA_DV
               == lax.broadcasted_iota(jnp.int32, (GLA_V, GLA_QK), 1) // GLA_DK)
    nchunk = qf_ref.shape[0] // c
    nt = (((1,), (1,)), ((), ()))
    tn = (((0,), (0,)), ((), ()))

    def one_chunk(refs, o_ref, d, r0):
        q_ref, k_ref, v_ref, la_ref = refs
        fwd = d == 0
        sl = pl.ds(r0, c)
        qc, kc, vc, lac = q_ref[sl, :], k_ref[sl, :], v_ref[sl, :], la_ref[sl, :]
        tri = ((row >= col) if fwd else (row <= col)).astype(BF16)
        la_hi, la_lo = _split_bf16(lac)
        b = (jnp.dot(tri, la_hi, preferred_element_type=F32)
             + jnp.dot(tri, la_lo, preferred_element_type=F32))
        b_last = b[c - 1:c, :] if fwd else b[0:1, :]
        qe = (qc * jnp.exp(b)).astype(BF16)
        ke = (kc * jnp.exp(-b)).astype(BF16)
        kd = (kc * jnp.exp(b_last - b)).astype(BF16)
        st = s_scr[d]
        o = lax.dot_general(qe, st.astype(BF16), nt, preferred_element_type=F32)
        q4 = jnp.concatenate([jnp.where(lane_head == h, qe, jnp.zeros_like(qe)) for h in range(nh)], axis=0)
        att = lax.dot_general(q4, ke, nt, preferred_element_type=F32)
        att_mask = (row4 >= col4) if fwd else (row4 < col4)
        o4 = jnp.dot(jnp.where(att_mask, att, 0.0).astype(BF16), vc, preferred_element_type=F32)
        for h in range(nh):
            o = o + jnp.where(out_head == h, o4[h * c:(h + 1) * c, :], 0.0)
        o_ref[sl, :] = o.astype(o_ref.dtype)
        upd_t = lax.dot_general(vc, kd, tn, preferred_element_type=F32)
        s_scr[d] = st * jnp.exp(b_last) + jnp.where(bd_mask, upd_t, 0.0)

    def body(j, carry):
        one_chunk((qf_ref, kf_ref, vf_ref, laf_ref), of_ref, 0, pl.multiple_of(j * c, c))
        one_chunk((qb_ref, kb_ref, vb_ref, lab_ref), ob_ref, 1, pl.multiple_of((nchunk - 1 - j) * c, c))
        return carry

    lax.fori_loop(0, nchunk, body, 0, unroll=True)


def _gla(q, k, v, la):
    bsz, lt, _ = q.shape
    nblk = lt // TB
    fwd_map = lambda b, i: (b, i, 0)
    bwd_blk = lambda i: jnp.where(i == 0, 0, nblk - i)
    bwd_map = lambda b, i: (b, bwd_blk(i), 0)
    bwd_map_la = lambda b, i: (b, bwd_blk(i), 1)
    spec = lambda n, m: pl.BlockSpec((None, TB, n), m)
    return pl.pallas_call(
        _gla_kernel,
        grid=(bsz, nblk),
        in_specs=[spec(GLA_QK, fwd_map), spec(GLA_QK, fwd_map), spec(GLA_V, fwd_map), spec(GLA_QK, fwd_map),
                  spec(GLA_QK, bwd_map), spec(GLA_QK, bwd_map), spec(GLA_V, bwd_map), spec(GLA_QK, bwd_map_la)],
        out_specs=[spec(GLA_V, fwd_map), spec(GLA_V, bwd_map)],
        out_shape=[jax.ShapeDtypeStruct((bsz, lt, GLA_V), BF16), jax.ShapeDtypeStruct((bsz, lt, GLA_V), BF16)],
        scratch_shapes=[pltpu.VMEM((2, GLA_V, GLA_QK), F32)],
        compiler_params=_cparams(("parallel", "arbitrary")),
        name="gla_scan",
    )(q, k, v, la, q, k, v, la)


def _s5_kernel(u_ref, wm_ref, tm_ref, cm_ref, ar_ref, ai_ref, y_ref, w_scr, hp_scr, h_scr, *, rev):
    @pl.when(pl.program_id(1) == 0)
    def _():
        h_scr[...] = jnp.zeros_like(h_scr)

    bsz, ntok, _ = u_ref.shape
    nc = ntok // S5_CHUNK
    half = 8 * S5_P
    x = jnp.concatenate(
        [jnp.concatenate([u_ref[b, pl.ds(s, nc, stride=S5_CHUNK), :] for s in range(S5_CHUNK)], axis=1)
         for b in range(bsz)], axis=0).astype(BF16)
    w_scr[...] = jnp.dot(x, wm_ref[...], preferred_element_type=F32)
    ar, ai = ar_ref[...], ai_ref[...]

    def body(j, hs):
        c = (nc - 1 - j) if rev else j
        out = []
        for b in range(bsz):
            re, im = hs[b]
            r = b * nc + c
            hp_scr[pl.ds(r, 1), :] = jnp.concatenate([re, im], axis=1)
            w = w_scr[pl.ds(r, 1), :]
            out.append((ar * re - ai * im + w[:, :half], ar * im + ai * re + w[:, half:]))
        return tuple(out)

    hs = lax.fori_loop(0, nc, body, tuple((h_scr[b:b + 1, :half], h_scr[b:b + 1, half:]) for b in range(bsz)))
    for b in range(bsz):
        h_scr[b:b + 1, :] = jnp.concatenate(hs[b], axis=1)
    mt = 256
    ntile = x.shape[1] // mt
    cols = []
    for jt in range(ntile):
        acc = None
        for it in (range(jt, ntile) if rev else range(jt + 1)):
            part = jnp.dot(x[:, it * mt:(it + 1) * mt], tm_ref[it * mt:(it + 1) * mt, jt * mt:(jt + 1) * mt],
                           preferred_element_type=F32)
            acc = part if acc is None else acc + part
        cols.append(acc)
    y = (jnp.concatenate(cols, axis=1)
         + jnp.dot(hp_scr[...].astype(BF16), cm_ref[...], preferred_element_type=F32))
    for b in range(bsz):
        for s in range(S5_CHUNK):
            y_ref[b, pl.ds(s, nc, stride=S5_CHUNK), :] = y[b * nc:(b + 1) * nc, s * 128:(s + 1) * 128]


def _s5_mats(lam_re, lam_im, log_dt, b_re, b_im, c_re, c_im, rev):
    t16 = S5_CHUNK
    dt = jnp.exp(log_dt)[:, None]
    mag = jnp.exp(lam_re * dt)
    a_re, a_im = mag * jnp.cos(lam_im * dt), mag * jnp.sin(lam_im * dt)
    den = lam_re * lam_re + lam_im * lam_im
    nr = a_re - 1.0
    co_re = ((nr * lam_re + a_im * lam_im) / den)[..., None]
    co_im = ((a_im * lam_re - nr * lam_im) / den)[..., None]
    bb_re, bb_im = co_re * b_re - co_im * b_im, co_re * b_im + co_im * b_re
    pr, pi = [jnp.ones_like(a_re)], [jnp.zeros_like(a_im)]
    for _ in range(t16):
        pr, pi = pr + [pr[-1] * a_re - pi[-1] * a_im], pi + [pr[-1] * a_im + pi[-1] * a_re]
    pw_re, pw_im = jnp.stack(pr), jnp.stack(pi)
    g = lam_re.shape[0]
    e_re, e_im = pw_re[t16 - 1::-1][:t16], pw_im[t16 - 1::-1][:t16]
    wre = jnp.einsum('sgp,gph->gshp', e_re, bb_re) - jnp.einsum('sgp,gph->gshp', e_im, bb_im)
    wim = jnp.einsum('sgp,gph->gshp', e_re, bb_im) + jnp.einsum('sgp,gph->gshp', e_im, bb_re)
    cb_re = jnp.einsum('gkp,gph->gpkh', c_re, bb_re) - jnp.einsum('gkp,gph->gpkh', c_im, bb_im)
    cb_im = jnp.einsum('gkp,gph->gpkh', c_re, bb_im) + jnp.einsum('gkp,gph->gpkh', c_im, bb_re)
    kd = jnp.einsum('dgp,gpkh->dgkh', pw_re[:t16], cb_re) - jnp.einsum('dgp,gpkh->dgkh', pw_im[:t16], cb_im)
    lag = np.arange(t16)[None, :] - np.arange(t16)[:, None]
    toe = jnp.where((lag >= 0)[:, :, None, None, None], kd[np.clip(lag, 0, t16 - 1)], 0.0)
    toe = toe.transpose(2, 0, 4, 1, 3)
    q_re, q_im = pw_re[1:], pw_im[1:]
    ca_re = jnp.einsum('gkp,tgp->gptk', c_re, q_re) - jnp.einsum('gkp,tgp->gptk', c_im, q_im)
    ca_im = jnp.einsum('gkp,tgp->gptk', c_re, q_im) + jnp.einsum('gkp,tgp->gptk', c_im, q_re)
    if rev:
        wre, wim = wre[:, ::-1], wim[:, ::-1]
        toe = toe[:, ::-1, :, ::-1]
        ca_re, ca_im = ca_re[:, :, ::-1], ca_im[:, :, ::-1]
    nq, gl = g // 8, 8
    nd = t16 * S5_H
    tok_hot = np.zeros((gl, nd, t16 * 128), np.float32)
    st_hot = np.zeros((gl, 2 * S5_P, 2 * gl * S5_P), np.float32)
    for gi in range(gl):
        a = np.arange(nd)
        tok_hot[gi, a, (a // S5_H) * 128 + gi * S5_H + a % S5_H] = 1.0
        a = np.arange(2 * S5_P)
        st_hot[gi, a, (a // S5_P) * gl * S5_P + gi * S5_P + a % S5_P] = 1.0
    place = lambda rows, blk, cols: jnp.einsum('gar,qgab,gbc->qrc', rows, blk, cols)
    wg = jnp.concatenate([wre, wim], axis=-1).reshape(nq, gl, nd, 2 * S5_P)
    tg = toe.reshape(nq, gl, nd, nd)
    cg = jnp.concatenate([ca_re, -ca_im], axis=1).reshape(nq, gl, 2 * S5_P, nd)
    wm = place(tok_hot, wg, st_hot)
    tmat = place(tok_hot, tg, tok_hot)
    cm = place(st_hot, cg, tok_hot)
    return (wm.astype(BF16), tmat.astype(BF16), cm.astype(BF16),
            pw_re[t16].reshape(nq, 1, gl * S5_P), pw_im[t16].reshape(nq, 1, gl * S5_P))


def _s5_scan(u, mats, rev):
    bsz, lt, _ = u.shape
    tblk = max(d for d in range(64, S5_TBLK + 1, 64) if lt % d == 0)
    nblk = lt // tblk
    nq = S5_G // 8
    wm, tmat, cm, ar, ai = mats
    kw, ks = S5_CHUNK * 128, 8 * S5_P
    tmap = (lambda q, t: (0, nblk - 1 - t, q)) if rev else (lambda q, t: (0, t, q))
    per = lambda shp: pl.BlockSpec((None,) + shp, lambda q, t: (q,) + tuple(0 for _ in shp))
    rows = bsz * tblk // S5_CHUNK
    return pl.pallas_call(
        functools.partial(_s5_kernel, rev=rev),
        grid=(nq, nblk),
        in_specs=[pl.BlockSpec((bsz, tblk, 128), tmap),
                  per((kw, 2 * ks)), per((kw, kw)), per((2 * ks, kw)), per((1, ks)), per((1, ks))],
        out_specs=pl.BlockSpec((bsz, tblk, 128), tmap),
        out_shape=jax.ShapeDtypeStruct((bsz, lt, S5_W), F32),
        scratch_shapes=[pltpu.VMEM((rows, 2 * ks), F32), pltpu.VMEM((rows, 2 * ks), F32),
                        pltpu.VMEM((8, 2 * ks), F32)],
        compiler_params=_cparams(("parallel", "arbitrary")),
        name="s5_scan",
    )(u, wm, tmat, cm, ar, ai)


def _mixer_tail(x, mix, mod_ref, wr_ref, br_ref, x_out, h2_out, lg_out):
    xn = x + mod_ref[2:3, :] * mix
    x_out[...] = xn
    h2 = _rms_rows(xn) * (1.0 + mod_ref[4:5, :]) + mod_ref[3:4, :]
    _to_token_tiles(h2_out, h2)
    lg_out[...] = _route_tail(_dot_x3(h2, wr_ref[...]) + br_ref[...])


def _route_tail(lg):
    lane = lax.broadcasted_iota(jnp.int32, lg.shape, 1)
    lane_f = lane.astype(F32)
    big = jnp.float32(128.0)
    row_max = lambda t: jnp.max(t, axis=1, keepdims=True)
    first_lane = lambda hit: jnp.min(jnp.where(hit, lane_f, big), axis=1, keepdims=True)
    is_g = lane < N_GROUPS
    g_logit = jnp.where(is_g, lg, NEG_INF)
    g_max = row_max(g_logit)
    grp = first_lane(g_logit == g_max)
    g_w = 1.0 / jnp.sum(jnp.where(is_g, jnp.exp(lg - g_max), 0.0), axis=1, keepdims=True)
    lo = N_GROUPS + EXP_PER_GROUP * grp
    in_grp = jnp.logical_and(lane_f >= lo, lane_f < lo + EXP_PER_GROUP)
    e_max = row_max(jnp.where(in_grp, lg, NEG_INF))
    pe = jnp.where(in_grp, jnp.exp(lg - e_max), 0.0)
    p = pe / jnp.sum(pe, axis=1, keepdims=True)
    cand = jnp.where(in_grp, p, -1.0)
    p1 = row_max(cand)
    i1 = first_lane(cand == p1)
    cand2 = jnp.where(lane_f == i1, -1.0, cand)
    p2 = row_max(cand2)
    i2 = first_lane(cand2 == p2)
    scale = g_w / (p1 + p2)
    out = jnp.where(lane == 0, scale * p1, jnp.where(lane == 1, scale * p2,
                    jnp.where(lane == 2, i1 - N_GROUPS, jnp.where(lane == 3, i2 - N_GROUPS, 0.0))))
    return out


def _ev_out_kernel(of_ref, ob_ref, g_ref, yf_ref, yb_ref, u_ref, ctx_ref, x_ref, mod_ref,
                   gn_ref, ds_ref, wglu_ref, wout_ref, wr_ref, br_ref,
                   x_out, h2_out, lg_out):
    o = of_ref[...].astype(F32) + ob_ref[...].astype(F32)
    og = jnp.concatenate([_rms_rows(o[:, h * GLA_DV:(h + 1) * GLA_DV]) for h in range(GLA_HEADS)], axis=1)
    g = g_ref[...].astype(F32)
    og = og * gn_ref[...] * (g * _sigmoid(g))
    t = yf_ref[...] + yb_ref[...] + ds_ref[...] * u_ref[...]
    y = t * (0.5 * (1.0 + jnp.tanh(math.sqrt(2.0 / math.pi) * (t + 0.044715 * (t * t * t)))))
    y = y * _sigmoid(jnp.dot(y.astype(BF16), wglu_ref[...], preferred_element_type=F32))
    cat = jnp.concatenate([og, y], axis=1).astype(BF16)
    mix = jnp.dot(cat, wout_ref[...], preferred_element_type=F32)
    _mixer_tail(_stream_tile(ctx_ref, x_ref), mix, mod_ref, wr_ref, br_ref, x_out, h2_out, lg_out)


def _ev_out(o_f, o_b, g, y_f, y_b, u, ctx, x, mods, gn, ds, wglu, wout, wr, br):
    bsz, lt, _ = u.shape
    nblk = lt // TB
    tok = lambda n: pl.BlockSpec((None, TB, n), lambda b, i: (b, i, 0))
    const = lambda shp: pl.BlockSpec(shp, lambda b, i: tuple(0 for _ in shp))
    return pl.pallas_call(
        _ev_out_kernel,
        grid=(bsz, nblk),
        in_specs=[tok(512), tok(512), tok(512), tok(512),
                  pl.BlockSpec((None, TB, S5_W), functools.partial(_swapped_index, nblk)),
                  tok(512)] + _stream_specs() + [
                  pl.BlockSpec((None, None, 6, D_MODEL), _mod_index),
                  const((1, 512)), const((1, 512)), const((512, 512)), const((D_MODEL, D_MODEL)),
                  const((D_MODEL, 128)), const((1, 128))],
        out_specs=[tok(D_MODEL), pl.BlockSpec((TB * ROW_TILE, 128), lambda b, i: (b * nblk + i, 0)), tok(128)],
        out_shape=[jax.ShapeDtypeStruct((bsz, lt, D_MODEL), F32),
                   jax.ShapeDtypeStruct((bsz * lt * ROW_TILE, 128), F32),
                   jax.ShapeDtypeStruct((bsz, lt, 128), F32)],
        compiler_params=_cparams(("parallel", "parallel")),
        name="ev_out",
    )(o_f, o_b, g, y_f, y_b, u, ctx, x, mods, gn, ds, wglu, wout, wr, br)


def _route(route):
    n_tok = route.shape[0]
    gate = route[:, :TOP_K]
    eid = route[:, TOP_K:2 * TOP_K].astype(jnp.int32).reshape(-1)
    n_asg = n_tok * TOP_K
    order = jnp.argsort(eid).astype(jnp.int32)
    counts = jnp.sum((eid[:, None] == jnp.arange(N_EXPERTS)[None, :]).astype(jnp.int32), axis=0)
    padded = (counts + MOE_BLOCK - 1) // MOE_BLOCK * MOE_BLOCK
    pad_end = jnp.cumsum(padded)
    pad_start = pad_end - padded
    cnt_start = jnp.cumsum(counts) - counts
    n_blocks = -(-n_asg // MOE_BLOCK) + N_EXPERTS
    blk_start = jnp.arange(n_blocks, dtype=jnp.int32) * MOE_BLOCK
    blk_e = jnp.minimum(jnp.sum((pad_end[None, :] <= blk_start[:, None]).astype(jnp.int32), axis=1), N_EXPERTS - 1)
    pos = jnp.arange(n_blocks * MOE_BLOCK, dtype=jnp.int32)
    pos_e = jnp.repeat(blk_e, MOE_BLOCK)
    rank = pos - pad_start[pos_e]
    src = jnp.clip(cnt_start[pos_e] + rank, 0, n_asg - 1)
    slot_buf = jnp.where(rank < counts[pos_e], order[src], n_asg).astype(jnp.int32)
    n_valid = jnp.sum((slot_buf < n_asg).reshape(n_blocks, MOE_BLOCK), axis=1).astype(jnp.int32)
    tok = lax.shift_right_logical(slot_buf, 1)
    src_rows = jnp.minimum(tok, n_tok - 1) * ROW_TILE
    spare = n_asg + (pos // MOE_BLOCK % 2) * MOE_BLOCK + pos % MOE_BLOCK
    dst_rows = jnp.where(slot_buf < n_asg, (slot_buf & 1) * n_tok + tok, spare)
    lead = n_asg + jnp.arange(2 * MOE_BLOCK, dtype=jnp.int32)
    dst_rows = jnp.concatenate([lead, dst_rows]) * ROW_TILE
    return src_rows, dst_rows, blk_e, n_valid, gate.astype(F32)


def _moe_kernel(src_ref, dst_ref, blke_ref, nvalid_ref, h_hbm, w1_ref, w3_ref, w2_ref, z_hbm,
                xbuf0, xbuf1, ybuf0, ybuf1, wb1, wb3, wb2, gsem, ssem):
    i = pl.program_id(0)
    nblk = pl.num_programs(0)
    ns = ROW_TILE
    xb, yb = (xbuf0, xbuf1), (ybuf0, ybuf1)
    lead = 2 * MOE_BLOCK

    def issue_gather(blk, buf):
        base = blk * MOE_BLOCK
        for r in range(MOE_BLOCK):
            src = pl.multiple_of(src_ref[base + r], ns)
            pltpu.make_async_copy(h_hbm.at[pl.ds(src, ns)], xb[buf].at[pl.ds(r * ns, ns)],
                                  gsem.at[buf]).start(priority=r % 2)

    def issue_scatter(blk, buf):
        base = lead + blk * MOE_BLOCK
        for r in range(MOE_BLOCK):
            dst = pl.multiple_of(dst_ref[base + r], ns)
            pltpu.make_async_copy(yb[buf].at[pl.ds(r * ns, ns)], z_hbm.at[pl.ds(dst, ns)],
                                  ssem.at[buf]).start(priority=r % 2)

    def wait_gather(buf):
        pltpu.make_async_copy(h_hbm.at[pl.ds(0, MOE_BLOCK * ns)], xb[buf], gsem.at[buf]).wait()

    def wait_scatter(buf):
        pltpu.make_async_copy(yb[buf], z_hbm.at[pl.ds(0, MOE_BLOCK * ns)], ssem.at[buf]).wait()

    used = nvalid_ref[i] > 0

    @pl.when(i == 0)
    def _():
        ybuf0[...] = jnp.zeros_like(ybuf0)
        ybuf1[...] = jnp.zeros_like(ybuf1)
        issue_scatter(-2, 0)
        issue_gather(0, 0)

    def step(cur):
        oth = 1 - cur
        wait_gather(cur)
        issue_gather(jnp.minimum(i + 1, nblk - 1), oth)
        issue_scatter(i - 1, oth)
        x = _from_token_tiles(xb[cur], MOE_BLOCK).astype(BF16)

        @pl.when(jnp.logical_or(i == 0, blke_ref[i] != blke_ref[jnp.maximum(i - 1, 0)]))
        def _():
            wb1[...] = w1_ref[...].astype(BF16)
            wb3[...] = w3_ref[...].astype(BF16)
            wb2[...] = w2_ref[...].astype(BF16)

        h1 = jnp.dot(x, wb1[...], preferred_element_type=F32)
        h3 = jnp.dot(x, wb3[...], preferred_element_type=F32)
        a = (h1 * _sigmoid(h1) * h3).astype(BF16)
        y = jnp.dot(a, wb2[...], preferred_element_type=F32)
        wait_scatter(cur)
        _to_token_tiles(yb[cur], y)

    def drain(last_par):
        wait_gather(1 - last_par)
        issue_scatter(jnp.where(used, i, i - 1), last_par)
        wait_scatter(1 - last_par)
        wait_scatter(last_par)

    for par in range(2):
        @pl.when(jnp.logical_and(used, i % 2 == par))
        def _():
            step(par)

    first_unused = jnp.logical_and(jnp.logical_not(used),
                                   jnp.logical_and(i > 0, nvalid_ref[jnp.maximum(i - 1, 0)] > 0))
    last_used = jnp.logical_and(used, i == nblk - 1)
    for par in range(2):
        @pl.when(jnp.logical_or(jnp.logical_and(first_unused, (i - 1) % 2 == par),
                                jnp.logical_and(last_used, i % 2 == par)))
        def _():
            drain(par)


def _moe_experts(h2, src_rows, dst_rows, blk_e, n_valid, w1, w3, w2, layer):
    n_tok = h2.shape[0] // ROW_TILE
    n_blocks = blk_e.shape[0]
    wspec = lambda shp: pl.BlockSpec((None, None) + shp, lambda i, src, dst, blke, nvalid: (layer, blke[i], 0, 0))
    grid_spec = pltpu.PrefetchScalarGridSpec(
        num_scalar_prefetch=4,
        grid=(n_blocks,),
        in_specs=[pl.BlockSpec(memory_space=pl.ANY),
                  wspec((D_MODEL, D_EXPERT)), wspec((D_MODEL, D_EXPERT)), wspec((D_EXPERT, D_MODEL))],
        out_specs=pl.BlockSpec(memory_space=pl.ANY),
        scratch_shapes=[pltpu.VMEM((MOE_BLOCK * ROW_TILE, 128), F32), pltpu.VMEM((MOE_BLOCK * ROW_TILE, 128), F32),
                        pltpu.VMEM((MOE_BLOCK * ROW_TILE, 128), F32), pltpu.VMEM((MOE_BLOCK * ROW_TILE, 128), F32),
                        pltpu.VMEM((D_MODEL, D_EXPERT), BF16), pltpu.VMEM((D_MODEL, D_EXPERT), BF16),
                        pltpu.VMEM((D_EXPERT, D_MODEL), BF16),
                        pltpu.SemaphoreType.DMA((2,)), pltpu.SemaphoreType.DMA((2,))])
    return pl.pallas_call(
        _moe_kernel,
        grid_spec=grid_spec,
        out_shape=jax.ShapeDtypeStruct(((TOP_K * n_tok + 2 * MOE_BLOCK) * ROW_TILE, 128), F32),
        compiler_params=_cparams(("arbitrary",)),
        name="moe_experts",
    )(src_rows, dst_rows, blk_e, n_valid, h2, w1, w3, w2)


def _moe_combine_kernel(x_ref, z0_ref, z1_ref, gate_ref, mod_ref, o_ref, *, n_ctx):
    gate = gate_ref[...]
    tm = x_ref.shape[0]
    y = gate[:, 0:1] * _from_token_tiles(z0_ref, tm) + gate[:, 1:2] * _from_token_tiles(z1_ref, tm)
    g_out = mod_ref[1, 5:6, :]
    if n_ctx:
        is_ctx = pl.program_id(1) * tm + lax.broadcasted_iota(jnp.int32, (tm, 1), 0) < n_ctx
        g_out = jnp.where(is_ctx, mod_ref[0, 5:6, :], g_out)
    o_ref[...] = x_ref[...] + g_out * y


def _moe_combine(x, z, gate, mods, n_ctx):
    bsz, lt, _ = x.shape
    tm = max(d for d in range(TB, COMBINE_TILE + 1, TB) if lt % d == 0)
    nblk = lt // tm
    gate3 = gate.reshape(bsz, lt, TOP_K)
    return pl.pallas_call(
        functools.partial(_moe_combine_kernel, n_ctx=n_ctx),
        grid=(bsz, nblk),
        in_specs=[pl.BlockSpec((None, tm, D_MODEL), lambda b, i: (b, i, 0)),
                  pl.BlockSpec((tm * ROW_TILE, 128), lambda b, i: (b * nblk + i, 0)),
                  pl.BlockSpec((tm * ROW_TILE, 128), lambda b, i: ((bsz + b) * nblk + i, 0)),
                  pl.BlockSpec((None, tm, TOP_K), lambda b, i: (b, i, 0)),
                  pl.BlockSpec((None, 2, 6, D_MODEL), lambda b, i: (b, 0, 0, 0))],
        out_specs=pl.BlockSpec((None, tm, D_MODEL), lambda b, i: (b, i, 0)),
        out_shape=jax.ShapeDtypeStruct((bsz, lt, D_MODEL), F32),
        compiler_params=_cparams(("parallel", "parallel")),
        name="moe_combine",
    )(x, z, z, gate3, mods)


def _moe(x, h2, logits, mods, n_ctx, w1, w3, w2, layer):
    bsz, lt, _ = x.shape
    src_rows, dst_rows, blk_e, n_valid, gate = _route(logits.reshape(bsz * lt, 128))
    z = _moe_experts(h2, src_rows, dst_rows, blk_e, n_valid, w1, w3, w2, layer)
    return _moe_combine(x, z, gate, mods, n_ctx)


def _od_proj_kernel(x_ref, mod_ref, w_ref, gm_ref, qn_ref, kn_ref, q_ref, k_ref, v_ref, zh_ref):
    tm = x_ref.shape[0]
    is_ctx = pl.program_id(1) * tm + lax.broadcasted_iota(jnp.int32, (tm, 1), 0) < TB
    shift = jnp.where(is_ctx, mod_ref[0, 0:1, :], mod_ref[1, 0:1, :])
    scale = jnp.where(is_ctx, mod_ref[0, 1:2, :], mod_ref[1, 1:2, :])
    h = _rms_rows(x_ref[...]) * (1.0 + scale) + shift
    z = jnp.dot(h.astype(BF16), w_ref[...], preferred_element_type=F32)

    def head_norm(t, gain):
        sq_hi, sq_lo = _split_bf16(t * t)
        gm = gm_ref[...].astype(BF16)
        ms = jnp.dot(sq_hi, gm, preferred_element_type=F32) + jnp.dot(sq_lo, gm, preferred_element_type=F32)
        return t * lax.rsqrt(ms + EPS) * gain

    q_ref[...] = (head_norm(z[:, :NA_W], qn_ref[...]) * (NA_DH ** -0.5)).astype(BF16)
    k_ref[...] = head_norm(z[:, NA_W:2 * NA_W], kn_ref[...]).astype(BF16)
    v_ref[...] = z[:, 2 * NA_W:3 * NA_W].astype(BF16)
    zh_ref[...] = z[:, 3 * NA_W:].astype(BF16)


def _od_proj(xcat, mods, w, gm, qn, kn):
    bsz, lt, _ = xcat.shape
    tm = max(d for d in range(TB, PROJ_TILE + 1, TB) if lt % d == 0)
    nblk = lt // tm
    tok = lambda n: pl.BlockSpec((None, tm, n), lambda b, i: (b, i, 0))
    const = lambda shp: pl.BlockSpec(shp, lambda b, i: tuple(0 for _ in shp))
    return pl.pallas_call(
        _od_proj_kernel,
        grid=(bsz, nblk),
        in_specs=[tok(D_MODEL), pl.BlockSpec((None, 2, 6, D_MODEL), lambda b, i: (b, 0, 0, 0)),
                  const((D_MODEL, 3 * NA_W + 3 * HY_W)), const((NA_W, NA_W)), const((1, NA_W)), const((1, NA_W))],
        out_specs=[tok(NA_W), tok(NA_W), tok(NA_W), tok(3 * HY_W)],
        out_shape=[jax.ShapeDtypeStruct((bsz, lt, NA_W), BF16), jax.ShapeDtypeStruct((bsz, lt, NA_W), BF16),
                   jax.ShapeDtypeStruct((bsz, lt, NA_W), BF16), jax.ShapeDtypeStruct((bsz, lt, 3 * HY_W), BF16)],
        compiler_params=_cparams(("parallel", "parallel")),
        name="od_proj",
    )(xcat, mods, w, gm, qn, kn)


def _na_kernel(q_ref, k_ref, v_ref, t2_ref, o_ref):
    for j in range(NA_ROWS):
        _na_one_row(q_ref, k_ref, v_ref, t2_ref, o_ref, j)


def _na_one_row(q_ref, k_ref, v_ref, t2_ref, o_ref, j):
    r = pl.program_id(1) * NA_ROWS + j
    n_rows = pl.num_programs(1) * NA_ROWS
    r0 = jnp.clip(r - WIN_R // 2, 0, n_rows - WIN_R)
    off = r0 - r + WIN_R - 1
    base = pl.multiple_of(TB + r0 * GRID_W, GRID_W)
    nloc = WIN_R * GRID_W
    q = q_ref[j * GRID_W:(j + 1) * GRID_W, :]
    hg = NA_HG
    gw = hg * NA_DH
    lane_head = lax.broadcasted_iota(jnp.int32, (GRID_W, gw), 1) // NA_DH
    nt = (((1,), (1,)), ((), ()))
    outs = []
    for grp in range(NA_HEADS // hg):
        cs = slice(gw * grp, gw * (grp + 1))
        q2 = q[:, cs]
        q4 = jnp.concatenate([jnp.where(lane_head == h, q2, jnp.zeros_like(q2)) for h in range(hg)], axis=0)
        kw, vw = k_ref[pl.ds(base, nloc), cs], v_ref[pl.ds(base, nloc), cs]
        kc, vc = k_ref[0:TB, cs], v_ref[0:TB, cs]
        bias = jnp.concatenate(
            [jnp.concatenate([t2_ref[hg * grp + h, off + 2 * m] for m in range(WIN_R // 2)], axis=1)
             for h in range(hg)], axis=0)
        s_loc = lax.dot_general(q4, kw, nt, preferred_element_type=F32) + bias
        s_ctx = lax.dot_general(q4, kc, nt, preferred_element_type=F32)
        m = jnp.maximum(jnp.max(s_loc, axis=1, keepdims=True), jnp.max(s_ctx, axis=1, keepdims=True))
        p_loc, p_ctx = jnp.exp(s_loc - m), jnp.exp(s_ctx - m)
        den = jnp.sum(p_loc, axis=1, keepdims=True) + jnp.sum(p_ctx, axis=1, keepdims=True)
        o4 = (jnp.dot(p_loc.astype(BF16), vw, preferred_element_type=F32)
              + jnp.dot(p_ctx.astype(BF16), vc, preferred_element_type=F32)) / den
        acc = jnp.zeros((GRID_W, gw), F32)
        for h in range(hg):
            acc = jnp.where(lane_head == h, o4[h * GRID_W:(h + 1) * GRID_W, :], acc)
        outs.append(acc)
    o_ref[j * GRID_W:(j + 1) * GRID_W, :] = jnp.concatenate(outs, axis=1)


def _na_bias_table(rpb):
    qc = np.arange(GRID_W)[:, None]
    kc = np.arange(GRID_W)[None, :]
    q_start = np.clip(qc - WIN_C // 2, 0, GRID_W - WIN_C)
    valid = (kc >= q_start) & (kc < q_start + WIN_C)
    col_idx = np.clip(kc - qc + WIN_C - 1, 0, 2 * WIN_C - 2)
    hot = np.zeros((GRID_W, GRID_W, 2 * WIN_C - 1), np.float32)
    hot[qc, kc, col_idx] = 1.0
    t = jnp.einsum('hrj,qkj->hrqk', rpb.astype(F32), hot, precision=HI)
    t = jnp.where(valid[None, None], t, NEG_INF)
    return jnp.concatenate([t[:, :-1], t[:, 1:]], axis=-1)


def _na(q, k, v, t2):
    bsz, lt, _ = q.shape
    qrows = NA_ROWS * GRID_W
    n_rows = (lt - TB) // qrows
    qoff = TB // qrows
    return pl.pallas_call(
        _na_kernel,
        grid=(bsz, n_rows),
        in_specs=[pl.BlockSpec((None, qrows, NA_W), lambda b, r: (b, r + qoff, 0)),
                  pl.BlockSpec((None, lt, NA_W), lambda b, r: (b, 0, 0)),
                  pl.BlockSpec((None, lt, NA_W), lambda b, r: (b, 0, 0)),
                  pl.BlockSpec(t2.shape, lambda b, r: (0, 0, 0, 0))],
        out_specs=pl.BlockSpec((None, qrows, NA_W), lambda b, r: (b, r, 0)),
        out_shape=jax.ShapeDtypeStruct((bsz, lt - TB, NA_W), F32),
        compiler_params=_cparams(("parallel", "arbitrary")),
        name="na_attn",
    )(q, k, v, t2)


def _hy_pre_kernel(z_ref, zp_ref, zn_ref, cw_ref, cb_ref, x0_ref, u_ref, ut_ref):
    i = pl.program_id(1)
    n = pl.num_programs(1)
    z = z_ref[...].astype(F32)
    tb = z.shape[0]
    prev_row = jnp.where(i > 0, zp_ref[HALO - 1:HALO, :].astype(F32), 0.0)
    next_row = jnp.where(i < n - 1, zn_ref[0:1, :].astype(F32), 0.0)
    rowid = lax.broadcasted_iota(jnp.int32, z.shape, 0)
    zm = jnp.where(rowid == 0, prev_row, pltpu.roll(z, 1, 0))
    zp = jnp.where(rowid == tb - 1, next_row, pltpu.roll(z, tb - 1, 0))
    zc = cb_ref[...] + cw_ref[0:1, :] * zm
    zc = zc + cw_ref[1:2, :] * z
    zc = zc + cw_ref[2:3, :] * zp
    x0_ref[...] = zc[:, :HY_W].astype(BF16)
    u = zc[:, HY_W:2 * HY_W] * zc[:, 2 * HY_W:]
    u_ref[...] = u.astype(BF16)
    for j in range(tb // FFT_N1):
        ut_ref[j] = u[j * FFT_N1:(j + 1) * FFT_N1, :].T.astype(BF16)


def _hy_pre(zh, cw, cb):
    bsz, lt, _ = zh.shape
    l = lt - TB
    nblk = l // TB
    h8 = TB // HALO
    return pl.pallas_call(
        _hy_pre_kernel,
        grid=(bsz, nblk),
        in_specs=[pl.BlockSpec((None, TB, 3 * HY_W), lambda b, i: (b, i + 1, 0)),
                  pl.BlockSpec((None, HALO, 3 * HY_W), lambda b, i: (b, (i + 1) * h8 - 1, 0)),
                  pl.BlockSpec((None, HALO, 3 * HY_W), lambda b, i: (b, jnp.minimum((i + 2) * h8, lt // HALO - 1), 0)),
                  pl.BlockSpec((HY_SHORT, 3 * HY_W), lambda b, i: (0, 0)),
                  pl.BlockSpec((1, 3 * HY_W), lambda b, i: (0, 0))],
        out_specs=[pl.BlockSpec((None, TB, HY_W), lambda b, i: (b, i, 0)),
                   pl.BlockSpec((None, TB, HY_W), lambda b, i: (b, i, 0)),
                   pl.BlockSpec((None, TB // FFT_N1, HY_W, FFT_N1), lambda b, i: (b, i, 0, 0))],
        out_shape=[jax.ShapeDtypeStruct((bsz, l, HY_W), BF16), jax.ShapeDtypeStruct((bsz, l, HY_W), BF16),
                   jax.ShapeDtypeStruct((bsz, l // FFT_N1, HY_W, FFT_N1), BF16)],
        compiler_params=_cparams(("parallel", "parallel")),
        name="hy_pre",
    )(zh, zh, zh, cw, cb)


def _fft_consts(n1_in):
    n = FFT_N1
    idx = np.arange(n)
    ang1 = 2.0 * np.pi * np.outer(idx, idx) / n
    c, s = np.cos(ang1), np.sin(ang1)
    angt = 2.0 * np.pi * np.outer(idx, idx) / (n * n)
    tw = np.concatenate([np.cos(angt), -np.sin(angt)], axis=1)
    f3 = np.block([[c, -s], [s, c]])
    f3i = np.block([[c, s], [-s, c]])
    ch, sh = c[:, :n1_in], s[:, :n1_in]
    f1_pair = np.block([[ch, sh], [-sh, ch]])
    f1_real = np.concatenate([c, -s], axis=0)
    f1i = np.block([[ch.T, -sh.T], [sh.T, ch.T]]) / (n * n)
    return tw, f3, f3i, f1_pair, f1_real, f1i


def _fft_forward(a, tw_re, tw_im, lhs_scr, ncg):
    for cix in range(ncg):
        cs = slice(cix * FFT_N1, (cix + 1) * FFT_N1)
        are, aim = a[:FFT_N1, cs], a[FFT_N1:, cs]
        lhs_scr[cs, :FFT_N1] = are * tw_re - aim * tw_im
        lhs_scr[cs, FFT_N1:] = are * tw_im + aim * tw_re


def _hy_filt_kernel(k_ref, f1_ref, tw_ref, f3_ref, o_ref, lhs_scr):
    a = _dot_x3(f1_ref[...], k_ref[...])
    _fft_forward(a, tw_ref[:, :FFT_N1], tw_ref[:, FFT_N1:], lhs_scr, FFT_CG)
    o_ref[...] = _dot_x3(lhs_scr[...], f3_ref[...])


def _hy_fft_kernel(u_ref, kf_ref, f1_ref, tw_ref, f3_ref, f3i_ref, f1i_ref, y_ref, lhs_scr, a2_scr):
    cgl = FFT_CG * FFT_N1
    x = u_ref[...].reshape(2 * u_ref.shape[1], cgl)
    a = jnp.dot(f1_ref[...], x, preferred_element_type=F32)
    tw_re, tw_im = tw_ref[:, :FFT_N1], tw_ref[:, FFT_N1:]
    _fft_forward(a, tw_re, tw_im, lhs_scr, FFT_CG)
    y = jnp.dot(lhs_scr[...].astype(BF16), f3_ref[...], preferred_element_type=F32)
    yre, yim = y[:, :FFT_N1], y[:, FFT_N1:]
    kre, kim = kf_ref[:, :FFT_N1], kf_ref[:, FFT_N1:]
    z = jnp.concatenate([yre * kre - yim * kim, yre * kim + yim * kre], axis=1).astype(BF16)
    bp = jnp.dot(z, f3i_ref[...], preferred_element_type=F32)
    for cix in range(FFT_CG):
        cs = slice(cix * FFT_N1, (cix + 1) * FFT_N1)
        bre, bim = bp[cs, :FFT_N1], bp[cs, FFT_N1:]
        a2_scr[:FFT_N1, cs] = bre * tw_re + bim * tw_im
        a2_scr[FFT_N1:, cs] = bim * tw_re - bre * tw_im
    out = jnp.dot(f1i_ref[...], a2_scr[...].astype(BF16), preferred_element_type=F32)
    y_ref[...] = out.reshape(2, u_ref.shape[1], cgl)


def _hy_conv(ut, kfilt):
    bsz, n1h, nch, _ = ut.shape
    assert 2 * n1h == FFT_N1 and bsz % 2 == 0
    cgl = FFT_CG * FFT_N1
    ncol = nch * FFT_N1
    tw, f3, f3i, f1_pair, f1_real, f1i = _fft_consts(n1h)
    kt = kfilt.reshape(FFT_N1, FFT_N1, nch).transpose(0, 2, 1).reshape(FFT_N1, ncol)
    const2 = lambda shp: pl.BlockSpec(shp, lambda *a: (0, 0))
    kf = pl.pallas_call(
        _hy_filt_kernel,
        grid=(nch // FFT_CG,),
        in_specs=[pl.BlockSpec((FFT_N1, cgl), lambda j: (0, j)), const2((2 * FFT_N1, FFT_N1)),
                  const2((FFT_N1, 2 * FFT_N1)), const2((2 * FFT_N1, 2 * FFT_N1))],
        out_specs=pl.BlockSpec((cgl, 2 * FFT_N1), lambda j: (j, 0)),
        out_shape=jax.ShapeDtypeStruct((ncol, 2 * FFT_N1), F32),
        scratch_shapes=[pltpu.VMEM((cgl, 2 * FFT_N1), F32)],
        compiler_params=_cparams(("parallel",)),
        name="hy_filter_dft",
    )(kt, jnp.asarray(f1_real, F32), jnp.asarray(tw, F32), jnp.asarray(f3, F32))
    u2 = ut.reshape(bsz, n1h, ncol)
    y = pl.pallas_call(
        _hy_fft_kernel,
        grid=(bsz // 2, nch // FFT_CG),
        in_specs=[pl.BlockSpec((2, n1h, cgl), lambda p, j: (p, 0, j)),
                  pl.BlockSpec((cgl, 2 * FFT_N1), lambda p, j: (j, 0)),
                  const2((2 * FFT_N1, FFT_N1)), const2((FFT_N1, 2 * FFT_N1)),
                  const2((2 * FFT_N1, 2 * FFT_N1)), const2((2 * FFT_N1, 2 * FFT_N1)), const2((FFT_N1, 2 * FFT_N1))],
        out_specs=pl.BlockSpec((2, n1h, cgl), lambda p, j: (p, 0, j)),
        out_shape=jax.ShapeDtypeStruct((bsz, n1h, ncol), F32),
        scratch_shapes=[pltpu.VMEM((cgl, 2 * FFT_N1), F32), pltpu.VMEM((2 * FFT_N1, cgl), F32)],
        compiler_params=_cparams(("parallel", "parallel")),
        name="hy_fft_conv",
    )(u2, kf, jnp.asarray(f1_pair, F32).astype(BF16), jnp.asarray(tw, F32), jnp.asarray(f3, F32).astype(BF16),
      jnp.asarray(f3i, F32).astype(BF16), jnp.asarray(f1i, F32).astype(BF16))
    return y.reshape(bsz, n1h, nch, FFT_N1)


def _hy_filter_kernel(f_ref, w1_ref, b1_ref, w2_ref, b2_ref, w3_ref, b3_ref, fr_ref, w4_ref, dl_ref, o_ref):
    f = f_ref[...]
    fr = fr_ref[...]
    h = jnp.sin(fr * (_dot_x3(f, w1_ref[...]) + b1_ref[...]))
    h = jnp.sin(fr * (_dot_x3(h, w2_ref[...]) + b2_ref[...]))
    h = jnp.sin(fr * (_dot_x3(h, w3_ref[...]) + b3_ref[...]))
    out = _dot_x3(h, w4_ref[...])
    tm = f.shape[0]
    row = pl.program_id(0) * tm + lax.broadcasted_iota(jnp.int32, (tm, 1), 0)
    o_ref[0] = out[:, :HY_W] * jnp.exp(-f[:, 0:1] * dl_ref[...])
    o_ref[1] = jnp.where(row == 0, 0.0, out[:, HY_W:] * jnp.exp(-f[:, 128:129] * dl_ref[...]))


def _hy_filter(seqlen, fw1, fb1, fw2, fb2, fw3, fb3, freq, fw4):
    t = jnp.linspace(0.0, 1.0, seqlen, dtype=F32)[:, None]
    bands = (HY_EMB - 1) // 2
    w = 2.0 * math.pi * jnp.arange(seqlen, dtype=F32)[:, None] / seqlen
    f = jnp.linspace(1e-4, bands - 1, bands, dtype=F32)[None, :]
    feat = jnp.concatenate([t, jnp.cos(f * w), -jnp.sin(f * w)], axis=-1)
    lane_pad = lambda a: jnp.pad(a, ((0, 0), (0, 128 - HY_EMB)))
    feat_b = jnp.concatenate([feat[:1], feat[:0:-1]], axis=0)
    feat2 = jnp.concatenate([lane_pad(feat), lane_pad(feat_b)], axis=1)
    nh = fw2.shape[0]
    two = lambda m: jnp.kron(jnp.eye(2, dtype=F32), m.astype(F32))
    w1 = two(jnp.pad(fw1.astype(F32), ((0, 128 - HY_EMB), (0, 0))))
    w4 = jnp.concatenate([jnp.pad(fw4[:, :HY_W].astype(F32), ((0, nh), (0, 0))),
                          jnp.pad(fw4[:, HY_W:].astype(F32), ((nh, 0), (0, 0)))], axis=1)
    deltas = jnp.abs(jnp.linspace(math.log(HY_DECAY_TARGET) / HY_DECAY_LONG_PCT,
                                  math.log(HY_DECAY_TARGET) / HY_DECAY_SHORT_PCT, HY_W, dtype=F32))
    tm = 1024
    const = lambda shp: pl.BlockSpec(shp, lambda i: (0, 0))
    row2 = lambda v: jnp.tile(v.astype(F32).reshape(1, -1), (1, 2))
    out = pl.pallas_call(
        _hy_filter_kernel,
        grid=(seqlen // tm,),
        in_specs=[pl.BlockSpec((tm, 256), lambda i: (i, 0)),
                  const((256, 2 * nh)), const((1, 2 * nh)), const((2 * nh, 2 * nh)), const((1, 2 * nh)),
                  const((2 * nh, 2 * nh)), const((1, 2 * nh)), const((1, 2 * nh)),
                  const((2 * nh, 2 * HY_W)), const((1, HY_W))],
        out_specs=pl.BlockSpec((2, tm, HY_W), lambda i: (0, i, 0)),
        out_shape=jax.ShapeDtypeStruct((2, seqlen, HY_W), F32),
        compiler_params=_cparams(("parallel",)),
        name="hy_filter_mlp",
    )(feat2, w1, row2(fb1), two(fw2), row2(fb2), two(fw3), row2(fb3), row2(freq), w4,
      deltas.reshape(1, HY_W))
    return out.reshape(2 * seqlen, HY_W)


def _od_out_kernel(na_ref, yt_ref, x0_ref, u_ref, x_ref, mod_ref, hb_ref, wout_ref, wr_ref, br_ref,
                   x_out, h2_out, lg_out):
    y = jnp.concatenate([yt_ref[j].T for j in range(yt_ref.shape[0])], axis=0)
    hy = x0_ref[...].astype(F32) * (y + u_ref[...].astype(F32) * hb_ref[...])
    cat = jnp.concatenate([na_ref[...], hy], axis=1).astype(BF16)
    mix = jnp.dot(cat, wout_ref[...], preferred_element_type=F32)
    _mixer_tail(x_ref[...], mix, mod_ref, wr_ref, br_ref, x_out, h2_out, lg_out)


def _latent_mod_index(b, i):
    return (b, 1, 0, 0)


def _od_out(na, yt, x0, u, xcat, mods, hb, wout, wr, br):
    bsz, l, _ = na.shape
    nblk = l // TB
    tok = lambda n: pl.BlockSpec((None, TB, n), lambda b, i: (b, i, 0))
    const = lambda shp: pl.BlockSpec(shp, lambda b, i: tuple(0 for _ in shp))
    return pl.pallas_call(
        _od_out_kernel,
        grid=(bsz, nblk),
        in_specs=[tok(NA_W), pl.BlockSpec((None, TB // FFT_N1, HY_W, FFT_N1), lambda b, i: (b, i, 0, 0)),
                  tok(HY_W), tok(HY_W),
                  pl.BlockSpec((None, TB, D_MODEL), lambda b, i: (b, i + 1, 0)),
                  pl.BlockSpec((None, None, 6, D_MODEL), _latent_mod_index),
                  const((1, HY_W)), const((D_MODEL, D_MODEL)), const((D_MODEL, 128)), const((1, 128))],
        out_specs=[tok(D_MODEL), pl.BlockSpec((TB * ROW_TILE, 128), lambda b, i: (b * nblk + i, 0)), tok(128)],
        out_shape=[jax.ShapeDtypeStruct((bsz, l, D_MODEL), F32), jax.ShapeDtypeStruct((bsz * l * ROW_TILE, 128), F32),
                   jax.ShapeDtypeStruct((bsz, l, 128), F32)],
        compiler_params=_cparams(("parallel", "parallel")),
        name="od_out",
    )(na, yt, x0, u, xcat, mods, hb, wout, wr, br)


def _mods(c, c_ctx, ada_w, ada_b):
    bsz = c.shape[0]
    depth, _, n = ada_w.shape
    cc = jnp.concatenate([c, c_ctx[None]], axis=0)
    a = jnp.pad(cc * _sigmoid(cc), ((0, 8 - (bsz + 1) % 8), (0, 0)))
    mp, tn = a.shape[0], 1024
    m = pl.pallas_call(
        _adaln_kernel,
        grid=(depth, n // tn),
        in_specs=[pl.BlockSpec((mp, D_MODEL), lambda l, j: (0, 0)),
                  pl.BlockSpec((None, D_MODEL, tn), lambda l, j: (l, 0, j)),
                  pl.BlockSpec((None, 1, tn), lambda l, j: (l, 0, j))],
        out_specs=pl.BlockSpec((None, mp, tn), lambda l, j: (l, 0, j)),
        out_shape=jax.ShapeDtypeStruct((depth, mp, n), F32),
        compiler_params=_cparams(("parallel", "parallel")),
        name="adaln_dense",
    )(a, ada_w, ada_b.reshape(depth, 1, n))
    mod_l = m[:, :bsz].reshape(depth, bsz, 1, 6, D_MODEL)
    mod_c = jnp.broadcast_to(m[:, bsz].reshape(depth, 1, 1, 6, D_MODEL), (depth, bsz, 1, 6, D_MODEL))
    return jnp.concatenate([mod_c, mod_l], axis=2)


def _rope_tables(seqlen):
    pos = jnp.arange(seqlen)
    half = GLA_DK // 4
    freqs = ROPE_BASE ** (-jnp.arange(half, dtype=F32) / half)
    ar = (pos // GRID_W).astype(F32)[:, None] * freqs
    ac = (pos % GRID_W).astype(F32)[:, None] * freqs
    cos = jnp.concatenate([jnp.cos(ar), jnp.cos(ar), jnp.cos(ac), jnp.cos(ac)], axis=1)
    sin = jnp.concatenate([-jnp.sin(ar), jnp.sin(ar), -jnp.sin(ac), jnp.sin(ac)], axis=1)
    cos = jnp.concatenate([jnp.ones((TB, GLA_DK), F32), cos], axis=0)
    sin = jnp.concatenate([jnp.zeros((TB, GLA_DK), F32), sin], axis=0)
    return jnp.tile(cos, (1, GLA_HEADS)), jnp.tile(sin, (1, GLA_HEADS))


def _router_weights(wg, bg, we, be):
    pad = 128 - N_GROUPS - N_EXPERTS
    wr = jnp.concatenate([wg, we, jnp.zeros((D_MODEL, pad), F32)], axis=1)
    br = jnp.concatenate([bg, be, jnp.zeros((pad,), F32)]).reshape(1, 128)
    return wr, br


def kernel(x, c, ctx, c_ctx, ada_w, ada_b, moe_wg, moe_bg, moe_we, moe_be, moe_w1, moe_w3, moe_w2, ev_w_in, ev_w_out, gla_wa2, gla_ba, gla_norm, s5_lam_re, s5_lam_im, s5_log_dt, s5_b_re, s5_b_im, s5_c_re, s5_c_im, s5_d, s5_w_glu, od_w_in, od_w_out, na_q_norm, na_k_norm, na_rpb, hy_conv_w, hy_conv_b, hy_fw1, hy_fb1, hy_fw2, hy_fb2, hy_fw3, hy_fb3, hy_freq, hy_fw4, hy_bias):
    bsz, seqlen, _ = x.shape
    assert ctx.shape[1] == TB and seqlen % TB == 0

    mods_all = _mods(c, c_ctx, ada_w, ada_b)
    mods = mods_all[0]
    w_in = ev_w_in[0]
    n_a = 2 * GLA_RANK
    a0 = 2 * GLA_QK + 2 * GLA_V
    w_ev = jnp.concatenate([w_in[:, :a0], w_in[:, a0 + n_a:], w_in[:, a0:a0 + n_a],
                            jnp.zeros((D_MODEL, 128 - n_a), F32)], axis=1).astype(BF16)
    wa = jnp.zeros((128, 2 * GLA_QK), F32)
    for d in range(2):
        wa = wa.at[d * GLA_RANK:(d + 1) * GLA_RANK, d * GLA_QK:(d + 1) * GLA_QK].set(gla_wa2[0, d])
    cos, sin = _rope_tables(seqlen)
    q, k, v, g, u, u_sw, la = _ev_proj(ctx, x, mods, w_ev, wa, gla_ba[0].reshape(1, 2 * GLA_QK), cos, sin)
    o_f, o_b = _gla(q, k, v, la)
    s5p = [t[0].astype(F32) for t in (s5_lam_re, s5_lam_im, s5_log_dt, s5_b_re, s5_b_im, s5_c_re, s5_c_im)]
    y_f = _s5_scan(u, _s5_mats(*[t[0] for t in s5p], rev=False), rev=False)
    y_b = _s5_scan(u_sw, _s5_mats(*[t[1] for t in s5p], rev=True), rev=True)
    wr, br = _router_weights(moe_wg[0], moe_bg[0], moe_we[0], moe_be[0])
    x1, h2, lg = _ev_out(o_f, o_b, g, y_f, y_b, u, ctx, x, mods,
                         jnp.tile(gla_norm[0], GLA_HEADS).reshape(1, GLA_V), s5_d[0].reshape(1, S5_W),
                         s5_w_glu[0].astype(BF16), ev_w_out[0].astype(BF16), wr, br)
    xcat = _moe(x1, h2, lg, mods, TB, moe_w1, moe_w3, moe_w2, 0)

    mods = mods_all[1]
    hd = np.arange(NA_W) // NA_DH
    gm = jnp.asarray((hd[:, None] == hd[None, :]).astype(np.float32) / NA_DH)
    qh, kh, vh, zh = _od_proj(xcat, mods, od_w_in[0].astype(BF16), gm,
                              jnp.tile(na_q_norm[0], NA_HEADS).reshape(1, NA_W),
                              jnp.tile(na_k_norm[0], NA_HEADS).reshape(1, NA_W))
    na = _na(qh, kh, vh, _na_bias_table(na_rpb[0]))
    x0, uh, ut = _hy_pre(zh, hy_conv_w[0], hy_conv_b[0].reshape(1, 3 * HY_W))
    kfilt = _hy_filter(seqlen, hy_fw1[0], hy_fb1[0], hy_fw2[0], hy_fb2[0], hy_fw3[0], hy_fb3[0],
                       hy_freq[0], hy_fw4[0])
    yt = _hy_conv(ut, kfilt)
    wr, br = _router_weights(moe_wg[1], moe_bg[1], moe_we[1], moe_be[1])
    xl, h2, lg = _od_out(na, yt, x0, uh, xcat, mods, hy_bias[0].reshape(1, HY_W),
                         od_w_out[0].astype(BF16), wr, br)
    return _moe(xl, h2, lg, mods, 0, moe_w1, moe_w3, moe_w2, 1)
```

```python
import functools
import math

import numpy as np
import jax
import jax.numpy as jnp
from jax import lax
from jax.experimental import pallas as pl
from jax.experimental.pallas import tpu as pltpu

F32, BF16 = jnp.float32, jnp.bfloat16
HI = lax.Precision.HIGHEST

D_MODEL = 1024
GRID_W = 64
EPS = 1e-6
ROPE_BASE = 10000.0
NEG_INF = -1e30
GLA_HEADS, GLA_DK, GLA_DV = 4, 64, 128
GLA_QK, GLA_V = GLA_HEADS * GLA_DK, GLA_HEADS * GLA_DV
GLA_RANK = 16
GLA_TAU = 16.0
GLA_CHUNK = 64
GLA_LOG_ALPHA_MIN = -1.0
S5_W, S5_H, S5_P = 512, 16, 64
S5_G = S5_W // S5_H
S5_CHUNK = 8
S5_TBLK = 1408
NA_HEADS, NA_DH = 8, 64
NA_W = NA_HEADS * NA_DH
WIN_R, WIN_C = 8, 16
NA_HG = 4
NA_ROWS = 4
HY_W = 512
HY_SHORT = 3
HY_EMB = 33
HY_DECAY_TARGET = 1e-2
HY_DECAY_SHORT_PCT = 0.3
HY_DECAY_LONG_PCT = 1.5
N_GROUPS, EXP_PER_GROUP = 4, 8
N_EXPERTS = N_GROUPS * EXP_PER_GROUP
D_EXPERT = 512
TOP_K = 2
MOE_BLOCK = 256

TB = 256
PROJ_TILE = 768
COMBINE_TILE = 1024
HALO = 16
FFT_N1 = 128
FFT_CG = 32
V7X_VMEM_LIMIT = 52 * 1024 * 1024


def _cparams(sem):
    return pltpu.CompilerParams(dimension_semantics=sem, vmem_limit_bytes=V7X_VMEM_LIMIT)


def _sigmoid(x):
    return 1.0 / (1.0 + jnp.exp(-x))


ROW_TILE = D_MODEL // 128


def _to_token_tiles(ref, val):
    n = val.shape[0]
    for j in range(ROW_TILE):
        ref[pl.ds(j, n, stride=ROW_TILE), :] = val[:, j * 128:(j + 1) * 128]


def _from_token_tiles(ref, n):
    return jnp.concatenate([ref[pl.ds(j, n, stride=ROW_TILE), :] for j in range(ROW_TILE)], axis=1)


def _split_bf16(x):
    hi = x.astype(BF16)
    return hi, (x - hi.astype(F32)).astype(BF16)


def _dot_x3(a, b):
    a_hi, a_lo = _split_bf16(a)
    b_hi, b_lo = _split_bf16(b)
    d = lambda p, q: jnp.dot(p, q, preferred_element_type=F32)
    return d(a_hi, b_hi) + d(a_lo, b_hi) + d(a_hi, b_lo)


def _rms_rows(x):
    return x * lax.rsqrt(jnp.mean(x * x, axis=-1, keepdims=True) + EPS)


def _adaln_kernel(a_ref, w_ref, b_ref, o_ref):
    o_ref[...] = jnp.dot(a_ref[...], w_ref[...], precision=HI, preferred_element_type=F32) + b_ref[...]


def _mod_index(b, i):
    return (b, jnp.minimum(i, 1), 0, 0)


def _swapped_index(nblk, b, i):
    return (b, jnp.where(i == 0, nblk - 1, i - 1), 0)


EV_NQ, EV_NK, EV_NV, EV_NG, EV_NU = 0, 256, 512, 1024, 1536
EV_NA = 2048
EV_NTOT = 2176


def _stream_tile(ctx_ref, x_ref):
    return jnp.where(pl.program_id(1) == 0, ctx_ref[...], x_ref[...])


def _stream_specs():
    return [pl.BlockSpec((None, TB, D_MODEL), lambda b, i: (b, 0, 0)),
            pl.BlockSpec((None, TB, D_MODEL), lambda b, i: (b, jnp.maximum(i - 1, 0), 0))]


def _ev_proj_kernel(ctx_ref, x_ref, mod_ref, w_ref, wa_ref, ba_ref, cos_ref, sin_ref,
                    q_ref, k_ref, v_ref, g_ref, u_ref, usw_ref, la_ref):
    x = _stream_tile(ctx_ref, x_ref)
    h = _rms_rows(x) * (1.0 + mod_ref[1:2, :]) + mod_ref[0:1, :]
    z = jnp.dot(h.astype(BF16), w_ref[...], preferred_element_type=F32)
    lane = lax.broadcasted_iota(jnp.int32, (x.shape[0], GLA_QK), 1)
    first = (lane % 32) < 16
    cos, sin = cos_ref[...], sin_ref[...]

    def rot(t):
        partner = jnp.where(first, pltpu.roll(t, GLA_QK - 16, 1), pltpu.roll(t, 16, 1))
        return t * cos + partner * sin

    q_ref[...] = rot(z[:, EV_NQ:EV_NQ + GLA_QK]) * (GLA_DK ** -0.5)
    k_ref[...] = rot(z[:, EV_NK:EV_NK + GLA_QK])
    v_ref[...] = z[:, EV_NV:EV_NV + GLA_V].astype(BF16)
    g_ref[...] = z[:, EV_NG:EV_NG + GLA_V].astype(BF16)
    u_ref[...] = z[:, EV_NU:EV_NU + S5_W]
    usw_ref[...] = z[:, EV_NU:EV_NU + S5_W]
    a = z[:, EV_NA:EV_NA + 128]
    pre = _dot_x3(a, wa_ref[...]) + ba_ref[...]
    ls = jnp.minimum(pre, 0.0) - jnp.log1p(jnp.exp(-jnp.abs(pre)))
    la_ref[...] = jnp.maximum(ls / GLA_TAU, GLA_LOG_ALPHA_MIN)


def _ev_proj(ctx, x, mods, w, wa, ba, cos, sin):
    bsz = x.shape[0]
    lt = ctx.shape[1] + x.shape[1]
    nblk = lt // TB
    tok = lambda n: pl.BlockSpec((None, TB, n), lambda b, i: (b, i, 0))
    const = lambda shp: pl.BlockSpec(shp, lambda b, i: tuple(0 for _ in shp))
    return pl.pallas_call(
        _ev_proj_kernel,
        grid=(bsz, nblk),
        in_specs=_stream_specs() + [
                  pl.BlockSpec((None, None, 6, D_MODEL), _mod_index),
                  const((D_MODEL, EV_NTOT)), const((128, 2 * GLA_QK)), const((1, 2 * GLA_QK)),
                  pl.BlockSpec((TB, GLA_QK), lambda b, i: (i, 0)),
                  pl.BlockSpec((TB, GLA_QK), lambda b, i: (i, 0))],
        out_specs=[tok(GLA_QK), tok(GLA_QK), tok(GLA_V), tok(GLA_V), tok(S5_W),
                   pl.BlockSpec((None, TB, S5_W), functools.partial(_swapped_index, nblk)), tok(2 * GLA_QK)],
        out_shape=[jax.ShapeDtypeStruct((bsz, lt, GLA_QK), F32),
                   jax.ShapeDtypeStruct((bsz, lt, GLA_QK), F32),
                   jax.ShapeDtypeStruct((bsz, lt, GLA_V), BF16),
                   jax.ShapeDtypeStruct((bsz, lt, GLA_V), BF16),
                   jax.ShapeDtypeStruct((bsz, lt, S5_W), F32),
                   jax.ShapeDtypeStruct((bsz, lt, S5_W), F32),
                   jax.ShapeDtypeStruct((bsz, lt, 2 * GLA_QK), F32)],
        compiler_params=_cparams(("parallel", "parallel")),
        name="ev_proj",
    )(ctx, x, mods, w, wa, ba, cos, sin)


def _gla_kernel(qf_ref, kf_ref, vf_ref, laf_ref, qb_ref, kb_ref, vb_ref, lab_ref,
                of_ref, ob_ref, s_scr):
    i = pl.program_id(1)

    @pl.when(i == 0)
    def _():
        s_scr[...] = jnp.zeros_like(s_scr)

    c = GLA_CHUNK
    nh = GLA_HEADS
    row = lax.broadcasted_iota(jnp.int32, (c, c), 0)
    col = lax.broadcasted_iota(jnp.int32, (c, c), 1)
    row4 = lax.broadcasted_iota(jnp.int32, (nh * c, c), 0) % c
    col4 = lax.broadcasted_iota(jnp.int32, (nh * c, c), 1)
    lane_head = lax.broadcasted_iota(jnp.int32, (c, GLA_QK), 1) // GLA_DK
    out_head = lax.broadcasted_iota(jnp.int32, (c, GLA_V), 1) // GLA_DV
    bd_mask = (lax.broadcasted_iota(jnp.int32, (GLA_V, GLA_QK), 0) // GLA_DV
               == lax.broadcasted_iota(jnp.int32, (GLA_V, GLA_QK), 1) // GLA_DK)
    nchunk = qf_ref.shape[0] // c
    nt = (((1,), (1,)), ((), ()))
    tn = (((0,), (0,)), ((), ()))

    def one_chunk(refs, o_ref, d, r0):
        q_ref, k_ref, v_ref, la_ref = refs
        fwd = d == 0
        sl = pl.ds(r0, c)
        qc, kc, vc, lac = q_ref[sl, :], k_ref[sl, :], v_ref[sl, :], la_ref[sl, :]
        tri = ((row >= col) if fwd else (row <= col)).astype(BF16)
        la_hi, la_lo = _split_bf16(lac)
        b = (jnp.dot(tri, la_hi, preferred_element_type=F32)
             + jnp.dot(tri, la_lo, preferred_element_type=F32))
        b_last = b[c - 1:c, :] if fwd else b[0:1, :]
        qe = (qc * jnp.exp(b)).astype(BF16)
        ke = (kc * jnp.exp(-b)).astype(BF16)
        kd = (kc * jnp.exp(b_last - b)).astype(BF16)
        st = s_scr[d]
        o = lax.dot_general(qe, st.astype(BF16), nt, preferred_element_type=F32)
        q4 = jnp.concatenate([jnp.where(lane_head == h, qe, jnp.zeros_like(qe)) for h in range(nh)], axis=0)
        att = lax.dot_general(q4, ke, nt, preferred_element_type=F32)
        att_mask = (row4 >= col4) if fwd else (row4 < col4)
        o4 = jnp.dot(jnp.where(att_mask, att, 0.0).astype(BF16), vc, preferred_element_type=F32)
        for h in range(nh):
            o = o + jnp.where(out_head == h, o4[h * c:(h + 1) * c, :], 0.0)
        o_ref[sl, :] = o.astype(o_ref.dtype)
        upd_t = lax.dot_general(vc, kd, tn, preferred_element_type=F32)
        s_scr[d] = st * jnp.exp(b_last) + jnp.where(bd_mask, upd_t, 0.0)

    def body(j, carry):
        one_chunk((qf_ref, kf_ref, vf_ref, laf_ref), of_ref, 0, pl.multiple_of(j * c, c))
        one_chunk((qb_ref, kb_ref, vb_ref, lab_ref), ob_ref, 1, pl.multiple_of((nchunk - 1 - j) * c, c))
        return carry

    lax.fori_loop(0, nchunk, body, 0, unroll=True)


def _gla(q, k, v, la):
    bsz, lt, _ = q.shape
    nblk = lt // TB
    fwd_map = lambda b, i: (b, i, 0)
    bwd_blk = lambda i: jnp.where(i == 0, 0, nblk - i)
    bwd_map = lambda b, i: (b, bwd_blk(i), 0)
    bwd_map_la = lambda b, i: (b, bwd_blk(i), 1)
    spec = lambda n, m: pl.BlockSpec((None, TB, n), m)
    return pl.pallas_call(
        _gla_kernel,
        grid=(bsz, nblk),
        in_specs=[spec(GLA_QK, fwd_map), spec(GLA_QK, fwd_map), spec(GLA_V, fwd_map), spec(GLA_QK, fwd_map),
                  spec(GLA_QK, bwd_map), spec(GLA_QK, bwd_map), spec(GLA_V, bwd_map), spec(GLA_QK, bwd_map_la)],
        out_specs=[spec(GLA_V, fwd_map), spec(GLA_V, bwd_map)],
        out_shape=[jax.ShapeDtypeStruct((bsz, lt, GLA_V), BF16), jax.ShapeDtypeStruct((bsz, lt, GLA_V), BF16)],
        scratch_shapes=[pltpu.VMEM((2, GLA_V, GLA_QK), F32)],
        compiler_params=_cparams(("parallel", "arbitrary")),
        name="gla_scan",
    )(q, k, v, la, q, k, v, la)


def _s5_kernel(u_ref, wm_ref, tm_ref, cm_ref, ar_ref, ai_ref, y_ref, w_scr, hp_scr, h_scr, *, rev):
    @pl.when(pl.program_id(1) == 0)
    def _():
        h_scr[...] = jnp.zeros_like(h_scr)

    bsz, ntok, _ = u_ref.shape
    nc = ntok // S5_CHUNK
    half = 8 * S5_P
    x = jnp.concatenate(
        [jnp.concatenate([u_ref[b, pl.ds(s, nc, stride=S5_CHUNK), :] for s in range(S5_CHUNK)], axis=1)
         for b in range(bsz)], axis=0).astype(BF16)
    w_scr[...] = jnp.dot(x, wm_ref[...], preferred_element_type=F32)
    ar, ai = ar_ref[...], ai_ref[...]

    def body(j, hs):
        c = (nc - 1 - j) if rev else j
        out = []
        for b in range(bsz):
            re, im = hs[b]
            r = b * nc + c
            hp_scr[pl.ds(r, 1), :] = jnp.concatenate([re, im], axis=1)
            w = w_scr[pl.ds(r, 1), :]
            out.append((ar * re - ai * im + w[:, :half], ar * im + ai * re + w[:, half:]))
        return tuple(out)

    hs = lax.fori_loop(0, nc, body, tuple((h_scr[b:b + 1, :half], h_scr[b:b + 1, half:]) for b in range(bsz)))
    for b in range(bsz):
        h_scr[b:b + 1, :] = jnp.concatenate(hs[b], axis=1)
    mt = 256
    ntile = x.shape[1] // mt
    cols = []
    for jt in range(ntile):
        acc = None
        for it in (range(jt, ntile) if rev else range(jt + 1)):
            part = jnp.dot(x[:, it * mt:(it + 1) * mt], tm_ref[it * mt:(it + 1) * mt, jt * mt:(jt + 1) * mt],
                           preferred_element_type=F32)
            acc = part if acc is None else acc + part
        cols.append(acc)
    y = (jnp.concatenate(cols, axis=1)
         + jnp.dot(hp_scr[...].astype(BF16), cm_ref[...], preferred_element_type=F32))
    for b in range(bsz):
        for s in range(S5_CHUNK):
            y_ref[b, pl.ds(s, nc, stride=S5_CHUNK), :] = y[b * nc:(b + 1) * nc, s * 128:(s + 1) * 128]


def _s5_mats(lam_re, lam_im, log_dt, b_re, b_im, c_re, c_im, rev):
    t16 = S5_CHUNK
    dt = jnp.exp(log_dt)[:, None]
    mag = jnp.exp(lam_re * dt)
    a_re, a_im = mag * jnp.cos(lam_im * dt), mag * jnp.sin(lam_im * dt)
    den = lam_re * lam_re + lam_im * lam_im
    nr = a_re - 1.0
    co_re = ((nr * lam_re + a_im * lam_im) / den)[..., None]
    co_im = ((a_im * lam_re - nr * lam_im) / den)[..., None]
    bb_re, bb_im = co_re * b_re - co_im * b_im, co_re * b_im + co_im * b_re
    pr, pi = [jnp.ones_like(a_re)], [jnp.zeros_like(a_im)]
    for _ in range(t16):
        pr, pi = pr + [pr[-1] * a_re - pi[-1] * a_im], pi + [pr[-1] * a_im + pi[-1] * a_re]
    pw_re, pw_im = jnp.stack(pr), jnp.stack(pi)
    g = lam_re.shape[0]
    e_re, e_im = pw_re[t16 - 1::-1][:t16], pw_im[t16 - 1::-1][:t16]
    wre = jnp.einsum('sgp,gph->gshp', e_re, bb_re) - jnp.einsum('sgp,gph->gshp', e_im, bb_im)
    wim = jnp.einsum('sgp,gph->gshp', e_re, bb_im) + jnp.einsum('sgp,gph->gshp', e_im, bb_re)
    cb_re = jnp.einsum('gkp,gph->gpkh', c_re, bb_re) - jnp.einsum('gkp,gph->gpkh', c_im, bb_im)
    cb_im = jnp.einsum('gkp,gph->gpkh', c_re, bb_im) + jnp.einsum('gkp,gph->gpkh', c_im, bb_re)
    kd = jnp.einsum('dgp,gpkh->dgkh', pw_re[:t16], cb_re) - jnp.einsum('dgp,gpkh->dgkh', pw_im[:t16], cb_im)
    lag = np.arange(t16)[None, :] - np.arange(t16)[:, None]
    toe = jnp.where((lag >= 0)[:, :, None, None, None], kd[np.clip(lag, 0, t16 - 1)], 0.0)
    toe = toe.transpose(2, 0, 4, 1, 3)
    q_re, q_im = pw_re[1:], pw_im[1:]
    ca_re = jnp.einsum('gkp,tgp->gptk', c_re, q_re) - jnp.einsum('gkp,tgp->gptk', c_im, q_im)
    ca_im = jnp.einsum('gkp,tgp->gptk', c_re, q_im) + jnp.einsum('gkp,tgp->gptk', c_im, q_re)
    if rev:
        wre, wim = wre[:, ::-1], wim[:, ::-1]
        toe = toe[:, ::-1, :, ::-1]
        ca_re, ca_im = ca_re[:, :, ::-1], ca_im[:, :, ::-1]
    nq, gl = g // 8, 8
    nd = t16 * S5_H
    tok_hot = np.zeros((gl, nd, t16 * 128), np.float32)
    st_hot = np.zeros((gl, 2 * S5_P, 2 * gl * S5_P), np.float32)
    for gi in range(gl):
        a = np.arange(nd)
        tok_hot[gi, a, (a // S5_H) * 128 + gi * S5_H + a % S5_H] = 1.0
        a = np.arange(2 * S5_P)
        st_hot[gi, a, (a // S5_P) * gl * S5_P + gi * S5_P + a % S5_P] = 1.0
    place = lambda rows, blk, cols: jnp.einsum('gar,qgab,gbc->qrc', rows, blk, cols)
    wg = jnp.concatenate([wre, wim], axis=-1).reshape(nq, gl, nd, 2 * S5_P)
    tg = toe.reshape(nq, gl, nd, nd)
    cg = jnp.concatenate([ca_re, -ca_im], axis=1).reshape(nq, gl, 2 * S5_P, nd)
    wm = place(tok_hot, wg, st_hot)
    tmat = place(tok_hot, tg, tok_hot)
    cm = place(st_hot, cg, tok_hot)
    return (wm.astype(BF16), tmat.astype(BF16), cm.astype(BF16),
            pw_re[t16].reshape(nq, 1, gl * S5_P), pw_im[t16].reshape(nq, 1, gl * S5_P))


def _s5_scan(u, mats, rev):
    bsz, lt, _ = u.shape
    tblk = max(d for d in range(64, S5_TBLK + 1, 64) if lt % d == 0)
    nblk = lt // tblk
    nq = S5_G // 8
    wm, tmat, cm, ar, ai = mats
    kw, ks = S5_CHUNK * 128, 8 * S5_P
    tmap = (lambda q, t: (0, nblk - 1 - t, q)) if rev else (lambda q, t: (0, t, q))
    per = lambda shp: pl.BlockSpec((None,) + shp, lambda q, t: (q,) + tuple(0 for _ in shp))
    rows = bsz * tblk // S5_CHUNK
    return pl.pallas_call(
        functools.partial(_s5_kernel, rev=rev),
        grid=(nq, nblk),
        in_specs=[pl.BlockSpec((bsz, tblk, 128), tmap),
                  per((kw, 2 * ks)), per((kw, kw)), per((2 * ks, kw)), per((1, ks)), per((1, ks))],
        out_specs=pl.BlockSpec((bsz, tblk, 128), tmap),
        out_shape=jax.ShapeDtypeStruct((bsz, lt, S5_W), F32),
        scratch_shapes=[pltpu.VMEM((rows, 2 * ks), F32), pltpu.VMEM((rows, 2 * ks), F32),
                        pltpu.VMEM((8, 2 * ks), F32)],
        compiler_params=_cparams(("parallel", "arbitrary")),
        name="s5_scan",
    )(u, wm, tmat, cm, ar, ai)


def _mixer_tail(x, mix, mods, wr_ref, br_ref, x_out, h2_out, lg_out):
    gate, shift, scale = mods
    xn = x + gate * mix
    x_out[...] = xn
    h2 = _rms_rows(xn) * (1.0 + scale) + shift
    _to_token_tiles(h2_out, h2)
    lg_out[...] = _route_tail(_dot_x3(h2, wr_ref[...]) + br_ref[...])


def _route_tail(lg):
    lane = lax.broadcasted_iota(jnp.int32, lg.shape, 1)
    lane_f = lane.astype(F32)
    big = jnp.float32(128.0)
    row_max = lambda t: jnp.max(t, axis=1, keepdims=True)
    first_lane = lambda hit: jnp.min(jnp.where(hit, lane_f, big), axis=1, keepdims=True)
    is_g = lane < N_GROUPS
    g_logit = jnp.where(is_g, lg, NEG_INF)
    g_max = row_max(g_logit)
    grp = first_lane(g_logit == g_max)
    g_w = 1.0 / jnp.sum(jnp.where(is_g, jnp.exp(lg - g_max), 0.0), axis=1, keepdims=True)
    lo = N_GROUPS + EXP_PER_GROUP * grp
    in_grp = jnp.logical_and(lane_f >= lo, lane_f < lo + EXP_PER_GROUP)
    e_max = row_max(jnp.where(in_grp, lg, NEG_INF))
    pe = jnp.where(in_grp, jnp.exp(lg - e_max), 0.0)
    p = pe / jnp.sum(pe, axis=1, keepdims=True)
    cand = jnp.where(in_grp, p, -1.0)
    p1 = row_max(cand)
    i1 = first_lane(cand == p1)
    cand2 = jnp.where(lane_f == i1, -1.0, cand)
    p2 = row_max(cand2)
    i2 = first_lane(cand2 == p2)
    scale = g_w / (p1 + p2)
    out = jnp.where(lane == 0, scale * p1, jnp.where(lane == 1, scale * p2,
                    jnp.where(lane == 2, i1 - N_GROUPS, jnp.where(lane == 3, i2 - N_GROUPS, 0.0))))
    return out


def _ev_out_kernel(of_ref, ob_ref, g_ref, yf_ref, u_ref, ctx_ref, mod_ref,
                   gn_ref, ds_ref, wglu_ref, wout_ref, wr_ref, br_ref, *rest, nsub):
    yb_refs, x_refs, (x_out, h2_out, lg_out) = rest[:nsub], rest[nsub:2 * nsub], rest[2 * nsub:]
    i = pl.program_id(1)
    tm = nsub * TB
    yb = jnp.concatenate([r[...] for r in yb_refs], axis=0)
    x = jnp.concatenate([jnp.where(nsub * i + k == 0, ctx_ref[...], r[...]) for k, r in enumerate(x_refs)], axis=0)
    is_ctx = i * tm + lax.broadcasted_iota(jnp.int32, (tm, 1), 0) < TB
    mod = lambda r: jnp.where(is_ctx, mod_ref[0, r:r + 1, :], mod_ref[1, r:r + 1, :])
    o = of_ref[...].astype(F32) + ob_ref[...].astype(F32)
    og = jnp.concatenate([_rms_rows(o[:, h * GLA_DV:(h + 1) * GLA_DV]) for h in range(GLA_HEADS)], axis=1)
    g = g_ref[...].astype(F32)
    og = og * gn_ref[...] * (g * _sigmoid(g))
    t = yf_ref[...] + yb + ds_ref[...] * u_ref[...]
    y = t * (0.5 * (1.0 + jnp.tanh(math.sqrt(2.0 / math.pi) * (t + 0.044715 * (t * t * t)))))
    y = y * _sigmoid(jnp.dot(y.astype(BF16), wglu_ref[...], preferred_element_type=F32))
    cat = jnp.concatenate([og, y], axis=1).astype(BF16)
    mix = jnp.dot(cat, wout_ref[...], preferred_element_type=F32)
    _mixer_tail(x, mix, (mod(2), mod(3), mod(4)), wr_ref, br_ref, x_out, h2_out, lg_out)


def _ev_out(o_f, o_b, g, y_f, y_b, u, ctx, x, mods, gn, ds, wglu, wout, wr, br):
    bsz, lt, _ = u.shape
    nstream = lt // TB
    nsub = 3 if nstream % 3 == 0 else 1
    tm = nsub * TB
    nblk = lt // tm
    tok = lambda n: pl.BlockSpec((None, tm, n), lambda b, i: (b, i, 0))
    const = lambda shp: pl.BlockSpec(shp, lambda b, i: tuple(0 for _ in shp))
    sub = lambda n, fn: [pl.BlockSpec((None, TB, n), functools.partial(fn, k)) for k in range(nsub)]
    yb_map = lambda k, b, i: _swapped_index(nstream, b, nsub * i + k)
    x_map = lambda k, b, i: (b, jnp.maximum(nsub * i + k - 1, 0), 0)
    return pl.pallas_call(
        functools.partial(_ev_out_kernel, nsub=nsub),
        grid=(bsz, nblk),
        in_specs=[tok(512), tok(512), tok(512), tok(512), tok(512),
                  pl.BlockSpec((None, TB, D_MODEL), lambda b, i: (b, 0, 0)),
                  pl.BlockSpec((None, 2, 6, D_MODEL), lambda b, i: (b, 0, 0, 0)),
                  const((1, 512)), const((1, 512)), const((512, 512)), const((D_MODEL, D_MODEL)),
                  const((D_MODEL, 128)), const((1, 128))] + sub(S5_W, yb_map) + sub(D_MODEL, x_map),
        out_specs=[tok(D_MODEL), pl.BlockSpec((tm * ROW_TILE, 128), lambda b, i: (b * nblk + i, 0)), tok(128)],
        out_shape=[jax.ShapeDtypeStruct((bsz, lt, D_MODEL), F32),
                   jax.ShapeDtypeStruct((bsz * lt * ROW_TILE, 128), F32),
                   jax.ShapeDtypeStruct((bsz, lt, 128), F32)],
        compiler_params=_cparams(("parallel", "parallel")),
        name="ev_out",
    )(o_f, o_b, g, y_f, u, ctx, mods, gn, ds, wglu, wout, wr, br, *([y_b] * nsub), *([x] * nsub))


def _route(route):
    n_tok = route.shape[0]
    gate = route[:, :TOP_K]
    eid = route[:, TOP_K:2 * TOP_K].astype(jnp.int32).reshape(-1)
    n_asg = n_tok * TOP_K
    order = jnp.argsort(eid).astype(jnp.int32)
    counts = jnp.sum((eid[:, None] == jnp.arange(N_EXPERTS)[None, :]).astype(jnp.int32), axis=0)
    padded = (counts + MOE_BLOCK - 1) // MOE_BLOCK * MOE_BLOCK
    pad_end = jnp.cumsum(padded)
    pad_start = pad_end - padded
    cnt_start = jnp.cumsum(counts) - counts
    n_blocks = -(-n_asg // MOE_BLOCK) + N_EXPERTS
    blk_start = jnp.arange(n_blocks, dtype=jnp.int32) * MOE_BLOCK
    blk_e = jnp.minimum(jnp.sum((pad_end[None, :] <= blk_start[:, None]).astype(jnp.int32), axis=1), N_EXPERTS - 1)
    pos = jnp.arange(n_blocks * MOE_BLOCK, dtype=jnp.int32)
    pos_e = jnp.repeat(blk_e, MOE_BLOCK)
    rank = pos - pad_start[pos_e]
    src = jnp.clip(cnt_start[pos_e] + rank, 0, n_asg - 1)
    slot_buf = jnp.where(rank < counts[pos_e], order[src], n_asg).astype(jnp.int32)
    n_valid = jnp.sum((slot_buf < n_asg).reshape(n_blocks, MOE_BLOCK), axis=1).astype(jnp.int32)
    tok = lax.shift_right_logical(slot_buf, 1)
    src_rows = jnp.minimum(tok, n_tok - 1) * ROW_TILE
    spare = n_asg + (pos // MOE_BLOCK % 2) * MOE_BLOCK + pos % MOE_BLOCK
    dst_rows = jnp.where(slot_buf < n_asg, (slot_buf & 1) * n_tok + tok, spare)
    lead = n_asg + jnp.arange(2 * MOE_BLOCK, dtype=jnp.int32)
    dst_rows = jnp.concatenate([lead, dst_rows]) * ROW_TILE
    return src_rows, dst_rows, blk_e, n_valid, gate.astype(F32)


def _moe_kernel(src_ref, dst_ref, blke_ref, nvalid_ref, h_hbm, w1_ref, w3_ref, w2_ref, z_hbm,
                xbuf0, xbuf1, ybuf0, ybuf1, wb1, wb3, wb2, gsem, ssem):
    i = pl.program_id(0)
    nblk = pl.num_programs(0)
    ns = ROW_TILE
    xb, yb = (xbuf0, xbuf1), (ybuf0, ybuf1)
    lead = 2 * MOE_BLOCK

    def issue_gather(blk, buf):
        base = blk * MOE_BLOCK
        for r in range(MOE_BLOCK):
            src = pl.multiple_of(src_ref[base + r], ns)
            pltpu.make_async_copy(h_hbm.at[pl.ds(src, ns)], xb[buf].at[pl.ds(r * ns, ns)],
                                  gsem.at[buf]).start(priority=r % 2)

    def issue_scatter(blk, buf):
        base = lead + blk * MOE_BLOCK
        for r in range(MOE_BLOCK):
            dst = pl.multiple_of(dst_ref[base + r], ns)
            pltpu.make_async_copy(yb[buf].at[pl.ds(r * ns, ns)], z_hbm.at[pl.ds(dst, ns)],
                                  ssem.at[buf]).start(priority=r % 2)

    def wait_gather(buf):
        pltpu.make_async_copy(h_hbm.at[pl.ds(0, MOE_BLOCK * ns)], xb[buf], gsem.at[buf]).wait()

    def wait_scatter(buf):
        pltpu.make_async_copy(yb[buf], z_hbm.at[pl.ds(0, MOE_BLOCK * ns)], ssem.at[buf]).wait()

    used = nvalid_ref[i] > 0

    @pl.when(i == 0)
    def _():
        ybuf0[...] = jnp.zeros_like(ybuf0)
        ybuf1[...] = jnp.zeros_like(ybuf1)
        issue_scatter(-2, 0)
        issue_gather(0, 0)

    def step(cur):
        oth = 1 - cur
        wait_gather(cur)
        issue_gather(jnp.minimum(i + 1, nblk - 1), oth)
        issue_scatter(i - 1, oth)
        x = _from_token_tiles(xb[cur], MOE_BLOCK).astype(BF16)

        @pl.when(jnp.logical_or(i == 0, blke_ref[i] != blke_ref[jnp.maximum(i - 1, 0)]))
        def _():
            wb1[...] = w1_ref[...].astype(BF16)
            wb3[...] = w3_ref[...].astype(BF16)
            wb2[...] = w2_ref[...].astype(BF16)

        h1 = jnp.dot(x, wb1[...], preferred_element_type=F32)
        h3 = jnp.dot(x, wb3[...], preferred_element_type=F32)
        a = (h1 * _sigmoid(h1) * h3).astype(BF16)
        y = jnp.dot(a, wb2[...], preferred_element_type=F32)
        wait_scatter(cur)
        _to_token_tiles(yb[cur], y)

    def drain(last_par):
        wait_gather(1 - last_par)
        issue_scatter(jnp.where(used, i, i - 1), last_par)
        wait_scatter(1 - last_par)
        wait_scatter(last_par)

    for par in range(2):
        @pl.when(jnp.logical_and(used, i % 2 == par))
        def _():
            step(par)

    first_unused = jnp.logical_and(jnp.logical_not(used),
                                   jnp.logical_and(i > 0, nvalid_ref[jnp.maximum(i - 1, 0)] > 0))
    last_used = jnp.logical_and(used, i == nblk - 1)
    for par in range(2):
        @pl.when(jnp.logical_or(jnp.logical_and(first_unused, (i - 1) % 2 == par),
                                jnp.logical_and(last_used, i % 2 == par)))
        def _():
            drain(par)


def _moe_experts(h2, src_rows, dst_rows, blk_e, n_valid, w1, w3, w2, layer):
    n_tok = h2.shape[0] // ROW_TILE
    n_blocks = blk_e.shape[0]
    wspec = lambda shp: pl.BlockSpec((None, None) + shp, lambda i, src, dst, blke, nvalid: (layer, blke[i], 0, 0))
    grid_spec = pltpu.PrefetchScalarGridSpec(
        num_scalar_prefetch=4,
        grid=(n_blocks,),
        in_specs=[pl.BlockSpec(memory_space=pl.ANY),
                  wspec((D_MODEL, D_EXPERT)), wspec((D_MODEL, D_EXPERT)), wspec((D_EXPERT, D_MODEL))],
        out_specs=pl.BlockSpec(memory_space=pl.ANY),
        scratch_shapes=[pltpu.VMEM((MOE_BLOCK * ROW_TILE, 128), F32), pltpu.VMEM((MOE_BLOCK * ROW_TILE, 128), F32),
                        pltpu.VMEM((MOE_BLOCK * ROW_TILE, 128), F32), pltpu.VMEM((MOE_BLOCK * ROW_TILE, 128), F32),
                        pltpu.VMEM((D_MODEL, D_EXPERT), BF16), pltpu.VMEM((D_MODEL, D_EXPERT), BF16),
                        pltpu.VMEM((D_EXPERT, D_MODEL), BF16),
                        pltpu.SemaphoreType.DMA((2,)), pltpu.SemaphoreType.DMA((2,))])
    return pl.pallas_call(
        _moe_kernel,
        grid_spec=grid_spec,
        out_shape=jax.ShapeDtypeStruct(((TOP_K * n_tok + 2 * MOE_BLOCK) * ROW_TILE, 128), F32),
        compiler_params=_cparams(("arbitrary",)),
        name="moe_experts",
    )(src_rows, dst_rows, blk_e, n_valid, h2, w1, w3, w2)


def _moe_combine_kernel(x_ref, z0_ref, z1_ref, gate_ref, mod_ref, o_ref, *, n_ctx):
    gate = gate_ref[...]
    tm = x_ref.shape[0]
    y = gate[:, 0:1] * _from_token_tiles(z0_ref, tm) + gate[:, 1:2] * _from_token_tiles(z1_ref, tm)
    g_out = mod_ref[1, 5:6, :]
    if n_ctx:
        is_ctx = pl.program_id(1) * tm + lax.broadcasted_iota(jnp.int32, (tm, 1), 0) < n_ctx
        g_out = jnp.where(is_ctx, mod_ref[0, 5:6, :], g_out)
    o_ref[...] = x_ref[...] + g_out * y


def _moe_combine(x, z, gate, mods, n_ctx):
    bsz, lt, _ = x.shape
    tm = max(d for d in range(TB, COMBINE_TILE + 1, TB) if lt % d == 0)
    nblk = lt // tm
    gate3 = gate.reshape(bsz, lt, TOP_K)
    return pl.pallas_call(
        functools.partial(_moe_combine_kernel, n_ctx=n_ctx),
        grid=(bsz, nblk),
        in_specs=[pl.BlockSpec((None, tm, D_MODEL), lambda b, i: (b, i, 0)),
                  pl.BlockSpec((tm * ROW_TILE, 128), lambda b, i: (b * nblk + i, 0)),
                  pl.BlockSpec((tm * ROW_TILE, 128), lambda b, i: ((bsz + b) * nblk + i, 0)),
                  pl.BlockSpec((None, tm, TOP_K), lambda b, i: (b, i, 0)),
                  pl.BlockSpec((None, 2, 6, D_MODEL), lambda b, i: (b, 0, 0, 0))],
        out_specs=pl.BlockSpec((None, tm, D_MODEL), lambda b, i: (b, i, 0)),
        out_shape=jax.ShapeDtypeStruct((bsz, lt, D_MODEL), F32),
        compiler_params=_cparams(("parallel", "parallel")),
        name="moe_combine",
    )(x, z, z, gate3, mods)


def _moe(x, h2, logits, mods, n_ctx, w1, w3, w2, layer):
    bsz, lt, _ = x.shape
    src_rows, dst_rows, blk_e, n_valid, gate = _route(logits.reshape(bsz * lt, 128))
    z = _moe_experts(h2, src_rows, dst_rows, blk_e, n_valid, w1, w3, w2, layer)
    return _moe_combine(x, z, gate, mods, n_ctx)


def _od_proj_kernel(x_ref, mod_ref, w_ref, gm_ref, qn_ref, kn_ref, q_ref, k_ref, v_ref, zh_ref):
    tm = x_ref.shape[0]
    is_ctx = pl.program_id(1) * tm + lax.broadcasted_iota(jnp.int32, (tm, 1), 0) < TB
    shift = jnp.where(is_ctx, mod_ref[0, 0:1, :], mod_ref[1, 0:1, :])
    scale = jnp.where(is_ctx, mod_ref[0, 1:2, :], mod_ref[1, 1:2, :])
    h = _rms_rows(x_ref[...]) * (1.0 + scale) + shift
    z = jnp.dot(h.astype(BF16), w_ref[...], preferred_element_type=F32)

    def head_norm(t, gain):
        sq_hi, sq_lo = _split_bf16(t * t)
        gm = gm_ref[...].astype(BF16)
        ms = jnp.dot(sq_hi, gm, preferred_element_type=F32) + jnp.dot(sq_lo, gm, preferred_element_type=F32)
        return t * lax.rsqrt(ms + EPS) * gain

    q_ref[...] = (head_norm(z[:, :NA_W], qn_ref[...]) * (NA_DH ** -0.5)).astype(BF16)
    k_ref[...] = head_norm(z[:, NA_W:2 * NA_W], kn_ref[...]).astype(BF16)
    v_ref[...] = z[:, 2 * NA_W:3 * NA_W].astype(BF16)
    zh_ref[...] = z[:, 3 * NA_W:].astype(BF16)


def _od_proj(xcat, mods, w, gm, qn, kn):
    bsz, lt, _ = xcat.shape
    tm = max(d for d in range(TB, PROJ_TILE + 1, TB) if lt % d == 0)
    nblk = lt // tm
    tok = lambda n: pl.BlockSpec((None, tm, n), lambda b, i: (b, i, 0))
    const = lambda shp: pl.BlockSpec(shp, lambda b, i: tuple(0 for _ in shp))
    return pl.pallas_call(
        _od_proj_kernel,
        grid=(bsz, nblk),
        in_specs=[tok(D_MODEL), pl.BlockSpec((None, 2, 6, D_MODEL), lambda b, i: (b, 0, 0, 0)),
                  const((D_MODEL, 3 * NA_W + 3 * HY_W)), const((NA_W, NA_W)), const((1, NA_W)), const((1, NA_W))],
        out_specs=[tok(NA_W), tok(NA_W), tok(NA_W), tok(3 * HY_W)],
        out_shape=[jax.ShapeDtypeStruct((bsz, lt, NA_W), BF16), jax.ShapeDtypeStruct((bsz, lt, NA_W), BF16),
                   jax.ShapeDtypeStruct((bsz, lt, NA_W), BF16), jax.ShapeDtypeStruct((bsz, lt, 3 * HY_W), BF16)],
        compiler_params=_cparams(("parallel", "parallel")),
        name="od_proj",
    )(xcat, mods, w, gm, qn, kn)


def _na_kernel(q_ref, k_ref, v_ref, t2_ref, o_ref):
    for j in range(NA_ROWS):
        _na_one_row(q_ref, k_ref, v_ref, t2_ref, o_ref, j)


def _na_one_row(q_ref, k_ref, v_ref, t2_ref, o_ref, j):
    r = pl.program_id(1) * NA_ROWS + j
    n_rows = pl.num_programs(1) * NA_ROWS
    r0 = jnp.clip(r - WIN_R // 2, 0, n_rows - WIN_R)
    off = r0 - r + WIN_R - 1
    base = pl.multiple_of(TB + r0 * GRID_W, GRID_W)
    nloc = WIN_R * GRID_W
    q = q_ref[j * GRID_W:(j + 1) * GRID_W, :]
    hg = NA_HG
    gw = hg * NA_DH
    lane_head = lax.broadcasted_iota(jnp.int32, (GRID_W, gw), 1) // NA_DH
    nt = (((1,), (1,)), ((), ()))
    outs = []
    for grp in range(NA_HEADS // hg):
        cs = slice(gw * grp, gw * (grp + 1))
        q2 = q[:, cs]
        q4 = jnp.concatenate([jnp.where(lane_head == h, q2, jnp.zeros_like(q2)) for h in range(hg)], axis=0)
        kw, vw = k_ref[pl.ds(base, nloc), cs], v_ref[pl.ds(base, nloc), cs]
        kc, vc = k_ref[0:TB, cs], v_ref[0:TB, cs]
        bias = jnp.concatenate(
            [jnp.concatenate([t2_ref[hg * grp + h, off + 2 * m] for m in range(WIN_R // 2)], axis=1)
             for h in range(hg)], axis=0)
        s_loc = lax.dot_general(q4, kw, nt, preferred_element_type=F32) + bias
        s_ctx = lax.dot_general(q4, kc, nt, preferred_element_type=F32)
        m = jnp.maximum(jnp.max(s_loc, axis=1, keepdims=True), jnp.max(s_ctx, axis=1, keepdims=True))
        p_loc, p_ctx = jnp.exp(s_loc - m), jnp.exp(s_ctx - m)
        den = jnp.sum(p_loc, axis=1, keepdims=True) + jnp.sum(p_ctx, axis=1, keepdims=True)
        o4 = (jnp.dot(p_loc.astype(BF16), vw, preferred_element_type=F32)
              + jnp.dot(p_ctx.astype(BF16), vc, preferred_element_type=F32)) / den
        acc = jnp.zeros((GRID_W, gw), F32)
        for h in range(hg):
            acc = jnp.where(lane_head == h, o4[h * GRID_W:(h + 1) * GRID_W, :], acc)
        outs.append(acc)
    o_ref[j * GRID_W:(j + 1) * GRID_W, :] = jnp.concatenate(outs, axis=1)


def _na_bias_table(rpb):
    qc = np.arange(GRID_W)[:, None]
    kc = np.arange(GRID_W)[None, :]
    q_start = np.clip(qc - WIN_C // 2, 0, GRID_W - WIN_C)
    valid = (kc >= q_start) & (kc < q_start + WIN_C)
    col_idx = np.clip(kc - qc + WIN_C - 1, 0, 2 * WIN_C - 2)
    hot = np.zeros((GRID_W, GRID_W, 2 * WIN_C - 1), np.float32)
    hot[qc, kc, col_idx] = 1.0
    t = jnp.einsum('hrj,qkj->hrqk', rpb.astype(F32), hot, precision=HI)
    t = jnp.where(valid[None, None], t, NEG_INF)
    return jnp.concatenate([t[:, :-1], t[:, 1:]], axis=-1)


def _na(q, k, v, t2):
    bsz, lt, _ = q.shape
    qrows = NA_ROWS * GRID_W
    n_rows = (lt - TB) // qrows
    qoff = TB // qrows
    return pl.pallas_call(
        _na_kernel,
        grid=(bsz, n_rows),
        in_specs=[pl.BlockSpec((None, qrows, NA_W), lambda b, r: (b, r + qoff, 0)),
                  pl.BlockSpec((None, lt, NA_W), lambda b, r: (b, 0, 0)),
                  pl.BlockSpec((None, lt, NA_W), lambda b, r: (b, 0, 0)),
                  pl.BlockSpec(t2.shape, lambda b, r: (0, 0, 0, 0))],
        out_specs=pl.BlockSpec((None, qrows, NA_W), lambda b, r: (b, r, 0)),
        out_shape=jax.ShapeDtypeStruct((bsz, lt - TB, NA_W), F32),
        compiler_params=_cparams(("parallel", "arbitrary")),
        name="na_attn",
    )(q, k, v, t2)


def _hy_pre_kernel(z_ref, zp_ref, zn_ref, cw_ref, cb_ref, x0_ref, u_ref, ut_ref):
    i = pl.program_id(1)
    n = pl.num_programs(1)
    z = z_ref[...].astype(F32)
    tb = z.shape[0]
    prev_row = jnp.where(i > 0, zp_ref[HALO - 1:HALO, :].astype(F32), 0.0)
    next_row = jnp.where(i < n - 1, zn_ref[0:1, :].astype(F32), 0.0)
    rowid = lax.broadcasted_iota(jnp.int32, z.shape, 0)
    zm = jnp.where(rowid == 0, prev_row, pltpu.roll(z, 1, 0))
    zp = jnp.where(rowid == tb - 1, next_row, pltpu.roll(z, tb - 1, 0))
    zc = cb_ref[...] + cw_ref[0:1, :] * zm
    zc = zc + cw_ref[1:2, :] * z
    zc = zc + cw_ref[2:3, :] * zp
    x0_ref[...] = zc[:, :HY_W].astype(BF16)
    u = zc[:, HY_W:2 * HY_W] * zc[:, 2 * HY_W:]
    u_ref[...] = u.astype(BF16)
    for j in range(tb // FFT_N1):
        ut_ref[j] = u[j * FFT_N1:(j + 1) * FFT_N1, :].T.astype(BF16)


def _hy_pre(zh, cw, cb):
    bsz, lt, _ = zh.shape
    l = lt - TB
    nblk = l // TB
    h8 = TB // HALO
    return pl.pallas_call(
        _hy_pre_kernel,
        grid=(bsz, nblk),
        in_specs=[pl.BlockSpec((None, TB, 3 * HY_W), lambda b, i: (b, i + 1, 0)),
                  pl.BlockSpec((None, HALO, 3 * HY_W), lambda b, i: (b, (i + 1) * h8 - 1, 0)),
                  pl.BlockSpec((None, HALO, 3 * HY_W), lambda b, i: (b, jnp.minimum((i + 2) * h8, lt // HALO - 1), 0)),
                  pl.BlockSpec((HY_SHORT, 3 * HY_W), lambda b, i: (0, 0)),
                  pl.BlockSpec((1, 3 * HY_W), lambda b, i: (0, 0))],
        out_specs=[pl.BlockSpec((None, TB, HY_W), lambda b, i: (b, i, 0)),
                   pl.BlockSpec((None, TB, HY_W), lambda b, i: (b, i, 0)),
                   pl.BlockSpec((None, TB // FFT_N1, HY_W, FFT_N1), lambda b, i: (b, i, 0, 0))],
        out_shape=[jax.ShapeDtypeStruct((bsz, l, HY_W), BF16), jax.ShapeDtypeStruct((bsz, l, HY_W), BF16),
                   jax.ShapeDtypeStruct((bsz, l // FFT_N1, HY_W, FFT_N1), BF16)],
        compiler_params=_cparams(("parallel", "parallel")),
        name="hy_pre",
    )(zh, zh, zh, cw, cb)


def _fft_consts(n1_in):
    n = FFT_N1
    idx = np.arange(n)
    ang1 = 2.0 * np.pi * np.outer(idx, idx) / n
    c, s = np.cos(ang1), np.sin(ang1)
    angt = 2.0 * np.pi * np.outer(idx, idx) / (n * n)
    tw = np.concatenate([np.cos(angt), -np.sin(angt)], axis=1)
    f3 = np.block([[c, -s], [s, c]])
    f3i = np.block([[c, s], [-s, c]])
    ch, sh = c[:, :n1_in], s[:, :n1_in]
    f1_pair = np.block([[ch, sh], [-sh, ch]])
    f1_real = np.concatenate([c, -s], axis=0)
    f1i = np.block([[ch.T, -sh.T], [sh.T, ch.T]]) / (n * n)
    return tw, f3, f3i, f1_pair, f1_real, f1i


def _fft_forward(a, tw_re, tw_im, lhs_scr, ncg):
    for cix in range(ncg):
        cs = slice(cix * FFT_N1, (cix + 1) * FFT_N1)
        are, aim = a[:FFT_N1, cs], a[FFT_N1:, cs]
        lhs_scr[cs, :FFT_N1] = are * tw_re - aim * tw_im
        lhs_scr[cs, FFT_N1:] = are * tw_im + aim * tw_re


def _hy_filt_kernel(k_ref, f1_ref, tw_ref, f3_ref, o_ref, lhs_scr):
    a = _dot_x3(f1_ref[...], k_ref[...])
    _fft_forward(a, tw_ref[:, :FFT_N1], tw_ref[:, FFT_N1:], lhs_scr, FFT_CG)
    o_ref[...] = _dot_x3(lhs_scr[...], f3_ref[...])


def _hy_fft_kernel(u_ref, kf_ref, f1_ref, tw_ref, f3_ref, f3i_ref, f1i_ref, y_ref, lhs_scr, a2_scr):
    cgl = FFT_CG * FFT_N1
    x = u_ref[...].reshape(2 * u_ref.shape[1], cgl)
    a = jnp.dot(f1_ref[...], x, preferred_element_type=F32)
    tw_re, tw_im = tw_ref[:, :FFT_N1], tw_ref[:, FFT_N1:]
    _fft_forward(a, tw_re, tw_im, lhs_scr, FFT_CG)
    y = jnp.dot(lhs_scr[...].astype(BF16), f3_ref[...], preferred_element_type=F32)
    yre, yim = y[:, :FFT_N1], y[:, FFT_N1:]
    kre, kim = kf_ref[:, :FFT_N1], kf_ref[:, FFT_N1:]
    z = jnp.concatenate([yre * kre - yim * kim, yre * kim + yim * kre], axis=1).astype(BF16)
    bp = jnp.dot(z, f3i_ref[...], preferred_element_type=F32)
    for cix in range(FFT_CG):
        cs = slice(cix * FFT_N1, (cix + 1) * FFT_N1)
        bre, bim = bp[cs, :FFT_N1], bp[cs, FFT_N1:]
        a2_scr[:FFT_N1, cs] = bre * tw_re + bim * tw_im
        a2_scr[FFT_N1:, cs] = bim * tw_re - bre * tw_im
    out = jnp.dot(f1i_ref[...], a2_scr[...].astype(BF16), preferred_element_type=F32)
    y_ref[...] = out.reshape(2, u_ref.shape[1], cgl)


def _hy_conv(ut, kfilt):
    bsz, n1h, nch, _ = ut.shape
    assert 2 * n1h == FFT_N1 and bsz % 2 == 0
    cgl = FFT_CG * FFT_N1
    ncol = nch * FFT_N1
    tw, f3, f3i, f1_pair, f1_real, f1i = _fft_consts(n1h)
    kt = kfilt.reshape(FFT_N1, FFT_N1, nch).transpose(0, 2, 1).reshape(FFT_N1, ncol)
    const2 = lambda shp: pl.BlockSpec(shp, lambda *a: (0, 0))
    kf = pl.pallas_call(
        _hy_filt_kernel,
        grid=(nch // FFT_CG,),
        in_specs=[pl.BlockSpec((FFT_N1, cgl), lambda j: (0, j)), const2((2 * FFT_N1, FFT_N1)),
                  const2((FFT_N1, 2 * FFT_N1)), const2((2 * FFT_N1, 2 * FFT_N1))],
        out_specs=pl.BlockSpec((cgl, 2 * FFT_N1), lambda j: (j, 0)),
        out_shape=jax.ShapeDtypeStruct((ncol, 2 * FFT_N1), F32),
        scratch_shapes=[pltpu.VMEM((cgl, 2 * FFT_N1), F32)],
        compiler_params=_cparams(("parallel",)),
        name="hy_filter_dft",
    )(kt, jnp.asarray(f1_real, F32), jnp.asarray(tw, F32), jnp.asarray(f3, F32))
    u2 = ut.reshape(bsz, n1h, ncol)
    y = pl.pallas_call(
        _hy_fft_kernel,
        grid=(bsz // 2, nch // FFT_CG),
        in_specs=[pl.BlockSpec((2, n1h, cgl), lambda p, j: (p, 0, j)),
                  pl.BlockSpec((cgl, 2 * FFT_N1), lambda p, j: (j, 0)),
                  const2((2 * FFT_N1, FFT_N1)), const2((FFT_N1, 2 * FFT_N1)),
                  const2((2 * FFT_N1, 2 * FFT_N1)), const2((2 * FFT_N1, 2 * FFT_N1)), const2((FFT_N1, 2 * FFT_N1))],
        out_specs=pl.BlockSpec((2, n1h, cgl), lambda p, j: (p, 0, j)),
        out_shape=jax.ShapeDtypeStruct((bsz, n1h, ncol), F32),
        scratch_shapes=[pltpu.VMEM((cgl, 2 * FFT_N1), F32), pltpu.VMEM((2 * FFT_N1, cgl), F32)],
        compiler_params=_cparams(("parallel", "parallel")),
        name="hy_fft_conv",
    )(u2, kf, jnp.asarray(f1_pair, F32).astype(BF16), jnp.asarray(tw, F32), jnp.asarray(f3, F32).astype(BF16),
      jnp.asarray(f3i, F32).astype(BF16), jnp.asarray(f1i, F32).astype(BF16))
    return y.reshape(bsz, n1h, nch, FFT_N1)


def _hy_filter_kernel(f_ref, w1_ref, b1_ref, w2_ref, b2_ref, w3_ref, b3_ref, fr_ref, w4_ref, dl_ref, o_ref):
    f = f_ref[...]
    fr = fr_ref[...]
    h = jnp.sin(fr * (_dot_x3(f, w1_ref[...]) + b1_ref[...]))
    h = jnp.sin(fr * (_dot_x3(h, w2_ref[...]) + b2_ref[...]))
    h = jnp.sin(fr * (_dot_x3(h, w3_ref[...]) + b3_ref[...]))
    out = _dot_x3(h, w4_ref[...])
    tm = f.shape[0]
    row = pl.program_id(0) * tm + lax.broadcasted_iota(jnp.int32, (tm, 1), 0)
    o_ref[0] = out[:, :HY_W] * jnp.exp(-f[:, 0:1] * dl_ref[...])
    o_ref[1] = jnp.where(row == 0, 0.0, out[:, HY_W:] * jnp.exp(-f[:, 128:129] * dl_ref[...]))


def _hy_filter(seqlen, fw1, fb1, fw2, fb2, fw3, fb3, freq, fw4):
    t = jnp.linspace(0.0, 1.0, seqlen, dtype=F32)[:, None]
    bands = (HY_EMB - 1) // 2
    w = 2.0 * math.pi * jnp.arange(seqlen, dtype=F32)[:, None] / seqlen
    f = jnp.linspace(1e-4, bands - 1, bands, dtype=F32)[None, :]
    feat = jnp.concatenate([t, jnp.cos(f * w), -jnp.sin(f * w)], axis=-1)
    lane_pad = lambda a: jnp.pad(a, ((0, 0), (0, 128 - HY_EMB)))
    feat_b = jnp.concatenate([feat[:1], feat[:0:-1]], axis=0)
    feat2 = jnp.concatenate([lane_pad(feat), lane_pad(feat_b)], axis=1)
    nh = fw2.shape[0]
    two = lambda m: jnp.kron(jnp.eye(2, dtype=F32), m.astype(F32))
    w1 = two(jnp.pad(fw1.astype(F32), ((0, 128 - HY_EMB), (0, 0))))
    w4 = jnp.concatenate([jnp.pad(fw4[:, :HY_W].astype(F32), ((0, nh), (0, 0))),
                          jnp.pad(fw4[:, HY_W:].astype(F32), ((nh, 0), (0, 0)))], axis=1)
    deltas = jnp.abs(jnp.linspace(math.log(HY_DECAY_TARGET) / HY_DECAY_LONG_PCT,
                                  math.log(HY_DECAY_TARGET) / HY_DECAY_SHORT_PCT, HY_W, dtype=F32))
    tm = 1024
    const = lambda shp: pl.BlockSpec(shp, lambda i: (0, 0))
    row2 = lambda v: jnp.tile(v.astype(F32).reshape(1, -1), (1, 2))
    out = pl.pallas_call(
        _hy_filter_kernel,
        grid=(seqlen // tm,),
        in_specs=[pl.BlockSpec((tm, 256), lambda i: (i, 0)),
                  const((256, 2 * nh)), const((1, 2 * nh)), const((2 * nh, 2 * nh)), const((1, 2 * nh)),
                  const((2 * nh, 2 * nh)), const((1, 2 * nh)), const((1, 2 * nh)),
                  const((2 * nh, 2 * HY_W)), const((1, HY_W))],
        out_specs=pl.BlockSpec((2, tm, HY_W), lambda i: (0, i, 0)),
        out_shape=jax.ShapeDtypeStruct((2, seqlen, HY_W), F32),
        compiler_params=_cparams(("parallel",)),
        name="hy_filter_mlp",
    )(feat2, w1, row2(fb1), two(fw2), row2(fb2), two(fw3), row2(fb3), row2(freq), w4,
      deltas.reshape(1, HY_W))
    return out.reshape(2 * seqlen, HY_W)


def _od_out_kernel(na_ref, yt_ref, x0_ref, u_ref, xa_ref, xb_ref, mod_ref, hb_ref, wout_ref, wr_ref, br_ref,
                   x_out, h2_out, lg_out):
    y = jnp.concatenate([yt_ref[j].T for j in range(yt_ref.shape[0])], axis=0)
    hy = x0_ref[...].astype(F32) * (y + u_ref[...].astype(F32) * hb_ref[...])
    cat = jnp.concatenate([na_ref[...], hy], axis=1).astype(BF16)
    mix = jnp.dot(cat, wout_ref[...], preferred_element_type=F32)
    x = jnp.concatenate([xa_ref[...], xb_ref[...]], axis=0)
    _mixer_tail(x, mix, (mod_ref[2:3, :], mod_ref[3:4, :], mod_ref[4:5, :]), wr_ref, br_ref, x_out, h2_out, lg_out)


def _latent_mod_index(b, i):
    return (b, 1, 0, 0)


def _od_out(na, yt, x0, u, xcat, mods, hb, wout, wr, br):
    bsz, l, _ = na.shape
    tm = 2 * TB
    nblk = l // tm
    tok = lambda n: pl.BlockSpec((None, tm, n), lambda b, i: (b, i, 0))
    const = lambda shp: pl.BlockSpec(shp, lambda b, i: tuple(0 for _ in shp))
    return pl.pallas_call(
        _od_out_kernel,
        grid=(bsz, nblk),
        in_specs=[tok(NA_W), pl.BlockSpec((None, tm // FFT_N1, HY_W, FFT_N1), lambda b, i: (b, i, 0, 0)),
                  tok(HY_W), tok(HY_W),
                  pl.BlockSpec((None, TB, D_MODEL), lambda b, i: (b, 2 * i + 1, 0)),
                  pl.BlockSpec((None, TB, D_MODEL), lambda b, i: (b, 2 * i + 2, 0)),
                  pl.BlockSpec((None, None, 6, D_MODEL), _latent_mod_index),
                  const((1, HY_W)), const((D_MODEL, D_MODEL)), const((D_MODEL, 128)), const((1, 128))],
        out_specs=[tok(D_MODEL), pl.BlockSpec((tm * ROW_TILE, 128), lambda b, i: (b * nblk + i, 0)), tok(128)],
        out_shape=[jax.ShapeDtypeStruct((bsz, l, D_MODEL), F32), jax.ShapeDtypeStruct((bsz * l * ROW_TILE, 128), F32),
                   jax.ShapeDtypeStruct((bsz, l, 128), F32)],
        compiler_params=_cparams(("parallel", "parallel")),
        name="od_out",
    )(na, yt, x0, u, xcat, xcat, mods, hb, wout, wr, br)


def _mods(c, c_ctx, ada_w, ada_b):
    bsz = c.shape[0]
    depth, _, n = ada_w.shape
    cc = jnp.concatenate([c, c_ctx[None]], axis=0)
    a = jnp.pad(cc * _sigmoid(cc), ((0, 8 - (bsz + 1) % 8), (0, 0)))
    mp, tn = a.shape[0], 1024
    m = pl.pallas_call(
        _adaln_kernel,
        grid=(depth, n // tn),
        in_specs=[pl.BlockSpec((mp, D_MODEL), lambda l, j: (0, 0)),
                  pl.BlockSpec((None, D_MODEL, tn), lambda l, j: (l, 0, j)),
                  pl.BlockSpec((None, 1, tn), lambda l, j: (l, 0, j))],
        out_specs=pl.BlockSpec((None, mp, tn), lambda l, j: (l, 0, j)),
        out_shape=jax.ShapeDtypeStruct((depth, mp, n), F32),
        compiler_params=_cparams(("parallel", "parallel")),
        name="adaln_dense",
    )(a, ada_w, ada_b.reshape(depth, 1, n))
    mod_l = m[:, :bsz].reshape(depth, bsz, 1, 6, D_MODEL)
    mod_c = jnp.broadcast_to(m[:, bsz].reshape(depth, 1, 1, 6, D_MODEL), (depth, bsz, 1, 6, D_MODEL))
    return jnp.concatenate([mod_c, mod_l], axis=2)


def _rope_tables(seqlen):
    pos = jnp.arange(seqlen)
    half = GLA_DK // 4
    freqs = ROPE_BASE ** (-jnp.arange(half, dtype=F32) / half)
    ar = (pos // GRID_W).astype(F32)[:, None] * freqs
    ac = (pos % GRID_W).astype(F32)[:, None] * freqs
    cos = jnp.concatenate([jnp.cos(ar), jnp.cos(ar), jnp.cos(ac), jnp.cos(ac)], axis=1)
    sin = jnp.concatenate([-jnp.sin(ar), jnp.sin(ar), -jnp.sin(ac), jnp.sin(ac)], axis=1)
    cos = jnp.concatenate([jnp.ones((TB, GLA_DK), F32), cos], axis=0)
    sin = jnp.concatenate([jnp.zeros((TB, GLA_DK), F32), sin], axis=0)
    return jnp.tile(cos, (1, GLA_HEADS)), jnp.tile(sin, (1, GLA_HEADS))


def _router_weights(wg, bg, we, be):
    pad = 128 - N_GROUPS - N_EXPERTS
    wr = jnp.concatenate([wg, we, jnp.zeros((D_MODEL, pad), F32)], axis=1)
    br = jnp.concatenate([bg, be, jnp.zeros((pad,), F32)]).reshape(1, 128)
    return wr, br


def kernel(x, c, ctx, c_ctx, ada_w, ada_b, moe_wg, moe_bg, moe_we, moe_be, moe_w1, moe_w3, moe_w2, ev_w_in, ev_w_out, gla_wa2, gla_ba, gla_norm, s5_lam_re, s5_lam_im, s5_log_dt, s5_b_re, s5_b_im, s5_c_re, s5_c_im, s5_d, s5_w_glu, od_w_in, od_w_out, na_q_norm, na_k_norm, na_rpb, hy_conv_w, hy_conv_b, hy_fw1, hy_fb1, hy_fw2, hy_fb2, hy_fw3, hy_fb3, hy_freq, hy_fw4, hy_bias):
    bsz, seqlen, _ = x.shape
    assert ctx.shape[1] == TB and seqlen % TB == 0

    mods_all = _mods(c, c_ctx, ada_w, ada_b)
    mods = mods_all[0]
    w_in = ev_w_in[0]
    n_a = 2 * GLA_RANK
    a0 = 2 * GLA_QK + 2 * GLA_V
    w_ev = jnp.concatenate([w_in[:, :a0], w_in[:, a0 + n_a:], w_in[:, a0:a0 + n_a],
                            jnp.zeros((D_MODEL, 128 - n_a), F32)], axis=1).astype(BF16)
    wa = jnp.zeros((128, 2 * GLA_QK), F32)
    for d in range(2):
        wa = wa.at[d * GLA_RANK:(d + 1) * GLA_RANK, d * GLA_QK:(d + 1) * GLA_QK].set(gla_wa2[0, d])
    cos, sin = _rope_tables(seqlen)
    q, k, v, g, u, u_sw, la = _ev_proj(ctx, x, mods, w_ev, wa, gla_ba[0].reshape(1, 2 * GLA_QK), cos, sin)
    o_f, o_b = _gla(q, k, v, la)
    s5p = [t[0].astype(F32) for t in (s5_lam_re, s5_lam_im, s5_log_dt, s5_b_re, s5_b_im, s5_c_re, s5_c_im)]
    y_f = _s5_scan(u, _s5_mats(*[t[0] for t in s5p], rev=False), rev=False)
    y_b = _s5_scan(u_sw, _s5_mats(*[t[1] for t in s5p], rev=True), rev=True)
    wr, br = _router_weights(moe_wg[0], moe_bg[0], moe_we[0], moe_be[0])
    x1, h2, lg = _ev_out(o_f, o_b, g, y_f, y_b, u, ctx, x, mods,
                         jnp.tile(gla_norm[0], GLA_HEADS).reshape(1, GLA_V), s5_d[0].reshape(1, S5_W),
                         s5_w_glu[0].astype(BF16), ev_w_out[0].astype(BF16), wr, br)
    xcat = _moe(x1, h2, lg, mods, TB, moe_w1, moe_w3, moe_w2, 0)

    mods = mods_all[1]
    hd = np.arange(NA_W) // NA_DH
    gm = jnp.asarray((hd[:, None] == hd[None, :]).astype(np.float32) / NA_DH)
    qh, kh, vh, zh = _od_proj(xcat, mods, od_w_in[0].astype(BF16), gm,
                              jnp.tile(na_q_norm[0], NA_HEADS).reshape(1, NA_W),
                              jnp.tile(na_k_norm[0], NA_HEADS).reshape(1, NA_W))
    na = _na(qh, kh, vh, _na_bias_table(na_rpb[0]))
    x0, uh, ut = _hy_pre(zh, hy_conv_w[0], hy_conv_b[0].reshape(1, 3 * HY_W))
    kfilt = _hy_filter(seqlen, hy_fw1[0], hy_fb1[0], hy_fw2[0], hy_fb2[0], hy_fw3[0], hy_fb3[0],
                       hy_freq[0], hy_fw4[0])
    yt = _hy_conv(ut, kfilt)
    wr, br = _router_weights(moe_wg[1], moe_bg[1], moe_we[1], moe_be[1])
    xl, h2, lg = _od_out(na, yt, x0, uh, xcat, mods, hy_bias[0].reshape(1, HY_W),
                         od_w_out[0].astype(BF16), wr, br)
    return _moe(xl, h2, lg, mods, 0, moe_w1, moe_w3, moe_w2, 1)
```

```python
import functools
import math

import numpy as np
import jax
import jax.numpy as jnp
from jax import lax
from jax.experimental import pallas as pl
from jax.experimental.pallas import tpu as pltpu

F32, BF16 = jnp.float32, jnp.bfloat16
HI = lax.Precision.HIGHEST

D_MODEL = 1024
GRID_W = 64
EPS = 1e-6
ROPE_BASE = 10000.0
NEG_INF = -1e30
GLA_HEADS, GLA_DK, GLA_DV = 4, 64, 128
GLA_QK, GLA_V = GLA_HEADS * GLA_DK, GLA_HEADS * GLA_DV
GLA_RANK = 16
GLA_TAU = 16.0
GLA_CHUNK = 64
GLA_NB = 2
GLA_LOG_ALPHA_MIN = -1.0
S5_W, S5_H, S5_P = 512, 16, 64
S5_G = S5_W // S5_H
S5_CHUNK = 8
S5_TBLK = 1408
NA_HEADS, NA_DH = 8, 64
NA_W = NA_HEADS * NA_DH
WIN_R, WIN_C = 8, 16
NA_HG = 4
NA_ROWS = 4
HY_W = 512
HY_SHORT = 3
HY_EMB = 33
HY_DECAY_TARGET = 1e-2
HY_DECAY_SHORT_PCT = 0.3
HY_DECAY_LONG_PCT = 1.5
N_GROUPS, EXP_PER_GROUP = 4, 8
N_EXPERTS = N_GROUPS * EXP_PER_GROUP
D_EXPERT = 512
TOP_K = 2
MOE_BLOCK = 256

TB = 256
PROJ_TILE = 768
COMBINE_TILE = 1024
HALO = 16
FFT_N1 = 128
FFT_CG = 32
V7X_VMEM_LIMIT = 52 * 1024 * 1024


def _cparams(sem):
    return pltpu.CompilerParams(dimension_semantics=sem, vmem_limit_bytes=V7X_VMEM_LIMIT)


def _sigmoid(x):
    return 1.0 / (1.0 + jnp.exp(-x))


ROW_TILE = D_MODEL // 128


def _to_token_tiles(ref, val):
    n = val.shape[0]
    for j in range(ROW_TILE):
        ref[pl.ds(j, n, stride=ROW_TILE), :] = val[:, j * 128:(j + 1) * 128]


def _from_token_tiles(ref, n):
    return jnp.concatenate([ref[pl.ds(j, n, stride=ROW_TILE), :] for j in range(ROW_TILE)], axis=1)


def _split_bf16(x):
    hi = x.astype(BF16)
    return hi, (x - hi.astype(F32)).astype(BF16)


def _dot_x3(a, b):
    a_hi, a_lo = _split_bf16(a)
    b_hi, b_lo = _split_bf16(b)
    d = lambda p, q: jnp.dot(p, q, preferred_element_type=F32)
    return d(a_hi, b_hi) + d(a_lo, b_hi) + d(a_hi, b_lo)


def _rms_rows(x):
    return x * lax.rsqrt(jnp.mean(x * x, axis=-1, keepdims=True) + EPS)


def _adaln_kernel(a_ref, w_ref, b_ref, o_ref):
    o_ref[...] = jnp.dot(a_ref[...], w_ref[...], precision=HI, preferred_element_type=F32) + b_ref[...]


def _mod_index(b, i):
    return (b, jnp.minimum(i, 1), 0, 0)


def _swapped_index(nblk, b, i):
    return (b, jnp.where(i == 0, nblk - 1, i - 1), 0)


EV_NQ, EV_NK, EV_NV, EV_NG, EV_NU = 0, 256, 512, 1024, 1536
EV_NA = 2048
EV_NTOT = 2176


def _stream_tile(ctx_ref, x_ref):
    return jnp.where(pl.program_id(1) == 0, ctx_ref[...], x_ref[...])


def _stream_specs():
    return [pl.BlockSpec((None, TB, D_MODEL), lambda b, i: (b, 0, 0)),
            pl.BlockSpec((None, TB, D_MODEL), lambda b, i: (b, jnp.maximum(i - 1, 0), 0))]


def _ev_proj_kernel(ctx_ref, x_ref, mod_ref, w_ref, wa_ref, ba_ref, cos_ref, sin_ref,
                    q_ref, k_ref, v_ref, g_ref, u_ref, usw_ref, la_ref):
    x = _stream_tile(ctx_ref, x_ref)
    h = _rms_rows(x) * (1.0 + mod_ref[1:2, :]) + mod_ref[0:1, :]
    z = jnp.dot(h.astype(BF16), w_ref[...], preferred_element_type=F32)
    lane = lax.broadcasted_iota(jnp.int32, (x.shape[0], GLA_QK), 1)
    first = (lane % 32) < 16
    cos, sin = cos_ref[...], sin_ref[...]

    def rot(t):
        partner = jnp.where(first, pltpu.roll(t, GLA_QK - 16, 1), pltpu.roll(t, 16, 1))
        return t * cos + partner * sin

    q_ref[...] = rot(z[:, EV_NQ:EV_NQ + GLA_QK]) * (GLA_DK ** -0.5)
    k_ref[...] = rot(z[:, EV_NK:EV_NK + GLA_QK])
    v_ref[...] = z[:, EV_NV:EV_NV + GLA_V].astype(BF16)
    g_ref[...] = z[:, EV_NG:EV_NG + GLA_V].astype(BF16)
    u_ref[...] = z[:, EV_NU:EV_NU + S5_W]
    usw_ref[...] = z[:, EV_NU:EV_NU + S5_W]
    a = z[:, EV_NA:EV_NA + 128]
    pre = _dot_x3(a, wa_ref[...]) + ba_ref[...]
    ls = jnp.minimum(pre, 0.0) - jnp.log1p(jnp.exp(-jnp.abs(pre)))
    la_ref[...] = jnp.maximum(ls / GLA_TAU, GLA_LOG_ALPHA_MIN)


def _ev_proj(ctx, x, mods, w, wa, ba, cos, sin):
    bsz = x.shape[0]
    lt = ctx.shape[1] + x.shape[1]
    nblk = lt // TB
    tok = lambda n: pl.BlockSpec((None, TB, n), lambda b, i: (b, i, 0))
    const = lambda shp: pl.BlockSpec(shp, lambda b, i: tuple(0 for _ in shp))
    return pl.pallas_call(
        _ev_proj_kernel,
        grid=(bsz, nblk),
        in_specs=_stream_specs() + [
                  pl.BlockSpec((None, None, 6, D_MODEL), _mod_index),
                  const((D_MODEL, EV_NTOT)), const((128, 2 * GLA_QK)), const((1, 2 * GLA_QK)),
                  pl.BlockSpec((TB, GLA_QK), lambda b, i: (i, 0)),
                  pl.BlockSpec((TB, GLA_QK), lambda b, i: (i, 0))],
        out_specs=[tok(GLA_QK), tok(GLA_QK), tok(GLA_V), tok(GLA_V), tok(S5_W),
                   pl.BlockSpec((None, TB, S5_W), functools.partial(_swapped_index, nblk)), tok(2 * GLA_QK)],
        out_shape=[jax.ShapeDtypeStruct((bsz, lt, GLA_QK), F32),
                   jax.ShapeDtypeStruct((bsz, lt, GLA_QK), F32),
                   jax.ShapeDtypeStruct((bsz, lt, GLA_V), BF16),
                   jax.ShapeDtypeStruct((bsz, lt, GLA_V), BF16),
                   jax.ShapeDtypeStruct((bsz, lt, S5_W), F32),
                   jax.ShapeDtypeStruct((bsz, lt, S5_W), F32),
                   jax.ShapeDtypeStruct((bsz, lt, 2 * GLA_QK), F32)],
        compiler_params=_cparams(("parallel", "parallel")),
        name="ev_proj",
    )(ctx, x, mods, w, wa, ba, cos, sin)


def _gla_kernel(qf_ref, kf_ref, vf_ref, laf_ref, qb_ref, kb_ref, vb_ref, lab_ref,
                of_ref, ob_ref, s_scr):
    i = pl.program_id(1)

    @pl.when(i == 0)
    def _():
        s_scr[...] = jnp.zeros_like(s_scr)

    c = GLA_CHUNK
    nh = GLA_HEADS
    row = lax.broadcasted_iota(jnp.int32, (c, c), 0)
    col = lax.broadcasted_iota(jnp.int32, (c, c), 1)
    row4 = lax.broadcasted_iota(jnp.int32, (nh * c, c), 0) % c
    col4 = lax.broadcasted_iota(jnp.int32, (nh * c, c), 1)
    lane_head = lax.broadcasted_iota(jnp.int32, (c, GLA_QK), 1) // GLA_DK
    out_head = lax.broadcasted_iota(jnp.int32, (c, GLA_V), 1) // GLA_DV
    bd_mask = (lax.broadcasted_iota(jnp.int32, (GLA_V, GLA_QK), 0) // GLA_DV
               == lax.broadcasted_iota(jnp.int32, (GLA_V, GLA_QK), 1) // GLA_DK)
    nb = qf_ref.shape[0]
    nchunk = qf_ref.shape[1] // c
    nt = (((1,), (1,)), ((), ()))
    tn = (((0,), (0,)), ((), ()))

    def one_chunk(refs, o_ref, bb, d, r0):
        q_ref, k_ref, v_ref, la_ref = refs
        fwd = d == 0
        sl = pl.ds(r0, c)
        qc, kc, vc, lac = q_ref[bb, sl, :], k_ref[bb, sl, :], v_ref[bb, sl, :], la_ref[bb, sl, :]
        tri = ((row >= col) if fwd else (row <= col)).astype(BF16)
        la_hi, la_lo = _split_bf16(lac)
        b = (jnp.dot(tri, la_hi, preferred_element_type=F32)
             + jnp.dot(tri, la_lo, preferred_element_type=F32))
        b_last = b[c - 1:c, :] if fwd else b[0:1, :]
        qe = (qc * jnp.exp(b)).astype(BF16)
        ke = (kc * jnp.exp(-b)).astype(BF16)
        kd = (kc * jnp.exp(b_last - b)).astype(BF16)
        st = s_scr[bb, d]
        o = lax.dot_general(qe, st.astype(BF16), nt, preferred_element_type=F32)
        q4 = jnp.concatenate([jnp.where(lane_head == h, qe, jnp.zeros_like(qe)) for h in range(nh)], axis=0)
        att = lax.dot_general(q4, ke, nt, preferred_element_type=F32)
        att_mask = (row4 >= col4) if fwd else (row4 < col4)
        o4 = jnp.dot(jnp.where(att_mask, att, 0.0).astype(BF16), vc, preferred_element_type=F32)
        for h in range(nh):
            o = o + jnp.where(out_head == h, o4[h * c:(h + 1) * c, :], 0.0)
        o_ref[bb, sl, :] = o.astype(o_ref.dtype)
        upd_t = lax.dot_general(vc, kd, tn, preferred_element_type=F32)
        s_scr[bb, d] = st * jnp.exp(b_last) + jnp.where(bd_mask, upd_t, 0.0)

    def body(j, carry):
        for bb in range(nb):
            one_chunk((qf_ref, kf_ref, vf_ref, laf_ref), of_ref, bb, 0, pl.multiple_of(j * c, c))
            one_chunk((qb_ref, kb_ref, vb_ref, lab_ref), ob_ref, bb, 1, pl.multiple_of((nchunk - 1 - j) * c, c))
        return carry

    lax.fori_loop(0, nchunk, body, 0, unroll=True)


def _gla(q, k, v, la):
    bsz, lt, _ = q.shape
    nblk = lt // TB
    fwd_map = lambda b, i: (b, i, 0)
    bwd_blk = lambda i: jnp.where(i == 0, 0, nblk - i)
    bwd_map = lambda b, i: (b, bwd_blk(i), 0)
    bwd_map_la = lambda b, i: (b, bwd_blk(i), 1)
    nb = GLA_NB if bsz % GLA_NB == 0 else 1
    spec = lambda n, m: pl.BlockSpec((nb, TB, n), m)
    return pl.pallas_call(
        _gla_kernel,
        grid=(bsz // nb, nblk),
        in_specs=[spec(GLA_QK, fwd_map), spec(GLA_QK, fwd_map), spec(GLA_V, fwd_map), spec(GLA_QK, fwd_map),
                  spec(GLA_QK, bwd_map), spec(GLA_QK, bwd_map), spec(GLA_V, bwd_map), spec(GLA_QK, bwd_map_la)],
        out_specs=[spec(GLA_V, fwd_map), spec(GLA_V, bwd_map)],
        out_shape=[jax.ShapeDtypeStruct((bsz, lt, GLA_V), BF16), jax.ShapeDtypeStruct((bsz, lt, GLA_V), BF16)],
        scratch_shapes=[pltpu.VMEM((nb, 2, GLA_V, GLA_QK), F32)],
        compiler_params=_cparams(("parallel", "arbitrary")),
        name="gla_scan",
    )(q, k, v, la, q, k, v, la)


def _s5_kernel(u_ref, wm_ref, tm_ref, cm_ref, ar_ref, ai_ref, y_ref, w_scr, hp_scr, h_scr, *, rev):
    @pl.when(pl.program_id(1) == 0)
    def _():
        h_scr[...] = jnp.zeros_like(h_scr)

    bsz, ntok, _ = u_ref.shape
    nc = ntok // S5_CHUNK
    half = 8 * S5_P
    x = jnp.concatenate(
        [jnp.concatenate([u_ref[b, pl.ds(s, nc, stride=S5_CHUNK), :] for s in range(S5_CHUNK)], axis=1)
         for b in range(bsz)], axis=0).astype(BF16)
    w_scr[...] = jnp.dot(x, wm_ref[...], preferred_element_type=F32)
    ar, ai = ar_ref[...], ai_ref[...]

    def body(j, hs):
        c = (nc - 1 - j) if rev else j
        out = []
        for b in range(bsz):
            re, im = hs[b]
            r = b * nc + c
            hp_scr[pl.ds(r, 1), :] = jnp.concatenate([re, im], axis=1)
            w = w_scr[pl.ds(r, 1), :]
            out.append((ar * re - ai * im + w[:, :half], ar * im + ai * re + w[:, half:]))
        return tuple(out)

    hs = lax.fori_loop(0, nc, body, tuple((h_scr[b:b + 1, :half], h_scr[b:b + 1, half:]) for b in range(bsz)))
    for b in range(bsz):
        h_scr[b:b + 1, :] = jnp.concatenate(hs[b], axis=1)
    mt = 256
    ntile = x.shape[1] // mt
    cols = []
    for jt in range(ntile):
        acc = None
        for it in (range(jt, ntile) if rev else range(jt + 1)):
            part = jnp.dot(x[:, it * mt:(it + 1) * mt], tm_ref[it * mt:(it + 1) * mt, jt * mt:(jt + 1) * mt],
                           preferred_element_type=F32)
            acc = part if acc is None else acc + part
        cols.append(acc)
    y = (jnp.concatenate(cols, axis=1)
         + jnp.dot(hp_scr[...].astype(BF16), cm_ref[...], preferred_element_type=F32))
    for b in range(bsz):
        for s in range(S5_CHUNK):
            y_ref[b, pl.ds(s, nc, stride=S5_CHUNK), :] = y[b * nc:(b + 1) * nc, s * 128:(s + 1) * 128]


def _s5_mats(lam_re, lam_im, log_dt, b_re, b_im, c_re, c_im, rev):
    t16 = S5_CHUNK
    dt = jnp.exp(log_dt)[:, None]
    mag = jnp.exp(lam_re * dt)
    a_re, a_im = mag * jnp.cos(lam_im * dt), mag * jnp.sin(lam_im * dt)
    den = lam_re * lam_re + lam_im * lam_im
    nr = a_re - 1.0
    co_re = ((nr * lam_re + a_im * lam_im) / den)[..., None]
    co_im = ((a_im * lam_re - nr * lam_im) / den)[..., None]
    bb_re, bb_im = co_re * b_re - co_im * b_im, co_re * b_im + co_im * b_re
    pr, pi = [jnp.ones_like(a_re)], [jnp.zeros_like(a_im)]
    for _ in range(t16):
        pr, pi = pr + [pr[-1] * a_re - pi[-1] * a_im], pi + [pr[-1] * a_im + pi[-1] * a_re]
    pw_re, pw_im = jnp.stack(pr), jnp.stack(pi)
    g = lam_re.shape[0]
    e_re, e_im = pw_re[t16 - 1::-1][:t16], pw_im[t16 - 1::-1][:t16]
    wre = jnp.einsum('sgp,gph->gshp', e_re, bb_re) - jnp.einsum('sgp,gph->gshp', e_im, bb_im)
    wim = jnp.einsum('sgp,gph->gshp', e_re, bb_im) + jnp.einsum('sgp,gph->gshp', e_im, bb_re)
    cb_re = jnp.einsum('gkp,gph->gpkh', c_re, bb_re) - jnp.einsum('gkp,gph->gpkh', c_im, bb_im)
    cb_im = jnp.einsum('gkp,gph->gpkh', c_re, bb_im) + jnp.einsum('gkp,gph->gpkh', c_im, bb_re)
    kd = jnp.einsum('dgp,gpkh->dgkh', pw_re[:t16], cb_re) - jnp.einsum('dgp,gpkh->dgkh', pw_im[:t16], cb_im)
    lag = np.arange(t16)[None, :] - np.arange(t16)[:, None]
    toe = jnp.where((lag >= 0)[:, :, None, None, None], kd[np.clip(lag, 0, t16 - 1)], 0.0)
    toe = toe.transpose(2, 0, 4, 1, 3)
    q_re, q_im = pw_re[1:], pw_im[1:]
    ca_re = jnp.einsum('gkp,tgp->gptk', c_re, q_re) - jnp.einsum('gkp,tgp->gptk', c_im, q_im)
    ca_im = jnp.einsum('gkp,tgp->gptk', c_re, q_im) + jnp.einsum('gkp,tgp->gptk', c_im, q_re)
    if rev:
        wre, wim = wre[:, ::-1], wim[:, ::-1]
        toe = toe[:, ::-1, :, ::-1]
        ca_re, ca_im = ca_re[:, :, ::-1], ca_im[:, :, ::-1]
    nq, gl = g // 8, 8
    nd = t16 * S5_H
    tok_hot = np.zeros((gl, nd, t16 * 128), np.float32)
    st_hot = np.zeros((gl, 2 * S5_P, 2 * gl * S5_P), np.float32)
    for gi in range(gl):
        a = np.arange(nd)
        tok_hot[gi, a, (a // S5_H) * 128 + gi * S5_H + a % S5_H] = 1.0
        a = np.arange(2 * S5_P)
        st_hot[gi, a, (a // S5_P) * gl * S5_P + gi * S5_P + a % S5_P] = 1.0
    place = lambda rows, blk, cols: jnp.einsum('gar,qgab,gbc->qrc', rows, blk, cols)
    wg = jnp.concatenate([wre, wim], axis=-1).reshape(nq, gl, nd, 2 * S5_P)
    tg = toe.reshape(nq, gl, nd, nd)
    cg = jnp.concatenate([ca_re, -ca_im], axis=1).reshape(nq, gl, 2 * S5_P, nd)
    wm = place(tok_hot, wg, st_hot)
    tmat = place(tok_hot, tg, tok_hot)
    cm = place(st_hot, cg, tok_hot)
    return (wm.astype(BF16), tmat.astype(BF16), cm.astype(BF16),
            pw_re[t16].reshape(nq, 1, gl * S5_P), pw_im[t16].reshape(nq, 1, gl * S5_P))


def _s5_scan(u, mats, rev):
    bsz, lt, _ = u.shape
    tblk = max(d for d in range(64, S5_TBLK + 1, 64) if lt % d == 0)
    nblk = lt // tblk
    nq = S5_G // 8
    wm, tmat, cm, ar, ai = mats
    kw, ks = S5_CHUNK * 128, 8 * S5_P
    tmap = (lambda q, t: (0, nblk - 1 - t, q)) if rev else (lambda q, t: (0, t, q))
    per = lambda shp: pl.BlockSpec((None,) + shp, lambda q, t: (q,) + tuple(0 for _ in shp))
    rows = bsz * tblk // S5_CHUNK
    return pl.pallas_call(
        functools.partial(_s5_kernel, rev=rev),
        grid=(nq, nblk),
        in_specs=[pl.BlockSpec((bsz, tblk, 128), tmap),
                  per((kw, 2 * ks)), per((kw, kw)), per((2 * ks, kw)), per((1, ks)), per((1, ks))],
        out_specs=pl.BlockSpec((bsz, tblk, 128), tmap),
        out_shape=jax.ShapeDtypeStruct((bsz, lt, S5_W), F32),
        scratch_shapes=[pltpu.VMEM((rows, 2 * ks), F32), pltpu.VMEM((rows, 2 * ks), F32),
                        pltpu.VMEM((8, 2 * ks), F32)],
        compiler_params=_cparams(("parallel", "arbitrary")),
        name="s5_scan",
    )(u, wm, tmat, cm, ar, ai)


def _mixer_tail(x, mix, mods, wr_ref, br_ref, x_out, h2_out, lg_out):
    gate, shift, scale = mods
    xn = x + gate * mix
    x_out[...] = xn
    h2 = _rms_rows(xn) * (1.0 + scale) + shift
    _to_token_tiles(h2_out, h2)
    lg_out[...] = _route_tail(_dot_x3(h2, wr_ref[...]) + br_ref[...])


def _route_tail(lg):
    lane = lax.broadcasted_iota(jnp.int32, lg.shape, 1)
    lane_f = lane.astype(F32)
    big = jnp.float32(128.0)
    row_max = lambda t: jnp.max(t, axis=1, keepdims=True)
    first_lane = lambda hit: jnp.min(jnp.where(hit, lane_f, big), axis=1, keepdims=True)
    is_g = lane < N_GROUPS
    g_logit = jnp.where(is_g, lg, NEG_INF)
    g_max = row_max(g_logit)
    grp = first_lane(g_logit == g_max)
    g_w = 1.0 / jnp.sum(jnp.where(is_g, jnp.exp(lg - g_max), 0.0), axis=1, keepdims=True)
    lo = N_GROUPS + EXP_PER_GROUP * grp
    in_grp = jnp.logical_and(lane_f >= lo, lane_f < lo + EXP_PER_GROUP)
    e_max = row_max(jnp.where(in_grp, lg, NEG_INF))
    pe = jnp.where(in_grp, jnp.exp(lg - e_max), 0.0)
    p = pe / jnp.sum(pe, axis=1, keepdims=True)
    cand = jnp.where(in_grp, p, -1.0)
    p1 = row_max(cand)
    i1 = first_lane(cand == p1)
    cand2 = jnp.where(lane_f == i1, -1.0, cand)
    p2 = row_max(cand2)
    i2 = first_lane(cand2 == p2)
    scale = g_w / (p1 + p2)
    out = jnp.where(lane == 0, scale * p1, jnp.where(lane == 1, scale * p2,
                    jnp.where(lane == 2, i1 - N_GROUPS, jnp.where(lane == 3, i2 - N_GROUPS, 0.0))))
    return out


def _ev_out_kernel(of_ref, ob_ref, g_ref, yf_ref, u_ref, ctx_ref, mod_ref,
                   gn_ref, ds_ref, wglu_ref, wout_ref, wr_ref, br_ref, *rest, nsub):
    yb_refs, x_refs, (x_out, h2_out, lg_out) = rest[:nsub], rest[nsub:2 * nsub], rest[2 * nsub:]
    i = pl.program_id(1)
    tm = nsub * TB
    yb = jnp.concatenate([r[...] for r in yb_refs], axis=0)
    x = jnp.concatenate([jnp.where(nsub * i + k == 0, ctx_ref[...], r[...]) for k, r in enumerate(x_refs)], axis=0)
    is_ctx = i * tm + lax.broadcasted_iota(jnp.int32, (tm, 1), 0) < TB
    mod = lambda r: jnp.where(is_ctx, mod_ref[0, r:r + 1, :], mod_ref[1, r:r + 1, :])
    o = of_ref[...].astype(F32) + ob_ref[...].astype(F32)
    og = jnp.concatenate([_rms_rows(o[:, h * GLA_DV:(h + 1) * GLA_DV]) for h in range(GLA_HEADS)], axis=1)
    g = g_ref[...].astype(F32)
    og = og * gn_ref[...] * (g * _sigmoid(g))
    t = yf_ref[...] + yb + ds_ref[...] * u_ref[...]
    y = t * (0.5 * (1.0 + jnp.tanh(math.sqrt(2.0 / math.pi) * (t + 0.044715 * (t * t * t)))))
    y = y * _sigmoid(jnp.dot(y.astype(BF16), wglu_ref[...], preferred_element_type=F32))
    cat = jnp.concatenate([og, y], axis=1).astype(BF16)
    mix = jnp.dot(cat, wout_ref[...], preferred_element_type=F32)
    _mixer_tail(x, mix, (mod(2), mod(3), mod(4)), wr_ref, br_ref, x_out, h2_out, lg_out)


def _ev_out(o_f, o_b, g, y_f, y_b, u, ctx, x, mods, gn, ds, wglu, wout, wr, br):
    bsz, lt, _ = u.shape
    nstream = lt // TB
    nsub = 3 if nstream % 3 == 0 else 1
    tm = nsub * TB
    nblk = lt // tm
    tok = lambda n: pl.BlockSpec((None, tm, n), lambda b, i: (b, i, 0))
    const = lambda shp: pl.BlockSpec(shp, lambda b, i: tuple(0 for _ in shp))
    sub = lambda n, fn: [pl.BlockSpec((None, TB, n), functools.partial(fn, k)) for k in range(nsub)]
    yb_map = lambda k, b, i: _swapped_index(nstream, b, nsub * i + k)
    x_map = lambda k, b, i: (b, jnp.maximum(nsub * i + k - 1, 0), 0)
    return pl.pallas_call(
        functools.partial(_ev_out_kernel, nsub=nsub),
        grid=(bsz, nblk),
        in_specs=[tok(512), tok(512), tok(512), tok(512), tok(512),
                  pl.BlockSpec((None, TB, D_MODEL), lambda b, i: (b, 0, 0)),
                  pl.BlockSpec((None, 2, 6, D_MODEL), lambda b, i: (b, 0, 0, 0)),
                  const((1, 512)), const((1, 512)), const((512, 512)), const((D_MODEL, D_MODEL)),
                  const((D_MODEL, 128)), const((1, 128))] + sub(S5_W, yb_map) + sub(D_MODEL, x_map),
        out_specs=[tok(D_MODEL), pl.BlockSpec((tm * ROW_TILE, 128), lambda b, i: (b * nblk + i, 0)), tok(128)],
        out_shape=[jax.ShapeDtypeStruct((bsz, lt, D_MODEL), F32),
                   jax.ShapeDtypeStruct((bsz * lt * ROW_TILE, 128), F32),
                   jax.ShapeDtypeStruct((bsz, lt, 128), F32)],
        compiler_params=_cparams(("parallel", "parallel")),
        name="ev_out",
    )(o_f, o_b, g, y_f, u, ctx, mods, gn, ds, wglu, wout, wr, br, *([y_b] * nsub), *([x] * nsub))


def _route(route):
    n_tok = route.shape[0]
    gate = route[:, :TOP_K]
    eid = route[:, TOP_K:2 * TOP_K].astype(jnp.int32).reshape(-1)
    n_asg = n_tok * TOP_K
    order = jnp.argsort(eid).astype(jnp.int32)
    counts = jnp.sum((eid[:, None] == jnp.arange(N_EXPERTS)[None, :]).astype(jnp.int32), axis=0)
    padded = (counts + MOE_BLOCK - 1) // MOE_BLOCK * MOE_BLOCK
    pad_end = jnp.cumsum(padded)
    pad_start = pad_end - padded
    cnt_start = jnp.cumsum(counts) - counts
    n_blocks = -(-n_asg // MOE_BLOCK) + N_EXPERTS
    blk_start = jnp.arange(n_blocks, dtype=jnp.int32) * MOE_BLOCK
    blk_e = jnp.minimum(jnp.sum((pad_end[None, :] <= blk_start[:, None]).astype(jnp.int32), axis=1), N_EXPERTS - 1)
    pos = jnp.arange(n_blocks * MOE_BLOCK, dtype=jnp.int32)
    pos_e = jnp.repeat(blk_e, MOE_BLOCK)
    rank = pos - pad_start[pos_e]
    src = jnp.clip(cnt_start[pos_e] + rank, 0, n_asg - 1)
    slot_buf = jnp.where(rank < counts[pos_e], order[src], n_asg).astype(jnp.int32)
    n_valid = jnp.sum((slot_buf < n_asg).reshape(n_blocks, MOE_BLOCK), axis=1).astype(jnp.int32)
    tok = lax.shift_right_logical(slot_buf, 1)
    src_rows = jnp.minimum(tok, n_tok - 1) * ROW_TILE
    spare = n_asg + (pos // MOE_BLOCK % 2) * MOE_BLOCK + pos % MOE_BLOCK
    dst_rows = jnp.where(slot_buf < n_asg, (slot_buf & 1) * n_tok + tok, spare)
    lead = n_asg + jnp.arange(2 * MOE_BLOCK, dtype=jnp.int32)
    dst_rows = jnp.concatenate([lead, dst_rows]) * ROW_TILE
    return src_rows, dst_rows, blk_e, n_valid, gate.astype(F32)


def _moe_kernel(src_ref, dst_ref, blke_ref, nvalid_ref, h_hbm, w1_ref, w3_ref, w2_ref, z_hbm,
                xbuf0, xbuf1, ybuf0, ybuf1, wb1, wb3, wb2, gsem, ssem):
    i = pl.program_id(0)
    nblk = pl.num_programs(0)
    ns = ROW_TILE
    xb, yb = (xbuf0, xbuf1), (ybuf0, ybuf1)
    lead = 2 * MOE_BLOCK

    def issue_gather(blk, buf):
        base = blk * MOE_BLOCK
        for r in range(MOE_BLOCK):
            src = pl.multiple_of(src_ref[base + r], ns)
            pltpu.make_async_copy(h_hbm.at[pl.ds(src, ns)], xb[buf].at[pl.ds(r * ns, ns)],
                                  gsem.at[buf]).start(priority=r % 2)

    def issue_scatter(blk, buf):
        base = lead + blk * MOE_BLOCK
        for r in range(MOE_BLOCK):
            dst = pl.multiple_of(dst_ref[base + r], ns)
            pltpu.make_async_copy(yb[buf].at[pl.ds(r * ns, ns)], z_hbm.at[pl.ds(dst, ns)],
                                  ssem.at[buf]).start(priority=r % 2)

    def wait_gather(buf):
        pltpu.make_async_copy(h_hbm.at[pl.ds(0, MOE_BLOCK * ns)], xb[buf], gsem.at[buf]).wait()

    def wait_scatter(buf):
        pltpu.make_async_copy(yb[buf], z_hbm.at[pl.ds(0, MOE_BLOCK * ns)], ssem.at[buf]).wait()

    used = nvalid_ref[i] > 0

    @pl.when(i == 0)
    def _():
        ybuf0[...] = jnp.zeros_like(ybuf0)
        ybuf1[...] = jnp.zeros_like(ybuf1)
        issue_scatter(-2, 0)
        issue_gather(0, 0)

    def step(cur):
        oth = 1 - cur
        wait_gather(cur)
        issue_gather(jnp.minimum(i + 1, nblk - 1), oth)
        issue_scatter(i - 1, oth)
        x = _from_token_tiles(xb[cur], MOE_BLOCK).astype(BF16)

        @pl.when(jnp.logical_or(i == 0, blke_ref[i] != blke_ref[jnp.maximum(i - 1, 0)]))
        def _():
            wb1[...] = w1_ref[...].astype(BF16)
            wb3[...] = w3_ref[...].astype(BF16)
            wb2[...] = w2_ref[...].astype(BF16)

        h1 = jnp.dot(x, wb1[...], preferred_element_type=F32)
        h3 = jnp.dot(x, wb3[...], preferred_element_type=F32)
        a = (h1 * _sigmoid(h1) * h3).astype(BF16)
        y = jnp.dot(a, wb2[...], preferred_element_type=F32)
        wait_scatter(cur)
        _to_token_tiles(yb[cur], y)

    def drain(last_par):
        wait_gather(1 - last_par)
        issue_scatter(jnp.where(used, i, i - 1), last_par)
        wait_scatter(1 - last_par)
        wait_scatter(last_par)

    for par in range(2):
        @pl.when(jnp.logical_and(used, i % 2 == par))
        def _():
            step(par)

    first_unused = jnp.logical_and(jnp.logical_not(used),
                                   jnp.logical_and(i > 0, nvalid_ref[jnp.maximum(i - 1, 0)] > 0))
    last_used = jnp.logical_and(used, i == nblk - 1)
    for par in range(2):
        @pl.when(jnp.logical_or(jnp.logical_and(first_unused, (i - 1) % 2 == par),
                                jnp.logical_and(last_used, i % 2 == par)))
        def _():
            drain(par)


def _moe_experts(h2, src_rows, dst_rows, blk_e, n_valid, w1, w3, w2, layer):
    n_tok = h2.shape[0] // ROW_TILE
    n_blocks = blk_e.shape[0]
    wspec = lambda shp: pl.BlockSpec((None, None) + shp, lambda i, src, dst, blke, nvalid: (layer, blke[i], 0, 0))
    grid_spec = pltpu.PrefetchScalarGridSpec(
        num_scalar_prefetch=4,
        grid=(n_blocks,),
        in_specs=[pl.BlockSpec(memory_space=pl.ANY),
                  wspec((D_MODEL, D_EXPERT)), wspec((D_MODEL, D_EXPERT)), wspec((D_EXPERT, D_MODEL))],
        out_specs=pl.BlockSpec(memory_space=pl.ANY),
        scratch_shapes=[pltpu.VMEM((MOE_BLOCK * ROW_TILE, 128), F32), pltpu.VMEM((MOE_BLOCK * ROW_TILE, 128), F32),
                        pltpu.VMEM((MOE_BLOCK * ROW_TILE, 128), F32), pltpu.VMEM((MOE_BLOCK * ROW_TILE, 128), F32),
                        pltpu.VMEM((D_MODEL, D_EXPERT), BF16), pltpu.VMEM((D_MODEL, D_EXPERT), BF16),
                        pltpu.VMEM((D_EXPERT, D_MODEL), BF16),
                        pltpu.SemaphoreType.DMA((2,)), pltpu.SemaphoreType.DMA((2,))])
    return pl.pallas_call(
        _moe_kernel,
        grid_spec=grid_spec,
        out_shape=jax.ShapeDtypeStruct(((TOP_K * n_tok + 2 * MOE_BLOCK) * ROW_TILE, 128), F32),
        compiler_params=_cparams(("arbitrary",)),
        name="moe_experts",
    )(src_rows, dst_rows, blk_e, n_valid, h2, w1, w3, w2)


def _moe_combine_kernel(x_ref, z0_ref, z1_ref, gate_ref, mod_ref, o_ref, *, n_ctx):
    gate = gate_ref[...]
    tm = x_ref.shape[0]
    y = gate[:, 0:1] * _from_token_tiles(z0_ref, tm) + gate[:, 1:2] * _from_token_tiles(z1_ref, tm)
    g_out = mod_ref[1, 5:6, :]
    if n_ctx:
        is_ctx = pl.program_id(1) * tm + lax.broadcasted_iota(jnp.int32, (tm, 1), 0) < n_ctx
        g_out = jnp.where(is_ctx, mod_ref[0, 5:6, :], g_out)
    o_ref[...] = x_ref[...] + g_out * y


def _moe_combine(x, z, gate, mods, n_ctx):
    bsz, lt, _ = x.shape
    tm = max(d for d in range(TB, COMBINE_TILE + 1, TB) if lt % d == 0)
    nblk = lt // tm
    gate3 = gate.reshape(bsz, lt, TOP_K)
    return pl.pallas_call(
        functools.partial(_moe_combine_kernel, n_ctx=n_ctx),
        grid=(bsz, nblk),
        in_specs=[pl.BlockSpec((None, tm, D_MODEL), lambda b, i: (b, i, 0)),
                  pl.BlockSpec((tm * ROW_TILE, 128), lambda b, i: (b * nblk + i, 0)),
                  pl.BlockSpec((tm * ROW_TILE, 128), lambda b, i: ((bsz + b) * nblk + i, 0)),
                  pl.BlockSpec((None, tm, TOP_K), lambda b, i: (b, i, 0)),
                  pl.BlockSpec((None, 2, 6, D_MODEL), lambda b, i: (b, 0, 0, 0))],
        out_specs=pl.BlockSpec((None, tm, D_MODEL), lambda b, i: (b, i, 0)),
        out_shape=jax.ShapeDtypeStruct((bsz, lt, D_MODEL), F32),
        compiler_params=_cparams(("parallel", "parallel")),
        name="moe_combine",
    )(x, z, z, gate3, mods)


def _moe(x, h2, logits, mods, n_ctx, w1, w3, w2, layer):
    bsz, lt, _ = x.shape
    src_rows, dst_rows, blk_e, n_valid, gate = _route(logits.reshape(bsz * lt, 128))
    z = _moe_experts(h2, src_rows, dst_rows, blk_e, n_valid, w1, w3, w2, layer)
    return _moe_combine(x, z, gate, mods, n_ctx)


def _od_proj_kernel(x_ref, mod_ref, w_ref, gm_ref, qn_ref, kn_ref, q_ref, k_ref, v_ref, zh_ref):
    tm = x_ref.shape[0]
    is_ctx = pl.program_id(1) * tm + lax.broadcasted_iota(jnp.int32, (tm, 1), 0) < TB
    shift = jnp.where(is_ctx, mod_ref[0, 0:1, :], mod_ref[1, 0:1, :])
    scale = jnp.where(is_ctx, mod_ref[0, 1:2, :], mod_ref[1, 1:2, :])
    h = _rms_rows(x_ref[...]) * (1.0 + scale) + shift
    z = jnp.dot(h.astype(BF16), w_ref[...], preferred_element_type=F32)

    def head_norm(t, gain):
        sq_hi, sq_lo = _split_bf16(t * t)
        gm = gm_ref[...].astype(BF16)
        ms = jnp.dot(sq_hi, gm, preferred_element_type=F32) + jnp.dot(sq_lo, gm, preferred_element_type=F32)
        return t * lax.rsqrt(ms + EPS) * gain

    q_ref[...] = (head_norm(z[:, :NA_W], qn_ref[...]) * (NA_DH ** -0.5)).astype(BF16)
    k_ref[...] = head_norm(z[:, NA_W:2 * NA_W], kn_ref[...]).astype(BF16)
    v_ref[...] = z[:, 2 * NA_W:3 * NA_W].astype(BF16)
    zh_ref[...] = z[:, 3 * NA_W:].astype(BF16)


def _od_proj(xcat, mods, w, gm, qn, kn):
    bsz, lt, _ = xcat.shape
    tm = max(d for d in range(TB, PROJ_TILE + 1, TB) if lt % d == 0)
    nblk = lt // tm
    tok = lambda n: pl.BlockSpec((None, tm, n), lambda b, i: (b, i, 0))
    const = lambda shp: pl.BlockSpec(shp, lambda b, i: tuple(0 for _ in shp))
    return pl.pallas_call(
        _od_proj_kernel,
        grid=(bsz, nblk),
        in_specs=[tok(D_MODEL), pl.BlockSpec((None, 2, 6, D_MODEL), lambda b, i: (b, 0, 0, 0)),
                  const((D_MODEL, 3 * NA_W + 3 * HY_W)), const((NA_W, NA_W)), const((1, NA_W)), const((1, NA_W))],
        out_specs=[tok(NA_W), tok(NA_W), tok(NA_W), tok(3 * HY_W)],
        out_shape=[jax.ShapeDtypeStruct((bsz, lt, NA_W), BF16), jax.ShapeDtypeStruct((bsz, lt, NA_W), BF16),
                   jax.ShapeDtypeStruct((bsz, lt, NA_W), BF16), jax.ShapeDtypeStruct((bsz, lt, 3 * HY_W), BF16)],
        compiler_params=_cparams(("parallel", "parallel")),
        name="od_proj",
    )(xcat, mods, w, gm, qn, kn)


def _na_kernel(q_ref, k_ref, v_ref, t2_ref, o_ref):
    for j in range(NA_ROWS):
        _na_one_row(q_ref, k_ref, v_ref, t2_ref, o_ref, j)


def _na_one_row(q_ref, k_ref, v_ref, t2_ref, o_ref, j):
    r = pl.program_id(1) * NA_ROWS + j
    n_rows = pl.num_programs(1) * NA_ROWS
    r0 = jnp.clip(r - WIN_R // 2, 0, n_rows - WIN_R)
    off = r0 - r + WIN_R - 1
    base = pl.multiple_of(TB + r0 * GRID_W, GRID_W)
    nloc = WIN_R * GRID_W
    q = q_ref[j * GRID_W:(j + 1) * GRID_W, :]
    hg = NA_HG
    gw = hg * NA_DH
    lane_head = lax.broadcasted_iota(jnp.int32, (GRID_W, gw), 1) // NA_DH
    nt = (((1,), (1,)), ((), ()))
    outs = []
    for grp in range(NA_HEADS // hg):
        cs = slice(gw * grp, gw * (grp + 1))
        q2 = q[:, cs]
        q4 = jnp.concatenate([jnp.where(lane_head == h, q2, jnp.zeros_like(q2)) for h in range(hg)], axis=0)
        kw, vw = k_ref[pl.ds(base, nloc), cs], v_ref[pl.ds(base, nloc), cs]
        kc, vc = k_ref[0:TB, cs], v_ref[0:TB, cs]
        bias = jnp.concatenate(
            [jnp.concatenate([t2_ref[hg * grp + h, off + 2 * m] for m in range(WIN_R // 2)], axis=1)
             for h in range(hg)], axis=0)
        s_loc = lax.dot_general(q4, kw, nt, preferred_element_type=F32) + bias
        s_ctx = lax.dot_general(q4, kc, nt, preferred_element_type=F32)
        m = jnp.maximum(jnp.max(s_loc, axis=1, keepdims=True), jnp.max(s_ctx, axis=1, keepdims=True))
        p_loc, p_ctx = jnp.exp(s_loc - m), jnp.exp(s_ctx - m)
        den = jnp.sum(p_loc, axis=1, keepdims=True) + jnp.sum(p_ctx, axis=1, keepdims=True)
        o4 = (jnp.dot(p_loc.astype(BF16), vw, preferred_element_type=F32)
              + jnp.dot(p_ctx.astype(BF16), vc, preferred_element_type=F32)) / den
        acc = jnp.zeros((GRID_W, gw), F32)
        for h in range(hg):
            acc = jnp.where(lane_head == h, o4[h * GRID_W:(h + 1) * GRID_W, :], acc)
        outs.append(acc)
    o_ref[j * GRID_W:(j + 1) * GRID_W, :] = jnp.concatenate(outs, axis=1)


def _na_bias_table(rpb):
    qc = np.arange(GRID_W)[:, None]
    kc = np.arange(GRID_W)[None, :]
    q_start = np.clip(qc - WIN_C // 2, 0, GRID_W - WIN_C)
    valid = (kc >= q_start) & (kc < q_start + WIN_C)
    col_idx = np.clip(kc - qc + WIN_C - 1, 0, 2 * WIN_C - 2)
    hot = np.zeros((GRID_W, GRID_W, 2 * WIN_C - 1), np.float32)
    hot[qc, kc, col_idx] = 1.0
    t = jnp.einsum('hrj,qkj->hrqk', rpb.astype(F32), hot, precision=HI)
    t = jnp.where(valid[None, None], t, NEG_INF)
    return jnp.concatenate([t[:, :-1], t[:, 1:]], axis=-1)


def _na(q, k, v, t2):
    bsz, lt, _ = q.shape
    qrows = NA_ROWS * GRID_W
    n_rows = (lt - TB) // qrows
    qoff = TB // qrows
    return pl.pallas_call(
        _na_kernel,
        grid=(bsz, n_rows),
        in_specs=[pl.BlockSpec((None, qrows, NA_W), lambda b, r: (b, r + qoff, 0)),
                  pl.BlockSpec((None, lt, NA_W), lambda b, r: (b, 0, 0)),
                  pl.BlockSpec((None, lt, NA_W), lambda b, r: (b, 0, 0)),
                  pl.BlockSpec(t2.shape, lambda b, r: (0, 0, 0, 0))],
        out_specs=pl.BlockSpec((None, qrows, NA_W), lambda b, r: (b, r, 0)),
        out_shape=jax.ShapeDtypeStruct((bsz, lt - TB, NA_W), F32),
        compiler_params=_cparams(("parallel", "arbitrary")),
        name="na_attn",
    )(q, k, v, t2)


def _hy_pre_kernel(za_ref, zb_ref, zp_ref, zn_ref, cw_ref, cb_ref, x0_ref, u_ref, ut_ref):
    i = pl.program_id(1)
    n = pl.num_programs(1)
    z = jnp.concatenate([za_ref[...], zb_ref[...]], axis=0).astype(F32)
    tb = z.shape[0]
    prev_row = jnp.where(i > 0, zp_ref[HALO - 1:HALO, :].astype(F32), 0.0)
    next_row = jnp.where(i < n - 1, zn_ref[0:1, :].astype(F32), 0.0)
    rowid = lax.broadcasted_iota(jnp.int32, z.shape, 0)
    zm = jnp.where(rowid == 0, prev_row, pltpu.roll(z, 1, 0))
    zp = jnp.where(rowid == tb - 1, next_row, pltpu.roll(z, tb - 1, 0))
    zc = cb_ref[...] + cw_ref[0:1, :] * zm
    zc = zc + cw_ref[1:2, :] * z
    zc = zc + cw_ref[2:3, :] * zp
    x0_ref[...] = zc[:, :HY_W].astype(BF16)
    u = zc[:, HY_W:2 * HY_W] * zc[:, 2 * HY_W:]
    u_ref[...] = u.astype(BF16)
    for j in range(tb // FFT_N1):
        ut_ref[j] = u[j * FFT_N1:(j + 1) * FFT_N1, :].T.astype(BF16)


def _hy_pre(zh, cw, cb):
    bsz, lt, _ = zh.shape
    l = lt - TB
    tm = 2 * TB
    nblk = l // tm
    hpt = TB // HALO
    return pl.pallas_call(
        _hy_pre_kernel,
        grid=(bsz, nblk),
        in_specs=[pl.BlockSpec((None, TB, 3 * HY_W), lambda b, i: (b, 2 * i + 1, 0)),
                  pl.BlockSpec((None, TB, 3 * HY_W), lambda b, i: (b, 2 * i + 2, 0)),
                  pl.BlockSpec((None, HALO, 3 * HY_W), lambda b, i: (b, (2 * i + 1) * hpt - 1, 0)),
                  pl.BlockSpec((None, HALO, 3 * HY_W),
                               lambda b, i: (b, jnp.minimum((2 * i + 3) * hpt, lt // HALO - 1), 0)),
                  pl.BlockSpec((HY_SHORT, 3 * HY_W), lambda b, i: (0, 0)),
                  pl.BlockSpec((1, 3 * HY_W), lambda b, i: (0, 0))],
        out_specs=[pl.BlockSpec((None, tm, HY_W), lambda b, i: (b, i, 0)),
                   pl.BlockSpec((None, tm, HY_W), lambda b, i: (b, i, 0)),
                   pl.BlockSpec((None, tm // FFT_N1, HY_W, FFT_N1), lambda b, i: (b, i, 0, 0))],
        out_shape=[jax.ShapeDtypeStruct((bsz, l, HY_W), BF16), jax.ShapeDtypeStruct((bsz, l, HY_W), BF16),
                   jax.ShapeDtypeStruct((bsz, l // FFT_N1, HY_W, FFT_N1), BF16)],
        compiler_params=_cparams(("parallel", "parallel")),
        name="hy_pre",
    )(zh, zh, zh, zh, cw, cb)


def _fft_consts(n1_in):
    n = FFT_N1
    idx = np.arange(n)
    ang1 = 2.0 * np.pi * np.outer(idx, idx) / n
    c, s = np.cos(ang1), np.sin(ang1)
    angt = 2.0 * np.pi * np.outer(idx, idx) / (n * n)
    tw = np.concatenate([np.cos(angt), -np.sin(angt)], axis=1)
    f3 = np.block([[c, -s], [s, c]])
    f3i = np.block([[c, s], [-s, c]])
    ch, sh = c[:, :n1_in], s[:, :n1_in]
    f1_pair = np.block([[ch, sh], [-sh, ch]])
    f1_real = np.concatenate([c, -s], axis=0)
    f1i = np.block([[ch.T, -sh.T], [sh.T, ch.T]]) / (n * n)
    return tw, f3, f3i, f1_pair, f1_real, f1i


def _fft_forward(a, tw_re, tw_im, lhs_scr, ncg):
    for cix in range(ncg):
        cs = slice(cix * FFT_N1, (cix + 1) * FFT_N1)
        are, aim = a[:FFT_N1, cs], a[FFT_N1:, cs]
        lhs_scr[cs, :FFT_N1] = are * tw_re - aim * tw_im
        lhs_scr[cs, FFT_N1:] = are * tw_im + aim * tw_re


def _hy_filt_kernel(k_ref, f1_ref, tw_ref, f3_ref, o_ref, lhs_scr):
    a = _dot_x3(f1_ref[...], k_ref[...])
    _fft_forward(a, tw_ref[:, :FFT_N1], tw_ref[:, FFT_N1:], lhs_scr, FFT_CG)
    o_ref[...] = _dot_x3(lhs_scr[...], f3_ref[...])


def _hy_fft_kernel(u_ref, kf_ref, f1_ref, tw_ref, f3_ref, f3i_ref, f1i_ref, y_ref, lhs_scr, a2_scr):
    cgl = FFT_CG * FFT_N1
    x = u_ref[...].reshape(2 * u_ref.shape[1], cgl)
    a = jnp.dot(f1_ref[...], x, preferred_element_type=F32)
    tw_re, tw_im = tw_ref[:, :FFT_N1], tw_ref[:, FFT_N1:]
    _fft_forward(a, tw_re, tw_im, lhs_scr, FFT_CG)
    y = jnp.dot(lhs_scr[...].astype(BF16), f3_ref[...], preferred_element_type=F32)
    yre, yim = y[:, :FFT_N1], y[:, FFT_N1:]
    kre, kim = kf_ref[:, :FFT_N1], kf_ref[:, FFT_N1:]
    z = jnp.concatenate([yre * kre - yim * kim, yre * kim + yim * kre], axis=1).astype(BF16)
    bp = jnp.dot(z, f3i_ref[...], preferred_element_type=F32)
    for cix in range(FFT_CG):
        cs = slice(cix * FFT_N1, (cix + 1) * FFT_N1)
        bre, bim = bp[cs, :FFT_N1], bp[cs, FFT_N1:]
        a2_scr[:FFT_N1, cs] = bre * tw_re + bim * tw_im
        a2_scr[FFT_N1:, cs] = bim * tw_re - bre * tw_im
    out = jnp.dot(f1i_ref[...], a2_scr[...].astype(BF16), preferred_element_type=F32)
    y_ref[...] = out.reshape(2, u_ref.shape[1], cgl)


def _hy_conv(ut, kfilt):
    bsz, n1h, nch, _ = ut.shape
    assert 2 * n1h == FFT_N1 and bsz % 2 == 0
    cgl = FFT_CG * FFT_N1
    ncol = nch * FFT_N1
    tw, f3, f3i, f1_pair, f1_real, f1i = _fft_consts(n1h)
    kt = kfilt.reshape(FFT_N1, FFT_N1, nch).transpose(0, 2, 1).reshape(FFT_N1, ncol)
    const2 = lambda shp: pl.BlockSpec(shp, lambda *a: (0, 0))
    kf = pl.pallas_call(
        _hy_filt_kernel,
        grid=(nch // FFT_CG,),
        in_specs=[pl.BlockSpec((FFT_N1, cgl), lambda j: (0, j)), const2((2 * FFT_N1, FFT_N1)),
                  const2((FFT_N1, 2 * FFT_N1)), const2((2 * FFT_N1, 2 * FFT_N1))],
        out_specs=pl.BlockSpec((cgl, 2 * FFT_N1), lambda j: (j, 0)),
        out_shape=jax.ShapeDtypeStruct((ncol, 2 * FFT_N1), F32),
        scratch_shapes=[pltpu.VMEM((cgl, 2 * FFT_N1), F32)],
        compiler_params=_cparams(("parallel",)),
        name="hy_filter_dft",
    )(kt, jnp.asarray(f1_real, F32), jnp.asarray(tw, F32), jnp.asarray(f3, F32))
    u2 = ut.reshape(bsz, n1h, ncol)
    y = pl.pallas_call(
        _hy_fft_kernel,
        grid=(bsz // 2, nch // FFT_CG),
        in_specs=[pl.BlockSpec((2, n1h, cgl), lambda p, j: (p, 0, j)),
                  pl.BlockSpec((cgl, 2 * FFT_N1), lambda p, j: (j, 0)),
                  const2((2 * FFT_N1, FFT_N1)), const2((FFT_N1, 2 * FFT_N1)),
                  const2((2 * FFT_N1, 2 * FFT_N1)), const2((2 * FFT_N1, 2 * FFT_N1)), const2((FFT_N1, 2 * FFT_N1))],
        out_specs=pl.BlockSpec((2, n1h, cgl), lambda p, j: (p, 0, j)),
        out_shape=jax.ShapeDtypeStruct((bsz, n1h, ncol), F32),
        scratch_shapes=[pltpu.VMEM((cgl, 2 * FFT_N1), F32), pltpu.VMEM((2 * FFT_N1, cgl), F32)],
        compiler_params=_cparams(("parallel", "parallel")),
        name="hy_fft_conv",
    )(u2, kf, jnp.asarray(f1_pair, F32).astype(BF16), jnp.asarray(tw, F32), jnp.asarray(f3, F32).astype(BF16),
      jnp.asarray(f3i, F32).astype(BF16), jnp.asarray(f1i, F32).astype(BF16))
    return y.reshape(bsz, n1h, nch, FFT_N1)


def _hy_filter_kernel(f_ref, w1_ref, b1_ref, w2_ref, b2_ref, w3_ref, b3_ref, fr_ref, w4_ref, dl_ref, o_ref):
    f = f_ref[...]
    fr = fr_ref[...]
    h = jnp.sin(fr * (_dot_x3(f, w1_ref[...]) + b1_ref[...]))
    h = jnp.sin(fr * (_dot_x3(h, w2_ref[...]) + b2_ref[...]))
    h = jnp.sin(fr * (_dot_x3(h, w3_ref[...]) + b3_ref[...]))
    out = _dot_x3(h, w4_ref[...])
    tm = f.shape[0]
    row = pl.program_id(0) * tm + lax.broadcasted_iota(jnp.int32, (tm, 1), 0)
    o_ref[0] = out[:, :HY_W] * jnp.exp(-f[:, 0:1] * dl_ref[...])
    o_ref[1] = jnp.where(row == 0, 0.0, out[:, HY_W:] * jnp.exp(-f[:, 128:129] * dl_ref[...]))


def _hy_filter(seqlen, fw1, fb1, fw2, fb2, fw3, fb3, freq, fw4):
    t = jnp.linspace(0.0, 1.0, seqlen, dtype=F32)[:, None]
    bands = (HY_EMB - 1) // 2
    w = 2.0 * math.pi * jnp.arange(seqlen, dtype=F32)[:, None] / seqlen
    f = jnp.linspace(1e-4, bands - 1, bands, dtype=F32)[None, :]
    feat = jnp.concatenate([t, jnp.cos(f * w), -jnp.sin(f * w)], axis=-1)
    lane_pad = lambda a: jnp.pad(a, ((0, 0), (0, 128 - HY_EMB)))
    feat_b = jnp.concatenate([feat[:1], feat[:0:-1]], axis=0)
    feat2 = jnp.concatenate([lane_pad(feat), lane_pad(feat_b)], axis=1)
    nh = fw2.shape[0]
    two = lambda m: jnp.kron(jnp.eye(2, dtype=F32), m.astype(F32))
    w1 = two(jnp.pad(fw1.astype(F32), ((0, 128 - HY_EMB), (0, 0))))
    w4 = jnp.concatenate([jnp.pad(fw4[:, :HY_W].astype(F32), ((0, nh), (0, 0))),
                          jnp.pad(fw4[:, HY_W:].astype(F32), ((nh, 0), (0, 0)))], axis=1)
    deltas = jnp.abs(jnp.linspace(math.log(HY_DECAY_TARGET) / HY_DECAY_LONG_PCT,
                                  math.log(HY_DECAY_TARGET) / HY_DECAY_SHORT_PCT, HY_W, dtype=F32))
    tm = 1024
    const = lambda shp: pl.BlockSpec(shp, lambda i: (0, 0))
    row2 = lambda v: jnp.tile(v.astype(F32).reshape(1, -1), (1, 2))
    out = pl.pallas_call(
        _hy_filter_kernel,
        grid=(seqlen // tm,),
        in_specs=[pl.BlockSpec((tm, 256), lambda i: (i, 0)),
                  const((256, 2 * nh)), const((1, 2 * nh)), const((2 * nh, 2 * nh)), const((1, 2 * nh)),
                  const((2 * nh, 2 * nh)), const((1, 2 * nh)), const((1, 2 * nh)),
                  const((2 * nh, 2 * HY_W)), const((1, HY_W))],
        out_specs=pl.BlockSpec((2, tm, HY_W), lambda i: (0, i, 0)),
        out_shape=jax.ShapeDtypeStruct((2, seqlen, HY_W), F32),
        compiler_params=_cparams(("parallel",)),
        name="hy_filter_mlp",
    )(feat2, w1, row2(fb1), two(fw2), row2(fb2), two(fw3), row2(fb3), row2(freq), w4,
      deltas.reshape(1, HY_W))
    return out.reshape(2 * seqlen, HY_W)


def _od_out_kernel(na_ref, yt_ref, x0_ref, u_ref, xa_ref, xb_ref, mod_ref, hb_ref, wout_ref, wr_ref, br_ref,
                   x_out, h2_out, lg_out):
    y = jnp.concatenate([yt_ref[j].T for j in range(yt_ref.shape[0])], axis=0)
    hy = x0_ref[...].astype(F32) * (y + u_ref[...].astype(F32) * hb_ref[...])
    cat = jnp.concatenate([na_ref[...], hy], axis=1).astype(BF16)
    mix = jnp.dot(cat, wout_ref[...], preferred_element_type=F32)
    x = jnp.concatenate([xa_ref[...], xb_ref[...]], axis=0)
    _mixer_tail(x, mix, (mod_ref[2:3, :], mod_ref[3:4, :], mod_ref[4:5, :]), wr_ref, br_ref, x_out, h2_out, lg_out)


def _latent_mod_index(b, i):
    return (b, 1, 0, 0)


def _od_out(na, yt, x0, u, xcat, mods, hb, wout, wr, br):
    bsz, l, _ = na.shape
    tm = 2 * TB
    nblk = l // tm
    tok = lambda n: pl.BlockSpec((None, tm, n), lambda b, i: (b, i, 0))
    const = lambda shp: pl.BlockSpec(shp, lambda b, i: tuple(0 for _ in shp))
    return pl.pallas_call(
        _od_out_kernel,
        grid=(bsz, nblk),
        in_specs=[tok(NA_W), pl.BlockSpec((None, tm // FFT_N1, HY_W, FFT_N1), lambda b, i: (b, i, 0, 0)),
                  tok(HY_W), tok(HY_W),
                  pl.BlockSpec((None, TB, D_MODEL), lambda b, i: (b, 2 * i + 1, 0)),
                  pl.BlockSpec((None, TB, D_MODEL), lambda b, i: (b, 2 * i + 2, 0)),
                  pl.BlockSpec((None, None, 6, D_MODEL), _latent_mod_index),
                  const((1, HY_W)), const((D_MODEL, D_MODEL)), const((D_MODEL, 128)), const((1, 128))],
        out_specs=[tok(D_MODEL), pl.BlockSpec((tm * ROW_TILE, 128), lambda b, i: (b * nblk + i, 0)), tok(128)],
        out_shape=[jax.ShapeDtypeStruct((bsz, l, D_MODEL), F32), jax.ShapeDtypeStruct((bsz * l * ROW_TILE, 128), F32),
                   jax.ShapeDtypeStruct((bsz, l, 128), F32)],
        compiler_params=_cparams(("parallel", "parallel")),
        name="od_out",
    )(na, yt, x0, u, xcat, xcat, mods, hb, wout, wr, br)


def _mods(c, c_ctx, ada_w, ada_b):
    bsz = c.shape[0]
    depth, _, n = ada_w.shape
    cc = jnp.concatenate([c, c_ctx[None]], axis=0)
    a = jnp.pad(cc * _sigmoid(cc), ((0, 8 - (bsz + 1) % 8), (0, 0)))
    mp, tn = a.shape[0], 1024
    m = pl.pallas_call(
        _adaln_kernel,
        grid=(depth, n // tn),
        in_specs=[pl.BlockSpec((mp, D_MODEL), lambda l, j: (0, 0)),
                  pl.BlockSpec((None, D_MODEL, tn), lambda l, j: (l, 0, j)),
                  pl.BlockSpec((None, 1, tn), lambda l, j: (l, 0, j))],
        out_specs=pl.BlockSpec((None, mp, tn), lambda l, j: (l, 0, j)),
        out_shape=jax.ShapeDtypeStruct((depth, mp, n), F32),
        compiler_params=_cparams(("parallel", "parallel")),
        name="adaln_dense",
    )(a, ada_w, ada_b.reshape(depth, 1, n))
    mod_l = m[:, :bsz].reshape(depth, bsz, 1, 6, D_MODEL)
    mod_c = jnp.broadcast_to(m[:, bsz].reshape(depth, 1, 1, 6, D_MODEL), (depth, bsz, 1, 6, D_MODEL))
    return jnp.concatenate([mod_c, mod_l], axis=2)


def _rope_tables(seqlen):
    pos = jnp.arange(seqlen)
    half = GLA_DK // 4
    freqs = ROPE_BASE ** (-jnp.arange(half, dtype=F32) / half)
    ar = (pos // GRID_W).astype(F32)[:, None] * freqs
    ac = (pos % GRID_W).astype(F32)[:, None] * freqs
    cos = jnp.concatenate([jnp.cos(ar), jnp.cos(ar), jnp.cos(ac), jnp.cos(ac)], axis=1)
    sin = jnp.concatenate([-jnp.sin(ar), jnp.sin(ar), -jnp.sin(ac), jnp.sin(ac)], axis=1)
    cos = jnp.concatenate([jnp.ones((TB, GLA_DK), F32), cos], axis=0)
    sin = jnp.concatenate([jnp.zeros((TB, GLA_DK), F32), sin], axis=0)
    return jnp.tile(cos, (1, GLA_HEADS)), jnp.tile(sin, (1, GLA_HEADS))


def _router_weights(wg, bg, we, be):
    pad = 128 - N_GROUPS - N_EXPERTS
    wr = jnp.concatenate([wg, we, jnp.zeros((D_MODEL, pad), F32)], axis=1)
    br = jnp.concatenate([bg, be, jnp.zeros((pad,), F32)]).reshape(1, 128)
    return wr, br


def kernel(x, c, ctx, c_ctx, ada_w, ada_b, moe_wg, moe_bg, moe_we, moe_be, moe_w1, moe_w3, moe_w2, ev_w_in, ev_w_out, gla_wa2, gla_ba, gla_norm, s5_lam_re, s5_lam_im, s5_log_dt, s5_b_re, s5_b_im, s5_c_re, s5_c_im, s5_d, s5_w_glu, od_w_in, od_w_out, na_q_norm, na_k_norm, na_rpb, hy_conv_w, hy_conv_b, hy_fw1, hy_fb1, hy_fw2, hy_fb2, hy_fw3, hy_fb3, hy_freq, hy_fw4, hy_bias):
    bsz, seqlen, _ = x.shape
    assert ctx.shape[1] == TB and seqlen % TB == 0

    mods_all = _mods(c, c_ctx, ada_w, ada_b)
    mods = mods_all[0]
    w_in = ev_w_in[0]
    n_a = 2 * GLA_RANK
    a0 = 2 * GLA_QK + 2 * GLA_V
    w_ev = jnp.concatenate([w_in[:, :a0], w_in[:, a0 + n_a:], w_in[:, a0:a0 + n_a],
                            jnp.zeros((D_MODEL, 128 - n_a), F32)], axis=1).astype(BF16)
    wa = jnp.zeros((128, 2 * GLA_QK), F32)
    for d in range(2):
        wa = wa.at[d * GLA_RANK:(d + 1) * GLA_RANK, d * GLA_QK:(d + 1) * GLA_QK].set(gla_wa2[0, d])
    cos, sin = _rope_tables(seqlen)
    q, k, v, g, u, u_sw, la = _ev_proj(ctx, x, mods, w_ev, wa, gla_ba[0].reshape(1, 2 * GLA_QK), cos, sin)
    o_f, o_b = _gla(q, k, v, la)
    s5p = [t[0].astype(F32) for t in (s5_lam_re, s5_lam_im, s5_log_dt, s5_b_re, s5_b_im, s5_c_re, s5_c_im)]
    y_f = _s5_scan(u, _s5_mats(*[t[0] for t in s5p], rev=False), rev=False)
    y_b = _s5_scan(u_sw, _s5_mats(*[t[1] for t in s5p], rev=True), rev=True)
    wr, br = _router_weights(moe_wg[0], moe_bg[0], moe_we[0], moe_be[0])
    x1, h2, lg = _ev_out(o_f, o_b, g, y_f, y_b, u, ctx, x, mods,
                         jnp.tile(gla_norm[0], GLA_HEADS).reshape(1, GLA_V), s5_d[0].reshape(1, S5_W),
                         s5_w_glu[0].astype(BF16), ev_w_out[0].astype(BF16), wr, br)
    xcat = _moe(x1, h2, lg, mods, TB, moe_w1, moe_w3, moe_w2, 0)

    mods = mods_all[1]
    hd = np.arange(NA_W) // NA_DH
    gm = jnp.asarray((hd[:, None] == hd[None, :]).astype(np.float32) / NA_DH)
    qh, kh, vh, zh = _od_proj(xcat, mods, od_w_in[0].astype(BF16), gm,
                              jnp.tile(na_q_norm[0], NA_HEADS).reshape(1, NA_W),
                              jnp.tile(na_k_norm[0], NA_HEADS).reshape(1, NA_W))
    na = _na(qh, kh, vh, _na_bias_table(na_rpb[0]))
    x0, uh, ut = _hy_pre(zh, hy_conv_w[0], hy_conv_b[0].reshape(1, 3 * HY_W))
    kfilt = _hy_filter(seqlen, hy_fw1[0], hy_fb1[0], hy_fw2[0], hy_fb2[0], hy_fw3[0], hy_fb3[0],
                       hy_freq[0], hy_fw4[0])
    yt = _hy_conv(ut, kfilt)
    wr, br = _router_weights(moe_wg[1], moe_bg[1], moe_we[1], moe_be[1])
    xl, h2, lg = _od_out(na, yt, x0, uh, xcat, mods, hy_bias[0].reshape(1, HY_W),
                         od_w_out[0].astype(BF16), wr, br)
    return _moe(xl, h2, lg, mods, 0, moe_w1, moe_w3, moe_w2, 1)
```

```python
import functools
import math

import numpy as np
import jax
import jax.numpy as jnp
from jax import lax
from jax.experimental import pallas as pl
from jax.experimental.pallas import tpu as pltpu

F32, BF16 = jnp.float32, jnp.bfloat16
HI = lax.Precision.HIGHEST

D_MODEL = 1024
GRID_W = 64
EPS = 1e-6
ROPE_BASE = 10000.0
NEG_INF = -1e30
GLA_HEADS, GLA_DK, GLA_DV = 4, 64, 128
GLA_QK, GLA_V = GLA_HEADS * GLA_DK, GLA_HEADS * GLA_DV
GLA_RANK = 16
GLA_TAU = 16.0
GLA_CHUNK = 64
GLA_NB = 4
GLA_LOG_ALPHA_MIN = -1.0
S5_W, S5_H, S5_P = 512, 16, 64
S5_G = S5_W // S5_H
S5_CHUNK = 8
S5_TBLK = 1408
NA_HEADS, NA_DH = 8, 64
NA_W = NA_HEADS * NA_DH
WIN_R, WIN_C = 8, 16
NA_HG = 4
NA_ROWS = 4
HY_W = 512
HY_SHORT = 3
HY_EMB = 33
HY_DECAY_TARGET = 1e-2
HY_DECAY_SHORT_PCT = 0.3
HY_DECAY_LONG_PCT = 1.5
N_GROUPS, EXP_PER_GROUP = 4, 8
N_EXPERTS = N_GROUPS * EXP_PER_GROUP
D_EXPERT = 512
TOP_K = 2
MOE_BLOCK = 256

TB = 256
PROJ_TILE = 768
COMBINE_TILE = 1024
HALO = 16
FFT_N1 = 128
FFT_CG = 32
V7X_VMEM_LIMIT = 52 * 1024 * 1024


def _cparams(sem):
    return pltpu.CompilerParams(dimension_semantics=sem, vmem_limit_bytes=V7X_VMEM_LIMIT)


def _sigmoid(x):
    return 1.0 / (1.0 + jnp.exp(-x))


ROW_TILE = D_MODEL // 128


def _to_token_tiles(ref, val):
    n = val.shape[0]
    for j in range(ROW_TILE):
        ref[pl.ds(j, n, stride=ROW_TILE), :] = val[:, j * 128:(j + 1) * 128]


def _from_token_tiles(ref, n):
    return jnp.concatenate([ref[pl.ds(j, n, stride=ROW_TILE), :] for j in range(ROW_TILE)], axis=1)


def _split_bf16(x):
    hi = x.astype(BF16)
    return hi, (x - hi.astype(F32)).astype(BF16)


def _dot_x3(a, b):
    a_hi, a_lo = _split_bf16(a)
    b_hi, b_lo = _split_bf16(b)
    d = lambda p, q: jnp.dot(p, q, preferred_element_type=F32)
    return d(a_hi, b_hi) + d(a_lo, b_hi) + d(a_hi, b_lo)


def _rms_rows(x):
    return x * lax.rsqrt(jnp.mean(x * x, axis=-1, keepdims=True) + EPS)


def _adaln_kernel(a_ref, w_ref, b_ref, o_ref):
    o_ref[...] = jnp.dot(a_ref[...], w_ref[...], precision=HI, preferred_element_type=F32) + b_ref[...]


def _mod_index(b, i):
    return (b, jnp.minimum(i, 1), 0, 0)


def _swapped_index(nblk, b, i):
    return (b, jnp.where(i == 0, nblk - 1, i - 1), 0)


EV_NQ, EV_NK, EV_NV, EV_NG, EV_NU = 0, 256, 512, 1024, 1536
EV_NA = 2048
EV_NTOT = 2176


def _stream_tile(ctx_ref, x_ref):
    return jnp.where(pl.program_id(1) == 0, ctx_ref[...], x_ref[...])


def _stream_specs():
    return [pl.BlockSpec((None, TB, D_MODEL), lambda b, i: (b, 0, 0)),
            pl.BlockSpec((None, TB, D_MODEL), lambda b, i: (b, jnp.maximum(i - 1, 0), 0))]


def _ev_proj_kernel(ctx_ref, x_ref, mod_ref, w_ref, wa_ref, ba_ref, cos_ref, sin_ref,
                    q_ref, k_ref, v_ref, g_ref, u_ref, usw_ref, la_ref):
    x = _stream_tile(ctx_ref, x_ref)
    h = _rms_rows(x) * (1.0 + mod_ref[1:2, :]) + mod_ref[0:1, :]
    z = jnp.dot(h.astype(BF16), w_ref[...], preferred_element_type=F32)
    lane = lax.broadcasted_iota(jnp.int32, (x.shape[0], GLA_QK), 1)
    first = (lane % 32) < 16
    cos, sin = cos_ref[...], sin_ref[...]

    def rot(t):
        partner = jnp.where(first, pltpu.roll(t, GLA_QK - 16, 1), pltpu.roll(t, 16, 1))
        return t * cos + partner * sin

    q_ref[...] = rot(z[:, EV_NQ:EV_NQ + GLA_QK]) * (GLA_DK ** -0.5)
    k_ref[...] = rot(z[:, EV_NK:EV_NK + GLA_QK])
    v_ref[...] = z[:, EV_NV:EV_NV + GLA_V].astype(BF16)
    g_ref[...] = z[:, EV_NG:EV_NG + GLA_V].astype(BF16)
    u_ref[...] = z[:, EV_NU:EV_NU + S5_W]
    usw_ref[...] = z[:, EV_NU:EV_NU + S5_W]
    a = z[:, EV_NA:EV_NA + 128]
    pre = _dot_x3(a, wa_ref[...]) + ba_ref[...]
    ls = jnp.minimum(pre, 0.0) - jnp.log1p(jnp.exp(-jnp.abs(pre)))
    la_ref[...] = jnp.maximum(ls / GLA_TAU, GLA_LOG_ALPHA_MIN)


def _ev_proj(ctx, x, mods, w, wa, ba, cos, sin):
    bsz = x.shape[0]
    lt = ctx.shape[1] + x.shape[1]
    nblk = lt // TB
    tok = lambda n: pl.BlockSpec((None, TB, n), lambda b, i: (b, i, 0))
    const = lambda shp: pl.BlockSpec(shp, lambda b, i: tuple(0 for _ in shp))
    return pl.pallas_call(
        _ev_proj_kernel,
        grid=(bsz, nblk),
        in_specs=_stream_specs() + [
                  pl.BlockSpec((None, None, 6, D_MODEL), _mod_index),
                  const((D_MODEL, EV_NTOT)), const((128, 2 * GLA_QK)), const((1, 2 * GLA_QK)),
                  pl.BlockSpec((TB, GLA_QK), lambda b, i: (i, 0)),
                  pl.BlockSpec((TB, GLA_QK), lambda b, i: (i, 0))],
        out_specs=[tok(GLA_QK), tok(GLA_QK), tok(GLA_V), tok(GLA_V), tok(S5_W),
                   pl.BlockSpec((None, TB, S5_W), functools.partial(_swapped_index, nblk)), tok(2 * GLA_QK)],
        out_shape=[jax.ShapeDtypeStruct((bsz, lt, GLA_QK), F32),
                   jax.ShapeDtypeStruct((bsz, lt, GLA_QK), F32),
                   jax.ShapeDtypeStruct((bsz, lt, GLA_V), BF16),
                   jax.ShapeDtypeStruct((bsz, lt, GLA_V), BF16),
                   jax.ShapeDtypeStruct((bsz, lt, S5_W), F32),
                   jax.ShapeDtypeStruct((bsz, lt, S5_W), F32),
                   jax.ShapeDtypeStruct((bsz, lt, 2 * GLA_QK), F32)],
        compiler_params=_cparams(("parallel", "parallel")),
        name="ev_proj",
    )(ctx, x, mods, w, wa, ba, cos, sin)


def _gla_kernel(qf_ref, kf_ref, vf_ref, laf_ref, qb_ref, kb_ref, vb_ref, lab_ref,
                of_ref, ob_ref, s_scr):
    i = pl.program_id(1)

    @pl.when(i == 0)
    def _():
        s_scr[...] = jnp.zeros_like(s_scr)

    c = GLA_CHUNK
    nh = GLA_HEADS
    row = lax.broadcasted_iota(jnp.int32, (c, c), 0)
    col = lax.broadcasted_iota(jnp.int32, (c, c), 1)
    row4 = lax.broadcasted_iota(jnp.int32, (nh * c, c), 0) % c
    col4 = lax.broadcasted_iota(jnp.int32, (nh * c, c), 1)
    lane_head = lax.broadcasted_iota(jnp.int32, (c, GLA_QK), 1) // GLA_DK
    out_head = lax.broadcasted_iota(jnp.int32, (c, GLA_V), 1) // GLA_DV
    bd_mask = (lax.broadcasted_iota(jnp.int32, (GLA_V, GLA_QK), 0) // GLA_DV
               == lax.broadcasted_iota(jnp.int32, (GLA_V, GLA_QK), 1) // GLA_DK)
    nb = qf_ref.shape[0]
    nchunk = qf_ref.shape[1] // c
    nt = (((1,), (1,)), ((), ()))
    tn = (((0,), (0,)), ((), ()))

    def one_chunk(refs, o_ref, bb, d, r0):
        q_ref, k_ref, v_ref, la_ref = refs
        fwd = d == 0
        sl = pl.ds(r0, c)
        qc, kc, vc, lac = q_ref[bb, sl, :], k_ref[bb, sl, :], v_ref[bb, sl, :], la_ref[bb, sl, :]
        tri = ((row >= col) if fwd else (row <= col)).astype(BF16)
        la_hi, la_lo = _split_bf16(lac)
        b = (jnp.dot(tri, la_hi, preferred_element_type=F32)
             + jnp.dot(tri, la_lo, preferred_element_type=F32))
        b_last = b[c - 1:c, :] if fwd else b[0:1, :]
        qe = (qc * jnp.exp(b)).astype(BF16)
        ke = (kc * jnp.exp(-b)).astype(BF16)
        kd = (kc * jnp.exp(b_last - b)).astype(BF16)
        st = s_scr[bb, d]
        o = lax.dot_general(qe, st.astype(BF16), nt, preferred_element_type=F32)
        q4 = jnp.concatenate([jnp.where(lane_head == h, qe, jnp.zeros_like(qe)) for h in range(nh)], axis=0)
        att = lax.dot_general(q4, ke, nt, preferred_element_type=F32)
        att_mask = (row4 >= col4) if fwd else (row4 < col4)
        o4 = jnp.dot(jnp.where(att_mask, att, 0.0).astype(BF16), vc, preferred_element_type=F32)
        for h in range(nh):
            o = o + jnp.where(out_head == h, o4[h * c:(h + 1) * c, :], 0.0)
        o_ref[bb, sl, :] = o.astype(o_ref.dtype)
        upd_t = lax.dot_general(vc, kd, tn, preferred_element_type=F32)
        s_scr[bb, d] = st * jnp.exp(b_last) + jnp.where(bd_mask, upd_t, 0.0)

    def body(j, carry):
        for bb in range(nb):
            one_chunk((qf_ref, kf_ref, vf_ref, laf_ref), of_ref, bb, 0, pl.multiple_of(j * c, c))
            one_chunk((qb_ref, kb_ref, vb_ref, lab_ref), ob_ref, bb, 1, pl.multiple_of((nchunk - 1 - j) * c, c))
        return carry

    lax.fori_loop(0, nchunk, body, 0, unroll=True)


def _gla(q, k, v, la):
    bsz, lt, _ = q.shape
    nblk = lt // TB
    fwd_map = lambda b, i: (b, i, 0)
    bwd_blk = lambda i: jnp.where(i == 0, 0, nblk - i)
    bwd_map = lambda b, i: (b, bwd_blk(i), 0)
    bwd_map_la = lambda b, i: (b, bwd_blk(i), 1)
    nb = GLA_NB if bsz % GLA_NB == 0 else 1
    spec = lambda n, m: pl.BlockSpec((nb, TB, n), m)
    return pl.pallas_call(
        _gla_kernel,
        grid=(bsz // nb, nblk),
        in_specs=[spec(GLA_QK, fwd_map), spec(GLA_QK, fwd_map), spec(GLA_V, fwd_map), spec(GLA_QK, fwd_map),
                  spec(GLA_QK, bwd_map), spec(GLA_QK, bwd_map), spec(GLA_V, bwd_map), spec(GLA_QK, bwd_map_la)],
        out_specs=[spec(GLA_V, fwd_map), spec(GLA_V, bwd_map)],
        out_shape=[jax.ShapeDtypeStruct((bsz, lt, GLA_V), BF16), jax.ShapeDtypeStruct((bsz, lt, GLA_V), BF16)],
        scratch_shapes=[pltpu.VMEM((nb, 2, GLA_V, GLA_QK), F32)],
        compiler_params=_cparams(("parallel", "arbitrary")),
        name="gla_scan",
    )(q, k, v, la, q, k, v, la)


def _s5_kernel(u_ref, wm_ref, tm_ref, cm_ref, ar_ref, ai_ref, y_ref, w_scr, hp_scr, h_scr, *, rev):
    @pl.when(pl.program_id(1) == 0)
    def _():
        h_scr[...] = jnp.zeros_like(h_scr)

    bsz, ntok, _ = u_ref.shape
    nc = ntok // S5_CHUNK
    half = 8 * S5_P
    x = jnp.concatenate(
        [jnp.concatenate([u_ref[b, pl.ds(s, nc, stride=S5_CHUNK), :] for s in range(S5_CHUNK)], axis=1)
         for b in range(bsz)], axis=0).astype(BF16)
    w_scr[...] = jnp.dot(x, wm_ref[...], preferred_element_type=F32)
    ar, ai = ar_ref[...], ai_ref[...]

    def body(j, hs):
        c = (nc - 1 - j) if rev else j
        out = []
        for b in range(bsz):
            re, im = hs[b]
            r = b * nc + c
            hp_scr[pl.ds(r, 1), :] = jnp.concatenate([re, im], axis=1)
            w = w_scr[pl.ds(r, 1), :]
            out.append((ar * re - ai * im + w[:, :half], ar * im + ai * re + w[:, half:]))
        return tuple(out)

    hs = lax.fori_loop(0, nc, body, tuple((h_scr[b:b + 1, :half], h_scr[b:b + 1, half:]) for b in range(bsz)))
    for b in range(bsz):
        h_scr[b:b + 1, :] = jnp.concatenate(hs[b], axis=1)
    mt = 256
    ntile = x.shape[1] // mt
    cols = []
    for jt in range(ntile):
        acc = None
        for it in (range(jt, ntile) if rev else range(jt + 1)):
            part = jnp.dot(x[:, it * mt:(it + 1) * mt], tm_ref[it * mt:(it + 1) * mt, jt * mt:(jt + 1) * mt],
                           preferred_element_type=F32)
            acc = part if acc is None else acc + part
        cols.append(acc)
    y = (jnp.concatenate(cols, axis=1)
         + jnp.dot(hp_scr[...].astype(BF16), cm_ref[...], preferred_element_type=F32))
    for b in range(bsz):
        for s in range(S5_CHUNK):
            y_ref[b, pl.ds(s, nc, stride=S5_CHUNK), :] = y[b * nc:(b + 1) * nc, s * 128:(s + 1) * 128]


def _s5_mats(lam_re, lam_im, log_dt, b_re, b_im, c_re, c_im, rev):
    t16 = S5_CHUNK
    dt = jnp.exp(log_dt)[:, None]
    mag = jnp.exp(lam_re * dt)
    a_re, a_im = mag * jnp.cos(lam_im * dt), mag * jnp.sin(lam_im * dt)
    den = lam_re * lam_re + lam_im * lam_im
    nr = a_re - 1.0
    co_re = ((nr * lam_re + a_im * lam_im) / den)[..., None]
    co_im = ((a_im * lam_re - nr * lam_im) / den)[..., None]
    bb_re, bb_im = co_re * b_re - co_im * b_im, co_re * b_im + co_im * b_re
    pr, pi = [jnp.ones_like(a_re)], [jnp.zeros_like(a_im)]
    for _ in range(t16):
        pr, pi = pr + [pr[-1] * a_re - pi[-1] * a_im], pi + [pr[-1] * a_im + pi[-1] * a_re]
    pw_re, pw_im = jnp.stack(pr), jnp.stack(pi)
    g = lam_re.shape[0]
    e_re, e_im = pw_re[t16 - 1::-1][:t16], pw_im[t16 - 1::-1][:t16]
    wre = jnp.einsum('sgp,gph->gshp', e_re, bb_re) - jnp.einsum('sgp,gph->gshp', e_im, bb_im)
    wim = jnp.einsum('sgp,gph->gshp', e_re, bb_im) + jnp.einsum('sgp,gph->gshp', e_im, bb_re)
    cb_re = jnp.einsum('gkp,gph->gpkh', c_re, bb_re) - jnp.einsum('gkp,gph->gpkh', c_im, bb_im)
    cb_im = jnp.einsum('gkp,gph->gpkh', c_re, bb_im) + jnp.einsum('gkp,gph->gpkh', c_im, bb_re)
    kd = jnp.einsum('dgp,gpkh->dgkh', pw_re[:t16], cb_re) - jnp.einsum('dgp,gpkh->dgkh', pw_im[:t16], cb_im)
    lag = np.arange(t16)[None, :] - np.arange(t16)[:, None]
    toe = jnp.where((lag >= 0)[:, :, None, None, None], kd[np.clip(lag, 0, t16 - 1)], 0.0)
    toe = toe.transpose(2, 0, 4, 1, 3)
    q_re, q_im = pw_re[1:], pw_im[1:]
    ca_re = jnp.einsum('gkp,tgp->gptk', c_re, q_re) - jnp.einsum('gkp,tgp->gptk', c_im, q_im)
    ca_im = jnp.einsum('gkp,tgp->gptk', c_re, q_im) + jnp.einsum('gkp,tgp->gptk', c_im, q_re)
    if rev:
        wre, wim = wre[:, ::-1], wim[:, ::-1]
        toe = toe[:, ::-1, :, ::-1]
        ca_re, ca_im = ca_re[:, :, ::-1], ca_im[:, :, ::-1]
    nq, gl = g // 8, 8
    nd = t16 * S5_H
    tok_hot = np.zeros((gl, nd, t16 * 128), np.float32)
    st_hot = np.zeros((gl, 2 * S5_P, 2 * gl * S5_P), np.float32)
    for gi in range(gl):
        a = np.arange(nd)
        tok_hot[gi, a, (a // S5_H) * 128 + gi * S5_H + a % S5_H] = 1.0
        a = np.arange(2 * S5_P)
        st_hot[gi, a, (a // S5_P) * gl * S5_P + gi * S5_P + a % S5_P] = 1.0
    place = lambda rows, blk, cols: jnp.einsum('gar,qgab,gbc->qrc', rows, blk, cols)
    wg = jnp.concatenate([wre, wim], axis=-1).reshape(nq, gl, nd, 2 * S5_P)
    tg = toe.reshape(nq, gl, nd, nd)
    cg = jnp.concatenate([ca_re, -ca_im], axis=1).reshape(nq, gl, 2 * S5_P, nd)
    wm = place(tok_hot, wg, st_hot)
    tmat = place(tok_hot, tg, tok_hot)
    cm = place(st_hot, cg, tok_hot)
    return (wm.astype(BF16), tmat.astype(BF16), cm.astype(BF16),
            pw_re[t16].reshape(nq, 1, gl * S5_P), pw_im[t16].reshape(nq, 1, gl * S5_P))


def _s5_scan(u, mats, rev):
    bsz, lt, _ = u.shape
    tblk = max(d for d in range(64, S5_TBLK + 1, 64) if lt % d == 0)
    nblk = lt // tblk
    nq = S5_G // 8
    wm, tmat, cm, ar, ai = mats
    kw, ks = S5_CHUNK * 128, 8 * S5_P
    tmap = (lambda q, t: (0, nblk - 1 - t, q)) if rev else (lambda q, t: (0, t, q))
    per = lambda shp: pl.BlockSpec((None,) + shp, lambda q, t: (q,) + tuple(0 for _ in shp))
    rows = bsz * tblk // S5_CHUNK
    return pl.pallas_call(
        functools.partial(_s5_kernel, rev=rev),
        grid=(nq, nblk),
        in_specs=[pl.BlockSpec((bsz, tblk, 128), tmap),
                  per((kw, 2 * ks)), per((kw, kw)), per((2 * ks, kw)), per((1, ks)), per((1, ks))],
        out_specs=pl.BlockSpec((bsz, tblk, 128), tmap),
        out_shape=jax.ShapeDtypeStruct((bsz, lt, S5_W), F32),
        scratch_shapes=[pltpu.VMEM((rows, 2 * ks), F32), pltpu.VMEM((rows, 2 * ks), F32),
                        pltpu.VMEM((8, 2 * ks), F32)],
        compiler_params=_cparams(("parallel", "arbitrary")),
        name="s5_scan",
    )(u, wm, tmat, cm, ar, ai)


def _mixer_tail(x, mix, mods, wr_ref, br_ref, x_out, h2_out, lg_out):
    gate, shift, scale = mods
    xn = x + gate * mix
    x_out[...] = xn
    h2 = _rms_rows(xn) * (1.0 + scale) + shift
    _to_token_tiles(h2_out, h2)
    lg_out[...] = _route_tail(_dot_x3(h2, wr_ref[...]) + br_ref[...])


def _route_tail(lg):
    lane = lax.broadcasted_iota(jnp.int32, lg.shape, 1)
    lane_f = lane.astype(F32)
    big = jnp.float32(128.0)
    row_max = lambda t: jnp.max(t, axis=1, keepdims=True)
    first_lane = lambda hit: jnp.min(jnp.where(hit, lane_f, big), axis=1, keepdims=True)
    is_g = lane < N_GROUPS
    g_logit = jnp.where(is_g, lg, NEG_INF)
    g_max = row_max(g_logit)
    grp = first_lane(g_logit == g_max)
    g_w = 1.0 / jnp.sum(jnp.where(is_g, jnp.exp(lg - g_max), 0.0), axis=1, keepdims=True)
    lo = N_GROUPS + EXP_PER_GROUP * grp
    in_grp = jnp.logical_and(lane_f >= lo, lane_f < lo + EXP_PER_GROUP)
    e_max = row_max(jnp.where(in_grp, lg, NEG_INF))
    pe = jnp.where(in_grp, jnp.exp(lg - e_max), 0.0)
    p = pe / jnp.sum(pe, axis=1, keepdims=True)
    cand = jnp.where(in_grp, p, -1.0)
    p1 = row_max(cand)
    i1 = first_lane(cand == p1)
    cand2 = jnp.where(lane_f == i1, -1.0, cand)
    p2 = row_max(cand2)
    i2 = first_lane(cand2 == p2)
    scale = g_w / (p1 + p2)
    out = jnp.where(lane == 0, scale * p1, jnp.where(lane == 1, scale * p2,
                    jnp.where(lane == 2, i1 - N_GROUPS, jnp.where(lane == 3, i2 - N_GROUPS, 0.0))))
    return out


def _ev_out_kernel(of_ref, ob_ref, g_ref, yf_ref, u_ref, ctx_ref, mod_ref,
                   gn_ref, ds_ref, wglu_ref, wout_ref, wr_ref, br_ref, *rest, nsub):
    yb_refs, x_refs, (x_out, h2_out, lg_out) = rest[:nsub], rest[nsub:2 * nsub], rest[2 * nsub:]
    i = pl.program_id(1)
    tm = nsub * TB
    yb = jnp.concatenate([r[...] for r in yb_refs], axis=0)
    x = jnp.concatenate([jnp.where(nsub * i + k == 0, ctx_ref[...], r[...]) for k, r in enumerate(x_refs)], axis=0)
    is_ctx = i * tm + lax.broadcasted_iota(jnp.int32, (tm, 1), 0) < TB
    mod = lambda r: jnp.where(is_ctx, mod_ref[0, r:r + 1, :], mod_ref[1, r:r + 1, :])
    o = of_ref[...].astype(F32) + ob_ref[...].astype(F32)
    og = jnp.concatenate([_rms_rows(o[:, h * GLA_DV:(h + 1) * GLA_DV]) for h in range(GLA_HEADS)], axis=1)
    g = g_ref[...].astype(F32)
    og = og * gn_ref[...] * (g * _sigmoid(g))
    t = yf_ref[...] + yb + ds_ref[...] * u_ref[...]
    y = t * (0.5 * (1.0 + jnp.tanh(math.sqrt(2.0 / math.pi) * (t + 0.044715 * (t * t * t)))))
    y = y * _sigmoid(jnp.dot(y.astype(BF16), wglu_ref[...], preferred_element_type=F32))
    cat = jnp.concatenate([og, y], axis=1).astype(BF16)
    mix = jnp.dot(cat, wout_ref[...], preferred_element_type=F32)
    _mixer_tail(x, mix, (mod(2), mod(3), mod(4)), wr_ref, br_ref, x_out, h2_out, lg_out)


def _ev_out(o_f, o_b, g, y_f, y_b, u, ctx, x, mods, gn, ds, wglu, wout, wr, br):
    bsz, lt, _ = u.shape
    nstream = lt // TB
    nsub = 3 if nstream % 3 == 0 else 1
    tm = nsub * TB
    nblk = lt // tm
    tok = lambda n: pl.BlockSpec((None, tm, n), lambda b, i: (b, i, 0))
    const = lambda shp: pl.BlockSpec(shp, lambda b, i: tuple(0 for _ in shp))
    sub = lambda n, fn: [pl.BlockSpec((None, TB, n), functools.partial(fn, k)) for k in range(nsub)]
    yb_map = lambda k, b, i: _swapped_index(nstream, b, nsub * i + k)
    x_map = lambda k, b, i: (b, jnp.maximum(nsub * i + k - 1, 0), 0)
    return pl.pallas_call(
        functools.partial(_ev_out_kernel, nsub=nsub),
        grid=(bsz, nblk),
        in_specs=[tok(512), tok(512), tok(512), tok(512), tok(512),
                  pl.BlockSpec((None, TB, D_MODEL), lambda b, i: (b, 0, 0)),
                  pl.BlockSpec((None, 2, 6, D_MODEL), lambda b, i: (b, 0, 0, 0)),
                  const((1, 512)), const((1, 512)), const((512, 512)), const((D_MODEL, D_MODEL)),
                  const((D_MODEL, 128)), const((1, 128))] + sub(S5_W, yb_map) + sub(D_MODEL, x_map),
        out_specs=[tok(D_MODEL), pl.BlockSpec((tm * ROW_TILE, 128), lambda b, i: (b * nblk + i, 0)), tok(128)],
        out_shape=[jax.ShapeDtypeStruct((bsz, lt, D_MODEL), F32),
                   jax.ShapeDtypeStruct((bsz * lt * ROW_TILE, 128), F32),
                   jax.ShapeDtypeStruct((bsz, lt, 128), F32)],
        compiler_params=_cparams(("parallel", "parallel")),
        name="ev_out",
    )(o_f, o_b, g, y_f, u, ctx, mods, gn, ds, wglu, wout, wr, br, *([y_b] * nsub), *([x] * nsub))


def _route(route):
    n_tok = route.shape[0]
    gate = route[:, :TOP_K]
    eid = route[:, TOP_K:2 * TOP_K].astype(jnp.int32).reshape(-1)
    n_asg = n_tok * TOP_K
    order = jnp.argsort(eid).astype(jnp.int32)
    counts = jnp.sum((eid[:, None] == jnp.arange(N_EXPERTS)[None, :]).astype(jnp.int32), axis=0)
    padded = (counts + MOE_BLOCK - 1) // MOE_BLOCK * MOE_BLOCK
    pad_end = jnp.cumsum(padded)
    pad_start = pad_end - padded
    cnt_start = jnp.cumsum(counts) - counts
    n_blocks = -(-n_asg // MOE_BLOCK) + N_EXPERTS
    blk_start = jnp.arange(n_blocks, dtype=jnp.int32) * MOE_BLOCK
    blk_e = jnp.minimum(jnp.sum((pad_end[None, :] <= blk_start[:, None]).astype(jnp.int32), axis=1), N_EXPERTS - 1)
    pos = jnp.arange(n_blocks * MOE_BLOCK, dtype=jnp.int32)
    pos_e = jnp.repeat(blk_e, MOE_BLOCK)
    rank = pos - pad_start[pos_e]
    src = jnp.clip(cnt_start[pos_e] + rank, 0, n_asg - 1)
    slot_buf = jnp.where(rank < counts[pos_e], order[src], n_asg).astype(jnp.int32)
    n_valid = jnp.sum((slot_buf < n_asg).reshape(n_blocks, MOE_BLOCK), axis=1).astype(jnp.int32)
    tok = lax.shift_right_logical(slot_buf, 1)
    src_rows = jnp.minimum(tok, n_tok - 1) * ROW_TILE
    spare = n_asg + (pos // MOE_BLOCK % 2) * MOE_BLOCK + pos % MOE_BLOCK
    dst_rows = jnp.where(slot_buf < n_asg, (slot_buf & 1) * n_tok + tok, spare)
    lead = n_asg + jnp.arange(2 * MOE_BLOCK, dtype=jnp.int32)
    dst_rows = jnp.concatenate([lead, dst_rows]) * ROW_TILE
    return src_rows, dst_rows, blk_e, n_valid, gate.astype(F32)


def _moe_kernel(src_ref, dst_ref, blke_ref, nvalid_ref, h_hbm, w1_ref, w3_ref, w2_ref, z_hbm,
                xbuf0, xbuf1, ybuf0, ybuf1, wb1, wb3, wb2, gsem, ssem):
    i = pl.program_id(0)
    nblk = pl.num_programs(0)
    ns = ROW_TILE
    xb, yb = (xbuf0, xbuf1), (ybuf0, ybuf1)
    lead = 2 * MOE_BLOCK

    def issue_gather(blk, buf):
        base = blk * MOE_BLOCK
        for r in range(MOE_BLOCK):
            src = pl.multiple_of(src_ref[base + r], ns)
            pltpu.make_async_copy(h_hbm.at[pl.ds(src, ns)], xb[buf].at[pl.ds(r * ns, ns)],
                                  gsem.at[buf]).start(priority=r % 2)

    def issue_scatter(blk, buf):
        base = lead + blk * MOE_BLOCK
        for r in range(MOE_BLOCK):
            dst = pl.multiple_of(dst_ref[base + r], ns)
            pltpu.make_async_copy(yb[buf].at[pl.ds(r * ns, ns)], z_hbm.at[pl.ds(dst, ns)],
                                  ssem.at[buf]).start(priority=r % 2)

    def wait_gather(buf):
        pltpu.make_async_copy(h_hbm.at[pl.ds(0, MOE_BLOCK * ns)], xb[buf], gsem.at[buf]).wait()

    def wait_scatter(buf):
        pltpu.make_async_copy(yb[buf], z_hbm.at[pl.ds(0, MOE_BLOCK * ns)], ssem.at[buf]).wait()

    used = nvalid_ref[i] > 0

    @pl.when(i == 0)
    def _():
        ybuf0[...] = jnp.zeros_like(ybuf0)
        ybuf1[...] = jnp.zeros_like(ybuf1)
        issue_scatter(-2, 0)
        issue_gather(0, 0)

    def step(cur):
        oth = 1 - cur
        wait_gather(cur)
        issue_gather(jnp.minimum(i + 1, nblk - 1), oth)
        issue_scatter(i - 1, oth)
        x = _from_token_tiles(xb[cur], MOE_BLOCK).astype(BF16)

        @pl.when(jnp.logical_or(i == 0, blke_ref[i] != blke_ref[jnp.maximum(i - 1, 0)]))
        def _():
            wb1[...] = w1_ref[...].astype(BF16)
            wb3[...] = w3_ref[...].astype(BF16)
            wb2[...] = w2_ref[...].astype(BF16)

        h1 = jnp.dot(x, wb1[...], preferred_element_type=F32)
        h3 = jnp.dot(x, wb3[...], preferred_element_type=F32)
        a = (h1 * _sigmoid(h1) * h3).astype(BF16)
        y = jnp.dot(a, wb2[...], preferred_element_type=F32)
        wait_scatter(cur)
        _to_token_tiles(yb[cur], y)

    def drain(last_par):
        wait_gather(1 - last_par)
        issue_scatter(jnp.where(used, i, i - 1), last_par)
        wait_scatter(1 - last_par)
        wait_scatter(last_par)

    for par in range(2):
        @pl.when(jnp.logical_and(used, i % 2 == par))
        def _():
            step(par)

    first_unused = jnp.logical_and(jnp.logical_not(used),
                                   jnp.logical_and(i > 0, nvalid_ref[jnp.maximum(i - 1, 0)] > 0))
    last_used = jnp.logical_and(used, i == nblk - 1)
    for par in range(2):
        @pl.when(jnp.logical_or(jnp.logical_and(first_unused, (i - 1) % 2 == par),
                                jnp.logical_and(last_used, i % 2 == par)))
        def _():
            drain(par)


def _moe_experts(h2, src_rows, dst_rows, blk_e, n_valid, w1, w3, w2, layer):
    n_tok = h2.shape[0] // ROW_TILE
    n_blocks = blk_e.shape[0]
    wspec = lambda shp: pl.BlockSpec((None, None) + shp, lambda i, src, dst, blke, nvalid: (layer, blke[i], 0, 0))
    grid_spec = pltpu.PrefetchScalarGridSpec(
        num_scalar_prefetch=4,
        grid=(n_blocks,),
        in_specs=[pl.BlockSpec(memory_space=pl.ANY),
                  wspec((D_MODEL, D_EXPERT)), wspec((D_MODEL, D_EXPERT)), wspec((D_EXPERT, D_MODEL))],
        out_specs=pl.BlockSpec(memory_space=pl.ANY),
        scratch_shapes=[pltpu.VMEM((MOE_BLOCK * ROW_TILE, 128), F32), pltpu.VMEM((MOE_BLOCK * ROW_TILE, 128), F32),
                        pltpu.VMEM((MOE_BLOCK * ROW_TILE, 128), F32), pltpu.VMEM((MOE_BLOCK * ROW_TILE, 128), F32),
                        pltpu.VMEM((D_MODEL, D_EXPERT), BF16), pltpu.VMEM((D_MODEL, D_EXPERT), BF16),
                        pltpu.VMEM((D_EXPERT, D_MODEL), BF16),
                        pltpu.SemaphoreType.DMA((2,)), pltpu.SemaphoreType.DMA((2,))])
    return pl.pallas_call(
        _moe_kernel,
        grid_spec=grid_spec,
        out_shape=jax.ShapeDtypeStruct(((TOP_K * n_tok + 2 * MOE_BLOCK) * ROW_TILE, 128), F32),
        compiler_params=_cparams(("arbitrary",)),
        name="moe_experts",
    )(src_rows, dst_rows, blk_e, n_valid, h2, w1, w3, w2)


def _moe_combine_kernel(x_ref, z0_ref, z1_ref, gate_ref, mod_ref, o_ref, *, n_ctx):
    gate = gate_ref[...]
    tm = x_ref.shape[0]
    y = gate[:, 0:1] * _from_token_tiles(z0_ref, tm) + gate[:, 1:2] * _from_token_tiles(z1_ref, tm)
    g_out = mod_ref[1, 5:6, :]
    if n_ctx:
        is_ctx = pl.program_id(1) * tm + lax.broadcasted_iota(jnp.int32, (tm, 1), 0) < n_ctx
        g_out = jnp.where(is_ctx, mod_ref[0, 5:6, :], g_out)
    o_ref[...] = x_ref[...] + g_out * y


def _moe_combine(x, z, gate, mods, n_ctx):
    bsz, lt, _ = x.shape
    tm = max(d for d in range(TB, COMBINE_TILE + 1, TB) if lt % d == 0)
    nblk = lt // tm
    gate3 = gate.reshape(bsz, lt, TOP_K)
    return pl.pallas_call(
        functools.partial(_moe_combine_kernel, n_ctx=n_ctx),
        grid=(bsz, nblk),
        in_specs=[pl.BlockSpec((None, tm, D_MODEL), lambda b, i: (b, i, 0)),
                  pl.BlockSpec((tm * ROW_TILE, 128), lambda b, i: (b * nblk + i, 0)),
                  pl.BlockSpec((tm * ROW_TILE, 128), lambda b, i: ((bsz + b) * nblk + i, 0)),
                  pl.BlockSpec((None, tm, TOP_K), lambda b, i: (b, i, 0)),
                  pl.BlockSpec((None, 2, 6, D_MODEL), lambda b, i: (b, 0, 0, 0))],
        out_specs=pl.BlockSpec((None, tm, D_MODEL), lambda b, i: (b, i, 0)),
        out_shape=jax.ShapeDtypeStruct((bsz, lt, D_MODEL), F32),
        compiler_params=_cparams(("parallel", "parallel")),
        name="moe_combine",
    )(x, z, z, gate3, mods)


def _moe(x, h2, logits, mods, n_ctx, w1, w3, w2, layer):
    bsz, lt, _ = x.shape
    src_rows, dst_rows, blk_e, n_valid, gate = _route(logits.reshape(bsz * lt, 128))
    z = _moe_experts(h2, src_rows, dst_rows, blk_e, n_valid, w1, w3, w2, layer)
    return _moe_combine(x, z, gate, mods, n_ctx)


def _od_proj_kernel(x_ref, mod_ref, w_ref, gm_ref, qn_ref, kn_ref, q_ref, k_ref, v_ref, zh_ref):
    tm = x_ref.shape[0]
    is_ctx = pl.program_id(1) * tm + lax.broadcasted_iota(jnp.int32, (tm, 1), 0) < TB
    shift = jnp.where(is_ctx, mod_ref[0, 0:1, :], mod_ref[1, 0:1, :])
    scale = jnp.where(is_ctx, mod_ref[0, 1:2, :], mod_ref[1, 1:2, :])
    h = _rms_rows(x_ref[...]) * (1.0 + scale) + shift
    z = jnp.dot(h.astype(BF16), w_ref[...], preferred_element_type=F32)

    def head_norm(t, gain):
        sq_hi, sq_lo = _split_bf16(t * t)
        gm = gm_ref[...].astype(BF16)
        ms = jnp.dot(sq_hi, gm, preferred_element_type=F32) + jnp.dot(sq_lo, gm, preferred_element_type=F32)
        return t * lax.rsqrt(ms + EPS) * gain

    q_ref[...] = (head_norm(z[:, :NA_W], qn_ref[...]) * (NA_DH ** -0.5)).astype(BF16)
    k_ref[...] = head_norm(z[:, NA_W:2 * NA_W], kn_ref[...]).astype(BF16)
    v_ref[...] = z[:, 2 * NA_W:3 * NA_W].astype(BF16)
    zh_ref[...] = z[:, 3 * NA_W:].astype(BF16)


def _od_proj(xcat, mods, w, gm, qn, kn):
    bsz, lt, _ = xcat.shape
    tm = max(d for d in range(TB, PROJ_TILE + 1, TB) if lt % d == 0)
    nblk = lt // tm
    tok = lambda n: pl.BlockSpec((None, tm, n), lambda b, i: (b, i, 0))
    const = lambda shp: pl.BlockSpec(shp, lambda b, i: tuple(0 for _ in shp))
    return pl.pallas_call(
        _od_proj_kernel,
        grid=(bsz, nblk),
        in_specs=[tok(D_MODEL), pl.BlockSpec((None, 2, 6, D_MODEL), lambda b, i: (b, 0, 0, 0)),
                  const((D_MODEL, 3 * NA_W + 3 * HY_W)), const((NA_W, NA_W)), const((1, NA_W)), const((1, NA_W))],
        out_specs=[tok(NA_W), tok(NA_W), tok(NA_W), tok(3 * HY_W)],
        out_shape=[jax.ShapeDtypeStruct((bsz, lt, NA_W), BF16), jax.ShapeDtypeStruct((bsz, lt, NA_W), BF16),
                   jax.ShapeDtypeStruct((bsz, lt, NA_W), BF16), jax.ShapeDtypeStruct((bsz, lt, 3 * HY_W), BF16)],
        compiler_params=_cparams(("parallel", "parallel")),
        name="od_proj",
    )(xcat, mods, w, gm, qn, kn)


def _na_kernel(q_ref, k_ref, v_ref, t2_ref, o_ref):
    for j in range(NA_ROWS):
        _na_one_row(q_ref, k_ref, v_ref, t2_ref, o_ref, j)


def _na_one_row(q_ref, k_ref, v_ref, t2_ref, o_ref, j):
    r = pl.program_id(1) * NA_ROWS + j
    n_rows = pl.num_programs(1) * NA_ROWS
    r0 = jnp.clip(r - WIN_R // 2, 0, n_rows - WIN_R)
    off = r0 - r + WIN_R - 1
    base = pl.multiple_of(TB + r0 * GRID_W, GRID_W)
    nloc = WIN_R * GRID_W
    q = q_ref[j * GRID_W:(j + 1) * GRID_W, :]
    hg = NA_HG
    gw = hg * NA_DH
    lane_head = lax.broadcasted_iota(jnp.int32, (GRID_W, gw), 1) // NA_DH
    nt = (((1,), (1,)), ((), ()))
    outs = []
    for grp in range(NA_HEADS // hg):
        cs = slice(gw * grp, gw * (grp + 1))
        q2 = q[:, cs]
        q4 = jnp.concatenate([jnp.where(lane_head == h, q2, jnp.zeros_like(q2)) for h in range(hg)], axis=0)
        kw, vw = k_ref[pl.ds(base, nloc), cs], v_ref[pl.ds(base, nloc), cs]
        kc, vc = k_ref[0:TB, cs], v_ref[0:TB, cs]
        bias = jnp.concatenate(
            [jnp.concatenate([t2_ref[hg * grp + h, off + 2 * m] for m in range(WIN_R // 2)], axis=1)
             for h in range(hg)], axis=0)
        s_loc = lax.dot_general(q4, kw, nt, preferred_element_type=F32) + bias
        s_ctx = lax.dot_general(q4, kc, nt, preferred_element_type=F32)
        m = jnp.maximum(jnp.max(s_loc, axis=1, keepdims=True), jnp.max(s_ctx, axis=1, keepdims=True))
        p_loc, p_ctx = jnp.exp(s_loc - m), jnp.exp(s_ctx - m)
        den = jnp.sum(p_loc, axis=1, keepdims=True) + jnp.sum(p_ctx, axis=1, keepdims=True)
        o4 = (jnp.dot(p_loc.astype(BF16), vw, preferred_element_type=F32)
              + jnp.dot(p_ctx.astype(BF16), vc, preferred_element_type=F32)) / den
        acc = jnp.zeros((GRID_W, gw), F32)
        for h in range(hg):
            acc = jnp.where(lane_head == h, o4[h * GRID_W:(h + 1) * GRID_W, :], acc)
        outs.append(acc)
    o_ref[j * GRID_W:(j + 1) * GRID_W, :] = jnp.concatenate(outs, axis=1)


def _na_bias_table(rpb):
    qc = np.arange(GRID_W)[:, None]
    kc = np.arange(GRID_W)[None, :]
    q_start = np.clip(qc - WIN_C // 2, 0, GRID_W - WIN_C)
    valid = (kc >= q_start) & (kc < q_start + WIN_C)
    col_idx = np.clip(kc - qc + WIN_C - 1, 0, 2 * WIN_C - 2)
    hot = np.zeros((GRID_W, GRID_W, 2 * WIN_C - 1), np.float32)
    hot[qc, kc, col_idx] = 1.0
    t = jnp.einsum('hrj,qkj->hrqk', rpb.astype(F32), hot, precision=HI)
    t = jnp.where(valid[None, None], t, NEG_INF)
    return jnp.concatenate([t[:, :-1], t[:, 1:]], axis=-1)


def _na(q, k, v, t2):
    bsz, lt, _ = q.shape
    qrows = NA_ROWS * GRID_W
    n_rows = (lt - TB) // qrows
    qoff = TB // qrows
    return pl.pallas_call(
        _na_kernel,
        grid=(bsz, n_rows),
        in_specs=[pl.BlockSpec((None, qrows, NA_W), lambda b, r: (b, r + qoff, 0)),
                  pl.BlockSpec((None, lt, NA_W), lambda b, r: (b, 0, 0)),
                  pl.BlockSpec((None, lt, NA_W), lambda b, r: (b, 0, 0)),
                  pl.BlockSpec(t2.shape, lambda b, r: (0, 0, 0, 0))],
        out_specs=pl.BlockSpec((None, qrows, NA_W), lambda b, r: (b, r, 0)),
        out_shape=jax.ShapeDtypeStruct((bsz, lt - TB, NA_W), F32),
        compiler_params=_cparams(("parallel", "arbitrary")),
        name="na_attn",
    )(q, k, v, t2)


def _hy_pre_kernel(za_ref, zb_ref, zp_ref, zn_ref, cw_ref, cb_ref, x0_ref, u_ref, ut_ref):
    i = pl.program_id(1)
    n = pl.num_programs(1)
    z = jnp.concatenate([za_ref[...], zb_ref[...]], axis=0).astype(F32)
    tb = z.shape[0]
    prev_row = jnp.where(i > 0, zp_ref[HALO - 1:HALO, :].astype(F32), 0.0)
    next_row = jnp.where(i < n - 1, zn_ref[0:1, :].astype(F32), 0.0)
    rowid = lax.broadcasted_iota(jnp.int32, z.shape, 0)
    zm = jnp.where(rowid == 0, prev_row, pltpu.roll(z, 1, 0))
    zp = jnp.where(rowid == tb - 1, next_row, pltpu.roll(z, tb - 1, 0))
    zc = cb_ref[...] + cw_ref[0:1, :] * zm
    zc = zc + cw_ref[1:2, :] * z
    zc = zc + cw_ref[2:3, :] * zp
    x0_ref[...] = zc[:, :HY_W].astype(BF16)
    u = zc[:, HY_W:2 * HY_W] * zc[:, 2 * HY_W:]
    u_ref[...] = u.astype(BF16)
    for j in range(tb // FFT_N1):
        ut_ref[j] = u[j * FFT_N1:(j + 1) * FFT_N1, :].T.astype(BF16)


def _hy_pre(zh, cw, cb):
    bsz, lt, _ = zh.shape
    l = lt - TB
    tm = 2 * TB
    nblk = l // tm
    hpt = TB // HALO
    return pl.pallas_call(
        _hy_pre_kernel,
        grid=(bsz, nblk),
        in_specs=[pl.BlockSpec((None, TB, 3 * HY_W), lambda b, i: (b, 2 * i + 1, 0)),
                  pl.BlockSpec((None, TB, 3 * HY_W), lambda b, i: (b, 2 * i + 2, 0)),
                  pl.BlockSpec((None, HALO, 3 * HY_W), lambda b, i: (b, (2 * i + 1) * hpt - 1, 0)),
                  pl.BlockSpec((None, HALO, 3 * HY_W),
                               lambda b, i: (b, jnp.minimum((2 * i + 3) * hpt, lt // HALO - 1), 0)),
                  pl.BlockSpec((HY_SHORT, 3 * HY_W), lambda b, i: (0, 0)),
                  pl.BlockSpec((1, 3 * HY_W), lambda b, i: (0, 0))],
        out_specs=[pl.BlockSpec((None, tm, HY_W), lambda b, i: (b, i, 0)),
                   pl.BlockSpec((None, tm, HY_W), lambda b, i: (b, i, 0)),
                   pl.BlockSpec((None, tm // FFT_N1, HY_W, FFT_N1), lambda b, i: (b, i, 0, 0))],
        out_shape=[jax.ShapeDtypeStruct((bsz, l, HY_W), BF16), jax.ShapeDtypeStruct((bsz, l, HY_W), BF16),
                   jax.ShapeDtypeStruct((bsz, l // FFT_N1, HY_W, FFT_N1), BF16)],
        compiler_params=_cparams(("parallel", "parallel")),
        name="hy_pre",
    )(zh, zh, zh, zh, cw, cb)


def _fft_consts(n1_in):
    n = FFT_N1
    idx = np.arange(n)
    ang1 = 2.0 * np.pi * np.outer(idx, idx) / n
    c, s = np.cos(ang1), np.sin(ang1)
    angt = 2.0 * np.pi * np.outer(idx, idx) / (n * n)
    tw = np.concatenate([np.cos(angt), -np.sin(angt)], axis=1)
    f3 = np.block([[c, -s], [s, c]])
    f3i = np.block([[c, s], [-s, c]])
    ch, sh = c[:, :n1_in], s[:, :n1_in]
    f1_pair = np.block([[ch, sh], [-sh, ch]])
    f1_real = np.concatenate([c, -s], axis=0)
    f1i = np.block([[ch.T, -sh.T], [sh.T, ch.T]]) / (n * n)
    return tw, f3, f3i, f1_pair, f1_real, f1i


def _fft_forward(a, tw_re, tw_im, lhs_scr, ncg):
    for cix in range(ncg):
        cs = slice(cix * FFT_N1, (cix + 1) * FFT_N1)
        are, aim = a[:FFT_N1, cs], a[FFT_N1:, cs]
        lhs_scr[cs, :FFT_N1] = are * tw_re - aim * tw_im
        lhs_scr[cs, FFT_N1:] = are * tw_im + aim * tw_re


def _hy_filt_kernel(k_ref, f1_ref, tw_ref, f3_ref, o_ref, lhs_scr):
    a = _dot_x3(f1_ref[...], k_ref[...])
    _fft_forward(a, tw_ref[:, :FFT_N1], tw_ref[:, FFT_N1:], lhs_scr, FFT_CG)
    o_ref[...] = _dot_x3(lhs_scr[...], f3_ref[...])


def _hy_fft_kernel(u_ref, kf_ref, f1_ref, tw_ref, f3_ref, f3i_ref, f1i_ref, y_ref, lhs_scr, a2_scr):
    cgl = FFT_CG * FFT_N1
    x = u_ref[...].reshape(2 * u_ref.shape[1], cgl)
    a = jnp.dot(f1_ref[...], x, preferred_element_type=F32)
    tw_re, tw_im = tw_ref[:, :FFT_N1], tw_ref[:, FFT_N1:]
    _fft_forward(a, tw_re, tw_im, lhs_scr, FFT_CG)
    y = jnp.dot(lhs_scr[...].astype(BF16), f3_ref[...], preferred_element_type=F32)
    yre, yim = y[:, :FFT_N1], y[:, FFT_N1:]
    kre, kim = kf_ref[:, :FFT_N1], kf_ref[:, FFT_N1:]
    z = jnp.concatenate([yre * kre - yim * kim, yre * kim + yim * kre], axis=1).astype(BF16)
    bp = jnp.dot(z, f3i_ref[...], preferred_element_type=F32)
    for cix in range(FFT_CG):
        cs = slice(cix * FFT_N1, (cix + 1) * FFT_N1)
        bre, bim = bp[cs, :FFT_N1], bp[cs, FFT_N1:]
        a2_scr[:FFT_N1, cs] = bre * tw_re + bim * tw_im
        a2_scr[FFT_N1:, cs] = bim * tw_re - bre * tw_im
    out = jnp.dot(f1i_ref[...], a2_scr[...].astype(BF16), preferred_element_type=F32)
    y_ref[...] = out.reshape(2, u_ref.shape[1], cgl)


def _hy_conv(ut, kfilt):
    bsz, n1h, nch, _ = ut.shape
    assert 2 * n1h == FFT_N1 and bsz % 2 == 0
    cgl = FFT_CG * FFT_N1
    ncol = nch * FFT_N1
    tw, f3, f3i, f1_pair, f1_real, f1i = _fft_consts(n1h)
    kt = kfilt.reshape(FFT_N1, FFT_N1, nch).transpose(0, 2, 1).reshape(FFT_N1, ncol)
    const2 = lambda shp: pl.BlockSpec(shp, lambda *a: (0, 0))
    kf = pl.pallas_call(
        _hy_filt_kernel,
        grid=(nch // FFT_CG,),
        in_specs=[pl.BlockSpec((FFT_N1, cgl), lambda j: (0, j)), const2((2 * FFT_N1, FFT_N1)),
                  const2((FFT_N1, 2 * FFT_N1)), const2((2 * FFT_N1, 2 * FFT_N1))],
        out_specs=pl.BlockSpec((cgl, 2 * FFT_N1), lambda j: (j, 0)),
        out_shape=jax.ShapeDtypeStruct((ncol, 2 * FFT_N1), F32),
        scratch_shapes=[pltpu.VMEM((cgl, 2 * FFT_N1), F32)],
        compiler_params=_cparams(("parallel",)),
        name="hy_filter_dft",
    )(kt, jnp.asarray(f1_real, F32), jnp.asarray(tw, F32), jnp.asarray(f3, F32))
    u2 = ut.reshape(bsz, n1h, ncol)
    y = pl.pallas_call(
        _hy_fft_kernel,
        grid=(bsz // 2, nch // FFT_CG),
        in_specs=[pl.BlockSpec((2, n1h, cgl), lambda p, j: (p, 0, j)),
                  pl.BlockSpec((cgl, 2 * FFT_N1), lambda p, j: (j, 0)),
                  const2((2 * FFT_N1, FFT_N1)), const2((FFT_N1, 2 * FFT_N1)),
                  const2((2 * FFT_N1, 2 * FFT_N1)), const2((2 * FFT_N1, 2 * FFT_N1)), const2((FFT_N1, 2 * FFT_N1))],
        out_specs=pl.BlockSpec((2, n1h, cgl), lambda p, j: (p, 0, j)),
        out_shape=jax.ShapeDtypeStruct((bsz, n1h, ncol), F32),
        scratch_shapes=[pltpu.VMEM((cgl, 2 * FFT_N1), F32), pltpu.VMEM((2 * FFT_N1, cgl), F32)],
        compiler_params=_cparams(("parallel", "parallel")),
        name="hy_fft_conv",
    )(u2, kf, jnp.asarray(f1_pair, F32).astype(BF16), jnp.asarray(tw, F32), jnp.asarray(f3, F32).astype(BF16),
      jnp.asarray(f3i, F32).astype(BF16), jnp.asarray(f1i, F32).astype(BF16))
    return y.reshape(bsz, n1h, nch, FFT_N1)


def _hy_filter_kernel(f_ref, w1_ref, b1_ref, w2_ref, b2_ref, w3_ref, b3_ref, fr_ref, w4_ref, dl_ref, o_ref):
    f = f_ref[...]
    fr = fr_ref[...]
    h = jnp.sin(fr * (_dot_x3(f, w1_ref[...]) + b1_ref[...]))
    h = jnp.sin(fr * (_dot_x3(h, w2_ref[...]) + b2_ref[...]))
    h = jnp.sin(fr * (_dot_x3(h, w3_ref[...]) + b3_ref[...]))
    out = _dot_x3(h, w4_ref[...])
    tm = f.shape[0]
    row = pl.program_id(0) * tm + lax.broadcasted_iota(jnp.int32, (tm, 1), 0)
    o_ref[0] = out[:, :HY_W] * jnp.exp(-f[:, 0:1] * dl_ref[...])
    o_ref[1] = jnp.where(row == 0, 0.0, out[:, HY_W:] * jnp.exp(-f[:, 128:129] * dl_ref[...]))


def _hy_filter(seqlen, fw1, fb1, fw2, fb2, fw3, fb3, freq, fw4):
    bands = (HY_EMB - 1) // 2
    f = jnp.linspace(1e-4, bands - 1, bands, dtype=F32)[None, :]
    t_f = jnp.linspace(0.0, 1.0, seqlen, dtype=F32)
    pos_f = jnp.arange(seqlen, dtype=F32)
    t_b = jnp.concatenate([t_f[:1], t_f[:0:-1]])
    pos_b = jnp.concatenate([pos_f[:1], pos_f[:0:-1]])

    def features(t, pos):
        w = 2.0 * math.pi * pos[:, None] / seqlen
        feat = jnp.concatenate([t[:, None], jnp.cos(f * w), -jnp.sin(f * w)], axis=-1)
        return jnp.pad(feat, ((0, 0), (0, 128 - HY_EMB)))

    feat2 = jnp.concatenate([features(t_f, pos_f), features(t_b, pos_b)], axis=1)
    nh = fw2.shape[0]
    two = lambda m: jnp.kron(jnp.eye(2, dtype=F32), m.astype(F32))
    w1 = two(jnp.pad(fw1.astype(F32), ((0, 128 - HY_EMB), (0, 0))))
    w4 = jnp.concatenate([jnp.pad(fw4[:, :HY_W].astype(F32), ((0, nh), (0, 0))),
                          jnp.pad(fw4[:, HY_W:].astype(F32), ((nh, 0), (0, 0)))], axis=1)
    deltas = jnp.abs(jnp.linspace(math.log(HY_DECAY_TARGET) / HY_DECAY_LONG_PCT,
                                  math.log(HY_DECAY_TARGET) / HY_DECAY_SHORT_PCT, HY_W, dtype=F32))
    tm = 1024
    const = lambda shp: pl.BlockSpec(shp, lambda i: (0, 0))
    row2 = lambda v: jnp.tile(v.astype(F32).reshape(1, -1), (1, 2))
    out = pl.pallas_call(
        _hy_filter_kernel,
        grid=(seqlen // tm,),
        in_specs=[pl.BlockSpec((tm, 256), lambda i: (i, 0)),
                  const((256, 2 * nh)), const((1, 2 * nh)), const((2 * nh, 2 * nh)), const((1, 2 * nh)),
                  const((2 * nh, 2 * nh)), const((1, 2 * nh)), const((1, 2 * nh)),
                  const((2 * nh, 2 * HY_W)), const((1, HY_W))],
        out_specs=pl.BlockSpec((2, tm, HY_W), lambda i: (0, i, 0)),
        out_shape=jax.ShapeDtypeStruct((2, seqlen, HY_W), F32),
        compiler_params=_cparams(("parallel",)),
        name="hy_filter_mlp",
    )(feat2, w1, row2(fb1), two(fw2), row2(fb2), two(fw3), row2(fb3), row2(freq), w4,
      deltas.reshape(1, HY_W))
    return out.reshape(2 * seqlen, HY_W)


def _od_out_kernel(na_ref, yt_ref, x0_ref, u_ref, xa_ref, xb_ref, mod_ref, hb_ref, wout_ref, wr_ref, br_ref,
                   x_out, h2_out, lg_out):
    y = jnp.concatenate([yt_ref[j].T for j in range(yt_ref.shape[0])], axis=0)
    hy = x0_ref[...].astype(F32) * (y + u_ref[...].astype(F32) * hb_ref[...])
    cat = jnp.concatenate([na_ref[...], hy], axis=1).astype(BF16)
    mix = jnp.dot(cat, wout_ref[...], preferred_element_type=F32)
    x = jnp.concatenate([xa_ref[...], xb_ref[...]], axis=0)
    _mixer_tail(x, mix, (mod_ref[2:3, :], mod_ref[3:4, :], mod_ref[4:5, :]), wr_ref, br_ref, x_out, h2_out, lg_out)


def _latent_mod_index(b, i):
    return (b, 1, 0, 0)


def _od_out(na, yt, x0, u, xcat, mods, hb, wout, wr, br):
    bsz, l, _ = na.shape
    tm = 2 * TB
    nblk = l // tm
    tok = lambda n: pl.BlockSpec((None, tm, n), lambda b, i: (b, i, 0))
    const = lambda shp: pl.BlockSpec(shp, lambda b, i: tuple(0 for _ in shp))
    return pl.pallas_call(
        _od_out_kernel,
        grid=(bsz, nblk),
        in_specs=[tok(NA_W), pl.BlockSpec((None, tm // FFT_N1, HY_W, FFT_N1), lambda b, i: (b, i, 0, 0)),
                  tok(HY_W), tok(HY_W),
                  pl.BlockSpec((None, TB, D_MODEL), lambda b, i: (b, 2 * i + 1, 0)),
                  pl.BlockSpec((None, TB, D_MODEL), lambda b, i: (b, 2 * i + 2, 0)),
                  pl.BlockSpec((None, None, 6, D_MODEL), _latent_mod_index),
                  const((1, HY_W)), const((D_MODEL, D_MODEL)), const((D_MODEL, 128)), const((1, 128))],
        out_specs=[tok(D_MODEL), pl.BlockSpec((tm * ROW_TILE, 128), lambda b, i: (b * nblk + i, 0)), tok(128)],
        out_shape=[jax.ShapeDtypeStruct((bsz, l, D_MODEL), F32), jax.ShapeDtypeStruct((bsz * l * ROW_TILE, 128), F32),
                   jax.ShapeDtypeStruct((bsz, l, 128), F32)],
        compiler_params=_cparams(("parallel", "parallel")),
        name="od_out",
    )(na, yt, x0, u, xcat, xcat, mods, hb, wout, wr, br)


def _mods(c, c_ctx, ada_w, ada_b):
    bsz = c.shape[0]
    depth, _, n = ada_w.shape
    cc = jnp.concatenate([c, c_ctx[None]], axis=0)
    a = jnp.pad(cc * _sigmoid(cc), ((0, 8 - (bsz + 1) % 8), (0, 0)))
    mp, tn = a.shape[0], 1024
    m = pl.pallas_call(
        _adaln_kernel,
        grid=(depth, n // tn),
        in_specs=[pl.BlockSpec((mp, D_MODEL), lambda l, j: (0, 0)),
                  pl.BlockSpec((None, D_MODEL, tn), lambda l, j: (l, 0, j)),
                  pl.BlockSpec((None, 1, tn), lambda l, j: (l, 0, j))],
        out_specs=pl.BlockSpec((None, mp, tn), lambda l, j: (l, 0, j)),
        out_shape=jax.ShapeDtypeStruct((depth, mp, n), F32),
        compiler_params=_cparams(("parallel", "parallel")),
        name="adaln_dense",
    )(a, ada_w, ada_b.reshape(depth, 1, n))
    mod_l = m[:, :bsz].reshape(depth, bsz, 1, 6, D_MODEL)
    mod_c = jnp.broadcast_to(m[:, bsz].reshape(depth, 1, 1, 6, D_MODEL), (depth, bsz, 1, 6, D_MODEL))
    return jnp.concatenate([mod_c, mod_l], axis=2)


def _rope_tables(seqlen):
    pos = jnp.arange(seqlen)
    half = GLA_DK // 4
    freqs = ROPE_BASE ** (-jnp.arange(half, dtype=F32) / half)
    ar = (pos // GRID_W).astype(F32)[:, None] * freqs
    ac = (pos % GRID_W).astype(F32)[:, None] * freqs
    cos = jnp.concatenate([jnp.cos(ar), jnp.cos(ar), jnp.cos(ac), jnp.cos(ac)], axis=1)
    sin = jnp.concatenate([-jnp.sin(ar), jnp.sin(ar), -jnp.sin(ac), jnp.sin(ac)], axis=1)
    cos = jnp.concatenate([jnp.ones((TB, GLA_DK), F32), cos], axis=0)
    sin = jnp.concatenate([jnp.zeros((TB, GLA_DK), F32), sin], axis=0)
    return jnp.tile(cos, (1, GLA_HEADS)), jnp.tile(sin, (1, GLA_HEADS))


def _router_weights(wg, bg, we, be):
    pad = 128 - N_GROUPS - N_EXPERTS
    wr = jnp.concatenate([wg, we, jnp.zeros((D_MODEL, pad), F32)], axis=1)
    br = jnp.concatenate([bg, be, jnp.zeros((pad,), F32)]).reshape(1, 128)
    return wr, br


def kernel(x, c, ctx, c_ctx, ada_w, ada_b, moe_wg, moe_bg, moe_we, moe_be, moe_w1, moe_w3, moe_w2, ev_w_in, ev_w_out, gla_wa2, gla_ba, gla_norm, s5_lam_re, s5_lam_im, s5_log_dt, s5_b_re, s5_b_im, s5_c_re, s5_c_im, s5_d, s5_w_glu, od_w_in, od_w_out, na_q_norm, na_k_norm, na_rpb, hy_conv_w, hy_conv_b, hy_fw1, hy_fb1, hy_fw2, hy_fb2, hy_fw3, hy_fb3, hy_freq, hy_fw4, hy_bias):
    bsz, seqlen, _ = x.shape
    assert ctx.shape[1] == TB and seqlen % TB == 0

    mods_all = _mods(c, c_ctx, ada_w, ada_b)
    mods = mods_all[0]
    w_in = ev_w_in[0]
    n_a = 2 * GLA_RANK
    a0 = 2 * GLA_QK + 2 * GLA_V
    w_ev = jnp.concatenate([w_in[:, :a0], w_in[:, a0 + n_a:], w_in[:, a0:a0 + n_a],
                            jnp.zeros((D_MODEL, 128 - n_a), F32)], axis=1).astype(BF16)
    wa = jnp.zeros((128, 2 * GLA_QK), F32)
    for d in range(2):
        wa = wa.at[d * GLA_RANK:(d + 1) * GLA_RANK, d * GLA_QK:(d + 1) * GLA_QK].set(gla_wa2[0, d])
    cos, sin = _rope_tables(seqlen)
    q, k, v, g, u, u_sw, la = _ev_proj(ctx, x, mods, w_ev, wa, gla_ba[0].reshape(1, 2 * GLA_QK), cos, sin)
    o_f, o_b = _gla(q, k, v, la)
    s5p = [t[0].astype(F32) for t in (s5_lam_re, s5_lam_im, s5_log_dt, s5_b_re, s5_b_im, s5_c_re, s5_c_im)]
    y_f = _s5_scan(u, _s5_mats(*[t[0] for t in s5p], rev=False), rev=False)
    y_b = _s5_scan(u_sw, _s5_mats(*[t[1] for t in s5p], rev=True), rev=True)
    wr, br = _router_weights(moe_wg[0], moe_bg[0], moe_we[0], moe_be[0])
    x1, h2, lg = _ev_out(o_f, o_b, g, y_f, y_b, u, ctx, x, mods,
                         jnp.tile(gla_norm[0], GLA_HEADS).reshape(1, GLA_V), s5_d[0].reshape(1, S5_W),
                         s5_w_glu[0].astype(BF16), ev_w_out[0].astype(BF16), wr, br)
    xcat = _moe(x1, h2, lg, mods, TB, moe_w1, moe_w3, moe_w2, 0)

    mods = mods_all[1]
    hd = np.arange(NA_W) // NA_DH
    gm = jnp.asarray((hd[:, None] == hd[None, :]).astype(np.float32) / NA_DH)
    qh, kh, vh, zh = _od_proj(xcat, mods, od_w_in[0].astype(BF16), gm,
                              jnp.tile(na_q_norm[0], NA_HEADS).reshape(1, NA_W),
                              jnp.tile(na_k_norm[0], NA_HEADS).reshape(1, NA_W))
    na = _na(qh, kh, vh, _na_bias_table(na_rpb[0]))
    x0, uh, ut = _hy_pre(zh, hy_conv_w[0], hy_conv_b[0].reshape(1, 3 * HY_W))
    kfilt = _hy_filter(seqlen, hy_fw1[0], hy_fb1[0], hy_fw2[0], hy_fb2[0], hy_fw3[0], hy_fb3[0],
                       hy_freq[0], hy_fw4[0])
    yt = _hy_conv(ut, kfilt)
    wr, br = _router_weights(moe_wg[1], moe_bg[1], moe_we[1], moe_be[1])
    xl, h2, lg = _od_out(na, yt, x0, uh, xcat, mods, hy_bias[0].reshape(1, HY_W),
                         od_w_out[0].astype(BF16), wr, br)
    return _moe(xl, h2, lg, mods, 0, moe_w1, moe_w3, moe_w2, 1)
```

```python
import functools
import math

import numpy as np
import jax
import jax.numpy as jnp
from jax import lax
from jax.experimental import pallas as pl
from jax.experimental.pallas import tpu as pltpu

F32, BF16 = jnp.float32, jnp.bfloat16
HI = lax.Precision.HIGHEST

D_MODEL = 1024
GRID_W = 64
EPS = 1e-6
ROPE_BASE = 10000.0
NEG_INF = -1e30
GLA_HEADS, GLA_DK, GLA_DV = 4, 64, 128
GLA_QK, GLA_V = GLA_HEADS * GLA_DK, GLA_HEADS * GLA_DV
GLA_RANK = 16
GLA_TAU = 16.0
GLA_CHUNK = 64
GLA_NB = 4
GLA_LOG_ALPHA_MIN = -1.0
S5_W, S5_H, S5_P = 512, 16, 64
S5_G = S5_W // S5_H
S5_CHUNK = 8
S5_TBLK = 1408
NA_HEADS, NA_DH = 8, 64
NA_W = NA_HEADS * NA_DH
WIN_R, WIN_C = 8, 16
NA_HG = 4
NA_ROWS = 4
HY_W = 512
HY_SHORT = 3
HY_EMB = 33
HY_DECAY_TARGET = 1e-2
HY_DECAY_SHORT_PCT = 0.3
HY_DECAY_LONG_PCT = 1.5
N_GROUPS, EXP_PER_GROUP = 4, 8
N_EXPERTS = N_GROUPS * EXP_PER_GROUP
D_EXPERT = 512
TOP_K = 2
MOE_BLOCK = 256

TB = 256
PROJ_TILE = 768
COMBINE_TILE = 1024
HALO = 16
FFT_N1 = 128
FFT_CG = 32
V7X_VMEM_LIMIT = 52 * 1024 * 1024


def _cparams(sem):
    return pltpu.CompilerParams(dimension_semantics=sem, vmem_limit_bytes=V7X_VMEM_LIMIT)


def _sigmoid(x):
    return 1.0 / (1.0 + jnp.exp(-x))


ROW_TILE = D_MODEL // 128


def _to_token_tiles(ref, val):
    n = val.shape[0]
    for j in range(ROW_TILE):
        ref[pl.ds(j, n, stride=ROW_TILE), :] = val[:, j * 128:(j + 1) * 128]


def _from_token_tiles(ref, n):
    return jnp.concatenate([ref[pl.ds(j, n, stride=ROW_TILE), :] for j in range(ROW_TILE)], axis=1)


def _split_bf16(x):
    hi = x.astype(BF16)
    return hi, (x - hi.astype(F32)).astype(BF16)


def _dot_x3(a, b):
    a_hi, a_lo = _split_bf16(a)
    b_hi, b_lo = _split_bf16(b)
    d = lambda p, q: jnp.dot(p, q, preferred_element_type=F32)
    return d(a_hi, b_hi) + d(a_lo, b_hi) + d(a_hi, b_lo)


def _rms_rows(x):
    return x * lax.rsqrt(jnp.mean(x * x, axis=-1, keepdims=True) + EPS)


def _adaln_kernel(a_ref, w_ref, b_ref, o_ref):
    o_ref[...] = jnp.dot(a_ref[...], w_ref[...], precision=HI, preferred_element_type=F32) + b_ref[...]


def _mod_index(b, i):
    return (b, jnp.minimum(i, 1), 0, 0)


def _swapped_index(nblk, b, i):
    return (b, jnp.where(i == 0, nblk - 1, i - 1), 0)


EV_NQ, EV_NK, EV_NV, EV_NG, EV_NU = 0, 256, 512, 1024, 1536
EV_NA = 2048
EV_NTOT = 2176


def _stream_tile(ctx_ref, x_ref):
    return jnp.where(pl.program_id(1) == 0, ctx_ref[...], x_ref[...])


def _stream_specs():
    return [pl.BlockSpec((None, TB, D_MODEL), lambda b, i: (b, 0, 0)),
            pl.BlockSpec((None, TB, D_MODEL), lambda b, i: (b, jnp.maximum(i - 1, 0), 0))]


def _ev_proj_kernel(ctx_ref, x_ref, mod_ref, w_ref, wa_ref, ba_ref, cos_ref, sin_ref,
                    q_ref, k_ref, v_ref, g_ref, u_ref, usw_ref, la_ref):
    x = _stream_tile(ctx_ref, x_ref)
    h = _rms_rows(x) * (1.0 + mod_ref[1:2, :]) + mod_ref[0:1, :]
    z = jnp.dot(h.astype(BF16), w_ref[...], preferred_element_type=F32)
    lane = lax.broadcasted_iota(jnp.int32, (x.shape[0], GLA_QK), 1)
    first = (lane % 32) < 16
    cos, sin = cos_ref[...], sin_ref[...]

    def rot(t):
        partner = jnp.where(first, pltpu.roll(t, GLA_QK - 16, 1), pltpu.roll(t, 16, 1))
        return t * cos + partner * sin

    q_ref[...] = rot(z[:, EV_NQ:EV_NQ + GLA_QK]) * (GLA_DK ** -0.5)
    k_ref[...] = rot(z[:, EV_NK:EV_NK + GLA_QK])
    v_ref[...] = z[:, EV_NV:EV_NV + GLA_V].astype(BF16)
    g_ref[...] = z[:, EV_NG:EV_NG + GLA_V].astype(BF16)
    u_ref[...] = z[:, EV_NU:EV_NU + S5_W]
    usw_ref[...] = z[:, EV_NU:EV_NU + S5_W]
    a = z[:, EV_NA:EV_NA + 128]
    pre = _dot_x3(a, wa_ref[...]) + ba_ref[...]
    ls = jnp.minimum(pre, 0.0) - jnp.log1p(jnp.exp(-jnp.abs(pre)))
    la_ref[...] = jnp.maximum(ls / GLA_TAU, GLA_LOG_ALPHA_MIN)


def _ev_proj(ctx, x, mods, w, wa, ba, cos, sin):
    bsz = x.shape[0]
    lt = ctx.shape[1] + x.shape[1]
    nblk = lt // TB
    tok = lambda n: pl.BlockSpec((None, TB, n), lambda b, i: (b, i, 0))
    const = lambda shp: pl.BlockSpec(shp, lambda b, i: tuple(0 for _ in shp))
    return pl.pallas_call(
        _ev_proj_kernel,
        grid=(bsz, nblk),
        in_specs=_stream_specs() + [
                  pl.BlockSpec((None, None, 6, D_MODEL), _mod_index),
                  const((D_MODEL, EV_NTOT)), const((128, 2 * GLA_QK)), const((1, 2 * GLA_QK)),
                  pl.BlockSpec((TB, GLA_QK), lambda b, i: (i, 0)),
                  pl.BlockSpec((TB, GLA_QK), lambda b, i: (i, 0))],
        out_specs=[tok(GLA_QK), tok(GLA_QK), tok(GLA_V), tok(GLA_V), tok(S5_W),
                   pl.BlockSpec((None, TB, S5_W), functools.partial(_swapped_index, nblk)), tok(2 * GLA_QK)],
        out_shape=[jax.ShapeDtypeStruct((bsz, lt, GLA_QK), F32),
                   jax.ShapeDtypeStruct((bsz, lt, GLA_QK), F32),
                   jax.ShapeDtypeStruct((bsz, lt, GLA_V), BF16),
                   jax.ShapeDtypeStruct((bsz, lt, GLA_V), BF16),
                   jax.ShapeDtypeStruct((bsz, lt, S5_W), F32),
                   jax.ShapeDtypeStruct((bsz, lt, S5_W), F32),
                   jax.ShapeDtypeStruct((bsz, lt, 2 * GLA_QK), F32)],
        compiler_params=_cparams(("parallel", "parallel")),
        name="ev_proj",
    )(ctx, x, mods, w, wa, ba, cos, sin)


def _gla_kernel(qf_ref, kf_ref, vf_ref, laf_ref, qb_ref, kb_ref, vb_ref, lab_ref,
                of_ref, ob_ref, s_scr):
    i = pl.program_id(1)

    @pl.when(i == 0)
    def _():
        s_scr[...] = jnp.zeros_like(s_scr)

    c = GLA_CHUNK
    nh = GLA_HEADS
    row = lax.broadcasted_iota(jnp.int32, (c, c), 0)
    col = lax.broadcasted_iota(jnp.int32, (c, c), 1)
    row4 = lax.broadcasted_iota(jnp.int32, (nh * c, c), 0) % c
    col4 = lax.broadcasted_iota(jnp.int32, (nh * c, c), 1)
    lane_head = lax.broadcasted_iota(jnp.int32, (c, GLA_QK), 1) // GLA_DK
    out_head = lax.broadcasted_iota(jnp.int32, (c, GLA_V), 1) // GLA_DV
    bd_mask = (lax.broadcasted_iota(jnp.int32, (GLA_V, GLA_QK), 0) // GLA_DV
               == lax.broadcasted_iota(jnp.int32, (GLA_V, GLA_QK), 1) // GLA_DK)
    nb = qf_ref.shape[0]
    nchunk = qf_ref.shape[1] // c
    nt = (((1,), (1,)), ((), ()))
    tn = (((0,), (0,)), ((), ()))

    def one_chunk(refs, o_ref, bb, d, r0):
        q_ref, k_ref, v_ref, la_ref = refs
        fwd = d == 0
        sl = pl.ds(r0, c)
        qc, kc, vc, lac = q_ref[bb, sl, :], k_ref[bb, sl, :], v_ref[bb, sl, :], la_ref[bb, sl, :]
        tri = ((row >= col) if fwd else (row <= col)).astype(BF16)
        la_hi, la_lo = _split_bf16(lac)
        b = (jnp.dot(tri, la_hi, preferred_element_type=F32)
             + jnp.dot(tri, la_lo, preferred_element_type=F32))
        b_last = b[c - 1:c, :] if fwd else b[0:1, :]
        qe = (qc * jnp.exp(b)).astype(BF16)
        ke = (kc * jnp.exp(-b)).astype(BF16)
        kd = (kc * jnp.exp(b_last - b)).astype(BF16)
        st = s_scr[bb, d]
        o = lax.dot_general(qe, st.astype(BF16), nt, preferred_element_type=F32)
        q4 = jnp.concatenate([jnp.where(lane_head == h, qe, jnp.zeros_like(qe)) for h in range(nh)], axis=0)
        att = lax.dot_general(q4, ke, nt, preferred_element_type=F32)
        att_mask = (row4 >= col4) if fwd else (row4 < col4)
        o4 = jnp.dot(jnp.where(att_mask, att, 0.0).astype(BF16), vc, preferred_element_type=F32)
        for h in range(nh):
            o = o + jnp.where(out_head == h, o4[h * c:(h + 1) * c, :], 0.0)
        o_ref[bb, sl, :] = o.astype(o_ref.dtype)
        upd_t = lax.dot_general(vc, kd, tn, preferred_element_type=F32)
        s_scr[bb, d] = st * jnp.exp(b_last) + jnp.where(bd_mask, upd_t, 0.0)

    def body(j, carry):
        for bb in range(nb):
            one_chunk((qf_ref, kf_ref, vf_ref, laf_ref), of_ref, bb, 0, pl.multiple_of(j * c, c))
            one_chunk((qb_ref, kb_ref, vb_ref, lab_ref), ob_ref, bb, 1, pl.multiple_of((nchunk - 1 - j) * c, c))
        return carry

    lax.fori_loop(0, nchunk, body, 0, unroll=True)


def _gla(q, k, v, la):
    bsz, lt, _ = q.shape
    nblk = lt // TB
    fwd_map = lambda b, i: (b, i, 0)
    bwd_blk = lambda i: jnp.where(i == 0, 0, nblk - i)
    bwd_map = lambda b, i: (b, bwd_blk(i), 0)
    bwd_map_la = lambda b, i: (b, bwd_blk(i), 1)
    nb = GLA_NB if bsz % GLA_NB == 0 else 1
    spec = lambda n, m: pl.BlockSpec((nb, TB, n), m)
    return pl.pallas_call(
        _gla_kernel,
        grid=(bsz // nb, nblk),
        in_specs=[spec(GLA_QK, fwd_map), spec(GLA_QK, fwd_map), spec(GLA_V, fwd_map), spec(GLA_QK, fwd_map),
                  spec(GLA_QK, bwd_map), spec(GLA_QK, bwd_map), spec(GLA_V, bwd_map), spec(GLA_QK, bwd_map_la)],
        out_specs=[spec(GLA_V, fwd_map), spec(GLA_V, bwd_map)],
        out_shape=[jax.ShapeDtypeStruct((bsz, lt, GLA_V), BF16), jax.ShapeDtypeStruct((bsz, lt, GLA_V), BF16)],
        scratch_shapes=[pltpu.VMEM((nb, 2, GLA_V, GLA_QK), F32)],
        compiler_params=_cparams(("parallel", "arbitrary")),
        name="gla_scan",
    )(q, k, v, la, q, k, v, la)


def _s5_kernel(u_ref, wm_ref, tm_ref, cm_ref, ar_ref, ai_ref, y_ref, w_scr, hp_scr, h_scr, *, rev):
    @pl.when(pl.program_id(1) == 0)
    def _():
        h_scr[...] = jnp.zeros_like(h_scr)

    bsz, ntok, _ = u_ref.shape
    nc = ntok // S5_CHUNK
    half = 8 * S5_P
    x = jnp.concatenate(
        [jnp.concatenate([u_ref[b, pl.ds(s, nc, stride=S5_CHUNK), :] for s in range(S5_CHUNK)], axis=1)
         for b in range(bsz)], axis=0).astype(BF16)
    w_scr[...] = jnp.dot(x, wm_ref[...], preferred_element_type=F32)
    ar, ai = ar_ref[...], ai_ref[...]

    def body(j, hs):
        c = (nc - 1 - j) if rev else j
        out = []
        for b in range(bsz):
            re, im = hs[b]
            r = b * nc + c
            hp_scr[pl.ds(r, 1), :] = jnp.concatenate([re, im], axis=1)
            w = w_scr[pl.ds(r, 1), :]
            out.append((ar * re - ai * im + w[:, :half], ar * im + ai * re + w[:, half:]))
        return tuple(out)

    hs = lax.fori_loop(0, nc, body, tuple((h_scr[b:b + 1, :half], h_scr[b:b + 1, half:]) for b in range(bsz)))
    for b in range(bsz):
        h_scr[b:b + 1, :] = jnp.concatenate(hs[b], axis=1)
    mt = 256
    ntile = x.shape[1] // mt
    cols = []
    for jt in range(ntile):
        acc = None
        for it in (range(jt, ntile) if rev else range(jt + 1)):
            part = jnp.dot(x[:, it * mt:(it + 1) * mt], tm_ref[it * mt:(it + 1) * mt, jt * mt:(jt + 1) * mt],
                           preferred_element_type=F32)
            acc = part if acc is None else acc + part
        cols.append(acc)
    y = (jnp.concatenate(cols, axis=1)
         + jnp.dot(hp_scr[...].astype(BF16), cm_ref[...], preferred_element_type=F32))
    for b in range(bsz):
        for s in range(S5_CHUNK):
            y_ref[b, pl.ds(s, nc, stride=S5_CHUNK), :] = y[b * nc:(b + 1) * nc, s * 128:(s + 1) * 128]


def _s5_mats(lam_re, lam_im, log_dt, b_re, b_im, c_re, c_im, rev):
    t16 = S5_CHUNK
    dt = jnp.exp(log_dt)[:, None]
    mag = jnp.exp(lam_re * dt)
    a_re, a_im = mag * jnp.cos(lam_im * dt), mag * jnp.sin(lam_im * dt)
    den = lam_re * lam_re + lam_im * lam_im
    nr = a_re - 1.0
    co_re = ((nr * lam_re + a_im * lam_im) / den)[..., None]
    co_im = ((a_im * lam_re - nr * lam_im) / den)[..., None]
    bb_re, bb_im = co_re * b_re - co_im * b_im, co_re * b_im + co_im * b_re
    pr, pi = [jnp.ones_like(a_re)], [jnp.zeros_like(a_im)]
    for _ in range(t16):
        pr, pi = pr + [pr[-1] * a_re - pi[-1] * a_im], pi + [pr[-1] * a_im + pi[-1] * a_re]
    pw_re, pw_im = jnp.stack(pr), jnp.stack(pi)
    g = lam_re.shape[0]
    e_re, e_im = pw_re[t16 - 1::-1][:t16], pw_im[t16 - 1::-1][:t16]
    wre = jnp.einsum('sgp,gph->gshp', e_re, bb_re) - jnp.einsum('sgp,gph->gshp', e_im, bb_im)
    wim = jnp.einsum('sgp,gph->gshp', e_re, bb_im) + jnp.einsum('sgp,gph->gshp', e_im, bb_re)
    cb_re = jnp.einsum('gkp,gph->gpkh', c_re, bb_re) - jnp.einsum('gkp,gph->gpkh', c_im, bb_im)
    cb_im = jnp.einsum('gkp,gph->gpkh', c_re, bb_im) + jnp.einsum('gkp,gph->gpkh', c_im, bb_re)
    kd = jnp.einsum('dgp,gpkh->dgkh', pw_re[:t16], cb_re) - jnp.einsum('dgp,gpkh->dgkh', pw_im[:t16], cb_im)
    lag = np.arange(t16)[None, :] - np.arange(t16)[:, None]
    toe = jnp.where((lag >= 0)[:, :, None, None, None], kd[np.clip(lag, 0, t16 - 1)], 0.0)
    toe = toe.transpose(2, 0, 4, 1, 3)
    q_re, q_im = pw_re[1:], pw_im[1:]
    ca_re = jnp.einsum('gkp,tgp->gptk', c_re, q_re) - jnp.einsum('gkp,tgp->gptk', c_im, q_im)
    ca_im = jnp.einsum('gkp,tgp->gptk', c_re, q_im) + jnp.einsum('gkp,tgp->gptk', c_im, q_re)
    if rev:
        wre, wim = wre[:, ::-1], wim[:, ::-1]
        toe = toe[:, ::-1, :, ::-1]
        ca_re, ca_im = ca_re[:, :, ::-1], ca_im[:, :, ::-1]
    nq, gl = g // 8, 8
    nd = t16 * S5_H
    tok_hot = np.zeros((gl, nd, t16 * 128), np.float32)
    st_hot = np.zeros((gl, 2 * S5_P, 2 * gl * S5_P), np.float32)
    for gi in range(gl):
        a = np.arange(nd)
        tok_hot[gi, a, (a // S5_H) * 128 + gi * S5_H + a % S5_H] = 1.0
        a = np.arange(2 * S5_P)
        st_hot[gi, a, (a // S5_P) * gl * S5_P + gi * S5_P + a % S5_P] = 1.0
    place = lambda rows, blk, cols: jnp.einsum('gar,qgab,gbc->qrc', rows, blk, cols)
    wg = jnp.concatenate([wre, wim], axis=-1).reshape(nq, gl, nd, 2 * S5_P)
    tg = toe.reshape(nq, gl, nd, nd)
    cg = jnp.concatenate([ca_re, -ca_im], axis=1).reshape(nq, gl, 2 * S5_P, nd)
    wm = place(tok_hot, wg, st_hot)
    tmat = place(tok_hot, tg, tok_hot)
    cm = place(st_hot, cg, tok_hot)
    return (wm.astype(BF16), tmat.astype(BF16), cm.astype(BF16),
            pw_re[t16].reshape(nq, 1, gl * S5_P), pw_im[t16].reshape(nq, 1, gl * S5_P))


def _s5_scan(u, mats, rev):
    bsz, lt, _ = u.shape
    tblk = max(d for d in range(64, S5_TBLK + 1, 64) if lt % d == 0)
    nblk = lt // tblk
    nq = S5_G // 8
    wm, tmat, cm, ar, ai = mats
    kw, ks = S5_CHUNK * 128, 8 * S5_P
    tmap = (lambda q, t: (0, nblk - 1 - t, q)) if rev else (lambda q, t: (0, t, q))
    per = lambda shp: pl.BlockSpec((None,) + shp, lambda q, t: (q,) + tuple(0 for _ in shp))
    rows = bsz * tblk // S5_CHUNK
    return pl.pallas_call(
        functools.partial(_s5_kernel, rev=rev),
        grid=(nq, nblk),
        in_specs=[pl.BlockSpec((bsz, tblk, 128), tmap),
                  per((kw, 2 * ks)), per((kw, kw)), per((2 * ks, kw)), per((1, ks)), per((1, ks))],
        out_specs=pl.BlockSpec((bsz, tblk, 128), tmap),
        out_shape=jax.ShapeDtypeStruct((bsz, lt, S5_W), F32),
        scratch_shapes=[pltpu.VMEM((rows, 2 * ks), F32), pltpu.VMEM((rows, 2 * ks), F32),
                        pltpu.VMEM((8, 2 * ks), F32)],
        compiler_params=_cparams(("parallel", "arbitrary")),
        name="s5_scan",
    )(u, wm, tmat, cm, ar, ai)


def _mixer_tail(x, mix, mods, wr_ref, br_ref, x_out, h2_out, lg_out):
    gate, shift, scale = mods
    xn = x + gate * mix
    x_out[...] = xn
    h2 = _rms_rows(xn) * (1.0 + scale) + shift
    _to_token_tiles(h2_out, h2)
    lg_out[...] = _route_tail(_dot_x3(h2, wr_ref[...]) + br_ref[...])


def _route_tail(lg):
    lane = lax.broadcasted_iota(jnp.int32, lg.shape, 1)
    lane_f = lane.astype(F32)
    big = jnp.float32(128.0)
    row_max = lambda t: jnp.max(t, axis=1, keepdims=True)
    first_lane = lambda hit: jnp.min(jnp.where(hit, lane_f, big), axis=1, keepdims=True)
    is_g = lane < N_GROUPS
    g_logit = jnp.where(is_g, lg, NEG_INF)
    g_max = row_max(g_logit)
    grp = first_lane(g_logit == g_max)
    g_w = 1.0 / jnp.sum(jnp.where(is_g, jnp.exp(lg - g_max), 0.0), axis=1, keepdims=True)
    lo = N_GROUPS + EXP_PER_GROUP * grp
    in_grp = jnp.logical_and(lane_f >= lo, lane_f < lo + EXP_PER_GROUP)
    e_max = row_max(jnp.where(in_grp, lg, NEG_INF))
    pe = jnp.where(in_grp, jnp.exp(lg - e_max), 0.0)
    p = pe / jnp.sum(pe, axis=1, keepdims=True)
    cand = jnp.where(in_grp, p, -1.0)
    p1 = row_max(cand)
    i1 = first_lane(cand == p1)
    cand2 = jnp.where(lane_f == i1, -1.0, cand)
    p2 = row_max(cand2)
    i2 = first_lane(cand2 == p2)
    scale = g_w / (p1 + p2)
    out = jnp.where(lane == 0, scale * p1, jnp.where(lane == 1, scale * p2,
                    jnp.where(lane == 2, i1 - N_GROUPS, jnp.where(lane == 3, i2 - N_GROUPS, 0.0))))
    return out


def _ev_out_kernel(of_ref, ob_ref, g_ref, yf_ref, u_ref, ctx_ref, mod_ref,
                   gn_ref, ds_ref, wglu_ref, wout_ref, wr_ref, br_ref, *rest, nsub):
    yb_refs, x_refs = rest[:nsub], rest[nsub:2 * nsub]
    x_out, h2_out, lg_out, wglu_b, wout_b = rest[2 * nsub:]
    i = pl.program_id(1)

    @pl.when(jnp.logical_and(pl.program_id(0) == 0, i == 0))
    def _():
        wglu_b[...] = wglu_ref[...].astype(BF16)
        wout_b[...] = wout_ref[...].astype(BF16)

    tm = nsub * TB
    yb = jnp.concatenate([r[...] for r in yb_refs], axis=0)
    x = jnp.concatenate([jnp.where(nsub * i + k == 0, ctx_ref[...], r[...]) for k, r in enumerate(x_refs)], axis=0)
    is_ctx = i * tm + lax.broadcasted_iota(jnp.int32, (tm, 1), 0) < TB
    mod = lambda r: jnp.where(is_ctx, mod_ref[0, r:r + 1, :], mod_ref[1, r:r + 1, :])
    o = of_ref[...].astype(F32) + ob_ref[...].astype(F32)
    og = jnp.concatenate([_rms_rows(o[:, h * GLA_DV:(h + 1) * GLA_DV]) for h in range(GLA_HEADS)], axis=1)
    g = g_ref[...].astype(F32)
    og = og * gn_ref[...] * (g * _sigmoid(g))
    t = yf_ref[...] + yb + ds_ref[...] * u_ref[...]
    y = t * (0.5 * (1.0 + jnp.tanh(math.sqrt(2.0 / math.pi) * (t + 0.044715 * (t * t * t)))))
    y = y * _sigmoid(jnp.dot(y.astype(BF16), wglu_b[...], preferred_element_type=F32))
    cat = jnp.concatenate([og, y], axis=1).astype(BF16)
    mix = jnp.dot(cat, wout_b[...], preferred_element_type=F32)
    _mixer_tail(x, mix, (mod(2), mod(3), mod(4)), wr_ref, br_ref, x_out, h2_out, lg_out)


def _ev_out(o_f, o_b, g, y_f, y_b, u, ctx, x, mods, gn, ds, wglu, wout, wr, br):
    bsz, lt, _ = u.shape
    nstream = lt // TB
    nsub = 3 if nstream % 3 == 0 else 1
    tm = nsub * TB
    nblk = lt // tm
    tok = lambda n: pl.BlockSpec((None, tm, n), lambda b, i: (b, i, 0))
    const = lambda shp: pl.BlockSpec(shp, lambda b, i: tuple(0 for _ in shp))
    sub = lambda n, fn: [pl.BlockSpec((None, TB, n), functools.partial(fn, k)) for k in range(nsub)]
    yb_map = lambda k, b, i: _swapped_index(nstream, b, nsub * i + k)
    x_map = lambda k, b, i: (b, jnp.maximum(nsub * i + k - 1, 0), 0)
    return pl.pallas_call(
        functools.partial(_ev_out_kernel, nsub=nsub),
        grid=(bsz, nblk),
        in_specs=[tok(512), tok(512), tok(512), tok(512), tok(512),
                  pl.BlockSpec((None, TB, D_MODEL), lambda b, i: (b, 0, 0)),
                  pl.BlockSpec((None, 2, 6, D_MODEL), lambda b, i: (b, 0, 0, 0)),
                  const((1, 512)), const((1, 512)), const((512, 512)), const((D_MODEL, D_MODEL)),
                  const((D_MODEL, 128)), const((1, 128))] + sub(S5_W, yb_map) + sub(D_MODEL, x_map),
        out_specs=[tok(D_MODEL), pl.BlockSpec((tm * ROW_TILE, 128), lambda b, i: (b * nblk + i, 0)), tok(128)],
        out_shape=[jax.ShapeDtypeStruct((bsz, lt, D_MODEL), F32),
                   jax.ShapeDtypeStruct((bsz * lt * ROW_TILE, 128), F32),
                   jax.ShapeDtypeStruct((bsz, lt, 128), F32)],
        scratch_shapes=[pltpu.VMEM((S5_W, S5_W), BF16), pltpu.VMEM((D_MODEL, D_MODEL), BF16)],
        compiler_params=_cparams(("arbitrary", "arbitrary")),
        name="ev_out",
    )(o_f, o_b, g, y_f, u, ctx, mods, gn, ds, wglu, wout, wr, br, *([y_b] * nsub), *([x] * nsub))


def _route(route):
    n_tok = route.shape[0]
    gate = route[:, :TOP_K]
    eid = route[:, TOP_K:2 * TOP_K].astype(jnp.int32).reshape(-1)
    n_asg = n_tok * TOP_K
    order = jnp.argsort(eid).astype(jnp.int32)
    counts = jnp.sum((eid[:, None] == jnp.arange(N_EXPERTS)[None, :]).astype(jnp.int32), axis=0)
    padded = (counts + MOE_BLOCK - 1) // MOE_BLOCK * MOE_BLOCK
    pad_end = jnp.cumsum(padded)
    pad_start = pad_end - padded
    cnt_start = jnp.cumsum(counts) - counts
    n_blocks = -(-n_asg // MOE_BLOCK) + N_EXPERTS
    blk_start = jnp.arange(n_blocks, dtype=jnp.int32) * MOE_BLOCK
    blk_e = jnp.minimum(jnp.sum((pad_end[None, :] <= blk_start[:, None]).astype(jnp.int32), axis=1), N_EXPERTS - 1)
    pos = jnp.arange(n_blocks * MOE_BLOCK, dtype=jnp.int32)
    pos_e = jnp.repeat(blk_e, MOE_BLOCK)
    rank = pos - pad_start[pos_e]
    src = jnp.clip(cnt_start[pos_e] + rank, 0, n_asg - 1)
    slot_buf = jnp.where(rank < counts[pos_e], order[src], n_asg).astype(jnp.int32)
    n_valid = jnp.sum((slot_buf < n_asg).reshape(n_blocks, MOE_BLOCK), axis=1).astype(jnp.int32)
    tok = lax.shift_right_logical(slot_buf, 1)
    src_rows = jnp.minimum(tok, n_tok - 1) * ROW_TILE
    spare = n_asg + (pos // MOE_BLOCK % 2) * MOE_BLOCK + pos % MOE_BLOCK
    dst_rows = jnp.where(slot_buf < n_asg, (slot_buf & 1) * n_tok + tok, spare)
    lead = n_asg + jnp.arange(2 * MOE_BLOCK, dtype=jnp.int32)
    dst_rows = jnp.concatenate([lead, dst_rows]) * ROW_TILE
    return src_rows, dst_rows, blk_e, n_valid, gate.astype(F32)


def _moe_kernel(src_ref, dst_ref, blke_ref, nvalid_ref, h_hbm, w1_ref, w3_ref, w2_ref, z_hbm,
                xbuf0, xbuf1, ybuf0, ybuf1, wb1, wb3, wb2, gsem, ssem):
    i = pl.program_id(0)
    nblk = pl.num_programs(0)
    ns = ROW_TILE
    xb, yb = (xbuf0, xbuf1), (ybuf0, ybuf1)
    lead = 2 * MOE_BLOCK

    def issue_gather(blk, buf):
        base = blk * MOE_BLOCK
        for r in range(MOE_BLOCK):
            src = pl.multiple_of(src_ref[base + r], ns)
            pltpu.make_async_copy(h_hbm.at[pl.ds(src, ns)], xb[buf].at[pl.ds(r * ns, ns)],
                                  gsem.at[buf]).start(priority=r % 2)

    def issue_scatter(blk, buf):
        base = lead + blk * MOE_BLOCK
        for r in range(MOE_BLOCK):
            dst = pl.multiple_of(dst_ref[base + r], ns)
            pltpu.make_async_copy(yb[buf].at[pl.ds(r * ns, ns)], z_hbm.at[pl.ds(dst, ns)],
                                  ssem.at[buf]).start(priority=r % 2)

    def wait_gather(buf):
        pltpu.make_async_copy(h_hbm.at[pl.ds(0, MOE_BLOCK * ns)], xb[buf], gsem.at[buf]).wait()

    def wait_scatter(buf):
        pltpu.make_async_copy(yb[buf], z_hbm.at[pl.ds(0, MOE_BLOCK * ns)], ssem.at[buf]).wait()

    used = nvalid_ref[i] > 0

    @pl.when(i == 0)
    def _():
        ybuf0[...] = jnp.zeros_like(ybuf0)
        ybuf1[...] = jnp.zeros_like(ybuf1)
        issue_scatter(-2, 0)
        issue_gather(0, 0)

    def step(cur):
        oth = 1 - cur
        wait_gather(cur)
        issue_gather(jnp.minimum(i + 1, nblk - 1), oth)
        issue_scatter(i - 1, oth)
        x = _from_token_tiles(xb[cur], MOE_BLOCK).astype(BF16)

        @pl.when(jnp.logical_or(i == 0, blke_ref[i] != blke_ref[jnp.maximum(i - 1, 0)]))
        def _():
            wb1[...] = w1_ref[...].astype(BF16)
            wb3[...] = w3_ref[...].astype(BF16)
            wb2[...] = w2_ref[...].astype(BF16)

        h1 = jnp.dot(x, wb1[...], preferred_element_type=F32)
        h3 = jnp.dot(x, wb3[...], preferred_element_type=F32)
        a = (h1 * _sigmoid(h1) * h3).astype(BF16)
        y = jnp.dot(a, wb2[...], preferred_element_type=F32)
        wait_scatter(cur)
        _to_token_tiles(yb[cur], y)

    def drain(last_par):
        wait_gather(1 - last_par)
        issue_scatter(jnp.where(used, i, i - 1), last_par)
        wait_scatter(1 - last_par)
        wait_scatter(last_par)

    for par in range(2):
        @pl.when(jnp.logical_and(used, i % 2 == par))
        def _():
            step(par)

    first_unused = jnp.logical_and(jnp.logical_not(used),
                                   jnp.logical_and(i > 0, nvalid_ref[jnp.maximum(i - 1, 0)] > 0))
    last_used = jnp.logical_and(used, i == nblk - 1)
    for par in range(2):
        @pl.when(jnp.logical_or(jnp.logical_and(first_unused, (i - 1) % 2 == par),
                                jnp.logical_and(last_used, i % 2 == par)))
        def _():
            drain(par)


def _moe_experts(h2, src_rows, dst_rows, blk_e, n_valid, w1, w3, w2, layer):
    n_tok = h2.shape[0] // ROW_TILE
    n_blocks = blk_e.shape[0]
    wspec = lambda shp: pl.BlockSpec((None, None) + shp, lambda i, src, dst, blke, nvalid: (layer, blke[i], 0, 0))
    grid_spec = pltpu.PrefetchScalarGridSpec(
        num_scalar_prefetch=4,
        grid=(n_blocks,),
        in_specs=[pl.BlockSpec(memory_space=pl.ANY),
                  wspec((D_MODEL, D_EXPERT)), wspec((D_MODEL, D_EXPERT)), wspec((D_EXPERT, D_MODEL))],
        out_specs=pl.BlockSpec(memory_space=pl.ANY),
        scratch_shapes=[pltpu.VMEM((MOE_BLOCK * ROW_TILE, 128), F32), pltpu.VMEM((MOE_BLOCK * ROW_TILE, 128), F32),
                        pltpu.VMEM((MOE_BLOCK * ROW_TILE, 128), F32), pltpu.VMEM((MOE_BLOCK * ROW_TILE, 128), F32),
                        pltpu.VMEM((D_MODEL, D_EXPERT), BF16), pltpu.VMEM((D_MODEL, D_EXPERT), BF16),
                        pltpu.VMEM((D_EXPERT, D_MODEL), BF16),
                        pltpu.SemaphoreType.DMA((2,)), pltpu.SemaphoreType.DMA((2,))])
    return pl.pallas_call(
        _moe_kernel,
        grid_spec=grid_spec,
        out_shape=jax.ShapeDtypeStruct(((TOP_K * n_tok + 2 * MOE_BLOCK) * ROW_TILE, 128), F32),
        compiler_params=_cparams(("arbitrary",)),
        name="moe_experts",
    )(src_rows, dst_rows, blk_e, n_valid, h2, w1, w3, w2)


def _moe_combine_kernel(x_ref, z0_ref, z1_ref, gate_ref, mod_ref, o_ref, *, n_ctx):
    gate = gate_ref[...]
    tm = x_ref.shape[0]
    y = gate[:, 0:1] * _from_token_tiles(z0_ref, tm) + gate[:, 1:2] * _from_token_tiles(z1_ref, tm)
    g_out = mod_ref[1, 5:6, :]
    if n_ctx:
        is_ctx = pl.program_id(1) * tm + lax.broadcasted_iota(jnp.int32, (tm, 1), 0) < n_ctx
        g_out = jnp.where(is_ctx, mod_ref[0, 5:6, :], g_out)
    o_ref[...] = x_ref[...] + g_out * y


def _moe_combine(x, z, gate, mods, n_ctx):
    bsz, lt, _ = x.shape
    tm = max(d for d in range(TB, COMBINE_TILE + 1, TB) if lt % d == 0)
    nblk = lt // tm
    gate3 = gate.reshape(bsz, lt, TOP_K)
    return pl.pallas_call(
        functools.partial(_moe_combine_kernel, n_ctx=n_ctx),
        grid=(bsz, nblk),
        in_specs=[pl.BlockSpec((None, tm, D_MODEL), lambda b, i: (b, i, 0)),
                  pl.BlockSpec((tm * ROW_TILE, 128), lambda b, i: (b * nblk + i, 0)),
                  pl.BlockSpec((tm * ROW_TILE, 128), lambda b, i: ((bsz + b) * nblk + i, 0)),
                  pl.BlockSpec((None, tm, TOP_K), lambda b, i: (b, i, 0)),
                  pl.BlockSpec((None, 2, 6, D_MODEL), lambda b, i: (b, 0, 0, 0))],
        out_specs=pl.BlockSpec((None, tm, D_MODEL), lambda b, i: (b, i, 0)),
        out_shape=jax.ShapeDtypeStruct((bsz, lt, D_MODEL), F32),
        compiler_params=_cparams(("parallel", "parallel")),
        name="moe_combine",
    )(x, z, z, gate3, mods)


def _moe(x, h2, logits, mods, n_ctx, w1, w3, w2, layer):
    bsz, lt, _ = x.shape
    src_rows, dst_rows, blk_e, n_valid, gate = _route(logits.reshape(bsz * lt, 128))
    z = _moe_experts(h2, src_rows, dst_rows, blk_e, n_valid, w1, w3, w2, layer)
    return _moe_combine(x, z, gate, mods, n_ctx)


def _od_proj_kernel(x_ref, mod_ref, w_ref, gm_ref, qn_ref, kn_ref, q_ref, k_ref, v_ref, zh_ref):
    tm = x_ref.shape[0]
    is_ctx = pl.program_id(1) * tm + lax.broadcasted_iota(jnp.int32, (tm, 1), 0) < TB
    shift = jnp.where(is_ctx, mod_ref[0, 0:1, :], mod_ref[1, 0:1, :])
    scale = jnp.where(is_ctx, mod_ref[0, 1:2, :], mod_ref[1, 1:2, :])
    h = _rms_rows(x_ref[...]) * (1.0 + scale) + shift
    z = jnp.dot(h.astype(BF16), w_ref[...], preferred_element_type=F32)

    def head_norm(t, gain):
        sq_hi, sq_lo = _split_bf16(t * t)
        gm = gm_ref[...].astype(BF16)
        ms = jnp.dot(sq_hi, gm, preferred_element_type=F32) + jnp.dot(sq_lo, gm, preferred_element_type=F32)
        return t * lax.rsqrt(ms + EPS) * gain

    q_ref[...] = (head_norm(z[:, :NA_W], qn_ref[...]) * (NA_DH ** -0.5)).astype(BF16)
    k_ref[...] = head_norm(z[:, NA_W:2 * NA_W], kn_ref[...]).astype(BF16)
    v_ref[...] = z[:, 2 * NA_W:3 * NA_W].astype(BF16)
    zh_ref[...] = z[:, 3 * NA_W:].astype(BF16)


def _od_proj(xcat, mods, w, gm, qn, kn):
    bsz, lt, _ = xcat.shape
    tm = max(d for d in range(TB, PROJ_TILE + 1, TB) if lt % d == 0)
    nblk = lt // tm
    tok = lambda n: pl.BlockSpec((None, tm, n), lambda b, i: (b, i, 0))
    const = lambda shp: pl.BlockSpec(shp, lambda b, i: tuple(0 for _ in shp))
    return pl.pallas_call(
        _od_proj_kernel,
        grid=(bsz, nblk),
        in_specs=[tok(D_MODEL), pl.BlockSpec((None, 2, 6, D_MODEL), lambda b, i: (b, 0, 0, 0)),
                  const((D_MODEL, 3 * NA_W + 3 * HY_W)), const((NA_W, NA_W)), const((1, NA_W)), const((1, NA_W))],
        out_specs=[tok(NA_W), tok(NA_W), tok(NA_W), tok(3 * HY_W)],
        out_shape=[jax.ShapeDtypeStruct((bsz, lt, NA_W), BF16), jax.ShapeDtypeStruct((bsz, lt, NA_W), BF16),
                   jax.ShapeDtypeStruct((bsz, lt, NA_W), BF16), jax.ShapeDtypeStruct((bsz, lt, 3 * HY_W), BF16)],
        compiler_params=_cparams(("parallel", "parallel")),
        name="od_proj",
    )(xcat, mods, w, gm, qn, kn)


def _na_kernel(q_ref, k_ref, v_ref, t2_ref, o_ref):
    for j in range(NA_ROWS):
        _na_one_row(q_ref, k_ref, v_ref, t2_ref, o_ref, j)


def _na_one_row(q_ref, k_ref, v_ref, t2_ref, o_ref, j):
    r = pl.program_id(1) * NA_ROWS + j
    n_rows = pl.num_programs(1) * NA_ROWS
    r0 = jnp.clip(r - WIN_R // 2, 0, n_rows - WIN_R)
    off = r0 - r + WIN_R - 1
    base = pl.multiple_of(TB + r0 * GRID_W, GRID_W)
    nloc = WIN_R * GRID_W
    q = q_ref[j * GRID_W:(j + 1) * GRID_W, :]
    hg = NA_HG
    gw = hg * NA_DH
    lane_head = lax.broadcasted_iota(jnp.int32, (GRID_W, gw), 1) // NA_DH
    nt = (((1,), (1,)), ((), ()))
    outs = []
    for grp in range(NA_HEADS // hg):
        cs = slice(gw * grp, gw * (grp + 1))
        q2 = q[:, cs]
        q4 = jnp.concatenate([jnp.where(lane_head == h, q2, jnp.zeros_like(q2)) for h in range(hg)], axis=0)
        kw, vw = k_ref[pl.ds(base, nloc), cs], v_ref[pl.ds(base, nloc), cs]
        kc, vc = k_ref[0:TB, cs], v_ref[0:TB, cs]
        bias = jnp.concatenate(
            [jnp.concatenate([t2_ref[hg * grp + h, off + 2 * m] for m in range(WIN_R // 2)], axis=1)
             for h in range(hg)], axis=0)
        s_loc = lax.dot_general(q4, kw, nt, preferred_element_type=F32) + bias
        s_ctx = lax.dot_general(q4, kc, nt, preferred_element_type=F32)
        m = jnp.maximum(jnp.max(s_loc, axis=1, keepdims=True), jnp.max(s_ctx, axis=1, keepdims=True))
        p_loc, p_ctx = jnp.exp(s_loc - m), jnp.exp(s_ctx - m)
        den = jnp.sum(p_loc, axis=1, keepdims=True) + jnp.sum(p_ctx, axis=1, keepdims=True)
        o4 = (jnp.dot(p_loc.astype(BF16), vw, preferred_element_type=F32)
              + jnp.dot(p_ctx.astype(BF16), vc, preferred_element_type=F32)) / den
        acc = jnp.zeros((GRID_W, gw), F32)
        for h in range(hg):
            acc = jnp.where(lane_head == h, o4[h * GRID_W:(h + 1) * GRID_W, :], acc)
        outs.append(acc)
    o_ref[j * GRID_W:(j + 1) * GRID_W, :] = jnp.concatenate(outs, axis=1)


def _na_bias_table(rpb):
    qc = np.arange(GRID_W)[:, None]
    kc = np.arange(GRID_W)[None, :]
    q_start = np.clip(qc - WIN_C // 2, 0, GRID_W - WIN_C)
    valid = (kc >= q_start) & (kc < q_start + WIN_C)
    col_idx = np.clip(kc - qc + WIN_C - 1, 0, 2 * WIN_C - 2)
    hot = np.zeros((GRID_W, GRID_W, 2 * WIN_C - 1), np.float32)
    hot[qc, kc, col_idx] = 1.0
    t = jnp.einsum('hrj,qkj->hrqk', rpb.astype(F32), hot, precision=HI)
    t = jnp.where(valid[None, None], t, NEG_INF)
    return jnp.concatenate([t[:, :-1], t[:, 1:]], axis=-1)


def _na(q, k, v, t2):
    bsz, lt, _ = q.shape
    qrows = NA_ROWS * GRID_W
    n_rows = (lt - TB) // qrows
    qoff = TB // qrows
    return pl.pallas_call(
        _na_kernel,
        grid=(bsz, n_rows),
        in_specs=[pl.BlockSpec((None, qrows, NA_W), lambda b, r: (b, r + qoff, 0)),
                  pl.BlockSpec((None, lt, NA_W), lambda b, r: (b, 0, 0)),
                  pl.BlockSpec((None, lt, NA_W), lambda b, r: (b, 0, 0)),
                  pl.BlockSpec(t2.shape, lambda b, r: (0, 0, 0, 0))],
        out_specs=pl.BlockSpec((None, qrows, NA_W), lambda b, r: (b, r, 0)),
        out_shape=jax.ShapeDtypeStruct((bsz, lt - TB, NA_W), F32),
        compiler_params=_cparams(("parallel", "arbitrary")),
        name="na_attn",
    )(q, k, v, t2)


def _hy_pre_kernel(za_ref, zb_ref, zp_ref, zn_ref, cw_ref, cb_ref, x0_ref, u_ref, ut_ref):
    i = pl.program_id(1)
    n = pl.num_programs(1)
    z = jnp.concatenate([za_ref[...], zb_ref[...]], axis=0).astype(F32)
    tb = z.shape[0]
    prev_row = jnp.where(i > 0, zp_ref[HALO - 1:HALO, :].astype(F32), 0.0)
    next_row = jnp.where(i < n - 1, zn_ref[0:1, :].astype(F32), 0.0)
    rowid = lax.broadcasted_iota(jnp.int32, z.shape, 0)
    zm = jnp.where(rowid == 0, prev_row, pltpu.roll(z, 1, 0))
    zp = jnp.where(rowid == tb - 1, next_row, pltpu.roll(z, tb - 1, 0))
    zc = cb_ref[...] + cw_ref[0:1, :] * zm
    zc = zc + cw_ref[1:2, :] * z
    zc = zc + cw_ref[2:3, :] * zp
    x0_ref[...] = zc[:, :HY_W].astype(BF16)
    u = zc[:, HY_W:2 * HY_W] * zc[:, 2 * HY_W:]
    u_ref[...] = u.astype(BF16)
    for j in range(tb // FFT_N1):
        ut_ref[j] = u[j * FFT_N1:(j + 1) * FFT_N1, :].T.astype(BF16)


def _hy_pre(zh, cw, cb):
    bsz, lt, _ = zh.shape
    l = lt - TB
    tm = 2 * TB
    nblk = l // tm
    hpt = TB // HALO
    return pl.pallas_call(
        _hy_pre_kernel,
        grid=(bsz, nblk),
        in_specs=[pl.BlockSpec((None, TB, 3 * HY_W), lambda b, i: (b, 2 * i + 1, 0)),
                  pl.BlockSpec((None, TB, 3 * HY_W), lambda b, i: (b, 2 * i + 2, 0)),
                  pl.BlockSpec((None, HALO, 3 * HY_W), lambda b, i: (b, (2 * i + 1) * hpt - 1, 0)),
                  pl.BlockSpec((None, HALO, 3 * HY_W),
                               lambda b, i: (b, jnp.minimum((2 * i + 3) * hpt, lt // HALO - 1), 0)),
                  pl.BlockSpec((HY_SHORT, 3 * HY_W), lambda b, i: (0, 0)),
                  pl.BlockSpec((1, 3 * HY_W), lambda b, i: (0, 0))],
        out_specs=[pl.BlockSpec((None, tm, HY_W), lambda b, i: (b, i, 0)),
                   pl.BlockSpec((None, tm, HY_W), lambda b, i: (b, i, 0)),
                   pl.BlockSpec((None, tm // FFT_N1, HY_W, FFT_N1), lambda b, i: (b, i, 0, 0))],
        out_shape=[jax.ShapeDtypeStruct((bsz, l, HY_W), BF16), jax.ShapeDtypeStruct((bsz, l, HY_W), BF16),
                   jax.ShapeDtypeStruct((bsz, l // FFT_N1, HY_W, FFT_N1), BF16)],
        compiler_params=_cparams(("parallel", "parallel")),
        name="hy_pre",
    )(zh, zh, zh, zh, cw, cb)


def _fft_consts(n1_in):
    n = FFT_N1
    idx = np.arange(n)
    ang1 = 2.0 * np.pi * np.outer(idx, idx) / n
    c, s = np.cos(ang1), np.sin(ang1)
    angt = 2.0 * np.pi * np.outer(idx, idx) / (n * n)
    tw = np.concatenate([np.cos(angt), -np.sin(angt)], axis=1)
    f3 = np.block([[c, -s], [s, c]])
    f3i = np.block([[c, s], [-s, c]])
    ch, sh = c[:, :n1_in], s[:, :n1_in]
    f1_pair = np.block([[ch, sh], [-sh, ch]])
    f1_real = np.concatenate([c, -s], axis=0)
    f1i = np.block([[ch.T, -sh.T], [sh.T, ch.T]]) / (n * n)
    return tw, f3, f3i, f1_pair, f1_real, f1i


def _fft_forward(a, tw_re, tw_im, lhs_scr, ncg):
    for cix in range(ncg):
        cs = slice(cix * FFT_N1, (cix + 1) * FFT_N1)
        are, aim = a[:FFT_N1, cs], a[FFT_N1:, cs]
        lhs_scr[cs, :FFT_N1] = are * tw_re - aim * tw_im
        lhs_scr[cs, FFT_N1:] = are * tw_im + aim * tw_re


def _hy_filt_kernel(k_ref, f1_ref, tw_ref, f3_ref, o_ref, lhs_scr):
    a = _dot_x3(f1_ref[...], k_ref[...])
    _fft_forward(a, tw_ref[:, :FFT_N1], tw_ref[:, FFT_N1:], lhs_scr, FFT_CG)
    o_ref[...] = _dot_x3(lhs_scr[...], f3_ref[...])


def _hy_fft_kernel(u_ref, kf_ref, f1_ref, tw_ref, f3_ref, f3i_ref, f1i_ref, y_ref, lhs_scr, a2_scr):
    cgl = FFT_CG * FFT_N1
    x = u_ref[...].reshape(2 * u_ref.shape[1], cgl)
    a = jnp.dot(f1_ref[...], x, preferred_element_type=F32)
    tw_re, tw_im = tw_ref[:, :FFT_N1], tw_ref[:, FFT_N1:]
    _fft_forward(a, tw_re, tw_im, lhs_scr, FFT_CG)
    y = jnp.dot(lhs_scr[...].astype(BF16), f3_ref[...], preferred_element_type=F32)
    yre, yim = y[:, :FFT_N1], y[:, FFT_N1:]
    kre, kim = kf_ref[:, :FFT_N1], kf_ref[:, FFT_N1:]
    z = jnp.concatenate([yre * kre - yim * kim, yre * kim + yim * kre], axis=1).astype(BF16)
    bp = jnp.dot(z, f3i_ref[...], preferred_element_type=F32)
    for cix in range(FFT_CG):
        cs = slice(cix * FFT_N1, (cix + 1) * FFT_N1)
        bre, bim = bp[cs, :FFT_N1], bp[cs, FFT_N1:]
        a2_scr[:FFT_N1, cs] = bre * tw_re + bim * tw_im
        a2_scr[FFT_N1:, cs] = bim * tw_re - bre * tw_im
    out = jnp.dot(f1i_ref[...], a2_scr[...].astype(BF16), preferred_element_type=F32)
    y_ref[...] = out.reshape(2, u_ref.shape[1], cgl)


def _hy_conv(ut, kfilt):
    bsz, n1h, nch, _ = ut.shape
    assert 2 * n1h == FFT_N1 and bsz % 2 == 0
    cgl = FFT_CG * FFT_N1
    ncol = nch * FFT_N1
    tw, f3, f3i, f1_pair, f1_real, f1i = _fft_consts(n1h)
    kt = kfilt.reshape(FFT_N1, FFT_N1, nch).transpose(0, 2, 1).reshape(FFT_N1, ncol)
    const2 = lambda shp: pl.BlockSpec(shp, lambda *a: (0, 0))
    kf = pl.pallas_call(
        _hy_filt_kernel,
        grid=(nch // FFT_CG,),
        in_specs=[pl.BlockSpec((FFT_N1, cgl), lambda j: (0, j)), const2((2 * FFT_N1, FFT_N1)),
                  const2((FFT_N1, 2 * FFT_N1)), const2((2 * FFT_N1, 2 * FFT_N1))],
        out_specs=pl.BlockSpec((cgl, 2 * FFT_N1), lambda j: (j, 0)),
        out_shape=jax.ShapeDtypeStruct((ncol, 2 * FFT_N1), F32),
        scratch_shapes=[pltpu.VMEM((cgl, 2 * FFT_N1), F32)],
        compiler_params=_cparams(("parallel",)),
        name="hy_filter_dft",
    )(kt, jnp.asarray(f1_real, F32), jnp.asarray(tw, F32), jnp.asarray(f3, F32))
    u2 = ut.reshape(bsz, n1h, ncol)
    y = pl.pallas_call(
        _hy_fft_kernel,
        grid=(bsz // 2, nch // FFT_CG),
        in_specs=[pl.BlockSpec((2, n1h, cgl), lambda p, j: (p, 0, j)),
                  pl.BlockSpec((cgl, 2 * FFT_N1), lambda p, j: (j, 0)),
                  const2((2 * FFT_N1, FFT_N1)), const2((FFT_N1, 2 * FFT_N1)),
                  const2((2 * FFT_N1, 2 * FFT_N1)), const2((2 * FFT_N1, 2 * FFT_N1)), const2((FFT_N1, 2 * FFT_N1))],
        out_specs=pl.BlockSpec((2, n1h, cgl), lambda p, j: (p, 0, j)),
        out_shape=jax.ShapeDtypeStruct((bsz, n1h, ncol), F32),
        scratch_shapes=[pltpu.VMEM((cgl, 2 * FFT_N1), F32), pltpu.VMEM((2 * FFT_N1, cgl), F32)],
        compiler_params=_cparams(("parallel", "parallel")),
        name="hy_fft_conv",
    )(u2, kf, jnp.asarray(f1_pair, F32).astype(BF16), jnp.asarray(tw, F32), jnp.asarray(f3, F32).astype(BF16),
      jnp.asarray(f3i, F32).astype(BF16), jnp.asarray(f1i, F32).astype(BF16))
    return y.reshape(bsz, n1h, nch, FFT_N1)


def _hy_filter_kernel(f_ref, w1_ref, b1_ref, w2_ref, b2_ref, w3_ref, b3_ref, fr_ref, w4_ref, dl_ref, o_ref):
    f = f_ref[...]
    fr = fr_ref[...]
    h = jnp.sin(fr * (_dot_x3(f, w1_ref[...]) + b1_ref[...]))
    h = jnp.sin(fr * (_dot_x3(h, w2_ref[...]) + b2_ref[...]))
    h = jnp.sin(fr * (_dot_x3(h, w3_ref[...]) + b3_ref[...]))
    out = _dot_x3(h, w4_ref[...])
    tm = f.shape[0]
    row = pl.program_id(0) * tm + lax.broadcasted_iota(jnp.int32, (tm, 1), 0)
    o_ref[0] = out[:, :HY_W] * jnp.exp(-f[:, 0:1] * dl_ref[...])
    o_ref[1] = jnp.where(row == 0, 0.0, out[:, HY_W:] * jnp.exp(-f[:, 128:129] * dl_ref[...]))


def _hy_filter(seqlen, fw1, fb1, fw2, fb2, fw3, fb3, freq, fw4):
    bands = (HY_EMB - 1) // 2
    f = jnp.linspace(1e-4, bands - 1, bands, dtype=F32)[None, :]
    t_f = jnp.linspace(0.0, 1.0, seqlen, dtype=F32)
    pos_f = jnp.arange(seqlen, dtype=F32)
    t_b = jnp.concatenate([t_f[:1], t_f[:0:-1]])
    pos_b = jnp.concatenate([pos_f[:1], pos_f[:0:-1]])

    def features(t, pos):
        w = 2.0 * math.pi * pos[:, None] / seqlen
        feat = jnp.concatenate([t[:, None], jnp.cos(f * w), -jnp.sin(f * w)], axis=-1)
        return jnp.pad(feat, ((0, 0), (0, 128 - HY_EMB)))

    feat2 = jnp.concatenate([features(t_f, pos_f), features(t_b, pos_b)], axis=1)
    nh = fw2.shape[0]
    two = lambda m: jnp.kron(jnp.eye(2, dtype=F32), m.astype(F32))
    w1 = two(jnp.pad(fw1.astype(F32), ((0, 128 - HY_EMB), (0, 0))))
    w4 = jnp.concatenate([jnp.pad(fw4[:, :HY_W].astype(F32), ((0, nh), (0, 0))),
                          jnp.pad(fw4[:, HY_W:].astype(F32), ((nh, 0), (0, 0)))], axis=1)
    deltas = jnp.abs(jnp.linspace(math.log(HY_DECAY_TARGET) / HY_DECAY_LONG_PCT,
                                  math.log(HY_DECAY_TARGET) / HY_DECAY_SHORT_PCT, HY_W, dtype=F32))
    tm = 1024
    const = lambda shp: pl.BlockSpec(shp, lambda i: (0, 0))
    row2 = lambda v: jnp.tile(v.astype(F32).reshape(1, -1), (1, 2))
    out = pl.pallas_call(
        _hy_filter_kernel,
        grid=(seqlen // tm,),
        in_specs=[pl.BlockSpec((tm, 256), lambda i: (i, 0)),
                  const((256, 2 * nh)), const((1, 2 * nh)), const((2 * nh, 2 * nh)), const((1, 2 * nh)),
                  const((2 * nh, 2 * nh)), const((1, 2 * nh)), const((1, 2 * nh)),
                  const((2 * nh, 2 * HY_W)), const((1, HY_W))],
        out_specs=pl.BlockSpec((2, tm, HY_W), lambda i: (0, i, 0)),
        out_shape=jax.ShapeDtypeStruct((2, seqlen, HY_W), F32),
        compiler_params=_cparams(("parallel",)),
        name="hy_filter_mlp",
    )(feat2, w1, row2(fb1), two(fw2), row2(fb2), two(fw3), row2(fb3), row2(freq), w4,
      deltas.reshape(1, HY_W))
    return out.reshape(2 * seqlen, HY_W)


def _od_out_kernel(na_ref, yt_ref, x0_ref, u_ref, xa_ref, xb_ref, mod_ref, hb_ref, wout_ref, wr_ref, br_ref,
                   x_out, h2_out, lg_out, wout_b):
    @pl.when(jnp.logical_and(pl.program_id(0) == 0, pl.program_id(1) == 0))
    def _():
        wout_b[...] = wout_ref[...].astype(BF16)

    y = jnp.concatenate([yt_ref[j].T for j in range(yt_ref.shape[0])], axis=0)
    hy = x0_ref[...].astype(F32) * (y + u_ref[...].astype(F32) * hb_ref[...])
    cat = jnp.concatenate([na_ref[...], hy], axis=1).astype(BF16)
    mix = jnp.dot(cat, wout_b[...], preferred_element_type=F32)
    x = jnp.concatenate([xa_ref[...], xb_ref[...]], axis=0)
    _mixer_tail(x, mix, (mod_ref[2:3, :], mod_ref[3:4, :], mod_ref[4:5, :]), wr_ref, br_ref, x_out, h2_out, lg_out)


def _latent_mod_index(b, i):
    return (b, 1, 0, 0)


def _od_out(na, yt, x0, u, xcat, mods, hb, wout, wr, br):
    bsz, l, _ = na.shape
    tm = 2 * TB
    nblk = l // tm
    tok = lambda n: pl.BlockSpec((None, tm, n), lambda b, i: (b, i, 0))
    const = lambda shp: pl.BlockSpec(shp, lambda b, i: tuple(0 for _ in shp))
    return pl.pallas_call(
        _od_out_kernel,
        grid=(bsz, nblk),
        in_specs=[tok(NA_W), pl.BlockSpec((None, tm // FFT_N1, HY_W, FFT_N1), lambda b, i: (b, i, 0, 0)),
                  tok(HY_W), tok(HY_W),
                  pl.BlockSpec((None, TB, D_MODEL), lambda b, i: (b, 2 * i + 1, 0)),
                  pl.BlockSpec((None, TB, D_MODEL), lambda b, i: (b, 2 * i + 2, 0)),
                  pl.BlockSpec((None, None, 6, D_MODEL), _latent_mod_index),
                  const((1, HY_W)), const((D_MODEL, D_MODEL)), const((D_MODEL, 128)), const((1, 128))],
        out_specs=[tok(D_MODEL), pl.BlockSpec((tm * ROW_TILE, 128), lambda b, i: (b * nblk + i, 0)), tok(128)],
        out_shape=[jax.ShapeDtypeStruct((bsz, l, D_MODEL), F32), jax.ShapeDtypeStruct((bsz * l * ROW_TILE, 128), F32),
                   jax.ShapeDtypeStruct((bsz, l, 128), F32)],
        scratch_shapes=[pltpu.VMEM((D_MODEL, D_MODEL), BF16)],
        compiler_params=_cparams(("arbitrary", "arbitrary")),
        name="od_out",
    )(na, yt, x0, u, xcat, xcat, mods, hb, wout, wr, br)


def _mods(c, c_ctx, ada_w, ada_b):
    bsz = c.shape[0]
    depth, _, n = ada_w.shape
    cc = jnp.concatenate([c, c_ctx[None]], axis=0)
    a = jnp.pad(cc * _sigmoid(cc), ((0, 8 - (bsz + 1) % 8), (0, 0)))
    mp, tn = a.shape[0], 1024
    m = pl.pallas_call(
        _adaln_kernel,
        grid=(depth, n // tn),
        in_specs=[pl.BlockSpec((mp, D_MODEL), lambda l, j: (0, 0)),
                  pl.BlockSpec((None, D_MODEL, tn), lambda l, j: (l, 0, j)),
                  pl.BlockSpec((None, 1, tn), lambda l, j: (l, 0, j))],
        out_specs=pl.BlockSpec((None, mp, tn), lambda l, j: (l, 0, j)),
        out_shape=jax.ShapeDtypeStruct((depth, mp, n), F32),
        compiler_params=_cparams(("parallel", "parallel")),
        name="adaln_dense",
    )(a, ada_w, ada_b.reshape(depth, 1, n))
    mod_l = m[:, :bsz].reshape(depth, bsz, 1, 6, D_MODEL)
    mod_c = jnp.broadcast_to(m[:, bsz].reshape(depth, 1, 1, 6, D_MODEL), (depth, bsz, 1, 6, D_MODEL))
    return jnp.concatenate([mod_c, mod_l], axis=2)


def _rope_tables(seqlen):
    pos = jnp.arange(seqlen)
    half = GLA_DK // 4
    freqs = ROPE_BASE ** (-jnp.arange(half, dtype=F32) / half)
    ar = (pos // GRID_W).astype(F32)[:, None] * freqs
    ac = (pos % GRID_W).astype(F32)[:, None] * freqs
    cos = jnp.concatenate([jnp.cos(ar), jnp.cos(ar), jnp.cos(ac), jnp.cos(ac)], axis=1)
    sin = jnp.concatenate([-jnp.sin(ar), jnp.sin(ar), -jnp.sin(ac), jnp.sin(ac)], axis=1)
    cos = jnp.concatenate([jnp.ones((TB, GLA_DK), F32), cos], axis=0)
    sin = jnp.concatenate([jnp.zeros((TB, GLA_DK), F32), sin], axis=0)
    return jnp.tile(cos, (1, GLA_HEADS)), jnp.tile(sin, (1, GLA_HEADS))


def _router_weights(wg, bg, we, be):
    pad = 128 - N_GROUPS - N_EXPERTS
    wr = jnp.concatenate([wg, we, jnp.zeros((D_MODEL, pad), F32)], axis=1)
    br = jnp.concatenate([bg, be, jnp.zeros((pad,), F32)]).reshape(1, 128)
    return wr, br


def kernel(x, c, ctx, c_ctx, ada_w, ada_b, moe_wg, moe_bg, moe_we, moe_be, moe_w1, moe_w3, moe_w2, ev_w_in, ev_w_out, gla_wa2, gla_ba, gla_norm, s5_lam_re, s5_lam_im, s5_log_dt, s5_b_re, s5_b_im, s5_c_re, s5_c_im, s5_d, s5_w_glu, od_w_in, od_w_out, na_q_norm, na_k_norm, na_rpb, hy_conv_w, hy_conv_b, hy_fw1, hy_fb1, hy_fw2, hy_fb2, hy_fw3, hy_fb3, hy_freq, hy_fw4, hy_bias):
    bsz, seqlen, _ = x.shape
    assert ctx.shape[1] == TB and seqlen % TB == 0

    mods_all = _mods(c, c_ctx, ada_w, ada_b)
    mods = mods_all[0]
    w_in = ev_w_in[0]
    n_a = 2 * GLA_RANK
    a0 = 2 * GLA_QK + 2 * GLA_V
    w_ev = jnp.concatenate([w_in[:, :a0], w_in[:, a0 + n_a:], w_in[:, a0:a0 + n_a],
                            jnp.zeros((D_MODEL, 128 - n_a), F32)], axis=1).astype(BF16)
    wa = jnp.zeros((128, 2 * GLA_QK), F32)
    for d in range(2):
        wa = wa.at[d * GLA_RANK:(d + 1) * GLA_RANK, d * GLA_QK:(d + 1) * GLA_QK].set(gla_wa2[0, d])
    cos, sin = _rope_tables(seqlen)
    q, k, v, g, u, u_sw, la = _ev_proj(ctx, x, mods, w_ev, wa, gla_ba[0].reshape(1, 2 * GLA_QK), cos, sin)
    o_f, o_b = _gla(q, k, v, la)
    s5p = [t[0].astype(F32) for t in (s5_lam_re, s5_lam_im, s5_log_dt, s5_b_re, s5_b_im, s5_c_re, s5_c_im)]
    y_f = _s5_scan(u, _s5_mats(*[t[0] for t in s5p], rev=False), rev=False)
    y_b = _s5_scan(u_sw, _s5_mats(*[t[1] for t in s5p], rev=True), rev=True)
    wr, br = _router_weights(moe_wg[0], moe_bg[0], moe_we[0], moe_be[0])
    x1, h2, lg = _ev_out(o_f, o_b, g, y_f, y_b, u, ctx, x, mods,
                         jnp.tile(gla_norm[0], GLA_HEADS).reshape(1, GLA_V), s5_d[0].reshape(1, S5_W),
                         s5_w_glu[0], ev_w_out[0], wr, br)
    xcat = _moe(x1, h2, lg, mods, TB, moe_w1, moe_w3, moe_w2, 0)

    mods = mods_all[1]
    hd = np.arange(NA_W) // NA_DH
    gm = jnp.asarray((hd[:, None] == hd[None, :]).astype(np.float32) / NA_DH)
    qh, kh, vh, zh = _od_proj(xcat, mods, od_w_in[0].astype(BF16), gm,
                              jnp.tile(na_q_norm[0], NA_HEADS).reshape(1, NA_W),
                              jnp.tile(na_k_norm[0], NA_HEADS).reshape(1, NA_W))
    na = _na(qh, kh, vh, _na_bias_table(na_rpb[0]))
    x0, uh, ut = _hy_pre(zh, hy_conv_w[0], hy_conv_b[0].reshape(1, 3 * HY_W))
    kfilt = _hy_filter(seqlen, hy_fw1[0], hy_fb1[0], hy_fw2[0], hy_fb2[0], hy_fw3[0], hy_fb3[0],
                       hy_freq[0], hy_fw4[0])
    yt = _hy_conv(ut, kfilt)
    wr, br = _router_weights(moe_wg[1], moe_bg[1], moe_we[1], moe_be[1])
    xl, h2, lg = _od_out(na, yt, x0, uh, xcat, mods, hy_bias[0].reshape(1, HY_W),
                         od_w_out[0], wr, br)
    return _moe(xl, h2, lg, mods, 0, moe_w1, moe_w3, moe_w2, 1)
```

```python
import functools
import math

import numpy as np
import jax
import jax.numpy as jnp
from jax import lax
from jax.experimental import pallas as pl
from jax.experimental.pallas import tpu as pltpu

F32, BF16 = jnp.float32, jnp.bfloat16
HI = lax.Precision.HIGHEST

D_MODEL = 1024
GRID_W = 64
EPS = 1e-6
ROPE_BASE = 10000.0
NEG_INF = -1e30
GLA_HEADS, GLA_DK, GLA_DV = 4, 64, 128
GLA_QK, GLA_V = GLA_HEADS * GLA_DK, GLA_HEADS * GLA_DV
GLA_RANK = 16
GLA_TAU = 16.0
GLA_CHUNK = 64
GLA_NB = 4
GLA_LOG_ALPHA_MIN = -1.0
S5_W, S5_H, S5_P = 512, 16, 64
S5_G = S5_W // S5_H
S5_CHUNK = 8
S5_TBLK = 1408
NA_HEADS, NA_DH = 8, 64
NA_W = NA_HEADS * NA_DH
WIN_R, WIN_C = 8, 16
NA_HG = 4
NA_ROWS = 4
HY_W = 512
HY_SHORT = 3
HY_EMB = 33
HY_DECAY_TARGET = 1e-2
HY_DECAY_SHORT_PCT = 0.3
HY_DECAY_LONG_PCT = 1.5
N_GROUPS, EXP_PER_GROUP = 4, 8
N_EXPERTS = N_GROUPS * EXP_PER_GROUP
D_EXPERT = 512
TOP_K = 2
MOE_BLOCK = 256

TB = 256
PROJ_TILE = 768
COMBINE_TILE = 1024
HALO = 16
FFT_N1 = 128
FFT_CG = 32
V7X_VMEM_LIMIT = 52 * 1024 * 1024


def _cparams(sem):
    return pltpu.CompilerParams(dimension_semantics=sem, vmem_limit_bytes=V7X_VMEM_LIMIT)


def _sigmoid(x):
    return 1.0 / (1.0 + jnp.exp(-x))


ROW_TILE = D_MODEL // 128


def _to_token_tiles(ref, val):
    n = val.shape[0]
    for j in range(ROW_TILE):
        ref[pl.ds(j, n, stride=ROW_TILE), :] = val[:, j * 128:(j + 1) * 128]


def _from_token_tiles(ref, n):
    return jnp.concatenate([ref[pl.ds(j, n, stride=ROW_TILE), :] for j in range(ROW_TILE)], axis=1)


def _split_bf16(x):
    hi = x.astype(BF16)
    return hi, (x - hi.astype(F32)).astype(BF16)


def _dot_x3(a, b):
    a_hi, a_lo = _split_bf16(a)
    b_hi, b_lo = _split_bf16(b)
    d = lambda p, q: jnp.dot(p, q, preferred_element_type=F32)
    return d(a_hi, b_hi) + d(a_lo, b_hi) + d(a_hi, b_lo)


def _rms_rows(x):
    return x * lax.rsqrt(jnp.mean(x * x, axis=-1, keepdims=True) + EPS)


def _adaln_kernel(a_ref, w_ref, b_ref, o_ref):
    o_ref[...] = jnp.dot(a_ref[...], w_ref[...], precision=HI, preferred_element_type=F32) + b_ref[...]


def _mod_index(b, i):
    return (b, jnp.minimum(i, 1), 0, 0)


def _swapped_index(nblk, b, i):
    return (b, jnp.where(i == 0, nblk - 1, i - 1), 0)


EV_NQ, EV_NK, EV_NV, EV_NG, EV_NU = 0, 256, 512, 1024, 1536
EV_NA = 2048
EV_NTOT = 2176


def _stream_tile(ctx_ref, x_ref):
    return jnp.where(pl.program_id(1) == 0, ctx_ref[...], x_ref[...])


def _stream_specs():
    return [pl.BlockSpec((None, TB, D_MODEL), lambda b, i: (b, 0, 0)),
            pl.BlockSpec((None, TB, D_MODEL), lambda b, i: (b, jnp.maximum(i - 1, 0), 0))]


def _ev_proj_kernel(ctx_ref, x_ref, mod_ref, w_ref, wa_ref, ba_ref, cos_ref, sin_ref,
                    q_ref, k_ref, v_ref, g_ref, u_ref, usw_ref, la_ref):
    x = _stream_tile(ctx_ref, x_ref)
    h = _rms_rows(x) * (1.0 + mod_ref[1:2, :]) + mod_ref[0:1, :]
    z = jnp.dot(h.astype(BF16), w_ref[...], preferred_element_type=F32)
    lane = lax.broadcasted_iota(jnp.int32, (x.shape[0], GLA_QK), 1)
    first = (lane % 32) < 16
    cos, sin = cos_ref[...], sin_ref[...]

    def rot(t):
        partner = jnp.where(first, pltpu.roll(t, GLA_QK - 16, 1), pltpu.roll(t, 16, 1))
        return t * cos + partner * sin

    q_ref[...] = rot(z[:, EV_NQ:EV_NQ + GLA_QK]) * (GLA_DK ** -0.5)
    k_ref[...] = rot(z[:, EV_NK:EV_NK + GLA_QK])
    v_ref[...] = z[:, EV_NV:EV_NV + GLA_V].astype(BF16)
    g_ref[...] = z[:, EV_NG:EV_NG + GLA_V].astype(BF16)
    u_ref[...] = z[:, EV_NU:EV_NU + S5_W]
    usw_ref[...] = z[:, EV_NU:EV_NU + S5_W]
    a = z[:, EV_NA:EV_NA + 128]
    pre = _dot_x3(a, wa_ref[...]) + ba_ref[...]
    ls = jnp.minimum(pre, 0.0) - jnp.log1p(jnp.exp(-jnp.abs(pre)))
    la_ref[...] = jnp.maximum(ls / GLA_TAU, GLA_LOG_ALPHA_MIN)


def _ev_proj(ctx, x, mods, w, wa, ba, cos, sin):
    bsz = x.shape[0]
    lt = ctx.shape[1] + x.shape[1]
    nblk = lt // TB
    tok = lambda n: pl.BlockSpec((None, TB, n), lambda b, i: (b, i, 0))
    const = lambda shp: pl.BlockSpec(shp, lambda b, i: tuple(0 for _ in shp))
    return pl.pallas_call(
        _ev_proj_kernel,
        grid=(bsz, nblk),
        in_specs=_stream_specs() + [
                  pl.BlockSpec((None, None, 6, D_MODEL), _mod_index),
                  const((D_MODEL, EV_NTOT)), const((128, 2 * GLA_QK)), const((1, 2 * GLA_QK)),
                  pl.BlockSpec((TB, GLA_QK), lambda b, i: (i, 0)),
                  pl.BlockSpec((TB, GLA_QK), lambda b, i: (i, 0))],
        out_specs=[tok(GLA_QK), tok(GLA_QK), tok(GLA_V), tok(GLA_V), tok(S5_W),
                   pl.BlockSpec((None, TB, S5_W), functools.partial(_swapped_index, nblk)), tok(2 * GLA_QK)],
        out_shape=[jax.ShapeDtypeStruct((bsz, lt, GLA_QK), F32),
                   jax.ShapeDtypeStruct((bsz, lt, GLA_QK), F32),
                   jax.ShapeDtypeStruct((bsz, lt, GLA_V), BF16),
                   jax.ShapeDtypeStruct((bsz, lt, GLA_V), BF16),
                   jax.ShapeDtypeStruct((bsz, lt, S5_W), F32),
                   jax.ShapeDtypeStruct((bsz, lt, S5_W), F32),
                   jax.ShapeDtypeStruct((bsz, lt, 2 * GLA_QK), F32)],
        compiler_params=_cparams(("parallel", "parallel")),
        name="ev_proj",
    )(ctx, x, mods, w, wa, ba, cos, sin)


def _gla_kernel(qf_ref, kf_ref, vf_ref, laf_ref, qb_ref, kb_ref, vb_ref, lab_ref,
                of_ref, ob_ref, s_scr):
    i = pl.program_id(1)

    @pl.when(i == 0)
    def _():
        s_scr[...] = jnp.zeros_like(s_scr)

    c = GLA_CHUNK
    nh = GLA_HEADS
    row = lax.broadcasted_iota(jnp.int32, (c, c), 0)
    col = lax.broadcasted_iota(jnp.int32, (c, c), 1)
    row4 = lax.broadcasted_iota(jnp.int32, (nh * c, c), 0) % c
    col4 = lax.broadcasted_iota(jnp.int32, (nh * c, c), 1)
    lane_head = lax.broadcasted_iota(jnp.int32, (c, GLA_QK), 1) // GLA_DK
    out_head = lax.broadcasted_iota(jnp.int32, (c, GLA_V), 1) // GLA_DV
    bd_mask = (lax.broadcasted_iota(jnp.int32, (GLA_V, GLA_QK), 0) // GLA_DV
               == lax.broadcasted_iota(jnp.int32, (GLA_V, GLA_QK), 1) // GLA_DK)
    nb = qf_ref.shape[0]
    nchunk = qf_ref.shape[1] // c
    nt = (((1,), (1,)), ((), ()))
    tn = (((0,), (0,)), ((), ()))

    def one_chunk(refs, o_ref, bb, d, r0):
        q_ref, k_ref, v_ref, la_ref = refs
        fwd = d == 0
        sl = pl.ds(r0, c)
        qc, kc, vc, lac = q_ref[bb, sl, :], k_ref[bb, sl, :], v_ref[bb, sl, :], la_ref[bb, sl, :]
        tri = ((row >= col) if fwd else (row <= col)).astype(BF16)
        la_hi, la_lo = _split_bf16(lac)
        b = (jnp.dot(tri, la_hi, preferred_element_type=F32)
             + jnp.dot(tri, la_lo, preferred_element_type=F32))
        b_last = b[c - 1:c, :] if fwd else b[0:1, :]
        qe = (qc * jnp.exp(b)).astype(BF16)
        ke = (kc * jnp.exp(-b)).astype(BF16)
        kd = (kc * jnp.exp(b_last - b)).astype(BF16)
        st = s_scr[bb, d]
        o = lax.dot_general(qe, st.astype(BF16), nt, preferred_element_type=F32)
        q4 = jnp.concatenate([jnp.where(lane_head == h, qe, jnp.zeros_like(qe)) for h in range(nh)], axis=0)
        att = lax.dot_general(q4, ke, nt, preferred_element_type=F32)
        att_mask = (row4 >= col4) if fwd else (row4 < col4)
        o4 = jnp.dot(jnp.where(att_mask, att, 0.0).astype(BF16), vc, preferred_element_type=F32)
        for h in range(nh):
            o = o + jnp.where(out_head == h, o4[h * c:(h + 1) * c, :], 0.0)
        o_ref[bb, sl, :] = o.astype(o_ref.dtype)
        upd_t = lax.dot_general(vc, kd, tn, preferred_element_type=F32)
        s_scr[bb, d] = st * jnp.exp(b_last) + jnp.where(bd_mask, upd_t, 0.0)

    def body(j, carry):
        for bb in range(nb):
            one_chunk((qf_ref, kf_ref, vf_ref, laf_ref), of_ref, bb, 0, pl.multiple_of(j * c, c))
            one_chunk((qb_ref, kb_ref, vb_ref, lab_ref), ob_ref, bb, 1, pl.multiple_of((nchunk - 1 - j) * c, c))
        return carry

    lax.fori_loop(0, nchunk, body, 0, unroll=True)


def _gla(q, k, v, la):
    bsz, lt, _ = q.shape
    nblk = lt // TB
    fwd_map = lambda b, i: (b, i, 0)
    bwd_blk = lambda i: jnp.where(i == 0, 0, nblk - i)
    bwd_map = lambda b, i: (b, bwd_blk(i), 0)
    bwd_map_la = lambda b, i: (b, bwd_blk(i), 1)
    nb = GLA_NB if bsz % GLA_NB == 0 else 1
    spec = lambda n, m: pl.BlockSpec((nb, TB, n), m)
    return pl.pallas_call(
        _gla_kernel,
        grid=(bsz // nb, nblk),
        in_specs=[spec(GLA_QK, fwd_map), spec(GLA_QK, fwd_map), spec(GLA_V, fwd_map), spec(GLA_QK, fwd_map),
                  spec(GLA_QK, bwd_map), spec(GLA_QK, bwd_map), spec(GLA_V, bwd_map), spec(GLA_QK, bwd_map_la)],
        out_specs=[spec(GLA_V, fwd_map), spec(GLA_V, bwd_map)],
        out_shape=[jax.ShapeDtypeStruct((bsz, lt, GLA_V), BF16), jax.ShapeDtypeStruct((bsz, lt, GLA_V), BF16)],
        scratch_shapes=[pltpu.VMEM((nb, 2, GLA_V, GLA_QK), F32)],
        compiler_params=_cparams(("parallel", "arbitrary")),
        name="gla_scan",
    )(q, k, v, la, q, k, v, la)


def _s5_kernel(u_ref, wm_ref, tm_ref, cm_ref, ar_ref, ai_ref, y_ref, w_scr, hp_scr, h_scr, *, rev):
    @pl.when(pl.program_id(1) == 0)
    def _():
        h_scr[...] = jnp.zeros_like(h_scr)

    bsz, ntok, _ = u_ref.shape
    nc = ntok // S5_CHUNK
    half = 8 * S5_P
    x = jnp.concatenate(
        [jnp.concatenate([u_ref[b, pl.ds(s, nc, stride=S5_CHUNK), :] for s in range(S5_CHUNK)], axis=1)
         for b in range(bsz)], axis=0).astype(BF16)
    w_scr[...] = jnp.dot(x, wm_ref[...], preferred_element_type=F32)
    ar, ai = ar_ref[...], ai_ref[...]

    def body(j, hs):
        c = (nc - 1 - j) if rev else j
        out = []
        for b in range(bsz):
            re, im = hs[b]
            r = b * nc + c
            hp_scr[pl.ds(r, 1), :] = jnp.concatenate([re, im], axis=1)
            w = w_scr[pl.ds(r, 1), :]
            out.append((ar * re - ai * im + w[:, :half], ar * im + ai * re + w[:, half:]))
        return tuple(out)

    hs = lax.fori_loop(0, nc, body, tuple((h_scr[b:b + 1, :half], h_scr[b:b + 1, half:]) for b in range(bsz)))
    for b in range(bsz):
        h_scr[b:b + 1, :] = jnp.concatenate(hs[b], axis=1)
    mt = 256
    ntile = x.shape[1] // mt
    cols = []
    for jt in range(ntile):
        acc = None
        for it in (range(jt, ntile) if rev else range(jt + 1)):
            part = jnp.dot(x[:, it * mt:(it + 1) * mt], tm_ref[it * mt:(it + 1) * mt, jt * mt:(jt + 1) * mt],
                           preferred_element_type=F32)
            acc = part if acc is None else acc + part
        cols.append(acc)
    y = (jnp.concatenate(cols, axis=1)
         + jnp.dot(hp_scr[...].astype(BF16), cm_ref[...], preferred_element_type=F32))
    for b in range(bsz):
        for s in range(S5_CHUNK):
            y_ref[b, pl.ds(s, nc, stride=S5_CHUNK), :] = y[b * nc:(b + 1) * nc, s * 128:(s + 1) * 128]


def _s5_mats(lam_re, lam_im, log_dt, b_re, b_im, c_re, c_im, rev):
    t16 = S5_CHUNK
    dt = jnp.exp(log_dt)[:, None]
    mag = jnp.exp(lam_re * dt)
    a_re, a_im = mag * jnp.cos(lam_im * dt), mag * jnp.sin(lam_im * dt)
    den = lam_re * lam_re + lam_im * lam_im
    nr = a_re - 1.0
    co_re = ((nr * lam_re + a_im * lam_im) / den)[..., None]
    co_im = ((a_im * lam_re - nr * lam_im) / den)[..., None]
    bb_re, bb_im = co_re * b_re - co_im * b_im, co_re * b_im + co_im * b_re
    pr, pi = [jnp.ones_like(a_re)], [jnp.zeros_like(a_im)]
    for _ in range(t16):
        pr, pi = pr + [pr[-1] * a_re - pi[-1] * a_im], pi + [pr[-1] * a_im + pi[-1] * a_re]
    pw_re, pw_im = jnp.stack(pr), jnp.stack(pi)
    g = lam_re.shape[0]
    e_re, e_im = pw_re[t16 - 1::-1][:t16], pw_im[t16 - 1::-1][:t16]
    wre = jnp.einsum('sgp,gph->gshp', e_re, bb_re) - jnp.einsum('sgp,gph->gshp', e_im, bb_im)
    wim = jnp.einsum('sgp,gph->gshp', e_re, bb_im) + jnp.einsum('sgp,gph->gshp', e_im, bb_re)
    cb_re = jnp.einsum('gkp,gph->gpkh', c_re, bb_re) - jnp.einsum('gkp,gph->gpkh', c_im, bb_im)
    cb_im = jnp.einsum('gkp,gph->gpkh', c_re, bb_im) + jnp.einsum('gkp,gph->gpkh', c_im, bb_re)
    kd = jnp.einsum('dgp,gpkh->dgkh', pw_re[:t16], cb_re) - jnp.einsum('dgp,gpkh->dgkh', pw_im[:t16], cb_im)
    lag = np.arange(t16)[None, :] - np.arange(t16)[:, None]
    toe = jnp.where((lag >= 0)[:, :, None, None, None], kd[np.clip(lag, 0, t16 - 1)], 0.0)
    toe = toe.transpose(2, 0, 4, 1, 3)
    q_re, q_im = pw_re[1:], pw_im[1:]
    ca_re = jnp.einsum('gkp,tgp->gptk', c_re, q_re) - jnp.einsum('gkp,tgp->gptk', c_im, q_im)
    ca_im = jnp.einsum('gkp,tgp->gptk', c_re, q_im) + jnp.einsum('gkp,tgp->gptk', c_im, q_re)
    if rev:
        wre, wim = wre[:, ::-1], wim[:, ::-1]
        toe = toe[:, ::-1, :, ::-1]
        ca_re, ca_im = ca_re[:, :, ::-1], ca_im[:, :, ::-1]
    nq, gl = g // 8, 8
    nd = t16 * S5_H
    tok_hot = np.zeros((gl, nd, t16 * 128), np.float32)
    st_hot = np.zeros((gl, 2 * S5_P, 2 * gl * S5_P), np.float32)
    for gi in range(gl):
        a = np.arange(nd)
        tok_hot[gi, a, (a // S5_H) * 128 + gi * S5_H + a % S5_H] = 1.0
        a = np.arange(2 * S5_P)
        st_hot[gi, a, (a // S5_P) * gl * S5_P + gi * S5_P + a % S5_P] = 1.0
    place = lambda rows, blk, cols: jnp.einsum('gar,qgab,gbc->qrc', rows, blk, cols)
    wg = jnp.concatenate([wre, wim], axis=-1).reshape(nq, gl, nd, 2 * S5_P)
    tg = toe.reshape(nq, gl, nd, nd)
    cg = jnp.concatenate([ca_re, -ca_im], axis=1).reshape(nq, gl, 2 * S5_P, nd)
    wm = place(tok_hot, wg, st_hot)
    tmat = place(tok_hot, tg, tok_hot)
    cm = place(st_hot, cg, tok_hot)
    return (wm.astype(BF16), tmat.astype(BF16), cm.astype(BF16),
            pw_re[t16].reshape(nq, 1, gl * S5_P), pw_im[t16].reshape(nq, 1, gl * S5_P))


def _s5_scan(u, mats, rev):
    bsz, lt, _ = u.shape
    tblk = max(d for d in range(64, S5_TBLK + 1, 64) if lt % d == 0)
    nblk = lt // tblk
    nq = S5_G // 8
    wm, tmat, cm, ar, ai = mats
    kw, ks = S5_CHUNK * 128, 8 * S5_P
    tmap = (lambda q, t: (0, nblk - 1 - t, q)) if rev else (lambda q, t: (0, t, q))
    per = lambda shp: pl.BlockSpec((None,) + shp, lambda q, t: (q,) + tuple(0 for _ in shp))
    rows = bsz * tblk // S5_CHUNK
    return pl.pallas_call(
        functools.partial(_s5_kernel, rev=rev),
        grid=(nq, nblk),
        in_specs=[pl.BlockSpec((bsz, tblk, 128), tmap),
                  per((kw, 2 * ks)), per((kw, kw)), per((2 * ks, kw)), per((1, ks)), per((1, ks))],
        out_specs=pl.BlockSpec((bsz, tblk, 128), tmap),
        out_shape=jax.ShapeDtypeStruct((bsz, lt, S5_W), F32),
        scratch_shapes=[pltpu.VMEM((rows, 2 * ks), F32), pltpu.VMEM((rows, 2 * ks), F32),
                        pltpu.VMEM((8, 2 * ks), F32)],
        compiler_params=_cparams(("parallel", "arbitrary")),
        name="s5_scan",
    )(u, wm, tmat, cm, ar, ai)


def _mixer_tail(x, mix, mods, wr_ref, br_ref, x_out, h2_out, lg_out):
    gate, shift, scale = mods
    xn = x + gate * mix
    x_out[...] = xn
    h2 = _rms_rows(xn) * (1.0 + scale) + shift
    _to_token_tiles(h2_out, h2)
    lg_out[...] = _route_tail(_dot_x3(h2, wr_ref[...]) + br_ref[...])


def _route_tail(lg):
    lane = lax.broadcasted_iota(jnp.int32, lg.shape, 1)
    lane_f = lane.astype(F32)
    big = jnp.float32(128.0)
    row_max = lambda t: jnp.max(t, axis=1, keepdims=True)
    first_lane = lambda hit: jnp.min(jnp.where(hit, lane_f, big), axis=1, keepdims=True)
    is_g = lane < N_GROUPS
    g_logit = jnp.where(is_g, lg, NEG_INF)
    g_max = row_max(g_logit)
    grp = first_lane(g_logit == g_max)
    g_w = 1.0 / jnp.sum(jnp.where(is_g, jnp.exp(lg - g_max), 0.0), axis=1, keepdims=True)
    lo = N_GROUPS + EXP_PER_GROUP * grp
    in_grp = jnp.logical_and(lane_f >= lo, lane_f < lo + EXP_PER_GROUP)
    e_max = row_max(jnp.where(in_grp, lg, NEG_INF))
    pe = jnp.where(in_grp, jnp.exp(lg - e_max), 0.0)
    p = pe / jnp.sum(pe, axis=1, keepdims=True)
    cand = jnp.where(in_grp, p, -1.0)
    p1 = row_max(cand)
    i1 = first_lane(cand == p1)
    cand2 = jnp.where(lane_f == i1, -1.0, cand)
    p2 = row_max(cand2)
    i2 = first_lane(cand2 == p2)
    scale = g_w / (p1 + p2)
    out = jnp.where(lane == 0, scale * p1, jnp.where(lane == 1, scale * p2,
                    jnp.where(lane == 2, i1 - N_GROUPS, jnp.where(lane == 3, i2 - N_GROUPS, 0.0))))
    return out


def _ev_out_kernel(of_ref, ob_ref, g_ref, yf_ref, u_ref, ctx_ref, mod_ref,
                   gn_ref, ds_ref, wglu_ref, wout_ref, wr_ref, br_ref, *rest, nsub):
    yb_refs, x_refs, (x_out, h2_out, lg_out) = rest[:nsub], rest[nsub:2 * nsub], rest[2 * nsub:]
    i = pl.program_id(1)
    tm = nsub * TB
    yb = jnp.concatenate([r[...] for r in yb_refs], axis=0)
    x = jnp.concatenate([jnp.where(nsub * i + k == 0, ctx_ref[...], r[...]) for k, r in enumerate(x_refs)], axis=0)
    is_ctx = i * tm + lax.broadcasted_iota(jnp.int32, (tm, 1), 0) < TB
    mod = lambda r: jnp.where(is_ctx, mod_ref[0, r:r + 1, :], mod_ref[1, r:r + 1, :])
    o = of_ref[...].astype(F32) + ob_ref[...].astype(F32)
    og = jnp.concatenate([_rms_rows(o[:, h * GLA_DV:(h + 1) * GLA_DV]) for h in range(GLA_HEADS)], axis=1)
    g = g_ref[...].astype(F32)
    og = og * gn_ref[...] * (g * _sigmoid(g))
    t = yf_ref[...] + yb + ds_ref[...] * u_ref[...]
    y = t * (0.5 * (1.0 + jnp.tanh(math.sqrt(2.0 / math.pi) * (t + 0.044715 * (t * t * t)))))
    y = y * _sigmoid(jnp.dot(y.astype(BF16), wglu_ref[...], preferred_element_type=F32))
    cat = jnp.concatenate([og, y], axis=1).astype(BF16)
    mix = jnp.dot(cat, wout_ref[...], preferred_element_type=F32)
    _mixer_tail(x, mix, (mod(2), mod(3), mod(4)), wr_ref, br_ref, x_out, h2_out, lg_out)


def _ev_out(o_f, o_b, g, y_f, y_b, u, ctx, x, mods, gn, ds, wglu, wout, wr, br):
    bsz, lt, _ = u.shape
    nstream = lt // TB
    nsub = 3 if nstream % 3 == 0 else 1
    tm = nsub * TB
    nblk = lt // tm
    tok = lambda n: pl.BlockSpec((None, tm, n), lambda b, i: (b, i, 0))
    const = lambda shp: pl.BlockSpec(shp, lambda b, i: tuple(0 for _ in shp))
    sub = lambda n, fn: [pl.BlockSpec((None, TB, n), functools.partial(fn, k)) for k in range(nsub)]
    yb_map = lambda k, b, i: _swapped_index(nstream, b, nsub * i + k)
    x_map = lambda k, b, i: (b, jnp.maximum(nsub * i + k - 1, 0), 0)
    return pl.pallas_call(
        functools.partial(_ev_out_kernel, nsub=nsub),
        grid=(bsz, nblk),
        in_specs=[tok(512), tok(512), tok(512), tok(512), tok(512),
                  pl.BlockSpec((None, TB, D_MODEL), lambda b, i: (b, 0, 0)),
                  pl.BlockSpec((None, 2, 6, D_MODEL), lambda b, i: (b, 0, 0, 0)),
                  const((1, 512)), const((1, 512)), const((512, 512)), const((D_MODEL, D_MODEL)),
                  const((D_MODEL, 128)), const((1, 128))] + sub(S5_W, yb_map) + sub(D_MODEL, x_map),
        out_specs=[tok(D_MODEL), pl.BlockSpec((tm * ROW_TILE, 128), lambda b, i: (b * nblk + i, 0)), tok(128)],
        out_shape=[jax.ShapeDtypeStruct((bsz, lt, D_MODEL), F32),
                   jax.ShapeDtypeStruct((bsz * lt * ROW_TILE, 128), F32),
                   jax.ShapeDtypeStruct((bsz, lt, 128), F32)],
        compiler_params=_cparams(("parallel", "parallel")),
        name="ev_out",
    )(o_f, o_b, g, y_f, u, ctx, mods, gn, ds, wglu, wout, wr, br, *([y_b] * nsub), *([x] * nsub))


def _route(route):
    n_tok = route.shape[0]
    gate = route[:, :TOP_K]
    eid = route[:, TOP_K:2 * TOP_K].astype(jnp.int32).reshape(-1)
    n_asg = n_tok * TOP_K
    order = jnp.argsort(eid).astype(jnp.int32)
    counts = jnp.sum((eid[:, None] == jnp.arange(N_EXPERTS)[None, :]).astype(jnp.int32), axis=0)
    padded = (counts + MOE_BLOCK - 1) // MOE_BLOCK * MOE_BLOCK
    pad_end = jnp.cumsum(padded)
    pad_start = pad_end - padded
    cnt_start = jnp.cumsum(counts) - counts
    n_blocks = -(-n_asg // MOE_BLOCK) + N_EXPERTS
    blk_start = jnp.arange(n_blocks, dtype=jnp.int32) * MOE_BLOCK
    blk_e = jnp.minimum(jnp.sum((pad_end[None, :] <= blk_start[:, None]).astype(jnp.int32), axis=1), N_EXPERTS - 1)
    pos = jnp.arange(n_blocks * MOE_BLOCK, dtype=jnp.int32)
    pos_e = jnp.repeat(blk_e, MOE_BLOCK)
    rank = pos - pad_start[pos_e]
    src = jnp.clip(cnt_start[pos_e] + rank, 0, n_asg - 1)
    slot_buf = jnp.where(rank < counts[pos_e], order[src], n_asg).astype(jnp.int32)
    n_valid = jnp.sum((slot_buf < n_asg).reshape(n_blocks, MOE_BLOCK), axis=1).astype(jnp.int32)
    tok = lax.shift_right_logical(slot_buf, 1)
    src_rows = jnp.minimum(tok, n_tok - 1) * ROW_TILE
    spare = n_asg + (pos // MOE_BLOCK % 2) * MOE_BLOCK + pos % MOE_BLOCK
    dst_rows = jnp.where(slot_buf < n_asg, (slot_buf & 1) * n_tok + tok, spare)
    lead = n_asg + jnp.arange(2 * MOE_BLOCK, dtype=jnp.int32)
    dst_rows = jnp.concatenate([lead, dst_rows]) * ROW_TILE
    return src_rows, dst_rows, blk_e, n_valid, gate.astype(F32)


def _moe_kernel(src_ref, dst_ref, blke_ref, nvalid_ref, h_hbm, w1_ref, w3_ref, w2_ref, z_hbm,
                xbuf0, xbuf1, ybuf0, ybuf1, wb1, wb3, wb2, gsem, ssem):
    i = pl.program_id(0)
    nblk = pl.num_programs(0)
    ns = ROW_TILE
    xb, yb = (xbuf0, xbuf1), (ybuf0, ybuf1)
    lead = 2 * MOE_BLOCK

    def issue_gather(blk, buf):
        base = blk * MOE_BLOCK
        for r in range(MOE_BLOCK):
            src = pl.multiple_of(src_ref[base + r], ns)
            pltpu.make_async_copy(h_hbm.at[pl.ds(src, ns)], xb[buf].at[pl.ds(r * ns, ns)],
                                  gsem.at[buf]).start(priority=r % 2)

    def issue_scatter(blk, buf):
        base = lead + blk * MOE_BLOCK
        for r in range(MOE_BLOCK):
            dst = pl.multiple_of(dst_ref[base + r], ns)
            pltpu.make_async_copy(yb[buf].at[pl.ds(r * ns, ns)], z_hbm.at[pl.ds(dst, ns)],
                                  ssem.at[buf]).start(priority=r % 2)

    def wait_gather(buf):
        pltpu.make_async_copy(h_hbm.at[pl.ds(0, MOE_BLOCK * ns)], xb[buf], gsem.at[buf]).wait()

    def wait_scatter(buf):
        pltpu.make_async_copy(yb[buf], z_hbm.at[pl.ds(0, MOE_BLOCK * ns)], ssem.at[buf]).wait()

    used = nvalid_ref[i] > 0

    @pl.when(i == 0)
    def _():
        ybuf0[...] = jnp.zeros_like(ybuf0)
        ybuf1[...] = jnp.zeros_like(ybuf1)
        issue_scatter(-2, 0)
        issue_gather(0, 0)

    def step(cur):
        oth = 1 - cur
        wait_gather(cur)
        issue_gather(jnp.minimum(i + 1, nblk - 1), oth)
        issue_scatter(i - 1, oth)
        x = _from_token_tiles(xb[cur], MOE_BLOCK).astype(BF16)

        @pl.when(jnp.logical_or(i == 0, blke_ref[i] != blke_ref[jnp.maximum(i - 1, 0)]))
        def _():
            wb1[...] = w1_ref[...].astype(BF16)
            wb3[...] = w3_ref[...].astype(BF16)
            wb2[...] = w2_ref[...].astype(BF16)

        h1 = jnp.dot(x, wb1[...], preferred_element_type=F32)
        h3 = jnp.dot(x, wb3[...], preferred_element_type=F32)
        a = (h1 * _sigmoid(h1) * h3).astype(BF16)
        y = jnp.dot(a, wb2[...], preferred_element_type=F32)
        wait_scatter(cur)
        _to_token_tiles(yb[cur], y)

    def drain(last_par):
        wait_gather(1 - last_par)
        issue_scatter(jnp.where(used, i, i - 1), last_par)
        wait_scatter(1 - last_par)
        wait_scatter(last_par)

    for par in range(2):
        @pl.when(jnp.logical_and(used, i % 2 == par))
        def _():
            step(par)

    first_unused = jnp.logical_and(jnp.logical_not(used),
                                   jnp.logical_and(i > 0, nvalid_ref[jnp.maximum(i - 1, 0)] > 0))
    last_used = jnp.logical_and(used, i == nblk - 1)
    for par in range(2):
        @pl.when(jnp.logical_or(jnp.logical_and(first_unused, (i - 1) % 2 == par),
                                jnp.logical_and(last_used, i % 2 == par)))
        def _():
            drain(par)


def _moe_experts(h2, src_rows, dst_rows, blk_e, n_valid, w1, w3, w2, layer):
    n_tok = h2.shape[0] // ROW_TILE
    n_blocks = blk_e.shape[0]
    wspec = lambda shp: pl.BlockSpec((None, None) + shp, lambda i, src, dst, blke, nvalid: (layer, blke[i], 0, 0))
    grid_spec = pltpu.PrefetchScalarGridSpec(
        num_scalar_prefetch=4,
        grid=(n_blocks,),
        in_specs=[pl.BlockSpec(memory_space=pl.ANY),
                  wspec((D_MODEL, D_EXPERT)), wspec((D_MODEL, D_EXPERT)), wspec((D_EXPERT, D_MODEL))],
        out_specs=pl.BlockSpec(memory_space=pl.ANY),
        scratch_shapes=[pltpu.VMEM((MOE_BLOCK * ROW_TILE, 128), F32), pltpu.VMEM((MOE_BLOCK * ROW_TILE, 128), F32),
                        pltpu.VMEM((MOE_BLOCK * ROW_TILE, 128), F32), pltpu.VMEM((MOE_BLOCK * ROW_TILE, 128), F32),
                        pltpu.VMEM((D_MODEL, D_EXPERT), BF16), pltpu.VMEM((D_MODEL, D_EXPERT), BF16),
                        pltpu.VMEM((D_EXPERT, D_MODEL), BF16),
                        pltpu.SemaphoreType.DMA((2,)), pltpu.SemaphoreType.DMA((2,))])
    return pl.pallas_call(
        _moe_kernel,
        grid_spec=grid_spec,
        out_shape=jax.ShapeDtypeStruct(((TOP_K * n_tok + 2 * MOE_BLOCK) * ROW_TILE, 128), F32),
        compiler_params=_cparams(("arbitrary",)),
        name="moe_experts",
    )(src_rows, dst_rows, blk_e, n_valid, h2, w1, w3, w2)


def _moe_combine_kernel(x_ref, z0_ref, z1_ref, gate_ref, mod_ref, o_ref, *, n_ctx):
    gate = gate_ref[...]
    tm = x_ref.shape[0]
    y = gate[:, 0:1] * _from_token_tiles(z0_ref, tm) + gate[:, 1:2] * _from_token_tiles(z1_ref, tm)
    g_out = mod_ref[1, 5:6, :]
    if n_ctx:
        is_ctx = pl.program_id(1) * tm + lax.broadcasted_iota(jnp.int32, (tm, 1), 0) < n_ctx
        g_out = jnp.where(is_ctx, mod_ref[0, 5:6, :], g_out)
    o_ref[...] = x_ref[...] + g_out * y


def _moe_combine(x, z, gate, mods, n_ctx):
    bsz, lt, _ = x.shape
    tm = max(d for d in range(TB, COMBINE_TILE + 1, TB) if lt % d == 0)
    nblk = lt // tm
    gate3 = gate.reshape(bsz, lt, TOP_K)
    return pl.pallas_call(
        functools.partial(_moe_combine_kernel, n_ctx=n_ctx),
        grid=(bsz, nblk),
        in_specs=[pl.BlockSpec((None, tm, D_MODEL), lambda b, i: (b, i, 0)),
                  pl.BlockSpec((tm * ROW_TILE, 128), lambda b, i: (b * nblk + i, 0)),
                  pl.BlockSpec((tm * ROW_TILE, 128), lambda b, i: ((bsz + b) * nblk + i, 0)),
                  pl.BlockSpec((None, tm, TOP_K), lambda b, i: (b, i, 0)),
                  pl.BlockSpec((None, 2, 6, D_MODEL), lambda b, i: (b, 0, 0, 0))],
        out_specs=pl.BlockSpec((None, tm, D_MODEL), lambda b, i: (b, i, 0)),
        out_shape=jax.ShapeDtypeStruct((bsz, lt, D_MODEL), F32),
        compiler_params=_cparams(("parallel", "parallel")),
        name="moe_combine",
    )(x, z, z, gate3, mods)


def _moe(x, h2, logits, mods, n_ctx, w1, w3, w2, layer):
    bsz, lt, _ = x.shape
    src_rows, dst_rows, blk_e, n_valid, gate = _route(logits.reshape(bsz * lt, 128))
    z = _moe_experts(h2, src_rows, dst_rows, blk_e, n_valid, w1, w3, w2, layer)
    return _moe_combine(x, z, gate, mods, n_ctx)


def _od_proj_kernel(x_ref, mod_ref, w_ref, gm_ref, qn_ref, kn_ref, q_ref, k_ref, v_ref, zh_ref):
    tm = x_ref.shape[0]
    is_ctx = pl.program_id(1) * tm + lax.broadcasted_iota(jnp.int32, (tm, 1), 0) < TB
    shift = jnp.where(is_ctx, mod_ref[0, 0:1, :], mod_ref[1, 0:1, :])
    scale = jnp.where(is_ctx, mod_ref[0, 1:2, :], mod_ref[1, 1:2, :])
    h = _rms_rows(x_ref[...]) * (1.0 + scale) + shift
    z = jnp.dot(h.astype(BF16), w_ref[...], preferred_element_type=F32)

    def head_norm(t, gain):
        sq_hi, sq_lo = _split_bf16(t * t)
        gm = gm_ref[...].astype(BF16)
        ms = jnp.dot(sq_hi, gm, preferred_element_type=F32) + jnp.dot(sq_lo, gm, preferred_element_type=F32)
        return t * lax.rsqrt(ms + EPS) * gain

    q_ref[...] = (head_norm(z[:, :NA_W], qn_ref[...]) * (NA_DH ** -0.5)).astype(BF16)
    k_ref[...] = head_norm(z[:, NA_W:2 * NA_W], kn_ref[...]).astype(BF16)
    v_ref[...] = z[:, 2 * NA_W:3 * NA_W].astype(BF16)
    zh_ref[...] = z[:, 3 * NA_W:].astype(BF16)


def _od_proj(xcat, mods, w, gm, qn, kn):
    bsz, lt, _ = xcat.shape
    tm = max(d for d in range(TB, PROJ_TILE + 1, TB) if lt % d == 0)
    nblk = lt // tm
    tok = lambda n: pl.BlockSpec((None, tm, n), lambda b, i: (b, i, 0))
    const = lambda shp: pl.BlockSpec(shp, lambda b, i: tuple(0 for _ in shp))
    return pl.pallas_call(
        _od_proj_kernel,
        grid=(bsz, nblk),
        in_specs=[tok(D_MODEL), pl.BlockSpec((None, 2, 6, D_MODEL), lambda b, i: (b, 0, 0, 0)),
                  const((D_MODEL, 3 * NA_W + 3 * HY_W)), const((NA_W, NA_W)), const((1, NA_W)), const((1, NA_W))],
        out_specs=[tok(NA_W), tok(NA_W), tok(NA_W), tok(3 * HY_W)],
        out_shape=[jax.ShapeDtypeStruct((bsz, lt, NA_W), BF16), jax.ShapeDtypeStruct((bsz, lt, NA_W), BF16),
                   jax.ShapeDtypeStruct((bsz, lt, NA_W), BF16), jax.ShapeDtypeStruct((bsz, lt, 3 * HY_W), BF16)],
        compiler_params=_cparams(("parallel", "parallel")),
        name="od_proj",
    )(xcat, mods, w, gm, qn, kn)


def _na_kernel(q_ref, k_ref, v_ref, t2_ref, o_ref):
    for j in range(NA_ROWS):
        _na_one_row(q_ref, k_ref, v_ref, t2_ref, o_ref, j)


def _na_one_row(q_ref, k_ref, v_ref, t2_ref, o_ref, j):
    r = pl.program_id(1) * NA_ROWS + j
    n_rows = pl.num_programs(1) * NA_ROWS
    r0 = jnp.clip(r - WIN_R // 2, 0, n_rows - WIN_R)
    off = r0 - r + WIN_R - 1
    base = pl.multiple_of(TB + r0 * GRID_W, GRID_W)
    nloc = WIN_R * GRID_W
    q = q_ref[j * GRID_W:(j + 1) * GRID_W, :]
    hg = NA_HG
    gw = hg * NA_DH
    lane_head = lax.broadcasted_iota(jnp.int32, (GRID_W, gw), 1) // NA_DH
    nt = (((1,), (1,)), ((), ()))
    outs = []
    for grp in range(NA_HEADS // hg):
        cs = slice(gw * grp, gw * (grp + 1))
        q2 = q[:, cs]
        q4 = jnp.concatenate([jnp.where(lane_head == h, q2, jnp.zeros_like(q2)) for h in range(hg)], axis=0)
        kw, vw = k_ref[pl.ds(base, nloc), cs], v_ref[pl.ds(base, nloc), cs]
        kc, vc = k_ref[0:TB, cs], v_ref[0:TB, cs]
        bias = jnp.concatenate(
            [jnp.concatenate([t2_ref[hg * grp + h, off + 2 * m] for m in range(WIN_R // 2)], axis=1)
             for h in range(hg)], axis=0)
        s_loc = lax.dot_general(q4, kw, nt, preferred_element_type=F32) + bias
        s_ctx = lax.dot_general(q4, kc, nt, preferred_element_type=F32)
        m = jnp.maximum(jnp.max(s_loc, axis=1, keepdims=True), jnp.max(s_ctx, axis=1, keepdims=True))
        p_loc, p_ctx = jnp.exp(s_loc - m), jnp.exp(s_ctx - m)
        den = jnp.sum(p_loc, axis=1, keepdims=True) + jnp.sum(p_ctx, axis=1, keepdims=True)
        o4 = (jnp.dot(p_loc.astype(BF16), vw, preferred_element_type=F32)
              + jnp.dot(p_ctx.astype(BF16), vc, preferred_element_type=F32)) / den
        acc = jnp.zeros((GRID_W, gw), F32)
        for h in range(hg):
            acc = jnp.where(lane_head == h, o4[h * GRID_W:(h + 1) * GRID_W, :], acc)
        outs.append(acc)
    o_ref[j * GRID_W:(j + 1) * GRID_W, :] = jnp.concatenate(outs, axis=1)


def _na_bias_table(rpb):
    qc = np.arange(GRID_W)[:, None]
    kc = np.arange(GRID_W)[None, :]
    q_start = np.clip(qc - WIN_C // 2, 0, GRID_W - WIN_C)
    valid = (kc >= q_start) & (kc < q_start + WIN_C)
    col_idx = np.clip(kc - qc + WIN_C - 1, 0, 2 * WIN_C - 2)
    hot = np.zeros((GRID_W, GRID_W, 2 * WIN_C - 1), np.float32)
    hot[qc, kc, col_idx] = 1.0
    t = jnp.einsum('hrj,qkj->hrqk', rpb.astype(F32), hot, precision=HI)
    t = jnp.where(valid[None, None], t, NEG_INF)
    return jnp.concatenate([t[:, :-1], t[:, 1:]], axis=-1)


def _na(q, k, v, t2):
    bsz, lt, _ = q.shape
    qrows = NA_ROWS * GRID_W
    n_rows = (lt - TB) // qrows
    qoff = TB // qrows
    return pl.pallas_call(
        _na_kernel,
        grid=(bsz, n_rows),
        in_specs=[pl.BlockSpec((None, qrows, NA_W), lambda b, r: (b, r + qoff, 0)),
                  pl.BlockSpec((None, lt, NA_W), lambda b, r: (b, 0, 0)),
                  pl.BlockSpec((None, lt, NA_W), lambda b, r: (b, 0, 0)),
                  pl.BlockSpec(t2.shape, lambda b, r: (0, 0, 0, 0))],
        out_specs=pl.BlockSpec((None, qrows, NA_W), lambda b, r: (b, r, 0)),
        out_shape=jax.ShapeDtypeStruct((bsz, lt - TB, NA_W), F32),
        compiler_params=_cparams(("parallel", "arbitrary")),
        name="na_attn",
    )(q, k, v, t2)


def _hy_pre_kernel(za_ref, zb_ref, zp_ref, zn_ref, cw_ref, cb_ref, x0_ref, u_ref, ut_ref):
    i = pl.program_id(1)
    n = pl.num_programs(1)
    z = jnp.concatenate([za_ref[...], zb_ref[...]], axis=0).astype(F32)
    tb = z.shape[0]
    prev_row = jnp.where(i > 0, zp_ref[HALO - 1:HALO, :].astype(F32), 0.0)
    next_row = jnp.where(i < n - 1, zn_ref[0:1, :].astype(F32), 0.0)
    rowid = lax.broadcasted_iota(jnp.int32, z.shape, 0)
    zm = jnp.where(rowid == 0, prev_row, pltpu.roll(z, 1, 0))
    zp = jnp.where(rowid == tb - 1, next_row, pltpu.roll(z, tb - 1, 0))
    zc = cb_ref[...] + cw_ref[0:1, :] * zm
    zc = zc + cw_ref[1:2, :] * z
    zc = zc + cw_ref[2:3, :] * zp
    x0_ref[...] = zc[:, :HY_W].astype(BF16)
    u = zc[:, HY_W:2 * HY_W] * zc[:, 2 * HY_W:]
    u_ref[...] = u.astype(BF16)
    for j in range(tb // FFT_N1):
        ut_ref[j] = u[j * FFT_N1:(j + 1) * FFT_N1, :].T.astype(BF16)


def _hy_pre(zh, cw, cb):
    bsz, lt, _ = zh.shape
    l = lt - TB
    tm = 2 * TB
    nblk = l // tm
    hpt = TB // HALO
    return pl.pallas_call(
        _hy_pre_kernel,
        grid=(bsz, nblk),
        in_specs=[pl.BlockSpec((None, TB, 3 * HY_W), lambda b, i: (b, 2 * i + 1, 0)),
                  pl.BlockSpec((None, TB, 3 * HY_W), lambda b, i: (b, 2 * i + 2, 0)),
                  pl.BlockSpec((None, HALO, 3 * HY_W), lambda b, i: (b, (2 * i + 1) * hpt - 1, 0)),
                  pl.BlockSpec((None, HALO, 3 * HY_W),
                               lambda b, i: (b, jnp.minimum((2 * i + 3) * hpt, lt // HALO - 1), 0)),
                  pl.BlockSpec((HY_SHORT, 3 * HY_W), lambda b, i: (0, 0)),
                  pl.BlockSpec((1, 3 * HY_W), lambda b, i: (0, 0))],
        out_specs=[pl.BlockSpec((None, tm, HY_W), lambda b, i: (b, i, 0)),
                   pl.BlockSpec((None, tm, HY_W), lambda b, i: (b, i, 0)),
                   pl.BlockSpec((None, tm // FFT_N1, HY_W, FFT_N1), lambda b, i: (b, i, 0, 0))],
        out_shape=[jax.ShapeDtypeStruct((bsz, l, HY_W), BF16), jax.ShapeDtypeStruct((bsz, l, HY_W), BF16),
                   jax.ShapeDtypeStruct((bsz, l // FFT_N1, HY_W, FFT_N1), BF16)],
        compiler_params=_cparams(("parallel", "parallel")),
        name="hy_pre",
    )(zh, zh, zh, zh, cw, cb)


def _fft_consts(n1_in):
    n = FFT_N1
    idx = np.arange(n)
    ang1 = 2.0 * np.pi * np.outer(idx, idx) / n
    c, s = np.cos(ang1), np.sin(ang1)
    angt = 2.0 * np.pi * np.outer(idx, idx) / (n * n)
    tw = np.concatenate([np.cos(angt), -np.sin(angt)], axis=1)
    f3 = np.block([[c, -s], [s, c]])
    f3i = np.block([[c, s], [-s, c]])
    ch, sh = c[:, :n1_in], s[:, :n1_in]
    f1_pair = np.block([[ch, sh], [-sh, ch]])
    f1_real = np.concatenate([c, -s], axis=0)
    f1i = np.block([[ch.T, -sh.T], [sh.T, ch.T]]) / (n * n)
    return tw, f3, f3i, f1_pair, f1_real, f1i


def _fft_forward(a, tw_re, tw_im, lhs_scr, ncg):
    for cix in range(ncg):
        cs = slice(cix * FFT_N1, (cix + 1) * FFT_N1)
        are, aim = a[:FFT_N1, cs], a[FFT_N1:, cs]
        lhs_scr[cs, :FFT_N1] = are * tw_re - aim * tw_im
        lhs_scr[cs, FFT_N1:] = are * tw_im + aim * tw_re


def _hy_filt_kernel(k_ref, f1_ref, tw_ref, f3_ref, o_ref, lhs_scr):
    a = _dot_x3(f1_ref[...], k_ref[...])
    _fft_forward(a, tw_ref[:, :FFT_N1], tw_ref[:, FFT_N1:], lhs_scr, FFT_CG)
    o_ref[...] = _dot_x3(lhs_scr[...], f3_ref[...])


def _hy_fft_kernel(u_ref, kf_ref, f1_ref, tw_ref, f3_ref, f3i_ref, f1i_ref, y_ref, lhs_scr, a2_scr):
    cgl = FFT_CG * FFT_N1
    x = u_ref[...].reshape(2 * u_ref.shape[1], cgl)
    a = jnp.dot(f1_ref[...], x, preferred_element_type=F32)
    tw_re, tw_im = tw_ref[:, :FFT_N1], tw_ref[:, FFT_N1:]
    _fft_forward(a, tw_re, tw_im, lhs_scr, FFT_CG)
    y = jnp.dot(lhs_scr[...].astype(BF16), f3_ref[...], preferred_element_type=F32)
    yre, yim = y[:, :FFT_N1], y[:, FFT_N1:]
    kre, kim = kf_ref[:, :FFT_N1], kf_ref[:, FFT_N1:]
    z = jnp.concatenate([yre * kre - yim * kim, yre * kim + yim * kre], axis=1).astype(BF16)
    bp = jnp.dot(z, f3i_ref[...], preferred_element_type=F32)
    for cix in range(FFT_CG):
        cs = slice(cix * FFT_N1, (cix + 1) * FFT_N1)
        bre, bim = bp[cs, :FFT_N1], bp[cs, FFT_N1:]
        a2_scr[:FFT_N1, cs] = bre * tw_re + bim * tw_im
        a2_scr[FFT_N1:, cs] = bim * tw_re - bre * tw_im
    out = jnp.dot(f1i_ref[...], a2_scr[...].astype(BF16), preferred_element_type=F32)
    y_ref[...] = out.reshape(2, u_ref.shape[1], cgl).astype(BF16)


def _hy_conv(ut, kfilt):
    bsz, n1h, nch, _ = ut.shape
    assert 2 * n1h == FFT_N1 and bsz % 2 == 0
    cgl = FFT_CG * FFT_N1
    ncol = nch * FFT_N1
    tw, f3, f3i, f1_pair, f1_real, f1i = _fft_consts(n1h)
    kt = kfilt.reshape(FFT_N1, FFT_N1, nch).transpose(0, 2, 1).reshape(FFT_N1, ncol)
    const2 = lambda shp: pl.BlockSpec(shp, lambda *a: (0, 0))
    kf = pl.pallas_call(
        _hy_filt_kernel,
        grid=(nch // FFT_CG,),
        in_specs=[pl.BlockSpec((FFT_N1, cgl), lambda j: (0, j)), const2((2 * FFT_N1, FFT_N1)),
                  const2((FFT_N1, 2 * FFT_N1)), const2((2 * FFT_N1, 2 * FFT_N1))],
        out_specs=pl.BlockSpec((cgl, 2 * FFT_N1), lambda j: (j, 0)),
        out_shape=jax.ShapeDtypeStruct((ncol, 2 * FFT_N1), F32),
        scratch_shapes=[pltpu.VMEM((cgl, 2 * FFT_N1), F32)],
        compiler_params=_cparams(("parallel",)),
        name="hy_filter_dft",
    )(kt, jnp.asarray(f1_real, F32), jnp.asarray(tw, F32), jnp.asarray(f3, F32))
    u2 = ut.reshape(bsz, n1h, ncol)
    y = pl.pallas_call(
        _hy_fft_kernel,
        grid=(bsz // 2, nch // FFT_CG),
        in_specs=[pl.BlockSpec((2, n1h, cgl), lambda p, j: (p, 0, j)),
                  pl.BlockSpec((cgl, 2 * FFT_N1), lambda p, j: (j, 0)),
                  const2((2 * FFT_N1, FFT_N1)), const2((FFT_N1, 2 * FFT_N1)),
                  const2((2 * FFT_N1, 2 * FFT_N1)), const2((2 * FFT_N1, 2 * FFT_N1)), const2((FFT_N1, 2 * FFT_N1))],
        out_specs=pl.BlockSpec((2, n1h, cgl), lambda p, j: (p, 0, j)),
        out_shape=jax.ShapeDtypeStruct((bsz, n1h, ncol), BF16),
        scratch_shapes=[pltpu.VMEM((cgl, 2 * FFT_N1), F32), pltpu.VMEM((2 * FFT_N1, cgl), F32)],
        compiler_params=_cparams(("parallel", "parallel")),
        name="hy_fft_conv",
    )(u2, kf, jnp.asarray(f1_pair, F32).astype(BF16), jnp.asarray(tw, F32), jnp.asarray(f3, F32).astype(BF16),
      jnp.asarray(f3i, F32).astype(BF16), jnp.asarray(f1i, F32).astype(BF16))
    return y.reshape(bsz, n1h, nch, FFT_N1)


def _hy_filter_kernel(f_ref, w1_ref, b1_ref, w2_ref, b2_ref, w3_ref, b3_ref, fr_ref, w4_ref, dl_ref, o_ref):
    f = f_ref[...]
    fr = fr_ref[...]
    h = jnp.sin(fr * (_dot_x3(f, w1_ref[...]) + b1_ref[...]))
    h = jnp.sin(fr * (_dot_x3(h, w2_ref[...]) + b2_ref[...]))
    h = jnp.sin(fr * (_dot_x3(h, w3_ref[...]) + b3_ref[...]))
    out = _dot_x3(h, w4_ref[...])
    tm = f.shape[0]
    row = pl.program_id(0) * tm + lax.broadcasted_iota(jnp.int32, (tm, 1), 0)
    o_ref[0] = out[:, :HY_W] * jnp.exp(-f[:, 0:1] * dl_ref[...])
    o_ref[1] = jnp.where(row == 0, 0.0, out[:, HY_W:] * jnp.exp(-f[:, 128:129] * dl_ref[...]))


def _hy_filter(seqlen, fw1, fb1, fw2, fb2, fw3, fb3, freq, fw4):
    bands = (HY_EMB - 1) // 2
    f = jnp.linspace(1e-4, bands - 1, bands, dtype=F32)[None, :]
    t_f = jnp.linspace(0.0, 1.0, seqlen, dtype=F32)
    pos_f = jnp.arange(seqlen, dtype=F32)
    t_b = jnp.concatenate([t_f[:1], t_f[:0:-1]])
    pos_b = jnp.concatenate([pos_f[:1], pos_f[:0:-1]])

    def features(t, pos):
        w = 2.0 * math.pi * pos[:, None] / seqlen
        feat = jnp.concatenate([t[:, None], jnp.cos(f * w), -jnp.sin(f * w)], axis=-1)
        return jnp.pad(feat, ((0, 0), (0, 128 - HY_EMB)))

    feat2 = jnp.concatenate([features(t_f, pos_f), features(t_b, pos_b)], axis=1)
    nh = fw2.shape[0]
    two = lambda m: jnp.kron(jnp.eye(2, dtype=F32), m.astype(F32))
    w1 = two(jnp.pad(fw1.astype(F32), ((0, 128 - HY_EMB), (0, 0))))
    w4 = jnp.concatenate([jnp.pad(fw4[:, :HY_W].astype(F32), ((0, nh), (0, 0))),
                          jnp.pad(fw4[:, HY_W:].astype(F32), ((nh, 0), (0, 0)))], axis=1)
    deltas = jnp.abs(jnp.linspace(math.log(HY_DECAY_TARGET) / HY_DECAY_LONG_PCT,
                                  math.log(HY_DECAY_TARGET) / HY_DECAY_SHORT_PCT, HY_W, dtype=F32))
    tm = 1024
    const = lambda shp: pl.BlockSpec(shp, lambda i: (0, 0))
    row2 = lambda v: jnp.tile(v.astype(F32).reshape(1, -1), (1, 2))
    out = pl.pallas_call(
        _hy_filter_kernel,
        grid=(seqlen // tm,),
        in_specs=[pl.BlockSpec((tm, 256), lambda i: (i, 0)),
                  const((256, 2 * nh)), const((1, 2 * nh)), const((2 * nh, 2 * nh)), const((1, 2 * nh)),
                  const((2 * nh, 2 * nh)), const((1, 2 * nh)), const((1, 2 * nh)),
                  const((2 * nh, 2 * HY_W)), const((1, HY_W))],
        out_specs=pl.BlockSpec((2, tm, HY_W), lambda i: (0, i, 0)),
        out_shape=jax.ShapeDtypeStruct((2, seqlen, HY_W), F32),
        compiler_params=_cparams(("parallel",)),
        name="hy_filter_mlp",
    )(feat2, w1, row2(fb1), two(fw2), row2(fb2), two(fw3), row2(fb3), row2(freq), w4,
      deltas.reshape(1, HY_W))
    return out.reshape(2 * seqlen, HY_W)


def _od_out_kernel(na_ref, yt_ref, x0_ref, u_ref, xa_ref, xb_ref, mod_ref, hb_ref, wout_ref, wr_ref, br_ref,
                   x_out, h2_out, lg_out):
    y = jnp.concatenate([yt_ref[j].astype(F32).T for j in range(yt_ref.shape[0])], axis=0)
    hy = x0_ref[...].astype(F32) * (y + u_ref[...].astype(F32) * hb_ref[...])
    cat = jnp.concatenate([na_ref[...], hy], axis=1).astype(BF16)
    mix = jnp.dot(cat, wout_ref[...], preferred_element_type=F32)
    x = jnp.concatenate([xa_ref[...], xb_ref[...]], axis=0)
    _mixer_tail(x, mix, (mod_ref[2:3, :], mod_ref[3:4, :], mod_ref[4:5, :]), wr_ref, br_ref, x_out, h2_out, lg_out)


def _latent_mod_index(b, i):
    return (b, 1, 0, 0)


def _od_out(na, yt, x0, u, xcat, mods, hb, wout, wr, br):
    bsz, l, _ = na.shape
    tm = 2 * TB
    nblk = l // tm
    tok = lambda n: pl.BlockSpec((None, tm, n), lambda b, i: (b, i, 0))
    const = lambda shp: pl.BlockSpec(shp, lambda b, i: tuple(0 for _ in shp))
    return pl.pallas_call(
        _od_out_kernel,
        grid=(bsz, nblk),
        in_specs=[tok(NA_W), pl.BlockSpec((None, tm // FFT_N1, HY_W, FFT_N1), lambda b, i: (b, i, 0, 0)),
                  tok(HY_W), tok(HY_W),
                  pl.BlockSpec((None, TB, D_MODEL), lambda b, i: (b, 2 * i + 1, 0)),
                  pl.BlockSpec((None, TB, D_MODEL), lambda b, i: (b, 2 * i + 2, 0)),
                  pl.BlockSpec((None, None, 6, D_MODEL), _latent_mod_index),
                  const((1, HY_W)), const((D_MODEL, D_MODEL)), const((D_MODEL, 128)), const((1, 128))],
        out_specs=[tok(D_MODEL), pl.BlockSpec((tm * ROW_TILE, 128), lambda b, i: (b * nblk + i, 0)), tok(128)],
        out_shape=[jax.ShapeDtypeStruct((bsz, l, D_MODEL), F32), jax.ShapeDtypeStruct((bsz * l * ROW_TILE, 128), F32),
                   jax.ShapeDtypeStruct((bsz, l, 128), F32)],
        compiler_params=_cparams(("parallel", "parallel")),
        name="od_out",
    )(na, yt, x0, u, xcat, xcat, mods, hb, wout, wr, br)


def _mods(c, c_ctx, ada_w, ada_b):
    bsz = c.shape[0]
    depth, _, n = ada_w.shape
    cc = jnp.concatenate([c, c_ctx[None]], axis=0)
    a = jnp.pad(cc * _sigmoid(cc), ((0, 8 - (bsz + 1) % 8), (0, 0)))
    mp, tn = a.shape[0], 1024
    m = pl.pallas_call(
        _adaln_kernel,
        grid=(depth, n // tn),
        in_specs=[pl.BlockSpec((mp, D_MODEL), lambda l, j: (0, 0)),
                  pl.BlockSpec((None, D_MODEL, tn), lambda l, j: (l, 0, j)),
                  pl.BlockSpec((None, 1, tn), lambda l, j: (l, 0, j))],
        out_specs=pl.BlockSpec((None, mp, tn), lambda l, j: (l, 0, j)),
        out_shape=jax.ShapeDtypeStruct((depth, mp, n), F32),
        compiler_params=_cparams(("parallel", "parallel")),
        name="adaln_dense",
    )(a, ada_w, ada_b.reshape(depth, 1, n))
    mod_l = m[:, :bsz].reshape(depth, bsz, 1, 6, D_MODEL)
    mod_c = jnp.broadcast_to(m[:, bsz].reshape(depth, 1, 1, 6, D_MODEL), (depth, bsz, 1, 6, D_MODEL))
    return jnp.concatenate([mod_c, mod_l], axis=2)


def _rope_tables(seqlen):
    pos = jnp.arange(seqlen)
    half = GLA_DK // 4
    freqs = ROPE_BASE ** (-jnp.arange(half, dtype=F32) / half)
    ar = (pos // GRID_W).astype(F32)[:, None] * freqs
    ac = (pos % GRID_W).astype(F32)[:, None] * freqs
    cos = jnp.concatenate([jnp.cos(ar), jnp.cos(ar), jnp.cos(ac), jnp.cos(ac)], axis=1)
    sin = jnp.concatenate([-jnp.sin(ar), jnp.sin(ar), -jnp.sin(ac), jnp.sin(ac)], axis=1)
    cos = jnp.concatenate([jnp.ones((TB, GLA_DK), F32), cos], axis=0)
    sin = jnp.concatenate([jnp.zeros((TB, GLA_DK), F32), sin], axis=0)
    return jnp.tile(cos, (1, GLA_HEADS)), jnp.tile(sin, (1, GLA_HEADS))


def _router_weights(wg, bg, we, be):
    pad = 128 - N_GROUPS - N_EXPERTS
    wr = jnp.concatenate([wg, we, jnp.zeros((D_MODEL, pad), F32)], axis=1)
    br = jnp.concatenate([bg, be, jnp.zeros((pad,), F32)]).reshape(1, 128)
    return wr, br


def kernel(x, c, ctx, c_ctx, ada_w, ada_b, moe_wg, moe_bg, moe_we, moe_be, moe_w1, moe_w3, moe_w2, ev_w_in, ev_w_out, gla_wa2, gla_ba, gla_norm, s5_lam_re, s5_lam_im, s5_log_dt, s5_b_re, s5_b_im, s5_c_re, s5_c_im, s5_d, s5_w_glu, od_w_in, od_w_out, na_q_norm, na_k_norm, na_rpb, hy_conv_w, hy_conv_b, hy_fw1, hy_fb1, hy_fw2, hy_fb2, hy_fw3, hy_fb3, hy_freq, hy_fw4, hy_bias):
    bsz, seqlen, _ = x.shape
    assert ctx.shape[1] == TB and seqlen % TB == 0

    mods_all = _mods(c, c_ctx, ada_w, ada_b)
    mods = mods_all[0]
    w_in = ev_w_in[0]
    n_a = 2 * GLA_RANK
    a0 = 2 * GLA_QK + 2 * GLA_V
    w_ev = jnp.concatenate([w_in[:, :a0], w_in[:, a0 + n_a:], w_in[:, a0:a0 + n_a],
                            jnp.zeros((D_MODEL, 128 - n_a), F32)], axis=1).astype(BF16)
    wa = jnp.zeros((128, 2 * GLA_QK), F32)
    for d in range(2):
        wa = wa.at[d * GLA_RANK:(d + 1) * GLA_RANK, d * GLA_QK:(d + 1) * GLA_QK].set(gla_wa2[0, d])
    cos, sin = _rope_tables(seqlen)
    q, k, v, g, u, u_sw, la = _ev_proj(ctx, x, mods, w_ev, wa, gla_ba[0].reshape(1, 2 * GLA_QK), cos, sin)
    o_f, o_b = _gla(q, k, v, la)
    s5p = [t[0].astype(F32) for t in (s5_lam_re, s5_lam_im, s5_log_dt, s5_b_re, s5_b_im, s5_c_re, s5_c_im)]
    y_f = _s5_scan(u, _s5_mats(*[t[0] for t in s5p], rev=False), rev=False)
    y_b = _s5_scan(u_sw, _s5_mats(*[t[1] for t in s5p], rev=True), rev=True)
    wr, br = _router_weights(moe_wg[0], moe_bg[0], moe_we[0], moe_be[0])
    x1, h2, lg = _ev_out(o_f, o_b, g, y_f, y_b, u, ctx, x, mods,
                         jnp.tile(gla_norm[0], GLA_HEADS).reshape(1, GLA_V), s5_d[0].reshape(1, S5_W),
                         s5_w_glu[0].astype(BF16), ev_w_out[0].astype(BF16), wr, br)
    xcat = _moe(x1, h2, lg, mods, TB, moe_w1, moe_w3, moe_w2, 0)

    mods = mods_all[1]
    hd = np.arange(NA_W) // NA_DH
    gm = jnp.asarray((hd[:, None] == hd[None, :]).astype(np.float32) / NA_DH)
    qh, kh, vh, zh = _od_proj(xcat, mods, od_w_in[0].astype(BF16), gm,
                              jnp.tile(na_q_norm[0], NA_HEADS).reshape(1, NA_W),
                              jnp.tile(na_k_norm[0], NA_HEADS).reshape(1, NA_W))
    na = _na(qh, kh, vh, _na_bias_table(na_rpb[0]))
    x0, uh, ut = _hy_pre(zh, hy_conv_w[0], hy_conv_b[0].reshape(1, 3 * HY_W))
    kfilt = _hy_filter(seqlen, hy_fw1[0], hy_fb1[0], hy_fw2[0], hy_fb2[0], hy_fw3[0], hy_fb3[0],
                       hy_freq[0], hy_fw4[0])
    yt = _hy_conv(ut, kfilt)
    wr, br = _router_weights(moe_wg[1], moe_bg[1], moe_we[1], moe_be[1])
    xl, h2, lg = _od_out(na, yt, x0, uh, xcat, mods, hy_bias[0].reshape(1, HY_W),
                         od_w_out[0].astype(BF16), wr, br)
    return _moe(xl, h2, lg, mods, 0, moe_w1, moe_w3, moe_w2, 1)
```
